```python
import math
import jax, jax.numpy as jnp
from jax import lax
import numpy as np

D_MODEL = 2048
BATCH = 8
SEQ = 4096
DEPTH = 4

N_A_LAYERS = DEPTH // 2
N_B_LAYERS = DEPTH - N_A_LAYERS
RMS_EPS = 1e-6

E_A = D_MODEL
POOL_WINDOWS = (2, 4, 8, 16)
N_POOL_GROUPS = len(POOL_WINDOWS)
POOL_GROUP_DIM = E_A // N_POOL_GROUPS

HEAD_DIM = 128
HEADS_PER_GROUP = D_MODEL // HEAD_DIM
DILATED_PAIRS = ((128, 1), (512, 4), (2048, 16))
N_DIL_GROUPS = len(DILATED_PAIRS)
E_B = HEADS_PER_GROUP * HEAD_DIM
ROPE_THETA = 10000.0
NEG_INF = -1e30

kernel_name = "yoco_pool_dilated_hybrid"


def rmsnorm(x, g):
    xf = x.astype(jnp.float32)
    inv = lax.rsqrt(jnp.mean(xf * xf, axis=-1, keepdims=True) + RMS_EPS)
    return (xf * inv).astype(x.dtype) * g


def rope_tables(seq):
    inv_freq = 1.0 / (ROPE_THETA ** (jnp.arange(0, HEAD_DIM, 2, dtype=jnp.float32) / HEAD_DIM))
    ang = jnp.arange(seq, dtype=jnp.float32)[:, None] * inv_freq[None, :]
    return jnp.cos(ang), jnp.sin(ang)


def apply_rope(t, cos, sin):
    tf = t.astype(jnp.float32)
    t1, t2 = tf[..., : HEAD_DIM // 2], tf[..., HEAD_DIM // 2:]
    c, s = cos[None, :, None, :], sin[None, :, None, :]
    return jnp.concatenate([t1 * c - t2 * s, t2 * c + t1 * s], axis=-1).astype(t.dtype)


def multiscale_causal_pool(u):
    b, s, _ = u.shape
    u4 = u.reshape(b, s, N_POOL_GROUPS, POOL_GROUP_DIM)
    csum = jnp.cumsum(u4.astype(jnp.float32), axis=1)
    csum = jnp.concatenate([jnp.zeros_like(csum[:, :1]), csum], axis=1)
    win = jnp.asarray(POOL_WINDOWS, dtype=jnp.int32)
    t1 = jnp.arange(1, s + 1, dtype=jnp.int32)[:, None]
    lower = jnp.maximum(t1 - win[None, :], 0)
    c_lo = csum[:, lower, jnp.arange(N_POOL_GROUPS)[None, :], :]
    count = jnp.minimum(t1, win[None, :]).astype(jnp.float32)
    mean = (csum[:, 1:] - c_lo) / count[None, :, :, None]
    return (mean - u4.astype(jnp.float32)).astype(u.dtype)


def dilated_window_attention(q, k, v, window, dilation):
    b, s, h, hd = q.shape
    d = dilation
    nb = window // dilation
    m = s // d
    nblk = -(-m // nb)
    m_pad = nblk * nb

    def residues(t):
        return t.reshape(b, m, d, h, hd).transpose(0, 2, 3, 1, 4)

    qr, kr, vr = residues(q), residues(k), residues(v)
    qb = jnp.pad(qr, ((0, 0), (0, 0), (0, 0), (0, m_pad - m), (0, 0))).reshape(b, d, h, nblk, nb, hd)
    kv_pad = ((0, 0), (0, 0), (0, 0), (nb, m_pad - m), (0, 0))
    kp = jnp.pad(kr, kv_pad).reshape(b, d, h, nblk + 1, nb, hd)
    vp = jnp.pad(vr, kv_pad).reshape(b, d, h, nblk + 1, nb, hd)
    kb = jnp.concatenate([kp[:, :, :, :-1], kp[:, :, :, 1:]], axis=4)
    vb = jnp.concatenate([vp[:, :, :, :-1], vp[:, :, :, 1:]], axis=4)

    scores = jnp.einsum('bdhnqc,bdhnkc->bdhnqk', qb, kb).astype(jnp.float32) * (1.0 / math.sqrt(hd))
    r_idx = jnp.arange(nb)[:, None]
    c_idx = jnp.arange(2 * nb)[None, :]
    band = (c_idx >= r_idx) & (c_idx <= r_idx + nb)
    blk = jnp.arange(nblk)[:, None, None]
    mask = band[None] & (blk * nb + c_idx[None] >= nb)
    scores = jnp.where(mask[None, None, None], scores, NEG_INF)
    lse = jax.nn.logsumexp(scores, axis=-1)
    p = jnp.exp(scores - lse[..., None]).astype(v.dtype)
    out = jnp.einsum('bdhnqk,bdhnkc->bdhnqc', p, vb)

    out = out.reshape(b, d, h, m_pad, hd)[:, :, :, :m]
    lse = lse.reshape(b, d, h, m_pad)[:, :, :, :m]
    out = out.transpose(0, 3, 1, 2, 4).reshape(b, s, h, hd)
    lse = lse.transpose(0, 3, 1, 2).reshape(b, s, h)
    return out, lse


def _fwd_setup_inputs(seed: int = 0) -> dict:
    key = jax.random.key(seed)
    ks = jax.random.split(key, 13)
    f32 = jnp.float32
    x = jax.random.normal(ks[0], (BATCH, SEQ, D_MODEL), f32)
    norm_a = 1.0 + 0.1 * jax.random.normal(ks[1], (N_A_LAYERS, D_MODEL), f32)
    w_in_a = jax.random.normal(ks[2], (N_A_LAYERS, D_MODEL, 2 * E_A), f32) * D_MODEL ** -0.5
    w_grp_a = jax.random.normal(ks[3], (N_A_LAYERS, N_POOL_GROUPS, POOL_GROUP_DIM, POOL_GROUP_DIM), f32) * POOL_GROUP_DIM ** -0.5
    scale_a = 1.0 + 0.1 * jax.random.normal(ks[4], (N_A_LAYERS, E_A), f32)
    w_out_a = jax.random.normal(ks[5], (N_A_LAYERS, E_A, D_MODEL), f32) * E_A ** -0.5
    norm_kv = 1.0 + 0.1 * jax.random.normal(ks[6], (D_MODEL,), f32)
    w_k = jax.random.normal(ks[7], (D_MODEL, E_B), f32) * D_MODEL ** -0.5
    w_v = jax.random.normal(ks[8], (D_MODEL, E_B), f32) * D_MODEL ** -0.5
    norm_b = 1.0 + 0.1 * jax.random.normal(ks[9], (N_B_LAYERS, D_MODEL), f32)
    w_in_b = jax.random.normal(ks[10], (N_B_LAYERS, D_MODEL, N_DIL_GROUPS * E_B + E_B), f32) * D_MODEL ** -0.5
    w_out_b = jax.random.normal(ks[11], (N_B_LAYERS, E_B, D_MODEL), f32) * E_B ** -0.5
    norm_f = 1.0 + 0.1 * jax.random.normal(ks[12], (D_MODEL,), f32)
    return {"x": x, "norm_a": norm_a, "w_in_a": w_in_a, "w_grp_a": w_grp_a, "scale_a": scale_a,
            "w_out_a": w_out_a, "norm_kv": norm_kv, "w_k": w_k, "w_v": w_v, "norm_b": norm_b,
            "w_in_b": w_in_b, "w_out_b": w_out_b, "norm_f": norm_f}


def _fwd_reference(x, norm_a, w_in_a, w_grp_a, scale_a, w_out_a, norm_kv, w_k, w_v, norm_b, w_in_b, w_out_b, norm_f):
    b, s, _ = x.shape
    cos, sin = rope_tables(s)
    k_shared = None
    v_shared = None
    for layer in range(DEPTH):
        if layer < N_A_LAYERS:
            i = layer
            hdn = rmsnorm(x, norm_a[i])
            proj = hdn @ w_in_a[i]
            u, gate = proj[..., :E_A], proj[..., E_A:]
            pooled = multiscale_causal_pool(u)
            y = jnp.einsum('bsgc,gcd->bsgd', pooled, w_grp_a[i]).reshape(b, s, E_A) * scale_a[i]
            x = x + (y * jax.nn.silu(gate)) @ w_out_a[i]
            if layer == N_A_LAYERS - 1:
                kv_in = rmsnorm(x, norm_kv)
                k_shared = apply_rope((kv_in @ w_k).reshape(b, s, HEADS_PER_GROUP, HEAD_DIM), cos, sin)
                v_shared = (kv_in @ w_v).reshape(b, s, HEADS_PER_GROUP, HEAD_DIM)
        else:
            i = layer - N_A_LAYERS
            hdn = rmsnorm(x, norm_b[i])
            proj = hdn @ w_in_b[i]
            q_all = proj[..., : N_DIL_GROUPS * E_B].reshape(b, s, N_DIL_GROUPS, HEADS_PER_GROUP, HEAD_DIM)
            gate = proj[..., N_DIL_GROUPS * E_B:]
            outs = []
            lses = []
            for g, (window, dilation) in enumerate(DILATED_PAIRS):
                q = apply_rope(q_all[:, :, g], cos, sin)
                o_g, lse_g = dilated_window_attention(q, k_shared, v_shared, window, dilation)
                outs.append(o_g)
                lses.append(lse_g)
            alpha = jax.nn.softmax(jnp.stack(lses, axis=0), axis=0)
            merged = jnp.sum(alpha[..., None].astype(x.dtype) * jnp.stack(outs, axis=0), axis=0)
            merged = merged.reshape(b, s, E_B)
            x = x + (merged * jax.nn.silu(gate)) @ w_out_b[i]
    return rmsnorm(x, norm_f)


import jax as _jax
import jax.numpy as _jnp

TWIN_FORMAT = 'train_step'
FWD_PARAMS = ['x', 'norm_a', 'w_in_a', 'w_grp_a', 'scale_a', 'w_out_a', 'norm_kv', 'w_k', 'w_v', 'norm_b', 'w_in_b', 'w_out_b', 'norm_f']
TWIN_WEIGHTS = ['norm_a', 'w_in_a', 'w_grp_a', 'scale_a', 'w_out_a', 'norm_kv', 'w_k', 'w_v', 'norm_b', 'w_in_b', 'w_out_b', 'norm_f']
TWIN_DIFF_INPUT = 'x'
TWIN_INPUTS = ['x', 'norm_a', 'w_in_a', 'w_grp_a', 'scale_a', 'w_out_a', 'norm_kv', 'w_k', 'w_v', 'norm_b', 'w_in_b', 'w_out_b', 'norm_f', 'loss_target', 'm_norm_a', 'm_w_in_a', 'm_w_grp_a', 'm_scale_a', 'm_w_out_a', 'm_norm_kv', 'm_w_k', 'm_w_v', 'm_norm_b', 'm_w_in_b', 'm_w_out_b', 'm_norm_f', 'v_norm_a', 'v_w_in_a', 'v_w_grp_a', 'v_scale_a', 'v_w_out_a', 'v_norm_kv', 'v_w_k', 'v_w_v', 'v_norm_b', 'v_w_in_b', 'v_w_out_b', 'v_norm_f']
TWIN_OUTPUTS = ['loss', 'grad_x', 'grad_norm_a', 'grad_w_in_a', 'grad_w_grp_a', 'grad_scale_a', 'grad_w_out_a', 'grad_norm_kv', 'grad_w_k', 'grad_w_v', 'grad_norm_b', 'grad_w_in_b', 'grad_w_out_b', 'grad_norm_f', 'delta_norm_a', 'delta_w_in_a', 'delta_w_grp_a', 'delta_scale_a', 'delta_w_out_a', 'delta_norm_kv', 'delta_w_k', 'delta_w_v', 'delta_norm_b', 'delta_w_in_b', 'delta_w_out_b', 'delta_norm_f', 'new_m_norm_a', 'new_m_w_in_a', 'new_m_w_grp_a', 'new_m_scale_a', 'new_m_w_out_a', 'new_m_norm_kv', 'new_m_w_k', 'new_m_w_v', 'new_m_norm_b', 'new_m_w_in_b', 'new_m_w_out_b', 'new_m_norm_f', 'new_v_norm_a', 'new_v_w_in_a', 'new_v_w_grp_a', 'new_v_scale_a', 'new_v_w_out_a', 'new_v_norm_kv', 'new_v_w_k', 'new_v_w_v', 'new_v_norm_b', 'new_v_w_in_b', 'new_v_w_out_b', 'new_v_norm_f']
TWIN_LEAF_KINDS = {'loss': 'loss', 'grad_x': 'grad_x', 'grad_norm_a': 'grad_w', 'grad_w_in_a': 'grad_w', 'grad_w_grp_a': 'grad_w', 'grad_scale_a': 'grad_w', 'grad_w_out_a': 'grad_w', 'grad_norm_kv': 'grad_w', 'grad_w_k': 'grad_w', 'grad_w_v': 'grad_w', 'grad_norm_b': 'grad_w', 'grad_w_in_b': 'grad_w', 'grad_w_out_b': 'grad_w', 'grad_norm_f': 'grad_w', 'delta_norm_a': 'delta_w', 'delta_w_in_a': 'delta_w', 'delta_w_grp_a': 'delta_w', 'delta_scale_a': 'delta_w', 'delta_w_out_a': 'delta_w', 'delta_norm_kv': 'delta_w', 'delta_w_k': 'delta_w', 'delta_w_v': 'delta_w', 'delta_norm_b': 'delta_w', 'delta_w_in_b': 'delta_w', 'delta_w_out_b': 'delta_w', 'delta_norm_f': 'delta_w', 'new_m_norm_a': 'new_m', 'new_m_w_in_a': 'new_m', 'new_m_w_grp_a': 'new_m', 'new_m_scale_a': 'new_m', 'new_m_w_out_a': 'new_m', 'new_m_norm_kv': 'new_m', 'new_m_w_k': 'new_m', 'new_m_w_v': 'new_m', 'new_m_norm_b': 'new_m', 'new_m_w_in_b': 'new_m', 'new_m_w_out_b': 'new_m', 'new_m_norm_f': 'new_m', 'new_v_norm_a': 'new_v', 'new_v_w_in_a': 'new_v', 'new_v_w_grp_a': 'new_v', 'new_v_scale_a': 'new_v', 'new_v_w_out_a': 'new_v', 'new_v_norm_kv': 'new_v', 'new_v_w_k': 'new_v', 'new_v_w_v': 'new_v', 'new_v_norm_b': 'new_v', 'new_v_w_in_b': 'new_v', 'new_v_w_out_b': 'new_v', 'new_v_norm_f': 'new_v'}


def _forward(args):
    return _fwd_reference(*[args[k] for k in FWD_PARAMS])


def _output_shape():
    def fwd():
        inp = _fwd_setup_inputs(0)
        return _fwd_reference(*[inp[k] for k in FWD_PARAMS])
    out = _jax.eval_shape(fwd)
    return out.shape, out.dtype

N_MICROBATCH = 1
ADAM_LR = 0.001
ADAM_B1 = 0.9
ADAM_B2 = 0.999
ADAM_EPS = 1e-08
ADAM_WD = 0.01
ADAM_STEP = 10
PER_EXAMPLE_BATCH_AXIS = {'x': 0, 'loss_target': 0}
SHARED_INPUTS = []
_WEIGHT_DTYPES = {'norm_a': _jnp.float32, 'w_in_a': _jnp.float32, 'w_grp_a': _jnp.float32, 'scale_a': _jnp.float32, 'w_out_a': _jnp.float32, 'norm_kv': _jnp.float32, 'w_k': _jnp.float32, 'w_v': _jnp.float32, 'norm_b': _jnp.float32, 'w_in_b': _jnp.float32, 'w_out_b': _jnp.float32, 'norm_f': _jnp.float32}
MOMENT_SCALE = {'norm_a': 6.140849e-02, 'w_in_a': 4.423979e-02, 'w_grp_a': 4.353625e-02, 'scale_a': 4.447569e-02, 'w_out_a': 4.411847e-02, 'norm_kv': 1.676147e-02, 'w_k': 9.970979e-03, 'w_v': 1.324481e-02, 'norm_b': 1.186804e-02, 'w_in_b': 5.994357e-03, 'w_out_b': 9.416375e-03, 'norm_f': 1.612288e+01}


def _to_microbatches(a, axis):
    t = _jnp.moveaxis(a, axis, 0)
    t = t.reshape((N_MICROBATCH, t.shape[0] // N_MICROBATCH) + t.shape[1:])
    return _jnp.moveaxis(t, 1, axis + 1)


def setup_inputs(seed: int = 0) -> dict:
    inp = _fwd_setup_inputs(seed)
    key = _jax.random.fold_in(_jax.random.key(seed), 7919)
    shape, _ = _output_shape()
    out = dict(inp)
    out["loss_target"] = _jax.random.normal(_jax.random.fold_in(key, 0), shape, _jnp.float32)
    for i, name in enumerate(TWIN_WEIGHTS):
        w = inp[name].astype(_jnp.float32)
        if MOMENT_SCALE is None:
            s = _jnp.sqrt(_jnp.mean(_jnp.square(w)) + 1e-30)
        else:
            s = MOMENT_SCALE[name]
        km, kv = _jax.random.split(_jax.random.fold_in(key, i + 1))
        out[name] = w
        out["m_" + name] = s * _jax.random.normal(km, w.shape, _jnp.float32)
        out["v_" + name] = (s * s) * _jax.random.uniform(kv, w.shape, _jnp.float32, 0.5, 1.5)
    if N_MICROBATCH > 1:
        for name, axis in PER_EXAMPLE_BATCH_AXIS.items():
            out[name] = _to_microbatches(out[name], axis)
    return {'x': out['x'], 'norm_a': out['norm_a'], 'w_in_a': out['w_in_a'], 'w_grp_a': out['w_grp_a'], 'scale_a': out['scale_a'], 'w_out_a': out['w_out_a'], 'norm_kv': out['norm_kv'], 'w_k': out['w_k'], 'w_v': out['w_v'], 'norm_b': out['norm_b'], 'w_in_b': out['w_in_b'], 'w_out_b': out['w_out_b'], 'norm_f': out['norm_f'], 'loss_target': out['loss_target'], 'm_norm_a': out['m_norm_a'], 'm_w_in_a': out['m_w_in_a'], 'm_w_grp_a': out['m_w_grp_a'], 'm_scale_a': out['m_scale_a'], 'm_w_out_a': out['m_w_out_a'], 'm_norm_kv': out['m_norm_kv'], 'm_w_k': out['m_w_k'], 'm_w_v': out['m_w_v'], 'm_norm_b': out['m_norm_b'], 'm_w_in_b': out['m_w_in_b'], 'm_w_out_b': out['m_w_out_b'], 'm_norm_f': out['m_norm_f'], 'v_norm_a': out['v_norm_a'], 'v_w_in_a': out['v_w_in_a'], 'v_w_grp_a': out['v_w_grp_a'], 'v_scale_a': out['v_scale_a'], 'v_w_out_a': out['v_w_out_a'], 'v_norm_kv': out['v_norm_kv'], 'v_w_k': out['v_w_k'], 'v_w_v': out['v_w_v'], 'v_norm_b': out['v_norm_b'], 'v_w_in_b': out['v_w_in_b'], 'v_w_out_b': out['v_w_out_b'], 'v_norm_f': out['v_norm_f']}


def _loss(weights, diff, rest, loss_target):
    with _jax.named_scope("forward"):
        args = {**rest, TWIN_DIFF_INPUT: diff, **{k: w.astype(_WEIGHT_DTYPES[k]) for k, w in weights.items()}}
        y = _forward(args)
    with _jax.named_scope("loss_head"):
        err = _jnp.square(y.astype(_jnp.float32) - loss_target)
        return 0.5 * _jnp.sum(_jnp.mean(err, axis=-1)) if err.ndim else 0.5 * err


def _adamw(w, g, m, v):
    m = ADAM_B1 * m + (1.0 - ADAM_B1) * g
    v = ADAM_B2 * v + (1.0 - ADAM_B2) * _jnp.square(g)
    m_hat = m / (1.0 - ADAM_B1 ** ADAM_STEP)
    v_hat = v / (1.0 - ADAM_B2 ** ADAM_STEP)
    delta = -ADAM_LR * (m_hat / (_jnp.sqrt(v_hat) + ADAM_EPS) + ADAM_WD * w)
    return delta, m, v


def reference(x, norm_a, w_in_a, w_grp_a, scale_a, w_out_a, norm_kv, w_k, w_v, norm_b, w_in_b, w_out_b, norm_f, loss_target, m_norm_a, m_w_in_a, m_w_grp_a, m_scale_a, m_w_out_a, m_norm_kv, m_w_k, m_w_v, m_norm_b, m_w_in_b, m_w_out_b, m_norm_f, v_norm_a, v_w_in_a, v_w_grp_a, v_scale_a, v_w_out_a, v_norm_kv, v_w_k, v_w_v, v_norm_b, v_w_in_b, v_w_out_b, v_norm_f):
    given = dict(x=x, norm_a=norm_a, w_in_a=w_in_a, w_grp_a=w_grp_a, scale_a=scale_a, w_out_a=w_out_a, norm_kv=norm_kv, w_k=w_k, w_v=w_v, norm_b=norm_b, w_in_b=w_in_b, w_out_b=w_out_b, norm_f=norm_f, loss_target=loss_target, m_norm_a=m_norm_a, m_w_in_a=m_w_in_a, m_w_grp_a=m_w_grp_a, m_scale_a=m_scale_a, m_w_out_a=m_w_out_a, m_norm_kv=m_norm_kv, m_w_k=m_w_k, m_w_v=m_w_v, m_norm_b=m_norm_b, m_w_in_b=m_w_in_b, m_w_out_b=m_w_out_b, m_norm_f=m_norm_f, v_norm_a=v_norm_a, v_w_in_a=v_w_in_a, v_w_grp_a=v_w_grp_a, v_scale_a=v_scale_a, v_w_out_a=v_w_out_a, v_norm_kv=v_norm_kv, v_w_k=v_w_k, v_w_v=v_w_v, v_norm_b=v_norm_b, v_w_in_b=v_w_in_b, v_w_out_b=v_w_out_b, v_norm_f=v_norm_f)
    weights = {n: given[n] for n in TWIN_WEIGHTS}
    shared = {n: given[n] for n in SHARED_INPUTS}
    per_example = {n: given[n] for n in ['x']}
    grad_fn = _jax.value_and_grad(_loss, argnums=(0, 1))

    def one_microbatch(ex, loss_target):
        ex = dict(ex)
        diff = ex.pop(TWIN_DIFF_INPUT)
        return grad_fn(weights, diff, {**shared, **ex}, loss_target)

    if N_MICROBATCH == 1:
        loss, (grad_w, grad_x) = one_microbatch(per_example, given["loss_target"])
    else:
        def body(carry, xs):
            loss_sum, grad_sum = carry
            l_k, (gw_k, gx_k) = one_microbatch(xs[0], xs[1])
            with _jax.named_scope("update"):
                return (loss_sum + l_k, _jax.tree.map(_jnp.add, grad_sum, gw_k)), gx_k

        init = (_jnp.zeros((), _jnp.float32), _jax.tree.map(_jnp.zeros_like, weights))
        (loss, grad_w), grad_x = _jax.lax.scan(body, init, (per_example, given["loss_target"]))
    with _jax.named_scope("update"):
        delta_w, new_m, new_v = {}, {}, {}
        for n in TWIN_WEIGHTS:
            delta_w[n], new_m[n], new_v[n] = _adamw(weights[n], grad_w[n], given["m_" + n], given["v_" + n])
    return (loss, grad_x, *[grad_w[n] for n in TWIN_WEIGHTS], *[delta_w[n] for n in TWIN_WEIGHTS],
            *[new_m[n] for n in TWIN_WEIGHTS], *[new_v[n] for n in TWIN_WEIGHTS])
```

```python
import functools
import math

import jax
import jax.numpy as jnp
from jax import lax
from jax.experimental import pallas as pl
from jax.experimental.pallas import tpu as pltpu

F32 = jnp.float32
BF16 = jnp.bfloat16

HEAD_DIM = 128
POOL_WINDOWS = (2, 4, 8, 16)
DILATED_PAIRS = ((128, 1), (512, 4), (2048, 16))
ROPE_THETA = 10000.0
RMS_EPS = 1e-6
NEG_INF = -1e30
N_CHIPS = 4

ADAM_LR = 0.001
ADAM_B1 = 0.9
ADAM_B2 = 0.999
ADAM_EPS = 1e-08
ADAM_WD = 0.01
ADAM_STEP = 10

VMEM_LIMIT_BYTES = 56 * 1024 * 1024
MESH = pl.DeviceIdType.MESH
ANY = pl.BlockSpec(memory_space=pl.ANY)


def _tile(n, pref):
    t = min(n, pref)
    assert n % t == 0, (n, pref)
    return t


def _params(sem=None):
    return pltpu.CompilerParams(dimension_semantics=sem, vmem_limit_bytes=VMEM_LIMIT_BYTES)


def _mm(name, a, b, *, grid2, nk, a_blk, a_map, b_blk, b_map, o_shape, o_blk, o_map, dims, out_dtype,
        epi=None, epi_in=(), epi_specs=(), acc_shape=None):
    n_epi = len(epi_in)

    def body(*refs):
        a_ref, b_ref = refs[0], refs[1]
        e_refs = refs[2:2 + n_epi]
        o_ref = refs[2 + n_epi]

        def contrib():
            return lax.dot_general(a_ref[...], b_ref[...], (dims, ((), ())), preferred_element_type=F32)

        def finish(acc):
            if epi is None:
                o_ref[...] = acc.reshape(o_ref.shape).astype(o_ref.dtype)
            else:
                epi(acc, e_refs, o_ref)

        if nk == 1:
            finish(contrib())
        else:
            acc_ref = refs[-1]
            k = pl.program_id(2)

            @pl.when(k == 0)
            def _():
                acc_ref[...] = contrib()

            @pl.when(k > 0)
            def _():
                acc_ref[...] += contrib()

            @pl.when(k == nk - 1)
            def _():
                finish(acc_ref[...])

    scratch = [] if nk == 1 else [pltpu.VMEM(acc_shape, F32)]
    return pl.pallas_call(
        body, name=name, grid=(grid2[0], grid2[1], nk),
        in_specs=[pl.BlockSpec(a_blk, a_map), pl.BlockSpec(b_blk, b_map), *epi_specs],
        out_specs=pl.BlockSpec(o_blk, o_map),
        out_shape=jax.ShapeDtypeStruct(o_shape, out_dtype),
        scratch_shapes=scratch,
        compiler_params=_params(("parallel", "parallel", "arbitrary")),
    )(a, b, *epi_in)


NN = ((1,), (0,))
NT = ((1,), (1,))
TN = ((0,), (0,))


def _rope_apply(t, cos, sin):
    return t * cos + pltpu.roll(t, HEAD_DIM // 2, 1) * sin


def _epi_add(acc, e_refs, o_ref):
    o_ref[...] = (acc + e_refs[0][...]).astype(o_ref.dtype)


def _make_epi_rope(scale):
    def epi(acc, e_refs, o_ref):
        cos = e_refs[0][...]
        sin = e_refs[1][...]
        for h in range(acc.shape[1] // HEAD_DIM):
            sl = slice(h * HEAD_DIM, (h + 1) * HEAD_DIM)
            o_ref[:, sl] = (_rope_apply(acc[:, sl], cos, sin) * scale).astype(o_ref.dtype)
    return epi


def _mm_act_w(name, a, w, *, out_dtype, add=None, rope=None, n_first=0, n_cols=None):
    s_len, k_len = a.shape
    bm = _tile(s_len, 1024)
    epi, epi_in, epi_specs = None, (), ()
    if w.ndim == 3:
        ns, _, c = w.shape
        ns_used = ns if n_cols is None else n_cols
        bn = _tile(c, 1024)
        sub = c // bn
        grid2 = (ns_used * sub, s_len // bm)
        b_blk, b_map = (None, k_len, bn), (lambda j, i, k: (j // sub + n_first, 0, j % sub))
        n_len = ns_used * c
    else:
        n_len = w.shape[1]
        bn = _tile(n_len, 1024)
        grid2 = (n_len // bn, s_len // bm)
        b_blk, b_map = (k_len, bn), (lambda j, i, k: (0, j))
    if add is not None:
        epi, epi_in = _epi_add, (add,)
        epi_specs = (pl.BlockSpec((bm, bn), lambda j, i, k: (i, j)),)
    if rope is not None:
        cos, sin, scale = rope
        epi, epi_in = _make_epi_rope(scale), (cos, sin)
        epi_specs = (pl.BlockSpec((bm, HEAD_DIM), lambda j, i, k: (i, 0)),) * 2
    return _mm(name, a, w, grid2=grid2, nk=1, a_blk=(bm, k_len), a_map=lambda j, i, k: (i, 0),
               b_blk=b_blk, b_map=b_map, o_shape=(s_len, n_len), o_blk=(bm, bn), o_map=lambda j, i, k: (i, j),
               dims=NN, out_dtype=out_dtype, epi=epi, epi_in=epi_in, epi_specs=epi_specs)


def _mm_grad_act(name, dy, w, *, add=None):
    s_len, n_len = dy.shape
    bm = _tile(s_len, 1024)
    if w.ndim == 3:
        ns, k_len, c = w.shape
        bk, nk = c, ns
        bn = _tile(k_len, 1024)
        b_blk, b_map = (None, bn, c), (lambda j, i, k: (k, j, 0))
    else:
        k_len = w.shape[0]
        bk = _tile(n_len, 1024)
        nk = n_len // bk
        bn = _tile(k_len, 1024)
        b_blk, b_map = (bn, bk), (lambda j, i, k: (j, k))
    epi, epi_in, epi_specs = None, (), ()
    if add is not None:
        epi, epi_in = _epi_add, (add,)
        epi_specs = (pl.BlockSpec((bm, bn), lambda j, i, k: (i, j)),)
    return _mm(name, dy, w, grid2=(k_len // bn, s_len // bm), nk=nk, a_blk=(bm, bk), a_map=lambda j, i, k: (i, k),
               b_blk=b_blk, b_map=b_map, o_shape=(s_len, k_len), o_blk=(bm, bn), o_map=lambda j, i, k: (i, j),
               dims=NT, out_dtype=F32, epi=epi, epi_in=epi_in, epi_specs=epi_specs, acc_shape=(bm, bn))


def _mm_grad_w(name, a, dy, *, col_shards=None):
    s_len, k_len = a.shape
    n_len = dy.shape[1]
    bk = _tile(s_len, 1024)
    bm = _tile(k_len, 1024)
    if col_shards:
        c = n_len // col_shards
        bn = _tile(c, 1024)
        sub = c // bn
        o_shape, o_blk, o_map = (col_shards, k_len, c), (None, bm, bn), (lambda j, i, k: (j // sub, i, j % sub))
    else:
        bn = _tile(n_len, 1024)
        o_shape, o_blk, o_map = (k_len, n_len), (bm, bn), (lambda j, i, k: (i, j))
    return _mm(name, a, dy, grid2=(n_len // bn, k_len // bm), nk=s_len // bk,
               a_blk=(bk, bm), a_map=lambda j, i, k: (k, i), b_blk=(bk, bn), b_map=lambda j, i, k: (k, j),
               o_shape=o_shape, o_blk=o_blk, o_map=o_map, dims=TN, out_dtype=BF16, acc_shape=(bm, bn))


def _mm_grp_fwd(name, pooled, wg):
    s_len, e = pooled.shape
    ng, g, _ = wg.shape
    bm = _tile(s_len, 1024)
    return _mm(name, pooled, wg, grid2=(ng, s_len // bm), nk=1, a_blk=(bm, g), a_map=lambda j, i, k: (i, j),
               b_blk=(None, g, g), b_map=lambda j, i, k: (j, 0, 0), o_shape=(s_len, e), o_blk=(bm, g),
               o_map=lambda j, i, k: (i, j), dims=NN, out_dtype=F32)


def _mm_grp_grad_act(name, dy, wg):
    s_len, e = dy.shape
    ng, g, _ = wg.shape
    bm = _tile(s_len, 1024)
    return _mm(name, dy, wg, grid2=(ng, s_len // bm), nk=1, a_blk=(bm, g), a_map=lambda j, i, k: (i, j),
               b_blk=(None, g, g), b_map=lambda j, i, k: (j, 0, 0), o_shape=(s_len, e), o_blk=(bm, g),
               o_map=lambda j, i, k: (i, j), dims=NT, out_dtype=F32)


def _mm_grp_grad_w(name, pooled, dy, ng):
    s_len, e = pooled.shape
    g = e // ng
    bk = _tile(s_len, 1024)
    return _mm(name, pooled, dy, grid2=(ng, 1), nk=s_len // bk, a_blk=(bk, g), a_map=lambda j, i, k: (k, j),
               b_blk=(bk, g), b_map=lambda j, i, k: (k, j), o_shape=(N_CHIPS, ng, g // N_CHIPS, g),
               o_blk=(N_CHIPS, None, g // N_CHIPS, g), o_map=lambda j, i, k: (0, j, 0, 0),
               dims=TN, out_dtype=BF16, acc_shape=(g, g))


def _row_spec(bs, width, col=0):
    return pl.BlockSpec((bs, width), lambda i: (i, col))


def _vec_spec(width):
    return pl.BlockSpec((1, width), lambda i: (0, 0))


def _rows_call(body, name, s_len, in_specs, out_specs, out_shape, bs, aliases=None, sequential=False):
    return pl.pallas_call(
        body, name=name, grid=(s_len // bs,), in_specs=in_specs, out_specs=out_specs, out_shape=out_shape,
        input_output_aliases=aliases or {},
        compiler_params=_params(("arbitrary",) if sequential else ("parallel",)))


def _accumulate(ref, part):
    i = pl.program_id(0)

    @pl.when(i == 0)
    def _():
        ref[...] = part

    @pl.when(i > 0)
    def _():
        ref[...] += part


def _rms_scale(xf):
    return lax.rsqrt(jnp.mean(xf * xf, axis=-1, keepdims=True) + RMS_EPS)


def _rmsnorm_fwd(name, x, gain):
    s_len, d = x.shape
    bs = _tile(s_len, 512)

    def body(x_ref, g_ref, h_ref):
        xf = x_ref[...]
        h_ref[...] = ((xf * _rms_scale(xf)) * g_ref[...]).astype(BF16)

    return _rows_call(body, name, s_len, [_row_spec(bs, d), _vec_spec(d)], _row_spec(bs, d),
                      jax.ShapeDtypeStruct((s_len, d), BF16), bs)(x, gain)


def _rmsnorm_bwd(name, x, gain, dh, dres):
    s_len, d = x.shape
    bs = _tile(s_len, 256)

    def body(x_ref, g_ref, dh_ref, dres_ref, dx_ref, dxb_ref, dg_ref):
        xf = x_ref[...]
        r = _rms_scale(xf)
        xh = xf * r
        dh_f = dh_ref[...]
        t = dh_f * g_ref[...]
        dx = dres_ref[...] + r * (t - xh * jnp.mean(t * xh, axis=-1, keepdims=True))
        dx_ref[...] = dx
        dxb_ref[...] = dx.astype(BF16)
        _accumulate(dg_ref, jnp.sum(dh_f * xh, axis=0, keepdims=True))

    return _rows_call(
        body, name, s_len,
        [_row_spec(bs, d), _vec_spec(d), _row_spec(bs, d), _row_spec(bs, d)],
        [_row_spec(bs, d), _row_spec(bs, d), _vec_spec(d)],
        [jax.ShapeDtypeStruct((s_len, d), F32), jax.ShapeDtypeStruct((s_len, d), BF16),
         jax.ShapeDtypeStruct((1, d), F32)], bs, sequential=True)(x, gain, dh, dres)


def _loss_head(name, x, gain, target):
    s_len, d = x.shape
    bs = _tile(s_len, 256)

    def body(x_ref, g_ref, t_ref, lv_ref, dx_ref, dxb_ref, dg_ref):
        xf = x_ref[...]
        r = _rms_scale(xf)
        xh = xf * r
        err = xh * g_ref[...] - t_ref[...]
        dy = err * (1.0 / d)
        t = dy * g_ref[...]
        dx = r * (t - xh * jnp.mean(t * xh, axis=-1, keepdims=True))
        dx_ref[...] = dx
        dxb_ref[...] = dx.astype(BF16)
        _accumulate(lv_ref, jnp.sum(err * err, axis=0, keepdims=True))
        _accumulate(dg_ref, jnp.sum(dy * xh, axis=0, keepdims=True))

    return _rows_call(
        body, name, s_len, [_row_spec(bs, d), _vec_spec(d), _row_spec(bs, d)],
        [_vec_spec(d), _row_spec(bs, d), _row_spec(bs, d), _vec_spec(d)],
        [jax.ShapeDtypeStruct((1, d), F32), jax.ShapeDtypeStruct((s_len, d), F32),
         jax.ShapeDtypeStruct((s_len, d), BF16), jax.ShapeDtypeStruct((1, d), F32)],
        bs, sequential=True)(x, gain, target)


def _sigmoid(g):
    return 1.0 / (1.0 + jnp.exp(-g))


def _gate_a_fwd(name, ypre, proj, scale):
    s_len, e = ypre.shape
    bs = _tile(s_len, 256)

    def body(y_ref, g_ref, sc_ref, z_ref):
        g = g_ref[...]
        z_ref[...] = (y_ref[...] * sc_ref[...] * (g * _sigmoid(g))).astype(BF16)

    return _rows_call(body, name, s_len, [_row_spec(bs, e), _row_spec(bs, e, 1), _vec_spec(e)], _row_spec(bs, e),
                      jax.ShapeDtypeStruct((s_len, e), BF16), bs)(ypre, proj, scale)


def _gate_a_bwd(name, dz, ypre, proj, scale):
    s_len, e = ypre.shape
    bs = _tile(s_len, 256)

    def body(dz_ref, y_ref, g_ref, sc_ref, dy_ref, dproj_ref, dsc_ref):
        g = g_ref[...]
        sg = _sigmoid(g)
        silu = g * sg
        dz_f = dz_ref[...]
        ypre_f = y_ref[...]
        dys = dz_f * silu
        dy_ref[...] = (dys * sc_ref[...]).astype(BF16)
        dproj_ref[...] = (dz_f * (ypre_f * sc_ref[...]) * (sg * (1.0 + g * (1.0 - sg)))).astype(BF16)
        _accumulate(dsc_ref, jnp.sum(dys * ypre_f, axis=0, keepdims=True))

    return _rows_call(
        body, name, s_len, [_row_spec(bs, e), _row_spec(bs, e), _row_spec(bs, e, 1), _vec_spec(e)],
        [_row_spec(bs, e), _row_spec(bs, e, 1), _vec_spec(e)],
        [jax.ShapeDtypeStruct((s_len, e), BF16), jax.ShapeDtypeStruct((s_len, 2 * e), BF16),
         jax.ShapeDtypeStruct((1, e), F32)], bs, sequential=True)(dz, ypre, proj, scale)


def _merge_gate_fwd(name, outs, lses, gate):
    s_len, e = gate.shape
    bs = _tile(s_len, 256)
    n = len(outs)

    def body(*refs):
        o_refs, l_refs, g_ref = refs[:n], refs[n:2 * n], refs[2 * n]
        m_ref, lj_ref, z_ref = refs[2 * n + 1:]
        ls = [r[...] for r in l_refs]
        mx = functools.reduce(jnp.maximum, ls)
        ws = [jnp.exp(l - mx) for l in ls]
        den = functools.reduce(lambda a, b: a + b, ws)
        merged = functools.reduce(lambda a, b: a + b, [w * o[...].astype(F32) for w, o in zip(ws, o_refs)]) / den
        g = g_ref[...]
        m_ref[...] = merged.astype(BF16)
        lj_ref[...] = mx + jnp.log(den)
        z_ref[...] = (merged * (g * _sigmoid(g))).astype(BF16)

    spec = _row_spec(bs, e)
    return _rows_call(
        body, name, s_len, [spec] * (2 * n + 1), [spec] * 3,
        [jax.ShapeDtypeStruct((s_len, e), BF16), jax.ShapeDtypeStruct((s_len, e), F32),
         jax.ShapeDtypeStruct((s_len, e), BF16)], bs)(*outs, *lses, gate)


def _gate_b_bwd(name, dz, merged, gate, n_q):
    s_len, e = gate.shape
    bs = _tile(s_len, 256)

    def body(dz_ref, m_ref, g_ref, dm_ref, dproj_ref):
        g = g_ref[...]
        sg = _sigmoid(g)
        dz_f = dz_ref[...]
        dm_ref[...] = (dz_f * (g * sg)).astype(BF16)
        dproj_ref[...] = (dz_f * m_ref[...].astype(F32) * (sg * (1.0 + g * (1.0 - sg)))).astype(BF16)

    spec = _row_spec(bs, e)
    return _rows_call(
        body, name, s_len, [spec] * 3, [spec, _row_spec(bs, e, n_q)],
        [jax.ShapeDtypeStruct((s_len, e), BF16), jax.ShapeDtypeStruct((s_len, (n_q + 1) * e), BF16)],
        bs)(dz, merged, gate)


def _kv_grad_prep(name, dk_acc, dv_acc, cos, sin_inv):
    s_len, e = dk_acc.shape
    bs = _tile(s_len, 256)

    def body(dk_ref, dv_ref, c_ref, s_ref, dkb_ref, dvb_ref):
        cos_t, sin_t = c_ref[...], s_ref[...]
        for h in range(e // HEAD_DIM):
            sl = slice(h * HEAD_DIM, (h + 1) * HEAD_DIM)
            dkb_ref[:, sl] = _rope_apply(dk_ref[:, sl], cos_t, sin_t).astype(BF16)
        dvb_ref[...] = dv_ref[...].astype(BF16)

    spec, rspec = _row_spec(bs, e), _row_spec(bs, HEAD_DIM)
    return _rows_call(body, name, s_len, [spec, spec, rspec, rspec], [spec, spec],
                      [jax.ShapeDtypeStruct((s_len, e), BF16)] * 2, bs)(dk_acc, dv_acc, cos, sin_inv)


def _pool_cols(e):
    return _tile(e // len(POOL_WINDOWS), 256)


def _window_sum(val, grp, s_len, forward):
    rows = lax.broadcasted_iota(jnp.int32, val.shape, 0)
    acc = val
    for level in range(len(POOL_WINDOWS)):
        step = 1 << level
        if forward:
            shifted = jnp.where(rows >= step, pltpu.roll(acc, step, 0), 0.0)
        else:
            shifted = jnp.where(rows < s_len - step, pltpu.roll(acc, s_len - step, 0), 0.0)
        acc = jnp.where(level <= grp, acc + shifted, acc)
    return acc


def _window_count(shape, grp):
    rows = lax.broadcasted_iota(jnp.int32, shape, 0)
    return jnp.minimum(rows + 1, jnp.left_shift(2, grp)).astype(F32)


def _pool_fwd(name, proj):
    s_len, e2 = proj.shape
    e = e2 // 2
    cb = _pool_cols(e)
    per_grp = e // len(POOL_WINDOWS) // cb
    assert POOL_WINDOWS == tuple(2 << g for g in range(len(POOL_WINDOWS)))

    def body(u_ref, p_ref):
        grp = pl.program_id(0)
        u = u_ref[...]
        total = _window_sum(u, grp, s_len, True)
        p_ref[...] = (total / _window_count(u.shape, grp) - u).astype(BF16)

    spec = pl.BlockSpec((s_len, cb), lambda g, c: (0, g * per_grp + c))
    return pl.pallas_call(
        body, name=name, grid=(len(POOL_WINDOWS), per_grp), in_specs=[spec], out_specs=spec,
        out_shape=jax.ShapeDtypeStruct((s_len, e), BF16), compiler_params=_params(("parallel", "parallel")))(proj)


def _pool_bwd(name, dpooled, dproj):
    s_len, e = dpooled.shape
    cb = _pool_cols(e)
    per_grp = e // len(POOL_WINDOWS) // cb

    def body(dp_ref, _, du_ref):
        grp = pl.program_id(0)
        dp = dp_ref[...]
        total = _window_sum(dp / _window_count(dp.shape, grp), grp, s_len, False)
        du_ref[...] = (total - dp).astype(BF16)

    spec = pl.BlockSpec((s_len, cb), lambda g, c: (0, g * per_grp + c))
    return pl.pallas_call(
        body, name=name, grid=(len(POOL_WINDOWS), per_grp), in_specs=[spec, ANY], out_specs=spec,
        out_shape=jax.ShapeDtypeStruct(dproj.shape, BF16), input_output_aliases={1: 0},
        compiler_params=_params(("parallel", "parallel")))(dpooled, dproj)


def _band_masks(nb, first):
    row = lax.broadcasted_iota(jnp.int32, (nb, nb), 0)
    col = lax.broadcasted_iota(jnp.int32, (nb, nb), 1)
    return col >= row + jnp.where(first, 2 * nb, 0), col <= row


def _dot(a, b, dims):
    return lax.dot_general(a, b, (dims, ((), ())), preferred_element_type=F32)


def _attn_fwd(name, grp, n_q, window, dil, q_all, k, v):
    s_len, e = k.shape
    nb = window // dil
    m = s_len // dil
    nblk = m // nb
    heads = e // HEAD_DIM

    def body(q_ref, kp_ref, kc_ref, vp_ref, vc_ref, o_ref, l_ref):
        mask_p, mask_c = _band_masks(nb, pl.program_id(1) == 0)
        for h in range(heads):
            sl = slice(h * HEAD_DIM, (h + 1) * HEAD_DIM)
            q = q_ref[:, sl]
            s_p = jnp.where(mask_p, _dot(q, kp_ref[:, sl], NT), NEG_INF)
            s_c = jnp.where(mask_c, _dot(q, kc_ref[:, sl], NT), NEG_INF)
            mx = jnp.maximum(jnp.max(s_p, axis=-1, keepdims=True), jnp.max(s_c, axis=-1, keepdims=True))
            p_p = jnp.exp(s_p - mx)
            p_c = jnp.exp(s_c - mx)
            den = jnp.sum(p_p, axis=-1, keepdims=True) + jnp.sum(p_c, axis=-1, keepdims=True)
            out = _dot(p_p.astype(BF16), vp_ref[:, sl], NN) + _dot(p_c.astype(BF16), vc_ref[:, sl], NN)
            o_ref[:, sl] = (out / den).astype(BF16)
            l_ref[:, sl] = jnp.broadcast_to(mx + jnp.log(den), (nb, HEAD_DIM))

    blk = (nb, e)
    prev = lambda r, n: (jnp.maximum(n - 1, 0), r)
    cur = lambda r, n: (n, r)
    return pl.pallas_call(
        body, name=name, grid=(dil, nblk),
        in_specs=[pl.BlockSpec(blk, lambda r, n: (n, r * n_q + grp)),
                  pl.BlockSpec(blk, prev), pl.BlockSpec(blk, cur), pl.BlockSpec(blk, prev), pl.BlockSpec(blk, cur)],
        out_specs=[pl.BlockSpec(blk, cur), pl.BlockSpec(blk, cur)],
        out_shape=[jax.ShapeDtypeStruct((m, dil * e), BF16), jax.ShapeDtypeStruct((m, dil * e), F32)],
        compiler_params=_params(("parallel", "arbitrary")),
    )(q_all.reshape(m, dil * n_q * e), *([k.reshape(m, dil * e)] * 2), *([v.reshape(m, dil * e)] * 2))


def _attn_bwd(name, grp, n_q, window, dil, scale, q_all, k, v, dmerged, merged, lse, cos, sin_inv, dproj, dk_acc, dv_acc):
    s_len, e = k.shape
    nb = window // dil
    m = s_len // dil
    nblk = m // nb
    heads = e // HEAD_DIM

    def body(q_ref, kp_ref, kc_ref, vp_ref, vc_ref, do_ref, o_ref, l_ref, c_ref, s_ref, _, dki_ref, dvi_ref,
             dq_ref, dko_ref, dvo_ref, ck_ref, cv_ref):
        n = pl.program_id(1)

        @pl.when(n == 0)
        def _():
            ck_ref[...] = jnp.zeros_like(ck_ref)
            cv_ref[...] = jnp.zeros_like(cv_ref)

        @pl.when(n < nblk)
        def _():
            mask_p, mask_c = _band_masks(nb, n == 0)
            cos_t, sin_t = c_ref[...], s_ref[...]
            for h in range(heads):
                sl = slice(h * HEAD_DIM, (h + 1) * HEAD_DIM)
                q, kp, kc, vp, vc = q_ref[:, sl], kp_ref[:, sl], kc_ref[:, sl], vp_ref[:, sl], vc_ref[:, sl]
                do = do_ref[:, sl]
                lj = l_ref[:, sl] if nb == HEAD_DIM else l_ref[:, sl][:, :1]
                delta = jnp.sum(do.astype(F32) * o_ref[:, sl].astype(F32), axis=-1, keepdims=True)
                p_p = jnp.where(mask_p, jnp.exp(_dot(q, kp, NT) - lj), 0.0)
                p_c = jnp.where(mask_c, jnp.exp(_dot(q, kc, NT) - lj), 0.0)
                ds_p = (p_p * (_dot(do, vp, NT) - delta)).astype(BF16)
                ds_c = (p_c * (_dot(do, vc, NT) - delta)).astype(BF16)
                dq = (_dot(ds_p, kp, NN) + _dot(ds_c, kc, NN)) * scale
                dq_ref[:, sl] = _rope_apply(dq, cos_t, sin_t).astype(BF16)
                dko_ref[:, sl] = dki_ref[:, sl] + ck_ref[:, sl] + _dot(ds_p, q, TN)
                dvo_ref[:, sl] = dvi_ref[:, sl] + cv_ref[:, sl] + _dot(p_p.astype(BF16), do, TN)
                ck_ref[:, sl] = _dot(ds_c, q, TN)
                cv_ref[:, sl] = _dot(p_c.astype(BF16), do, TN)

        @pl.when(n == nblk)
        def _():
            dko_ref[...] = dki_ref[...] + ck_ref[...]
            dvo_ref[...] = dvi_ref[...] + cv_ref[...]

    blk = (nb, e)
    qn = lambda n: jnp.minimum(n, nblk - 1)
    cur = lambda r, n: (qn(n), r)
    prev = lambda r, n: (jnp.maximum(qn(n) - 1, 0), r)
    kprev = lambda r, n: (jnp.maximum(n - 1, 0), r)
    rblk = (nb, HEAD_DIM)
    view = lambda a: a.reshape(m, -1)
    return pl.pallas_call(
        body, name=name, grid=(dil, nblk + 1),
        in_specs=[pl.BlockSpec(blk, lambda r, n: (qn(n), r * n_q + grp)),
                  pl.BlockSpec(blk, prev), pl.BlockSpec(blk, cur), pl.BlockSpec(blk, prev), pl.BlockSpec(blk, cur),
                  pl.BlockSpec(blk, cur), pl.BlockSpec(blk, cur), pl.BlockSpec(blk, cur),
                  pl.BlockSpec(rblk, cur), pl.BlockSpec(rblk, cur), ANY,
                  pl.BlockSpec(blk, kprev), pl.BlockSpec(blk, kprev)],
        out_specs=[pl.BlockSpec(blk, lambda r, n: (qn(n), r * (n_q + 1) + grp)),
                   pl.BlockSpec(blk, kprev), pl.BlockSpec(blk, kprev)],
        out_shape=[jax.ShapeDtypeStruct((m, dil * (n_q + 1) * e), BF16),
                   jax.ShapeDtypeStruct((m, dil * e), F32), jax.ShapeDtypeStruct((m, dil * e), F32)],
        scratch_shapes=[pltpu.VMEM(blk, F32), pltpu.VMEM(blk, F32)],
        input_output_aliases={10: 0, 11: 1, 12: 2},
        compiler_params=_params(("parallel", "arbitrary")),
    )(view(q_all), view(k), view(k), view(v), view(v), view(dmerged), view(merged), view(lse), view(cos), view(sin_inv),
      view(dproj), view(dk_acc), view(dv_acc))


def _rope_tables(s_len):
    inv_freq = 1.0 / (ROPE_THETA ** (jnp.arange(0, HEAD_DIM, 2, dtype=F32) / HEAD_DIM))
    ang = jnp.arange(s_len, dtype=F32)[:, None] * inv_freq[None, :]
    cos, sin = jnp.cos(ang), jnp.sin(ang)
    return jnp.concatenate([cos, cos], axis=1), jnp.concatenate([-sin, sin], axis=1)


def _row(vec):
    return vec.reshape(1, -1)


def _local_step(x, target, w):
    s_len, d = x.shape
    n_a, n_b = len(w["w_in_a"]), len(w["w_in_b"])
    n_q = len(DILATED_PAIRS)
    e = w["w_k"][0].shape[1]
    cos, sin = _rope_tables(s_len)
    sin_inv = -sin
    q_scale = 1.0 / math.sqrt(HEAD_DIM)

    saved_a = []
    for i in range(n_a):
        h = _rmsnorm_fwd(f"a{i}_norm", x, _row(w["norm_a"][i]))
        proj = _mm_act_w(f"a{i}_in", h, w["w_in_a"][i], out_dtype=F32)
        pooled = _pool_fwd(f"a{i}_pool", proj)
        ypre = _mm_grp_fwd(f"a{i}_grp", pooled, w["w_grp_a"][i])
        z = _gate_a_fwd(f"a{i}_gate", ypre, proj, _row(w["scale_a"][i]))
        x_next = _mm_act_w(f"a{i}_out", z, w["w_out_a"][i], out_dtype=F32, add=x)
        saved_a.append((x, h, proj, pooled, ypre, z))
        x = x_next

    x_kv = x
    kv_in = _rmsnorm_fwd("kv_norm", x, _row(w["norm_kv"][0]))
    k = _mm_act_w("kv_k", kv_in, w["w_k"][0], out_dtype=BF16, rope=(cos, sin, 1.0))
    v = _mm_act_w("kv_v", kv_in, w["w_v"][0], out_dtype=BF16)

    saved_b = []
    for i in range(n_b):
        h = _rmsnorm_fwd(f"b{i}_norm", x, _row(w["norm_b"][i]))
        q_all = _mm_act_w(f"b{i}_q", h, w["w_in_b"][i], out_dtype=BF16, rope=(cos, sin, q_scale), n_cols=n_q)
        gate = _mm_act_w(f"b{i}_g", h, w["w_in_b"][i], out_dtype=F32, n_first=n_q, n_cols=1)
        outs, lses = [], []
        for g, (window, dil) in enumerate(DILATED_PAIRS):
            o_g, l_g = _attn_fwd(f"b{i}_attn{g}", g, n_q, window, dil, q_all, k, v)
            outs.append(o_g.reshape(s_len, e))
            lses.append(l_g.reshape(s_len, e))
        merged, lse, z = _merge_gate_fwd(f"b{i}_merge", outs, lses, gate)
        x_next = _mm_act_w(f"b{i}_out", z, w["w_out_b"][i], out_dtype=F32, add=x)
        saved_b.append((x, h, q_all, gate, merged, lse, z))
        x = x_next

    loss_vec, dx, dxb, g_norm_f = _loss_head("loss_head", x, _row(w["norm_f"][0]), target)

    big = {name: [None] * len(w[name]) for name in BIG_WEIGHTS}
    small = {"norm_a": [None] * n_a, "scale_a": [None] * n_a, "norm_kv": [None], "norm_b": [None] * n_b,
             "norm_f": [g_norm_f]}
    shard_rows = lambda g2: g2.reshape(N_CHIPS, g2.shape[0] // N_CHIPS, g2.shape[1])

    dk_acc = jnp.zeros((s_len, e), F32)
    dv_acc = jnp.zeros((s_len, e), F32)
    for i in reversed(range(n_b)):
        x_in, h, q_all, gate, merged, lse, z = saved_b[i]
        dz = _mm_grad_act(f"b{i}_dz", dxb, w["w_out_b"][i])
        big["w_out_b"][i] = shard_rows(_mm_grad_w(f"b{i}_gwo", z, dxb))
        dmerged, dproj = _gate_b_bwd(f"b{i}_dgate", dz, merged, gate, n_q)
        for g, (window, dil) in enumerate(DILATED_PAIRS):
            dproj, dk_acc, dv_acc = _attn_bwd(f"b{i}_dattn{g}", g, n_q, window, dil, q_scale, q_all, k, v,
                                              dmerged, merged, lse, cos, sin_inv, dproj, dk_acc, dv_acc)
            dproj = dproj.reshape(s_len, (n_q + 1) * e)
        dh = _mm_grad_act(f"b{i}_dh", dproj, w["w_in_b"][i])
        big["w_in_b"][i] = _mm_grad_w(f"b{i}_gwi", h, dproj, col_shards=N_CHIPS)
        dx, dxb, small["norm_b"][i] = _rmsnorm_bwd(f"b{i}_dnorm", x_in, _row(w["norm_b"][i]), dh, dx)

    dkb, dvb = _kv_grad_prep("kv_dprep", dk_acc.reshape(s_len, e), dv_acc.reshape(s_len, e), cos, sin_inv)
    dkv = _mm_grad_act("kv_dk", dkb, w["w_k"][0])
    dkv = _mm_grad_act("kv_dv", dvb, w["w_v"][0], add=dkv)
    big["w_k"][0] = shard_rows(_mm_grad_w("kv_gwk", kv_in, dkb))
    big["w_v"][0] = shard_rows(_mm_grad_w("kv_gwv", kv_in, dvb))
    dx, dxb, small["norm_kv"][0] = _rmsnorm_bwd("kv_dnorm", x_kv, _row(w["norm_kv"][0]), dkv, dx)

    for i in reversed(range(n_a)):
        x_in, h, proj, pooled, ypre, z = saved_a[i]
        dz = _mm_grad_act(f"a{i}_dz", dxb, w["w_out_a"][i])
        big["w_out_a"][i] = shard_rows(_mm_grad_w(f"a{i}_gwo", z, dxb))
        dypre, dproj, small["scale_a"][i] = _gate_a_bwd(f"a{i}_dgate", dz, ypre, proj, _row(w["scale_a"][i]))
        dpooled = _mm_grp_grad_act(f"a{i}_dgrp", dypre, w["w_grp_a"][i])
        g_grp = _mm_grp_grad_w(f"a{i}_gwg", pooled, dypre, len(POOL_WINDOWS))
        big["w_grp_a"][i] = g_grp.reshape(N_CHIPS, -1, g_grp.shape[-1])
        dproj = _pool_bwd(f"a{i}_dpool", dpooled, dproj)
        dh = _mm_grad_act(f"a{i}_dh", dproj, w["w_in_a"][i])
        big["w_in_a"][i] = _mm_grad_w(f"a{i}_gwi", h, dproj, col_shards=N_CHIPS)
        dx, dxb, small["norm_a"][i] = _rmsnorm_bwd(f"a{i}_dnorm", x_in, _row(w["norm_a"][i]), dh, dx)

    return loss_vec, dx, big, small


BIG_WEIGHTS = ("w_in_a", "w_grp_a", "w_out_a", "w_k", "w_v", "w_in_b", "w_out_b")


def _cast_bf16(name, arr):
    rows, cols = arr.shape
    bs = _tile(rows, 512)

    def body(a_ref, o_ref):
        o_ref[...] = a_ref[...].astype(BF16)

    return _rows_call(body, name, rows, [_row_spec(bs, cols)], _row_spec(bs, cols),
                      jax.ShapeDtypeStruct((rows, cols), BF16), bs)(arr)


def _pair_add(name, grad, recv, c_idx):
    _, r, cols = grad.shape
    half = r // 2
    rb = _tile(half, 256)
    nrb = half // rb

    def body(c_ref, g_ref, r_ref, o_ref):
        o_ref[...] = (g_ref[...].astype(F32) + r_ref[...].astype(F32)).astype(BF16)

    blk = (None, rb, cols)
    grid_spec = pltpu.PrefetchScalarGridSpec(
        num_scalar_prefetch=1, grid=(N_CHIPS, nrb),
        in_specs=[pl.BlockSpec(blk, lambda s, i, c: (s, c[0] * nrb + i, 0)), pl.BlockSpec(blk, lambda s, i, c: (s, i, 0))],
        out_specs=pl.BlockSpec(blk, lambda s, i, c: (s, i, 0)))
    return pl.pallas_call(body, name=name, grid_spec=grid_spec,
                          out_shape=jax.ShapeDtypeStruct((N_CHIPS, half, cols), BF16),
                          compiler_params=_params(("parallel", "parallel")))(c_idx, grad, recv)


def _final_add(name, part, recv, s_idx):
    _, half, cols = part.shape
    rb = _tile(half, 256)
    n_peer = recv.shape[0]

    def body(s_ref, p_ref, *refs):
        acc = p_ref[...].astype(F32)
        for r_ref in refs[:n_peer]:
            acc = acc + r_ref[...].astype(F32)
        refs[n_peer][...] = acc

    blk = (None, rb, cols)
    peer_spec = lambda k: pl.BlockSpec(blk, lambda i, s: (k, i, 0))
    grid_spec = pltpu.PrefetchScalarGridSpec(
        num_scalar_prefetch=1, grid=(half // rb,),
        in_specs=[pl.BlockSpec(blk, lambda i, s: (s[0], i, 0))] + [peer_spec(k) for k in range(n_peer)],
        out_specs=pl.BlockSpec((rb, cols), lambda i, s: (i, 0)))
    return pl.pallas_call(body, name=name, grid_spec=grid_spec, out_shape=jax.ShapeDtypeStruct((half, cols), F32),
                          compiler_params=_params(("parallel",)))(s_idx, part, *([recv] * n_peer))


def _sum_devices(name, gathered):
    n_dev, p, d = gathered.shape

    def body(g_ref, o_ref):
        acc = g_ref[0]
        for j in range(1, n_dev):
            acc = acc + g_ref[j]
        o_ref[...] = acc

    return pl.pallas_call(body, name=name, out_shape=jax.ShapeDtypeStruct((p, d), F32),
                          compiler_params=_params())(gathered)


def _adamw(name, w, g, m, v):
    shape = w.shape
    cols = shape[-1]
    flat = lambda a: a.reshape(-1, cols)
    rows = flat(w).shape[0]
    bs = _tile(rows, 256)

    def body(w_ref, g_ref, m_ref, v_ref, d_ref, mo_ref, vo_ref):
        grad = g_ref[...]
        m_new = ADAM_B1 * m_ref[...] + (1.0 - ADAM_B1) * grad
        v_new = ADAM_B2 * v_ref[...] + (1.0 - ADAM_B2) * (grad * grad)
        m_hat = m_new / (1.0 - ADAM_B1 ** ADAM_STEP)
        v_hat = v_new / (1.0 - ADAM_B2 ** ADAM_STEP)
        d_ref[...] = -ADAM_LR * (m_hat / (jnp.sqrt(v_hat) + ADAM_EPS) + ADAM_WD * w_ref[...])
        mo_ref[...] = m_new
        vo_ref[...] = v_new

    spec = _row_spec(bs, cols)
    outs = _rows_call(body, name, rows, [spec] * 4, [spec] * 3, [jax.ShapeDtypeStruct((rows, cols), F32)] * 3, bs)(
        flat(w), flat(g), flat(m), flat(v))
    return tuple(o.reshape(shape) for o in outs)


def _place():
    x, y, c = lax.axis_index("x"), lax.axis_index("y"), lax.axis_index("c")
    chips = [(1 - x, y), (x, 1 - y), (1 - x, 1 - y)]
    return x, y, c, chips


def _chip_index(chip):
    return 2 * chip[0] + chip[1]


def _comm_call(body, name, n_in, out_shape, scratch):
    return pl.pallas_call(body, name=name, in_specs=[ANY] * n_in, out_specs=[ANY] * len(out_shape), out_shape=out_shape,
                          scratch_shapes=scratch)


def _gather_weights(bigs, smalls):
    items = [(a, l) for a, arr in enumerate(bigs) for l in range(arr.shape[0])]
    n_big, n_small, n_items = len(bigs), len(smalls), len(items)

    def body(*refs):
        big_in, small_in = refs[:n_big], refs[n_big:n_big + n_small]
        outs = refs[n_big + n_small:n_big + n_small + n_items]
        small_out = refs[n_big + n_small + n_items:n_big + n_small + n_items + n_small]
        send_sems, recv_sems, local_sems, s_send, s_recv, s_local = refs[-6:]
        x, y, c, chips = _place()
        me, sibling = _chip_index((x, y)), (x, y, 1 - c)

        def rows(t, core):
            half = outs[t].shape[2] // 2
            return pl.ds(core * half, half)

        def landing(t, chip, core):
            return outs[t].at[:, _chip_index(chip), rows(t, core), :]

        def copy(t, k, src, dst, to):
            return pltpu.make_async_remote_copy(src_ref=src, dst_ref=dst, send_sem=send_sems.at[t, k],
                                                recv_sem=recv_sems.at[t, k], device_id=to, device_id_type=MESH)

        def small_copy(j, k, chip):
            return pltpu.make_async_remote_copy(src_ref=small_in[j], dst_ref=small_out[j].at[me],
                                                send_sem=s_send.at[j, k], recv_sem=s_recv.at[j, k],
                                                device_id=(*chip, c), device_id_type=MESH)

        started = []
        local = []
        for t, (a, l) in enumerate(items):
            own = pltpu.make_async_copy(big_in[a].at[l], outs[t].at[:, me], local_sems.at[t])
            own.start()
            local.append(own)
            for k, chip in enumerate(chips):
                cp = copy(t, k, big_in[a].at[l, :, rows(t, c), :], landing(t, (x, y), c), (*chip, c))
                cp.start()
                started.append(cp)
        for j in range(n_small):
            own = pltpu.make_async_copy(small_in[j], small_out[j].at[me], s_local.at[j])
            own.start()
            local.append(own)
            for k, chip in enumerate(chips):
                cp = small_copy(j, k, chip)
                cp.start()
                started.append(cp)
        for t in range(n_items):
            for k, chip in enumerate(chips):
                copy(t, k, landing(t, chip, c), landing(t, chip, c), (x, y, c)).wait_recv()
                fwd = copy(t, 3 + k, landing(t, chip, c), landing(t, chip, c), sibling)
                fwd.start()
                started.append(fwd)
        for t in range(n_items):
            for k, chip in enumerate(chips):
                copy(t, 3 + k, landing(t, chip, 1 - c), landing(t, chip, 1 - c), sibling).wait_recv()
        for j in range(n_small):
            for k, chip in enumerate(chips):
                pltpu.make_async_remote_copy(src_ref=small_in[j], dst_ref=small_out[j].at[_chip_index(chip)],
                                             send_sem=s_send.at[j, k], recv_sem=s_recv.at[j, k],
                                             device_id=(*chip, c), device_id_type=MESH).wait_recv()
        for cp in started:
            cp.wait_send()
        for own in local:
            own.wait()

    out_shape = [jax.ShapeDtypeStruct((bigs[a].shape[1], N_CHIPS) + bigs[a].shape[2:], BF16) for a, _ in items]
    out_shape += [jax.ShapeDtypeStruct((N_CHIPS,) + s.shape, F32) for s in smalls]
    dma = pltpu.SemaphoreType.DMA
    res = _comm_call(body, "gather_weights", n_big + n_small, out_shape,
                     [dma((n_items, 6)), dma((n_items, 6)), dma((n_items,)),
                      dma((n_small, 3)), dma((n_small, 3)), dma((n_small,))])(*bigs, *smalls)
    per_weight, t = [], 0
    for arr in bigs:
        per_weight.append(list(res[t:t + arr.shape[0]]))
        t += arr.shape[0]
    return per_weight, list(res[n_items:])


def _exchange_halves(grads):
    n = len(grads)

    def body(*refs):
        g_in, outs = refs[:n], refs[n:2 * n]
        send_sems, recv_sems = refs[-2:]
        x, y, c, _ = _place()
        copies = []
        for t in range(n):
            half = g_in[t].shape[1] // 2
            cp = pltpu.make_async_remote_copy(
                src_ref=g_in[t].at[:, pl.ds((1 - c) * half, half), :], dst_ref=outs[t], send_sem=send_sems.at[t],
                recv_sem=recv_sems.at[t], device_id=(x, y, 1 - c), device_id_type=MESH)
            cp.start()
            copies.append(cp)
        for cp in copies:
            cp.wait()

    out_shape = [jax.ShapeDtypeStruct((g.shape[0], g.shape[1] // 2, g.shape[2]), BF16) for g in grads]
    dma = pltpu.SemaphoreType.DMA
    return list(_comm_call(body, "grad_exchange_halves", n, out_shape, [dma((n,)), dma((n,))])(*grads))


def _scatter_partials(parts):
    n = len(parts)

    def body(*refs):
        p_in, outs = refs[:n], refs[n:2 * n]
        send_sems, recv_sems = refs[-2:]
        x, y, c, chips = _place()
        copies = []
        for t in range(n):
            for k, chip in enumerate(chips):
                cp = pltpu.make_async_remote_copy(
                    src_ref=p_in[t].at[_chip_index(chip)], dst_ref=outs[t].at[k], send_sem=send_sems.at[t, k],
                    recv_sem=recv_sems.at[t, k], device_id=(*chip, c), device_id_type=MESH)
                cp.start()
                copies.append(cp)
        for cp in copies:
            cp.wait()

    out_shape = [jax.ShapeDtypeStruct((3,) + p.shape[1:], BF16) for p in parts]
    dma = pltpu.SemaphoreType.DMA
    return list(_comm_call(body, "grad_scatter_partials", n, out_shape, [dma((n, 3)), dma((n, 3))])(*parts))


def _share_halves(halves):
    items = [(a, l) for a, lst in enumerate(halves) for l in range(len(lst))]
    flat = [h for lst in halves for h in lst]
    n, n_w = len(flat), len(halves)

    def body(*refs):
        h_in, outs = refs[:n], refs[n:n + n_w]
        send_sems, recv_sems, local_sems = refs[-3:]
        x, y, c, _ = _place()
        pending = []
        for t, (a, l) in enumerate(items):
            half = h_in[t].shape[0]
            mine = outs[a].at[l, pl.ds(c * half, half), :]
            own = pltpu.make_async_copy(h_in[t], mine, local_sems.at[t])
            own.start()
            cp = pltpu.make_async_remote_copy(src_ref=h_in[t], dst_ref=mine, send_sem=send_sems.at[t],
                                              recv_sem=recv_sems.at[t], device_id=(x, y, 1 - c), device_id_type=MESH)
            cp.start()
            pending.append((own, cp))
        for t, (a, l) in enumerate(items):
            half = h_in[t].shape[0]
            theirs = outs[a].at[l, pl.ds((1 - c) * half, half), :]
            pltpu.make_async_remote_copy(src_ref=h_in[t], dst_ref=theirs, send_sem=send_sems.at[t],
                                         recv_sem=recv_sems.at[t], device_id=(x, y, 1 - c),
                                         device_id_type=MESH).wait_recv()
        for own, cp in pending:
            cp.wait_send()
            own.wait()

    out_shape = [jax.ShapeDtypeStruct((len(lst), 2 * lst[0].shape[0], lst[0].shape[1]), F32) for lst in halves]
    dma = pltpu.SemaphoreType.DMA
    return list(_comm_call(body, "grad_share_halves", n, out_shape, [dma((n,)), dma((n,)), dma((n,))])(*flat))


def _allgather_small(packed):
    def body(p_ref, o_ref, send_sems, recv_sems, local_sem):
        x, y, c, _ = _place()
        me = 4 * x + 2 * y + c
        own = pltpu.make_async_copy(p_ref, o_ref.at[me], local_sem)
        own.start()
        flips = [(fx, fy, fc) for fx in (0, 1) for fy in (0, 1) for fc in (0, 1)][1:]
        peers = [(x ^ fx, y ^ fy, c ^ fc) for fx, fy, fc in flips]
        copies = []
        for k, peer in enumerate(peers):
            cp = pltpu.make_async_remote_copy(src_ref=p_ref, dst_ref=o_ref.at[me], send_sem=send_sems.at[k],
                                              recv_sem=recv_sems.at[k], device_id=peer, device_id_type=MESH)
            cp.start()
            copies.append(cp)
        for k, (px, py, pc) in enumerate(peers):
            pltpu.make_async_remote_copy(src_ref=p_ref, dst_ref=o_ref.at[4 * px + 2 * py + pc], send_sem=send_sems.at[k],
                                         recv_sem=recv_sems.at[k], device_id=peers[k], device_id_type=MESH).wait_recv()
        for cp in copies:
            cp.wait_send()
        own.wait()

    dma = pltpu.SemaphoreType.DMA
    return _comm_call(body, "small_allgather", 1, [jax.ShapeDtypeStruct((8,) + packed.shape, F32)],
                      [dma((7,)), dma((7,)), dma(())])(packed)[0]


PAD_ROWS = 8


def kernel(x, norm_a, w_in_a, w_grp_a, scale_a, w_out_a, norm_kv, w_k, w_v, norm_b, w_in_b, w_out_b, norm_f, loss_target, m_norm_a, m_w_in_a, m_w_grp_a, m_scale_a, m_w_out_a, m_norm_kv, m_w_k, m_w_v, m_norm_b, m_w_in_b, m_w_out_b, m_norm_f, v_norm_a, v_w_in_a, v_w_grp_a, v_scale_a, v_w_out_a, v_norm_kv, v_w_k, v_w_v, v_norm_b, v_w_in_b, v_w_out_b, v_norm_f):
    weights = dict(norm_a=norm_a, w_in_a=w_in_a, w_grp_a=w_grp_a, scale_a=scale_a, w_out_a=w_out_a, norm_kv=norm_kv,
                   w_k=w_k, w_v=w_v, norm_b=norm_b, w_in_b=w_in_b, w_out_b=w_out_b, norm_f=norm_f)
    moments_m = dict(norm_a=m_norm_a, w_in_a=m_w_in_a, w_grp_a=m_w_grp_a, scale_a=m_scale_a, w_out_a=m_w_out_a,
                     norm_kv=m_norm_kv, w_k=m_w_k, w_v=m_w_v, norm_b=m_norm_b, w_in_b=m_w_in_b, w_out_b=m_w_out_b,
                     norm_f=m_norm_f)
    moments_v = dict(norm_a=v_norm_a, w_in_a=v_w_in_a, w_grp_a=v_w_grp_a, scale_a=v_scale_a, w_out_a=v_w_out_a,
                     norm_kv=v_norm_kv, w_k=v_w_k, w_v=v_w_v, norm_b=v_norm_b, w_in_b=v_w_in_b, w_out_b=v_w_out_b,
                     norm_f=v_norm_f)
    names = list(weights)
    d = x.shape[-1]
    c_idx = lax.axis_index("c").astype(jnp.int32).reshape(1)
    s_me = 2 * lax.axis_index("x") + lax.axis_index("y")
    s_idx = s_me.astype(jnp.int32).reshape(1)

    def as_lbrc(name):
        a = weights[name]
        if name == "w_grp_a":
            return a
        if a.ndim == 2:
            return a.reshape(1, 1, *a.shape)
        return a.reshape(a.shape[0], 1, *a.shape[1:])

    local_bf = []
    for name in BIG_WEIGHTS:
        a = as_lbrc(name)
        local_bf.append(_cast_bf16(f"cast_{name}", a.reshape(-1, a.shape[-1])).reshape(a.shape))
    gathered, small_g = _gather_weights(local_bf, [norm_a, scale_a])

    w = {}
    for name, per_layer in zip(BIG_WEIGHTS, gathered):
        if name in ("w_in_a", "w_in_b"):
            w[name] = [g[0] for g in per_layer]
        elif name == "w_grp_a":
            w[name] = [g.reshape(g.shape[0], -1, g.shape[-1]) for g in per_layer]
        else:
            w[name] = [g.reshape(-1, g.shape[-1]) for g in per_layer]
    n_a = norm_a.shape[0]
    full_small = lambda g: g.transpose(1, 0, 2).reshape(g.shape[1], -1)
    norm_a_full, scale_a_full = full_small(small_g[0]), full_small(small_g[1])
    w["norm_a"] = [norm_a_full[i] for i in range(n_a)]
    w["scale_a"] = [scale_a_full[i] for i in range(n_a)]
    w["norm_kv"] = [norm_kv]
    w["norm_b"] = [norm_b[i] for i in range(norm_b.shape[0])]
    w["norm_f"] = [norm_f]

    loss_vec, grad_x, big, small = _local_step(x[0], loss_target[0], w)

    small_order = [("norm_a", i) for i in range(n_a)] + [("scale_a", i) for i in range(n_a)] + [("norm_kv", 0)] + \
                  [("norm_b", i) for i in range(norm_b.shape[0])] + [("norm_f", 0)]
    pad = lambda vec: jnp.pad(vec, ((0, PAD_ROWS - 1), (0, 0)))
    packed = jnp.concatenate([pad(loss_vec)] + [pad(small[n][i]) for n, i in small_order], axis=0)
    totals = _sum_devices("small_sum", _allgather_small(packed))
    loss = 0.5 * jnp.sum(totals[0]) / d
    small_tot = {}
    for j, (n, i) in enumerate(small_order):
        small_tot.setdefault(n, []).append(totals[PAD_ROWS * (j + 1)])
    grads = {}
    shard_w = norm_a.shape[1]
    for n in ("norm_a", "scale_a"):
        full = jnp.stack(small_tot[n])
        grads[n] = lax.dynamic_slice_in_dim(full, s_me * shard_w, shard_w, axis=1)
    grads["norm_kv"] = small_tot["norm_kv"][0]
    grads["norm_b"] = jnp.stack(small_tot["norm_b"])
    grads["norm_f"] = small_tot["norm_f"][0]

    flat = [(name, i) for name in BIG_WEIGHTS for i in range(len(big[name]))]
    g_list = [big[name][i] for name, i in flat]
    recv1 = _exchange_halves(g_list)
    parts = [_pair_add(f"pair_add_{name}{i}", g, r, c_idx) for (name, i), g, r in zip(flat, g_list, recv1)]
    recv2 = _scatter_partials(parts)
    halves = {name: [] for name in BIG_WEIGHTS}
    for (name, i), p, r in zip(flat, parts, recv2):
        halves[name].append(_final_add(f"final_add_{name}{i}", p, r, s_idx))
    shared = _share_halves([halves[name] for name in BIG_WEIGHTS])
    for name, g in zip(BIG_WEIGHTS, shared):
        grads[name] = g.reshape(weights[name].shape)

    deltas, new_m, new_v = {}, {}, {}
    for n in names:
        shape = weights[n].shape
        as2d = (lambda a: a.reshape(1, -1)) if len(shape) == 1 else (lambda a: a)
        dl, mn, vn = _adamw(f"adamw_{n}", as2d(weights[n]), as2d(grads[n]), as2d(moments_m[n]), as2d(moments_v[n]))
        deltas[n], new_m[n], new_v[n] = dl.reshape(shape), mn.reshape(shape), vn.reshape(shape)

    return (loss, grad_x[None], *[grads[n] for n in names], *[deltas[n] for n in names],
            *[new_m[n] for n in names], *[new_v[n] for n in names])
```

```python
import functools
import math

import jax
import jax.numpy as jnp
from jax import lax
from jax.experimental import pallas as pl
from jax.experimental.pallas import tpu as pltpu

F32 = jnp.float32
BF16 = jnp.bfloat16

HEAD_DIM = 128
POOL_WINDOWS = (2, 4, 8, 16)
DILATED_PAIRS = ((128, 1), (512, 4), (2048, 16))
ROPE_THETA = 10000.0
RMS_EPS = 1e-6
NEG_INF = -1e30
N_CHIPS = 4

ADAM_LR = 0.001
ADAM_B1 = 0.9
ADAM_B2 = 0.999
ADAM_EPS = 1e-08
ADAM_WD = 0.01
ADAM_STEP = 10

VMEM_LIMIT_BYTES = 56 * 1024 * 1024
MESH = pl.DeviceIdType.MESH
ANY = pl.BlockSpec(memory_space=pl.ANY)


def _tile(n, pref):
    t = min(n, pref)
    assert n % t == 0, (n, pref)
    return t


def _params(sem=None):
    return pltpu.CompilerParams(dimension_semantics=sem, vmem_limit_bytes=VMEM_LIMIT_BYTES)


def _mm(name, a, b, *, grid2, nk, a_blk, a_map, b_blk, b_map, o_shape, o_blk, o_map, dims, out_dtype,
        epi=None, epi_in=(), epi_specs=(), acc_shape=None):
    n_epi = len(epi_in)

    def body(*refs):
        a_ref, b_ref = refs[0], refs[1]
        e_refs = refs[2:2 + n_epi]
        o_ref = refs[2 + n_epi]

        def contrib():
            return lax.dot_general(a_ref[...], b_ref[...], (dims, ((), ())), preferred_element_type=F32)

        def finish(acc):
            if epi is None:
                o_ref[...] = acc.reshape(o_ref.shape).astype(o_ref.dtype)
            else:
                epi(acc, e_refs, o_ref)

        if nk == 1:
            finish(contrib())
        else:
            acc_ref = refs[-1]
            k = pl.program_id(2)

            @pl.when(k == 0)
            def _():
                acc_ref[...] = contrib()

            @pl.when(k > 0)
            def _():
                acc_ref[...] += contrib()

            @pl.when(k == nk - 1)
            def _():
                finish(acc_ref[...])

    scratch = [] if nk == 1 else [pltpu.VMEM(acc_shape, F32)]
    return pl.pallas_call(
        body, name=name, grid=(grid2[0], grid2[1], nk),
        in_specs=[pl.BlockSpec(a_blk, a_map), pl.BlockSpec(b_blk, b_map), *epi_specs],
        out_specs=pl.BlockSpec(o_blk, o_map),
        out_shape=jax.ShapeDtypeStruct(o_shape, out_dtype),
        scratch_shapes=scratch,
        compiler_params=_params(("parallel", "parallel", "arbitrary")),
    )(a, b, *epi_in)


NN = ((1,), (0,))
NT = ((1,), (1,))
TN = ((0,), (0,))


def _rope_apply(t, cos, sin):
    return t * cos + pltpu.roll(t, HEAD_DIM // 2, 1) * sin


def _epi_add(acc, e_refs, o_ref):
    o_ref[...] = (acc + e_refs[0][...]).astype(o_ref.dtype)


def _make_epi_rope(scale):
    def epi(acc, e_refs, o_ref):
        cos = e_refs[0][...]
        sin = e_refs[1][...]
        for h in range(acc.shape[1] // HEAD_DIM):
            sl = slice(h * HEAD_DIM, (h + 1) * HEAD_DIM)
            o_ref[:, sl] = (_rope_apply(acc[:, sl], cos, sin) * scale).astype(o_ref.dtype)
    return epi


def _mm_act_w(name, a, w, *, out_dtype, add=None, rope=None, n_first=0, n_cols=None):
    s_len, k_len = a.shape
    bm = _tile(s_len, 1024)
    epi, epi_in, epi_specs = None, (), ()
    if w.ndim == 3:
        ns, _, c = w.shape
        ns_used = ns if n_cols is None else n_cols
        bn = _tile(c, 1024)
        sub = c // bn
        grid2 = (ns_used * sub, s_len // bm)
        b_blk, b_map = (None, k_len, bn), (lambda j, i, k: (j // sub + n_first, 0, j % sub))
        n_len = ns_used * c
    else:
        n_len = w.shape[1]
        bn = _tile(n_len, 1024)
        grid2 = (n_len // bn, s_len // bm)
        b_blk, b_map = (k_len, bn), (lambda j, i, k: (0, j))
    if add is not None:
        epi, epi_in = _epi_add, (add,)
        epi_specs = (pl.BlockSpec((bm, bn), lambda j, i, k: (i, j)),)
    if rope is not None:
        cos, sin, scale = rope
        epi, epi_in = _make_epi_rope(scale), (cos, sin)
        epi_specs = (pl.BlockSpec((bm, HEAD_DIM), lambda j, i, k: (i, 0)),) * 2
    return _mm(name, a, w, grid2=grid2, nk=1, a_blk=(bm, k_len), a_map=lambda j, i, k: (i, 0),
               b_blk=b_blk, b_map=b_map, o_shape=(s_len, n_len), o_blk=(bm, bn), o_map=lambda j, i, k: (i, j),
               dims=NN, out_dtype=out_dtype, epi=epi, epi_in=epi_in, epi_specs=epi_specs)


def _mm_grad_act(name, dy, w, *, add=None):
    s_len, n_len = dy.shape
    bm = _tile(s_len, 1024)
    if w.ndim == 3:
        ns, k_len, c = w.shape
        bk, nk = c, ns
        bn = _tile(k_len, 1024)
        b_blk, b_map = (None, bn, c), (lambda j, i, k: (k, j, 0))
    else:
        k_len = w.shape[0]
        bk = _tile(n_len, 1024)
        nk = n_len // bk
        bn = _tile(k_len, 1024)
        b_blk, b_map = (bn, bk), (lambda j, i, k: (j, k))
    epi, epi_in, epi_specs = None, (), ()
    if add is not None:
        epi, epi_in = _epi_add, (add,)
        epi_specs = (pl.BlockSpec((bm, bn), lambda j, i, k: (i, j)),)
    return _mm(name, dy, w, grid2=(k_len // bn, s_len // bm), nk=nk, a_blk=(bm, bk), a_map=lambda j, i, k: (i, k),
               b_blk=b_blk, b_map=b_map, o_shape=(s_len, k_len), o_blk=(bm, bn), o_map=lambda j, i, k: (i, j),
               dims=NT, out_dtype=F32, epi=epi, epi_in=epi_in, epi_specs=epi_specs, acc_shape=(bm, bn))


def _mm_grad_w(name, a, dy, *, col_shards=None):
    s_len, k_len = a.shape
    n_len = dy.shape[1]
    bk = _tile(s_len, 1024)
    bm = _tile(k_len, 1024)
    if col_shards:
        c = n_len // col_shards
        bn = _tile(c, 1024)
        sub = c // bn
        o_shape, o_blk, o_map = (col_shards, k_len, c), (None, bm, bn), (lambda j, i, k: (j // sub, i, j % sub))
    else:
        bn = _tile(n_len, 1024)
        o_shape, o_blk, o_map = (k_len, n_len), (bm, bn), (lambda j, i, k: (i, j))
    return _mm(name, a, dy, grid2=(n_len // bn, k_len // bm), nk=s_len // bk,
               a_blk=(bk, bm), a_map=lambda j, i, k: (k, i), b_blk=(bk, bn), b_map=lambda j, i, k: (k, j),
               o_shape=o_shape, o_blk=o_blk, o_map=o_map, dims=TN, out_dtype=BF16, acc_shape=(bm, bn))


def _mm_grp_fwd(name, pooled, wg):
    s_len, e = pooled.shape
    ng, g, _ = wg.shape
    bm = _tile(s_len, 1024)
    return _mm(name, pooled, wg, grid2=(ng, s_len // bm), nk=1, a_blk=(bm, g), a_map=lambda j, i, k: (i, j),
               b_blk=(None, g, g), b_map=lambda j, i, k: (j, 0, 0), o_shape=(s_len, e), o_blk=(bm, g),
               o_map=lambda j, i, k: (i, j), dims=NN, out_dtype=F32)


def _mm_grp_grad_act(name, dy, wg):
    s_len, e = dy.shape
    ng, g, _ = wg.shape
    bm = _tile(s_len, 1024)
    return _mm(name, dy, wg, grid2=(ng, s_len // bm), nk=1, a_blk=(bm, g), a_map=lambda j, i, k: (i, j),
               b_blk=(None, g, g), b_map=lambda j, i, k: (j, 0, 0), o_shape=(s_len, e), o_blk=(bm, g),
               o_map=lambda j, i, k: (i, j), dims=NT, out_dtype=F32)


def _mm_grp_grad_w(name, pooled, dy, ng):
    s_len, e = pooled.shape
    g = e // ng
    bk = _tile(s_len, 1024)
    return _mm(name, pooled, dy, grid2=(ng, 1), nk=s_len // bk, a_blk=(bk, g), a_map=lambda j, i, k: (k, j),
               b_blk=(bk, g), b_map=lambda j, i, k: (k, j), o_shape=(N_CHIPS, ng, g // N_CHIPS, g),
               o_blk=(N_CHIPS, None, g // N_CHIPS, g), o_map=lambda j, i, k: (0, j, 0, 0),
               dims=TN, out_dtype=BF16, acc_shape=(g, g))


def _row_spec(bs, width, col=0):
    return pl.BlockSpec((bs, width), lambda i: (i, col))


def _vec_spec(width):
    return pl.BlockSpec((1, width), lambda i: (0, 0))


def _rows_call(body, name, s_len, in_specs, out_specs, out_shape, bs, aliases=None, sequential=False):
    return pl.pallas_call(
        body, name=name, grid=(s_len // bs,), in_specs=in_specs, out_specs=out_specs, out_shape=out_shape,
        input_output_aliases=aliases or {},
        compiler_params=_params(("arbitrary",) if sequential else ("parallel",)))


def _accumulate(ref, part):
    i = pl.program_id(0)

    @pl.when(i == 0)
    def _():
        ref[...] = part

    @pl.when(i > 0)
    def _():
        ref[...] += part


def _rms_scale(xf):
    return lax.rsqrt(jnp.mean(xf * xf, axis=-1, keepdims=True) + RMS_EPS)


def _rmsnorm_fwd(name, x, gain):
    s_len, d = x.shape
    bs = _tile(s_len, 512)

    def body(x_ref, g_ref, h_ref):
        xf = x_ref[...]
        h_ref[...] = ((xf * _rms_scale(xf)) * g_ref[...]).astype(BF16)

    return _rows_call(body, name, s_len, [_row_spec(bs, d), _vec_spec(d)], _row_spec(bs, d),
                      jax.ShapeDtypeStruct((s_len, d), BF16), bs)(x, gain)


def _rmsnorm_bwd(name, x, gain, dh, dres):
    s_len, d = x.shape
    bs = _tile(s_len, 256)

    def body(x_ref, g_ref, dh_ref, dres_ref, dx_ref, dxb_ref, dg_ref):
        xf = x_ref[...]
        r = _rms_scale(xf)
        xh = xf * r
        dh_f = dh_ref[...]
        t = dh_f * g_ref[...]
        dx = dres_ref[...] + r * (t - xh * jnp.mean(t * xh, axis=-1, keepdims=True))
        dx_ref[...] = dx
        dxb_ref[...] = dx.astype(BF16)
        _accumulate(dg_ref, jnp.sum(dh_f * xh, axis=0, keepdims=True))

    return _rows_call(
        body, name, s_len,
        [_row_spec(bs, d), _vec_spec(d), _row_spec(bs, d), _row_spec(bs, d)],
        [_row_spec(bs, d), _row_spec(bs, d), _vec_spec(d)],
        [jax.ShapeDtypeStruct((s_len, d), F32), jax.ShapeDtypeStruct((s_len, d), BF16),
         jax.ShapeDtypeStruct((1, d), F32)], bs, sequential=True)(x, gain, dh, dres)


def _loss_head(name, x, gain, target):
    s_len, d = x.shape
    bs = _tile(s_len, 256)

    def body(x_ref, g_ref, t_ref, lv_ref, dx_ref, dxb_ref, dg_ref):
        xf = x_ref[...]
        r = _rms_scale(xf)
        xh = xf * r
        err = xh * g_ref[...] - t_ref[...]
        dy = err * (1.0 / d)
        t = dy * g_ref[...]
        dx = r * (t - xh * jnp.mean(t * xh, axis=-1, keepdims=True))
        dx_ref[...] = dx
        dxb_ref[...] = dx.astype(BF16)
        _accumulate(lv_ref, jnp.sum(err * err, axis=0, keepdims=True))
        _accumulate(dg_ref, jnp.sum(dy * xh, axis=0, keepdims=True))

    return _rows_call(
        body, name, s_len, [_row_spec(bs, d), _vec_spec(d), _row_spec(bs, d)],
        [_vec_spec(d), _row_spec(bs, d), _row_spec(bs, d), _vec_spec(d)],
        [jax.ShapeDtypeStruct((1, d), F32), jax.ShapeDtypeStruct((s_len, d), F32),
         jax.ShapeDtypeStruct((s_len, d), BF16), jax.ShapeDtypeStruct((1, d), F32)],
        bs, sequential=True)(x, gain, target)


def _sigmoid(g):
    return 1.0 / (1.0 + jnp.exp(-g))


def _gate_a_fwd(name, ypre, proj, scale):
    s_len, e = ypre.shape
    bs = _tile(s_len, 256)

    def body(y_ref, g_ref, sc_ref, z_ref):
        g = g_ref[...]
        z_ref[...] = (y_ref[...] * sc_ref[...] * (g * _sigmoid(g))).astype(BF16)

    return _rows_call(body, name, s_len, [_row_spec(bs, e), _row_spec(bs, e, 1), _vec_spec(e)], _row_spec(bs, e),
                      jax.ShapeDtypeStruct((s_len, e), BF16), bs)(ypre, proj, scale)


def _gate_a_bwd(name, dz, ypre, proj, scale):
    s_len, e = ypre.shape
    bs = _tile(s_len, 256)

    def body(dz_ref, y_ref, g_ref, sc_ref, dy_ref, dproj_ref, dsc_ref):
        g = g_ref[...]
        sg = _sigmoid(g)
        silu = g * sg
        dz_f = dz_ref[...]
        ypre_f = y_ref[...]
        dys = dz_f * silu
        dy_ref[...] = (dys * sc_ref[...]).astype(BF16)
        dproj_ref[...] = (dz_f * (ypre_f * sc_ref[...]) * (sg * (1.0 + g * (1.0 - sg)))).astype(BF16)
        _accumulate(dsc_ref, jnp.sum(dys * ypre_f, axis=0, keepdims=True))

    return _rows_call(
        body, name, s_len, [_row_spec(bs, e), _row_spec(bs, e), _row_spec(bs, e, 1), _vec_spec(e)],
        [_row_spec(bs, e), _row_spec(bs, e, 1), _vec_spec(e)],
        [jax.ShapeDtypeStruct((s_len, e), BF16), jax.ShapeDtypeStruct((s_len, 2 * e), BF16),
         jax.ShapeDtypeStruct((1, e), F32)], bs, sequential=True)(dz, ypre, proj, scale)


def _merge_gate_fwd(name, outs, lses, gate):
    s_len, e = gate.shape
    bs = _tile(s_len, 256)
    n = len(outs)

    def body(*refs):
        o_refs, l_refs, g_ref = refs[:n], refs[n:2 * n], refs[2 * n]
        m_ref, lj_ref, z_ref = refs[2 * n + 1:]
        ls = [r[...] for r in l_refs]
        mx = functools.reduce(jnp.maximum, ls)
        ws = [jnp.exp(l - mx) for l in ls]
        den = functools.reduce(lambda a, b: a + b, ws)
        merged = functools.reduce(lambda a, b: a + b, [w * o[...].astype(F32) for w, o in zip(ws, o_refs)]) / den
        g = g_ref[...]
        m_ref[...] = merged.astype(BF16)
        lj_ref[...] = mx + jnp.log(den)
        z_ref[...] = (merged * (g * _sigmoid(g))).astype(BF16)

    spec = _row_spec(bs, e)
    return _rows_call(
        body, name, s_len, [spec] * (2 * n + 1), [spec] * 3,
        [jax.ShapeDtypeStruct((s_len, e), BF16), jax.ShapeDtypeStruct((s_len, e), F32),
         jax.ShapeDtypeStruct((s_len, e), BF16)], bs)(*outs, *lses, gate)


def _gate_b_bwd(name, dz, merged, gate, n_q):
    s_len, e = gate.shape
    bs = _tile(s_len, 256)

    def body(dz_ref, m_ref, g_ref, dm_ref, dproj_ref):
        g = g_ref[...]
        sg = _sigmoid(g)
        dz_f = dz_ref[...]
        dm_ref[...] = (dz_f * (g * sg)).astype(BF16)
        dproj_ref[...] = (dz_f * m_ref[...].astype(F32) * (sg * (1.0 + g * (1.0 - sg)))).astype(BF16)

    spec = _row_spec(bs, e)
    return _rows_call(
        body, name, s_len, [spec] * 3, [spec, _row_spec(bs, e, n_q)],
        [jax.ShapeDtypeStruct((s_len, e), BF16), jax.ShapeDtypeStruct((s_len, (n_q + 1) * e), BF16)],
        bs)(dz, merged, gate)


def _kv_grad_prep(name, dk_acc, dv_acc, cos, sin_inv):
    s_len, e = dk_acc.shape
    bs = _tile(s_len, 256)

    def body(dk_ref, dv_ref, c_ref, s_ref, dkb_ref, dvb_ref):
        cos_t, sin_t = c_ref[...], s_ref[...]
        for h in range(e // HEAD_DIM):
            sl = slice(h * HEAD_DIM, (h + 1) * HEAD_DIM)
            dkb_ref[:, sl] = _rope_apply(dk_ref[:, sl], cos_t, sin_t).astype(BF16)
        dvb_ref[...] = dv_ref[...].astype(BF16)

    spec, rspec = _row_spec(bs, e), _row_spec(bs, HEAD_DIM)
    return _rows_call(body, name, s_len, [spec, spec, rspec, rspec], [spec, spec],
                      [jax.ShapeDtypeStruct((s_len, e), BF16)] * 2, bs)(dk_acc, dv_acc, cos, sin_inv)


def _pool_cols(e):
    return _tile(e // len(POOL_WINDOWS), 256)


def _window_sum(val, grp, s_len, forward):
    rows = lax.broadcasted_iota(jnp.int32, val.shape, 0)
    acc = val
    for level in range(len(POOL_WINDOWS)):
        step = 1 << level
        if forward:
            shifted = jnp.where(rows >= step, pltpu.roll(acc, step, 0), 0.0)
        else:
            shifted = jnp.where(rows < s_len - step, pltpu.roll(acc, s_len - step, 0), 0.0)
        acc = jnp.where(level <= grp, acc + shifted, acc)
    return acc


def _window_count(shape, grp):
    rows = lax.broadcasted_iota(jnp.int32, shape, 0)
    return jnp.minimum(rows + 1, jnp.left_shift(2, grp)).astype(F32)


def _pool_fwd(name, proj):
    s_len, e2 = proj.shape
    e = e2 // 2
    cb = _pool_cols(e)
    per_grp = e // len(POOL_WINDOWS) // cb
    assert POOL_WINDOWS == tuple(2 << g for g in range(len(POOL_WINDOWS)))

    def body(u_ref, p_ref):
        grp = pl.program_id(0)
        u = u_ref[...]
        total = _window_sum(u, grp, s_len, True)
        p_ref[...] = (total / _window_count(u.shape, grp) - u).astype(BF16)

    spec = pl.BlockSpec((s_len, cb), lambda g, c: (0, g * per_grp + c))
    return pl.pallas_call(
        body, name=name, grid=(len(POOL_WINDOWS), per_grp), in_specs=[spec], out_specs=spec,
        out_shape=jax.ShapeDtypeStruct((s_len, e), BF16), compiler_params=_params(("parallel", "parallel")))(proj)


def _pool_bwd(name, dpooled, dproj):
    s_len, e = dpooled.shape
    cb = _pool_cols(e)
    per_grp = e // len(POOL_WINDOWS) // cb

    def body(dp_ref, _, du_ref):
        grp = pl.program_id(0)
        dp = dp_ref[...]
        total = _window_sum(dp / _window_count(dp.shape, grp), grp, s_len, False)
        du_ref[...] = (total - dp).astype(BF16)

    spec = pl.BlockSpec((s_len, cb), lambda g, c: (0, g * per_grp + c))
    return pl.pallas_call(
        body, name=name, grid=(len(POOL_WINDOWS), per_grp), in_specs=[spec, ANY], out_specs=spec,
        out_shape=jax.ShapeDtypeStruct(dproj.shape, BF16), input_output_aliases={1: 0},
        compiler_params=_params(("parallel", "parallel")))(dpooled, dproj)


def _band_masks(nb, first):
    row = lax.broadcasted_iota(jnp.int32, (nb, nb), 0)
    col = lax.broadcasted_iota(jnp.int32, (nb, nb), 1)
    return col >= row + jnp.where(first, 2 * nb, 0), col <= row


def _dot(a, b, dims):
    return lax.dot_general(a, b, (dims, ((), ())), preferred_element_type=F32)


def _attn_fwd(name, grp, n_q, window, dil, q_all, k, v):
    s_len, e = k.shape
    nb = window // dil
    m = s_len // dil
    nblk = m // nb
    heads = e // HEAD_DIM

    def body(q_ref, kp_ref, kc_ref, vp_ref, vc_ref, o_ref, l_ref):
        mask_p, mask_c = _band_masks(nb, pl.program_id(1) == 0)
        for h in range(heads):
            sl = slice(h * HEAD_DIM, (h + 1) * HEAD_DIM)
            q = q_ref[:, sl]
            s_p = jnp.where(mask_p, _dot(q, kp_ref[:, sl], NT), NEG_INF)
            s_c = jnp.where(mask_c, _dot(q, kc_ref[:, sl], NT), NEG_INF)
            mx = jnp.maximum(jnp.max(s_p, axis=-1, keepdims=True), jnp.max(s_c, axis=-1, keepdims=True))
            p_p = jnp.exp(s_p - mx)
            p_c = jnp.exp(s_c - mx)
            den = jnp.sum(p_p, axis=-1, keepdims=True) + jnp.sum(p_c, axis=-1, keepdims=True)
            out = _dot(p_p.astype(BF16), vp_ref[:, sl], NN) + _dot(p_c.astype(BF16), vc_ref[:, sl], NN)
            o_ref[:, sl] = (out / den).astype(BF16)
            l_ref[:, sl] = jnp.broadcast_to(mx + jnp.log(den), (nb, HEAD_DIM))

    blk = (nb, e)
    prev = lambda r, n: (jnp.maximum(n - 1, 0), r)
    cur = lambda r, n: (n, r)
    return pl.pallas_call(
        body, name=name, grid=(dil, nblk),
        in_specs=[pl.BlockSpec(blk, lambda r, n: (n, r * n_q + grp)),
                  pl.BlockSpec(blk, prev), pl.BlockSpec(blk, cur), pl.BlockSpec(blk, prev), pl.BlockSpec(blk, cur)],
        out_specs=[pl.BlockSpec(blk, cur), pl.BlockSpec(blk, cur)],
        out_shape=[jax.ShapeDtypeStruct((m, dil * e), BF16), jax.ShapeDtypeStruct((m, dil * e), F32)],
        compiler_params=_params(("parallel", "arbitrary")),
    )(q_all.reshape(m, dil * n_q * e), *([k.reshape(m, dil * e)] * 2), *([v.reshape(m, dil * e)] * 2))


def _attn_bwd(name, grp, n_q, window, dil, scale, q_all, k, v, dmerged, merged, lse, cos, sin_inv, dproj, dk_acc, dv_acc):
    s_len, e = k.shape
    nb = window // dil
    m = s_len // dil
    nblk = m // nb
    heads = e // HEAD_DIM

    def body(q_ref, kp_ref, kc_ref, vp_ref, vc_ref, do_ref, o_ref, l_ref, c_ref, s_ref, _, dki_ref, dvi_ref,
             dq_ref, dko_ref, dvo_ref, ck_ref, cv_ref):
        n = pl.program_id(1)

        @pl.when(n == 0)
        def _():
            ck_ref[...] = jnp.zeros_like(ck_ref)
            cv_ref[...] = jnp.zeros_like(cv_ref)

        @pl.when(n < nblk)
        def _():
            mask_p, mask_c = _band_masks(nb, n == 0)
            cos_t, sin_t = c_ref[...], s_ref[...]
            for h in range(heads):
                sl = slice(h * HEAD_DIM, (h + 1) * HEAD_DIM)
                q, kp, kc, vp, vc = q_ref[:, sl], kp_ref[:, sl], kc_ref[:, sl], vp_ref[:, sl], vc_ref[:, sl]
                do = do_ref[:, sl]
                lj = l_ref[:, sl] if nb == HEAD_DIM else l_ref[:, sl][:, :1]
                delta = jnp.sum(do.astype(F32) * o_ref[:, sl].astype(F32), axis=-1, keepdims=True)
                p_p = jnp.where(mask_p, jnp.exp(_dot(q, kp, NT) - lj), 0.0)
                p_c = jnp.where(mask_c, jnp.exp(_dot(q, kc, NT) - lj), 0.0)
                ds_p = (p_p * (_dot(do, vp, NT) - delta)).astype(BF16)
                ds_c = (p_c * (_dot(do, vc, NT) - delta)).astype(BF16)
                dq = (_dot(ds_p, kp, NN) + _dot(ds_c, kc, NN)) * scale
                dq_ref[:, sl] = _rope_apply(dq, cos_t, sin_t).astype(BF16)
                dko_ref[:, sl] = dki_ref[:, sl] + ck_ref[:, sl] + _dot(ds_p, q, TN)
                dvo_ref[:, sl] = dvi_ref[:, sl] + cv_ref[:, sl] + _dot(p_p.astype(BF16), do, TN)
                ck_ref[:, sl] = _dot(ds_c, q, TN)
                cv_ref[:, sl] = _dot(p_c.astype(BF16), do, TN)

        @pl.when(n == nblk)
        def _():
            dko_ref[...] = dki_ref[...] + ck_ref[...]
            dvo_ref[...] = dvi_ref[...] + cv_ref[...]

    blk = (nb, e)
    qn = lambda n: jnp.minimum(n, nblk - 1)
    cur = lambda r, n: (qn(n), r)
    prev = lambda r, n: (jnp.maximum(qn(n) - 1, 0), r)
    kprev = lambda r, n: (jnp.maximum(n - 1, 0), r)
    rblk = (nb, HEAD_DIM)
    view = lambda a: a.reshape(m, -1)
    return pl.pallas_call(
        body, name=name, grid=(dil, nblk + 1),
        in_specs=[pl.BlockSpec(blk, lambda r, n: (qn(n), r * n_q + grp)),
                  pl.BlockSpec(blk, prev), pl.BlockSpec(blk, cur), pl.BlockSpec(blk, prev), pl.BlockSpec(blk, cur),
                  pl.BlockSpec(blk, cur), pl.BlockSpec(blk, cur), pl.BlockSpec(blk, cur),
                  pl.BlockSpec(rblk, cur), pl.BlockSpec(rblk, cur), ANY,
                  pl.BlockSpec(blk, kprev), pl.BlockSpec(blk, kprev)],
        out_specs=[pl.BlockSpec(blk, lambda r, n: (qn(n), r * (n_q + 1) + grp)),
                   pl.BlockSpec(blk, kprev), pl.BlockSpec(blk, kprev)],
        out_shape=[jax.ShapeDtypeStruct((m, dil * (n_q + 1) * e), BF16),
                   jax.ShapeDtypeStruct((m, dil * e), F32), jax.ShapeDtypeStruct((m, dil * e), F32)],
        scratch_shapes=[pltpu.VMEM(blk, F32), pltpu.VMEM(blk, F32)],
        input_output_aliases={10: 0, 11: 1, 12: 2},
        compiler_params=_params(("parallel", "arbitrary")),
    )(view(q_all), view(k), view(k), view(v), view(v), view(dmerged), view(merged), view(lse), view(cos), view(sin_inv),
      view(dproj), view(dk_acc), view(dv_acc))


def _rope_tables(s_len):
    inv_freq = 1.0 / (ROPE_THETA ** (jnp.arange(0, HEAD_DIM, 2, dtype=F32) / HEAD_DIM))
    ang = jnp.arange(s_len, dtype=F32)[:, None] * inv_freq[None, :]
    cos, sin = jnp.cos(ang), jnp.sin(ang)
    return jnp.concatenate([cos, cos], axis=1), jnp.concatenate([-sin, sin], axis=1)


def _row(vec):
    return vec.reshape(1, -1)


def _local_step(x, target, w):
    s_len, d = x.shape
    n_a, n_b = len(w["w_in_a"]), len(w["w_in_b"])
    n_q = len(DILATED_PAIRS)
    e = w["w_k"][0].shape[1]
    cos, sin = _rope_tables(s_len)
    sin_inv = -sin
    q_scale = 1.0 / math.sqrt(HEAD_DIM)

    saved_a = []
    for i in range(n_a):
        h = _rmsnorm_fwd(f"a{i}_norm", x, _row(w["norm_a"][i]))
        proj = _mm_act_w(f"a{i}_in", h, w["w_in_a"][i], out_dtype=F32)
        pooled = _pool_fwd(f"a{i}_pool", proj)
        ypre = _mm_grp_fwd(f"a{i}_grp", pooled, w["w_grp_a"][i])
        z = _gate_a_fwd(f"a{i}_gate", ypre, proj, _row(w["scale_a"][i]))
        x_next = _mm_act_w(f"a{i}_out", z, w["w_out_a"][i], out_dtype=F32, add=x)
        saved_a.append((x, h, proj, pooled, ypre, z))
        x = x_next

    x_kv = x
    kv_in = _rmsnorm_fwd("kv_norm", x, _row(w["norm_kv"][0]))
    k = _mm_act_w("kv_k", kv_in, w["w_k"][0], out_dtype=BF16, rope=(cos, sin, 1.0))
    v = _mm_act_w("kv_v", kv_in, w["w_v"][0], out_dtype=BF16)

    saved_b = []
    for i in range(n_b):
        h = _rmsnorm_fwd(f"b{i}_norm", x, _row(w["norm_b"][i]))
        q_all = _mm_act_w(f"b{i}_q", h, w["w_in_b"][i], out_dtype=BF16, rope=(cos, sin, q_scale), n_cols=n_q)
        gate = _mm_act_w(f"b{i}_g", h, w["w_in_b"][i], out_dtype=F32, n_first=n_q, n_cols=1)
        outs, lses = [], []
        for g, (window, dil) in enumerate(DILATED_PAIRS):
            o_g, l_g = _attn_fwd(f"b{i}_attn{g}", g, n_q, window, dil, q_all, k, v)
            outs.append(o_g.reshape(s_len, e))
            lses.append(l_g.reshape(s_len, e))
        merged, lse, z = _merge_gate_fwd(f"b{i}_merge", outs, lses, gate)
        x_next = _mm_act_w(f"b{i}_out", z, w["w_out_b"][i], out_dtype=F32, add=x)
        saved_b.append((x, h, q_all, gate, merged, lse, z))
        x = x_next

    loss_vec, dx, dxb, g_norm_f = _loss_head("loss_head", x, _row(w["norm_f"][0]), target)

    big = {name: [None] * len(w[name]) for name in BIG_WEIGHTS}
    small = {"norm_a": [None] * n_a, "scale_a": [None] * n_a, "norm_kv": [None], "norm_b": [None] * n_b,
             "norm_f": [g_norm_f]}
    shard_rows = lambda g2: g2.reshape(N_CHIPS, g2.shape[0] // N_CHIPS, g2.shape[1])

    dk_acc = jnp.zeros((s_len, e), F32)
    dv_acc = jnp.zeros((s_len, e), F32)
    for i in reversed(range(n_b)):
        x_in, h, q_all, gate, merged, lse, z = saved_b[i]
        dz = _mm_grad_act(f"b{i}_dz", dxb, w["w_out_b"][i])
        big["w_out_b"][i] = shard_rows(_mm_grad_w(f"b{i}_gwo", z, dxb))
        dmerged, dproj = _gate_b_bwd(f"b{i}_dgate", dz, merged, gate, n_q)
        for g, (window, dil) in enumerate(DILATED_PAIRS):
            dproj, dk_acc, dv_acc = _attn_bwd(f"b{i}_dattn{g}", g, n_q, window, dil, q_scale, q_all, k, v,
                                              dmerged, merged, lse, cos, sin_inv, dproj, dk_acc, dv_acc)
            dproj = dproj.reshape(s_len, (n_q + 1) * e)
        dh = _mm_grad_act(f"b{i}_dh", dproj, w["w_in_b"][i])
        big["w_in_b"][i] = _mm_grad_w(f"b{i}_gwi", h, dproj, col_shards=N_CHIPS)
        dx, dxb, small["norm_b"][i] = _rmsnorm_bwd(f"b{i}_dnorm", x_in, _row(w["norm_b"][i]), dh, dx)

    dkb, dvb = _kv_grad_prep("kv_dprep", dk_acc.reshape(s_len, e), dv_acc.reshape(s_len, e), cos, sin_inv)
    dkv = _mm_grad_act("kv_dk", dkb, w["w_k"][0])
    dkv = _mm_grad_act("kv_dv", dvb, w["w_v"][0], add=dkv)
    big["w_k"][0] = shard_rows(_mm_grad_w("kv_gwk", kv_in, dkb))
    big["w_v"][0] = shard_rows(_mm_grad_w("kv_gwv", kv_in, dvb))
    dx, dxb, small["norm_kv"][0] = _rmsnorm_bwd("kv_dnorm", x_kv, _row(w["norm_kv"][0]), dkv, dx)

    for i in reversed(range(n_a)):
        x_in, h, proj, pooled, ypre, z = saved_a[i]
        dz = _mm_grad_act(f"a{i}_dz", dxb, w["w_out_a"][i])
        big["w_out_a"][i] = shard_rows(_mm_grad_w(f"a{i}_gwo", z, dxb))
        dypre, dproj, small["scale_a"][i] = _gate_a_bwd(f"a{i}_dgate", dz, ypre, proj, _row(w["scale_a"][i]))
        dpooled = _mm_grp_grad_act(f"a{i}_dgrp", dypre, w["w_grp_a"][i])
        g_grp = _mm_grp_grad_w(f"a{i}_gwg", pooled, dypre, len(POOL_WINDOWS))
        big["w_grp_a"][i] = g_grp.reshape(N_CHIPS, -1, g_grp.shape[-1])
        dproj = _pool_bwd(f"a{i}_dpool", dpooled, dproj)
        dh = _mm_grad_act(f"a{i}_dh", dproj, w["w_in_a"][i])
        big["w_in_a"][i] = _mm_grad_w(f"a{i}_gwi", h, dproj, col_shards=N_CHIPS)
        dx, dxb, small["norm_a"][i] = _rmsnorm_bwd(f"a{i}_dnorm", x_in, _row(w["norm_a"][i]), dh, dx)

    return loss_vec, dx, big, small


BIG_WEIGHTS = ("w_in_a", "w_grp_a", "w_out_a", "w_k", "w_v", "w_in_b", "w_out_b")


def _pair_add(name, grad, recv, c_idx):
    _, r, cols = grad.shape
    half = r // 2
    rb = _tile(half, 256)
    nrb = half // rb

    def body(c_ref, g_ref, r_ref, o_ref):
        o_ref[...] = (g_ref[...].astype(F32) + r_ref[...].astype(F32)).astype(BF16)

    blk = (None, rb, cols)
    grid_spec = pltpu.PrefetchScalarGridSpec(
        num_scalar_prefetch=1, grid=(N_CHIPS, nrb),
        in_specs=[pl.BlockSpec(blk, lambda s, i, c: (s, c[0] * nrb + i, 0)), pl.BlockSpec(blk, lambda s, i, c: (s, i, 0))],
        out_specs=pl.BlockSpec(blk, lambda s, i, c: (s, i, 0)))
    return pl.pallas_call(body, name=name, grid_spec=grid_spec,
                          out_shape=jax.ShapeDtypeStruct((N_CHIPS, half, cols), BF16),
                          compiler_params=_params(("parallel", "parallel")))(c_idx, grad, recv)


def _final_add(name, part, recv, sc_idx, layer, n_layers, into=None):
    _, half, cols = part.shape
    rb = _tile(half, 256)
    nrb = half // rb
    n_peer = recv.shape[0]

    def body(sc_ref, p_ref, *refs):
        acc = p_ref[...].astype(F32)
        for r_ref in refs[:n_peer]:
            acc = acc + r_ref[...].astype(F32)
        refs[-1][...] = acc

    blk = (None, rb, cols)
    peer_spec = lambda k: pl.BlockSpec(blk, lambda i, sc: (k, i, 0))
    grid_spec = pltpu.PrefetchScalarGridSpec(
        num_scalar_prefetch=1, grid=(nrb,),
        in_specs=[pl.BlockSpec(blk, lambda i, sc: (sc[0], i, 0))] + [peer_spec(k) for k in range(n_peer)]
                 + ([] if into is None else [ANY]),
        out_specs=pl.BlockSpec(blk, lambda i, sc: (layer, sc[1] * nrb + i, 0)))
    extra = () if into is None else (into,)
    return pl.pallas_call(body, name=name, grid_spec=grid_spec,
                          out_shape=jax.ShapeDtypeStruct((n_layers, 2 * half, cols), F32),
                          input_output_aliases={} if into is None else {2 + n_peer: 0},
                          compiler_params=_params(("parallel",)))(sc_idx, part, *([recv] * n_peer), *extra)


def _cast_into_slot(name, arr, layer, s_idx):
    _, b, r, cols = arr.shape
    rb = _tile(r, 512)

    def body(s_ref, a_ref, o_ref):
        o_ref[...] = a_ref[...].astype(BF16)

    blk = (None, None, rb, cols)
    grid_spec = pltpu.PrefetchScalarGridSpec(
        num_scalar_prefetch=1, grid=(b, r // rb),
        in_specs=[pl.BlockSpec(blk, lambda j, i, s: (layer, j, i, 0))],
        out_specs=pl.BlockSpec(blk, lambda j, i, s: (j, s[0], i, 0)))
    return pl.pallas_call(body, name=name, grid_spec=grid_spec,
                          out_shape=jax.ShapeDtypeStruct((b, N_CHIPS, r, cols), BF16),
                          compiler_params=_params(("parallel", "parallel")))(s_idx, arr)


def _sum_devices(name, gathered):
    n_dev, p, d = gathered.shape

    def body(g_ref, o_ref):
        acc = g_ref[0]
        for j in range(1, n_dev):
            acc = acc + g_ref[j]
        o_ref[...] = acc

    return pl.pallas_call(body, name=name, out_shape=jax.ShapeDtypeStruct((p, d), F32),
                          compiler_params=_params())(gathered)


def _adamw(name, w, g, m, v):
    shape = w.shape
    cols = shape[-1]
    flat = lambda a: a.reshape(-1, cols)
    rows = flat(w).shape[0]
    bs = _tile(rows, 256)

    def body(w_ref, g_ref, m_ref, v_ref, d_ref, mo_ref, vo_ref):
        grad = g_ref[...]
        m_new = ADAM_B1 * m_ref[...] + (1.0 - ADAM_B1) * grad
        v_new = ADAM_B2 * v_ref[...] + (1.0 - ADAM_B2) * (grad * grad)
        m_hat = m_new / (1.0 - ADAM_B1 ** ADAM_STEP)
        v_hat = v_new / (1.0 - ADAM_B2 ** ADAM_STEP)
        d_ref[...] = -ADAM_LR * (m_hat / (jnp.sqrt(v_hat) + ADAM_EPS) + ADAM_WD * w_ref[...])
        mo_ref[...] = m_new
        vo_ref[...] = v_new

    spec = _row_spec(bs, cols)
    outs = _rows_call(body, name, rows, [spec] * 4, [spec] * 3, [jax.ShapeDtypeStruct((rows, cols), F32)] * 3, bs)(
        flat(w), flat(g), flat(m), flat(v))
    return tuple(o.reshape(shape) for o in outs)


def _place():
    x, y, c = lax.axis_index("x"), lax.axis_index("y"), lax.axis_index("c")
    chips = [(1 - x, y), (x, 1 - y), (1 - x, 1 - y)]
    return x, y, c, chips


def _chip_index(chip):
    return 2 * chip[0] + chip[1]


def _comm_call(body, name, n_in, out_shape, scratch, aliases=None):
    return pl.pallas_call(body, name=name, in_specs=[ANY] * n_in, out_specs=[ANY] * len(out_shape), out_shape=out_shape,
                          scratch_shapes=scratch, input_output_aliases=aliases or {})


def _gather_weights(slots, smalls):
    n_items, n_small = len(slots), len(smalls)

    def body(*refs):
        small_in = refs[n_items:n_items + n_small]
        outs = refs[n_items + n_small:2 * n_items + n_small]
        small_out = refs[2 * n_items + n_small:2 * n_items + 2 * n_small]
        send_sems, recv_sems, s_send, s_recv, s_local = refs[-5:]
        x, y, c, chips = _place()
        me, sibling = _chip_index((x, y)), (x, y, 1 - c)

        def landing(t, chip, core):
            half = outs[t].shape[2] // 2
            return outs[t].at[:, _chip_index(chip), pl.ds(core * half, half), :]

        def copy(t, k, block, to):
            return pltpu.make_async_remote_copy(src_ref=block, dst_ref=block, send_sem=send_sems.at[t, k],
                                                recv_sem=recv_sems.at[t, k], device_id=to, device_id_type=MESH)

        def small_copy(j, k, chip, slot):
            return pltpu.make_async_remote_copy(src_ref=small_in[j], dst_ref=small_out[j].at[slot],
                                                send_sem=s_send.at[j, k], recv_sem=s_recv.at[j, k],
                                                device_id=(*chip, c), device_id_type=MESH)

        started, local = [], []
        for t in range(n_items):
            for k, chip in enumerate(chips):
                cp = copy(t, k, landing(t, (x, y), c), (*chip, c))
                cp.start()
                started.append(cp)
        for j in range(n_small):
            own = pltpu.make_async_copy(small_in[j], small_out[j].at[me], s_local.at[j])
            own.start()
            local.append(own)
            for k, chip in enumerate(chips):
                cp = small_copy(j, k, chip, me)
                cp.start()
                started.append(cp)
        for t in range(n_items):
            for k, chip in enumerate(chips):
                copy(t, k, landing(t, chip, c), (x, y, c)).wait_recv()
                fwd = copy(t, 3 + k, landing(t, chip, c), sibling)
                fwd.start()
                started.append(fwd)
        for t in range(n_items):
            for k, chip in enumerate(chips):
                copy(t, 3 + k, landing(t, chip, 1 - c), sibling).wait_recv()
        for j in range(n_small):
            for k, chip in enumerate(chips):
                small_copy(j, k, chip, _chip_index(chip)).wait_recv()
        for cp in started:
            cp.wait_send()
        for own in local:
            own.wait()

    out_shape = [jax.ShapeDtypeStruct(s.shape, BF16) for s in slots]
    out_shape += [jax.ShapeDtypeStruct((N_CHIPS,) + s.shape, F32) for s in smalls]
    dma = pltpu.SemaphoreType.DMA
    res = _comm_call(body, "gather_weights", n_items + n_small, out_shape,
                     [dma((n_items, 6)), dma((n_items, 6)), dma((n_small, 3)), dma((n_small, 3)), dma((n_small,))],
                     aliases={t: t for t in range(n_items)})(*slots, *smalls)
    return list(res[:n_items]), list(res[n_items:])


def _exchange_halves(grads):
    n = len(grads)

    def body(*refs):
        g_in, outs = refs[:n], refs[n:2 * n]
        send_sems, recv_sems = refs[-2:]
        x, y, c, _ = _place()
        copies = []
        for t in range(n):
            half = g_in[t].shape[1] // 2
            cp = pltpu.make_async_remote_copy(
                src_ref=g_in[t].at[:, pl.ds((1 - c) * half, half), :], dst_ref=outs[t], send_sem=send_sems.at[t],
                recv_sem=recv_sems.at[t], device_id=(x, y, 1 - c), device_id_type=MESH)
            cp.start()
            copies.append(cp)
        for cp in copies:
            cp.wait()

    out_shape = [jax.ShapeDtypeStruct((g.shape[0], g.shape[1] // 2, g.shape[2]), BF16) for g in grads]
    dma = pltpu.SemaphoreType.DMA
    return list(_comm_call(body, "grad_exchange_halves", n, out_shape, [dma((n,)), dma((n,))])(*grads))


def _scatter_partials(parts):
    n = len(parts)

    def body(*refs):
        p_in, outs = refs[:n], refs[n:2 * n]
        send_sems, recv_sems = refs[-2:]
        x, y, c, chips = _place()
        copies = []
        for t in range(n):
            for k, chip in enumerate(chips):
                cp = pltpu.make_async_remote_copy(
                    src_ref=p_in[t].at[_chip_index(chip)], dst_ref=outs[t].at[k], send_sem=send_sems.at[t, k],
                    recv_sem=recv_sems.at[t, k], device_id=(*chip, c), device_id_type=MESH)
                cp.start()
                copies.append(cp)
        for cp in copies:
            cp.wait()

    out_shape = [jax.ShapeDtypeStruct((3,) + p.shape[1:], BF16) for p in parts]
    dma = pltpu.SemaphoreType.DMA
    return list(_comm_call(body, "grad_scatter_partials", n, out_shape, [dma((n, 3)), dma((n, 3))])(*parts))


def _share_halves(fulls):
    n = len(fulls)
    items = [(a, l) for a in range(n) for l in range(fulls[a].shape[0])]

    def body(*refs):
        outs = refs[n:2 * n]
        send_sems, recv_sems = refs[-2:]
        x, y, c, _ = _place()

        def copy(t, core):
            a, l = items[t]
            half = outs[a].shape[1] // 2
            block = outs[a].at[l, pl.ds(core * half, half), :]
            return pltpu.make_async_remote_copy(src_ref=block, dst_ref=block, send_sem=send_sems.at[t],
                                                recv_sem=recv_sems.at[t], device_id=(x, y, 1 - c), device_id_type=MESH)

        for t in range(len(items)):
            copy(t, c).start()
        for t in range(len(items)):
            copy(t, 1 - c).wait_recv()
        for t in range(len(items)):
            copy(t, c).wait_send()

    out_shape = [jax.ShapeDtypeStruct(f.shape, F32) for f in fulls]
    dma = pltpu.SemaphoreType.DMA
    return list(_comm_call(body, "grad_share_halves", n, out_shape, [dma((len(items),)), dma((len(items),))],
                           aliases={a: a for a in range(n)})(*fulls))


def _allgather_small(packed):
    def body(p_ref, o_ref, send_sems, recv_sems, local_sem):
        x, y, c, _ = _place()
        me = 4 * x + 2 * y + c
        own = pltpu.make_async_copy(p_ref, o_ref.at[me], local_sem)
        own.start()
        flips = [(fx, fy, fc) for fx in (0, 1) for fy in (0, 1) for fc in (0, 1)][1:]
        peers = [(x ^ fx, y ^ fy, c ^ fc) for fx, fy, fc in flips]
        copies = []
        for k, peer in enumerate(peers):
            cp = pltpu.make_async_remote_copy(src_ref=p_ref, dst_ref=o_ref.at[me], send_sem=send_sems.at[k],
                                              recv_sem=recv_sems.at[k], device_id=peer, device_id_type=MESH)
            cp.start()
            copies.append(cp)
        for k, (px, py, pc) in enumerate(peers):
            pltpu.make_async_remote_copy(src_ref=p_ref, dst_ref=o_ref.at[4 * px + 2 * py + pc], send_sem=send_sems.at[k],
                                         recv_sem=recv_sems.at[k], device_id=peers[k], device_id_type=MESH).wait_recv()
        for cp in copies:
            cp.wait_send()
        own.wait()

    dma = pltpu.SemaphoreType.DMA
    return _comm_call(body, "small_allgather", 1, [jax.ShapeDtypeStruct((8,) + packed.shape, F32)],
                      [dma((7,)), dma((7,)), dma(())])(packed)[0]


PAD_ROWS = 8


def kernel(x, norm_a, w_in_a, w_grp_a, scale_a, w_out_a, norm_kv, w_k, w_v, norm_b, w_in_b, w_out_b, norm_f, loss_target, m_norm_a, m_w_in_a, m_w_grp_a, m_scale_a, m_w_out_a, m_norm_kv, m_w_k, m_w_v, m_norm_b, m_w_in_b, m_w_out_b, m_norm_f, v_norm_a, v_w_in_a, v_w_grp_a, v_scale_a, v_w_out_a, v_norm_kv, v_w_k, v_w_v, v_norm_b, v_w_in_b, v_w_out_b, v_norm_f):
    weights = dict(norm_a=norm_a, w_in_a=w_in_a, w_grp_a=w_grp_a, scale_a=scale_a, w_out_a=w_out_a, norm_kv=norm_kv,
                   w_k=w_k, w_v=w_v, norm_b=norm_b, w_in_b=w_in_b, w_out_b=w_out_b, norm_f=norm_f)
    moments_m = dict(norm_a=m_norm_a, w_in_a=m_w_in_a, w_grp_a=m_w_grp_a, scale_a=m_scale_a, w_out_a=m_w_out_a,
                     norm_kv=m_norm_kv, w_k=m_w_k, w_v=m_w_v, norm_b=m_norm_b, w_in_b=m_w_in_b, w_out_b=m_w_out_b,
                     norm_f=m_norm_f)
    moments_v = dict(norm_a=v_norm_a, w_in_a=v_w_in_a, w_grp_a=v_w_grp_a, scale_a=v_scale_a, w_out_a=v_w_out_a,
                     norm_kv=v_norm_kv, w_k=v_w_k, w_v=v_w_v, norm_b=v_norm_b, w_in_b=v_w_in_b, w_out_b=v_w_out_b,
                     norm_f=v_norm_f)
    names = list(weights)
    d = x.shape[-1]
    c_idx = lax.axis_index("c").astype(jnp.int32).reshape(1)
    s_me = 2 * lax.axis_index("x") + lax.axis_index("y")
    s_idx = s_me.astype(jnp.int32).reshape(1)

    def as_lbrc(name):
        a = weights[name]
        if name == "w_grp_a":
            return a
        if a.ndim == 2:
            return a.reshape(1, 1, *a.shape)
        return a.reshape(a.shape[0], 1, *a.shape[1:])

    slots, slot_of = [], {}
    for name in BIG_WEIGHTS:
        a = as_lbrc(name)
        slot_of[name] = list(range(len(slots), len(slots) + a.shape[0]))
        slots += [_cast_into_slot(f"cast_{name}{l}", a, l, s_idx) for l in range(a.shape[0])]
    slots, small_g = _gather_weights(slots, [norm_a, scale_a])
    gathered = [[slots[t] for t in slot_of[name]] for name in BIG_WEIGHTS]

    w = {}
    for name, per_layer in zip(BIG_WEIGHTS, gathered):
        if name in ("w_in_a", "w_in_b"):
            w[name] = [g[0] for g in per_layer]
        elif name == "w_grp_a":
            w[name] = [g.reshape(g.shape[0], -1, g.shape[-1]) for g in per_layer]
        else:
            w[name] = [g.reshape(-1, g.shape[-1]) for g in per_layer]
    n_a = norm_a.shape[0]
    full_small = lambda g: g.transpose(1, 0, 2).reshape(g.shape[1], -1)
    norm_a_full, scale_a_full = full_small(small_g[0]), full_small(small_g[1])
    w["norm_a"] = [norm_a_full[i] for i in range(n_a)]
    w["scale_a"] = [scale_a_full[i] for i in range(n_a)]
    w["norm_kv"] = [norm_kv]
    w["norm_b"] = [norm_b[i] for i in range(norm_b.shape[0])]
    w["norm_f"] = [norm_f]

    loss_vec, grad_x, big, small = _local_step(x[0], loss_target[0], w)

    small_order = [("norm_a", i) for i in range(n_a)] + [("scale_a", i) for i in range(n_a)] + [("norm_kv", 0)] + \
                  [("norm_b", i) for i in range(norm_b.shape[0])] + [("norm_f", 0)]
    pad = lambda vec: jnp.pad(vec, ((0, PAD_ROWS - 1), (0, 0)))
    packed = jnp.concatenate([pad(loss_vec)] + [pad(small[n][i]) for n, i in small_order], axis=0)
    totals = _sum_devices("small_sum", _allgather_small(packed))
    loss = 0.5 * jnp.sum(totals[0]) / d
    small_tot = {}
    for j, (n, i) in enumerate(small_order):
        small_tot.setdefault(n, []).append(totals[PAD_ROWS * (j + 1)])
    grads = {}
    shard_w = norm_a.shape[1]
    for n in ("norm_a", "scale_a"):
        full = jnp.stack(small_tot[n])
        grads[n] = lax.dynamic_slice_in_dim(full, s_me * shard_w, shard_w, axis=1)
    grads["norm_kv"] = small_tot["norm_kv"][0]
    grads["norm_b"] = jnp.stack(small_tot["norm_b"])
    grads["norm_f"] = small_tot["norm_f"][0]

    flat = [(name, i) for name in BIG_WEIGHTS for i in range(len(big[name]))]
    g_list = [big[name][i] for name, i in flat]
    recv1 = _exchange_halves(g_list)
    parts = [_pair_add(f"pair_add_{name}{i}", g, r, c_idx) for (name, i), g, r in zip(flat, g_list, recv1)]
    recv2 = _scatter_partials(parts)
    sc_idx = jnp.concatenate([s_idx, c_idx])
    fulls = {name: None for name in BIG_WEIGHTS}
    for (name, i), p, r in zip(flat, parts, recv2):
        fulls[name] = _final_add(f"final_add_{name}{i}", p, r, sc_idx, i, len(big[name]), into=fulls[name])
    shared = _share_halves([fulls[name] for name in BIG_WEIGHTS])
    for name, g in zip(BIG_WEIGHTS, shared):
        grads[name] = g.reshape(weights[name].shape)

    deltas, new_m, new_v = {}, {}, {}
    for n in names:
        shape = weights[n].shape
        as2d = (lambda a: a.reshape(1, -1)) if len(shape) == 1 else (lambda a: a)
        dl, mn, vn = _adamw(f"adamw_{n}", as2d(weights[n]), as2d(grads[n]), as2d(moments_m[n]), as2d(moments_v[n]))
        deltas[n], new_m[n], new_v[n] = dl.reshape(shape), mn.reshape(shape), vn.reshape(shape)

    return (loss, grad_x[None], *[grads[n] for n in names], *[deltas[n] for n in names],
            *[new_m[n] for n in names], *[new_v[n] for n in names])
```

```python
import functools
import math

import jax
import jax.numpy as jnp
from jax import lax
from jax.experimental import pallas as pl
from jax.experimental.pallas import tpu as pltpu

F32 = jnp.float32
BF16 = jnp.bfloat16

HEAD_DIM = 128
POOL_WINDOWS = (2, 4, 8, 16)
DILATED_PAIRS = ((128, 1), (512, 4), (2048, 16))
ROPE_THETA = 10000.0
RMS_EPS = 1e-6
NEG_INF = -1e30
N_CHIPS = 4

ADAM_LR = 0.001
ADAM_B1 = 0.9
ADAM_B2 = 0.999
ADAM_EPS = 1e-08
ADAM_WD = 0.01
ADAM_STEP = 10

VMEM_LIMIT_BYTES = 56 * 1024 * 1024
MESH = pl.DeviceIdType.MESH
ANY = pl.BlockSpec(memory_space=pl.ANY)


def _tile(n, pref):
    t = min(n, pref)
    assert n % t == 0, (n, pref)
    return t


def _params(sem=None):
    return pltpu.CompilerParams(dimension_semantics=sem, vmem_limit_bytes=VMEM_LIMIT_BYTES)


def _mm(name, a, b, *, grid2, nk, a_blk, a_map, b_blk, b_map, outs, dims, epi=None, epi_in=(), epi_specs=(),
        acc_shape=None, epi_scratch=(), into=None):
    n_epi, n_out = len(epi_in), len(outs)

    def body(*refs):
        a_ref, b_ref = refs[0], refs[1]
        e_refs = refs[2:2 + n_epi]
        first_out = 2 + n_epi + (0 if into is None else 1)
        o_refs = refs[first_out:first_out + n_out]
        s_refs = refs[first_out + n_out + (0 if nk == 1 else 1):]

        def contrib():
            a_val = a_ref[...]
            if a_val.ndim == 3:
                a_val = a_val.reshape(-1, a_val.shape[-1])
            return lax.dot_general(a_val, b_ref[...], (dims, ((), ())), preferred_element_type=F32)

        def finish(acc):
            if epi is None:
                o_refs[0][...] = acc.reshape(o_refs[0].shape).astype(o_refs[0].dtype)
            else:
                epi(acc, e_refs, o_refs, s_refs)

        if nk == 1:
            finish(contrib())
        else:
            acc_ref = refs[first_out + n_out]
            k = pl.program_id(2)

            @pl.when(k == 0)
            def _():
                acc_ref[...] = contrib()

            @pl.when(k > 0)
            def _():
                acc_ref[...] += contrib()

            @pl.when(k == nk - 1)
            def _():
                finish(acc_ref[...])

    scratch = ([] if nk == 1 else [pltpu.VMEM(acc_shape, F32)]) + list(epi_scratch)
    extra_in, extra_specs, aliases = (), (), {}
    if into is not None:
        extra_in, extra_specs, aliases = (into[0],), (ANY,), {2 + n_epi: into[1]}
    res = pl.pallas_call(
        body, name=name, grid=(grid2[0], grid2[1], nk),
        in_specs=[pl.BlockSpec(a_blk, a_map), pl.BlockSpec(b_blk, b_map), *epi_specs, *extra_specs],
        out_specs=[pl.BlockSpec(blk, imap) for _, blk, imap, _ in outs],
        out_shape=[jax.ShapeDtypeStruct(shape, dtype) for shape, _, _, dtype in outs],
        scratch_shapes=scratch, input_output_aliases=aliases,
        compiler_params=_params(("parallel", "parallel", "arbitrary")),
    )(a, b, *epi_in, *extra_in)
    return res[0] if n_out == 1 else tuple(res)


NN = ((1,), (0,))
NT = ((1,), (1,))
TN = ((0,), (0,))


def _rope_apply(t, cos, sin):
    return t * cos + pltpu.roll(t, HEAD_DIM // 2, 1) * sin


def _epi_add(acc, e_refs, o_refs, s_refs):
    o_refs[0][...] = (acc + e_refs[0][...]).astype(o_refs[0].dtype)


def _col_blocks(width):
    return [slice(c * HEAD_DIM, (c + 1) * HEAD_DIM) for c in range(width // HEAD_DIM)]


def _col_scratch(rows, width):
    return pltpu.VMEM((width // HEAD_DIM, rows, HEAD_DIM), F32)


def _to_residue_major(o_ref, scr, d, sl):
    if d == 1:
        o_ref[0, :, sl] = scr[...].astype(o_ref.dtype)
        return
    rows = scr.shape[0] // d
    for r in range(d):
        o_ref[r, :, sl] = scr[pl.ds(r, rows, stride=d), :].astype(o_ref.dtype)


def _from_residue_major(i_ref, scr, d, sl):
    if d == 1:
        return i_ref[0, :, sl].astype(F32)
    rows = i_ref.shape[1]
    for r in range(d):
        scr[pl.ds(r, rows, stride=d), :] = i_ref[r, :, sl].astype(F32)
    return scr[...]


def _make_epi_orders(dils, rope_scale):
    def epi(acc, e_refs, o_refs, s_refs):
        if rope_scale is not None:
            cos = e_refs[0][...]
            sin = e_refs[1][...]
        for c, sl in enumerate(_col_blocks(acc.shape[1])):
            scr = s_refs[0].at[c]
            scr[...] = acc[:, sl] if rope_scale is None else _rope_apply(acc[:, sl], cos, sin) * rope_scale
            for o_ref, d in zip(o_refs, dils):
                _to_residue_major(o_ref, scr, d, sl)
    return epi


def _make_epi_token_order(d, has_add):
    def epi(acc, e_refs, o_refs, s_refs):
        o_ref = o_refs[0]
        if d == 1:
            o_ref[...] = acc + e_refs[0][...] if has_add else acc
            return
        rows = acc.shape[0] // d
        for c, sl in enumerate(_col_blocks(acc.shape[1])):
            scr = s_refs[0].at[c]
            for r in range(d):
                scr[pl.ds(r, rows, stride=d), :] = acc[r * rows:(r + 1) * rows, sl]
            o_ref[:, sl] = scr[...] + e_refs[0][:, sl] if has_add else scr[...]
    return epi


def _mm_act_w(name, a, w, *, out_dtype=BF16, add=None, rope=None, n_first=0, n_cols=None, dils=None):
    s_len, k_len = a.shape
    bm = _tile(s_len, 1024)
    epi, epi_in, epi_specs, epi_scratch = None, (), (), ()
    if w.ndim == 3:
        ns, _, c = w.shape
        ns_used = ns if n_cols is None else n_cols
        bn = _tile(c, 1024)
        sub = c // bn
        grid2 = (ns_used * sub, s_len // bm)
        b_blk, b_map = (None, k_len, bn), (lambda j, i, k: (j // sub + n_first, 0, j % sub))
        n_len = ns_used * c
    else:
        n_len = w.shape[1]
        bn = _tile(n_len, 1024)
        grid2 = (n_len // bn, s_len // bm)
        b_blk, b_map = (k_len, bn), (lambda j, i, k: (0, j))
    if add is not None:
        epi, epi_in = _epi_add, (add,)
        epi_specs = (pl.BlockSpec((bm, bn), lambda j, i, k: (i, j)),)
    outs = [((s_len, n_len), (bm, bn), lambda j, i, k: (i, j), out_dtype)]
    if dils is not None:
        if rope is not None:
            epi_in = rope[:2]
            epi_specs = (pl.BlockSpec((bm, HEAD_DIM), lambda j, i, k: (i, 0)),) * 2
        epi = _make_epi_orders(dils, None if rope is None else rope[2])
        epi_scratch = (_col_scratch(bm, bn),)
        outs = [((d, s_len // d, n_len), (d, bm // d, bn), lambda j, i, k: (0, i, j), BF16) for d in dils]
    res = _mm(name, a, w, grid2=grid2, nk=1, a_blk=(bm, k_len), a_map=lambda j, i, k: (i, 0),
              b_blk=b_blk, b_map=b_map, outs=outs, dims=NN, epi=epi, epi_in=epi_in, epi_specs=epi_specs,
              epi_scratch=epi_scratch)
    return (res,) if dils is not None and len(dils) == 1 else res


def _mm_grad_act(name, dy, w, *, add=None, slot=None):
    if slot is not None:
        d = 1 if dy.ndim == 2 else dy.shape[0]
        s_len = dy.shape[-2] * d
        _, k_len, c = w.shape
        bm, bn = _tile(s_len, 1024), _tile(k_len, 1024)
        a_blk, a_map = ((bm, c), lambda j, i, k: (i, 0)) if dy.ndim == 2 else ((d, bm // d, c), lambda j, i, k: (0, i, 0))
        epi_in = () if add is None else (add,)
        return _mm(name, dy, w, grid2=(k_len // bn, s_len // bm), nk=1, a_blk=a_blk, a_map=a_map,
                   b_blk=(None, bn, c), b_map=lambda j, i, k: (slot, j, 0),
                   outs=[((s_len, k_len), (bm, bn), lambda j, i, k: (i, j), F32)], dims=NT,
                   epi=_make_epi_token_order(d, add is not None), epi_in=epi_in,
                   epi_specs=(pl.BlockSpec((bm, bn), lambda j, i, k: (i, j)),) * len(epi_in),
                   epi_scratch=(_col_scratch(bm, bn),) if d > 1 else ())
    s_len, n_len = dy.shape
    bm = _tile(s_len, 1024)
    if w.ndim == 3:
        ns, k_len, c = w.shape
        bk, nk = c, ns
        bn = _tile(k_len, 1024)
        b_blk, b_map = (None, bn, c), (lambda j, i, k: (k, j, 0))
    else:
        k_len = w.shape[0]
        bk = _tile(n_len, 1024)
        nk = n_len // bk
        bn = _tile(k_len, 1024)
        b_blk, b_map = (bn, bk), (lambda j, i, k: (j, k))
    epi, epi_in, epi_specs = None, (), ()
    if add is not None:
        epi, epi_in = _epi_add, (add,)
        epi_specs = (pl.BlockSpec((bm, bn), lambda j, i, k: (i, j)),)
    return _mm(name, dy, w, grid2=(k_len // bn, s_len // bm), nk=nk, a_blk=(bm, bk), a_map=lambda j, i, k: (i, k),
               b_blk=b_blk, b_map=b_map, outs=[((s_len, k_len), (bm, bn), lambda j, i, k: (i, j), F32)],
               dims=NT, epi=epi, epi_in=epi_in, epi_specs=epi_specs, acc_shape=(bm, bn))


def _mm_grad_w(name, a, dy, *, col_shards=None, slot=None, into=None):
    s_len, k_len = a.shape
    n_len = dy.shape[1]
    bk = _tile(s_len, 1024)
    bm = _tile(k_len, 1024)
    if slot is not None:
        bn = _tile(n_len, 1024)
        out = ((col_shards, k_len, n_len), (None, bm, bn), lambda j, i, k: (slot, i, j), BF16)
    elif col_shards:
        c = n_len // col_shards
        bn = _tile(c, 1024)
        sub = c // bn
        out = ((col_shards, k_len, c), (None, bm, bn), lambda j, i, k: (j // sub, i, j % sub), BF16)
    else:
        bn = _tile(n_len, 1024)
        out = ((k_len, n_len), (bm, bn), lambda j, i, k: (i, j), BF16)
    return _mm(name, a, dy, grid2=(n_len // bn, k_len // bm), nk=s_len // bk,
               a_blk=(bk, bm), a_map=lambda j, i, k: (k, i), b_blk=(bk, bn), b_map=lambda j, i, k: (k, j),
               outs=[out], dims=TN, acc_shape=(bm, bn), into=None if into is None else (into, 0))


def _mm_grp_fwd(name, pooled, wg):
    s_len, e = pooled.shape
    ng, g, _ = wg.shape
    bm = _tile(s_len, 1024)
    return _mm(name, pooled, wg, grid2=(ng, s_len // bm), nk=1, a_blk=(bm, g), a_map=lambda j, i, k: (i, j),
               b_blk=(None, g, g), b_map=lambda j, i, k: (j, 0, 0),
               outs=[((s_len, e), (bm, g), lambda j, i, k: (i, j), F32)], dims=NN)


def _mm_grp_grad_act(name, dy, wg):
    s_len, e = dy.shape
    ng, g, _ = wg.shape
    bm = _tile(s_len, 1024)
    return _mm(name, dy, wg, grid2=(ng, s_len // bm), nk=1, a_blk=(bm, g), a_map=lambda j, i, k: (i, j),
               b_blk=(None, g, g), b_map=lambda j, i, k: (j, 0, 0),
               outs=[((s_len, e), (bm, g), lambda j, i, k: (i, j), F32)], dims=NT)


def _mm_grp_grad_w(name, pooled, dy, ng):
    s_len, e = pooled.shape
    g = e // ng
    bk = _tile(s_len, 1024)
    return _mm(name, pooled, dy, grid2=(ng, 1), nk=s_len // bk, a_blk=(bk, g), a_map=lambda j, i, k: (k, j),
               b_blk=(bk, g), b_map=lambda j, i, k: (k, j),
               outs=[((N_CHIPS, ng, g // N_CHIPS, g), (N_CHIPS, None, g // N_CHIPS, g), lambda j, i, k: (0, j, 0, 0), BF16)],
               dims=TN, acc_shape=(g, g))


def _row_spec(bs, width, col=0):
    return pl.BlockSpec((bs, width), lambda i: (i, col))


def _vec_spec(width):
    return pl.BlockSpec((1, width), lambda i: (0, 0))


def _rows_call(body, name, s_len, in_specs, out_specs, out_shape, bs, aliases=None, sequential=False):
    return pl.pallas_call(
        body, name=name, grid=(s_len // bs,), in_specs=in_specs, out_specs=out_specs, out_shape=out_shape,
        input_output_aliases=aliases or {},
        compiler_params=_params(("arbitrary",) if sequential else ("parallel",)))


def _accumulate(ref, part):
    i = pl.program_id(0)

    @pl.when(i == 0)
    def _():
        ref[...] = part

    @pl.when(i > 0)
    def _():
        ref[...] += part


def _rms_scale(xf):
    return lax.rsqrt(jnp.mean(xf * xf, axis=-1, keepdims=True) + RMS_EPS)


def _res_spec(dil, bs, width):
    return pl.BlockSpec((dil, bs // dil, width), lambda i: (0, i, 0))


def _res_shape(dil, s_len, width, dtype):
    return jax.ShapeDtypeStruct((dil, s_len // dil, width), dtype)


def _rmsnorm_fwd(name, x, gain, dils=()):
    s_len, d = x.shape
    bs = _tile(s_len, 256)

    def body(x_ref, g_ref, h_ref, *rest):
        xf = x_ref[...]
        h = (xf * _rms_scale(xf)) * g_ref[...]
        h_ref[...] = h.astype(BF16)
        if dils:
            for c, sl in enumerate(_col_blocks(d)):
                scr = rest[-1].at[c]
                scr[...] = h[:, sl]
                for o_ref, dil in zip(rest[:-1], dils):
                    _to_residue_major(o_ref, scr, dil, sl)

    res = pl.pallas_call(
        body, name=name, grid=(s_len // bs,), in_specs=[_row_spec(bs, d), _vec_spec(d)],
        out_specs=[_row_spec(bs, d)] + [_res_spec(dil, bs, d) for dil in dils],
        out_shape=[jax.ShapeDtypeStruct((s_len, d), BF16)] + [_res_shape(dil, s_len, d, BF16) for dil in dils],
        scratch_shapes=[_col_scratch(bs, d)] if dils else [],
        compiler_params=_params(("parallel",)))(x, gain)
    return res[0] if not dils else tuple(res)


def _rmsnorm_bwd(name, x, gain, dh, dres):
    s_len, d = x.shape
    bs = _tile(s_len, 256)

    def body(x_ref, g_ref, dh_ref, dres_ref, dx_ref, dxb_ref, dg_ref):
        xf = x_ref[...]
        r = _rms_scale(xf)
        xh = xf * r
        dh_f = dh_ref[...]
        t = dh_f * g_ref[...]
        dx = dres_ref[...] + r * (t - xh * jnp.mean(t * xh, axis=-1, keepdims=True))
        dx_ref[...] = dx
        dxb_ref[...] = dx.astype(BF16)
        _accumulate(dg_ref, jnp.sum(dh_f * xh, axis=0, keepdims=True))

    return _rows_call(
        body, name, s_len,
        [_row_spec(bs, d), _vec_spec(d), _row_spec(bs, d), _row_spec(bs, d)],
        [_row_spec(bs, d), _row_spec(bs, d), _vec_spec(d)],
        [jax.ShapeDtypeStruct((s_len, d), F32), jax.ShapeDtypeStruct((s_len, d), BF16),
         jax.ShapeDtypeStruct((1, d), F32)], bs, sequential=True)(x, gain, dh, dres)


def _loss_head(name, x, gain, target):
    s_len, d = x.shape
    bs = _tile(s_len, 256)

    def body(x_ref, g_ref, t_ref, lv_ref, dx_ref, dxb_ref, dg_ref):
        xf = x_ref[...]
        r = _rms_scale(xf)
        xh = xf * r
        err = xh * g_ref[...] - t_ref[...]
        dy = err * (1.0 / d)
        t = dy * g_ref[...]
        dx = r * (t - xh * jnp.mean(t * xh, axis=-1, keepdims=True))
        dx_ref[...] = dx
        dxb_ref[...] = dx.astype(BF16)
        _accumulate(lv_ref, jnp.sum(err * err, axis=0, keepdims=True))
        _accumulate(dg_ref, jnp.sum(dy * xh, axis=0, keepdims=True))

    return _rows_call(
        body, name, s_len, [_row_spec(bs, d), _vec_spec(d), _row_spec(bs, d)],
        [_vec_spec(d), _row_spec(bs, d), _row_spec(bs, d), _vec_spec(d)],
        [jax.ShapeDtypeStruct((1, d), F32), jax.ShapeDtypeStruct((s_len, d), F32),
         jax.ShapeDtypeStruct((s_len, d), BF16), jax.ShapeDtypeStruct((1, d), F32)],
        bs, sequential=True)(x, gain, target)


def _sigmoid(g):
    return 1.0 / (1.0 + jnp.exp(-g))


def _gate_a_fwd(name, ypre, proj, scale):
    s_len, e = ypre.shape
    bs = _tile(s_len, 256)

    def body(y_ref, g_ref, sc_ref, z_ref):
        g = g_ref[...]
        z_ref[...] = (y_ref[...] * sc_ref[...] * (g * _sigmoid(g))).astype(BF16)

    return _rows_call(body, name, s_len, [_row_spec(bs, e), _row_spec(bs, e, 1), _vec_spec(e)], _row_spec(bs, e),
                      jax.ShapeDtypeStruct((s_len, e), BF16), bs)(ypre, proj, scale)


def _gate_a_bwd(name, dz, ypre, proj, scale):
    s_len, e = ypre.shape
    bs = _tile(s_len, 256)

    def body(dz_ref, y_ref, g_ref, sc_ref, dy_ref, dproj_ref, dsc_ref):
        g = g_ref[...]
        sg = _sigmoid(g)
        silu = g * sg
        dz_f = dz_ref[...]
        ypre_f = y_ref[...]
        dys = dz_f * silu
        dy_ref[...] = (dys * sc_ref[...]).astype(BF16)
        dproj_ref[...] = (dz_f * (ypre_f * sc_ref[...]) * (sg * (1.0 + g * (1.0 - sg)))).astype(BF16)
        _accumulate(dsc_ref, jnp.sum(dys * ypre_f, axis=0, keepdims=True))

    return _rows_call(
        body, name, s_len, [_row_spec(bs, e), _row_spec(bs, e), _row_spec(bs, e, 1), _vec_spec(e)],
        [_row_spec(bs, e), _row_spec(bs, e, 1), _vec_spec(e)],
        [jax.ShapeDtypeStruct((s_len, e), BF16), jax.ShapeDtypeStruct((s_len, 2 * e), BF16),
         jax.ShapeDtypeStruct((1, e), F32)], bs, sequential=True)(dz, ypre, proj, scale)


def _merge_gate_fwd(name, outs, lses, gate, dils):
    s_len, e = gate.shape
    bs = _tile(s_len, 256)
    n = len(outs)

    def body(*refs):
        o_refs, l_refs, g_ref = refs[:n], refs[n:2 * n], refs[2 * n]
        m_ref, lj_ref, z_ref = refs[2 * n + 1:2 * n + 4]
        scratch = refs[2 * n + 4]
        for c, sl in enumerate(_col_blocks(e)):
            ls = [_from_residue_major(r, scratch.at[2 * j, c], dil, sl) for j, (r, dil) in enumerate(zip(l_refs, dils))]
            os_ = [_from_residue_major(r, scratch.at[2 * j + 1, c], dil, sl) for j, (r, dil) in enumerate(zip(o_refs, dils))]
            mx = functools.reduce(jnp.maximum, ls)
            ws = [jnp.exp(l - mx) for l in ls]
            den = functools.reduce(lambda a, b: a + b, ws)
            merged = functools.reduce(lambda a, b: a + b, [w * o for w, o in zip(ws, os_)]) / den
            g = g_ref[:, sl]
            m_ref[:, sl] = merged.astype(BF16)
            lj_ref[:, sl] = mx + jnp.log(den)
            z_ref[:, sl] = (merged * (g * _sigmoid(g))).astype(BF16)

    spec = _row_spec(bs, e)
    res_specs = [_res_spec(dil, bs, e) for dil in dils]
    return pl.pallas_call(
        body, name=name, grid=(s_len // bs,), in_specs=res_specs + res_specs + [spec], out_specs=[spec] * 3,
        out_shape=[jax.ShapeDtypeStruct((s_len, e), BF16), jax.ShapeDtypeStruct((s_len, e), F32),
                   jax.ShapeDtypeStruct((s_len, e), BF16)],
        scratch_shapes=[pltpu.VMEM((2 * n, e // HEAD_DIM, bs, HEAD_DIM), F32)],
        compiler_params=_params(("parallel",)))(*outs, *lses, gate)


def _gate_b_bwd(name, dz, merged, gate, lse, dils):
    s_len, e = gate.shape
    bs = _tile(s_len, 256)
    n = len(dils)

    def body(dz_ref, m_ref, g_ref, l_ref, dg_ref, *rest):
        out_refs, scratch = rest[:3 * n], rest[3 * n]
        for c, sl in enumerate(_col_blocks(e)):
            g = g_ref[:, sl]
            sg = _sigmoid(g)
            dz_f = dz_ref[:, sl]
            merged = m_ref[:, sl].astype(F32)
            dmerged = dz_f * (g * sg)
            dg_ref[:, sl] = (dz_f * merged * (sg * (1.0 + g * (1.0 - sg)))).astype(BF16)
            values = (dmerged, l_ref[:, sl],
                      jnp.broadcast_to(jnp.sum(dmerged * merged, axis=-1, keepdims=True), (bs, HEAD_DIM)))
            for t, val in enumerate(values):
                scr = scratch.at[t, c]
                scr[...] = val
                for j, dil in enumerate(dils):
                    _to_residue_major(out_refs[3 * j + t], scr, dil, sl)

    spec = _row_spec(bs, e)
    out_specs, out_shape = [spec], [jax.ShapeDtypeStruct((s_len, e), BF16)]
    for dil in dils:
        out_specs += [_res_spec(dil, bs, e)] * 3
        out_shape += [_res_shape(dil, s_len, e, BF16), _res_shape(dil, s_len, e, F32), _res_shape(dil, s_len, e, F32)]
    res = pl.pallas_call(
        body, name=name, grid=(s_len // bs,), in_specs=[spec] * 4, out_specs=out_specs, out_shape=out_shape,
        scratch_shapes=[pltpu.VMEM((3, e // HEAD_DIM, bs, HEAD_DIM), F32)],
        compiler_params=_params(("parallel",)))(dz, merged, gate, lse)
    return res[0], [tuple(res[1 + 3 * j:4 + 3 * j]) for j in range(n)]


def _kv_grad_prep(name, dk_accs, dv_accs, dils, cos, sin_inv):
    n = len(dils)
    e = dk_accs[0].shape[-1]
    s_len = dk_accs[0].shape[0] * dk_accs[0].shape[1]
    bs = _tile(s_len, 256)

    def body(*refs):
        dk_refs, dv_refs = refs[:n], refs[n:2 * n]
        c_ref, s_ref, dkb_ref, dvb_ref, scratch = refs[2 * n:]
        cos_t, sin_t = c_ref[...], s_ref[...]
        add = lambda a, b: a + b
        for c, sl in enumerate(_col_blocks(e)):
            dk = functools.reduce(add, [_from_residue_major(r, scratch.at[j, c], dil, sl)
                                        for j, (r, dil) in enumerate(zip(dk_refs, dils))])
            dkb_ref[:, sl] = _rope_apply(dk, cos_t, sin_t).astype(BF16)
            dv = functools.reduce(add, [_from_residue_major(r, scratch.at[n + j, c], dil, sl)
                                        for j, (r, dil) in enumerate(zip(dv_refs, dils))])
            dvb_ref[:, sl] = dv.astype(BF16)

    spec, rspec = _row_spec(bs, e), _row_spec(bs, HEAD_DIM)
    res_specs = [_res_spec(dil, bs, e) for dil in dils]
    return pl.pallas_call(
        body, name=name, grid=(s_len // bs,), in_specs=res_specs + res_specs + [rspec, rspec], out_specs=[spec, spec],
        out_shape=[jax.ShapeDtypeStruct((s_len, e), BF16)] * 2,
        scratch_shapes=[pltpu.VMEM((2 * n, e // HEAD_DIM, bs, HEAD_DIM), F32)],
        compiler_params=_params(("parallel",)))(*dk_accs, *dv_accs, cos, sin_inv)


def _pool_cols(e):
    return _tile(e // len(POOL_WINDOWS), 256)


def _window_sum(val, grp, s_len, forward):
    rows = lax.broadcasted_iota(jnp.int32, val.shape, 0)
    acc = val
    for level in range(len(POOL_WINDOWS)):
        step = 1 << level
        if forward:
            shifted = jnp.where(rows >= step, pltpu.roll(acc, step, 0), 0.0)
        else:
            shifted = jnp.where(rows < s_len - step, pltpu.roll(acc, s_len - step, 0), 0.0)
        acc = jnp.where(level <= grp, acc + shifted, acc)
    return acc


def _window_count(shape, grp):
    rows = lax.broadcasted_iota(jnp.int32, shape, 0)
    return jnp.minimum(rows + 1, jnp.left_shift(2, grp)).astype(F32)


def _pool_fwd(name, proj):
    s_len, e2 = proj.shape
    e = e2 // 2
    cb = _pool_cols(e)
    per_grp = e // len(POOL_WINDOWS) // cb
    assert POOL_WINDOWS == tuple(2 << g for g in range(len(POOL_WINDOWS)))

    def body(u_ref, p_ref):
        grp = pl.program_id(0)
        u = u_ref[...]
        total = _window_sum(u, grp, s_len, True)
        p_ref[...] = (total / _window_count(u.shape, grp) - u).astype(BF16)

    spec = pl.BlockSpec((s_len, cb), lambda g, c: (0, g * per_grp + c))
    return pl.pallas_call(
        body, name=name, grid=(len(POOL_WINDOWS), per_grp), in_specs=[spec], out_specs=spec,
        out_shape=jax.ShapeDtypeStruct((s_len, e), BF16), compiler_params=_params(("parallel", "parallel")))(proj)


def _pool_bwd(name, dpooled, dproj):
    s_len, e = dpooled.shape
    cb = _pool_cols(e)
    per_grp = e // len(POOL_WINDOWS) // cb

    def body(dp_ref, _, du_ref):
        grp = pl.program_id(0)
        dp = dp_ref[...]
        total = _window_sum(dp / _window_count(dp.shape, grp), grp, s_len, False)
        du_ref[...] = (total - dp).astype(BF16)

    spec = pl.BlockSpec((s_len, cb), lambda g, c: (0, g * per_grp + c))
    return pl.pallas_call(
        body, name=name, grid=(len(POOL_WINDOWS), per_grp), in_specs=[spec, ANY], out_specs=spec,
        out_shape=jax.ShapeDtypeStruct(dproj.shape, BF16), input_output_aliases={1: 0},
        compiler_params=_params(("parallel", "parallel")))(dpooled, dproj)


def _band_masks(nb, first):
    row = lax.broadcasted_iota(jnp.int32, (nb, nb), 0)
    col = lax.broadcasted_iota(jnp.int32, (nb, nb), 1)
    return col >= row + jnp.where(first, 2 * nb, 0), col <= row


def _dot(a, b, dims):
    return lax.dot_general(a, b, (dims, ((), ())), preferred_element_type=F32)


def _attn_fwd(name, window, q, k, v):
    dil, m, e = k.shape
    nb = window // dil
    nblk = m // nb
    heads = e // HEAD_DIM

    def body(q_ref, kp_ref, kc_ref, vp_ref, vc_ref, o_ref, l_ref):
        mask_p, mask_c = _band_masks(nb, pl.program_id(1) == 0)
        for h in range(heads):
            sl = slice(h * HEAD_DIM, (h + 1) * HEAD_DIM)
            q = q_ref[:, sl]
            s_p = jnp.where(mask_p, _dot(q, kp_ref[:, sl], NT), NEG_INF)
            s_c = jnp.where(mask_c, _dot(q, kc_ref[:, sl], NT), NEG_INF)
            mx = jnp.maximum(jnp.max(s_p, axis=-1, keepdims=True), jnp.max(s_c, axis=-1, keepdims=True))
            p_p = jnp.exp(s_p - mx)
            p_c = jnp.exp(s_c - mx)
            den = jnp.sum(p_p, axis=-1, keepdims=True) + jnp.sum(p_c, axis=-1, keepdims=True)
            out = _dot(p_p.astype(BF16), vp_ref[:, sl], NN) + _dot(p_c.astype(BF16), vc_ref[:, sl], NN)
            o_ref[:, sl] = (out / den).astype(BF16)
            l_ref[:, sl] = jnp.broadcast_to(mx + jnp.log(den), (nb, HEAD_DIM))

    blk = (None, nb, e)
    prev = lambda r, n: (r, jnp.maximum(n - 1, 0), 0)
    cur = lambda r, n: (r, n, 0)
    return pl.pallas_call(
        body, name=name, grid=(dil, nblk),
        in_specs=[pl.BlockSpec(blk, cur), pl.BlockSpec(blk, prev), pl.BlockSpec(blk, cur), pl.BlockSpec(blk, prev),
                  pl.BlockSpec(blk, cur)],
        out_specs=[pl.BlockSpec(blk, cur), pl.BlockSpec(blk, cur)],
        out_shape=[jax.ShapeDtypeStruct((dil, m, e), BF16), jax.ShapeDtypeStruct((dil, m, e), F32)],
        compiler_params=_params(("parallel", "arbitrary")),
    )(q, k, k, v, v)


def _attn_bwd(name, window, scale, q, k, v, dout, lse, delta, cos, sin_inv, dk_acc, dv_acc):
    dil, m, e = k.shape
    nb = window // dil
    nblk = m // nb
    heads = e // HEAD_DIM

    def body(q_ref, kp_ref, kc_ref, vp_ref, vc_ref, do_ref, l_ref, dl_ref, c_ref, s_ref, dki_ref, dvi_ref,
             dq_ref, dko_ref, dvo_ref, ck_ref, cv_ref):
        n = pl.program_id(1)

        @pl.when(n == 0)
        def _():
            ck_ref[...] = jnp.zeros_like(ck_ref)
            cv_ref[...] = jnp.zeros_like(cv_ref)

        @pl.when(n < nblk)
        def _():
            mask_p, mask_c = _band_masks(nb, n == 0)
            cos_t, sin_t = c_ref[...], s_ref[...]
            for h in range(heads):
                sl = slice(h * HEAD_DIM, (h + 1) * HEAD_DIM)
                q, kp, kc, vp, vc = q_ref[:, sl], kp_ref[:, sl], kc_ref[:, sl], vp_ref[:, sl], vc_ref[:, sl]
                do = do_ref[:, sl]
                lj = l_ref[:, sl] if nb == HEAD_DIM else l_ref[:, sl][:, :1]
                delta = dl_ref[:, sl] if nb == HEAD_DIM else dl_ref[:, sl][:, :1]
                p_p = jnp.where(mask_p, jnp.exp(_dot(q, kp, NT) - lj), 0.0)
                p_c = jnp.where(mask_c, jnp.exp(_dot(q, kc, NT) - lj), 0.0)
                ds_p = (p_p * (_dot(do, vp, NT) - delta)).astype(BF16)
                ds_c = (p_c * (_dot(do, vc, NT) - delta)).astype(BF16)
                dq = (_dot(ds_p, kp, NN) + _dot(ds_c, kc, NN)) * scale
                dq_ref[:, sl] = _rope_apply(dq, cos_t, sin_t).astype(BF16)
                dko_ref[:, sl] = dki_ref[:, sl] + ck_ref[:, sl] + _dot(ds_p, q, TN)
                dvo_ref[:, sl] = dvi_ref[:, sl] + cv_ref[:, sl] + _dot(p_p.astype(BF16), do, TN)
                ck_ref[:, sl] = _dot(ds_c, q, TN)
                cv_ref[:, sl] = _dot(p_c.astype(BF16), do, TN)

        @pl.when(n == nblk)
        def _():
            dko_ref[...] = dki_ref[...] + ck_ref[...]
            dvo_ref[...] = dvi_ref[...] + cv_ref[...]

    blk = (None, nb, e)
    qn = lambda n: jnp.minimum(n, nblk - 1)
    cur = lambda r, n: (r, qn(n), 0)
    prev = lambda r, n: (r, jnp.maximum(qn(n) - 1, 0), 0)
    kprev = lambda r, n: (r, jnp.maximum(n - 1, 0), 0)
    rblk = (None, nb, HEAD_DIM)
    return pl.pallas_call(
        body, name=name, grid=(dil, nblk + 1),
        in_specs=[pl.BlockSpec(blk, cur),
                  pl.BlockSpec(blk, prev), pl.BlockSpec(blk, cur), pl.BlockSpec(blk, prev), pl.BlockSpec(blk, cur),
                  pl.BlockSpec(blk, cur), pl.BlockSpec(blk, cur), pl.BlockSpec(blk, cur),
                  pl.BlockSpec(rblk, cur), pl.BlockSpec(rblk, cur),
                  pl.BlockSpec(blk, kprev), pl.BlockSpec(blk, kprev)],
        out_specs=[pl.BlockSpec(blk, cur), pl.BlockSpec(blk, kprev), pl.BlockSpec(blk, kprev)],
        out_shape=[jax.ShapeDtypeStruct((dil, m, e), BF16),
                   jax.ShapeDtypeStruct((dil, m, e), F32), jax.ShapeDtypeStruct((dil, m, e), F32)],
        scratch_shapes=[pltpu.VMEM((nb, e), F32), pltpu.VMEM((nb, e), F32)],
        input_output_aliases={10: 1, 11: 2},
        compiler_params=_params(("parallel", "arbitrary")),
    )(q, k, k, v, v, dout, lse, delta, cos, sin_inv, dk_acc, dv_acc)


def _rope_tables(s_len):
    inv_freq = 1.0 / (ROPE_THETA ** (jnp.arange(0, HEAD_DIM, 2, dtype=F32) / HEAD_DIM))
    ang = jnp.arange(s_len, dtype=F32)[:, None] * inv_freq[None, :]
    cos, sin = jnp.cos(ang), jnp.sin(ang)
    return jnp.concatenate([cos, cos], axis=1), jnp.concatenate([-sin, sin], axis=1)


def _row(vec):
    return vec.reshape(1, -1)


def _local_step(x, target, w):
    s_len, d = x.shape
    n_a, n_b = len(w["w_in_a"]), len(w["w_in_b"])
    n_q = len(DILATED_PAIRS)
    e = w["w_k"][0].shape[1]
    cos, sin = _rope_tables(s_len)
    sin_inv = -sin
    q_scale = 1.0 / math.sqrt(HEAD_DIM)

    saved_a = []
    for i in range(n_a):
        h = _rmsnorm_fwd(f"a{i}_norm", x, _row(w["norm_a"][i]))
        proj = _mm_act_w(f"a{i}_in", h, w["w_in_a"][i], out_dtype=F32)
        pooled = _pool_fwd(f"a{i}_pool", proj)
        ypre = _mm_grp_fwd(f"a{i}_grp", pooled, w["w_grp_a"][i])
        z = _gate_a_fwd(f"a{i}_gate", ypre, proj, _row(w["scale_a"][i]))
        x_next = _mm_act_w(f"a{i}_out", z, w["w_out_a"][i], out_dtype=F32, add=x)
        saved_a.append((x, h, proj, pooled, ypre, z))
        x = x_next

    x_kv = x
    kv_in = _rmsnorm_fwd("kv_norm", x, _row(w["norm_kv"][0]))
    windows = [window for window, _ in DILATED_PAIRS]
    dils = tuple(dil for _, dil in DILATED_PAIRS)
    far_dils = tuple(dil for dil in dils if dil > 1)
    ks = _mm_act_w("kv_k", kv_in, w["w_k"][0], rope=(cos, sin, 1.0), dils=dils)
    vs = _mm_act_w("kv_v", kv_in, w["w_v"][0], dils=dils)

    saved_b = []
    for i in range(n_b):
        hs = _rmsnorm_fwd(f"b{i}_norm", x, _row(w["norm_b"][i]), dils=far_dils)
        hs = {1: hs[0], **{dil: h_d.reshape(s_len, d) for dil, h_d in zip(far_dils, hs[1:])}}
        qs = [_mm_act_w(f"b{i}_q{g}", hs[1], w["w_in_b"][i], rope=(cos, sin, q_scale), n_first=g, n_cols=1,
                        dils=(dil,))[0] for g, dil in enumerate(dils)]
        gate = _mm_act_w(f"b{i}_g", hs[1], w["w_in_b"][i], out_dtype=F32, n_first=n_q, n_cols=1)
        outs, lses = [], []
        for g in range(n_q):
            o_g, l_g = _attn_fwd(f"b{i}_attn{g}", windows[g], qs[g], ks[g], vs[g])
            outs.append(o_g)
            lses.append(l_g)
        merged, lse, z = _merge_gate_fwd(f"b{i}_merge", outs, lses, gate, dils)
        x_next = _mm_act_w(f"b{i}_out", z, w["w_out_b"][i], out_dtype=F32, add=x)
        saved_b.append((x, hs, qs, gate, merged, lse, z))
        x = x_next

    loss_vec, dx, dxb, g_norm_f = _loss_head("loss_head", x, _row(w["norm_f"][0]), target)

    big = {name: [None] * len(w[name]) for name in BIG_WEIGHTS}
    small = {"norm_a": [None] * n_a, "scale_a": [None] * n_a, "norm_kv": [None], "norm_b": [None] * n_b,
             "norm_f": [g_norm_f]}
    shard_rows = lambda g2: g2.reshape(N_CHIPS, g2.shape[0] // N_CHIPS, g2.shape[1])

    res_major = lambda t, dil: t.reshape(s_len // dil, dil, t.shape[1]).transpose(1, 0, 2)
    cos_r = [res_major(cos, dil) for dil in dils]
    sin_inv_r = [res_major(sin_inv, dil) for dil in dils]
    dk_accs = [jnp.zeros((dil, s_len // dil, e), F32) for dil in dils]
    dv_accs = [jnp.zeros((dil, s_len // dil, e), F32) for dil in dils]
    for i in reversed(range(n_b)):
        x_in, hs, qs, gate, merged, lse, z = saved_b[i]
        dz = _mm_grad_act(f"b{i}_dz", dxb, w["w_out_b"][i])
        big["w_out_b"][i] = shard_rows(_mm_grad_w(f"b{i}_gwo", z, dxb))
        dgate, stats = _gate_b_bwd(f"b{i}_dgate", dz, merged, gate, lse, dils)
        dh = _mm_grad_act(f"b{i}_dh{n_q}", dgate, w["w_in_b"][i], slot=n_q)
        g_in = _mm_grad_w(f"b{i}_gwi{n_q}", hs[1], dgate, col_shards=n_q + 1, slot=n_q)
        for g, dil in enumerate(dils):
            dout, lse_g, delta_g = stats[g]
            dq, dk_accs[g], dv_accs[g] = _attn_bwd(f"b{i}_dattn{g}", windows[g], q_scale, qs[g], ks[g], vs[g], dout,
                                                   lse_g, delta_g, cos_r[g], sin_inv_r[g], dk_accs[g], dv_accs[g])
            dh = _mm_grad_act(f"b{i}_dh{g}", dq if dil > 1 else dq[0], w["w_in_b"][i], add=dh, slot=g)
            g_in = _mm_grad_w(f"b{i}_gwi{g}", hs[dil], dq.reshape(s_len, e), col_shards=n_q + 1, slot=g, into=g_in)
        big["w_in_b"][i] = g_in
        dx, dxb, small["norm_b"][i] = _rmsnorm_bwd(f"b{i}_dnorm", x_in, _row(w["norm_b"][i]), dh, dx)

    dkb, dvb = _kv_grad_prep("kv_dprep", dk_accs, dv_accs, dils, cos, sin_inv)
    dkv = _mm_grad_act("kv_dk", dkb, w["w_k"][0])
    dkv = _mm_grad_act("kv_dv", dvb, w["w_v"][0], add=dkv)
    big["w_k"][0] = shard_rows(_mm_grad_w("kv_gwk", kv_in, dkb))
    big["w_v"][0] = shard_rows(_mm_grad_w("kv_gwv", kv_in, dvb))
    dx, dxb, small["norm_kv"][0] = _rmsnorm_bwd("kv_dnorm", x_kv, _row(w["norm_kv"][0]), dkv, dx)

    for i in reversed(range(n_a)):
        x_in, h, proj, pooled, ypre, z = saved_a[i]
        dz = _mm_grad_act(f"a{i}_dz", dxb, w["w_out_a"][i])
        big["w_out_a"][i] = shard_rows(_mm_grad_w(f"a{i}_gwo", z, dxb))
        dypre, dproj, small["scale_a"][i] = _gate_a_bwd(f"a{i}_dgate", dz, ypre, proj, _row(w["scale_a"][i]))
        dpooled = _mm_grp_grad_act(f"a{i}_dgrp", dypre, w["w_grp_a"][i])
        g_grp = _mm_grp_grad_w(f"a{i}_gwg", pooled, dypre, len(POOL_WINDOWS))
        big["w_grp_a"][i] = g_grp.reshape(N_CHIPS, -1, g_grp.shape[-1])
        dproj = _pool_bwd(f"a{i}_dpool", dpooled, dproj)
        dh = _mm_grad_act(f"a{i}_dh", dproj, w["w_in_a"][i])
        big["w_in_a"][i] = _mm_grad_w(f"a{i}_gwi", h, dproj, col_shards=N_CHIPS)
        dx, dxb, small["norm_a"][i] = _rmsnorm_bwd(f"a{i}_dnorm", x_in, _row(w["norm_a"][i]), dh, dx)

    return loss_vec, dx, big, small


BIG_WEIGHTS = ("w_in_a", "w_grp_a", "w_out_a", "w_k", "w_v", "w_in_b", "w_out_b")


def _pair_add(name, grad, recv, c_idx):
    _, r, cols = grad.shape
    half = r // 2
    rb = _tile(half, 256)
    nrb = half // rb

    def body(c_ref, g_ref, r_ref, o_ref):
        o_ref[...] = (g_ref[...].astype(F32) + r_ref[...].astype(F32)).astype(BF16)

    blk = (None, rb, cols)
    grid_spec = pltpu.PrefetchScalarGridSpec(
        num_scalar_prefetch=1, grid=(N_CHIPS, nrb),
        in_specs=[pl.BlockSpec(blk, lambda s, i, c: (s, c[0] * nrb + i, 0)), pl.BlockSpec(blk, lambda s, i, c: (s, i, 0))],
        out_specs=pl.BlockSpec(blk, lambda s, i, c: (s, i, 0)))
    return pl.pallas_call(body, name=name, grid_spec=grid_spec,
                          out_shape=jax.ShapeDtypeStruct((N_CHIPS, half, cols), BF16),
                          compiler_params=_params(("parallel", "parallel")))(c_idx, grad, recv)


def _final_add(name, part, recv, sc_idx, layer, n_layers, into=None):
    _, half, cols = part.shape
    rb = _tile(half, 256)
    nrb = half // rb
    n_peer = recv.shape[0]

    def body(sc_ref, p_ref, *refs):
        acc = p_ref[...].astype(F32)
        for r_ref in refs[:n_peer]:
            acc = acc + r_ref[...].astype(F32)
        refs[-1][...] = acc

    blk = (None, rb, cols)
    peer_spec = lambda k: pl.BlockSpec(blk, lambda i, sc: (k, i, 0))
    grid_spec = pltpu.PrefetchScalarGridSpec(
        num_scalar_prefetch=1, grid=(nrb,),
        in_specs=[pl.BlockSpec(blk, lambda i, sc: (sc[0], i, 0))] + [peer_spec(k) for k in range(n_peer)]
                 + ([] if into is None else [ANY]),
        out_specs=pl.BlockSpec(blk, lambda i, sc: (layer, sc[1] * nrb + i, 0)))
    extra = () if into is None else (into,)
    return pl.pallas_call(body, name=name, grid_spec=grid_spec,
                          out_shape=jax.ShapeDtypeStruct((n_layers, 2 * half, cols), F32),
                          input_output_aliases={} if into is None else {2 + n_peer: 0},
                          compiler_params=_params(("parallel",)))(sc_idx, part, *([recv] * n_peer), *extra)


def _cast_into_slot(name, arr, layer, s_idx):
    _, b, r, cols = arr.shape
    rb = _tile(r, 512)

    def body(s_ref, a_ref, o_ref):
        o_ref[...] = a_ref[...].astype(BF16)

    blk = (None, None, rb, cols)
    grid_spec = pltpu.PrefetchScalarGridSpec(
        num_scalar_prefetch=1, grid=(b, r // rb),
        in_specs=[pl.BlockSpec(blk, lambda j, i, s: (layer, j, i, 0))],
        out_specs=pl.BlockSpec(blk, lambda j, i, s: (j, s[0], i, 0)))
    return pl.pallas_call(body, name=name, grid_spec=grid_spec,
                          out_shape=jax.ShapeDtypeStruct((b, N_CHIPS, r, cols), BF16),
                          compiler_params=_params(("parallel", "parallel")))(s_idx, arr)


def _sum_devices(name, gathered):
    n_dev, p, d = gathered.shape

    def body(g_ref, o_ref):
        acc = g_ref[0]
        for j in range(1, n_dev):
            acc = acc + g_ref[j]
        o_ref[...] = acc

    return pl.pallas_call(body, name=name, out_shape=jax.ShapeDtypeStruct((p, d), F32),
                          compiler_params=_params())(gathered)


def _adamw(name, w, g, m, v):
    shape = w.shape
    cols = shape[-1]
    flat = lambda a: a.reshape(-1, cols)
    rows = flat(w).shape[0]
    bs = _tile(rows, 256)

    def body(w_ref, g_ref, m_ref, v_ref, d_ref, mo_ref, vo_ref):
        grad = g_ref[...]
        m_new = ADAM_B1 * m_ref[...] + (1.0 - ADAM_B1) * grad
        v_new = ADAM_B2 * v_ref[...] + (1.0 - ADAM_B2) * (grad * grad)
        m_hat = m_new / (1.0 - ADAM_B1 ** ADAM_STEP)
        v_hat = v_new / (1.0 - ADAM_B2 ** ADAM_STEP)
        d_ref[...] = -ADAM_LR * (m_hat / (jnp.sqrt(v_hat) + ADAM_EPS) + ADAM_WD * w_ref[...])
        mo_ref[...] = m_new
        vo_ref[...] = v_new

    spec = _row_spec(bs, cols)
    outs = _rows_call(body, name, rows, [spec] * 4, [spec] * 3, [jax.ShapeDtypeStruct((rows, cols), F32)] * 3, bs)(
        flat(w), flat(g), flat(m), flat(v))
    return tuple(o.reshape(shape) for o in outs)


def _place():
    x, y, c = lax.axis_index("x"), lax.axis_index("y"), lax.axis_index("c")
    chips = [(1 - x, y), (x, 1 - y), (1 - x, 1 - y)]
    return x, y, c, chips


def _chip_index(chip):
    return 2 * chip[0] + chip[1]


def _comm_call(body, name, n_in, out_shape, scratch, aliases=None):
    return pl.pallas_call(body, name=name, in_specs=[ANY] * n_in, out_specs=[ANY] * len(out_shape), out_shape=out_shape,
                          scratch_shapes=scratch, input_output_aliases=aliases or {})


def _gather_weights(slots, smalls):
    n_items, n_small = len(slots), len(smalls)

    def body(*refs):
        small_in = refs[n_items:n_items + n_small]
        outs = refs[n_items + n_small:2 * n_items + n_small]
        small_out = refs[2 * n_items + n_small:2 * n_items + 2 * n_small]
        send_sems, recv_sems, s_send, s_recv, s_local = refs[-5:]
        x, y, c, chips = _place()
        me, sibling = _chip_index((x, y)), (x, y, 1 - c)

        def landing(t, chip, core):
            half = outs[t].shape[2] // 2
            return outs[t].at[:, _chip_index(chip), pl.ds(core * half, half), :]

        def copy(t, k, block, to):
            return pltpu.make_async_remote_copy(src_ref=block, dst_ref=block, send_sem=send_sems.at[t, k],
                                                recv_sem=recv_sems.at[t, k], device_id=to, device_id_type=MESH)

        def small_copy(j, k, chip, slot):
            return pltpu.make_async_remote_copy(src_ref=small_in[j], dst_ref=small_out[j].at[slot],
                                                send_sem=s_send.at[j, k], recv_sem=s_recv.at[j, k],
                                                device_id=(*chip, c), device_id_type=MESH)

        started, local = [], []
        for t in range(n_items):
            for k, chip in enumerate(chips):
                cp = copy(t, k, landing(t, (x, y), c), (*chip, c))
                cp.start()
                started.append(cp)
        for j in range(n_small):
            own = pltpu.make_async_copy(small_in[j], small_out[j].at[me], s_local.at[j])
            own.start()
            local.append(own)
            for k, chip in enumerate(chips):
                cp = small_copy(j, k, chip, me)
                cp.start()
                started.append(cp)
        for t in range(n_items):
            for k, chip in enumerate(chips):
                copy(t, k, landing(t, chip, c), (x, y, c)).wait_recv()
                fwd = copy(t, 3 + k, landing(t, chip, c), sibling)
                fwd.start()
                started.append(fwd)
        for t in range(n_items):
            for k, chip in enumerate(chips):
                copy(t, 3 + k, landing(t, chip, 1 - c), sibling).wait_recv()
        for j in range(n_small):
            for k, chip in enumerate(chips):
                small_copy(j, k, chip, _chip_index(chip)).wait_recv()
        for cp in started:
            cp.wait_send()
        for own in local:
            own.wait()

    out_shape = [jax.ShapeDtypeStruct(s.shape, BF16) for s in slots]
    out_shape += [jax.ShapeDtypeStruct((N_CHIPS,) + s.shape, F32) for s in smalls]
    dma = pltpu.SemaphoreType.DMA
    res = _comm_call(body, "gather_weights", n_items + n_small, out_shape,
                     [dma((n_items, 6)), dma((n_items, 6)), dma((n_small, 3)), dma((n_small, 3)), dma((n_small,))],
                     aliases={t: t for t in range(n_items)})(*slots, *smalls)
    return list(res[:n_items]), list(res[n_items:])


def _exchange_halves(grads):
    n = len(grads)

    def body(*refs):
        g_in, outs = refs[:n], refs[n:2 * n]
        send_sems, recv_sems = refs[-2:]
        x, y, c, _ = _place()
        copies = []
        for t in range(n):
            half = g_in[t].shape[1] // 2
            cp = pltpu.make_async_remote_copy(
                src_ref=g_in[t].at[:, pl.ds((1 - c) * half, half), :], dst_ref=outs[t], send_sem=send_sems.at[t],
                recv_sem=recv_sems.at[t], device_id=(x, y, 1 - c), device_id_type=MESH)
            cp.start()
            copies.append(cp)
        for cp in copies:
            cp.wait()

    out_shape = [jax.ShapeDtypeStruct((g.shape[0], g.shape[1] // 2, g.shape[2]), BF16) for g in grads]
    dma = pltpu.SemaphoreType.DMA
    return list(_comm_call(body, "grad_exchange_halves", n, out_shape, [dma((n,)), dma((n,))])(*grads))


def _scatter_partials(parts):
    n = len(parts)

    def body(*refs):
        p_in, outs = refs[:n], refs[n:2 * n]
        send_sems, recv_sems = refs[-2:]
        x, y, c, chips = _place()
        copies = []
        for t in range(n):
            for k, chip in enumerate(chips):
                cp = pltpu.make_async_remote_copy(
                    src_ref=p_in[t].at[_chip_index(chip)], dst_ref=outs[t].at[k], send_sem=send_sems.at[t, k],
                    recv_sem=recv_sems.at[t, k], device_id=(*chip, c), device_id_type=MESH)
                cp.start()
                copies.append(cp)
        for cp in copies:
            cp.wait()

    out_shape = [jax.ShapeDtypeStruct((3,) + p.shape[1:], BF16) for p in parts]
    dma = pltpu.SemaphoreType.DMA
    return list(_comm_call(body, "grad_scatter_partials", n, out_shape, [dma((n, 3)), dma((n, 3))])(*parts))


def _share_halves(fulls):
    n = len(fulls)
    items = [(a, l) for a in range(n) for l in range(fulls[a].shape[0])]

    def body(*refs):
        outs = refs[n:2 * n]
        send_sems, recv_sems = refs[-2:]
        x, y, c, _ = _place()

        def copy(t, core):
            a, l = items[t]
            half = outs[a].shape[1] // 2
            block = outs[a].at[l, pl.ds(core * half, half), :]
            return pltpu.make_async_remote_copy(src_ref=block, dst_ref=block, send_sem=send_sems.at[t],
                                                recv_sem=recv_sems.at[t], device_id=(x, y, 1 - c), device_id_type=MESH)

        for t in range(len(items)):
            copy(t, c).start()
        for t in range(len(items)):
            copy(t, 1 - c).wait_recv()
        for t in range(len(items)):
            copy(t, c).wait_send()

    out_shape = [jax.ShapeDtypeStruct(f.shape, F32) for f in fulls]
    dma = pltpu.SemaphoreType.DMA
    return list(_comm_call(body, "grad_share_halves", n, out_shape, [dma((len(items),)), dma((len(items),))],
                           aliases={a: a for a in range(n)})(*fulls))


def _allgather_small(packed):
    def body(p_ref, o_ref, send_sems, recv_sems, local_sem):
        x, y, c, _ = _place()
        me = 4 * x + 2 * y + c
        own = pltpu.make_async_copy(p_ref, o_ref.at[me], local_sem)
        own.start()
        flips = [(fx, fy, fc) for fx in (0, 1) for fy in (0, 1) for fc in (0, 1)][1:]
        peers = [(x ^ fx, y ^ fy, c ^ fc) for fx, fy, fc in flips]
        copies = []
        for k, peer in enumerate(peers):
            cp = pltpu.make_async_remote_copy(src_ref=p_ref, dst_ref=o_ref.at[me], send_sem=send_sems.at[k],
                                              recv_sem=recv_sems.at[k], device_id=peer, device_id_type=MESH)
            cp.start()
            copies.append(cp)
        for k, (px, py, pc) in enumerate(peers):
            pltpu.make_async_remote_copy(src_ref=p_ref, dst_ref=o_ref.at[4 * px + 2 * py + pc], send_sem=send_sems.at[k],
                                         recv_sem=recv_sems.at[k], device_id=peers[k], device_id_type=MESH).wait_recv()
        for cp in copies:
            cp.wait_send()
        own.wait()

    dma = pltpu.SemaphoreType.DMA
    return _comm_call(body, "small_allgather", 1, [jax.ShapeDtypeStruct((8,) + packed.shape, F32)],
                      [dma((7,)), dma((7,)), dma(())])(packed)[0]


PAD_ROWS = 8


def kernel(x, norm_a, w_in_a, w_grp_a, scale_a, w_out_a, norm_kv, w_k, w_v, norm_b, w_in_b, w_out_b, norm_f, loss_target, m_norm_a, m_w_in_a, m_w_grp_a, m_scale_a, m_w_out_a, m_norm_kv, m_w_k, m_w_v, m_norm_b, m_w_in_b, m_w_out_b, m_norm_f, v_norm_a, v_w_in_a, v_w_grp_a, v_scale_a, v_w_out_a, v_norm_kv, v_w_k, v_w_v, v_norm_b, v_w_in_b, v_w_out_b, v_norm_f):
    weights = dict(norm_a=norm_a, w_in_a=w_in_a, w_grp_a=w_grp_a, scale_a=scale_a, w_out_a=w_out_a, norm_kv=norm_kv,
                   w_k=w_k, w_v=w_v, norm_b=norm_b, w_in_b=w_in_b, w_out_b=w_out_b, norm_f=norm_f)
    moments_m = dict(norm_a=m_norm_a, w_in_a=m_w_in_a, w_grp_a=m_w_grp_a, scale_a=m_scale_a, w_out_a=m_w_out_a,
                     norm_kv=m_norm_kv, w_k=m_w_k, w_v=m_w_v, norm_b=m_norm_b, w_in_b=m_w_in_b, w_out_b=m_w_out_b,
                     norm_f=m_norm_f)
    moments_v = dict(norm_a=v_norm_a, w_in_a=v_w_in_a, w_grp_a=v_w_grp_a, scale_a=v_scale_a, w_out_a=v_w_out_a,
                     norm_kv=v_norm_kv, w_k=v_w_k, w_v=v_w_v, norm_b=v_norm_b, w_in_b=v_w_in_b, w_out_b=v_w_out_b,
                     norm_f=v_norm_f)
    names = list(weights)
    d = x.shape[-1]
    c_idx = lax.axis_index("c").astype(jnp.int32).reshape(1)
    s_me = 2 * lax.axis_index("x") + lax.axis_index("y")
    s_idx = s_me.astype(jnp.int32).reshape(1)

    def as_lbrc(name):
        a = weights[name]
        if name == "w_grp_a":
            return a
        if a.ndim == 2:
            return a.reshape(1, 1, *a.shape)
        return a.reshape(a.shape[0], 1, *a.shape[1:])

    slots, slot_of = [], {}
    for name in BIG_WEIGHTS:
        a = as_lbrc(name)
        slot_of[name] = list(range(len(slots), len(slots) + a.shape[0]))
        slots += [_cast_into_slot(f"cast_{name}{l}", a, l, s_idx) for l in range(a.shape[0])]
    slots, small_g = _gather_weights(slots, [norm_a, scale_a])
    gathered = [[slots[t] for t in slot_of[name]] for name in BIG_WEIGHTS]

    w = {}
    for name, per_layer in zip(BIG_WEIGHTS, gathered):
        if name in ("w_in_a", "w_in_b"):
            w[name] = [g[0] for g in per_layer]
        elif name == "w_grp_a":
            w[name] = [g.reshape(g.shape[0], -1, g.shape[-1]) for g in per_layer]
        else:
            w[name] = [g.reshape(-1, g.shape[-1]) for g in per_layer]
    n_a = norm_a.shape[0]
    full_small = lambda g: g.transpose(1, 0, 2).reshape(g.shape[1], -1)
    norm_a_full, scale_a_full = full_small(small_g[0]), full_small(small_g[1])
    w["norm_a"] = [norm_a_full[i] for i in range(n_a)]
    w["scale_a"] = [scale_a_full[i] for i in range(n_a)]
    w["norm_kv"] = [norm_kv]
    w["norm_b"] = [norm_b[i] for i in range(norm_b.shape[0])]
    w["norm_f"] = [norm_f]

    loss_vec, grad_x, big, small = _local_step(x[0], loss_target[0], w)

    small_order = [("norm_a", i) for i in range(n_a)] + [("scale_a", i) for i in range(n_a)] + [("norm_kv", 0)] + \
                  [("norm_b", i) for i in range(norm_b.shape[0])] + [("norm_f", 0)]
    pad = lambda vec: jnp.pad(vec, ((0, PAD_ROWS - 1), (0, 0)))
    packed = jnp.concatenate([pad(loss_vec)] + [pad(small[n][i]) for n, i in small_order], axis=0)
    totals = _sum_devices("small_sum", _allgather_small(packed))
    loss = 0.5 * jnp.sum(totals[0]) / d
    small_tot = {}
    for j, (n, i) in enumerate(small_order):
        small_tot.setdefault(n, []).append(totals[PAD_ROWS * (j + 1)])
    grads = {}
    shard_w = norm_a.shape[1]
    for n in ("norm_a", "scale_a"):
        full = jnp.stack(small_tot[n])
        grads[n] = lax.dynamic_slice_in_dim(full, s_me * shard_w, shard_w, axis=1)
    grads["norm_kv"] = small_tot["norm_kv"][0]
    grads["norm_b"] = jnp.stack(small_tot["norm_b"])
    grads["norm_f"] = small_tot["norm_f"][0]

    flat = [(name, i) for name in BIG_WEIGHTS for i in range(len(big[name]))]
    g_list = [big[name][i] for name, i in flat]
    recv1 = _exchange_halves(g_list)
    parts = [_pair_add(f"pair_add_{name}{i}", g, r, c_idx) for (name, i), g, r in zip(flat, g_list, recv1)]
    recv2 = _scatter_partials(parts)
    sc_idx = jnp.concatenate([s_idx, c_idx])
    fulls = {name: None for name in BIG_WEIGHTS}
    for (name, i), p, r in zip(flat, parts, recv2):
        fulls[name] = _final_add(f"final_add_{name}{i}", p, r, sc_idx, i, len(big[name]), into=fulls[name])
    shared = _share_halves([fulls[name] for name in BIG_WEIGHTS])
    for name, g in zip(BIG_WEIGHTS, shared):
        grads[name] = g.reshape(weights[name].shape)

    deltas, new_m, new_v = {}, {}, {}
    for n in names:
        shape = weights[n].shape
        as2d = (lambda a: a.reshape(1, -1)) if len(shape) == 1 else (lambda a: a)
        dl, mn, vn = _adamw(f"adamw_{n}", as2d(weights[n]), as2d(grads[n]), as2d(moments_m[n]), as2d(moments_v[n]))
        deltas[n], new_m[n], new_v[n] = dl.reshape(shape), mn.reshape(shape), vn.reshape(shape)

    return (loss, grad_x[None], *[grads[n] for n in names], *[deltas[n] for n in names],
            *[new_m[n] for n in names], *[new_v[n] for n in names])
```

```python
import functools
import math

import jax
import jax.numpy as jnp
from jax import lax
from jax.experimental import pallas as pl
from jax.experimental.pallas import tpu as pltpu

F32 = jnp.float32
BF16 = jnp.bfloat16

HEAD_DIM = 128
POOL_WINDOWS = (2, 4, 8, 16)
DILATED_PAIRS = ((128, 1), (512, 4), (2048, 16))
ROPE_THETA = 10000.0
RMS_EPS = 1e-6
NEG_INF = -1e30
N_CHIPS = 4

ADAM_LR = 0.001
ADAM_B1 = 0.9
ADAM_B2 = 0.999
ADAM_EPS = 1e-08
ADAM_WD = 0.01
ADAM_STEP = 10

VMEM_LIMIT_BYTES = 56 * 1024 * 1024
MESH = pl.DeviceIdType.MESH
ANY = pl.BlockSpec(memory_space=pl.ANY)


def _tile(n, pref):
    t = min(n, pref)
    assert n % t == 0, (n, pref)
    return t


def _params(sem=None):
    return pltpu.CompilerParams(dimension_semantics=sem, vmem_limit_bytes=VMEM_LIMIT_BYTES)


def _mm(name, a, b, *, grid2, nk, a_blk, a_map, b_blk, b_map, outs, dims, epi=None, epi_in=(), epi_specs=(),
        acc_shape=None, epi_scratch=(), into=None):
    n_epi, n_out = len(epi_in), len(outs)

    def body(*refs):
        a_ref, b_ref = refs[0], refs[1]
        e_refs = refs[2:2 + n_epi]
        first_out = 2 + n_epi + (0 if into is None else 1)
        o_refs = refs[first_out:first_out + n_out]
        s_refs = refs[first_out + n_out + (0 if nk == 1 else 1):]

        def contrib():
            a_val = a_ref[...]
            if a_val.ndim == 3:
                a_val = a_val.reshape(-1, a_val.shape[-1])
            return lax.dot_general(a_val, b_ref[...], (dims, ((), ())), preferred_element_type=F32)

        def finish(acc):
            if epi is None:
                o_refs[0][...] = acc.reshape(o_refs[0].shape).astype(o_refs[0].dtype)
            else:
                epi(acc, e_refs, o_refs, s_refs)

        if nk == 1:
            finish(contrib())
        else:
            acc_ref = refs[first_out + n_out]
            k = pl.program_id(2)

            @pl.when(k == 0)
            def _():
                acc_ref[...] = contrib()

            @pl.when(k > 0)
            def _():
                acc_ref[...] += contrib()

            @pl.when(k == nk - 1)
            def _():
                finish(acc_ref[...])

    scratch = ([] if nk == 1 else [pltpu.VMEM(acc_shape, F32)]) + list(epi_scratch)
    extra_in, extra_specs, aliases = (), (), {}
    if into is not None:
        extra_in, extra_specs, aliases = (into[0],), (ANY,), {2 + n_epi: into[1]}
    res = pl.pallas_call(
        body, name=name, grid=(grid2[0], grid2[1], nk),
        in_specs=[pl.BlockSpec(a_blk, a_map), pl.BlockSpec(b_blk, b_map), *epi_specs, *extra_specs],
        out_specs=[pl.BlockSpec(blk, imap) for _, blk, imap, _ in outs],
        out_shape=[jax.ShapeDtypeStruct(shape, dtype) for shape, _, _, dtype in outs],
        scratch_shapes=scratch, input_output_aliases=aliases,
        compiler_params=_params(("parallel", "parallel", "arbitrary")),
    )(a, b, *epi_in, *extra_in)
    return res[0] if n_out == 1 else tuple(res)


NN = ((1,), (0,))
NT = ((1,), (1,))
TN = ((0,), (0,))


def _rope_apply(t, cos, sin):
    return t * cos + pltpu.roll(t, HEAD_DIM // 2, 1) * sin


def _epi_add(acc, e_refs, o_refs, s_refs):
    o_refs[0][...] = (acc + e_refs[0][...]).astype(o_refs[0].dtype)


def _col_blocks(width):
    return [slice(c * HEAD_DIM, (c + 1) * HEAD_DIM) for c in range(width // HEAD_DIM)]


def _col_scratch(rows, width):
    return pltpu.VMEM((width // HEAD_DIM, rows, HEAD_DIM), F32)


def _to_residue_major(o_ref, scr, d, sl):
    if d == 1:
        o_ref[0, :, sl] = scr[...].astype(o_ref.dtype)
        return
    rows = scr.shape[0] // d
    for r in range(d):
        o_ref[r, :, sl] = scr[pl.ds(r, rows, stride=d), :].astype(o_ref.dtype)


def _from_residue_major(i_ref, scr, d, sl):
    if d == 1:
        return i_ref[0, :, sl].astype(F32)
    rows = i_ref.shape[1]
    for r in range(d):
        scr[pl.ds(r, rows, stride=d), :] = i_ref[r, :, sl].astype(F32)
    return scr[...]


def _make_epi_orders(dils, rope_scale):
    def epi(acc, e_refs, o_refs, s_refs):
        if rope_scale is not None:
            cos = e_refs[0][...]
            sin = e_refs[1][...]
        for c, sl in enumerate(_col_blocks(acc.shape[1])):
            scr = s_refs[0].at[c]
            scr[...] = acc[:, sl] if rope_scale is None else _rope_apply(acc[:, sl], cos, sin) * rope_scale
            for o_ref, d in zip(o_refs, dils):
                _to_residue_major(o_ref, scr, d, sl)
    return epi


def _make_epi_token_order(d, has_add):
    def epi(acc, e_refs, o_refs, s_refs):
        o_ref = o_refs[0]
        if d == 1:
            o_ref[...] = acc + e_refs[0][...] if has_add else acc
            return
        rows = acc.shape[0] // d
        for c, sl in enumerate(_col_blocks(acc.shape[1])):
            scr = s_refs[0].at[c]
            for r in range(d):
                scr[pl.ds(r, rows, stride=d), :] = acc[r * rows:(r + 1) * rows, sl]
            o_ref[:, sl] = scr[...] + e_refs[0][:, sl] if has_add else scr[...]
    return epi


def _mm_act_w(name, a, w, *, out_dtype=BF16, add=None, rope=None, n_first=0, n_cols=None, dils=None):
    s_len, k_len = a.shape
    bm = _tile(s_len, 1024)
    epi, epi_in, epi_specs, epi_scratch = None, (), (), ()
    if w.ndim == 3:
        ns, _, c = w.shape
        ns_used = ns if n_cols is None else n_cols
        bn = _tile(c, 1024)
        sub = c // bn
        grid2 = (ns_used * sub, s_len // bm)
        b_blk, b_map = (None, k_len, bn), (lambda j, i, k: (j // sub + n_first, 0, j % sub))
        n_len = ns_used * c
    else:
        n_len = w.shape[1]
        bn = _tile(n_len, 1024)
        grid2 = (n_len // bn, s_len // bm)
        b_blk, b_map = (k_len, bn), (lambda j, i, k: (0, j))
    if add is not None:
        epi, epi_in = _epi_add, (add,)
        epi_specs = (pl.BlockSpec((bm, bn), lambda j, i, k: (i, j)),)
    outs = [((s_len, n_len), (bm, bn), lambda j, i, k: (i, j), out_dtype)]
    if dils is not None:
        if rope is not None:
            epi_in = rope[:2]
            epi_specs = (pl.BlockSpec((bm, HEAD_DIM), lambda j, i, k: (i, 0)),) * 2
        epi = _make_epi_orders(dils, None if rope is None else rope[2])
        epi_scratch = (_col_scratch(bm, bn),)
        outs = [((d, s_len // d, n_len), (d, bm // d, bn), lambda j, i, k: (0, i, j), BF16) for d in dils]
    res = _mm(name, a, w, grid2=grid2, nk=1, a_blk=(bm, k_len), a_map=lambda j, i, k: (i, 0),
              b_blk=b_blk, b_map=b_map, outs=outs, dims=NN, epi=epi, epi_in=epi_in, epi_specs=epi_specs,
              epi_scratch=epi_scratch)
    return (res,) if dils is not None and len(dils) == 1 else res


def _mm_grad_act(name, dy, w, *, add=None, slot=None):
    if slot is not None:
        d = 1 if dy.ndim == 2 else dy.shape[0]
        s_len = dy.shape[-2] * d
        _, k_len, c = w.shape
        bm, bn = _tile(s_len, 1024), _tile(k_len, 1024)
        a_blk, a_map = ((bm, c), lambda j, i, k: (i, 0)) if dy.ndim == 2 else ((d, bm // d, c), lambda j, i, k: (0, i, 0))
        epi_in = () if add is None else (add,)
        return _mm(name, dy, w, grid2=(k_len // bn, s_len // bm), nk=1, a_blk=a_blk, a_map=a_map,
                   b_blk=(None, bn, c), b_map=lambda j, i, k: (slot, j, 0),
                   outs=[((s_len, k_len), (bm, bn), lambda j, i, k: (i, j), F32)], dims=NT,
                   epi=_make_epi_token_order(d, add is not None), epi_in=epi_in,
                   epi_specs=(pl.BlockSpec((bm, bn), lambda j, i, k: (i, j)),) * len(epi_in),
                   epi_scratch=(_col_scratch(bm, bn),) if d > 1 else ())
    s_len, n_len = dy.shape
    bm = _tile(s_len, 1024)
    if w.ndim == 3:
        ns, k_len, c = w.shape
        bk, nk = c, ns
        bn = _tile(k_len, 1024)
        b_blk, b_map = (None, bn, c), (lambda j, i, k: (k, j, 0))
    else:
        k_len = w.shape[0]
        bk = _tile(n_len, 1024)
        nk = n_len // bk
        bn = _tile(k_len, 1024)
        b_blk, b_map = (bn, bk), (lambda j, i, k: (j, k))
    epi, epi_in, epi_specs = None, (), ()
    if add is not None:
        epi, epi_in = _epi_add, (add,)
        epi_specs = (pl.BlockSpec((bm, bn), lambda j, i, k: (i, j)),)
    return _mm(name, dy, w, grid2=(k_len // bn, s_len // bm), nk=nk, a_blk=(bm, bk), a_map=lambda j, i, k: (i, k),
               b_blk=b_blk, b_map=b_map, outs=[((s_len, k_len), (bm, bn), lambda j, i, k: (i, j), F32)],
               dims=NT, epi=epi, epi_in=epi_in, epi_specs=epi_specs, acc_shape=(bm, bn))


def _mm_grad_w(name, a, dy, *, col_shards=None, slot=None, into=None):
    s_len, k_len = a.shape
    n_len = dy.shape[1]
    bk = _tile(s_len, 1024)
    bm = _tile(k_len, 1024)
    if slot is not None:
        bn = _tile(n_len, 1024)
        out = ((col_shards, k_len, n_len), (None, bm, bn), lambda j, i, k: (slot, i, j), BF16)
    elif col_shards:
        c = n_len // col_shards
        bn = _tile(c, 1024)
        sub = c // bn
        out = ((col_shards, k_len, c), (None, bm, bn), lambda j, i, k: (j // sub, i, j % sub), BF16)
    else:
        bn = _tile(n_len, 1024)
        out = ((k_len, n_len), (bm, bn), lambda j, i, k: (i, j), BF16)
    return _mm(name, a, dy, grid2=(n_len // bn, k_len // bm), nk=s_len // bk,
               a_blk=(bk, bm), a_map=lambda j, i, k: (k, i), b_blk=(bk, bn), b_map=lambda j, i, k: (k, j),
               outs=[out], dims=TN, acc_shape=(bm, bn), into=None if into is None else (into, 0))


def _mm_grp_fwd(name, pooled, wg):
    s_len, e = pooled.shape
    ng, g, _ = wg.shape
    bm = _tile(s_len, 1024)
    return _mm(name, pooled, wg, grid2=(ng, s_len // bm), nk=1, a_blk=(bm, g), a_map=lambda j, i, k: (i, j),
               b_blk=(None, g, g), b_map=lambda j, i, k: (j, 0, 0),
               outs=[((s_len, e), (bm, g), lambda j, i, k: (i, j), F32)], dims=NN)


def _mm_grp_grad_act(name, dy, wg):
    s_len, e = dy.shape
    ng, g, _ = wg.shape
    bm = _tile(s_len, 1024)
    return _mm(name, dy, wg, grid2=(ng, s_len // bm), nk=1, a_blk=(bm, g), a_map=lambda j, i, k: (i, j),
               b_blk=(None, g, g), b_map=lambda j, i, k: (j, 0, 0),
               outs=[((s_len, e), (bm, g), lambda j, i, k: (i, j), F32)], dims=NT)


def _mm_grp_grad_w(name, pooled, dy, ng):
    s_len, e = pooled.shape
    g = e // ng
    bk = _tile(s_len, 1024)
    return _mm(name, pooled, dy, grid2=(ng, 1), nk=s_len // bk, a_blk=(bk, g), a_map=lambda j, i, k: (k, j),
               b_blk=(bk, g), b_map=lambda j, i, k: (k, j),
               outs=[((N_CHIPS, ng, g // N_CHIPS, g), (N_CHIPS, None, g // N_CHIPS, g), lambda j, i, k: (0, j, 0, 0), BF16)],
               dims=TN, acc_shape=(g, g))


def _row_spec(bs, width, col=0):
    return pl.BlockSpec((bs, width), lambda i: (i, col))


def _vec_spec(width):
    return pl.BlockSpec((1, width), lambda i: (0, 0))


def _rows_call(body, name, s_len, in_specs, out_specs, out_shape, bs, aliases=None, sequential=False):
    return pl.pallas_call(
        body, name=name, grid=(s_len // bs,), in_specs=in_specs, out_specs=out_specs, out_shape=out_shape,
        input_output_aliases=aliases or {},
        compiler_params=_params(("arbitrary",) if sequential else ("parallel",)))


def _accumulate(ref, part):
    i = pl.program_id(0)

    @pl.when(i == 0)
    def _():
        ref[...] = part

    @pl.when(i > 0)
    def _():
        ref[...] += part


def _rms_scale(xf):
    return lax.rsqrt(jnp.mean(xf * xf, axis=-1, keepdims=True) + RMS_EPS)


def _res_spec(dil, bs, width):
    return pl.BlockSpec((dil, bs // dil, width), lambda i: (0, i, 0))


def _res_shape(dil, s_len, width, dtype):
    return jax.ShapeDtypeStruct((dil, s_len // dil, width), dtype)


def _rmsnorm_fwd(name, x, gain, dils=()):
    s_len, d = x.shape
    bs = _tile(s_len, 256)

    def body(x_ref, g_ref, h_ref, *rest):
        xf = x_ref[...]
        h = (xf * _rms_scale(xf)) * g_ref[...]
        h_ref[...] = h.astype(BF16)
        if dils:
            for c, sl in enumerate(_col_blocks(d)):
                scr = rest[-1].at[c]
                scr[...] = h[:, sl]
                for o_ref, dil in zip(rest[:-1], dils):
                    _to_residue_major(o_ref, scr, dil, sl)

    res = pl.pallas_call(
        body, name=name, grid=(s_len // bs,), in_specs=[_row_spec(bs, d), _vec_spec(d)],
        out_specs=[_row_spec(bs, d)] + [_res_spec(dil, bs, d) for dil in dils],
        out_shape=[jax.ShapeDtypeStruct((s_len, d), BF16)] + [_res_shape(dil, s_len, d, BF16) for dil in dils],
        scratch_shapes=[_col_scratch(bs, d)] if dils else [],
        compiler_params=_params(("parallel",)))(x, gain)
    return res[0] if not dils else tuple(res)


def _rmsnorm_bwd(name, x, gain, dh, dres):
    s_len, d = x.shape
    bs = _tile(s_len, 256)

    def body(x_ref, g_ref, dh_ref, dres_ref, dx_ref, dxb_ref, dg_ref):
        xf = x_ref[...]
        r = _rms_scale(xf)
        xh = xf * r
        dh_f = dh_ref[...]
        t = dh_f * g_ref[...]
        dx = dres_ref[...] + r * (t - xh * jnp.mean(t * xh, axis=-1, keepdims=True))
        dx_ref[...] = dx
        dxb_ref[...] = dx.astype(BF16)
        _accumulate(dg_ref, jnp.sum(dh_f * xh, axis=0, keepdims=True))

    return _rows_call(
        body, name, s_len,
        [_row_spec(bs, d), _vec_spec(d), _row_spec(bs, d), _row_spec(bs, d)],
        [_row_spec(bs, d), _row_spec(bs, d), _vec_spec(d)],
        [jax.ShapeDtypeStruct((s_len, d), F32), jax.ShapeDtypeStruct((s_len, d), BF16),
         jax.ShapeDtypeStruct((1, d), F32)], bs, sequential=True)(x, gain, dh, dres)


def _loss_head(name, x, gain, target):
    s_len, d = x.shape
    bs = _tile(s_len, 256)

    def body(x_ref, g_ref, t_ref, lv_ref, dx_ref, dxb_ref, dg_ref):
        xf = x_ref[...]
        r = _rms_scale(xf)
        xh = xf * r
        err = xh * g_ref[...] - t_ref[...]
        dy = err * (1.0 / d)
        t = dy * g_ref[...]
        dx = r * (t - xh * jnp.mean(t * xh, axis=-1, keepdims=True))
        dx_ref[...] = dx
        dxb_ref[...] = dx.astype(BF16)
        _accumulate(lv_ref, jnp.sum(err * err, axis=0, keepdims=True))
        _accumulate(dg_ref, jnp.sum(dy * xh, axis=0, keepdims=True))

    return _rows_call(
        body, name, s_len, [_row_spec(bs, d), _vec_spec(d), _row_spec(bs, d)],
        [_vec_spec(d), _row_spec(bs, d), _row_spec(bs, d), _vec_spec(d)],
        [jax.ShapeDtypeStruct((1, d), F32), jax.ShapeDtypeStruct((s_len, d), F32),
         jax.ShapeDtypeStruct((s_len, d), BF16), jax.ShapeDtypeStruct((1, d), F32)],
        bs, sequential=True)(x, gain, target)


def _sigmoid(g):
    return 1.0 / (1.0 + jnp.exp(-g))


def _gate_a_fwd(name, ypre, proj, scale):
    s_len, e = ypre.shape
    bs = _tile(s_len, 256)

    def body(y_ref, g_ref, sc_ref, z_ref):
        g = g_ref[...]
        z_ref[...] = (y_ref[...] * sc_ref[...] * (g * _sigmoid(g))).astype(BF16)

    return _rows_call(body, name, s_len, [_row_spec(bs, e), _row_spec(bs, e, 1), _vec_spec(e)], _row_spec(bs, e),
                      jax.ShapeDtypeStruct((s_len, e), BF16), bs)(ypre, proj, scale)


def _gate_a_bwd(name, dz, ypre, proj, scale):
    s_len, e = ypre.shape
    bs = _tile(s_len, 256)

    def body(dz_ref, y_ref, g_ref, sc_ref, dy_ref, dproj_ref, dsc_ref):
        g = g_ref[...]
        sg = _sigmoid(g)
        silu = g * sg
        dz_f = dz_ref[...]
        ypre_f = y_ref[...]
        dys = dz_f * silu
        dy_ref[...] = (dys * sc_ref[...]).astype(BF16)
        dproj_ref[...] = (dz_f * (ypre_f * sc_ref[...]) * (sg * (1.0 + g * (1.0 - sg)))).astype(BF16)
        _accumulate(dsc_ref, jnp.sum(dys * ypre_f, axis=0, keepdims=True))

    return _rows_call(
        body, name, s_len, [_row_spec(bs, e), _row_spec(bs, e), _row_spec(bs, e, 1), _vec_spec(e)],
        [_row_spec(bs, e), _row_spec(bs, e, 1), _vec_spec(e)],
        [jax.ShapeDtypeStruct((s_len, e), BF16), jax.ShapeDtypeStruct((s_len, 2 * e), BF16),
         jax.ShapeDtypeStruct((1, e), F32)], bs, sequential=True)(dz, ypre, proj, scale)


def _merge_gate_fwd(name, outs, lses, gate, dils):
    s_len, e = gate.shape
    bs = _tile(s_len, 256)
    n = len(outs)

    def body(*refs):
        o_refs, l_refs, g_ref = refs[:n], refs[n:2 * n], refs[2 * n]
        m_ref, lj_ref, z_ref = refs[2 * n + 1:2 * n + 4]
        scratch = refs[2 * n + 4]
        for c, sl in enumerate(_col_blocks(e)):
            ls = [_from_residue_major(r, scratch.at[2 * j, c], dil, sl) for j, (r, dil) in enumerate(zip(l_refs, dils))]
            os_ = [_from_residue_major(r, scratch.at[2 * j + 1, c], dil, sl) for j, (r, dil) in enumerate(zip(o_refs, dils))]
            mx = functools.reduce(jnp.maximum, ls)
            ws = [jnp.exp(l - mx) for l in ls]
            den = functools.reduce(lambda a, b: a + b, ws)
            merged = functools.reduce(lambda a, b: a + b, [w * o for w, o in zip(ws, os_)]) / den
            g = g_ref[:, sl]
            m_ref[:, sl] = merged.astype(BF16)
            lj_ref[:, sl] = mx + jnp.log(den)
            z_ref[:, sl] = (merged * (g * _sigmoid(g))).astype(BF16)

    spec = _row_spec(bs, e)
    res_specs = [_res_spec(dil, bs, e) for dil in dils]
    return pl.pallas_call(
        body, name=name, grid=(s_len // bs,), in_specs=res_specs + res_specs + [spec], out_specs=[spec] * 3,
        out_shape=[jax.ShapeDtypeStruct((s_len, e), BF16), jax.ShapeDtypeStruct((s_len, e), F32),
                   jax.ShapeDtypeStruct((s_len, e), BF16)],
        scratch_shapes=[pltpu.VMEM((2 * n, e // HEAD_DIM, bs, HEAD_DIM), F32)],
        compiler_params=_params(("parallel",)))(*outs, *lses, gate)


def _gate_b_bwd(name, dz, merged, gate, lse, dils):
    s_len, e = gate.shape
    bs = _tile(s_len, 256)
    n = len(dils)

    def body(dz_ref, m_ref, g_ref, l_ref, dg_ref, *rest):
        out_refs, scratch = rest[:3 * n], rest[3 * n]
        for c, sl in enumerate(_col_blocks(e)):
            g = g_ref[:, sl]
            sg = _sigmoid(g)
            dz_f = dz_ref[:, sl]
            merged = m_ref[:, sl].astype(F32)
            dmerged = dz_f * (g * sg)
            dg_ref[:, sl] = (dz_f * merged * (sg * (1.0 + g * (1.0 - sg)))).astype(BF16)
            values = (dmerged, l_ref[:, sl],
                      jnp.broadcast_to(jnp.sum(dmerged * merged, axis=-1, keepdims=True), (bs, HEAD_DIM)))
            for t, val in enumerate(values):
                scr = scratch.at[t, c]
                scr[...] = val
                for j, dil in enumerate(dils):
                    _to_residue_major(out_refs[3 * j + t], scr, dil, sl)

    spec = _row_spec(bs, e)
    out_specs, out_shape = [spec], [jax.ShapeDtypeStruct((s_len, e), BF16)]
    for dil in dils:
        out_specs += [_res_spec(dil, bs, e)] * 3
        out_shape += [_res_shape(dil, s_len, e, BF16), _res_shape(dil, s_len, e, F32), _res_shape(dil, s_len, e, F32)]
    res = pl.pallas_call(
        body, name=name, grid=(s_len // bs,), in_specs=[spec] * 4, out_specs=out_specs, out_shape=out_shape,
        scratch_shapes=[pltpu.VMEM((3, e // HEAD_DIM, bs, HEAD_DIM), F32)],
        compiler_params=_params(("parallel",)))(dz, merged, gate, lse)
    return res[0], [tuple(res[1 + 3 * j:4 + 3 * j]) for j in range(n)]


def _kv_grad_prep(name, dk_accs, dv_accs, dils, cos, sin_inv):
    n = len(dils)
    e = dk_accs[0].shape[-1]
    s_len = dk_accs[0].shape[0] * dk_accs[0].shape[1]
    bs = _tile(s_len, 256)

    def body(*refs):
        dk_refs, dv_refs = refs[:n], refs[n:2 * n]
        c_ref, s_ref, dkb_ref, dvb_ref, scratch = refs[2 * n:]
        cos_t, sin_t = c_ref[...], s_ref[...]
        add = lambda a, b: a + b
        for c, sl in enumerate(_col_blocks(e)):
            dk = functools.reduce(add, [_from_residue_major(r, scratch.at[j, c], dil, sl)
                                        for j, (r, dil) in enumerate(zip(dk_refs, dils))])
            dkb_ref[:, sl] = _rope_apply(dk, cos_t, sin_t).astype(BF16)
            dv = functools.reduce(add, [_from_residue_major(r, scratch.at[n + j, c], dil, sl)
                                        for j, (r, dil) in enumerate(zip(dv_refs, dils))])
            dvb_ref[:, sl] = dv.astype(BF16)

    spec, rspec = _row_spec(bs, e), _row_spec(bs, HEAD_DIM)
    res_specs = [_res_spec(dil, bs, e) for dil in dils]
    return pl.pallas_call(
        body, name=name, grid=(s_len // bs,), in_specs=res_specs + res_specs + [rspec, rspec], out_specs=[spec, spec],
        out_shape=[jax.ShapeDtypeStruct((s_len, e), BF16)] * 2,
        scratch_shapes=[pltpu.VMEM((2 * n, e // HEAD_DIM, bs, HEAD_DIM), F32)],
        compiler_params=_params(("parallel",)))(*dk_accs, *dv_accs, cos, sin_inv)


def _pool_cols(e):
    return _tile(e // len(POOL_WINDOWS), 256)


def _window_sum(val, grp, s_len, forward):
    rows = lax.broadcasted_iota(jnp.int32, val.shape, 0)
    acc = val
    for level in range(len(POOL_WINDOWS)):
        step = 1 << level
        if forward:
            shifted = jnp.where(rows >= step, pltpu.roll(acc, step, 0), 0.0)
        else:
            shifted = jnp.where(rows < s_len - step, pltpu.roll(acc, s_len - step, 0), 0.0)
        acc = jnp.where(level <= grp, acc + shifted, acc)
    return acc


def _window_count(shape, grp):
    rows = lax.broadcasted_iota(jnp.int32, shape, 0)
    return jnp.minimum(rows + 1, jnp.left_shift(2, grp)).astype(F32)


def _pool_fwd(name, proj):
    s_len, e2 = proj.shape
    e = e2 // 2
    cb = _pool_cols(e)
    per_grp = e // len(POOL_WINDOWS) // cb
    assert POOL_WINDOWS == tuple(2 << g for g in range(len(POOL_WINDOWS)))

    def body(u_ref, p_ref):
        grp = pl.program_id(0)
        u = u_ref[...]
        total = _window_sum(u, grp, s_len, True)
        p_ref[...] = (total / _window_count(u.shape, grp) - u).astype(BF16)

    spec = pl.BlockSpec((s_len, cb), lambda g, c: (0, g * per_grp + c))
    return pl.pallas_call(
        body, name=name, grid=(len(POOL_WINDOWS), per_grp), in_specs=[spec], out_specs=spec,
        out_shape=jax.ShapeDtypeStruct((s_len, e), BF16), compiler_params=_params(("parallel", "parallel")))(proj)


def _pool_bwd(name, dpooled, dproj):
    s_len, e = dpooled.shape
    cb = _pool_cols(e)
    per_grp = e // len(POOL_WINDOWS) // cb

    def body(dp_ref, _, du_ref):
        grp = pl.program_id(0)
        dp = dp_ref[...]
        total = _window_sum(dp / _window_count(dp.shape, grp), grp, s_len, False)
        du_ref[...] = (total - dp).astype(BF16)

    spec = pl.BlockSpec((s_len, cb), lambda g, c: (0, g * per_grp + c))
    return pl.pallas_call(
        body, name=name, grid=(len(POOL_WINDOWS), per_grp), in_specs=[spec, ANY], out_specs=spec,
        out_shape=jax.ShapeDtypeStruct(dproj.shape, BF16), input_output_aliases={1: 0},
        compiler_params=_params(("parallel", "parallel")))(dpooled, dproj)


def _band_masks(nb, first):
    row = lax.broadcasted_iota(jnp.int32, (nb, nb), 0)
    col = lax.broadcasted_iota(jnp.int32, (nb, nb), 1)
    return col >= row + jnp.where(first, 2 * nb, 0), col <= row


def _dot(a, b, dims):
    return lax.dot_general(a, b, (dims, ((), ())), preferred_element_type=F32)


def _attn_fwd(name, window, q, k, v):
    dil, m, e = k.shape
    nb = window // dil
    nblk = m // nb
    heads = e // HEAD_DIM

    def body(q_ref, kp_ref, kc_ref, vp_ref, vc_ref, o_ref, l_ref):
        mask_p, mask_c = _band_masks(nb, pl.program_id(1) == 0)
        for h in range(heads):
            sl = slice(h * HEAD_DIM, (h + 1) * HEAD_DIM)
            q = q_ref[:, sl]
            s_p = jnp.where(mask_p, _dot(q, kp_ref[:, sl], NT), NEG_INF)
            s_c = jnp.where(mask_c, _dot(q, kc_ref[:, sl], NT), NEG_INF)
            mx = jnp.maximum(jnp.max(s_p, axis=-1, keepdims=True), jnp.max(s_c, axis=-1, keepdims=True))
            p_p = jnp.exp(s_p - mx)
            p_c = jnp.exp(s_c - mx)
            den = jnp.sum(p_p, axis=-1, keepdims=True) + jnp.sum(p_c, axis=-1, keepdims=True)
            out = _dot(p_p.astype(BF16), vp_ref[:, sl], NN) + _dot(p_c.astype(BF16), vc_ref[:, sl], NN)
            o_ref[:, sl] = (out / den).astype(BF16)
            l_ref[:, sl] = jnp.broadcast_to(mx + jnp.log(den), (nb, HEAD_DIM))

    blk = (None, nb, e)
    prev = lambda r, n: (r, jnp.maximum(n - 1, 0), 0)
    cur = lambda r, n: (r, n, 0)
    return pl.pallas_call(
        body, name=name, grid=(dil, nblk),
        in_specs=[pl.BlockSpec(blk, cur), pl.BlockSpec(blk, prev), pl.BlockSpec(blk, cur), pl.BlockSpec(blk, prev),
                  pl.BlockSpec(blk, cur)],
        out_specs=[pl.BlockSpec(blk, cur), pl.BlockSpec(blk, cur)],
        out_shape=[jax.ShapeDtypeStruct((dil, m, e), BF16), jax.ShapeDtypeStruct((dil, m, e), F32)],
        compiler_params=_params(("parallel", "arbitrary")),
    )(q, k, k, v, v)


def _attn_bwd(name, window, scale, q, k, v, dout, lse, delta, cos, sin_inv, dk_acc, dv_acc):
    dil, m, e = k.shape
    nb = window // dil
    nblk = m // nb
    heads = e // HEAD_DIM

    def body(q_ref, kp_ref, kc_ref, vp_ref, vc_ref, do_ref, l_ref, dl_ref, c_ref, s_ref, dki_ref, dvi_ref,
             dq_ref, dko_ref, dvo_ref, ck_ref, cv_ref):
        n = pl.program_id(1)

        @pl.when(n == 0)
        def _():
            ck_ref[...] = jnp.zeros_like(ck_ref)
            cv_ref[...] = jnp.zeros_like(cv_ref)

        @pl.when(n < nblk)
        def _():
            mask_p, mask_c = _band_masks(nb, n == 0)
            cos_t, sin_t = c_ref[...], s_ref[...]
            for h in range(heads):
                sl = slice(h * HEAD_DIM, (h + 1) * HEAD_DIM)
                q, kp, kc, vp, vc = q_ref[:, sl], kp_ref[:, sl], kc_ref[:, sl], vp_ref[:, sl], vc_ref[:, sl]
                do = do_ref[:, sl]
                lj = l_ref[:, sl] if nb == HEAD_DIM else l_ref[:, sl][:, :1]
                delta = dl_ref[:, sl] if nb == HEAD_DIM else dl_ref[:, sl][:, :1]
                p_p = jnp.where(mask_p, jnp.exp(_dot(q, kp, NT) - lj), 0.0)
                p_c = jnp.where(mask_c, jnp.exp(_dot(q, kc, NT) - lj), 0.0)
                ds_p = (p_p * (_dot(do, vp, NT) - delta)).astype(BF16)
                ds_c = (p_c * (_dot(do, vc, NT) - delta)).astype(BF16)
                dq = (_dot(ds_p, kp, NN) + _dot(ds_c, kc, NN)) * scale
                dq_ref[:, sl] = _rope_apply(dq, cos_t, sin_t).astype(BF16)
                dko_ref[:, sl] = dki_ref[:, sl] + ck_ref[:, sl] + _dot(ds_p, q, TN)
                dvo_ref[:, sl] = dvi_ref[:, sl] + cv_ref[:, sl] + _dot(p_p.astype(BF16), do, TN)
                ck_ref[:, sl] = _dot(ds_c, q, TN)
                cv_ref[:, sl] = _dot(p_c.astype(BF16), do, TN)

        @pl.when(n == nblk)
        def _():
            dko_ref[...] = dki_ref[...] + ck_ref[...]
            dvo_ref[...] = dvi_ref[...] + cv_ref[...]

    blk = (None, nb, e)
    qn = lambda n: jnp.minimum(n, nblk - 1)
    cur = lambda r, n: (r, qn(n), 0)
    prev = lambda r, n: (r, jnp.maximum(qn(n) - 1, 0), 0)
    kprev = lambda r, n: (r, jnp.maximum(n - 1, 0), 0)
    rblk = (None, nb, HEAD_DIM)
    return pl.pallas_call(
        body, name=name, grid=(dil, nblk + 1),
        in_specs=[pl.BlockSpec(blk, cur),
                  pl.BlockSpec(blk, prev), pl.BlockSpec(blk, cur), pl.BlockSpec(blk, prev), pl.BlockSpec(blk, cur),
                  pl.BlockSpec(blk, cur), pl.BlockSpec(blk, cur), pl.BlockSpec(blk, cur),
                  pl.BlockSpec(rblk, cur), pl.BlockSpec(rblk, cur),
                  pl.BlockSpec(blk, kprev), pl.BlockSpec(blk, kprev)],
        out_specs=[pl.BlockSpec(blk, cur), pl.BlockSpec(blk, kprev), pl.BlockSpec(blk, kprev)],
        out_shape=[jax.ShapeDtypeStruct((dil, m, e), BF16),
                   jax.ShapeDtypeStruct((dil, m, e), F32), jax.ShapeDtypeStruct((dil, m, e), F32)],
        scratch_shapes=[pltpu.VMEM((nb, e), F32), pltpu.VMEM((nb, e), F32)],
        input_output_aliases={10: 1, 11: 2},
        compiler_params=_params(("parallel", "arbitrary")),
    )(q, k, k, v, v, dout, lse, delta, cos, sin_inv, dk_acc, dv_acc)


def _rope_tables(s_len):
    inv_freq = 1.0 / (ROPE_THETA ** (jnp.arange(0, HEAD_DIM, 2, dtype=F32) / HEAD_DIM))
    ang = jnp.arange(s_len, dtype=F32)[:, None] * inv_freq[None, :]
    cos, sin = jnp.cos(ang), jnp.sin(ang)
    return jnp.concatenate([cos, cos], axis=1), jnp.concatenate([-sin, sin], axis=1)


def _row(vec):
    return vec.reshape(1, -1)


def _local_step(x, target, n_a, n_b, fetch, emit):
    s_len, d = x.shape
    n_q = len(DILATED_PAIRS)
    cos, sin = _rope_tables(s_len)
    sin_inv = -sin
    q_scale = 1.0 / math.sqrt(HEAD_DIM)
    w = {}

    def need(group, after):
        for name, (layer, arr) in fetch(group, after).items():
            w.setdefault(name, {})[layer] = arr

    saved_a = []
    for i in range(n_a):
        need(f"a{i}", x)
        h = _rmsnorm_fwd(f"a{i}_norm", x, _row(w["norm_a"][i]))
        proj = _mm_act_w(f"a{i}_in", h, w["w_in_a"][i], out_dtype=F32)
        pooled = _pool_fwd(f"a{i}_pool", proj)
        ypre = _mm_grp_fwd(f"a{i}_grp", pooled, w["w_grp_a"][i])
        z = _gate_a_fwd(f"a{i}_gate", ypre, proj, _row(w["scale_a"][i]))
        x_next = _mm_act_w(f"a{i}_out", z, w["w_out_a"][i], out_dtype=F32, add=x)
        saved_a.append((x, h, proj, pooled, ypre, z))
        x = x_next

    x_kv = x
    need("kv", x)
    e = w["w_k"][0].shape[1]
    kv_in = _rmsnorm_fwd("kv_norm", x, _row(w["norm_kv"][0]))
    windows = [window for window, _ in DILATED_PAIRS]
    dils = tuple(dil for _, dil in DILATED_PAIRS)
    far_dils = tuple(dil for dil in dils if dil > 1)
    ks = _mm_act_w("kv_k", kv_in, w["w_k"][0], rope=(cos, sin, 1.0), dils=dils)
    vs = _mm_act_w("kv_v", kv_in, w["w_v"][0], dils=dils)

    saved_b = []
    for i in range(n_b):
        need(f"b{i}", x)
        hs = _rmsnorm_fwd(f"b{i}_norm", x, _row(w["norm_b"][i]), dils=far_dils)
        hs = {1: hs[0], **{dil: h_d.reshape(s_len, d) for dil, h_d in zip(far_dils, hs[1:])}}
        qs = [_mm_act_w(f"b{i}_q{g}", hs[1], w["w_in_b"][i], rope=(cos, sin, q_scale), n_first=g, n_cols=1,
                        dils=(dil,))[0] for g, dil in enumerate(dils)]
        gate = _mm_act_w(f"b{i}_g", hs[1], w["w_in_b"][i], out_dtype=F32, n_first=n_q, n_cols=1)
        outs, lses = [], []
        for g in range(n_q):
            o_g, l_g = _attn_fwd(f"b{i}_attn{g}", windows[g], qs[g], ks[g], vs[g])
            outs.append(o_g)
            lses.append(l_g)
        merged, lse, z = _merge_gate_fwd(f"b{i}_merge", outs, lses, gate, dils)
        x_next = _mm_act_w(f"b{i}_out", z, w["w_out_b"][i], out_dtype=F32, add=x)
        saved_b.append((x, hs, qs, gate, merged, lse, z))
        x = x_next

    need("head", x)
    loss_vec, dx, dxb, g_norm_f = _loss_head("loss_head", x, _row(w["norm_f"][0]), target)

    small = {"norm_a": {}, "scale_a": {}, "norm_kv": {}, "norm_b": {}, "norm_f": {0: g_norm_f}}
    shard_rows = lambda g2: g2.reshape(N_CHIPS, g2.shape[0] // N_CHIPS, g2.shape[1])

    res_major = lambda t, dil: t.reshape(s_len // dil, dil, t.shape[1]).transpose(1, 0, 2)
    cos_r = [res_major(cos, dil) for dil in dils]
    sin_inv_r = [res_major(sin_inv, dil) for dil in dils]
    dk_accs = [jnp.zeros((dil, s_len // dil, e), F32) for dil in dils]
    dv_accs = [jnp.zeros((dil, s_len // dil, e), F32) for dil in dils]
    for i in reversed(range(n_b)):
        x_in, hs, qs, gate, merged, lse, z = saved_b[i]
        dz = _mm_grad_act(f"b{i}_dz", dxb, w["w_out_b"][i])
        g_out = shard_rows(_mm_grad_w(f"b{i}_gwo", z, dxb))
        dgate, stats = _gate_b_bwd(f"b{i}_dgate", dz, merged, gate, lse, dils)
        dh = _mm_grad_act(f"b{i}_dh{n_q}", dgate, w["w_in_b"][i], slot=n_q)
        g_in = _mm_grad_w(f"b{i}_gwi{n_q}", hs[1], dgate, col_shards=n_q + 1, slot=n_q)
        for g, dil in enumerate(dils):
            dout, lse_g, delta_g = stats[g]
            dq, dk_accs[g], dv_accs[g] = _attn_bwd(f"b{i}_dattn{g}", windows[g], q_scale, qs[g], ks[g], vs[g], dout,
                                                   lse_g, delta_g, cos_r[g], sin_inv_r[g], dk_accs[g], dv_accs[g])
            dh = _mm_grad_act(f"b{i}_dh{g}", dq if dil > 1 else dq[0], w["w_in_b"][i], add=dh, slot=g)
            g_in = _mm_grad_w(f"b{i}_gwi{g}", hs[dil], dq.reshape(s_len, e), col_shards=n_q + 1, slot=g, into=g_in)
        dx, dxb, small["norm_b"][i] = _rmsnorm_bwd(f"b{i}_dnorm", x_in, _row(w["norm_b"][i]), dh, dx)
        dxb = emit(f"b{i}", {"w_in_b": (i, g_in), "w_out_b": (i, g_out)}, dxb)

    dkb, dvb = _kv_grad_prep("kv_dprep", dk_accs, dv_accs, dils, cos, sin_inv)
    dkv = _mm_grad_act("kv_dk", dkb, w["w_k"][0])
    dkv = _mm_grad_act("kv_dv", dvb, w["w_v"][0], add=dkv)
    g_k = shard_rows(_mm_grad_w("kv_gwk", kv_in, dkb))
    g_v = shard_rows(_mm_grad_w("kv_gwv", kv_in, dvb))
    dx, dxb, small["norm_kv"][0] = _rmsnorm_bwd("kv_dnorm", x_kv, _row(w["norm_kv"][0]), dkv, dx)
    dxb = emit("kv", {"w_k": (0, g_k), "w_v": (0, g_v)}, dxb)

    for i in reversed(range(n_a)):
        x_in, h, proj, pooled, ypre, z = saved_a[i]
        dz = _mm_grad_act(f"a{i}_dz", dxb, w["w_out_a"][i])
        g_out = shard_rows(_mm_grad_w(f"a{i}_gwo", z, dxb))
        dypre, dproj, small["scale_a"][i] = _gate_a_bwd(f"a{i}_dgate", dz, ypre, proj, _row(w["scale_a"][i]))
        dpooled = _mm_grp_grad_act(f"a{i}_dgrp", dypre, w["w_grp_a"][i])
        g_grp = _mm_grp_grad_w(f"a{i}_gwg", pooled, dypre, len(POOL_WINDOWS))
        g_grp = g_grp.reshape(N_CHIPS, -1, g_grp.shape[-1])
        dproj = _pool_bwd(f"a{i}_dpool", dpooled, dproj)
        dh = _mm_grad_act(f"a{i}_dh", dproj, w["w_in_a"][i])
        g_in = _mm_grad_w(f"a{i}_gwi", h, dproj, col_shards=N_CHIPS)
        dx, dxb, small["norm_a"][i] = _rmsnorm_bwd(f"a{i}_dnorm", x_in, _row(w["norm_a"][i]), dh, dx)
        dxb = emit(f"a{i}", {"w_in_a": (i, g_in), "w_grp_a": (i, g_grp), "w_out_a": (i, g_out)}, dxb)

    return loss_vec, dx, small


BIG_WEIGHTS = ("w_in_a", "w_grp_a", "w_out_a", "w_k", "w_v", "w_in_b", "w_out_b")


def _pair_add(name, grad, recv, c_idx):
    _, r, cols = grad.shape
    half = r // 2
    rb = _tile(half, 256)
    nrb = half // rb

    def body(c_ref, g_ref, r_ref, o_ref):
        o_ref[...] = (g_ref[...].astype(F32) + r_ref[...].astype(F32)).astype(BF16)

    blk = (None, rb, cols)
    grid_spec = pltpu.PrefetchScalarGridSpec(
        num_scalar_prefetch=1, grid=(N_CHIPS, nrb),
        in_specs=[pl.BlockSpec(blk, lambda s, i, c: (s, c[0] * nrb + i, 0)), pl.BlockSpec(blk, lambda s, i, c: (s, i, 0))],
        out_specs=pl.BlockSpec(blk, lambda s, i, c: (s, i, 0)))
    return pl.pallas_call(body, name=name, grid_spec=grid_spec,
                          out_shape=jax.ShapeDtypeStruct((N_CHIPS, half, cols), BF16),
                          compiler_params=_params(("parallel", "parallel")))(c_idx, grad, recv)


def _final_add(name, part, recv, sc_idx, layer, n_layers, into=None):
    _, half, cols = part.shape
    rb = _tile(half, 256)
    nrb = half // rb
    n_peer = recv.shape[0]

    def body(sc_ref, p_ref, *refs):
        acc = p_ref[...].astype(F32)
        for r_ref in refs[:n_peer]:
            acc = acc + r_ref[...].astype(F32)
        refs[-1][...] = acc

    blk = (None, rb, cols)
    peer_spec = lambda k: pl.BlockSpec(blk, lambda i, sc: (k, i, 0))
    grid_spec = pltpu.PrefetchScalarGridSpec(
        num_scalar_prefetch=1, grid=(nrb,),
        in_specs=[pl.BlockSpec(blk, lambda i, sc: (sc[0], i, 0))] + [peer_spec(k) for k in range(n_peer)]
                 + ([] if into is None else [ANY]),
        out_specs=pl.BlockSpec(blk, lambda i, sc: (layer, sc[1] * nrb + i, 0)))
    extra = () if into is None else (into,)
    return pl.pallas_call(body, name=name, grid_spec=grid_spec,
                          out_shape=jax.ShapeDtypeStruct((n_layers, 2 * half, cols), F32),
                          input_output_aliases={} if into is None else {2 + n_peer: 0},
                          compiler_params=_params(("parallel",)))(sc_idx, part, *([recv] * n_peer), *extra)


def _cast_into_slot(name, arr, layer, s_idx):
    _, b, r, cols = arr.shape
    rb = _tile(r, 512)

    def body(s_ref, a_ref, o_ref):
        o_ref[...] = a_ref[...].astype(BF16)

    blk = (None, None, rb, cols)
    grid_spec = pltpu.PrefetchScalarGridSpec(
        num_scalar_prefetch=1, grid=(b, r // rb),
        in_specs=[pl.BlockSpec(blk, lambda j, i, s: (layer, j, i, 0))],
        out_specs=pl.BlockSpec(blk, lambda j, i, s: (j, s[0], i, 0)))
    return pl.pallas_call(body, name=name, grid_spec=grid_spec,
                          out_shape=jax.ShapeDtypeStruct((b, N_CHIPS, r, cols), BF16),
                          compiler_params=_params(("parallel", "parallel")))(s_idx, arr)


def _sum_devices(name, gathered):
    n_dev, p, d = gathered.shape

    def body(g_ref, o_ref):
        acc = g_ref[0]
        for j in range(1, n_dev):
            acc = acc + g_ref[j]
        o_ref[...] = acc

    return pl.pallas_call(body, name=name, out_shape=jax.ShapeDtypeStruct((p, d), F32),
                          compiler_params=_params())(gathered)


def _adamw(name, w, g, m, v):
    shape = w.shape
    cols = shape[-1]
    flat = lambda a: a.reshape(-1, cols)
    rows = flat(w).shape[0]
    bs = _tile(rows, 256)

    def body(w_ref, g_ref, m_ref, v_ref, d_ref, mo_ref, vo_ref):
        grad = g_ref[...]
        m_new = ADAM_B1 * m_ref[...] + (1.0 - ADAM_B1) * grad
        v_new = ADAM_B2 * v_ref[...] + (1.0 - ADAM_B2) * (grad * grad)
        m_hat = m_new / (1.0 - ADAM_B1 ** ADAM_STEP)
        v_hat = v_new / (1.0 - ADAM_B2 ** ADAM_STEP)
        d_ref[...] = -ADAM_LR * (m_hat / (jnp.sqrt(v_hat) + ADAM_EPS) + ADAM_WD * w_ref[...])
        mo_ref[...] = m_new
        vo_ref[...] = v_new

    spec = _row_spec(bs, cols)
    outs = _rows_call(body, name, rows, [spec] * 4, [spec] * 3, [jax.ShapeDtypeStruct((rows, cols), F32)] * 3, bs)(
        flat(w), flat(g), flat(m), flat(v))
    return tuple(o.reshape(shape) for o in outs)


def _place():
    x, y, c = lax.axis_index("x"), lax.axis_index("y"), lax.axis_index("c")
    chips = [(1 - x, y), (x, 1 - y), (1 - x, 1 - y)]
    return x, y, c, chips


def _chip_index(chip):
    return 2 * chip[0] + chip[1]


def _comm_call(body, name, n_in, out_shape, scratch, aliases=None):
    return pl.pallas_call(body, name=name, in_specs=[ANY] * n_in, out_specs=[ANY] * len(out_shape), out_shape=out_shape,
                          scratch_shapes=scratch, input_output_aliases=aliases or {})


HBM_SPEC = pl.BlockSpec(memory_space=pltpu.HBM)
SEM_SPEC = pl.BlockSpec(memory_space=pltpu.SEMAPHORE)
SPLIT_PARAMS = pltpu.CompilerParams(has_side_effects=pltpu.SideEffectType.DATAFLOW_SIDE_EFFECTING)


def _in_hbm(arr):
    return pltpu.with_memory_space_constraint(arr, pltpu.HBM)


def _slot_half(ref, chip, core):
    half = ref.shape[2] // 2
    return ref.at[:, _chip_index(chip), pl.ds(core * half, half), :]


def _gather_start(slots, groups):
    n, n_g = len(slots), len(groups)

    def body(*refs):
        ins, sems = refs[:n], refs[n:n + 2 * n_g]
        x, y, c, chips = _place()
        for gi, items in enumerate(groups):
            for a, t in enumerate(items):
                block = _slot_half(ins[t], (x, y), c)
                for k, chip in enumerate(chips):
                    pltpu.make_async_remote_copy(src_ref=block, dst_ref=block, send_sem=sems[2 * gi].at[3 * a + k],
                                                 recv_sem=sems[2 * gi + 1].at[3 * a + k], device_id=(*chip, c),
                                                 device_id_type=MESH).start()

    dma = pltpu.SemaphoreType.DMA
    sem_shapes = [dma((3 * len(items),)) for items in groups for _ in range(2)]
    res = pl.pallas_call(
        body, name="gather_start", in_specs=[HBM_SPEC] * n, out_specs=[SEM_SPEC] * (2 * n_g) + [HBM_SPEC] * n,
        out_shape=sem_shapes + [pltpu.HBM(s.shape, s.dtype) for s in slots],
        input_output_aliases={t: 2 * n_g + t for t in range(n)}, compiler_params=SPLIT_PARAMS,
    )(*[_in_hbm(s) for s in slots])
    return [(res[2 * gi], res[2 * gi + 1]) for gi in range(n_g)], list(res[2 * n_g:])


def _gather_wait(name, sems, bufs, after):
    n = len(bufs)

    def body(*refs):
        ins, (send_sems, recv_sems) = refs[:n], refs[n:n + 2]
        x, y, c, chips = _place()
        for a in range(n):
            for k, chip in enumerate(chips):
                mine, theirs = _slot_half(ins[a], (x, y), c), _slot_half(ins[a], chip, c)
                copy = pltpu.make_async_remote_copy(src_ref=mine, dst_ref=theirs, send_sem=send_sems.at[3 * a + k],
                                                    recv_sem=recv_sems.at[3 * a + k], device_id=(*chip, c),
                                                    device_id_type=MESH)
                copy.wait_send()
                copy.wait_recv()

    res = pl.pallas_call(
        body, name=name, in_specs=[HBM_SPEC] * n + [SEM_SPEC, SEM_SPEC, ANY], out_specs=[HBM_SPEC] * n,
        out_shape=[pltpu.HBM(b.shape, b.dtype) for b in bufs], input_output_aliases={a: a for a in range(n)},
        compiler_params=SPLIT_PARAMS)(*bufs, *sems, after)
    return list(res)


def _gather_forward(name, bufs, smalls=()):
    n, n_small = len(bufs), len(smalls)

    def body(*refs):
        small_in = refs[n:n + n_small]
        outs = refs[n + n_small:2 * n + n_small]
        small_out = refs[2 * n + n_small:2 * n + 2 * n_small]
        send_sems, recv_sems, s_send, s_recv, s_local = refs[-5:]
        x, y, c, chips = _place()
        me, sibling = _chip_index((x, y)), (x, y, 1 - c)

        def forward(t, k, core):
            block = _slot_half(outs[t], chips[k], core)
            return pltpu.make_async_remote_copy(src_ref=block, dst_ref=block, send_sem=send_sems.at[t, k],
                                                recv_sem=recv_sems.at[t, k], device_id=sibling, device_id_type=MESH)

        def small_copy(j, k, slot):
            return pltpu.make_async_remote_copy(src_ref=small_in[j], dst_ref=small_out[j].at[slot],
                                                send_sem=s_send.at[j, k], recv_sem=s_recv.at[j, k],
                                                device_id=(*chips[k], c), device_id_type=MESH)

        local = []
        for t in range(n):
            for k in range(3):
                forward(t, k, c).start()
        for j in range(n_small):
            own = pltpu.make_async_copy(small_in[j], small_out[j].at[me], s_local.at[j])
            own.start()
            local.append(own)
            for k in range(3):
                small_copy(j, k, me).start()
        for t in range(n):
            for k in range(3):
                forward(t, k, 1 - c).wait_recv()
        for j in range(n_small):
            for k in range(3):
                small_copy(j, k, _chip_index(chips[k])).wait_recv()
        for t in range(n):
            for k in range(3):
                forward(t, k, c).wait_send()
        for j in range(n_small):
            for k in range(3):
                small_copy(j, k, me).wait_send()
        for own in local:
            own.wait()

    out_shape = [jax.ShapeDtypeStruct(b.shape, BF16) for b in bufs]
    out_shape += [jax.ShapeDtypeStruct((N_CHIPS,) + s.shape, F32) for s in smalls]
    dma = pltpu.SemaphoreType.DMA
    n_s = max(n_small, 1)
    res = _comm_call(body, name, n + n_small, out_shape,
                     [dma((n, 3)), dma((n, 3)), dma((n_s, 3)), dma((n_s, 3)), dma((n_s,))],
                     aliases={t: t for t in range(n)})(*bufs, *smalls)
    return list(res[:n]), list(res[n:])


def _exchange_halves(name, grads):
    n = len(grads)

    def body(*refs):
        g_in, outs = refs[:n], refs[n:2 * n]
        send_sems, recv_sems = refs[-2:]
        x, y, c, _ = _place()
        copies = []
        for t in range(n):
            half = g_in[t].shape[1] // 2
            cp = pltpu.make_async_remote_copy(
                src_ref=g_in[t].at[:, pl.ds((1 - c) * half, half), :], dst_ref=outs[t], send_sem=send_sems.at[t],
                recv_sem=recv_sems.at[t], device_id=(x, y, 1 - c), device_id_type=MESH)
            cp.start()
            copies.append(cp)
        for cp in copies:
            cp.wait()

    out_shape = [jax.ShapeDtypeStruct((g.shape[0], g.shape[1] // 2, g.shape[2]), BF16) for g in grads]
    dma = pltpu.SemaphoreType.DMA
    return list(_comm_call(body, name, n, out_shape, [dma((n,)), dma((n,))])(*grads))


def _scatter_copy(part_ref, land_ref, send_sems, recv_sems, t, k, chip, c):
    return pltpu.make_async_remote_copy(
        src_ref=part_ref.at[_chip_index(chip)], dst_ref=land_ref.at[k], send_sem=send_sems.at[3 * t + k],
        recv_sem=recv_sems.at[3 * t + k], device_id=(*chip, c), device_id_type=MESH)


def _scatter_start(name, parts, carry):
    n = len(parts)
    lands = [lax.empty((3,) + p.shape[1:], BF16) for p in parts]

    def body(*refs):
        p_in, l_in = refs[:n], refs[n:2 * n]
        send_sems, recv_sems = refs[2 * n + 1:2 * n + 3]
        x, y, c, chips = _place()
        for t in range(n):
            for k, chip in enumerate(chips):
                _scatter_copy(p_in[t], l_in[t], send_sems, recv_sems, t, k, chip, c).start()

    dma = pltpu.SemaphoreType.DMA
    hbm_out = lambda a: pltpu.HBM(a.shape, a.dtype)
    res = pl.pallas_call(
        body, name=name, in_specs=[HBM_SPEC] * (2 * n + 1), out_specs=[SEM_SPEC] * 2 + [HBM_SPEC] * (2 * n + 1),
        out_shape=[dma((3 * n,)), dma((3 * n,))] + [hbm_out(a) for a in parts + lands + [carry]],
        input_output_aliases={t: 2 + t for t in range(2 * n + 1)}, compiler_params=SPLIT_PARAMS,
    )(*[_in_hbm(a) for a in parts + lands + [carry]])
    return (res[0], res[1]), list(res[2:2 + n]), list(res[2 + n:2 + 2 * n]), res[2 + 2 * n]


def _scatter_wait(name, sems, parts, lands):
    n = len(parts)

    def body(*refs):
        p_in, l_in = refs[:n], refs[n:2 * n]
        send_sems, recv_sems = refs[2 * n:2 * n + 2]
        x, y, c, chips = _place()
        for t in range(n):
            for k, chip in enumerate(chips):
                copy = _scatter_copy(p_in[t], l_in[t], send_sems, recv_sems, t, k, chip, c)
                copy.wait_send()
                copy.wait_recv()

    hbm_out = lambda a: pltpu.HBM(a.shape, a.dtype)
    res = pl.pallas_call(
        body, name=name, in_specs=[HBM_SPEC] * (2 * n) + [SEM_SPEC, SEM_SPEC], out_specs=[HBM_SPEC] * (2 * n),
        out_shape=[hbm_out(a) for a in parts + lands], input_output_aliases={t: t for t in range(2 * n)},
        compiler_params=SPLIT_PARAMS)(*parts, *lands, *sems)
    return list(res[:n]), list(res[n:])


def _share_halves(fulls):
    n = len(fulls)
    items = [(a, l) for a in range(n) for l in range(fulls[a].shape[0])]

    def body(*refs):
        outs = refs[n:2 * n]
        send_sems, recv_sems = refs[-2:]
        x, y, c, _ = _place()

        def copy(t, core):
            a, l = items[t]
            half = outs[a].shape[1] // 2
            block = outs[a].at[l, pl.ds(core * half, half), :]
            return pltpu.make_async_remote_copy(src_ref=block, dst_ref=block, send_sem=send_sems.at[t],
                                                recv_sem=recv_sems.at[t], device_id=(x, y, 1 - c), device_id_type=MESH)

        for t in range(len(items)):
            copy(t, c).start()
        for t in range(len(items)):
            copy(t, 1 - c).wait_recv()
        for t in range(len(items)):
            copy(t, c).wait_send()

    out_shape = [jax.ShapeDtypeStruct(f.shape, F32) for f in fulls]
    dma = pltpu.SemaphoreType.DMA
    return list(_comm_call(body, "grad_share_halves", n, out_shape, [dma((len(items),)), dma((len(items),))],
                           aliases={a: a for a in range(n)})(*fulls))


def _allgather_small(packed):
    def body(p_ref, o_ref, send_sems, recv_sems, local_sem):
        x, y, c, _ = _place()
        me = 4 * x + 2 * y + c
        own = pltpu.make_async_copy(p_ref, o_ref.at[me], local_sem)
        own.start()
        flips = [(fx, fy, fc) for fx in (0, 1) for fy in (0, 1) for fc in (0, 1)][1:]
        peers = [(x ^ fx, y ^ fy, c ^ fc) for fx, fy, fc in flips]
        copies = []
        for k, peer in enumerate(peers):
            cp = pltpu.make_async_remote_copy(src_ref=p_ref, dst_ref=o_ref.at[me], send_sem=send_sems.at[k],
                                              recv_sem=recv_sems.at[k], device_id=peer, device_id_type=MESH)
            cp.start()
            copies.append(cp)
        for k, (px, py, pc) in enumerate(peers):
            pltpu.make_async_remote_copy(src_ref=p_ref, dst_ref=o_ref.at[4 * px + 2 * py + pc], send_sem=send_sems.at[k],
                                         recv_sem=recv_sems.at[k], device_id=peers[k], device_id_type=MESH).wait_recv()
        for cp in copies:
            cp.wait_send()
        own.wait()

    dma = pltpu.SemaphoreType.DMA
    return _comm_call(body, "small_allgather", 1, [jax.ShapeDtypeStruct((8,) + packed.shape, F32)],
                      [dma((7,)), dma((7,)), dma(())])(packed)[0]


PAD_ROWS = 8


def kernel(x, norm_a, w_in_a, w_grp_a, scale_a, w_out_a, norm_kv, w_k, w_v, norm_b, w_in_b, w_out_b, norm_f, loss_target, m_norm_a, m_w_in_a, m_w_grp_a, m_scale_a, m_w_out_a, m_norm_kv, m_w_k, m_w_v, m_norm_b, m_w_in_b, m_w_out_b, m_norm_f, v_norm_a, v_w_in_a, v_w_grp_a, v_scale_a, v_w_out_a, v_norm_kv, v_w_k, v_w_v, v_norm_b, v_w_in_b, v_w_out_b, v_norm_f):
    weights = dict(norm_a=norm_a, w_in_a=w_in_a, w_grp_a=w_grp_a, scale_a=scale_a, w_out_a=w_out_a, norm_kv=norm_kv,
                   w_k=w_k, w_v=w_v, norm_b=norm_b, w_in_b=w_in_b, w_out_b=w_out_b, norm_f=norm_f)
    moments_m = dict(norm_a=m_norm_a, w_in_a=m_w_in_a, w_grp_a=m_w_grp_a, scale_a=m_scale_a, w_out_a=m_w_out_a,
                     norm_kv=m_norm_kv, w_k=m_w_k, w_v=m_w_v, norm_b=m_norm_b, w_in_b=m_w_in_b, w_out_b=m_w_out_b,
                     norm_f=m_norm_f)
    moments_v = dict(norm_a=v_norm_a, w_in_a=v_w_in_a, w_grp_a=v_w_grp_a, scale_a=v_scale_a, w_out_a=v_w_out_a,
                     norm_kv=v_norm_kv, w_k=v_w_k, w_v=v_w_v, norm_b=v_norm_b, w_in_b=v_w_in_b, w_out_b=v_w_out_b,
                     norm_f=v_norm_f)
    names = list(weights)
    d = x.shape[-1]
    c_idx = lax.axis_index("c").astype(jnp.int32).reshape(1)
    s_me = 2 * lax.axis_index("x") + lax.axis_index("y")
    s_idx = s_me.astype(jnp.int32).reshape(1)

    def as_lbrc(name):
        a = weights[name]
        if name == "w_grp_a":
            return a
        if a.ndim == 2:
            return a.reshape(1, 1, *a.shape)
        return a.reshape(a.shape[0], 1, *a.shape[1:])

    n_a, n_b = norm_a.shape[0], norm_b.shape[0]
    group_weights = {**{f"a{i}": [("w_in_a", i), ("w_grp_a", i), ("w_out_a", i)] for i in range(n_a)},
                     "kv": [("w_k", 0), ("w_v", 0)],
                     **{f"b{i}": [("w_in_b", i), ("w_out_b", i)] for i in range(n_b)}}
    group_order = [f"a{i}" for i in range(n_a)] + ["kv"] + [f"b{i}" for i in range(n_b)]
    slots, slot_groups = [], []
    for group in group_order:
        slot_groups.append(list(range(len(slots), len(slots) + len(group_weights[group]))))
        slots += [_cast_into_slot(f"cast_{name}{l}", as_lbrc(name), l, s_idx) for name, l in group_weights[group]]
    group_sems, slots = _gather_start(slots, slot_groups)
    small_full = {}

    def gathered_form(name, g):
        if name in ("w_in_a", "w_in_b"):
            return g[0]
        if name == "w_grp_a":
            return g.reshape(g.shape[0], -1, g.shape[-1])
        return g.reshape(-1, g.shape[-1])

    def fetch(group, after):
        if group == "head":
            return {"norm_f": (0, norm_f)}
        gi = group_order.index(group)
        bufs = _gather_wait(f"gather_wait_{group}", group_sems[gi], [slots[t] for t in slot_groups[gi]], after)
        bufs, small_g = _gather_forward(f"gather_forward_{group}", bufs, [norm_a, scale_a] if gi == 0 else [])
        out = {name: (l, gathered_form(name, g)) for (name, l), g in zip(group_weights[group], bufs)}
        if gi == 0:
            for name, g in zip(("norm_a", "scale_a"), small_g):
                small_full[name] = g.transpose(1, 0, 2).reshape(g.shape[1], -1)
        if group.startswith("a"):
            layer = group_weights[group][0][1]
            out.update(norm_a=(layer, small_full["norm_a"][layer]), scale_a=(layer, small_full["scale_a"][layer]))
        elif group == "kv":
            out.update(norm_kv=(0, norm_kv))
        else:
            layer = group_weights[group][0][1]
            out.update(norm_b=(layer, norm_b[layer]))
        return out

    in_flight = []

    def emit(group, grads_of, carry):
        keys = list(grads_of)
        recv1 = _exchange_halves(f"grad_exchange_{group}", [grads_of[k][1] for k in keys])
        parts = [_pair_add(f"pair_add_{k}{grads_of[k][0]}", grads_of[k][1], r, c_idx) for k, r in zip(keys, recv1)]
        sems, parts, lands, carry = _scatter_start(f"grad_scatter_start_{group}", parts, carry)
        in_flight.append((group, [(k, grads_of[k][0]) for k in keys], sems, parts, lands))
        return carry

    loss_vec, grad_x, small = _local_step(x[0], loss_target[0], n_a, n_b, fetch, emit)

    small_order = [("norm_a", i) for i in range(n_a)] + [("scale_a", i) for i in range(n_a)] + [("norm_kv", 0)] + \
                  [("norm_b", i) for i in range(norm_b.shape[0])] + [("norm_f", 0)]
    pad = lambda vec: jnp.pad(vec, ((0, PAD_ROWS - 1), (0, 0)))
    packed = jnp.concatenate([pad(loss_vec)] + [pad(small[n][i]) for n, i in small_order], axis=0)
    totals = _sum_devices("small_sum", _allgather_small(packed))
    loss = 0.5 * jnp.sum(totals[0]) / d
    small_tot = {}
    for j, (n, i) in enumerate(small_order):
        small_tot.setdefault(n, []).append(totals[PAD_ROWS * (j + 1)])
    grads = {}
    shard_w = norm_a.shape[1]
    for n in ("norm_a", "scale_a"):
        full = jnp.stack(small_tot[n])
        grads[n] = lax.dynamic_slice_in_dim(full, s_me * shard_w, shard_w, axis=1)
    grads["norm_kv"] = small_tot["norm_kv"][0]
    grads["norm_b"] = jnp.stack(small_tot["norm_b"])
    grads["norm_f"] = small_tot["norm_f"][0]

    sc_idx = jnp.concatenate([s_idx, c_idx])
    fulls = {name: None for name in BIG_WEIGHTS}
    for group, keys, sems, parts, lands in in_flight:
        parts, lands = _scatter_wait(f"grad_scatter_wait_{group}", sems, parts, lands)
        for (name, i), p, r in zip(keys, parts, lands):
            n_layers = 1 if weights[name].ndim == 2 else weights[name].shape[0]
            fulls[name] = _final_add(f"final_add_{name}{i}", p, r, sc_idx, i, n_layers, into=fulls[name])
    shared = _share_halves([fulls[name] for name in BIG_WEIGHTS])
    for name, g in zip(BIG_WEIGHTS, shared):
        grads[name] = g.reshape(weights[name].shape)

    deltas, new_m, new_v = {}, {}, {}
    for n in names:
        shape = weights[n].shape
        as2d = (lambda a: a.reshape(1, -1)) if len(shape) == 1 else (lambda a: a)
        dl, mn, vn = _adamw(f"adamw_{n}", as2d(weights[n]), as2d(grads[n]), as2d(moments_m[n]), as2d(moments_v[n]))
        deltas[n], new_m[n], new_v[n] = dl.reshape(shape), mn.reshape(shape), vn.reshape(shape)

    return (loss, grad_x[None], *[grads[n] for n in names], *[deltas[n] for n in names],
            *[new_m[n] for n in names], *[new_v[n] for n in names])
```

```python
import functools
import math

import jax
import jax.numpy as jnp
from jax import lax
from jax.experimental import pallas as pl
from jax.experimental.pallas import tpu as pltpu

F32 = jnp.float32
BF16 = jnp.bfloat16

HEAD_DIM = 128
POOL_WINDOWS = (2, 4, 8, 16)
DILATED_PAIRS = ((128, 1), (512, 4), (2048, 16))
ROPE_THETA = 10000.0
RMS_EPS = 1e-6
NEG_INF = -1e30
N_CHIPS = 4

ADAM_LR = 0.001
ADAM_B1 = 0.9
ADAM_B2 = 0.999
ADAM_EPS = 1e-08
ADAM_WD = 0.01
ADAM_STEP = 10

VMEM_LIMIT_BYTES = 56 * 1024 * 1024
MESH = pl.DeviceIdType.MESH
ANY = pl.BlockSpec(memory_space=pl.ANY)


def _tile(n, pref):
    t = min(n, pref)
    assert n % t == 0, (n, pref)
    return t


def _params(sem=None):
    return pltpu.CompilerParams(dimension_semantics=sem, vmem_limit_bytes=VMEM_LIMIT_BYTES)


def _mm(name, a, b, *, grid2, nk, a_blk, a_map, b_blk, b_map, outs, dims, epi=None, epi_in=(), epi_specs=(),
        acc_shape=None, epi_scratch=(), into=None):
    n_epi, n_out = len(epi_in), len(outs)

    def body(*refs):
        a_ref, b_ref = refs[0], refs[1]
        e_refs = refs[2:2 + n_epi]
        first_out = 2 + n_epi + (0 if into is None else 1)
        o_refs = refs[first_out:first_out + n_out]
        s_refs = refs[first_out + n_out + (0 if nk == 1 else 1):]

        def contrib():
            a_val = a_ref[...]
            if a_val.ndim == 3:
                a_val = a_val.reshape(-1, a_val.shape[-1])
            return lax.dot_general(a_val, b_ref[...], (dims, ((), ())), preferred_element_type=F32)

        def finish(acc):
            if epi is None:
                o_refs[0][...] = acc.reshape(o_refs[0].shape).astype(o_refs[0].dtype)
            else:
                epi(acc, e_refs, o_refs, s_refs)

        if nk == 1:
            finish(contrib())
        else:
            acc_ref = refs[first_out + n_out]
            k = pl.program_id(2)

            @pl.when(k == 0)
            def _():
                acc_ref[...] = contrib()

            @pl.when(k > 0)
            def _():
                acc_ref[...] += contrib()

            @pl.when(k == nk - 1)
            def _():
                finish(acc_ref[...])

    scratch = ([] if nk == 1 else [pltpu.VMEM(acc_shape, F32)]) + list(epi_scratch)
    extra_in, extra_specs, aliases = (), (), {}
    if into is not None:
        extra_in, extra_specs, aliases = (into[0],), (ANY,), {2 + n_epi: into[1]}
    res = pl.pallas_call(
        body, name=name, grid=(grid2[0], grid2[1], nk),
        in_specs=[pl.BlockSpec(a_blk, a_map), pl.BlockSpec(b_blk, b_map), *epi_specs, *extra_specs],
        out_specs=[pl.BlockSpec(blk, imap) for _, blk, imap, _ in outs],
        out_shape=[jax.ShapeDtypeStruct(shape, dtype) for shape, _, _, dtype in outs],
        scratch_shapes=scratch, input_output_aliases=aliases,
        compiler_params=_params(("parallel", "parallel", "arbitrary")),
    )(a, b, *epi_in, *extra_in)
    return res[0] if n_out == 1 else tuple(res)


NN = ((1,), (0,))
NT = ((1,), (1,))
TN = ((0,), (0,))


def _rope_apply(t, cos, sin):
    return t * cos + pltpu.roll(t, HEAD_DIM // 2, 1) * sin


def _epi_add(acc, e_refs, o_refs, s_refs):
    o_refs[0][...] = (acc + e_refs[0][...]).astype(o_refs[0].dtype)


def _col_blocks(width):
    return [slice(c * HEAD_DIM, (c + 1) * HEAD_DIM) for c in range(width // HEAD_DIM)]


def _col_scratch(rows, width):
    return pltpu.VMEM((width // HEAD_DIM, rows, HEAD_DIM), F32)


def _to_residue_major(o_ref, scr, d, sl):
    if d == 1:
        o_ref[0, :, sl] = scr[...].astype(o_ref.dtype)
        return
    rows = scr.shape[0] // d
    for r in range(d):
        o_ref[r, :, sl] = scr[pl.ds(r, rows, stride=d), :].astype(o_ref.dtype)


def _from_residue_major(i_ref, scr, d, sl):
    if d == 1:
        return i_ref[0, :, sl].astype(F32)
    rows = i_ref.shape[1]
    for r in range(d):
        scr[pl.ds(r, rows, stride=d), :] = i_ref[r, :, sl].astype(F32)
    return scr[...]


def _make_epi_orders(dils, rope_scale):
    def epi(acc, e_refs, o_refs, s_refs):
        if rope_scale is not None:
            cos = e_refs[0][...]
            sin = e_refs[1][...]
        for c, sl in enumerate(_col_blocks(acc.shape[1])):
            scr = s_refs[0].at[c]
            scr[...] = acc[:, sl] if rope_scale is None else _rope_apply(acc[:, sl], cos, sin) * rope_scale
            for o_ref, d in zip(o_refs, dils):
                _to_residue_major(o_ref, scr, d, sl)
    return epi


def _make_epi_token_order(d, has_add):
    def epi(acc, e_refs, o_refs, s_refs):
        o_ref = o_refs[0]
        if d == 1:
            o_ref[...] = acc + e_refs[0][...] if has_add else acc
            return
        rows = acc.shape[0] // d
        for c, sl in enumerate(_col_blocks(acc.shape[1])):
            scr = s_refs[0].at[c]
            for r in range(d):
                scr[pl.ds(r, rows, stride=d), :] = acc[r * rows:(r + 1) * rows, sl]
            o_ref[:, sl] = scr[...] + e_refs[0][:, sl] if has_add else scr[...]
    return epi


def _mm_act_w(name, a, w, *, out_dtype=BF16, add=None, rope=None, n_first=0, n_cols=None, dils=None):
    s_len, k_len = a.shape
    bm = _tile(s_len, 1024)
    epi, epi_in, epi_specs, epi_scratch = None, (), (), ()
    if w.ndim == 3:
        ns, _, c = w.shape
        ns_used = ns if n_cols is None else n_cols
        bn = _tile(c, 1024)
        sub = c // bn
        grid2 = (ns_used * sub, s_len // bm)
        b_blk, b_map = (None, k_len, bn), (lambda j, i, k: (j // sub + n_first, 0, j % sub))
        n_len = ns_used * c
    else:
        n_len = w.shape[1]
        bn = _tile(n_len, 1024)
        grid2 = (n_len // bn, s_len // bm)
        b_blk, b_map = (k_len, bn), (lambda j, i, k: (0, j))
    if add is not None:
        epi, epi_in = _epi_add, (add,)
        epi_specs = (pl.BlockSpec((bm, bn), lambda j, i, k: (i, j)),)
    outs = [((s_len, n_len), (bm, bn), lambda j, i, k: (i, j), out_dtype)]
    if dils is not None:
        if rope is not None:
            epi_in = rope[:2]
            epi_specs = (pl.BlockSpec((bm, HEAD_DIM), lambda j, i, k: (i, 0)),) * 2
        epi = _make_epi_orders(dils, None if rope is None else rope[2])
        epi_scratch = (_col_scratch(bm, bn),)
        outs = [((d, s_len // d, n_len), (d, bm // d, bn), lambda j, i, k: (0, i, j), BF16) for d in dils]
    res = _mm(name, a, w, grid2=grid2, nk=1, a_blk=(bm, k_len), a_map=lambda j, i, k: (i, 0),
              b_blk=b_blk, b_map=b_map, outs=outs, dims=NN, epi=epi, epi_in=epi_in, epi_specs=epi_specs,
              epi_scratch=epi_scratch)
    return (res,) if dils is not None and len(dils) == 1 else res


def _mm_grad_act(name, dy, w, *, add=None, slot=None):
    if slot is not None:
        d = 1 if dy.ndim == 2 else dy.shape[0]
        s_len = dy.shape[-2] * d
        _, k_len, c = w.shape
        bm, bn = _tile(s_len, 1024), _tile(k_len, 1024)
        a_blk, a_map = ((bm, c), lambda j, i, k: (i, 0)) if dy.ndim == 2 else ((d, bm // d, c), lambda j, i, k: (0, i, 0))
        epi_in = () if add is None else (add,)
        return _mm(name, dy, w, grid2=(k_len // bn, s_len // bm), nk=1, a_blk=a_blk, a_map=a_map,
                   b_blk=(None, bn, c), b_map=lambda j, i, k: (slot, j, 0),
                   outs=[((s_len, k_len), (bm, bn), lambda j, i, k: (i, j), F32)], dims=NT,
                   epi=_make_epi_token_order(d, add is not None), epi_in=epi_in,
                   epi_specs=(pl.BlockSpec((bm, bn), lambda j, i, k: (i, j)),) * len(epi_in),
                   epi_scratch=(_col_scratch(bm, bn),) if d > 1 else ())
    s_len, n_len = dy.shape
    bm = _tile(s_len, 1024)
    if w.ndim == 3:
        ns, k_len, c = w.shape
        bk, nk = c, ns
        bn = _tile(k_len, 1024)
        b_blk, b_map = (None, bn, c), (lambda j, i, k: (k, j, 0))
    else:
        k_len = w.shape[0]
        bk = _tile(n_len, 1024)
        nk = n_len // bk
        bn = _tile(k_len, 1024)
        b_blk, b_map = (bn, bk), (lambda j, i, k: (j, k))
    epi, epi_in, epi_specs = None, (), ()
    if add is not None:
        epi, epi_in = _epi_add, (add,)
        epi_specs = (pl.BlockSpec((bm, bn), lambda j, i, k: (i, j)),)
    return _mm(name, dy, w, grid2=(k_len // bn, s_len // bm), nk=nk, a_blk=(bm, bk), a_map=lambda j, i, k: (i, k),
               b_blk=b_blk, b_map=b_map, outs=[((s_len, k_len), (bm, bn), lambda j, i, k: (i, j), F32)],
               dims=NT, epi=epi, epi_in=epi_in, epi_specs=epi_specs, acc_shape=(bm, bn))


def _mm_grad_w(name, a, dy, *, col_shards=None, slot=None, into=None):
    s_len, k_len = a.shape
    n_len = dy.shape[1]
    bk = _tile(s_len, 1024)
    bm = _tile(k_len, 1024)
    if slot is not None:
        bn = _tile(n_len, 1024)
        out = ((col_shards, k_len, n_len), (None, bm, bn), lambda j, i, k: (slot, i, j), BF16)
    elif col_shards:
        c = n_len // col_shards
        bn = _tile(c, 1024)
        sub = c // bn
        out = ((col_shards, k_len, c), (None, bm, bn), lambda j, i, k: (j // sub, i, j % sub), BF16)
    else:
        bn = _tile(n_len, 1024)
        out = ((k_len, n_len), (bm, bn), lambda j, i, k: (i, j), BF16)
    return _mm(name, a, dy, grid2=(n_len // bn, k_len // bm), nk=s_len // bk,
               a_blk=(bk, bm), a_map=lambda j, i, k: (k, i), b_blk=(bk, bn), b_map=lambda j, i, k: (k, j),
               outs=[out], dims=TN, acc_shape=(bm, bn), into=None if into is None else (into, 0))


def _mm_grp_fwd(name, pooled, wg):
    s_len, e = pooled.shape
    ng, g, _ = wg.shape
    bm = _tile(s_len, 1024)
    return _mm(name, pooled, wg, grid2=(ng, s_len // bm), nk=1, a_blk=(bm, g), a_map=lambda j, i, k: (i, j),
               b_blk=(None, g, g), b_map=lambda j, i, k: (j, 0, 0),
               outs=[((s_len, e), (bm, g), lambda j, i, k: (i, j), F32)], dims=NN)


def _mm_grp_grad_act(name, dy, wg):
    s_len, e = dy.shape
    ng, g, _ = wg.shape
    bm = _tile(s_len, 1024)
    return _mm(name, dy, wg, grid2=(ng, s_len // bm), nk=1, a_blk=(bm, g), a_map=lambda j, i, k: (i, j),
               b_blk=(None, g, g), b_map=lambda j, i, k: (j, 0, 0),
               outs=[((s_len, e), (bm, g), lambda j, i, k: (i, j), F32)], dims=NT)


def _mm_grp_grad_w(name, pooled, dy, ng):
    s_len, e = pooled.shape
    g = e // ng
    bk = _tile(s_len, 1024)
    return _mm(name, pooled, dy, grid2=(ng, 1), nk=s_len // bk, a_blk=(bk, g), a_map=lambda j, i, k: (k, j),
               b_blk=(bk, g), b_map=lambda j, i, k: (k, j),
               outs=[((N_CHIPS, ng, g // N_CHIPS, g), (N_CHIPS, None, g // N_CHIPS, g), lambda j, i, k: (0, j, 0, 0), BF16)],
               dims=TN, acc_shape=(g, g))


def _row_spec(bs, width, col=0):
    return pl.BlockSpec((bs, width), lambda i: (i, col))


def _vec_spec(width):
    return pl.BlockSpec((1, width), lambda i: (0, 0))


def _rows_call(body, name, s_len, in_specs, out_specs, out_shape, bs, aliases=None, sequential=False):
    return pl.pallas_call(
        body, name=name, grid=(s_len // bs,), in_specs=in_specs, out_specs=out_specs, out_shape=out_shape,
        input_output_aliases=aliases or {},
        compiler_params=_params(("arbitrary",) if sequential else ("parallel",)))


def _accumulate(ref, part):
    i = pl.program_id(0)

    @pl.when(i == 0)
    def _():
        ref[...] = part

    @pl.when(i > 0)
    def _():
        ref[...] += part


def _rms_scale(xf):
    return lax.rsqrt(jnp.mean(xf * xf, axis=-1, keepdims=True) + RMS_EPS)


def _res_spec(dil, bs, width):
    return pl.BlockSpec((dil, bs // dil, width), lambda i: (0, i, 0))


def _res_shape(dil, s_len, width, dtype):
    return jax.ShapeDtypeStruct((dil, s_len // dil, width), dtype)


def _rmsnorm_fwd(name, x, gain, dils=()):
    s_len, d = x.shape
    bs = _tile(s_len, 256)

    def body(x_ref, g_ref, h_ref, *rest):
        xf = x_ref[...]
        h = (xf * _rms_scale(xf)) * g_ref[...]
        h_ref[...] = h.astype(BF16)
        if dils:
            for c, sl in enumerate(_col_blocks(d)):
                scr = rest[-1].at[c]
                scr[...] = h[:, sl]
                for o_ref, dil in zip(rest[:-1], dils):
                    _to_residue_major(o_ref, scr, dil, sl)

    res = pl.pallas_call(
        body, name=name, grid=(s_len // bs,), in_specs=[_row_spec(bs, d), _vec_spec(d)],
        out_specs=[_row_spec(bs, d)] + [_res_spec(dil, bs, d) for dil in dils],
        out_shape=[jax.ShapeDtypeStruct((s_len, d), BF16)] + [_res_shape(dil, s_len, d, BF16) for dil in dils],
        scratch_shapes=[_col_scratch(bs, d)] if dils else [],
        compiler_params=_params(("parallel",)))(x, gain)
    return res[0] if not dils else tuple(res)


def _rmsnorm_bwd(name, x, gain, dh, dres):
    s_len, d = x.shape
    bs = _tile(s_len, 256)

    def body(x_ref, g_ref, dh_ref, dres_ref, dx_ref, dxb_ref, dg_ref):
        xf = x_ref[...]
        r = _rms_scale(xf)
        xh = xf * r
        dh_f = dh_ref[...]
        t = dh_f * g_ref[...]
        dx = dres_ref[...] + r * (t - xh * jnp.mean(t * xh, axis=-1, keepdims=True))
        dx_ref[...] = dx
        dxb_ref[...] = dx.astype(BF16)
        _accumulate(dg_ref, jnp.sum(dh_f * xh, axis=0, keepdims=True))

    return _rows_call(
        body, name, s_len,
        [_row_spec(bs, d), _vec_spec(d), _row_spec(bs, d), _row_spec(bs, d)],
        [_row_spec(bs, d), _row_spec(bs, d), _vec_spec(d)],
        [jax.ShapeDtypeStruct((s_len, d), F32), jax.ShapeDtypeStruct((s_len, d), BF16),
         jax.ShapeDtypeStruct((1, d), F32)], bs, sequential=True)(x, gain, dh, dres)


def _loss_head(name, x, gain, target):
    s_len, d = x.shape
    bs = _tile(s_len, 256)

    def body(x_ref, g_ref, t_ref, lv_ref, dx_ref, dxb_ref, dg_ref):
        xf = x_ref[...]
        r = _rms_scale(xf)
        xh = xf * r
        err = xh * g_ref[...] - t_ref[...]
        dy = err * (1.0 / d)
        t = dy * g_ref[...]
        dx = r * (t - xh * jnp.mean(t * xh, axis=-1, keepdims=True))
        dx_ref[...] = dx
        dxb_ref[...] = dx.astype(BF16)
        _accumulate(lv_ref, jnp.sum(err * err, axis=0, keepdims=True))
        _accumulate(dg_ref, jnp.sum(dy * xh, axis=0, keepdims=True))

    return _rows_call(
        body, name, s_len, [_row_spec(bs, d), _vec_spec(d), _row_spec(bs, d)],
        [_vec_spec(d), _row_spec(bs, d), _row_spec(bs, d), _vec_spec(d)],
        [jax.ShapeDtypeStruct((1, d), F32), jax.ShapeDtypeStruct((s_len, d), F32),
         jax.ShapeDtypeStruct((s_len, d), BF16), jax.ShapeDtypeStruct((1, d), F32)],
        bs, sequential=True)(x, gain, target)


def _sigmoid(g):
    return 1.0 / (1.0 + jnp.exp(-g))


def _gate_a_fwd(name, ypre, proj, scale):
    s_len, e = ypre.shape
    bs = _tile(s_len, 256)

    def body(y_ref, g_ref, sc_ref, z_ref):
        g = g_ref[...]
        z_ref[...] = (y_ref[...] * sc_ref[...] * (g * _sigmoid(g))).astype(BF16)

    return _rows_call(body, name, s_len, [_row_spec(bs, e), _row_spec(bs, e, 1), _vec_spec(e)], _row_spec(bs, e),
                      jax.ShapeDtypeStruct((s_len, e), BF16), bs)(ypre, proj, scale)


def _gate_a_bwd(name, dz, ypre, proj, scale):
    s_len, e = ypre.shape
    bs = _tile(s_len, 256)

    def body(dz_ref, y_ref, g_ref, sc_ref, dy_ref, dproj_ref, dsc_ref):
        g = g_ref[...]
        sg = _sigmoid(g)
        silu = g * sg
        dz_f = dz_ref[...]
        ypre_f = y_ref[...]
        dys = dz_f * silu
        dy_ref[...] = (dys * sc_ref[...]).astype(BF16)
        dproj_ref[...] = (dz_f * (ypre_f * sc_ref[...]) * (sg * (1.0 + g * (1.0 - sg)))).astype(BF16)
        _accumulate(dsc_ref, jnp.sum(dys * ypre_f, axis=0, keepdims=True))

    return _rows_call(
        body, name, s_len, [_row_spec(bs, e), _row_spec(bs, e), _row_spec(bs, e, 1), _vec_spec(e)],
        [_row_spec(bs, e), _row_spec(bs, e, 1), _vec_spec(e)],
        [jax.ShapeDtypeStruct((s_len, e), BF16), jax.ShapeDtypeStruct((s_len, 2 * e), BF16),
         jax.ShapeDtypeStruct((1, e), F32)], bs, sequential=True)(dz, ypre, proj, scale)


def _merge_gate_fwd(name, outs, lses, gate, dils):
    s_len, e = gate.shape
    bs = _tile(s_len, 256)
    n = len(outs)

    def body(*refs):
        o_refs, l_refs, g_ref = refs[:n], refs[n:2 * n], refs[2 * n]
        m_ref, lj_ref, z_ref = refs[2 * n + 1:2 * n + 4]
        scratch = refs[2 * n + 4]
        for c, sl in enumerate(_col_blocks(e)):
            ls = [_from_residue_major(r, scratch.at[2 * j, c], dil, sl) for j, (r, dil) in enumerate(zip(l_refs, dils))]
            os_ = [_from_residue_major(r, scratch.at[2 * j + 1, c], dil, sl) for j, (r, dil) in enumerate(zip(o_refs, dils))]
            mx = functools.reduce(jnp.maximum, ls)
            ws = [jnp.exp(l - mx) for l in ls]
            den = functools.reduce(lambda a, b: a + b, ws)
            merged = functools.reduce(lambda a, b: a + b, [w * o for w, o in zip(ws, os_)]) / den
            g = g_ref[:, sl]
            m_ref[:, sl] = merged.astype(BF16)
            lj_ref[:, sl] = mx + jnp.log(den)
            z_ref[:, sl] = (merged * (g * _sigmoid(g))).astype(BF16)

    spec = _row_spec(bs, e)
    res_specs = [_res_spec(dil, bs, e) for dil in dils]
    return pl.pallas_call(
        body, name=name, grid=(s_len // bs,), in_specs=res_specs + res_specs + [spec], out_specs=[spec] * 3,
        out_shape=[jax.ShapeDtypeStruct((s_len, e), BF16), jax.ShapeDtypeStruct((s_len, e), F32),
                   jax.ShapeDtypeStruct((s_len, e), BF16)],
        scratch_shapes=[pltpu.VMEM((2 * n, e // HEAD_DIM, bs, HEAD_DIM), F32)],
        compiler_params=_params(("parallel",)))(*outs, *lses, gate)


def _gate_b_bwd(name, dz, merged, gate, lse, dils):
    s_len, e = gate.shape
    bs = _tile(s_len, 256)
    n = len(dils)

    def body(dz_ref, m_ref, g_ref, l_ref, dg_ref, *rest):
        out_refs, scratch = rest[:3 * n], rest[3 * n]
        for c, sl in enumerate(_col_blocks(e)):
            g = g_ref[:, sl]
            sg = _sigmoid(g)
            dz_f = dz_ref[:, sl]
            merged = m_ref[:, sl].astype(F32)
            dmerged = dz_f * (g * sg)
            dg_ref[:, sl] = (dz_f * merged * (sg * (1.0 + g * (1.0 - sg)))).astype(BF16)
            values = (dmerged, l_ref[:, sl],
                      jnp.broadcast_to(jnp.sum(dmerged * merged, axis=-1, keepdims=True), (bs, HEAD_DIM)))
            for t, val in enumerate(values):
                scr = scratch.at[t, c]
                scr[...] = val
                for j, dil in enumerate(dils):
                    _to_residue_major(out_refs[3 * j + t], scr, dil, sl)

    spec = _row_spec(bs, e)
    out_specs, out_shape = [spec], [jax.ShapeDtypeStruct((s_len, e), BF16)]
    for dil in dils:
        out_specs += [_res_spec(dil, bs, e)] * 3
        out_shape += [_res_shape(dil, s_len, e, BF16), _res_shape(dil, s_len, e, F32), _res_shape(dil, s_len, e, F32)]
    res = pl.pallas_call(
        body, name=name, grid=(s_len // bs,), in_specs=[spec] * 4, out_specs=out_specs, out_shape=out_shape,
        scratch_shapes=[pltpu.VMEM((3, e // HEAD_DIM, bs, HEAD_DIM), F32)],
        compiler_params=_params(("parallel",)))(dz, merged, gate, lse)
    return res[0], [tuple(res[1 + 3 * j:4 + 3 * j]) for j in range(n)]


def _kv_grad_prep(name, dk_accs, dv_accs, dils, cos, sin_inv):
    n = len(dils)
    e = dk_accs[0].shape[-1]
    s_len = dk_accs[0].shape[0] * dk_accs[0].shape[1]
    bs = _tile(s_len, 256)

    def body(*refs):
        dk_refs, dv_refs = refs[:n], refs[n:2 * n]
        c_ref, s_ref, dkb_ref, dvb_ref, scratch = refs[2 * n:]
        cos_t, sin_t = c_ref[...], s_ref[...]
        add = lambda a, b: a + b
        for c, sl in enumerate(_col_blocks(e)):
            dk = functools.reduce(add, [_from_residue_major(r, scratch.at[j, c], dil, sl)
                                        for j, (r, dil) in enumerate(zip(dk_refs, dils))])
            dkb_ref[:, sl] = _rope_apply(dk, cos_t, sin_t).astype(BF16)
            dv = functools.reduce(add, [_from_residue_major(r, scratch.at[n + j, c], dil, sl)
                                        for j, (r, dil) in enumerate(zip(dv_refs, dils))])
            dvb_ref[:, sl] = dv.astype(BF16)

    spec, rspec = _row_spec(bs, e), _row_spec(bs, HEAD_DIM)
    res_specs = [_res_spec(dil, bs, e) for dil in dils]
    return pl.pallas_call(
        body, name=name, grid=(s_len // bs,), in_specs=res_specs + res_specs + [rspec, rspec], out_specs=[spec, spec],
        out_shape=[jax.ShapeDtypeStruct((s_len, e), BF16)] * 2,
        scratch_shapes=[pltpu.VMEM((2 * n, e // HEAD_DIM, bs, HEAD_DIM), F32)],
        compiler_params=_params(("parallel",)))(*dk_accs, *dv_accs, cos, sin_inv)


def _pool_cols(e):
    return _tile(e // len(POOL_WINDOWS), 256)


def _window_sum(val, grp, s_len, forward):
    rows = lax.broadcasted_iota(jnp.int32, val.shape, 0)
    acc = val
    for level in range(len(POOL_WINDOWS)):
        step = 1 << level
        if forward:
            shifted = jnp.where(rows >= step, pltpu.roll(acc, step, 0), 0.0)
        else:
            shifted = jnp.where(rows < s_len - step, pltpu.roll(acc, s_len - step, 0), 0.0)
        acc = jnp.where(level <= grp, acc + shifted, acc)
    return acc


def _window_count(shape, grp):
    rows = lax.broadcasted_iota(jnp.int32, shape, 0)
    return jnp.minimum(rows + 1, jnp.left_shift(2, grp)).astype(F32)


def _pool_fwd(name, proj):
    s_len, e2 = proj.shape
    e = e2 // 2
    cb = _pool_cols(e)
    per_grp = e // len(POOL_WINDOWS) // cb
    assert POOL_WINDOWS == tuple(2 << g for g in range(len(POOL_WINDOWS)))

    def body(u_ref, p_ref):
        grp = pl.program_id(0)
        u = u_ref[...]
        total = _window_sum(u, grp, s_len, True)
        p_ref[...] = (total / _window_count(u.shape, grp) - u).astype(BF16)

    spec = pl.BlockSpec((s_len, cb), lambda g, c: (0, g * per_grp + c))
    return pl.pallas_call(
        body, name=name, grid=(len(POOL_WINDOWS), per_grp), in_specs=[spec], out_specs=spec,
        out_shape=jax.ShapeDtypeStruct((s_len, e), BF16), compiler_params=_params(("parallel", "parallel")))(proj)


def _pool_bwd(name, dpooled, dproj):
    s_len, e = dpooled.shape
    cb = _pool_cols(e)
    per_grp = e // len(POOL_WINDOWS) // cb

    def body(dp_ref, _, du_ref):
        grp = pl.program_id(0)
        dp = dp_ref[...]
        total = _window_sum(dp / _window_count(dp.shape, grp), grp, s_len, False)
        du_ref[...] = (total - dp).astype(BF16)

    spec = pl.BlockSpec((s_len, cb), lambda g, c: (0, g * per_grp + c))
    return pl.pallas_call(
        body, name=name, grid=(len(POOL_WINDOWS), per_grp), in_specs=[spec, ANY], out_specs=spec,
        out_shape=jax.ShapeDtypeStruct(dproj.shape, BF16), input_output_aliases={1: 0},
        compiler_params=_params(("parallel", "parallel")))(dpooled, dproj)


def _band_masks(nb, first):
    row = lax.broadcasted_iota(jnp.int32, (nb, nb), 0)
    col = lax.broadcasted_iota(jnp.int32, (nb, nb), 1)
    return col >= row + jnp.where(first, 2 * nb, 0), col <= row


def _dot(a, b, dims):
    return lax.dot_general(a, b, (dims, ((), ())), preferred_element_type=F32)


def _attn_fwd(name, window, q, k, v):
    dil, m, e = k.shape
    nb = window // dil
    nblk = m // nb
    heads = e // HEAD_DIM

    def body(q_ref, kp_ref, kc_ref, vp_ref, vc_ref, o_ref, l_ref):
        mask_p, mask_c = _band_masks(nb, pl.program_id(1) == 0)
        for h in range(heads):
            sl = slice(h * HEAD_DIM, (h + 1) * HEAD_DIM)
            q = q_ref[:, sl]
            s_p = jnp.where(mask_p, _dot(q, kp_ref[:, sl], NT), NEG_INF)
            s_c = jnp.where(mask_c, _dot(q, kc_ref[:, sl], NT), NEG_INF)
            mx = jnp.maximum(jnp.max(s_p, axis=-1, keepdims=True), jnp.max(s_c, axis=-1, keepdims=True))
            p_p = jnp.exp(s_p - mx)
            p_c = jnp.exp(s_c - mx)
            den = jnp.sum(p_p, axis=-1, keepdims=True) + jnp.sum(p_c, axis=-1, keepdims=True)
            out = _dot(p_p.astype(BF16), vp_ref[:, sl], NN) + _dot(p_c.astype(BF16), vc_ref[:, sl], NN)
            o_ref[:, sl] = (out / den).astype(BF16)
            l_ref[:, sl] = jnp.broadcast_to(mx + jnp.log(den), (nb, HEAD_DIM))

    blk = (None, nb, e)
    prev = lambda r, n: (r, jnp.maximum(n - 1, 0), 0)
    cur = lambda r, n: (r, n, 0)
    return pl.pallas_call(
        body, name=name, grid=(dil, nblk),
        in_specs=[pl.BlockSpec(blk, cur), pl.BlockSpec(blk, prev), pl.BlockSpec(blk, cur), pl.BlockSpec(blk, prev),
                  pl.BlockSpec(blk, cur)],
        out_specs=[pl.BlockSpec(blk, cur), pl.BlockSpec(blk, cur)],
        out_shape=[jax.ShapeDtypeStruct((dil, m, e), BF16), jax.ShapeDtypeStruct((dil, m, e), F32)],
        compiler_params=_params(("parallel", "arbitrary")),
    )(q, k, k, v, v)


def _attn_bwd(name, window, scale, q, k, v, dout, lse, delta, cos, sin_inv, dk_acc, dv_acc):
    dil, m, e = k.shape
    nb = window // dil
    nblk = m // nb
    heads = e // HEAD_DIM

    def body(q_ref, kp_ref, kc_ref, vp_ref, vc_ref, do_ref, l_ref, dl_ref, c_ref, s_ref, dki_ref, dvi_ref,
             dq_ref, dko_ref, dvo_ref, ck_ref, cv_ref):
        n = pl.program_id(1)

        @pl.when(n == 0)
        def _():
            ck_ref[...] = jnp.zeros_like(ck_ref)
            cv_ref[...] = jnp.zeros_like(cv_ref)

        @pl.when(n < nblk)
        def _():
            mask_p, mask_c = _band_masks(nb, n == 0)
            cos_t, sin_t = c_ref[...], s_ref[...]
            for h in range(heads):
                sl = slice(h * HEAD_DIM, (h + 1) * HEAD_DIM)
                q, kp, kc, vp, vc = q_ref[:, sl], kp_ref[:, sl], kc_ref[:, sl], vp_ref[:, sl], vc_ref[:, sl]
                do = do_ref[:, sl]
                lj = l_ref[:, sl] if nb == HEAD_DIM else l_ref[:, sl][:, :1]
                delta = dl_ref[:, sl] if nb == HEAD_DIM else dl_ref[:, sl][:, :1]
                p_p = jnp.where(mask_p, jnp.exp(_dot(q, kp, NT) - lj), 0.0)
                p_c = jnp.where(mask_c, jnp.exp(_dot(q, kc, NT) - lj), 0.0)
                ds_p = (p_p * (_dot(do, vp, NT) - delta)).astype(BF16)
                ds_c = (p_c * (_dot(do, vc, NT) - delta)).astype(BF16)
                dq = (_dot(ds_p, kp, NN) + _dot(ds_c, kc, NN)) * scale
                dq_ref[:, sl] = _rope_apply(dq, cos_t, sin_t).astype(BF16)
                dko_ref[:, sl] = dki_ref[:, sl] + ck_ref[:, sl] + _dot(ds_p, q, TN)
                dvo_ref[:, sl] = dvi_ref[:, sl] + cv_ref[:, sl] + _dot(p_p.astype(BF16), do, TN)
                ck_ref[:, sl] = _dot(ds_c, q, TN)
                cv_ref[:, sl] = _dot(p_c.astype(BF16), do, TN)

        @pl.when(n == nblk)
        def _():
            dko_ref[...] = dki_ref[...] + ck_ref[...]
            dvo_ref[...] = dvi_ref[...] + cv_ref[...]

    blk = (None, nb, e)
    qn = lambda n: jnp.minimum(n, nblk - 1)
    cur = lambda r, n: (r, qn(n), 0)
    prev = lambda r, n: (r, jnp.maximum(qn(n) - 1, 0), 0)
    kprev = lambda r, n: (r, jnp.maximum(n - 1, 0), 0)
    rblk = (None, nb, HEAD_DIM)
    return pl.pallas_call(
        body, name=name, grid=(dil, nblk + 1),
        in_specs=[pl.BlockSpec(blk, cur),
                  pl.BlockSpec(blk, prev), pl.BlockSpec(blk, cur), pl.BlockSpec(blk, prev), pl.BlockSpec(blk, cur),
                  pl.BlockSpec(blk, cur), pl.BlockSpec(blk, cur), pl.BlockSpec(blk, cur),
                  pl.BlockSpec(rblk, cur), pl.BlockSpec(rblk, cur),
                  pl.BlockSpec(blk, kprev), pl.BlockSpec(blk, kprev)],
        out_specs=[pl.BlockSpec(blk, cur), pl.BlockSpec(blk, kprev), pl.BlockSpec(blk, kprev)],
        out_shape=[jax.ShapeDtypeStruct((dil, m, e), BF16),
                   jax.ShapeDtypeStruct((dil, m, e), F32), jax.ShapeDtypeStruct((dil, m, e), F32)],
        scratch_shapes=[pltpu.VMEM((nb, e), F32), pltpu.VMEM((nb, e), F32)],
        input_output_aliases={10: 1, 11: 2},
        compiler_params=_params(("parallel", "arbitrary")),
    )(q, k, k, v, v, dout, lse, delta, cos, sin_inv, dk_acc, dv_acc)


def _rope_tables(s_len):
    inv_freq = 1.0 / (ROPE_THETA ** (jnp.arange(0, HEAD_DIM, 2, dtype=F32) / HEAD_DIM))
    ang = jnp.arange(s_len, dtype=F32)[:, None] * inv_freq[None, :]
    cos, sin = jnp.cos(ang), jnp.sin(ang)
    return jnp.concatenate([cos, cos], axis=1), jnp.concatenate([-sin, sin], axis=1)


def _row(vec):
    return vec.reshape(1, -1)


def _local_step(x, target, n_a, n_b, fetch, emit):
    s_len, d = x.shape
    n_q = len(DILATED_PAIRS)
    cos, sin = _rope_tables(s_len)
    sin_inv = -sin
    q_scale = 1.0 / math.sqrt(HEAD_DIM)
    w = {}

    def need(group, after):
        for name, (layer, arr) in fetch(group, after).items():
            w.setdefault(name, {})[layer] = arr

    saved_a = []
    for i in range(n_a):
        need(f"a{i}", x)
        h = _rmsnorm_fwd(f"a{i}_norm", x, _row(w["norm_a"][i]))
        proj = _mm_act_w(f"a{i}_in", h, w["w_in_a"][i], out_dtype=F32)
        pooled = _pool_fwd(f"a{i}_pool", proj)
        ypre = _mm_grp_fwd(f"a{i}_grp", pooled, w["w_grp_a"][i])
        z = _gate_a_fwd(f"a{i}_gate", ypre, proj, _row(w["scale_a"][i]))
        x_next = _mm_act_w(f"a{i}_out", z, w["w_out_a"][i], out_dtype=F32, add=x)
        saved_a.append((x, h, proj, pooled, ypre, z))
        x = x_next

    x_kv = x
    need("kv", x)
    e = w["w_k"][0].shape[1]
    kv_in = _rmsnorm_fwd("kv_norm", x, _row(w["norm_kv"][0]))
    windows = [window for window, _ in DILATED_PAIRS]
    dils = tuple(dil for _, dil in DILATED_PAIRS)
    far_dils = tuple(dil for dil in dils if dil > 1)
    ks = _mm_act_w("kv_k", kv_in, w["w_k"][0], rope=(cos, sin, 1.0), dils=dils)
    vs = _mm_act_w("kv_v", kv_in, w["w_v"][0], dils=dils)

    saved_b = []
    for i in range(n_b):
        need(f"b{i}", x if i > 0 else vs[0])
        hs = _rmsnorm_fwd(f"b{i}_norm", x, _row(w["norm_b"][i]), dils=far_dils)
        hs = {1: hs[0], **{dil: h_d.reshape(s_len, d) for dil, h_d in zip(far_dils, hs[1:])}}
        qs = [_mm_act_w(f"b{i}_q{g}", hs[1], w["w_in_b"][i], rope=(cos, sin, q_scale), n_first=g, n_cols=1,
                        dils=(dil,))[0] for g, dil in enumerate(dils)]
        gate = _mm_act_w(f"b{i}_g", hs[1], w["w_in_b"][i], out_dtype=F32, n_first=n_q, n_cols=1)
        outs, lses = [], []
        for g in range(n_q):
            o_g, l_g = _attn_fwd(f"b{i}_attn{g}", windows[g], qs[g], ks[g], vs[g])
            outs.append(o_g)
            lses.append(l_g)
        merged, lse, z = _merge_gate_fwd(f"b{i}_merge", outs, lses, gate, dils)
        x_next = _mm_act_w(f"b{i}_out", z, w["w_out_b"][i], out_dtype=F32, add=x)
        saved_b.append((x, hs, qs, gate, merged, lse, z))
        x = x_next

    need("head", x)
    loss_vec, dx, dxb, g_norm_f = _loss_head("loss_head", x, _row(w["norm_f"][0]), target)

    small = {"norm_a": {}, "scale_a": {}, "norm_kv": {}, "norm_b": {}, "norm_f": {0: g_norm_f}}
    shard_rows = lambda g2: g2.reshape(N_CHIPS, g2.shape[0] // N_CHIPS, g2.shape[1])

    res_major = lambda t, dil: t.reshape(s_len // dil, dil, t.shape[1]).transpose(1, 0, 2)
    cos_r = [res_major(cos, dil) for dil in dils]
    sin_inv_r = [res_major(sin_inv, dil) for dil in dils]
    dk_accs = [jnp.zeros((dil, s_len // dil, e), F32) for dil in dils]
    dv_accs = [jnp.zeros((dil, s_len // dil, e), F32) for dil in dils]
    for i in reversed(range(n_b)):
        x_in, hs, qs, gate, merged, lse, z = saved_b[i]
        dz = _mm_grad_act(f"b{i}_dz", dxb, w["w_out_b"][i])
        g_out = shard_rows(_mm_grad_w(f"b{i}_gwo", z, dxb))
        dgate, stats = _gate_b_bwd(f"b{i}_dgate", dz, merged, gate, lse, dils)
        dh = _mm_grad_act(f"b{i}_dh{n_q}", dgate, w["w_in_b"][i], slot=n_q)
        g_in = _mm_grad_w(f"b{i}_gwi{n_q}", hs[1], dgate, col_shards=n_q + 1, slot=n_q)
        for g, dil in enumerate(dils):
            dout, lse_g, delta_g = stats[g]
            dq, dk_accs[g], dv_accs[g] = _attn_bwd(f"b{i}_dattn{g}", windows[g], q_scale, qs[g], ks[g], vs[g], dout,
                                                   lse_g, delta_g, cos_r[g], sin_inv_r[g], dk_accs[g], dv_accs[g])
            dh = _mm_grad_act(f"b{i}_dh{g}", dq if dil > 1 else dq[0], w["w_in_b"][i], add=dh, slot=g)
            g_in = _mm_grad_w(f"b{i}_gwi{g}", hs[dil], dq.reshape(s_len, e), col_shards=n_q + 1, slot=g, into=g_in)
        dx, dxb, small["norm_b"][i] = _rmsnorm_bwd(f"b{i}_dnorm", x_in, _row(w["norm_b"][i]), dh, dx)
        dx, dxb = emit(f"b{i}", {"w_in_b": (i, g_in), "w_out_b": (i, g_out)}, (dx, dxb))

    dkb, dvb = _kv_grad_prep("kv_dprep", dk_accs, dv_accs, dils, cos, sin_inv)
    dkv = _mm_grad_act("kv_dk", dkb, w["w_k"][0])
    dkv = _mm_grad_act("kv_dv", dvb, w["w_v"][0], add=dkv)
    g_k = shard_rows(_mm_grad_w("kv_gwk", kv_in, dkb))
    g_v = shard_rows(_mm_grad_w("kv_gwv", kv_in, dvb))
    dx, dxb, small["norm_kv"][0] = _rmsnorm_bwd("kv_dnorm", x_kv, _row(w["norm_kv"][0]), dkv, dx)
    dx, dxb = emit("kv", {"w_k": (0, g_k), "w_v": (0, g_v)}, (dx, dxb))

    for i in reversed(range(n_a)):
        x_in, h, proj, pooled, ypre, z = saved_a[i]
        dz = _mm_grad_act(f"a{i}_dz", dxb, w["w_out_a"][i])
        g_out = shard_rows(_mm_grad_w(f"a{i}_gwo", z, dxb))
        dypre, dproj, small["scale_a"][i] = _gate_a_bwd(f"a{i}_dgate", dz, ypre, proj, _row(w["scale_a"][i]))
        dpooled = _mm_grp_grad_act(f"a{i}_dgrp", dypre, w["w_grp_a"][i])
        g_grp = _mm_grp_grad_w(f"a{i}_gwg", pooled, dypre, len(POOL_WINDOWS))
        g_grp = g_grp.reshape(N_CHIPS, -1, g_grp.shape[-1])
        dproj = _pool_bwd(f"a{i}_dpool", dpooled, dproj)
        dh = _mm_grad_act(f"a{i}_dh", dproj, w["w_in_a"][i])
        g_in = _mm_grad_w(f"a{i}_gwi", h, dproj, col_shards=N_CHIPS)
        dx, dxb, small["norm_a"][i] = _rmsnorm_bwd(f"a{i}_dnorm", x_in, _row(w["norm_a"][i]), dh, dx)
        a_grads = {"w_in_a": (i, g_in), "w_grp_a": (i, g_grp), "w_out_a": (i, g_out)}
        if i > 0:
            dx, dxb = emit(f"a{i}", a_grads, (dx, dxb))
        else:
            emit(f"a{i}", a_grads, ())

    return loss_vec, dx, small


BIG_WEIGHTS = ("w_in_a", "w_grp_a", "w_out_a", "w_k", "w_v", "w_in_b", "w_out_b")


def _pair_add(name, grad, recv, c_idx):
    _, r, cols = grad.shape
    half = r // 2
    rb = _tile(half, 256)
    nrb = half // rb

    def body(c_ref, g_ref, r_ref, o_ref):
        o_ref[...] = (g_ref[...].astype(F32) + r_ref[...].astype(F32)).astype(BF16)

    blk = (None, rb, cols)
    grid_spec = pltpu.PrefetchScalarGridSpec(
        num_scalar_prefetch=1, grid=(N_CHIPS, nrb),
        in_specs=[pl.BlockSpec(blk, lambda s, i, c: (s, c[0] * nrb + i, 0)), pl.BlockSpec(blk, lambda s, i, c: (s, i, 0))],
        out_specs=pl.BlockSpec(blk, lambda s, i, c: (s, i, 0)))
    return pl.pallas_call(body, name=name, grid_spec=grid_spec,
                          out_shape=jax.ShapeDtypeStruct((N_CHIPS, half, cols), BF16),
                          compiler_params=_params(("parallel", "parallel")))(c_idx, grad, recv)


def _final_add(name, part, recv, sc_idx, layer, n_layers, into=None):
    _, half, cols = part.shape
    rb = _tile(half, 256)
    nrb = half // rb
    n_peer = recv.shape[0]

    def body(sc_ref, p_ref, *refs):
        acc = p_ref[...].astype(F32)
        for r_ref in refs[:n_peer]:
            acc = acc + r_ref[...].astype(F32)
        refs[-1][...] = acc

    blk = (None, rb, cols)
    peer_spec = lambda k: pl.BlockSpec(blk, lambda i, sc: (k, i, 0))
    grid_spec = pltpu.PrefetchScalarGridSpec(
        num_scalar_prefetch=1, grid=(nrb,),
        in_specs=[pl.BlockSpec(blk, lambda i, sc: (sc[0], i, 0))] + [peer_spec(k) for k in range(n_peer)]
                 + ([] if into is None else [ANY]),
        out_specs=pl.BlockSpec(blk, lambda i, sc: (layer, sc[1] * nrb + i, 0)))
    extra = () if into is None else (into,)
    return pl.pallas_call(body, name=name, grid_spec=grid_spec,
                          out_shape=jax.ShapeDtypeStruct((n_layers, 2 * half, cols), F32),
                          input_output_aliases={} if into is None else {2 + n_peer: 0},
                          compiler_params=_params(("parallel",)))(sc_idx, part, *([recv] * n_peer), *extra)


def _cast_into_slot(name, arr, layer, s_idx):
    _, b, r, cols = arr.shape
    rb = _tile(r, 512)

    def body(s_ref, a_ref, o_ref):
        o_ref[...] = a_ref[...].astype(BF16)

    blk = (None, None, rb, cols)
    grid_spec = pltpu.PrefetchScalarGridSpec(
        num_scalar_prefetch=1, grid=(b, r // rb),
        in_specs=[pl.BlockSpec(blk, lambda j, i, s: (layer, j, i, 0))],
        out_specs=pl.BlockSpec(blk, lambda j, i, s: (j, s[0], i, 0)))
    return pl.pallas_call(body, name=name, grid_spec=grid_spec,
                          out_shape=jax.ShapeDtypeStruct((b, N_CHIPS, r, cols), BF16),
                          compiler_params=_params(("parallel", "parallel")))(s_idx, arr)


def _sum_devices(name, gathered):
    n_dev, p, d = gathered.shape

    def body(g_ref, o_ref):
        acc = g_ref[0]
        for j in range(1, n_dev):
            acc = acc + g_ref[j]
        o_ref[...] = acc

    return pl.pallas_call(body, name=name, out_shape=jax.ShapeDtypeStruct((p, d), F32),
                          compiler_params=_params())(gathered)


def _adamw(name, w, g, m, v):
    shape = w.shape
    cols = shape[-1]
    flat = lambda a: a.reshape(-1, cols)
    rows = flat(w).shape[0]
    bs = _tile(rows, 256)

    def body(w_ref, g_ref, m_ref, v_ref, d_ref, mo_ref, vo_ref):
        grad = g_ref[...]
        m_new = ADAM_B1 * m_ref[...] + (1.0 - ADAM_B1) * grad
        v_new = ADAM_B2 * v_ref[...] + (1.0 - ADAM_B2) * (grad * grad)
        m_hat = m_new / (1.0 - ADAM_B1 ** ADAM_STEP)
        v_hat = v_new / (1.0 - ADAM_B2 ** ADAM_STEP)
        d_ref[...] = -ADAM_LR * (m_hat / (jnp.sqrt(v_hat) + ADAM_EPS) + ADAM_WD * w_ref[...])
        mo_ref[...] = m_new
        vo_ref[...] = v_new

    spec = _row_spec(bs, cols)
    outs = _rows_call(body, name, rows, [spec] * 4, [spec] * 3, [jax.ShapeDtypeStruct((rows, cols), F32)] * 3, bs)(
        flat(w), flat(g), flat(m), flat(v))
    return tuple(o.reshape(shape) for o in outs)


def _place():
    x, y, c = lax.axis_index("x"), lax.axis_index("y"), lax.axis_index("c")
    chips = [(1 - x, y), (x, 1 - y), (1 - x, 1 - y)]
    return x, y, c, chips


def _chip_index(chip):
    return 2 * chip[0] + chip[1]


def _comm_call(body, name, n_in, out_shape, scratch, aliases=None):
    return pl.pallas_call(body, name=name, in_specs=[ANY] * n_in, out_specs=[ANY] * len(out_shape), out_shape=out_shape,
                          scratch_shapes=scratch, input_output_aliases=aliases or {})


HBM_SPEC = pl.BlockSpec(memory_space=pltpu.HBM)
SEM_SPEC = pl.BlockSpec(memory_space=pltpu.SEMAPHORE)
SPLIT_PARAMS = pltpu.CompilerParams(has_side_effects=pltpu.SideEffectType.DATAFLOW_SIDE_EFFECTING)


def _in_hbm(arr):
    return pltpu.with_memory_space_constraint(arr, pltpu.HBM)


def _slot_half(ref, chip, core):
    half = ref.shape[2] // 2
    return ref.at[:, _chip_index(chip), pl.ds(core * half, half), :]


def _gather_start(name, bufs, carry=()):
    n, n_c = len(bufs), len(carry)

    def body(*refs):
        ins, (send_sems, recv_sems) = refs[:n], refs[n + n_c:n + n_c + 2]
        x, y, c, chips = _place()
        for a in range(n):
            block = _slot_half(ins[a], (x, y), c)
            for k, chip in enumerate(chips):
                pltpu.make_async_remote_copy(src_ref=block, dst_ref=block, send_sem=send_sems.at[3 * a + k],
                                             recv_sem=recv_sems.at[3 * a + k], device_id=(*chip, c),
                                             device_id_type=MESH).start()

    dma = pltpu.SemaphoreType.DMA
    thru = list(bufs) + list(carry)
    res = pl.pallas_call(
        body, name=name, in_specs=[HBM_SPEC] * (n + n_c), out_specs=[SEM_SPEC] * 2 + [HBM_SPEC] * (n + n_c),
        out_shape=[dma((3 * n,)), dma((3 * n,))] + [pltpu.HBM(a.shape, a.dtype) for a in thru],
        input_output_aliases={t: 2 + t for t in range(n + n_c)}, compiler_params=SPLIT_PARAMS,
    )(*[_in_hbm(a) for a in thru])
    return (res[0], res[1]), list(res[2:2 + n]), list(res[2 + n:])


def _gather_wait(name, sems, bufs, after):
    n = len(bufs)

    def body(*refs):
        ins, (send_sems, recv_sems) = refs[:n], refs[n:n + 2]
        x, y, c, chips = _place()
        for a in range(n):
            for k, chip in enumerate(chips):
                mine, theirs = _slot_half(ins[a], (x, y), c), _slot_half(ins[a], chip, c)
                copy = pltpu.make_async_remote_copy(src_ref=mine, dst_ref=theirs, send_sem=send_sems.at[3 * a + k],
                                                    recv_sem=recv_sems.at[3 * a + k], device_id=(*chip, c),
                                                    device_id_type=MESH)
                copy.wait_send()
                copy.wait_recv()

    res = pl.pallas_call(
        body, name=name, in_specs=[HBM_SPEC] * n + [SEM_SPEC, SEM_SPEC, ANY], out_specs=[HBM_SPEC] * n,
        out_shape=[pltpu.HBM(b.shape, b.dtype) for b in bufs], input_output_aliases={a: a for a in range(n)},
        compiler_params=SPLIT_PARAMS)(*bufs, *sems, after)
    return list(res)


def _gather_forward(name, bufs, smalls=()):
    n, n_small = len(bufs), len(smalls)

    def body(*refs):
        small_in = refs[n:n + n_small]
        outs = refs[n + n_small:2 * n + n_small]
        small_out = refs[2 * n + n_small:2 * n + 2 * n_small]
        send_sems, recv_sems, s_send, s_recv, s_local = refs[-5:]
        x, y, c, chips = _place()
        me, sibling = _chip_index((x, y)), (x, y, 1 - c)

        def forward(t, k, core):
            block = _slot_half(outs[t], chips[k], core)
            return pltpu.make_async_remote_copy(src_ref=block, dst_ref=block, send_sem=send_sems.at[t, k],
                                                recv_sem=recv_sems.at[t, k], device_id=sibling, device_id_type=MESH)

        def small_copy(j, k, slot):
            return pltpu.make_async_remote_copy(src_ref=small_in[j], dst_ref=small_out[j].at[slot],
                                                send_sem=s_send.at[j, k], recv_sem=s_recv.at[j, k],
                                                device_id=(*chips[k], c), device_id_type=MESH)

        local = []
        for t in range(n):
            for k in range(3):
                forward(t, k, c).start()
        for j in range(n_small):
            own = pltpu.make_async_copy(small_in[j], small_out[j].at[me], s_local.at[j])
            own.start()
            local.append(own)
            for k in range(3):
                small_copy(j, k, me).start()
        for t in range(n):
            for k in range(3):
                forward(t, k, 1 - c).wait_recv()
        for j in range(n_small):
            for k in range(3):
                small_copy(j, k, _chip_index(chips[k])).wait_recv()
        for t in range(n):
            for k in range(3):
                forward(t, k, c).wait_send()
        for j in range(n_small):
            for k in range(3):
                small_copy(j, k, me).wait_send()
        for own in local:
            own.wait()

    out_shape = [jax.ShapeDtypeStruct(b.shape, BF16) for b in bufs]
    out_shape += [jax.ShapeDtypeStruct((N_CHIPS,) + s.shape, F32) for s in smalls]
    dma = pltpu.SemaphoreType.DMA
    n_s = max(n_small, 1)
    res = _comm_call(body, name, n + n_small, out_shape,
                     [dma((n, 3)), dma((n, 3)), dma((n_s, 3)), dma((n_s, 3)), dma((n_s,))],
                     aliases={t: t for t in range(n)})(*bufs, *smalls)
    return list(res[:n]), list(res[n:])


def _exchange_halves(name, grads):
    n = len(grads)

    def body(*refs):
        g_in, outs = refs[:n], refs[n:2 * n]
        send_sems, recv_sems = refs[-2:]
        x, y, c, _ = _place()
        copies = []
        for t in range(n):
            half = g_in[t].shape[1] // 2
            cp = pltpu.make_async_remote_copy(
                src_ref=g_in[t].at[:, pl.ds((1 - c) * half, half), :], dst_ref=outs[t], send_sem=send_sems.at[t],
                recv_sem=recv_sems.at[t], device_id=(x, y, 1 - c), device_id_type=MESH)
            cp.start()
            copies.append(cp)
        for cp in copies:
            cp.wait()

    out_shape = [jax.ShapeDtypeStruct((g.shape[0], g.shape[1] // 2, g.shape[2]), BF16) for g in grads]
    dma = pltpu.SemaphoreType.DMA
    return list(_comm_call(body, name, n, out_shape, [dma((n,)), dma((n,))])(*grads))


def _scatter_copy(part_ref, land_ref, send_sems, recv_sems, t, k, chip, c):
    return pltpu.make_async_remote_copy(
        src_ref=part_ref.at[_chip_index(chip)], dst_ref=land_ref.at[k], send_sem=send_sems.at[3 * t + k],
        recv_sem=recv_sems.at[3 * t + k], device_id=(*chip, c), device_id_type=MESH)


def _scatter_start(name, parts, carry=()):
    n, n_c = len(parts), len(carry)
    lands = [lax.empty((3,) + p.shape[1:], BF16) for p in parts]

    def body(*refs):
        p_in, l_in = refs[:n], refs[n:2 * n]
        send_sems, recv_sems = refs[2 * n + n_c:2 * n + n_c + 2]
        x, y, c, chips = _place()
        for t in range(n):
            for k, chip in enumerate(chips):
                _scatter_copy(p_in[t], l_in[t], send_sems, recv_sems, t, k, chip, c).start()

    dma = pltpu.SemaphoreType.DMA
    thru = list(parts) + lands + list(carry)
    res = pl.pallas_call(
        body, name=name, in_specs=[HBM_SPEC] * len(thru), out_specs=[SEM_SPEC] * 2 + [HBM_SPEC] * len(thru),
        out_shape=[dma((3 * n,)), dma((3 * n,))] + [pltpu.HBM(a.shape, a.dtype) for a in thru],
        input_output_aliases={t: 2 + t for t in range(len(thru))}, compiler_params=SPLIT_PARAMS,
    )(*[_in_hbm(a) for a in thru])
    return (res[0], res[1]), list(res[2:2 + n]), list(res[2 + n:2 + 2 * n]), list(res[2 + 2 * n:])


def _scatter_wait(name, sems, parts, lands):
    n = len(parts)

    def body(*refs):
        p_in, l_in = refs[:n], refs[n:2 * n]
        send_sems, recv_sems = refs[2 * n:2 * n + 2]
        x, y, c, chips = _place()
        for t in range(n):
            for k, chip in enumerate(chips):
                copy = _scatter_copy(p_in[t], l_in[t], send_sems, recv_sems, t, k, chip, c)
                copy.wait_send()
                copy.wait_recv()

    hbm_out = lambda a: pltpu.HBM(a.shape, a.dtype)
    res = pl.pallas_call(
        body, name=name, in_specs=[HBM_SPEC] * (2 * n) + [SEM_SPEC, SEM_SPEC], out_specs=[HBM_SPEC] * (2 * n),
        out_shape=[hbm_out(a) for a in parts + lands], input_output_aliases={t: t for t in range(2 * n)},
        compiler_params=SPLIT_PARAMS)(*parts, *lands, *sems)
    return list(res[:n]), list(res[n:])


def _share_halves(fulls):
    n = len(fulls)
    items = [(a, l) for a in range(n) for l in range(fulls[a].shape[0])]

    def body(*refs):
        outs = refs[n:2 * n]
        send_sems, recv_sems = refs[-2:]
        x, y, c, _ = _place()

        def copy(t, core):
            a, l = items[t]
            half = outs[a].shape[1] // 2
            block = outs[a].at[l, pl.ds(core * half, half), :]
            return pltpu.make_async_remote_copy(src_ref=block, dst_ref=block, send_sem=send_sems.at[t],
                                                recv_sem=recv_sems.at[t], device_id=(x, y, 1 - c), device_id_type=MESH)

        for t in range(len(items)):
            copy(t, c).start()
        for t in range(len(items)):
            copy(t, 1 - c).wait_recv()
        for t in range(len(items)):
            copy(t, c).wait_send()

    out_shape = [jax.ShapeDtypeStruct(f.shape, F32) for f in fulls]
    dma = pltpu.SemaphoreType.DMA
    return list(_comm_call(body, "grad_share_halves", n, out_shape, [dma((len(items),)), dma((len(items),))],
                           aliases={a: a for a in range(n)})(*fulls))


def _allgather_small(packed):
    def body(p_ref, o_ref, send_sems, recv_sems, local_sem):
        x, y, c, _ = _place()
        me = 4 * x + 2 * y + c
        own = pltpu.make_async_copy(p_ref, o_ref.at[me], local_sem)
        own.start()
        flips = [(fx, fy, fc) for fx in (0, 1) for fy in (0, 1) for fc in (0, 1)][1:]
        peers = [(x ^ fx, y ^ fy, c ^ fc) for fx, fy, fc in flips]
        copies = []
        for k, peer in enumerate(peers):
            cp = pltpu.make_async_remote_copy(src_ref=p_ref, dst_ref=o_ref.at[me], send_sem=send_sems.at[k],
                                              recv_sem=recv_sems.at[k], device_id=peer, device_id_type=MESH)
            cp.start()
            copies.append(cp)
        for k, (px, py, pc) in enumerate(peers):
            pltpu.make_async_remote_copy(src_ref=p_ref, dst_ref=o_ref.at[4 * px + 2 * py + pc], send_sem=send_sems.at[k],
                                         recv_sem=recv_sems.at[k], device_id=peers[k], device_id_type=MESH).wait_recv()
        for cp in copies:
            cp.wait_send()
        own.wait()

    dma = pltpu.SemaphoreType.DMA
    return _comm_call(body, "small_allgather", 1, [jax.ShapeDtypeStruct((8,) + packed.shape, F32)],
                      [dma((7,)), dma((7,)), dma(())])(packed)[0]


PAD_ROWS = 8


def kernel(x, norm_a, w_in_a, w_grp_a, scale_a, w_out_a, norm_kv, w_k, w_v, norm_b, w_in_b, w_out_b, norm_f, loss_target, m_norm_a, m_w_in_a, m_w_grp_a, m_scale_a, m_w_out_a, m_norm_kv, m_w_k, m_w_v, m_norm_b, m_w_in_b, m_w_out_b, m_norm_f, v_norm_a, v_w_in_a, v_w_grp_a, v_scale_a, v_w_out_a, v_norm_kv, v_w_k, v_w_v, v_norm_b, v_w_in_b, v_w_out_b, v_norm_f):
    weights = dict(norm_a=norm_a, w_in_a=w_in_a, w_grp_a=w_grp_a, scale_a=scale_a, w_out_a=w_out_a, norm_kv=norm_kv,
                   w_k=w_k, w_v=w_v, norm_b=norm_b, w_in_b=w_in_b, w_out_b=w_out_b, norm_f=norm_f)
    moments_m = dict(norm_a=m_norm_a, w_in_a=m_w_in_a, w_grp_a=m_w_grp_a, scale_a=m_scale_a, w_out_a=m_w_out_a,
                     norm_kv=m_norm_kv, w_k=m_w_k, w_v=m_w_v, norm_b=m_norm_b, w_in_b=m_w_in_b, w_out_b=m_w_out_b,
                     norm_f=m_norm_f)
    moments_v = dict(norm_a=v_norm_a, w_in_a=v_w_in_a, w_grp_a=v_w_grp_a, scale_a=v_scale_a, w_out_a=v_w_out_a,
                     norm_kv=v_norm_kv, w_k=v_w_k, w_v=v_w_v, norm_b=v_norm_b, w_in_b=v_w_in_b, w_out_b=v_w_out_b,
                     norm_f=v_norm_f)
    names = list(weights)
    d = x.shape[-1]
    c_idx = lax.axis_index("c").astype(jnp.int32).reshape(1)
    s_me = 2 * lax.axis_index("x") + lax.axis_index("y")
    s_idx = s_me.astype(jnp.int32).reshape(1)

    def as_lbrc(name):
        a = weights[name]
        if name == "w_grp_a":
            return a
        if a.ndim == 2:
            return a.reshape(1, 1, *a.shape)
        return a.reshape(a.shape[0], 1, *a.shape[1:])

    n_a, n_b = norm_a.shape[0], norm_b.shape[0]
    group_weights = {**{f"a{i}": [("w_in_a", i), ("w_grp_a", i), ("w_out_a", i)] for i in range(n_a)},
                     "kv": [("w_k", 0), ("w_v", 0)],
                     **{f"b{i}": [("w_in_b", i), ("w_out_b", i)] for i in range(n_b)}}
    group_order = [f"a{i}" for i in range(n_a)] + ["kv"] + [f"b{i}" for i in range(n_b)]
    slots, slot_groups = [], []
    for group in group_order:
        slot_groups.append(list(range(len(slots), len(slots) + len(group_weights[group]))))
        slots += [_cast_into_slot(f"cast_{name}{l}", as_lbrc(name), l, s_idx) for name, l in group_weights[group]]
    small_full, started = {}, {}

    def start_group(gi, carry=()):
        sems, bufs, carry = _gather_start(f"gather_start_{group_order[gi]}", [slots[t] for t in slot_groups[gi]], carry)
        started[gi] = (sems, bufs)
        return carry

    start_group(0)

    def gathered_form(name, g):
        if name in ("w_in_a", "w_in_b"):
            return g[0]
        if name == "w_grp_a":
            return g.reshape(g.shape[0], -1, g.shape[-1])
        return g.reshape(-1, g.shape[-1])

    def fetch(group, after):
        if group == "head":
            return {"norm_f": (0, norm_f)}
        gi = group_order.index(group)
        sems, bufs = started[gi]
        bufs = _gather_wait(f"gather_wait_{group}", sems, bufs, after)
        bufs, small_g = _gather_forward(f"gather_forward_{group}", bufs, [norm_a, scale_a] if gi == 0 else [])
        out = {name: (l, gathered_form(name, g)) for (name, l), g in zip(group_weights[group], bufs)}
        if gi == 0:
            for name, g in zip(("norm_a", "scale_a"), small_g):
                small_full[name] = g.transpose(1, 0, 2).reshape(g.shape[1], -1)
        layer = group_weights[group][0][1]
        if group.startswith("a"):
            gain_name, gain = "norm_a", small_full["norm_a"][layer]
            out.update(scale_a=(layer, small_full["scale_a"][layer]))
        elif group == "kv":
            gain_name, gain = "norm_kv", norm_kv
        else:
            gain_name, gain = "norm_b", norm_b[layer]
        gain = gain.reshape(1, -1)
        ahead = [gi + 1] + ([gi + 2] if gi + 2 < len(group_order) and group_order[gi + 1] == "kv" else [])
        for gj in ahead:
            if gj < len(group_order) and gj not in started:
                (gain,) = start_group(gj, (gain,))
        out[gain_name] = (layer, gain)
        return out

    in_flight = []

    def emit(group, grads_of, carry):
        keys = list(grads_of)
        recv1 = _exchange_halves(f"grad_exchange_{group}", [grads_of[k][1] for k in keys])
        parts = [_pair_add(f"pair_add_{k}{grads_of[k][0]}", grads_of[k][1], r, c_idx) for k, r in zip(keys, recv1)]
        sems, parts, lands, carry = _scatter_start(f"grad_scatter_start_{group}", parts, tuple(carry))
        in_flight.append((group, [(k, grads_of[k][0]) for k in keys], sems, parts, lands))
        return carry

    loss_vec, grad_x, small = _local_step(x[0], loss_target[0], n_a, n_b, fetch, emit)

    small_order = [("norm_a", i) for i in range(n_a)] + [("scale_a", i) for i in range(n_a)] + [("norm_kv", 0)] + \
                  [("norm_b", i) for i in range(norm_b.shape[0])] + [("norm_f", 0)]
    pad = lambda vec: jnp.pad(vec, ((0, PAD_ROWS - 1), (0, 0)))
    packed = jnp.concatenate([pad(loss_vec)] + [pad(small[n][i]) for n, i in small_order], axis=0)
    totals = _sum_devices("small_sum", _allgather_small(packed))
    loss = 0.5 * jnp.sum(totals[0]) / d
    small_tot = {}
    for j, (n, i) in enumerate(small_order):
        small_tot.setdefault(n, []).append(totals[PAD_ROWS * (j + 1)])
    grads = {}
    shard_w = norm_a.shape[1]
    for n in ("norm_a", "scale_a"):
        full = jnp.stack(small_tot[n])
        grads[n] = lax.dynamic_slice_in_dim(full, s_me * shard_w, shard_w, axis=1)
    grads["norm_kv"] = small_tot["norm_kv"][0]
    grads["norm_b"] = jnp.stack(small_tot["norm_b"])
    grads["norm_f"] = small_tot["norm_f"][0]

    sc_idx = jnp.concatenate([s_idx, c_idx])
    fulls = {name: None for name in BIG_WEIGHTS}
    for group, keys, sems, parts, lands in in_flight:
        parts, lands = _scatter_wait(f"grad_scatter_wait_{group}", sems, parts, lands)
        for (name, i), p, r in zip(keys, parts, lands):
            n_layers = 1 if weights[name].ndim == 2 else weights[name].shape[0]
            fulls[name] = _final_add(f"final_add_{name}{i}", p, r, sc_idx, i, n_layers, into=fulls[name])
    shared = _share_halves([fulls[name] for name in BIG_WEIGHTS])
    for name, g in zip(BIG_WEIGHTS, shared):
        grads[name] = g.reshape(weights[name].shape)

    deltas, new_m, new_v = {}, {}, {}
    for n in names:
        shape = weights[n].shape
        as2d = (lambda a: a.reshape(1, -1)) if len(shape) == 1 else (lambda a: a)
        dl, mn, vn = _adamw(f"adamw_{n}", as2d(weights[n]), as2d(grads[n]), as2d(moments_m[n]), as2d(moments_v[n]))
        deltas[n], new_m[n], new_v[n] = dl.reshape(shape), mn.reshape(shape), vn.reshape(shape)

    return (loss, grad_x[None], *[grads[n] for n in names], *[deltas[n] for n in names],
            *[new_m[n] for n in names], *[new_v[n] for n in names])
```

```python
import functools
import math

import jax
import jax.numpy as jnp
from jax import lax
from jax.experimental import pallas as pl
from jax.experimental.pallas import tpu as pltpu

F32 = jnp.float32
BF16 = jnp.bfloat16

HEAD_DIM = 128
POOL_WINDOWS = (2, 4, 8, 16)
DILATED_PAIRS = ((128, 1), (512, 4), (2048, 16))
ROPE_THETA = 10000.0
RMS_EPS = 1e-6
NEG_INF = -1e30
N_CHIPS = 4

ADAM_LR = 0.001
ADAM_B1 = 0.9
ADAM_B2 = 0.999
ADAM_EPS = 1e-08
ADAM_WD = 0.01
ADAM_STEP = 10

VMEM_LIMIT_BYTES = 56 * 1024 * 1024
MESH = pl.DeviceIdType.MESH
ANY = pl.BlockSpec(memory_space=pl.ANY)


def _tile(n, pref):
    t = min(n, pref)
    assert n % t == 0, (n, pref)
    return t


def _params(sem=None):
    return pltpu.CompilerParams(dimension_semantics=sem, vmem_limit_bytes=VMEM_LIMIT_BYTES)


def _mm(name, a, b, *, grid2, nk, a_blk, a_map, b_blk, b_map, outs, dims, epi=None, epi_in=(), epi_specs=(),
        acc_shape=None, epi_scratch=(), into=None):
    n_epi, n_out = len(epi_in), len(outs)

    def body(*refs):
        a_ref, b_ref = refs[0], refs[1]
        e_refs = refs[2:2 + n_epi]
        first_out = 2 + n_epi + (0 if into is None else 1)
        o_refs = refs[first_out:first_out + n_out]
        s_refs = refs[first_out + n_out + (0 if nk == 1 else 1):]

        def contrib():
            a_val = a_ref[...]
            if a_val.ndim == 3:
                a_val = a_val.reshape(-1, a_val.shape[-1])
            return lax.dot_general(a_val, b_ref[...], (dims, ((), ())), preferred_element_type=F32)

        def finish(acc):
            if epi is None:
                o_refs[0][...] = acc.reshape(o_refs[0].shape).astype(o_refs[0].dtype)
            else:
                epi(acc, e_refs, o_refs, s_refs)

        if nk == 1:
            finish(contrib())
        else:
            acc_ref = refs[first_out + n_out]
            k = pl.program_id(2)

            @pl.when(k == 0)
            def _():
                acc_ref[...] = contrib()

            @pl.when(k > 0)
            def _():
                acc_ref[...] += contrib()

            @pl.when(k == nk - 1)
            def _():
                finish(acc_ref[...])

    scratch = ([] if nk == 1 else [pltpu.VMEM(acc_shape, F32)]) + list(epi_scratch)
    extra_in, extra_specs, aliases = (), (), {}
    if into is not None:
        extra_in, extra_specs, aliases = (into[0],), (ANY,), {2 + n_epi: into[1]}
    res = pl.pallas_call(
        body, name=name, grid=(grid2[0], grid2[1], nk),
        in_specs=[pl.BlockSpec(a_blk, a_map), pl.BlockSpec(b_blk, b_map), *epi_specs, *extra_specs],
        out_specs=[pl.BlockSpec(blk, imap) for _, blk, imap, _ in outs],
        out_shape=[jax.ShapeDtypeStruct(shape, dtype) for shape, _, _, dtype in outs],
        scratch_shapes=scratch, input_output_aliases=aliases,
        compiler_params=_params(("parallel", "parallel", "arbitrary")),
    )(a, b, *epi_in, *extra_in)
    return res[0] if n_out == 1 else tuple(res)


NN = ((1,), (0,))
NT = ((1,), (1,))
TN = ((0,), (0,))


def _rope_apply(t, cos, sin):
    return t * cos + pltpu.roll(t, HEAD_DIM // 2, 1) * sin


def _epi_add(acc, e_refs, o_refs, s_refs):
    o_refs[0][...] = (acc + e_refs[0][...]).astype(o_refs[0].dtype)


def _col_blocks(width):
    return [slice(c * HEAD_DIM, (c + 1) * HEAD_DIM) for c in range(width // HEAD_DIM)]


def _col_scratch(rows, width):
    return pltpu.VMEM((width // HEAD_DIM, rows, HEAD_DIM), F32)


def _to_residue_major(o_ref, scr, d, sl):
    if d == 1:
        o_ref[0, :, sl] = scr[...].astype(o_ref.dtype)
        return
    rows = scr.shape[0] // d
    for r in range(d):
        o_ref[r, :, sl] = scr[pl.ds(r, rows, stride=d), :].astype(o_ref.dtype)


def _from_residue_major(i_ref, scr, d, sl):
    if d == 1:
        return i_ref[0, :, sl].astype(F32)
    rows = i_ref.shape[1]
    for r in range(d):
        scr[pl.ds(r, rows, stride=d), :] = i_ref[r, :, sl].astype(F32)
    return scr[...]


def _make_epi_orders(dils, rope_scale):
    def epi(acc, e_refs, o_refs, s_refs):
        if rope_scale is not None:
            cos = e_refs[0][...]
            sin = e_refs[1][...]
        for c, sl in enumerate(_col_blocks(acc.shape[1])):
            scr = s_refs[0].at[c]
            scr[...] = acc[:, sl] if rope_scale is None else _rope_apply(acc[:, sl], cos, sin) * rope_scale
            for o_ref, d in zip(o_refs, dils):
                _to_residue_major(o_ref, scr, d, sl)
    return epi


def _make_epi_token_order(d, has_add):
    def epi(acc, e_refs, o_refs, s_refs):
        o_ref = o_refs[0]
        if d == 1:
            o_ref[...] = acc + e_refs[0][...] if has_add else acc
            return
        rows = acc.shape[0] // d
        for c, sl in enumerate(_col_blocks(acc.shape[1])):
            scr = s_refs[0].at[c]
            for r in range(d):
                scr[pl.ds(r, rows, stride=d), :] = acc[r * rows:(r + 1) * rows, sl]
            o_ref[:, sl] = scr[...] + e_refs[0][:, sl] if has_add else scr[...]
    return epi


def _mm_act_w(name, a, w, *, out_dtype=BF16, add=None, rope=None, n_first=0, n_cols=None, dils=None):
    s_len, k_len = a.shape
    bm = _tile(s_len, 1024)
    epi, epi_in, epi_specs, epi_scratch = None, (), (), ()
    if w.ndim == 3:
        ns, _, c = w.shape
        ns_used = ns if n_cols is None else n_cols
        bn = _tile(c, 1024)
        sub = c // bn
        grid2 = (ns_used * sub, s_len // bm)
        b_blk, b_map = (None, k_len, bn), (lambda j, i, k: (j // sub + n_first, 0, j % sub))
        n_len = ns_used * c
    else:
        n_len = w.shape[1]
        bn = _tile(n_len, 1024)
        grid2 = (n_len // bn, s_len // bm)
        b_blk, b_map = (k_len, bn), (lambda j, i, k: (0, j))
    if add is not None:
        epi, epi_in = _epi_add, (add,)
        epi_specs = (pl.BlockSpec((bm, bn), lambda j, i, k: (i, j)),)
    outs = [((s_len, n_len), (bm, bn), lambda j, i, k: (i, j), out_dtype)]
    if dils is not None:
        if rope is not None:
            epi_in = rope[:2]
            epi_specs = (pl.BlockSpec((bm, HEAD_DIM), lambda j, i, k: (i, 0)),) * 2
        epi = _make_epi_orders(dils, None if rope is None else rope[2])
        epi_scratch = (_col_scratch(bm, bn),)
        outs = [((d, s_len // d, n_len), (d, bm // d, bn), lambda j, i, k: (0, i, j), BF16) for d in dils]
    res = _mm(name, a, w, grid2=grid2, nk=1, a_blk=(bm, k_len), a_map=lambda j, i, k: (i, 0),
              b_blk=b_blk, b_map=b_map, outs=outs, dims=NN, epi=epi, epi_in=epi_in, epi_specs=epi_specs,
              epi_scratch=epi_scratch)
    return (res,) if dils is not None and len(dils) == 1 else res


def _mm_grad_act(name, dy, w, *, add=None, slot=None):
    if slot is not None:
        d = 1 if dy.ndim == 2 else dy.shape[0]
        s_len = dy.shape[-2] * d
        _, k_len, c = w.shape
        bm, bn = _tile(s_len, 1024), _tile(k_len, 1024)
        a_blk, a_map = ((bm, c), lambda j, i, k: (i, 0)) if dy.ndim == 2 else ((d, bm // d, c), lambda j, i, k: (0, i, 0))
        epi_in = () if add is None else (add,)
        return _mm(name, dy, w, grid2=(k_len // bn, s_len // bm), nk=1, a_blk=a_blk, a_map=a_map,
                   b_blk=(None, bn, c), b_map=lambda j, i, k: (slot, j, 0),
                   outs=[((s_len, k_len), (bm, bn), lambda j, i, k: (i, j), F32)], dims=NT,
                   epi=_make_epi_token_order(d, add is not None), epi_in=epi_in,
                   epi_specs=(pl.BlockSpec((bm, bn), lambda j, i, k: (i, j)),) * len(epi_in),
                   epi_scratch=(_col_scratch(bm, bn),) if d > 1 else ())
    s_len, n_len = dy.shape
    bm = _tile(s_len, 1024)
    if w.ndim == 3:
        ns, k_len, c = w.shape
        bk, nk = c, ns
        bn = _tile(k_len, 1024)
        b_blk, b_map = (None, bn, c), (lambda j, i, k: (k, j, 0))
    else:
        k_len = w.shape[0]
        bk = _tile(n_len, 1024)
        nk = n_len // bk
        bn = _tile(k_len, 1024)
        b_blk, b_map = (bn, bk), (lambda j, i, k: (j, k))
    epi, epi_in, epi_specs = None, (), ()
    if add is not None:
        epi, epi_in = _epi_add, (add,)
        epi_specs = (pl.BlockSpec((bm, bn), lambda j, i, k: (i, j)),)
    return _mm(name, dy, w, grid2=(k_len // bn, s_len // bm), nk=nk, a_blk=(bm, bk), a_map=lambda j, i, k: (i, k),
               b_blk=b_blk, b_map=b_map, outs=[((s_len, k_len), (bm, bn), lambda j, i, k: (i, j), F32)],
               dims=NT, epi=epi, epi_in=epi_in, epi_specs=epi_specs, acc_shape=(bm, bn))


def _mm_grad_w(name, a, dy, *, col_shards=None, slot=None, into=None):
    s_len, k_len = a.shape
    n_len = dy.shape[1]
    bk = _tile(s_len, 1024)
    bm = _tile(k_len, 1024)
    if slot is not None:
        bn = _tile(n_len, 1024)
        out = ((col_shards, k_len, n_len), (None, bm, bn), lambda j, i, k: (slot, i, j), BF16)
    elif col_shards:
        c = n_len // col_shards
        bn = _tile(c, 1024)
        sub = c // bn
        out = ((col_shards, k_len, c), (None, bm, bn), lambda j, i, k: (j // sub, i, j % sub), BF16)
    else:
        bn = _tile(n_len, 1024)
        out = ((k_len, n_len), (bm, bn), lambda j, i, k: (i, j), BF16)
    return _mm(name, a, dy, grid2=(n_len // bn, k_len // bm), nk=s_len // bk,
               a_blk=(bk, bm), a_map=lambda j, i, k: (k, i), b_blk=(bk, bn), b_map=lambda j, i, k: (k, j),
               outs=[out], dims=TN, acc_shape=(bm, bn), into=None if into is None else (into, 0))


def _mm_grp_fwd(name, pooled, wg):
    s_len, e = pooled.shape
    ng, g, _ = wg.shape
    bm = _tile(s_len, 1024)
    return _mm(name, pooled, wg, grid2=(ng, s_len // bm), nk=1, a_blk=(bm, g), a_map=lambda j, i, k: (i, j),
               b_blk=(None, g, g), b_map=lambda j, i, k: (j, 0, 0),
               outs=[((s_len, e), (bm, g), lambda j, i, k: (i, j), F32)], dims=NN)


def _mm_grp_grad_act(name, dy, wg):
    s_len, e = dy.shape
    ng, g, _ = wg.shape
    bm = _tile(s_len, 1024)
    return _mm(name, dy, wg, grid2=(ng, s_len // bm), nk=1, a_blk=(bm, g), a_map=lambda j, i, k: (i, j),
               b_blk=(None, g, g), b_map=lambda j, i, k: (j, 0, 0),
               outs=[((s_len, e), (bm, g), lambda j, i, k: (i, j), F32)], dims=NT)


def _mm_grp_grad_w(name, pooled, dy, ng):
    s_len, e = pooled.shape
    g = e // ng
    bk = _tile(s_len, 1024)
    return _mm(name, pooled, dy, grid2=(ng, 1), nk=s_len // bk, a_blk=(bk, g), a_map=lambda j, i, k: (k, j),
               b_blk=(bk, g), b_map=lambda j, i, k: (k, j),
               outs=[((N_CHIPS, ng, g // N_CHIPS, g), (N_CHIPS, None, g // N_CHIPS, g), lambda j, i, k: (0, j, 0, 0), BF16)],
               dims=TN, acc_shape=(g, g))


def _row_spec(bs, width, col=0):
    return pl.BlockSpec((bs, width), lambda i: (i, col))


def _vec_spec(width):
    return pl.BlockSpec((1, width), lambda i: (0, 0))


def _rows_call(body, name, s_len, in_specs, out_specs, out_shape, bs, aliases=None, sequential=False):
    return pl.pallas_call(
        body, name=name, grid=(s_len // bs,), in_specs=in_specs, out_specs=out_specs, out_shape=out_shape,
        input_output_aliases=aliases or {},
        compiler_params=_params(("arbitrary",) if sequential else ("parallel",)))


def _accumulate(ref, part):
    i = pl.program_id(0)

    @pl.when(i == 0)
    def _():
        ref[...] = part

    @pl.when(i > 0)
    def _():
        ref[...] += part


def _rms_scale(xf):
    return lax.rsqrt(jnp.mean(xf * xf, axis=-1, keepdims=True) + RMS_EPS)


def _res_spec(dil, bs, width):
    return pl.BlockSpec((dil, bs // dil, width), lambda i: (0, i, 0))


def _res_shape(dil, s_len, width, dtype):
    return jax.ShapeDtypeStruct((dil, s_len // dil, width), dtype)


def _rmsnorm_fwd(name, x, gain, dils=()):
    s_len, d = x.shape
    bs = _tile(s_len, 256)

    def body(x_ref, g_ref, h_ref, *rest):
        xf = x_ref[...]
        h = (xf * _rms_scale(xf)) * g_ref[...]
        h_ref[...] = h.astype(BF16)
        if dils:
            for c, sl in enumerate(_col_blocks(d)):
                scr = rest[-1].at[c]
                scr[...] = h[:, sl]
                for o_ref, dil in zip(rest[:-1], dils):
                    _to_residue_major(o_ref, scr, dil, sl)

    res = pl.pallas_call(
        body, name=name, grid=(s_len // bs,), in_specs=[_row_spec(bs, d), _vec_spec(d)],
        out_specs=[_row_spec(bs, d)] + [_res_spec(dil, bs, d) for dil in dils],
        out_shape=[jax.ShapeDtypeStruct((s_len, d), BF16)] + [_res_shape(dil, s_len, d, BF16) for dil in dils],
        scratch_shapes=[_col_scratch(bs, d)] if dils else [],
        compiler_params=_params(("parallel",)))(x, gain)
    return res[0] if not dils else tuple(res)


def _rmsnorm_bwd(name, x, gain, dh, dres):
    s_len, d = x.shape
    bs = _tile(s_len, 256)

    def body(x_ref, g_ref, dh_ref, dres_ref, dx_ref, dxb_ref, dg_ref):
        xf = x_ref[...]
        r = _rms_scale(xf)
        xh = xf * r
        dh_f = dh_ref[...]
        t = dh_f * g_ref[...]
        dx = dres_ref[...] + r * (t - xh * jnp.mean(t * xh, axis=-1, keepdims=True))
        dx_ref[...] = dx
        dxb_ref[...] = dx.astype(BF16)
        _accumulate(dg_ref, jnp.sum(dh_f * xh, axis=0, keepdims=True))

    return _rows_call(
        body, name, s_len,
        [_row_spec(bs, d), _vec_spec(d), _row_spec(bs, d), _row_spec(bs, d)],
        [_row_spec(bs, d), _row_spec(bs, d), _vec_spec(d)],
        [jax.ShapeDtypeStruct((s_len, d), F32), jax.ShapeDtypeStruct((s_len, d), BF16),
         jax.ShapeDtypeStruct((1, d), F32)], bs, sequential=True)(x, gain, dh, dres)


def _loss_head(name, x, gain, target):
    s_len, d = x.shape
    bs = _tile(s_len, 256)

    def body(x_ref, g_ref, t_ref, lv_ref, dx_ref, dxb_ref, dg_ref):
        xf = x_ref[...]
        r = _rms_scale(xf)
        xh = xf * r
        err = xh * g_ref[...] - t_ref[...]
        dy = err * (1.0 / d)
        t = dy * g_ref[...]
        dx = r * (t - xh * jnp.mean(t * xh, axis=-1, keepdims=True))
        dx_ref[...] = dx
        dxb_ref[...] = dx.astype(BF16)
        _accumulate(lv_ref, jnp.sum(err * err, axis=0, keepdims=True))
        _accumulate(dg_ref, jnp.sum(dy * xh, axis=0, keepdims=True))

    return _rows_call(
        body, name, s_len, [_row_spec(bs, d), _vec_spec(d), _row_spec(bs, d)],
        [_vec_spec(d), _row_spec(bs, d), _row_spec(bs, d), _vec_spec(d)],
        [jax.ShapeDtypeStruct((1, d), F32), jax.ShapeDtypeStruct((s_len, d), F32),
         jax.ShapeDtypeStruct((s_len, d), BF16), jax.ShapeDtypeStruct((1, d), F32)],
        bs, sequential=True)(x, gain, target)


def _sigmoid(g):
    return 1.0 / (1.0 + jnp.exp(-g))


def _gate_a_fwd(name, ypre, proj, scale):
    s_len, e = ypre.shape
    bs = _tile(s_len, 256)

    def body(y_ref, g_ref, sc_ref, z_ref):
        g = g_ref[...]
        z_ref[...] = (y_ref[...] * sc_ref[...] * (g * _sigmoid(g))).astype(BF16)

    return _rows_call(body, name, s_len, [_row_spec(bs, e), _row_spec(bs, e, 1), _vec_spec(e)], _row_spec(bs, e),
                      jax.ShapeDtypeStruct((s_len, e), BF16), bs)(ypre, proj, scale)


def _gate_a_bwd(name, dz, ypre, proj, scale):
    s_len, e = ypre.shape
    bs = _tile(s_len, 256)

    def body(dz_ref, y_ref, g_ref, sc_ref, dy_ref, dproj_ref, dsc_ref):
        g = g_ref[...]
        sg = _sigmoid(g)
        silu = g * sg
        dz_f = dz_ref[...]
        ypre_f = y_ref[...]
        dys = dz_f * silu
        dy_ref[...] = (dys * sc_ref[...]).astype(BF16)
        dproj_ref[...] = (dz_f * (ypre_f * sc_ref[...]) * (sg * (1.0 + g * (1.0 - sg)))).astype(BF16)
        _accumulate(dsc_ref, jnp.sum(dys * ypre_f, axis=0, keepdims=True))

    return _rows_call(
        body, name, s_len, [_row_spec(bs, e), _row_spec(bs, e), _row_spec(bs, e, 1), _vec_spec(e)],
        [_row_spec(bs, e), _row_spec(bs, e, 1), _vec_spec(e)],
        [jax.ShapeDtypeStruct((s_len, e), BF16), jax.ShapeDtypeStruct((s_len, 2 * e), BF16),
         jax.ShapeDtypeStruct((1, e), F32)], bs, sequential=True)(dz, ypre, proj, scale)


def _merge_gate_fwd(name, outs, lses, gate, dils):
    s_len, e = gate.shape
    bs = _tile(s_len, 256)
    n = len(outs)

    def body(*refs):
        o_refs, l_refs, g_ref = refs[:n], refs[n:2 * n], refs[2 * n]
        m_ref, lj_ref, z_ref = refs[2 * n + 1:2 * n + 4]
        scratch = refs[2 * n + 4]
        for c, sl in enumerate(_col_blocks(e)):
            ls = [_from_residue_major(r, scratch.at[2 * j, c], dil, sl) for j, (r, dil) in enumerate(zip(l_refs, dils))]
            os_ = [_from_residue_major(r, scratch.at[2 * j + 1, c], dil, sl) for j, (r, dil) in enumerate(zip(o_refs, dils))]
            mx = functools.reduce(jnp.maximum, ls)
            ws = [jnp.exp(l - mx) for l in ls]
            den = functools.reduce(lambda a, b: a + b, ws)
            merged = functools.reduce(lambda a, b: a + b, [w * o for w, o in zip(ws, os_)]) / den
            g = g_ref[:, sl]
            m_ref[:, sl] = merged.astype(BF16)
            lj_ref[:, sl] = mx + jnp.log(den)
            z_ref[:, sl] = (merged * (g * _sigmoid(g))).astype(BF16)

    spec = _row_spec(bs, e)
    res_specs = [_res_spec(dil, bs, e) for dil in dils]
    return pl.pallas_call(
        body, name=name, grid=(s_len // bs,), in_specs=res_specs + res_specs + [spec], out_specs=[spec] * 3,
        out_shape=[jax.ShapeDtypeStruct((s_len, e), BF16), jax.ShapeDtypeStruct((s_len, e), F32),
                   jax.ShapeDtypeStruct((s_len, e), BF16)],
        scratch_shapes=[pltpu.VMEM((2 * n, e // HEAD_DIM, bs, HEAD_DIM), F32)],
        compiler_params=_params(("parallel",)))(*outs, *lses, gate)


def _gate_b_bwd(name, dz, merged, gate, lse, dils):
    s_len, e = gate.shape
    bs = _tile(s_len, 256)
    n = len(dils)

    def body(dz_ref, m_ref, g_ref, l_ref, dg_ref, *rest):
        out_refs, scratch = rest[:3 * n], rest[3 * n]
        for c, sl in enumerate(_col_blocks(e)):
            g = g_ref[:, sl]
            sg = _sigmoid(g)
            dz_f = dz_ref[:, sl]
            merged = m_ref[:, sl].astype(F32)
            dmerged = dz_f * (g * sg)
            dg_ref[:, sl] = (dz_f * merged * (sg * (1.0 + g * (1.0 - sg)))).astype(BF16)
            values = (dmerged, l_ref[:, sl],
                      jnp.broadcast_to(jnp.sum(dmerged * merged, axis=-1, keepdims=True), (bs, HEAD_DIM)))
            for t, val in enumerate(values):
                scr = scratch.at[t, c]
                scr[...] = val
                for j, dil in enumerate(dils):
                    _to_residue_major(out_refs[3 * j + t], scr, dil, sl)

    spec = _row_spec(bs, e)
    out_specs, out_shape = [spec], [jax.ShapeDtypeStruct((s_len, e), BF16)]
    for dil in dils:
        out_specs += [_res_spec(dil, bs, e)] * 3
        out_shape += [_res_shape(dil, s_len, e, BF16), _res_shape(dil, s_len, e, F32), _res_shape(dil, s_len, e, F32)]
    res = pl.pallas_call(
        body, name=name, grid=(s_len // bs,), in_specs=[spec] * 4, out_specs=out_specs, out_shape=out_shape,
        scratch_shapes=[pltpu.VMEM((3, e // HEAD_DIM, bs, HEAD_DIM), F32)],
        compiler_params=_params(("parallel",)))(dz, merged, gate, lse)
    return res[0], [tuple(res[1 + 3 * j:4 + 3 * j]) for j in range(n)]


def _kv_grad_prep(name, dk_accs, dv_accs, dils, cos, sin_inv):
    n = len(dils)
    e = dk_accs[0].shape[-1]
    s_len = dk_accs[0].shape[0] * dk_accs[0].shape[1]
    bs = _tile(s_len, 256)

    def body(*refs):
        dk_refs, dv_refs = refs[:n], refs[n:2 * n]
        c_ref, s_ref, dkb_ref, dvb_ref, scratch = refs[2 * n:]
        cos_t, sin_t = c_ref[...], s_ref[...]
        add = lambda a, b: a + b
        for c, sl in enumerate(_col_blocks(e)):
            dk = functools.reduce(add, [_from_residue_major(r, scratch.at[j, c], dil, sl)
                                        for j, (r, dil) in enumerate(zip(dk_refs, dils))])
            dkb_ref[:, sl] = _rope_apply(dk, cos_t, sin_t).astype(BF16)
            dv = functools.reduce(add, [_from_residue_major(r, scratch.at[n + j, c], dil, sl)
                                        for j, (r, dil) in enumerate(zip(dv_refs, dils))])
            dvb_ref[:, sl] = dv.astype(BF16)

    spec, rspec = _row_spec(bs, e), _row_spec(bs, HEAD_DIM)
    res_specs = [_res_spec(dil, bs, e) for dil in dils]
    return pl.pallas_call(
        body, name=name, grid=(s_len // bs,), in_specs=res_specs + res_specs + [rspec, rspec], out_specs=[spec, spec],
        out_shape=[jax.ShapeDtypeStruct((s_len, e), BF16)] * 2,
        scratch_shapes=[pltpu.VMEM((2 * n, e // HEAD_DIM, bs, HEAD_DIM), F32)],
        compiler_params=_params(("parallel",)))(*dk_accs, *dv_accs, cos, sin_inv)


def _pool_cols(e):
    return _tile(e // len(POOL_WINDOWS), 256)


def _window_sum(val, grp, s_len, forward):
    rows = lax.broadcasted_iota(jnp.int32, val.shape, 0)
    acc = val
    for level in range(len(POOL_WINDOWS)):
        step = 1 << level
        if forward:
            shifted = jnp.where(rows >= step, pltpu.roll(acc, step, 0), 0.0)
        else:
            shifted = jnp.where(rows < s_len - step, pltpu.roll(acc, s_len - step, 0), 0.0)
        acc = jnp.where(level <= grp, acc + shifted, acc)
    return acc


def _window_count(shape, grp):
    rows = lax.broadcasted_iota(jnp.int32, shape, 0)
    return jnp.minimum(rows + 1, jnp.left_shift(2, grp)).astype(F32)


def _pool_fwd(name, proj):
    s_len, e2 = proj.shape
    e = e2 // 2
    cb = _pool_cols(e)
    per_grp = e // len(POOL_WINDOWS) // cb
    assert POOL_WINDOWS == tuple(2 << g for g in range(len(POOL_WINDOWS)))

    def body(u_ref, p_ref):
        grp = pl.program_id(0)
        u = u_ref[...]
        total = _window_sum(u, grp, s_len, True)
        p_ref[...] = (total / _window_count(u.shape, grp) - u).astype(BF16)

    spec = pl.BlockSpec((s_len, cb), lambda g, c: (0, g * per_grp + c))
    return pl.pallas_call(
        body, name=name, grid=(len(POOL_WINDOWS), per_grp), in_specs=[spec], out_specs=spec,
        out_shape=jax.ShapeDtypeStruct((s_len, e), BF16), compiler_params=_params(("parallel", "parallel")))(proj)


def _pool_bwd(name, dpooled, dproj):
    s_len, e = dpooled.shape
    cb = _pool_cols(e)
    per_grp = e // len(POOL_WINDOWS) // cb

    def body(dp_ref, _, du_ref):
        grp = pl.program_id(0)
        dp = dp_ref[...]
        total = _window_sum(dp / _window_count(dp.shape, grp), grp, s_len, False)
        du_ref[...] = (total - dp).astype(BF16)

    spec = pl.BlockSpec((s_len, cb), lambda g, c: (0, g * per_grp + c))
    return pl.pallas_call(
        body, name=name, grid=(len(POOL_WINDOWS), per_grp), in_specs=[spec, ANY], out_specs=spec,
        out_shape=jax.ShapeDtypeStruct(dproj.shape, BF16), input_output_aliases={1: 0},
        compiler_params=_params(("parallel", "parallel")))(dpooled, dproj)


def _band_masks(nb, first):
    row = lax.broadcasted_iota(jnp.int32, (nb, nb), 0)
    col = lax.broadcasted_iota(jnp.int32, (nb, nb), 1)
    return col >= row + jnp.where(first, 2 * nb, 0), col <= row


def _dot(a, b, dims):
    return lax.dot_general(a, b, (dims, ((), ())), preferred_element_type=F32)


def _attn_fwd(name, window, q, k, v):
    dil, m, e = k.shape
    nb = window // dil
    nblk = m // nb
    heads = e // HEAD_DIM

    def body(q_ref, kp_ref, kc_ref, vp_ref, vc_ref, o_ref, l_ref):
        mask_p, mask_c = _band_masks(nb, pl.program_id(1) == 0)
        cols = _col_blocks(e)
        s_p = [jnp.where(mask_p, _dot(q_ref[:, sl], kp_ref[:, sl], NT), NEG_INF) for sl in cols]
        s_c = [jnp.where(mask_c, _dot(q_ref[:, sl], kc_ref[:, sl], NT), NEG_INF) for sl in cols]
        mx = [jnp.maximum(jnp.max(a, axis=-1, keepdims=True), jnp.max(b, axis=-1, keepdims=True))
              for a, b in zip(s_p, s_c)]
        p_p = [jnp.exp(a - m) for a, m in zip(s_p, mx)]
        p_c = [jnp.exp(a - m) for a, m in zip(s_c, mx)]
        den = [jnp.sum(a, axis=-1, keepdims=True) + jnp.sum(b, axis=-1, keepdims=True) for a, b in zip(p_p, p_c)]
        for h, sl in enumerate(cols):
            out = _dot(p_p[h].astype(BF16), vp_ref[:, sl], NN) + _dot(p_c[h].astype(BF16), vc_ref[:, sl], NN)
            o_ref[:, sl] = (out / den[h]).astype(BF16)
            l_ref[:, sl] = jnp.broadcast_to(mx[h] + jnp.log(den[h]), (nb, HEAD_DIM))

    blk = (None, nb, e)
    prev = lambda r, n: (r, jnp.maximum(n - 1, 0), 0)
    cur = lambda r, n: (r, n, 0)
    return pl.pallas_call(
        body, name=name, grid=(dil, nblk),
        in_specs=[pl.BlockSpec(blk, cur), pl.BlockSpec(blk, prev), pl.BlockSpec(blk, cur), pl.BlockSpec(blk, prev),
                  pl.BlockSpec(blk, cur)],
        out_specs=[pl.BlockSpec(blk, cur), pl.BlockSpec(blk, cur)],
        out_shape=[jax.ShapeDtypeStruct((dil, m, e), BF16), jax.ShapeDtypeStruct((dil, m, e), F32)],
        compiler_params=_params(("parallel", "arbitrary")),
    )(q, k, k, v, v)


def _attn_bwd(name, window, scale, q, k, v, dout, lse, delta, cos, sin_inv, dk_acc, dv_acc):
    dil, m, e = k.shape
    nb = window // dil
    nblk = m // nb
    heads = e // HEAD_DIM

    def body(q_ref, kp_ref, kc_ref, vp_ref, vc_ref, do_ref, l_ref, dl_ref, c_ref, s_ref, dki_ref, dvi_ref,
             dq_ref, dko_ref, dvo_ref, ck_ref, cv_ref):
        n = pl.program_id(1)

        @pl.when(n == 0)
        def _():
            ck_ref[...] = jnp.zeros_like(ck_ref)
            cv_ref[...] = jnp.zeros_like(cv_ref)

        @pl.when(n < nblk)
        def _():
            mask_p, mask_c = _band_masks(nb, n == 0)
            cos_t, sin_t = c_ref[...], s_ref[...]
            cols = _col_blocks(e)
            stat = lambda ref, sl: ref[:, sl] if nb == HEAD_DIM else ref[:, sl][:, :1]
            s_p = [_dot(q_ref[:, sl], kp_ref[:, sl], NT) for sl in cols]
            s_c = [_dot(q_ref[:, sl], kc_ref[:, sl], NT) for sl in cols]
            dp_p = [_dot(do_ref[:, sl], vp_ref[:, sl], NT) for sl in cols]
            dp_c = [_dot(do_ref[:, sl], vc_ref[:, sl], NT) for sl in cols]
            p_p = [jnp.where(mask_p, jnp.exp(s - stat(l_ref, sl)), 0.0) for s, sl in zip(s_p, cols)]
            p_c = [jnp.where(mask_c, jnp.exp(s - stat(l_ref, sl)), 0.0) for s, sl in zip(s_c, cols)]
            ds_p = [(p * (dp - stat(dl_ref, sl))).astype(BF16) for p, dp, sl in zip(p_p, dp_p, cols)]
            ds_c = [(p * (dp - stat(dl_ref, sl))).astype(BF16) for p, dp, sl in zip(p_c, dp_c, cols)]
            for h, sl in enumerate(cols):
                dq = (_dot(ds_p[h], kp_ref[:, sl], NN) + _dot(ds_c[h], kc_ref[:, sl], NN)) * scale
                dq_ref[:, sl] = _rope_apply(dq, cos_t, sin_t).astype(BF16)
            for h, sl in enumerate(cols):
                dko_ref[:, sl] = dki_ref[:, sl] + ck_ref[:, sl] + _dot(ds_p[h], q_ref[:, sl], TN)
                dvo_ref[:, sl] = dvi_ref[:, sl] + cv_ref[:, sl] + _dot(p_p[h].astype(BF16), do_ref[:, sl], TN)
            for h, sl in enumerate(cols):
                ck_ref[:, sl] = _dot(ds_c[h], q_ref[:, sl], TN)
                cv_ref[:, sl] = _dot(p_c[h].astype(BF16), do_ref[:, sl], TN)

        @pl.when(n == nblk)
        def _():
            dko_ref[...] = dki_ref[...] + ck_ref[...]
            dvo_ref[...] = dvi_ref[...] + cv_ref[...]

    blk = (None, nb, e)
    qn = lambda n: jnp.minimum(n, nblk - 1)
    cur = lambda r, n: (r, qn(n), 0)
    prev = lambda r, n: (r, jnp.maximum(qn(n) - 1, 0), 0)
    kprev = lambda r, n: (r, jnp.maximum(n - 1, 0), 0)
    rblk = (None, nb, HEAD_DIM)
    return pl.pallas_call(
        body, name=name, grid=(dil, nblk + 1),
        in_specs=[pl.BlockSpec(blk, cur),
                  pl.BlockSpec(blk, prev), pl.BlockSpec(blk, cur), pl.BlockSpec(blk, prev), pl.BlockSpec(blk, cur),
                  pl.BlockSpec(blk, cur), pl.BlockSpec(blk, cur), pl.BlockSpec(blk, cur),
                  pl.BlockSpec(rblk, cur), pl.BlockSpec(rblk, cur),
                  pl.BlockSpec(blk, kprev), pl.BlockSpec(blk, kprev)],
        out_specs=[pl.BlockSpec(blk, cur), pl.BlockSpec(blk, kprev), pl.BlockSpec(blk, kprev)],
        out_shape=[jax.ShapeDtypeStruct((dil, m, e), BF16),
                   jax.ShapeDtypeStruct((dil, m, e), F32), jax.ShapeDtypeStruct((dil, m, e), F32)],
        scratch_shapes=[pltpu.VMEM((nb, e), F32), pltpu.VMEM((nb, e), F32)],
        input_output_aliases={10: 1, 11: 2},
        compiler_params=_params(("parallel", "arbitrary")),
    )(q, k, k, v, v, dout, lse, delta, cos, sin_inv, dk_acc, dv_acc)


def _rope_tables(s_len):
    inv_freq = 1.0 / (ROPE_THETA ** (jnp.arange(0, HEAD_DIM, 2, dtype=F32) / HEAD_DIM))
    ang = jnp.arange(s_len, dtype=F32)[:, None] * inv_freq[None, :]
    cos, sin = jnp.cos(ang), jnp.sin(ang)
    return jnp.concatenate([cos, cos], axis=1), jnp.concatenate([-sin, sin], axis=1)


def _row(vec):
    return vec.reshape(1, -1)


def _local_step(x, target, n_a, n_b, fetch, emit):
    s_len, d = x.shape
    n_q = len(DILATED_PAIRS)
    cos, sin = _rope_tables(s_len)
    sin_inv = -sin
    q_scale = 1.0 / math.sqrt(HEAD_DIM)
    w = {}

    def need(group, after):
        for name, (layer, arr) in fetch(group, after).items():
            w.setdefault(name, {})[layer] = arr

    saved_a = []
    for i in range(n_a):
        need(f"a{i}", x)
        h = _rmsnorm_fwd(f"a{i}_norm", x, _row(w["norm_a"][i]))
        proj = _mm_act_w(f"a{i}_in", h, w["w_in_a"][i], out_dtype=F32)
        pooled = _pool_fwd(f"a{i}_pool", proj)
        ypre = _mm_grp_fwd(f"a{i}_grp", pooled, w["w_grp_a"][i])
        z = _gate_a_fwd(f"a{i}_gate", ypre, proj, _row(w["scale_a"][i]))
        x_next = _mm_act_w(f"a{i}_out", z, w["w_out_a"][i], out_dtype=F32, add=x)
        saved_a.append((x, h, proj, pooled, ypre, z))
        x = x_next

    x_kv = x
    need("kv", x)
    e = w["w_k"][0].shape[1]
    kv_in = _rmsnorm_fwd("kv_norm", x, _row(w["norm_kv"][0]))
    windows = [window for window, _ in DILATED_PAIRS]
    dils = tuple(dil for _, dil in DILATED_PAIRS)
    far_dils = tuple(dil for dil in dils if dil > 1)
    ks = _mm_act_w("kv_k", kv_in, w["w_k"][0], rope=(cos, sin, 1.0), dils=dils)
    vs = _mm_act_w("kv_v", kv_in, w["w_v"][0], dils=dils)

    saved_b = []
    for i in range(n_b):
        need(f"b{i}", x if i > 0 else vs[0])
        hs = _rmsnorm_fwd(f"b{i}_norm", x, _row(w["norm_b"][i]), dils=far_dils)
        hs = {1: hs[0], **{dil: h_d.reshape(s_len, d) for dil, h_d in zip(far_dils, hs[1:])}}
        qs = [_mm_act_w(f"b{i}_q{g}", hs[1], w["w_in_b"][i], rope=(cos, sin, q_scale), n_first=g, n_cols=1,
                        dils=(dil,))[0] for g, dil in enumerate(dils)]
        gate = _mm_act_w(f"b{i}_g", hs[1], w["w_in_b"][i], out_dtype=F32, n_first=n_q, n_cols=1)
        outs, lses = [], []
        for g in range(n_q):
            o_g, l_g = _attn_fwd(f"b{i}_attn{g}", windows[g], qs[g], ks[g], vs[g])
            outs.append(o_g)
            lses.append(l_g)
        merged, lse, z = _merge_gate_fwd(f"b{i}_merge", outs, lses, gate, dils)
        x_next = _mm_act_w(f"b{i}_out", z, w["w_out_b"][i], out_dtype=F32, add=x)
        saved_b.append((x, hs, qs, gate, merged, lse, z))
        x = x_next

    need("head", x)
    loss_vec, dx, dxb, g_norm_f = _loss_head("loss_head", x, _row(w["norm_f"][0]), target)

    small = {"norm_a": {}, "scale_a": {}, "norm_kv": {}, "norm_b": {}, "norm_f": {0: g_norm_f}}
    shard_rows = lambda g2: g2.reshape(N_CHIPS, g2.shape[0] // N_CHIPS, g2.shape[1])

    res_major = lambda t, dil: t.reshape(s_len // dil, dil, t.shape[1]).transpose(1, 0, 2)
    cos_r = [res_major(cos, dil) for dil in dils]
    sin_inv_r = [res_major(sin_inv, dil) for dil in dils]
    dk_accs = [jnp.zeros((dil, s_len // dil, e), F32) for dil in dils]
    dv_accs = [jnp.zeros((dil, s_len // dil, e), F32) for dil in dils]
    for i in reversed(range(n_b)):
        x_in, hs, qs, gate, merged, lse, z = saved_b[i]
        dz = _mm_grad_act(f"b{i}_dz", dxb, w["w_out_b"][i])
        g_out = shard_rows(_mm_grad_w(f"b{i}_gwo", z, dxb))
        dgate, stats = _gate_b_bwd(f"b{i}_dgate", dz, merged, gate, lse, dils)
        dh = _mm_grad_act(f"b{i}_dh{n_q}", dgate, w["w_in_b"][i], slot=n_q)
        g_in = _mm_grad_w(f"b{i}_gwi{n_q}", hs[1], dgate, col_shards=n_q + 1, slot=n_q)
        for g, dil in enumerate(dils):
            dout, lse_g, delta_g = stats[g]
            dq, dk_accs[g], dv_accs[g] = _attn_bwd(f"b{i}_dattn{g}", windows[g], q_scale, qs[g], ks[g], vs[g], dout,
                                                   lse_g, delta_g, cos_r[g], sin_inv_r[g], dk_accs[g], dv_accs[g])
            dh = _mm_grad_act(f"b{i}_dh{g}", dq if dil > 1 else dq[0], w["w_in_b"][i], add=dh, slot=g)
            g_in = _mm_grad_w(f"b{i}_gwi{g}", hs[dil], dq.reshape(s_len, e), col_shards=n_q + 1, slot=g, into=g_in)
        dx, dxb, small["norm_b"][i] = _rmsnorm_bwd(f"b{i}_dnorm", x_in, _row(w["norm_b"][i]), dh, dx)
        dx, dxb = emit(f"b{i}", {"w_in_b": (i, g_in), "w_out_b": (i, g_out)}, (dx, dxb))

    dkb, dvb = _kv_grad_prep("kv_dprep", dk_accs, dv_accs, dils, cos, sin_inv)
    dkv = _mm_grad_act("kv_dk", dkb, w["w_k"][0])
    dkv = _mm_grad_act("kv_dv", dvb, w["w_v"][0], add=dkv)
    g_k = shard_rows(_mm_grad_w("kv_gwk", kv_in, dkb))
    g_v = shard_rows(_mm_grad_w("kv_gwv", kv_in, dvb))
    dx, dxb, small["norm_kv"][0] = _rmsnorm_bwd("kv_dnorm", x_kv, _row(w["norm_kv"][0]), dkv, dx)
    dx, dxb = emit("kv", {"w_k": (0, g_k), "w_v": (0, g_v)}, (dx, dxb))

    for i in reversed(range(n_a)):
        x_in, h, proj, pooled, ypre, z = saved_a[i]
        dz = _mm_grad_act(f"a{i}_dz", dxb, w["w_out_a"][i])
        g_out = shard_rows(_mm_grad_w(f"a{i}_gwo", z, dxb))
        dypre, dproj, small["scale_a"][i] = _gate_a_bwd(f"a{i}_dgate", dz, ypre, proj, _row(w["scale_a"][i]))
        dpooled = _mm_grp_grad_act(f"a{i}_dgrp", dypre, w["w_grp_a"][i])
        g_grp = _mm_grp_grad_w(f"a{i}_gwg", pooled, dypre, len(POOL_WINDOWS))
        g_grp = g_grp.reshape(N_CHIPS, -1, g_grp.shape[-1])
        dproj = _pool_bwd(f"a{i}_dpool", dpooled, dproj)
        dh = _mm_grad_act(f"a{i}_dh", dproj, w["w_in_a"][i])
        g_in = _mm_grad_w(f"a{i}_gwi", h, dproj, col_shards=N_CHIPS)
        dx, dxb, small["norm_a"][i] = _rmsnorm_bwd(f"a{i}_dnorm", x_in, _row(w["norm_a"][i]), dh, dx)
        a_grads = {"w_in_a": (i, g_in), "w_grp_a": (i, g_grp), "w_out_a": (i, g_out)}
        if i > 0:
            dx, dxb = emit(f"a{i}", a_grads, (dx, dxb))
        else:
            emit(f"a{i}", a_grads, ())

    return loss_vec, dx, small


BIG_WEIGHTS = ("w_in_a", "w_grp_a", "w_out_a", "w_k", "w_v", "w_in_b", "w_out_b")


def _pair_add(name, grad, recv, c_idx):
    _, r, cols = grad.shape
    half = r // 2
    rb = _tile(half, 256)
    nrb = half // rb

    def body(c_ref, g_ref, r_ref, o_ref):
        o_ref[...] = (g_ref[...].astype(F32) + r_ref[...].astype(F32)).astype(BF16)

    blk = (None, rb, cols)
    grid_spec = pltpu.PrefetchScalarGridSpec(
        num_scalar_prefetch=1, grid=(N_CHIPS, nrb),
        in_specs=[pl.BlockSpec(blk, lambda s, i, c: (s, c[0] * nrb + i, 0)), pl.BlockSpec(blk, lambda s, i, c: (s, i, 0))],
        out_specs=pl.BlockSpec(blk, lambda s, i, c: (s, i, 0)))
    return pl.pallas_call(body, name=name, grid_spec=grid_spec,
                          out_shape=jax.ShapeDtypeStruct((N_CHIPS, half, cols), BF16),
                          compiler_params=_params(("parallel", "parallel")))(c_idx, grad, recv)


def _final_add(name, part, recv, sc_idx, layer, n_layers, into=None):
    _, half, cols = part.shape
    rb = _tile(half, 256)
    nrb = half // rb
    n_peer = recv.shape[0]

    def body(sc_ref, p_ref, *refs):
        acc = p_ref[...].astype(F32)
        for r_ref in refs[:n_peer]:
            acc = acc + r_ref[...].astype(F32)
        refs[-1][...] = acc

    blk = (None, rb, cols)
    peer_spec = lambda k: pl.BlockSpec(blk, lambda i, sc: (k, i, 0))
    grid_spec = pltpu.PrefetchScalarGridSpec(
        num_scalar_prefetch=1, grid=(nrb,),
        in_specs=[pl.BlockSpec(blk, lambda i, sc: (sc[0], i, 0))] + [peer_spec(k) for k in range(n_peer)]
                 + ([] if into is None else [ANY]),
        out_specs=pl.BlockSpec(blk, lambda i, sc: (layer, sc[1] * nrb + i, 0)))
    extra = () if into is None else (into,)
    return pl.pallas_call(body, name=name, grid_spec=grid_spec,
                          out_shape=jax.ShapeDtypeStruct((n_layers, 2 * half, cols), F32),
                          input_output_aliases={} if into is None else {2 + n_peer: 0},
                          compiler_params=_params(("parallel",)))(sc_idx, part, *([recv] * n_peer), *extra)


def _cast_into_slot(name, arr, layer, s_idx):
    _, b, r, cols = arr.shape
    rb = _tile(r, 512)

    def body(s_ref, a_ref, o_ref):
        o_ref[...] = a_ref[...].astype(BF16)

    blk = (None, None, rb, cols)
    grid_spec = pltpu.PrefetchScalarGridSpec(
        num_scalar_prefetch=1, grid=(b, r // rb),
        in_specs=[pl.BlockSpec(blk, lambda j, i, s: (layer, j, i, 0))],
        out_specs=pl.BlockSpec(blk, lambda j, i, s: (j, s[0], i, 0)))
    return pl.pallas_call(body, name=name, grid_spec=grid_spec,
                          out_shape=jax.ShapeDtypeStruct((b, N_CHIPS, r, cols), BF16),
                          compiler_params=_params(("parallel", "parallel")))(s_idx, arr)


def _sum_devices(name, gathered):
    n_dev, p, d = gathered.shape

    def body(g_ref, o_ref):
        acc = g_ref[0]
        for j in range(1, n_dev):
            acc = acc + g_ref[j]
        o_ref[...] = acc

    return pl.pallas_call(body, name=name, out_shape=jax.ShapeDtypeStruct((p, d), F32),
                          compiler_params=_params())(gathered)


def _adamw(name, w, g, m, v):
    shape = w.shape
    cols = shape[-1]
    flat = lambda a: a.reshape(-1, cols)
    rows = flat(w).shape[0]
    bs = _tile(rows, 256)

    def body(w_ref, g_ref, m_ref, v_ref, d_ref, mo_ref, vo_ref):
        grad = g_ref[...]
        m_new = ADAM_B1 * m_ref[...] + (1.0 - ADAM_B1) * grad
        v_new = ADAM_B2 * v_ref[...] + (1.0 - ADAM_B2) * (grad * grad)
        m_hat = m_new / (1.0 - ADAM_B1 ** ADAM_STEP)
        v_hat = v_new / (1.0 - ADAM_B2 ** ADAM_STEP)
        d_ref[...] = -ADAM_LR * (m_hat / (jnp.sqrt(v_hat) + ADAM_EPS) + ADAM_WD * w_ref[...])
        mo_ref[...] = m_new
        vo_ref[...] = v_new

    spec = _row_spec(bs, cols)
    outs = _rows_call(body, name, rows, [spec] * 4, [spec] * 3, [jax.ShapeDtypeStruct((rows, cols), F32)] * 3, bs)(
        flat(w), flat(g), flat(m), flat(v))
    return tuple(o.reshape(shape) for o in outs)


def _place():
    x, y, c = lax.axis_index("x"), lax.axis_index("y"), lax.axis_index("c")
    chips = [(1 - x, y), (x, 1 - y), (1 - x, 1 - y)]
    return x, y, c, chips


def _chip_index(chip):
    return 2 * chip[0] + chip[1]


def _comm_call(body, name, n_in, out_shape, scratch, aliases=None):
    return pl.pallas_call(body, name=name, in_specs=[ANY] * n_in, out_specs=[ANY] * len(out_shape), out_shape=out_shape,
                          scratch_shapes=scratch, input_output_aliases=aliases or {})


HBM_SPEC = pl.BlockSpec(memory_space=pltpu.HBM)
SEM_SPEC = pl.BlockSpec(memory_space=pltpu.SEMAPHORE)
SPLIT_PARAMS = pltpu.CompilerParams(has_side_effects=pltpu.SideEffectType.DATAFLOW_SIDE_EFFECTING)


def _in_hbm(arr):
    return pltpu.with_memory_space_constraint(arr, pltpu.HBM)


def _slot_half(ref, chip, core):
    half = ref.shape[2] // 2
    return ref.at[:, _chip_index(chip), pl.ds(core * half, half), :]


def _gather_start(name, bufs, carry=()):
    n, n_c = len(bufs), len(carry)

    def body(*refs):
        ins, (send_sems, recv_sems) = refs[:n], refs[n + n_c:n + n_c + 2]
        x, y, c, chips = _place()
        for a in range(n):
            block = _slot_half(ins[a], (x, y), c)
            for k, chip in enumerate(chips):
                pltpu.make_async_remote_copy(src_ref=block, dst_ref=block, send_sem=send_sems.at[3 * a + k],
                                             recv_sem=recv_sems.at[3 * a + k], device_id=(*chip, c),
                                             device_id_type=MESH).start()

    dma = pltpu.SemaphoreType.DMA
    thru = list(bufs) + list(carry)
    res = pl.pallas_call(
        body, name=name, in_specs=[HBM_SPEC] * (n + n_c), out_specs=[SEM_SPEC] * 2 + [HBM_SPEC] * (n + n_c),
        out_shape=[dma((3 * n,)), dma((3 * n,))] + [pltpu.HBM(a.shape, a.dtype) for a in thru],
        input_output_aliases={t: 2 + t for t in range(n + n_c)}, compiler_params=SPLIT_PARAMS,
    )(*[_in_hbm(a) for a in thru])
    return (res[0], res[1]), list(res[2:2 + n]), list(res[2 + n:])


def _gather_wait(name, sems, bufs, after):
    n = len(bufs)

    def body(*refs):
        ins, (send_sems, recv_sems) = refs[:n], refs[n:n + 2]
        x, y, c, chips = _place()
        for a in range(n):
            for k, chip in enumerate(chips):
                mine, theirs = _slot_half(ins[a], (x, y), c), _slot_half(ins[a], chip, c)
                copy = pltpu.make_async_remote_copy(src_ref=mine, dst_ref=theirs, send_sem=send_sems.at[3 * a + k],
                                                    recv_sem=recv_sems.at[3 * a + k], device_id=(*chip, c),
                                                    device_id_type=MESH)
                copy.wait_send()
                copy.wait_recv()

    res = pl.pallas_call(
        body, name=name, in_specs=[HBM_SPEC] * n + [SEM_SPEC, SEM_SPEC, ANY], out_specs=[HBM_SPEC] * n,
        out_shape=[pltpu.HBM(b.shape, b.dtype) for b in bufs], input_output_aliases={a: a for a in range(n)},
        compiler_params=SPLIT_PARAMS)(*bufs, *sems, after)
    return list(res)


def _gather_forward(name, bufs, smalls=()):
    n, n_small = len(bufs), len(smalls)

    def body(*refs):
        small_in = refs[n:n + n_small]
        outs = refs[n + n_small:2 * n + n_small]
        small_out = refs[2 * n + n_small:2 * n + 2 * n_small]
        send_sems, recv_sems, s_send, s_recv, s_local = refs[-5:]
        x, y, c, chips = _place()
        me, sibling = _chip_index((x, y)), (x, y, 1 - c)

        def forward(t, k, core):
            block = _slot_half(outs[t], chips[k], core)
            return pltpu.make_async_remote_copy(src_ref=block, dst_ref=block, send_sem=send_sems.at[t, k],
                                                recv_sem=recv_sems.at[t, k], device_id=sibling, device_id_type=MESH)

        def small_copy(j, k, slot):
            return pltpu.make_async_remote_copy(src_ref=small_in[j], dst_ref=small_out[j].at[slot],
                                                send_sem=s_send.at[j, k], recv_sem=s_recv.at[j, k],
                                                device_id=(*chips[k], c), device_id_type=MESH)

        local = []
        for t in range(n):
            for k in range(3):
                forward(t, k, c).start()
        for j in range(n_small):
            own = pltpu.make_async_copy(small_in[j], small_out[j].at[me], s_local.at[j])
            own.start()
            local.append(own)
            for k in range(3):
                small_copy(j, k, me).start()
        for t in range(n):
            for k in range(3):
                forward(t, k, 1 - c).wait_recv()
        for j in range(n_small):
            for k in range(3):
                small_copy(j, k, _chip_index(chips[k])).wait_recv()
        for t in range(n):
            for k in range(3):
                forward(t, k, c).wait_send()
        for j in range(n_small):
            for k in range(3):
                small_copy(j, k, me).wait_send()
        for own in local:
            own.wait()

    out_shape = [jax.ShapeDtypeStruct(b.shape, BF16) for b in bufs]
    out_shape += [jax.ShapeDtypeStruct((N_CHIPS,) + s.shape, F32) for s in smalls]
    dma = pltpu.SemaphoreType.DMA
    n_s = max(n_small, 1)
    res = _comm_call(body, name, n + n_small, out_shape,
                     [dma((n, 3)), dma((n, 3)), dma((n_s, 3)), dma((n_s, 3)), dma((n_s,))],
                     aliases={t: t for t in range(n)})(*bufs, *smalls)
    return list(res[:n]), list(res[n:])


def _exchange_halves(name, grads):
    n = len(grads)

    def body(*refs):
        g_in, outs = refs[:n], refs[n:2 * n]
        send_sems, recv_sems = refs[-2:]
        x, y, c, _ = _place()
        copies = []
        for t in range(n):
            half = g_in[t].shape[1] // 2
            cp = pltpu.make_async_remote_copy(
                src_ref=g_in[t].at[:, pl.ds((1 - c) * half, half), :], dst_ref=outs[t], send_sem=send_sems.at[t],
                recv_sem=recv_sems.at[t], device_id=(x, y, 1 - c), device_id_type=MESH)
            cp.start()
            copies.append(cp)
        for cp in copies:
            cp.wait()

    out_shape = [jax.ShapeDtypeStruct((g.shape[0], g.shape[1] // 2, g.shape[2]), BF16) for g in grads]
    dma = pltpu.SemaphoreType.DMA
    return list(_comm_call(body, name, n, out_shape, [dma((n,)), dma((n,))])(*grads))


def _scatter_copy(part_ref, land_ref, send_sems, recv_sems, t, k, chip, c):
    return pltpu.make_async_remote_copy(
        src_ref=part_ref.at[_chip_index(chip)], dst_ref=land_ref.at[k], send_sem=send_sems.at[3 * t + k],
        recv_sem=recv_sems.at[3 * t + k], device_id=(*chip, c), device_id_type=MESH)


def _scatter_start(name, parts, carry=()):
    n, n_c = len(parts), len(carry)
    lands = [lax.empty((3,) + p.shape[1:], BF16) for p in parts]

    def body(*refs):
        p_in, l_in = refs[:n], refs[n:2 * n]
        send_sems, recv_sems = refs[2 * n + n_c:2 * n + n_c + 2]
        x, y, c, chips = _place()
        for t in range(n):
            for k, chip in enumerate(chips):
                _scatter_copy(p_in[t], l_in[t], send_sems, recv_sems, t, k, chip, c).start()

    dma = pltpu.SemaphoreType.DMA
    thru = list(parts) + lands + list(carry)
    res = pl.pallas_call(
        body, name=name, in_specs=[HBM_SPEC] * len(thru), out_specs=[SEM_SPEC] * 2 + [HBM_SPEC] * len(thru),
        out_shape=[dma((3 * n,)), dma((3 * n,))] + [pltpu.HBM(a.shape, a.dtype) for a in thru],
        input_output_aliases={t: 2 + t for t in range(len(thru))}, compiler_params=SPLIT_PARAMS,
    )(*[_in_hbm(a) for a in thru])
    return (res[0], res[1]), list(res[2:2 + n]), list(res[2 + n:2 + 2 * n]), list(res[2 + 2 * n:])


def _scatter_wait(name, sems, parts, lands):
    n = len(parts)

    def body(*refs):
        p_in, l_in = refs[:n], refs[n:2 * n]
        send_sems, recv_sems = refs[2 * n:2 * n + 2]
        x, y, c, chips = _place()
        for t in range(n):
            for k, chip in enumerate(chips):
                copy = _scatter_copy(p_in[t], l_in[t], send_sems, recv_sems, t, k, chip, c)
                copy.wait_send()
                copy.wait_recv()

    hbm_out = lambda a: pltpu.HBM(a.shape, a.dtype)
    res = pl.pallas_call(
        body, name=name, in_specs=[HBM_SPEC] * (2 * n) + [SEM_SPEC, SEM_SPEC], out_specs=[HBM_SPEC] * (2 * n),
        out_shape=[hbm_out(a) for a in parts + lands], input_output_aliases={t: t for t in range(2 * n)},
        compiler_params=SPLIT_PARAMS)(*parts, *lands, *sems)
    return list(res[:n]), list(res[n:])


def _share_halves(fulls):
    n = len(fulls)
    items = [(a, l) for a in range(n) for l in range(fulls[a].shape[0])]

    def body(*refs):
        outs = refs[n:2 * n]
        send_sems, recv_sems = refs[-2:]
        x, y, c, _ = _place()

        def copy(t, core):
            a, l = items[t]
            half = outs[a].shape[1] // 2
            block = outs[a].at[l, pl.ds(core * half, half), :]
            return pltpu.make_async_remote_copy(src_ref=block, dst_ref=block, send_sem=send_sems.at[t],
                                                recv_sem=recv_sems.at[t], device_id=(x, y, 1 - c), device_id_type=MESH)

        for t in range(len(items)):
            copy(t, c).start()
        for t in range(len(items)):
            copy(t, 1 - c).wait_recv()
        for t in range(len(items)):
            copy(t, c).wait_send()

    out_shape = [jax.ShapeDtypeStruct(f.shape, F32) for f in fulls]
    dma = pltpu.SemaphoreType.DMA
    return list(_comm_call(body, "grad_share_halves", n, out_shape, [dma((len(items),)), dma((len(items),))],
                           aliases={a: a for a in range(n)})(*fulls))


def _allgather_small(packed):
    def body(p_ref, o_ref, send_sems, recv_sems, local_sem):
        x, y, c, _ = _place()
        me = 4 * x + 2 * y + c
        own = pltpu.make_async_copy(p_ref, o_ref.at[me], local_sem)
        own.start()
        flips = [(fx, fy, fc) for fx in (0, 1) for fy in (0, 1) for fc in (0, 1)][1:]
        peers = [(x ^ fx, y ^ fy, c ^ fc) for fx, fy, fc in flips]
        copies = []
        for k, peer in enumerate(peers):
            cp = pltpu.make_async_remote_copy(src_ref=p_ref, dst_ref=o_ref.at[me], send_sem=send_sems.at[k],
                                              recv_sem=recv_sems.at[k], device_id=peer, device_id_type=MESH)
            cp.start()
            copies.append(cp)
        for k, (px, py, pc) in enumerate(peers):
            pltpu.make_async_remote_copy(src_ref=p_ref, dst_ref=o_ref.at[4 * px + 2 * py + pc], send_sem=send_sems.at[k],
                                         recv_sem=recv_sems.at[k], device_id=peers[k], device_id_type=MESH).wait_recv()
        for cp in copies:
            cp.wait_send()
        own.wait()

    dma = pltpu.SemaphoreType.DMA
    return _comm_call(body, "small_allgather", 1, [jax.ShapeDtypeStruct((8,) + packed.shape, F32)],
                      [dma((7,)), dma((7,)), dma(())])(packed)[0]


PAD_ROWS = 8


def kernel(x, norm_a, w_in_a, w_grp_a, scale_a, w_out_a, norm_kv, w_k, w_v, norm_b, w_in_b, w_out_b, norm_f, loss_target, m_norm_a, m_w_in_a, m_w_grp_a, m_scale_a, m_w_out_a, m_norm_kv, m_w_k, m_w_v, m_norm_b, m_w_in_b, m_w_out_b, m_norm_f, v_norm_a, v_w_in_a, v_w_grp_a, v_scale_a, v_w_out_a, v_norm_kv, v_w_k, v_w_v, v_norm_b, v_w_in_b, v_w_out_b, v_norm_f):
    weights = dict(norm_a=norm_a, w_in_a=w_in_a, w_grp_a=w_grp_a, scale_a=scale_a, w_out_a=w_out_a, norm_kv=norm_kv,
                   w_k=w_k, w_v=w_v, norm_b=norm_b, w_in_b=w_in_b, w_out_b=w_out_b, norm_f=norm_f)
    moments_m = dict(norm_a=m_norm_a, w_in_a=m_w_in_a, w_grp_a=m_w_grp_a, scale_a=m_scale_a, w_out_a=m_w_out_a,
                     norm_kv=m_norm_kv, w_k=m_w_k, w_v=m_w_v, norm_b=m_norm_b, w_in_b=m_w_in_b, w_out_b=m_w_out_b,
                     norm_f=m_norm_f)
    moments_v = dict(norm_a=v_norm_a, w_in_a=v_w_in_a, w_grp_a=v_w_grp_a, scale_a=v_scale_a, w_out_a=v_w_out_a,
                     norm_kv=v_norm_kv, w_k=v_w_k, w_v=v_w_v, norm_b=v_norm_b, w_in_b=v_w_in_b, w_out_b=v_w_out_b,
                     norm_f=v_norm_f)
    names = list(weights)
    d = x.shape[-1]
    c_idx = lax.axis_index("c").astype(jnp.int32).reshape(1)
    s_me = 2 * lax.axis_index("x") + lax.axis_index("y")
    s_idx = s_me.astype(jnp.int32).reshape(1)

    def as_lbrc(name):
        a = weights[name]
        if name == "w_grp_a":
            return a
        if a.ndim == 2:
            return a.reshape(1, 1, *a.shape)
        return a.reshape(a.shape[0], 1, *a.shape[1:])

    n_a, n_b = norm_a.shape[0], norm_b.shape[0]
    group_weights = {**{f"a{i}": [("w_in_a", i), ("w_grp_a", i), ("w_out_a", i)] for i in range(n_a)},
                     "kv": [("w_k", 0), ("w_v", 0)],
                     **{f"b{i}": [("w_in_b", i), ("w_out_b", i)] for i in range(n_b)}}
    group_order = [f"a{i}" for i in range(n_a)] + ["kv"] + [f"b{i}" for i in range(n_b)]
    slots, slot_groups = [], []
    for group in group_order:
        slot_groups.append(list(range(len(slots), len(slots) + len(group_weights[group]))))
        slots += [_cast_into_slot(f"cast_{name}{l}", as_lbrc(name), l, s_idx) for name, l in group_weights[group]]
    small_full, started = {}, {}

    def start_group(gi, carry=()):
        sems, bufs, carry = _gather_start(f"gather_start_{group_order[gi]}", [slots[t] for t in slot_groups[gi]], carry)
        started[gi] = (sems, bufs)
        return carry

    start_group(0)

    def gathered_form(name, g):
        if name in ("w_in_a", "w_in_b"):
            return g[0]
        if name == "w_grp_a":
            return g.reshape(g.shape[0], -1, g.shape[-1])
        return g.reshape(-1, g.shape[-1])

    def fetch(group, after):
        if group == "head":
            return {"norm_f": (0, norm_f)}
        gi = group_order.index(group)
        sems, bufs = started[gi]
        bufs = _gather_wait(f"gather_wait_{group}", sems, bufs, after)
        bufs, small_g = _gather_forward(f"gather_forward_{group}", bufs, [norm_a, scale_a] if gi == 0 else [])
        out = {name: (l, gathered_form(name, g)) for (name, l), g in zip(group_weights[group], bufs)}
        if gi == 0:
            for name, g in zip(("norm_a", "scale_a"), small_g):
                small_full[name] = g.transpose(1, 0, 2).reshape(g.shape[1], -1)
        layer = group_weights[group][0][1]
        if group.startswith("a"):
            gain_name, gain = "norm_a", small_full["norm_a"][layer]
            out.update(scale_a=(layer, small_full["scale_a"][layer]))
        elif group == "kv":
            gain_name, gain = "norm_kv", norm_kv
        else:
            gain_name, gain = "norm_b", norm_b[layer]
        gain = gain.reshape(1, -1)
        ahead = [gi + 1] + ([gi + 2] if gi + 2 < len(group_order) and group_order[gi + 1] == "kv" else [])
        for gj in ahead:
            if gj < len(group_order) and gj not in started:
                (gain,) = start_group(gj, (gain,))
        out[gain_name] = (layer, gain)
        return out

    in_flight = []

    def emit(group, grads_of, carry):
        keys = list(grads_of)
        recv1 = _exchange_halves(f"grad_exchange_{group}", [grads_of[k][1] for k in keys])
        parts = [_pair_add(f"pair_add_{k}{grads_of[k][0]}", grads_of[k][1], r, c_idx) for k, r in zip(keys, recv1)]
        sems, parts, lands, carry = _scatter_start(f"grad_scatter_start_{group}", parts, tuple(carry))
        in_flight.append((group, [(k, grads_of[k][0]) for k in keys], sems, parts, lands))
        return carry

    loss_vec, grad_x, small = _local_step(x[0], loss_target[0], n_a, n_b, fetch, emit)

    small_order = [("norm_a", i) for i in range(n_a)] + [("scale_a", i) for i in range(n_a)] + [("norm_kv", 0)] + \
                  [("norm_b", i) for i in range(norm_b.shape[0])] + [("norm_f", 0)]
    pad = lambda vec: jnp.pad(vec, ((0, PAD_ROWS - 1), (0, 0)))
    packed = jnp.concatenate([pad(loss_vec)] + [pad(small[n][i]) for n, i in small_order], axis=0)
    totals = _sum_devices("small_sum", _allgather_small(packed))
    loss = 0.5 * jnp.sum(totals[0]) / d
    small_tot = {}
    for j, (n, i) in enumerate(small_order):
        small_tot.setdefault(n, []).append(totals[PAD_ROWS * (j + 1)])
    grads = {}
    shard_w = norm_a.shape[1]
    for n in ("norm_a", "scale_a"):
        full = jnp.stack(small_tot[n])
        grads[n] = lax.dynamic_slice_in_dim(full, s_me * shard_w, shard_w, axis=1)
    grads["norm_kv"] = small_tot["norm_kv"][0]
    grads["norm_b"] = jnp.stack(small_tot["norm_b"])
    grads["norm_f"] = small_tot["norm_f"][0]

    sc_idx = jnp.concatenate([s_idx, c_idx])
    fulls = {name: None for name in BIG_WEIGHTS}
    for group, keys, sems, parts, lands in in_flight:
        parts, lands = _scatter_wait(f"grad_scatter_wait_{group}", sems, parts, lands)
        for (name, i), p, r in zip(keys, parts, lands):
            n_layers = 1 if weights[name].ndim == 2 else weights[name].shape[0]
            fulls[name] = _final_add(f"final_add_{name}{i}", p, r, sc_idx, i, n_layers, into=fulls[name])
    shared = _share_halves([fulls[name] for name in BIG_WEIGHTS])
    for name, g in zip(BIG_WEIGHTS, shared):
        grads[name] = g.reshape(weights[name].shape)

    deltas, new_m, new_v = {}, {}, {}
    for n in names:
        shape = weights[n].shape
        as2d = (lambda a: a.reshape(1, -1)) if len(shape) == 1 else (lambda a: a)
        dl, mn, vn = _adamw(f"adamw_{n}", as2d(weights[n]), as2d(grads[n]), as2d(moments_m[n]), as2d(moments_v[n]))
        deltas[n], new_m[n], new_v[n] = dl.reshape(shape), mn.reshape(shape), vn.reshape(shape)

    return (loss, grad_x[None], *[grads[n] for n in names], *[deltas[n] for n in names],
            *[new_m[n] for n in names], *[new_v[n] for n in names])
```

```python
import functools
import math

import jax
import jax.numpy as jnp
from jax import lax
from jax.experimental import pallas as pl
from jax.experimental.pallas import tpu as pltpu

F32 = jnp.float32
BF16 = jnp.bfloat16

HEAD_DIM = 128
POOL_WINDOWS = (2, 4, 8, 16)
DILATED_PAIRS = ((128, 1), (512, 4), (2048, 16))
ROPE_THETA = 10000.0
RMS_EPS = 1e-6
NEG_INF = -1e30
N_CHIPS = 4

ADAM_LR = 0.001
ADAM_B1 = 0.9
ADAM_B2 = 0.999
ADAM_EPS = 1e-08
ADAM_WD = 0.01
ADAM_STEP = 10

VMEM_LIMIT_BYTES = 56 * 1024 * 1024
MESH = pl.DeviceIdType.MESH
ANY = pl.BlockSpec(memory_space=pl.ANY)


def _tile(n, pref):
    t = min(n, pref)
    assert n % t == 0, (n, pref)
    return t


def _params(sem=None):
    return pltpu.CompilerParams(dimension_semantics=sem, vmem_limit_bytes=VMEM_LIMIT_BYTES)


def _mm(name, a, b, *, grid2, nk, a_blk, a_map, b_blk, b_map, outs, dims, epi=None, epi_in=(), epi_specs=(),
        acc_shape=None, epi_scratch=(), into=None):
    n_epi, n_out = len(epi_in), len(outs)

    def body(*refs):
        a_ref, b_ref = refs[0], refs[1]
        e_refs = refs[2:2 + n_epi]
        first_out = 2 + n_epi + (0 if into is None else 1)
        o_refs = refs[first_out:first_out + n_out]
        s_refs = refs[first_out + n_out + (0 if nk == 1 else 1):]

        def contrib():
            a_val = a_ref[...]
            if a_val.ndim == 3:
                a_val = a_val.reshape(-1, a_val.shape[-1])
            return lax.dot_general(a_val, b_ref[...], (dims, ((), ())), preferred_element_type=F32)

        def finish(acc):
            if epi is None:
                o_refs[0][...] = acc.reshape(o_refs[0].shape).astype(o_refs[0].dtype)
            else:
                epi(acc, e_refs, o_refs, s_refs)

        if nk == 1:
            finish(contrib())
        else:
            acc_ref = refs[first_out + n_out]
            k = pl.program_id(2)

            @pl.when(k == 0)
            def _():
                acc_ref[...] = contrib()

            @pl.when(k > 0)
            def _():
                acc_ref[...] += contrib()

            @pl.when(k == nk - 1)
            def _():
                finish(acc_ref[...])

    scratch = ([] if nk == 1 else [pltpu.VMEM(acc_shape, F32)]) + list(epi_scratch)
    extra_in, extra_specs, aliases = (), (), {}
    if into is not None:
        extra_in, extra_specs, aliases = (into[0],), (ANY,), {2 + n_epi: into[1]}
    res = pl.pallas_call(
        body, name=name, grid=(grid2[0], grid2[1], nk),
        in_specs=[pl.BlockSpec(a_blk, a_map), pl.BlockSpec(b_blk, b_map), *epi_specs, *extra_specs],
        out_specs=[pl.BlockSpec(blk, imap) for _, blk, imap, _ in outs],
        out_shape=[jax.ShapeDtypeStruct(shape, dtype) for shape, _, _, dtype in outs],
        scratch_shapes=scratch, input_output_aliases=aliases,
        compiler_params=_params(("parallel", "parallel", "arbitrary")),
    )(a, b, *epi_in, *extra_in)
    return res[0] if n_out == 1 else tuple(res)


NN = ((1,), (0,))
NT = ((1,), (1,))
TN = ((0,), (0,))


def _rope_apply(t, cos, sin):
    return t * cos + pltpu.roll(t, HEAD_DIM // 2, 1) * sin


def _epi_add(acc, e_refs, o_refs, s_refs):
    o_refs[0][...] = (acc + e_refs[0][...]).astype(o_refs[0].dtype)


def _col_blocks(width):
    return [slice(c * HEAD_DIM, (c + 1) * HEAD_DIM) for c in range(width // HEAD_DIM)]


def _col_scratch(rows, width):
    return pltpu.VMEM((width // HEAD_DIM, rows, HEAD_DIM), F32)


def _to_residue_major(o_ref, scr, d, sl):
    if d == 1:
        o_ref[0, :, sl] = scr[...].astype(o_ref.dtype)
        return
    rows = scr.shape[0] // d
    for r in range(d):
        o_ref[r, :, sl] = scr[pl.ds(r, rows, stride=d), :].astype(o_ref.dtype)


def _from_residue_major(i_ref, scr, d, sl):
    if d == 1:
        return i_ref[0, :, sl].astype(F32)
    rows = i_ref.shape[1]
    for r in range(d):
        scr[pl.ds(r, rows, stride=d), :] = i_ref[r, :, sl].astype(F32)
    return scr[...]


def _make_epi_orders(dils, rope_scale):
    def epi(acc, e_refs, o_refs, s_refs):
        if rope_scale is not None:
            cos = e_refs[0][...]
            sin = e_refs[1][...]
        for c, sl in enumerate(_col_blocks(acc.shape[1])):
            scr = s_refs[0].at[c]
            scr[...] = acc[:, sl] if rope_scale is None else _rope_apply(acc[:, sl], cos, sin) * rope_scale
            for o_ref, d in zip(o_refs, dils):
                _to_residue_major(o_ref, scr, d, sl)
    return epi


def _make_epi_token_order(d, has_add):
    def epi(acc, e_refs, o_refs, s_refs):
        o_ref = o_refs[0]
        if d == 1:
            o_ref[...] = acc + e_refs[0][...] if has_add else acc
            return
        rows = acc.shape[0] // d
        for c, sl in enumerate(_col_blocks(acc.shape[1])):
            scr = s_refs[0].at[c]
            for r in range(d):
                scr[pl.ds(r, rows, stride=d), :] = acc[r * rows:(r + 1) * rows, sl]
            o_ref[:, sl] = scr[...] + e_refs[0][:, sl] if has_add else scr[...]
    return epi


def _mm_act_w(name, a, w, *, out_dtype=BF16, add=None, rope=None, n_first=0, n_cols=None, dils=None):
    s_len, k_len = a.shape
    bm = _tile(s_len, 1024)
    epi, epi_in, epi_specs, epi_scratch = None, (), (), ()
    if w.ndim == 3:
        ns, _, c = w.shape
        ns_used = ns if n_cols is None else n_cols
        bn = _tile(c, 1024)
        sub = c // bn
        grid2 = (ns_used * sub, s_len // bm)
        b_blk, b_map = (None, k_len, bn), (lambda j, i, k: (j // sub + n_first, 0, j % sub))
        n_len = ns_used * c
    else:
        n_len = w.shape[1]
        bn = _tile(n_len, 1024)
        grid2 = (n_len // bn, s_len // bm)
        b_blk, b_map = (k_len, bn), (lambda j, i, k: (0, j))
    if add is not None:
        epi, epi_in = _epi_add, (add,)
        epi_specs = (pl.BlockSpec((bm, bn), lambda j, i, k: (i, j)),)
    outs = [((s_len, n_len), (bm, bn), lambda j, i, k: (i, j), out_dtype)]
    if dils is not None:
        if rope is not None:
            epi_in = rope[:2]
            epi_specs = (pl.BlockSpec((bm, HEAD_DIM), lambda j, i, k: (i, 0)),) * 2
        epi = _make_epi_orders(dils, None if rope is None else rope[2])
        epi_scratch = (_col_scratch(bm, bn),)
        outs = [((d, s_len // d, n_len), (d, bm // d, bn), lambda j, i, k: (0, i, j), BF16) for d in dils]
    res = _mm(name, a, w, grid2=grid2, nk=1, a_blk=(bm, k_len), a_map=lambda j, i, k: (i, 0),
              b_blk=b_blk, b_map=b_map, outs=outs, dims=NN, epi=epi, epi_in=epi_in, epi_specs=epi_specs,
              epi_scratch=epi_scratch)
    return (res,) if dils is not None and len(dils) == 1 else res


def _mm_grad_act(name, dy, w, *, add=None, slot=None):
    if slot is not None:
        d = 1 if dy.ndim == 2 else dy.shape[0]
        s_len = dy.shape[-2] * d
        _, k_len, c = w.shape
        bm, bn = _tile(s_len, 1024), _tile(k_len, 1024)
        a_blk, a_map = ((bm, c), lambda j, i, k: (i, 0)) if dy.ndim == 2 else ((d, bm // d, c), lambda j, i, k: (0, i, 0))
        epi_in = () if add is None else (add,)
        return _mm(name, dy, w, grid2=(k_len // bn, s_len // bm), nk=1, a_blk=a_blk, a_map=a_map,
                   b_blk=(None, bn, c), b_map=lambda j, i, k: (slot, j, 0),
                   outs=[((s_len, k_len), (bm, bn), lambda j, i, k: (i, j), F32)], dims=NT,
                   epi=_make_epi_token_order(d, add is not None), epi_in=epi_in,
                   epi_specs=(pl.BlockSpec((bm, bn), lambda j, i, k: (i, j)),) * len(epi_in),
                   epi_scratch=(_col_scratch(bm, bn),) if d > 1 else ())
    s_len, n_len = dy.shape
    bm = _tile(s_len, 1024)
    if w.ndim == 3:
        ns, k_len, c = w.shape
        bk, nk = c, ns
        bn = _tile(k_len, 1024)
        b_blk, b_map = (None, bn, c), (lambda j, i, k: (k, j, 0))
    else:
        k_len = w.shape[0]
        bk = _tile(n_len, 1024)
        nk = n_len // bk
        bn = _tile(k_len, 1024)
        b_blk, b_map = (bn, bk), (lambda j, i, k: (j, k))
    epi, epi_in, epi_specs = None, (), ()
    if add is not None:
        epi, epi_in = _epi_add, (add,)
        epi_specs = (pl.BlockSpec((bm, bn), lambda j, i, k: (i, j)),)
    return _mm(name, dy, w, grid2=(k_len // bn, s_len // bm), nk=nk, a_blk=(bm, bk), a_map=lambda j, i, k: (i, k),
               b_blk=b_blk, b_map=b_map, outs=[((s_len, k_len), (bm, bn), lambda j, i, k: (i, j), F32)],
               dims=NT, epi=epi, epi_in=epi_in, epi_specs=epi_specs, acc_shape=(bm, bn))


def _mm_grad_w(name, a, dy, *, col_shards=None, slot=None, into=None):
    s_len, k_len = a.shape
    n_len = dy.shape[1]
    bk = _tile(s_len, 1024)
    bm = _tile(k_len, 1024)
    if slot is not None:
        bn = _tile(n_len, 1024)
        out = ((col_shards, k_len, n_len), (None, bm, bn), lambda j, i, k: (slot, i, j), BF16)
    elif col_shards:
        c = n_len // col_shards
        bn = _tile(c, 1024)
        sub = c // bn
        out = ((col_shards, k_len, c), (None, bm, bn), lambda j, i, k: (j // sub, i, j % sub), BF16)
    else:
        bn = _tile(n_len, 1024)
        out = ((k_len, n_len), (bm, bn), lambda j, i, k: (i, j), BF16)
    return _mm(name, a, dy, grid2=(n_len // bn, k_len // bm), nk=s_len // bk,
               a_blk=(bk, bm), a_map=lambda j, i, k: (k, i), b_blk=(bk, bn), b_map=lambda j, i, k: (k, j),
               outs=[out], dims=TN, acc_shape=(bm, bn), into=None if into is None else (into, 0))


def _mm_grp_fwd(name, pooled, wg):
    s_len, e = pooled.shape
    ng, g, _ = wg.shape
    bm = _tile(s_len, 1024)
    return _mm(name, pooled, wg, grid2=(ng, s_len // bm), nk=1, a_blk=(bm, g), a_map=lambda j, i, k: (i, j),
               b_blk=(None, g, g), b_map=lambda j, i, k: (j, 0, 0),
               outs=[((s_len, e), (bm, g), lambda j, i, k: (i, j), F32)], dims=NN)


def _mm_grp_grad_act(name, dy, wg):
    s_len, e = dy.shape
    ng, g, _ = wg.shape
    bm = _tile(s_len, 1024)
    return _mm(name, dy, wg, grid2=(ng, s_len // bm), nk=1, a_blk=(bm, g), a_map=lambda j, i, k: (i, j),
               b_blk=(None, g, g), b_map=lambda j, i, k: (j, 0, 0),
               outs=[((s_len, e), (bm, g), lambda j, i, k: (i, j), F32)], dims=NT)


def _mm_grp_grad_w(name, pooled, dy, ng):
    s_len, e = pooled.shape
    g = e // ng
    bk = _tile(s_len, 1024)
    return _mm(name, pooled, dy, grid2=(ng, 1), nk=s_len // bk, a_blk=(bk, g), a_map=lambda j, i, k: (k, j),
               b_blk=(bk, g), b_map=lambda j, i, k: (k, j),
               outs=[((N_CHIPS, ng, g // N_CHIPS, g), (N_CHIPS, None, g // N_CHIPS, g), lambda j, i, k: (0, j, 0, 0), BF16)],
               dims=TN, acc_shape=(g, g))


def _row_spec(bs, width, col=0):
    return pl.BlockSpec((bs, width), lambda i: (i, col))


def _vec_spec(width):
    return pl.BlockSpec((1, width), lambda i: (0, 0))


def _rows_call(body, name, s_len, in_specs, out_specs, out_shape, bs, aliases=None, sequential=False):
    return pl.pallas_call(
        body, name=name, grid=(s_len // bs,), in_specs=in_specs, out_specs=out_specs, out_shape=out_shape,
        input_output_aliases=aliases or {},
        compiler_params=_params(("arbitrary",) if sequential else ("parallel",)))


def _accumulate(ref, part):
    i = pl.program_id(0)

    @pl.when(i == 0)
    def _():
        ref[...] = part

    @pl.when(i > 0)
    def _():
        ref[...] += part


def _rms_scale(xf):
    return lax.rsqrt(jnp.mean(xf * xf, axis=-1, keepdims=True) + RMS_EPS)


def _res_spec(dil, bs, width):
    return pl.BlockSpec((dil, bs // dil, width), lambda i: (0, i, 0))


def _res_shape(dil, s_len, width, dtype):
    return jax.ShapeDtypeStruct((dil, s_len // dil, width), dtype)


def _rmsnorm_fwd(name, x, gain, dils=()):
    s_len, d = x.shape
    bs = _tile(s_len, 256)

    def body(x_ref, g_ref, h_ref, *rest):
        xf = x_ref[...]
        h = (xf * _rms_scale(xf)) * g_ref[...]
        h_ref[...] = h.astype(BF16)
        if dils:
            for c, sl in enumerate(_col_blocks(d)):
                scr = rest[-1].at[c]
                scr[...] = h[:, sl]
                for o_ref, dil in zip(rest[:-1], dils):
                    _to_residue_major(o_ref, scr, dil, sl)

    res = pl.pallas_call(
        body, name=name, grid=(s_len // bs,), in_specs=[_row_spec(bs, d), _vec_spec(d)],
        out_specs=[_row_spec(bs, d)] + [_res_spec(dil, bs, d) for dil in dils],
        out_shape=[jax.ShapeDtypeStruct((s_len, d), BF16)] + [_res_shape(dil, s_len, d, BF16) for dil in dils],
        scratch_shapes=[_col_scratch(bs, d)] if dils else [],
        compiler_params=_params(("parallel",)))(x, gain)
    return res[0] if not dils else tuple(res)


def _rmsnorm_bwd(name, x, gain, dh, dres):
    s_len, d = x.shape
    bs = _tile(s_len, 256)

    def body(x_ref, g_ref, dh_ref, dres_ref, dx_ref, dxb_ref, dg_ref):
        xf = x_ref[...]
        r = _rms_scale(xf)
        xh = xf * r
        dh_f = dh_ref[...]
        t = dh_f * g_ref[...]
        dx = dres_ref[...] + r * (t - xh * jnp.mean(t * xh, axis=-1, keepdims=True))
        dx_ref[...] = dx
        dxb_ref[...] = dx.astype(BF16)
        _accumulate(dg_ref, jnp.sum(dh_f * xh, axis=0, keepdims=True))

    return _rows_call(
        body, name, s_len,
        [_row_spec(bs, d), _vec_spec(d), _row_spec(bs, d), _row_spec(bs, d)],
        [_row_spec(bs, d), _row_spec(bs, d), _vec_spec(d)],
        [jax.ShapeDtypeStruct((s_len, d), F32), jax.ShapeDtypeStruct((s_len, d), BF16),
         jax.ShapeDtypeStruct((1, d), F32)], bs, sequential=True)(x, gain, dh, dres)


def _loss_head(name, x, gain, target):
    s_len, d = x.shape
    bs = _tile(s_len, 256)

    def body(x_ref, g_ref, t_ref, lv_ref, dx_ref, dxb_ref, dg_ref):
        xf = x_ref[...]
        r = _rms_scale(xf)
        xh = xf * r
        err = xh * g_ref[...] - t_ref[...]
        dy = err * (1.0 / d)
        t = dy * g_ref[...]
        dx = r * (t - xh * jnp.mean(t * xh, axis=-1, keepdims=True))
        dx_ref[...] = dx
        dxb_ref[...] = dx.astype(BF16)
        _accumulate(lv_ref, jnp.sum(err * err, axis=0, keepdims=True))
        _accumulate(dg_ref, jnp.sum(dy * xh, axis=0, keepdims=True))

    return _rows_call(
        body, name, s_len, [_row_spec(bs, d), _vec_spec(d), _row_spec(bs, d)],
        [_vec_spec(d), _row_spec(bs, d), _row_spec(bs, d), _vec_spec(d)],
        [jax.ShapeDtypeStruct((1, d), F32), jax.ShapeDtypeStruct((s_len, d), F32),
         jax.ShapeDtypeStruct((s_len, d), BF16), jax.ShapeDtypeStruct((1, d), F32)],
        bs, sequential=True)(x, gain, target)


def _sigmoid(g):
    return 1.0 / (1.0 + jnp.exp(-g))


def _gate_a_fwd(name, ypre, proj, scale):
    s_len, e = ypre.shape
    bs = _tile(s_len, 256)

    def body(y_ref, g_ref, sc_ref, z_ref):
        g = g_ref[...]
        z_ref[...] = (y_ref[...] * sc_ref[...] * (g * _sigmoid(g))).astype(BF16)

    return _rows_call(body, name, s_len, [_row_spec(bs, e), _row_spec(bs, e, 1), _vec_spec(e)], _row_spec(bs, e),
                      jax.ShapeDtypeStruct((s_len, e), BF16), bs)(ypre, proj, scale)


def _gate_a_bwd(name, dz, ypre, proj, scale):
    s_len, e = ypre.shape
    bs = _tile(s_len, 256)

    def body(dz_ref, y_ref, g_ref, sc_ref, dy_ref, dproj_ref, dsc_ref):
        g = g_ref[...]
        sg = _sigmoid(g)
        silu = g * sg
        dz_f = dz_ref[...]
        ypre_f = y_ref[...]
        dys = dz_f * silu
        dy_ref[...] = (dys * sc_ref[...]).astype(BF16)
        dproj_ref[...] = (dz_f * (ypre_f * sc_ref[...]) * (sg * (1.0 + g * (1.0 - sg)))).astype(BF16)
        _accumulate(dsc_ref, jnp.sum(dys * ypre_f, axis=0, keepdims=True))

    return _rows_call(
        body, name, s_len, [_row_spec(bs, e), _row_spec(bs, e), _row_spec(bs, e, 1), _vec_spec(e)],
        [_row_spec(bs, e), _row_spec(bs, e, 1), _vec_spec(e)],
        [jax.ShapeDtypeStruct((s_len, e), BF16), jax.ShapeDtypeStruct((s_len, 2 * e), BF16),
         jax.ShapeDtypeStruct((1, e), F32)], bs, sequential=True)(dz, ypre, proj, scale)


def _merge_gate_fwd(name, outs, lses, gate, dils):
    s_len, e = gate.shape
    bs = _tile(s_len, 256)
    n = len(outs)

    def body(*refs):
        o_refs, l_refs, g_ref = refs[:n], refs[n:2 * n], refs[2 * n]
        m_ref, lj_ref, z_ref = refs[2 * n + 1:2 * n + 4]
        scratch = refs[2 * n + 4]
        for c, sl in enumerate(_col_blocks(e)):
            ls = [_from_residue_major(r, scratch.at[2 * j, c], dil, sl) for j, (r, dil) in enumerate(zip(l_refs, dils))]
            os_ = [_from_residue_major(r, scratch.at[2 * j + 1, c], dil, sl) for j, (r, dil) in enumerate(zip(o_refs, dils))]
            mx = functools.reduce(jnp.maximum, ls)
            ws = [jnp.exp(l - mx) for l in ls]
            den = functools.reduce(lambda a, b: a + b, ws)
            merged = functools.reduce(lambda a, b: a + b, [w * o for w, o in zip(ws, os_)]) / den
            g = g_ref[:, sl]
            m_ref[:, sl] = merged.astype(BF16)
            lj_ref[:, sl] = mx + jnp.log(den)
            z_ref[:, sl] = (merged * (g * _sigmoid(g))).astype(BF16)

    spec = _row_spec(bs, e)
    res_specs = [_res_spec(dil, bs, e) for dil in dils]
    return pl.pallas_call(
        body, name=name, grid=(s_len // bs,), in_specs=res_specs + res_specs + [spec], out_specs=[spec] * 3,
        out_shape=[jax.ShapeDtypeStruct((s_len, e), BF16), jax.ShapeDtypeStruct((s_len, e), F32),
                   jax.ShapeDtypeStruct((s_len, e), BF16)],
        scratch_shapes=[pltpu.VMEM((2 * n, e // HEAD_DIM, bs, HEAD_DIM), F32)],
        compiler_params=_params(("parallel",)))(*outs, *lses, gate)


def _gate_b_bwd(name, dz, merged, gate, lse, dils):
    s_len, e = gate.shape
    bs = _tile(s_len, 256)
    n = len(dils)

    def body(dz_ref, m_ref, g_ref, l_ref, dg_ref, *rest):
        out_refs, scratch = rest[:3 * n], rest[3 * n]
        for c, sl in enumerate(_col_blocks(e)):
            g = g_ref[:, sl]
            sg = _sigmoid(g)
            dz_f = dz_ref[:, sl]
            merged = m_ref[:, sl].astype(F32)
            dmerged = dz_f * (g * sg)
            dg_ref[:, sl] = (dz_f * merged * (sg * (1.0 + g * (1.0 - sg)))).astype(BF16)
            values = (dmerged, l_ref[:, sl],
                      jnp.broadcast_to(jnp.sum(dmerged * merged, axis=-1, keepdims=True), (bs, HEAD_DIM)))
            for t, val in enumerate(values):
                scr = scratch.at[t, c]
                scr[...] = val
                for j, dil in enumerate(dils):
                    _to_residue_major(out_refs[3 * j + t], scr, dil, sl)

    spec = _row_spec(bs, e)
    out_specs, out_shape = [spec], [jax.ShapeDtypeStruct((s_len, e), BF16)]
    for dil in dils:
        out_specs += [_res_spec(dil, bs, e)] * 3
        out_shape += [_res_shape(dil, s_len, e, BF16), _res_shape(dil, s_len, e, F32), _res_shape(dil, s_len, e, F32)]
    res = pl.pallas_call(
        body, name=name, grid=(s_len // bs,), in_specs=[spec] * 4, out_specs=out_specs, out_shape=out_shape,
        scratch_shapes=[pltpu.VMEM((3, e // HEAD_DIM, bs, HEAD_DIM), F32)],
        compiler_params=_params(("parallel",)))(dz, merged, gate, lse)
    return res[0], [tuple(res[1 + 3 * j:4 + 3 * j]) for j in range(n)]


def _kv_grad_prep(name, dk_accs, dv_accs, dils, cos, sin_inv):
    n = len(dils)
    e = dk_accs[0].shape[-1]
    s_len = dk_accs[0].shape[0] * dk_accs[0].shape[1]
    bs = _tile(s_len, 256)

    def body(*refs):
        dk_refs, dv_refs = refs[:n], refs[n:2 * n]
        c_ref, s_ref, dkb_ref, dvb_ref, scratch = refs[2 * n:]
        cos_t, sin_t = c_ref[...], s_ref[...]
        add = lambda a, b: a + b
        for c, sl in enumerate(_col_blocks(e)):
            dk = functools.reduce(add, [_from_residue_major(r, scratch.at[j, c], dil, sl)
                                        for j, (r, dil) in enumerate(zip(dk_refs, dils))])
            dkb_ref[:, sl] = _rope_apply(dk, cos_t, sin_t).astype(BF16)
            dv = functools.reduce(add, [_from_residue_major(r, scratch.at[n + j, c], dil, sl)
                                        for j, (r, dil) in enumerate(zip(dv_refs, dils))])
            dvb_ref[:, sl] = dv.astype(BF16)

    spec, rspec = _row_spec(bs, e), _row_spec(bs, HEAD_DIM)
    res_specs = [_res_spec(dil, bs, e) for dil in dils]
    return pl.pallas_call(
        body, name=name, grid=(s_len // bs,), in_specs=res_specs + res_specs + [rspec, rspec], out_specs=[spec, spec],
        out_shape=[jax.ShapeDtypeStruct((s_len, e), BF16)] * 2,
        scratch_shapes=[pltpu.VMEM((2 * n, e // HEAD_DIM, bs, HEAD_DIM), F32)],
        compiler_params=_params(("parallel",)))(*dk_accs, *dv_accs, cos, sin_inv)


def _pool_cols(e):
    return _tile(e // len(POOL_WINDOWS), 256)


def _window_sum(val, grp, s_len, forward):
    rows = lax.broadcasted_iota(jnp.int32, val.shape, 0)
    acc = val
    for level in range(len(POOL_WINDOWS)):
        step = 1 << level
        if forward:
            shifted = jnp.where(rows >= step, pltpu.roll(acc, step, 0), 0.0)
        else:
            shifted = jnp.where(rows < s_len - step, pltpu.roll(acc, s_len - step, 0), 0.0)
        acc = jnp.where(level <= grp, acc + shifted, acc)
    return acc


def _window_count(shape, grp):
    rows = lax.broadcasted_iota(jnp.int32, shape, 0)
    return jnp.minimum(rows + 1, jnp.left_shift(2, grp)).astype(F32)


def _pool_fwd(name, proj):
    s_len, e2 = proj.shape
    e = e2 // 2
    cb = _pool_cols(e)
    per_grp = e // len(POOL_WINDOWS) // cb
    assert POOL_WINDOWS == tuple(2 << g for g in range(len(POOL_WINDOWS)))

    def body(u_ref, p_ref):
        grp = pl.program_id(0)
        u = u_ref[...]
        total = _window_sum(u, grp, s_len, True)
        p_ref[...] = (total / _window_count(u.shape, grp) - u).astype(BF16)

    spec = pl.BlockSpec((s_len, cb), lambda g, c: (0, g * per_grp + c))
    return pl.pallas_call(
        body, name=name, grid=(len(POOL_WINDOWS), per_grp), in_specs=[spec], out_specs=spec,
        out_shape=jax.ShapeDtypeStruct((s_len, e), BF16), compiler_params=_params(("parallel", "parallel")))(proj)


def _pool_bwd(name, dpooled, dproj):
    s_len, e = dpooled.shape
    cb = _pool_cols(e)
    per_grp = e // len(POOL_WINDOWS) // cb

    def body(dp_ref, _, du_ref):
        grp = pl.program_id(0)
        dp = dp_ref[...]
        total = _window_sum(dp / _window_count(dp.shape, grp), grp, s_len, False)
        du_ref[...] = (total - dp).astype(BF16)

    spec = pl.BlockSpec((s_len, cb), lambda g, c: (0, g * per_grp + c))
    return pl.pallas_call(
        body, name=name, grid=(len(POOL_WINDOWS), per_grp), in_specs=[spec, ANY], out_specs=spec,
        out_shape=jax.ShapeDtypeStruct(dproj.shape, BF16), input_output_aliases={1: 0},
        compiler_params=_params(("parallel", "parallel")))(dpooled, dproj)


def _band_masks(nb, first):
    row = lax.broadcasted_iota(jnp.int32, (nb, nb), 0)
    col = lax.broadcasted_iota(jnp.int32, (nb, nb), 1)
    return col >= row + jnp.where(first, 2 * nb, 0), col <= row


def _dot(a, b, dims):
    return lax.dot_general(a, b, (dims, ((), ())), preferred_element_type=F32)


def _attn_fwd(name, window, q, k, v):
    dil, m, e = k.shape
    nb = window // dil
    nblk = m // nb
    heads = e // HEAD_DIM

    def body(q_ref, kp_ref, kc_ref, vp_ref, vc_ref, o_ref, l_ref):
        mask_p, mask_c = _band_masks(nb, pl.program_id(1) == 0)
        cols = _col_blocks(e)
        s_p = [jnp.where(mask_p, _dot(q_ref[:, sl], kp_ref[:, sl], NT), NEG_INF) for sl in cols]
        s_c = [jnp.where(mask_c, _dot(q_ref[:, sl], kc_ref[:, sl], NT), NEG_INF) for sl in cols]
        mx = [jnp.maximum(jnp.max(a, axis=-1, keepdims=True), jnp.max(b, axis=-1, keepdims=True))
              for a, b in zip(s_p, s_c)]
        p_p = [jnp.exp(a - m) for a, m in zip(s_p, mx)]
        p_c = [jnp.exp(a - m) for a, m in zip(s_c, mx)]
        den = [jnp.sum(a, axis=-1, keepdims=True) + jnp.sum(b, axis=-1, keepdims=True) for a, b in zip(p_p, p_c)]
        for h, sl in enumerate(cols):
            out = _dot(p_p[h].astype(BF16), vp_ref[:, sl], NN) + _dot(p_c[h].astype(BF16), vc_ref[:, sl], NN)
            o_ref[:, sl] = (out / den[h]).astype(BF16)
            l_ref[:, sl] = jnp.broadcast_to(mx[h] + jnp.log(den[h]), (nb, HEAD_DIM))

    blk = (None, nb, e)
    prev = lambda r, n: (r, jnp.maximum(n - 1, 0), 0)
    cur = lambda r, n: (r, n, 0)
    return pl.pallas_call(
        body, name=name, grid=(dil, nblk),
        in_specs=[pl.BlockSpec(blk, cur), pl.BlockSpec(blk, prev), pl.BlockSpec(blk, cur), pl.BlockSpec(blk, prev),
                  pl.BlockSpec(blk, cur)],
        out_specs=[pl.BlockSpec(blk, cur), pl.BlockSpec(blk, cur)],
        out_shape=[jax.ShapeDtypeStruct((dil, m, e), BF16), jax.ShapeDtypeStruct((dil, m, e), F32)],
        compiler_params=_params(("parallel", "arbitrary")),
    )(q, k, k, v, v)


def _attn_bwd(name, window, scale, q, k, v, dout, lse, delta, cos, sin_inv, dk_acc, dv_acc):
    dil, m, e = k.shape
    nb = window // dil
    nblk = m // nb
    heads = e // HEAD_DIM

    def body(k_ref, v_ref, qc_ref, qn_ref, doc_ref, don_ref, lc_ref, ln_ref, dlc_ref, dln_ref, c_ref, s_ref,
             dki_ref, dvi_ref, dq_ref, dko_ref, dvo_ref, carry_ref):
        n = pl.program_id(1)

        @pl.when(n == 0)
        def _():
            carry_ref[...] = jnp.zeros_like(carry_ref)

        mask_n, mask_c = _band_masks(nb, n == nblk - 1)
        cos_t, sin_t = c_ref[...], s_ref[...]
        cols = _col_blocks(e)
        stat = lambda ref, sl: ref[:, sl] if nb == HEAD_DIM else ref[:, sl][:, :1]
        s_c = [_dot(qc_ref[:, sl], k_ref[:, sl], NT) for sl in cols]
        s_n = [_dot(qn_ref[:, sl], k_ref[:, sl], NT) for sl in cols]
        dp_c = [_dot(doc_ref[:, sl], v_ref[:, sl], NT) for sl in cols]
        dp_n = [_dot(don_ref[:, sl], v_ref[:, sl], NT) for sl in cols]
        p_c = [jnp.where(mask_c, jnp.exp(s - stat(lc_ref, sl)), 0.0) for s, sl in zip(s_c, cols)]
        p_n = [jnp.where(mask_n, jnp.exp(s - stat(ln_ref, sl)), 0.0) for s, sl in zip(s_n, cols)]
        ds_c = [(p * (dp - stat(dlc_ref, sl))).astype(BF16) for p, dp, sl in zip(p_c, dp_c, cols)]
        ds_n = [(p * (dp - stat(dln_ref, sl))).astype(BF16) for p, dp, sl in zip(p_n, dp_n, cols)]
        for h, sl in enumerate(cols):
            dq = (carry_ref[:, sl] + _dot(ds_c[h], k_ref[:, sl], NN)) * scale
            dq_ref[:, sl] = _rope_apply(dq, cos_t, sin_t).astype(BF16)
        for h, sl in enumerate(cols):
            carry_ref[:, sl] = _dot(ds_n[h], k_ref[:, sl], NN)
        for h, sl in enumerate(cols):
            dko_ref[:, sl] = dki_ref[:, sl] + _dot(ds_c[h], qc_ref[:, sl], TN) + _dot(ds_n[h], qn_ref[:, sl], TN)
            dvo_ref[:, sl] = (dvi_ref[:, sl] + _dot(p_c[h].astype(BF16), doc_ref[:, sl], TN)
                              + _dot(p_n[h].astype(BF16), don_ref[:, sl], TN))

    blk = (None, nb, e)
    cur = lambda r, n: (r, n, 0)
    nxt = lambda r, n: (r, jnp.minimum(n + 1, nblk - 1), 0)
    rblk = (None, nb, HEAD_DIM)
    both = lambda shape: [pl.BlockSpec(shape, cur), pl.BlockSpec(shape, nxt)]
    return pl.pallas_call(
        body, name=name, grid=(dil, nblk),
        in_specs=[pl.BlockSpec(blk, cur), pl.BlockSpec(blk, cur), *both(blk), *both(blk), *both(blk), *both(blk),
                  pl.BlockSpec(rblk, cur), pl.BlockSpec(rblk, cur), pl.BlockSpec(blk, cur), pl.BlockSpec(blk, cur)],
        out_specs=[pl.BlockSpec(blk, cur)] * 3,
        out_shape=[jax.ShapeDtypeStruct((dil, m, e), BF16),
                   jax.ShapeDtypeStruct((dil, m, e), F32), jax.ShapeDtypeStruct((dil, m, e), F32)],
        scratch_shapes=[pltpu.VMEM((nb, e), F32)],
        input_output_aliases={12: 1, 13: 2},
        compiler_params=_params(("parallel", "arbitrary")),
    )(k, v, q, q, dout, dout, lse, lse, delta, delta, cos, sin_inv, dk_acc, dv_acc)


def _rope_tables(s_len):
    inv_freq = 1.0 / (ROPE_THETA ** (jnp.arange(0, HEAD_DIM, 2, dtype=F32) / HEAD_DIM))
    ang = jnp.arange(s_len, dtype=F32)[:, None] * inv_freq[None, :]
    cos, sin = jnp.cos(ang), jnp.sin(ang)
    return jnp.concatenate([cos, cos], axis=1), jnp.concatenate([-sin, sin], axis=1)


def _row(vec):
    return vec.reshape(1, -1)


def _local_step(x, target, n_a, n_b, fetch, emit):
    s_len, d = x.shape
    n_q = len(DILATED_PAIRS)
    cos, sin = _rope_tables(s_len)
    sin_inv = -sin
    q_scale = 1.0 / math.sqrt(HEAD_DIM)
    w = {}

    def need(group, after):
        for name, (layer, arr) in fetch(group, after).items():
            w.setdefault(name, {})[layer] = arr

    saved_a = []
    for i in range(n_a):
        need(f"a{i}", x)
        h = _rmsnorm_fwd(f"a{i}_norm", x, _row(w["norm_a"][i]))
        proj = _mm_act_w(f"a{i}_in", h, w["w_in_a"][i], out_dtype=F32)
        pooled = _pool_fwd(f"a{i}_pool", proj)
        ypre = _mm_grp_fwd(f"a{i}_grp", pooled, w["w_grp_a"][i])
        z = _gate_a_fwd(f"a{i}_gate", ypre, proj, _row(w["scale_a"][i]))
        x_next = _mm_act_w(f"a{i}_out", z, w["w_out_a"][i], out_dtype=F32, add=x)
        saved_a.append((x, h, proj, pooled, ypre, z))
        x = x_next

    x_kv = x
    need("kv", x)
    e = w["w_k"][0].shape[1]
    kv_in = _rmsnorm_fwd("kv_norm", x, _row(w["norm_kv"][0]))
    windows = [window for window, _ in DILATED_PAIRS]
    dils = tuple(dil for _, dil in DILATED_PAIRS)
    far_dils = tuple(dil for dil in dils if dil > 1)
    ks = _mm_act_w("kv_k", kv_in, w["w_k"][0], rope=(cos, sin, 1.0), dils=dils)
    vs = _mm_act_w("kv_v", kv_in, w["w_v"][0], dils=dils)

    saved_b = []
    for i in range(n_b):
        need(f"b{i}", x if i > 0 else vs[0])
        hs = _rmsnorm_fwd(f"b{i}_norm", x, _row(w["norm_b"][i]), dils=far_dils)
        hs = {1: hs[0], **{dil: h_d.reshape(s_len, d) for dil, h_d in zip(far_dils, hs[1:])}}
        qs = [_mm_act_w(f"b{i}_q{g}", hs[1], w["w_in_b"][i], rope=(cos, sin, q_scale), n_first=g, n_cols=1,
                        dils=(dil,))[0] for g, dil in enumerate(dils)]
        gate = _mm_act_w(f"b{i}_g", hs[1], w["w_in_b"][i], out_dtype=F32, n_first=n_q, n_cols=1)
        outs, lses = [], []
        for g in range(n_q):
            o_g, l_g = _attn_fwd(f"b{i}_attn{g}", windows[g], qs[g], ks[g], vs[g])
            outs.append(o_g)
            lses.append(l_g)
        merged, lse, z = _merge_gate_fwd(f"b{i}_merge", outs, lses, gate, dils)
        x_next = _mm_act_w(f"b{i}_out", z, w["w_out_b"][i], out_dtype=F32, add=x)
        saved_b.append((x, hs, qs, gate, merged, lse, z))
        x = x_next

    need("head", x)
    loss_vec, dx, dxb, g_norm_f = _loss_head("loss_head", x, _row(w["norm_f"][0]), target)

    small = {"norm_a": {}, "scale_a": {}, "norm_kv": {}, "norm_b": {}, "norm_f": {0: g_norm_f}}
    shard_rows = lambda g2: g2.reshape(N_CHIPS, g2.shape[0] // N_CHIPS, g2.shape[1])

    res_major = lambda t, dil: t.reshape(s_len // dil, dil, t.shape[1]).transpose(1, 0, 2)
    cos_r = [res_major(cos, dil) for dil in dils]
    sin_inv_r = [res_major(sin_inv, dil) for dil in dils]
    dk_accs = [jnp.zeros((dil, s_len // dil, e), F32) for dil in dils]
    dv_accs = [jnp.zeros((dil, s_len // dil, e), F32) for dil in dils]
    for i in reversed(range(n_b)):
        x_in, hs, qs, gate, merged, lse, z = saved_b[i]
        dz = _mm_grad_act(f"b{i}_dz", dxb, w["w_out_b"][i])
        g_out = shard_rows(_mm_grad_w(f"b{i}_gwo", z, dxb))
        dgate, stats = _gate_b_bwd(f"b{i}_dgate", dz, merged, gate, lse, dils)
        dh = _mm_grad_act(f"b{i}_dh{n_q}", dgate, w["w_in_b"][i], slot=n_q)
        g_in = _mm_grad_w(f"b{i}_gwi{n_q}", hs[1], dgate, col_shards=n_q + 1, slot=n_q)
        for g, dil in enumerate(dils):
            dout, lse_g, delta_g = stats[g]
            dq, dk_accs[g], dv_accs[g] = _attn_bwd(f"b{i}_dattn{g}", windows[g], q_scale, qs[g], ks[g], vs[g], dout,
                                                   lse_g, delta_g, cos_r[g], sin_inv_r[g], dk_accs[g], dv_accs[g])
            dh = _mm_grad_act(f"b{i}_dh{g}", dq if dil > 1 else dq[0], w["w_in_b"][i], add=dh, slot=g)
            g_in = _mm_grad_w(f"b{i}_gwi{g}", hs[dil], dq.reshape(s_len, e), col_shards=n_q + 1, slot=g, into=g_in)
        dx, dxb, small["norm_b"][i] = _rmsnorm_bwd(f"b{i}_dnorm", x_in, _row(w["norm_b"][i]), dh, dx)
        dx, dxb = emit(f"b{i}", {"w_in_b": (i, g_in), "w_out_b": (i, g_out)}, (dx, dxb))

    dkb, dvb = _kv_grad_prep("kv_dprep", dk_accs, dv_accs, dils, cos, sin_inv)
    dkv = _mm_grad_act("kv_dk", dkb, w["w_k"][0])
    dkv = _mm_grad_act("kv_dv", dvb, w["w_v"][0], add=dkv)
    g_k = shard_rows(_mm_grad_w("kv_gwk", kv_in, dkb))
    g_v = shard_rows(_mm_grad_w("kv_gwv", kv_in, dvb))
    dx, dxb, small["norm_kv"][0] = _rmsnorm_bwd("kv_dnorm", x_kv, _row(w["norm_kv"][0]), dkv, dx)
    dx, dxb = emit("kv", {"w_k": (0, g_k), "w_v": (0, g_v)}, (dx, dxb))

    for i in reversed(range(n_a)):
        x_in, h, proj, pooled, ypre, z = saved_a[i]
        dz = _mm_grad_act(f"a{i}_dz", dxb, w["w_out_a"][i])
        g_out = shard_rows(_mm_grad_w(f"a{i}_gwo", z, dxb))
        dypre, dproj, small["scale_a"][i] = _gate_a_bwd(f"a{i}_dgate", dz, ypre, proj, _row(w["scale_a"][i]))
        dpooled = _mm_grp_grad_act(f"a{i}_dgrp", dypre, w["w_grp_a"][i])
        g_grp = _mm_grp_grad_w(f"a{i}_gwg", pooled, dypre, len(POOL_WINDOWS))
        g_grp = g_grp.reshape(N_CHIPS, -1, g_grp.shape[-1])
        last = i == 0
        if last:
            (dpooled,) = emit(f"a{i}", {"w_grp_a": (i, g_grp), "w_out_a": (i, g_out)}, (dpooled,))
        dproj = _pool_bwd(f"a{i}_dpool", dpooled, dproj)
        g_in = _mm_grad_w(f"a{i}_gwi", h, dproj, col_shards=N_CHIPS)
        if last:
            (dproj,) = emit(f"a{i}i", {"w_in_a": (i, g_in)}, (dproj,))
        dh = _mm_grad_act(f"a{i}_dh", dproj, w["w_in_a"][i])
        dx, dxb, small["norm_a"][i] = _rmsnorm_bwd(f"a{i}_dnorm", x_in, _row(w["norm_a"][i]), dh, dx)
        if not last:
            dx, dxb = emit(f"a{i}", {"w_in_a": (i, g_in), "w_grp_a": (i, g_grp), "w_out_a": (i, g_out)}, (dx, dxb))

    return loss_vec, dx, small


BIG_WEIGHTS = ("w_in_a", "w_grp_a", "w_out_a", "w_k", "w_v", "w_in_b", "w_out_b")


def _pair_add(name, grad, recv, c_idx):
    _, r, cols = grad.shape
    half = r // 2
    rb = _tile(half, 256)
    nrb = half // rb

    def body(c_ref, g_ref, r_ref, o_ref):
        o_ref[...] = (g_ref[...].astype(F32) + r_ref[...].astype(F32)).astype(BF16)

    blk = (None, rb, cols)
    grid_spec = pltpu.PrefetchScalarGridSpec(
        num_scalar_prefetch=1, grid=(N_CHIPS, nrb),
        in_specs=[pl.BlockSpec(blk, lambda s, i, c: (s, c[0] * nrb + i, 0)), pl.BlockSpec(blk, lambda s, i, c: (s, i, 0))],
        out_specs=pl.BlockSpec(blk, lambda s, i, c: (s, i, 0)))
    return pl.pallas_call(body, name=name, grid_spec=grid_spec,
                          out_shape=jax.ShapeDtypeStruct((N_CHIPS, half, cols), BF16),
                          compiler_params=_params(("parallel", "parallel")))(c_idx, grad, recv)


def _final_add(name, part, recv, sc_idx, layer, n_layers, into=None):
    _, half, cols = part.shape
    rb = _tile(half, 256)
    nrb = half // rb
    n_peer = recv.shape[0]

    def body(sc_ref, p_ref, *refs):
        acc = p_ref[...].astype(F32)
        for r_ref in refs[:n_peer]:
            acc = acc + r_ref[...].astype(F32)
        refs[-1][...] = acc

    blk = (None, rb, cols)
    peer_spec = lambda k: pl.BlockSpec(blk, lambda i, sc: (k, i, 0))
    grid_spec = pltpu.PrefetchScalarGridSpec(
        num_scalar_prefetch=1, grid=(nrb,),
        in_specs=[pl.BlockSpec(blk, lambda i, sc: (sc[0], i, 0))] + [peer_spec(k) for k in range(n_peer)]
                 + ([] if into is None else [ANY]),
        out_specs=pl.BlockSpec(blk, lambda i, sc: (layer, sc[1] * nrb + i, 0)))
    extra = () if into is None else (into,)
    return pl.pallas_call(body, name=name, grid_spec=grid_spec,
                          out_shape=jax.ShapeDtypeStruct((n_layers, 2 * half, cols), F32),
                          input_output_aliases={} if into is None else {2 + n_peer: 0},
                          compiler_params=_params(("parallel",)))(sc_idx, part, *([recv] * n_peer), *extra)


def _cast_into_slot(name, arr, layer, s_idx):
    _, b, r, cols = arr.shape
    rb = _tile(r, 512)

    def body(s_ref, a_ref, o_ref):
        o_ref[...] = a_ref[...].astype(BF16)

    blk = (None, None, rb, cols)
    grid_spec = pltpu.PrefetchScalarGridSpec(
        num_scalar_prefetch=1, grid=(b, r // rb),
        in_specs=[pl.BlockSpec(blk, lambda j, i, s: (layer, j, i, 0))],
        out_specs=pl.BlockSpec(blk, lambda j, i, s: (j, s[0], i, 0)))
    return pl.pallas_call(body, name=name, grid_spec=grid_spec,
                          out_shape=jax.ShapeDtypeStruct((b, N_CHIPS, r, cols), BF16),
                          compiler_params=_params(("parallel", "parallel")))(s_idx, arr)


def _sum_devices(name, gathered):
    n_dev, p, d = gathered.shape

    def body(g_ref, o_ref):
        acc = g_ref[0]
        for j in range(1, n_dev):
            acc = acc + g_ref[j]
        o_ref[...] = acc

    return pl.pallas_call(body, name=name, out_shape=jax.ShapeDtypeStruct((p, d), F32),
                          compiler_params=_params())(gathered)


def _adamw(name, w, g, m, v):
    shape = w.shape
    cols = shape[-1]
    flat = lambda a: a.reshape(-1, cols)
    rows = flat(w).shape[0]
    bs = _tile(rows, 256)

    def body(w_ref, g_ref, m_ref, v_ref, d_ref, mo_ref, vo_ref):
        grad = g_ref[...]
        m_new = ADAM_B1 * m_ref[...] + (1.0 - ADAM_B1) * grad
        v_new = ADAM_B2 * v_ref[...] + (1.0 - ADAM_B2) * (grad * grad)
        m_hat = m_new / (1.0 - ADAM_B1 ** ADAM_STEP)
        v_hat = v_new / (1.0 - ADAM_B2 ** ADAM_STEP)
        d_ref[...] = -ADAM_LR * (m_hat / (jnp.sqrt(v_hat) + ADAM_EPS) + ADAM_WD * w_ref[...])
        mo_ref[...] = m_new
        vo_ref[...] = v_new

    spec = _row_spec(bs, cols)
    outs = _rows_call(body, name, rows, [spec] * 4, [spec] * 3, [jax.ShapeDtypeStruct((rows, cols), F32)] * 3, bs)(
        flat(w), flat(g), flat(m), flat(v))
    return tuple(o.reshape(shape) for o in outs)


def _place():
    x, y, c = lax.axis_index("x"), lax.axis_index("y"), lax.axis_index("c")
    chips = [(1 - x, y), (x, 1 - y), (1 - x, 1 - y)]
    return x, y, c, chips


def _chip_index(chip):
    return 2 * chip[0] + chip[1]


def _comm_call(body, name, n_in, out_shape, scratch, aliases=None):
    return pl.pallas_call(body, name=name, in_specs=[ANY] * n_in, out_specs=[ANY] * len(out_shape), out_shape=out_shape,
                          scratch_shapes=scratch, input_output_aliases=aliases or {})


HBM_SPEC = pl.BlockSpec(memory_space=pltpu.HBM)
SEM_SPEC = pl.BlockSpec(memory_space=pltpu.SEMAPHORE)
SPLIT_PARAMS = pltpu.CompilerParams(has_side_effects=pltpu.SideEffectType.DATAFLOW_SIDE_EFFECTING)


def _in_hbm(arr):
    return pltpu.with_memory_space_constraint(arr, pltpu.HBM)


def _slot_half(ref, chip, core):
    half = ref.shape[2] // 2
    return ref.at[:, _chip_index(chip), pl.ds(core * half, half), :]


def _gather_start(name, bufs, carry=()):
    n, n_c = len(bufs), len(carry)

    def body(*refs):
        ins, (send_sems, recv_sems) = refs[:n], refs[n + n_c:n + n_c + 2]
        x, y, c, chips = _place()
        for a in range(n):
            block = _slot_half(ins[a], (x, y), c)
            for k, chip in enumerate(chips):
                pltpu.make_async_remote_copy(src_ref=block, dst_ref=block, send_sem=send_sems.at[3 * a + k],
                                             recv_sem=recv_sems.at[3 * a + k], device_id=(*chip, c),
                                             device_id_type=MESH).start()

    dma = pltpu.SemaphoreType.DMA
    thru = list(bufs) + list(carry)
    res = pl.pallas_call(
        body, name=name, in_specs=[HBM_SPEC] * (n + n_c), out_specs=[SEM_SPEC] * 2 + [HBM_SPEC] * (n + n_c),
        out_shape=[dma((3 * n,)), dma((3 * n,))] + [pltpu.HBM(a.shape, a.dtype) for a in thru],
        input_output_aliases={t: 2 + t for t in range(n + n_c)}, compiler_params=SPLIT_PARAMS,
    )(*[_in_hbm(a) for a in thru])
    return (res[0], res[1]), list(res[2:2 + n]), list(res[2 + n:])


def _gather_wait(name, sems, bufs, after):
    n = len(bufs)

    def body(*refs):
        ins, (send_sems, recv_sems) = refs[:n], refs[n:n + 2]
        x, y, c, chips = _place()
        for a in range(n):
            for k, chip in enumerate(chips):
                mine, theirs = _slot_half(ins[a], (x, y), c), _slot_half(ins[a], chip, c)
                copy = pltpu.make_async_remote_copy(src_ref=mine, dst_ref=theirs, send_sem=send_sems.at[3 * a + k],
                                                    recv_sem=recv_sems.at[3 * a + k], device_id=(*chip, c),
                                                    device_id_type=MESH)
                copy.wait_send()
                copy.wait_recv()

    res = pl.pallas_call(
        body, name=name, in_specs=[HBM_SPEC] * n + [SEM_SPEC, SEM_SPEC, ANY], out_specs=[HBM_SPEC] * n,
        out_shape=[pltpu.HBM(b.shape, b.dtype) for b in bufs], input_output_aliases={a: a for a in range(n)},
        compiler_params=SPLIT_PARAMS)(*bufs, *sems, after)
    return list(res)


def _gather_forward(name, bufs, smalls=()):
    n, n_small = len(bufs), len(smalls)

    def body(*refs):
        small_in = refs[n:n + n_small]
        outs = refs[n + n_small:2 * n + n_small]
        small_out = refs[2 * n + n_small:2 * n + 2 * n_small]
        send_sems, recv_sems, s_send, s_recv, s_local = refs[-5:]
        x, y, c, chips = _place()
        me, sibling = _chip_index((x, y)), (x, y, 1 - c)

        def forward(t, k, core):
            block = _slot_half(outs[t], chips[k], core)
            return pltpu.make_async_remote_copy(src_ref=block, dst_ref=block, send_sem=send_sems.at[t, k],
                                                recv_sem=recv_sems.at[t, k], device_id=sibling, device_id_type=MESH)

        def small_copy(j, k, slot):
            return pltpu.make_async_remote_copy(src_ref=small_in[j], dst_ref=small_out[j].at[slot],
                                                send_sem=s_send.at[j, k], recv_sem=s_recv.at[j, k],
                                                device_id=(*chips[k], c), device_id_type=MESH)

        local = []
        for t in range(n):
            for k in range(3):
                forward(t, k, c).start()
        for j in range(n_small):
            own = pltpu.make_async_copy(small_in[j], small_out[j].at[me], s_local.at[j])
            own.start()
            local.append(own)
            for k in range(3):
                small_copy(j, k, me).start()
        for t in range(n):
            for k in range(3):
                forward(t, k, 1 - c).wait_recv()
        for j in range(n_small):
            for k in range(3):
                small_copy(j, k, _chip_index(chips[k])).wait_recv()
        for t in range(n):
            for k in range(3):
                forward(t, k, c).wait_send()
        for j in range(n_small):
            for k in range(3):
                small_copy(j, k, me).wait_send()
        for own in local:
            own.wait()

    out_shape = [jax.ShapeDtypeStruct(b.shape, BF16) for b in bufs]
    out_shape += [jax.ShapeDtypeStruct((N_CHIPS,) + s.shape, F32) for s in smalls]
    dma = pltpu.SemaphoreType.DMA
    n_s = max(n_small, 1)
    res = _comm_call(body, name, n + n_small, out_shape,
                     [dma((n, 3)), dma((n, 3)), dma((n_s, 3)), dma((n_s, 3)), dma((n_s,))],
                     aliases={t: t for t in range(n)})(*bufs, *smalls)
    return list(res[:n]), list(res[n:])


def _exchange_halves(name, grads):
    n = len(grads)

    def body(*refs):
        g_in, outs = refs[:n], refs[n:2 * n]
        send_sems, recv_sems = refs[-2:]
        x, y, c, _ = _place()
        copies = []
        for t in range(n):
            half = g_in[t].shape[1] // 2
            cp = pltpu.make_async_remote_copy(
                src_ref=g_in[t].at[:, pl.ds((1 - c) * half, half), :], dst_ref=outs[t], send_sem=send_sems.at[t],
                recv_sem=recv_sems.at[t], device_id=(x, y, 1 - c), device_id_type=MESH)
            cp.start()
            copies.append(cp)
        for cp in copies:
            cp.wait()

    out_shape = [jax.ShapeDtypeStruct((g.shape[0], g.shape[1] // 2, g.shape[2]), BF16) for g in grads]
    dma = pltpu.SemaphoreType.DMA
    return list(_comm_call(body, name, n, out_shape, [dma((n,)), dma((n,))])(*grads))


def _scatter_copy(part_ref, land_ref, send_sems, recv_sems, t, k, chip, c):
    return pltpu.make_async_remote_copy(
        src_ref=part_ref.at[_chip_index(chip)], dst_ref=land_ref.at[k], send_sem=send_sems.at[3 * t + k],
        recv_sem=recv_sems.at[3 * t + k], device_id=(*chip, c), device_id_type=MESH)


def _scatter_start(name, parts, carry=()):
    n, n_c = len(parts), len(carry)
    lands = [lax.empty((3,) + p.shape[1:], BF16) for p in parts]

    def body(*refs):
        p_in, l_in = refs[:n], refs[n:2 * n]
        send_sems, recv_sems = refs[2 * n + n_c:2 * n + n_c + 2]
        x, y, c, chips = _place()
        for t in range(n):
            for k, chip in enumerate(chips):
                _scatter_copy(p_in[t], l_in[t], send_sems, recv_sems, t, k, chip, c).start()

    dma = pltpu.SemaphoreType.DMA
    thru = list(parts) + lands + list(carry)
    res = pl.pallas_call(
        body, name=name, in_specs=[HBM_SPEC] * len(thru), out_specs=[SEM_SPEC] * 2 + [HBM_SPEC] * len(thru),
        out_shape=[dma((3 * n,)), dma((3 * n,))] + [pltpu.HBM(a.shape, a.dtype) for a in thru],
        input_output_aliases={t: 2 + t for t in range(len(thru))}, compiler_params=SPLIT_PARAMS,
    )(*[_in_hbm(a) for a in thru])
    return (res[0], res[1]), list(res[2:2 + n]), list(res[2 + n:2 + 2 * n]), list(res[2 + 2 * n:])


def _scatter_wait(name, sems, parts, lands):
    n = len(parts)

    def body(*refs):
        p_in, l_in = refs[:n], refs[n:2 * n]
        send_sems, recv_sems = refs[2 * n:2 * n + 2]
        x, y, c, chips = _place()
        for t in range(n):
            for k, chip in enumerate(chips):
                copy = _scatter_copy(p_in[t], l_in[t], send_sems, recv_sems, t, k, chip, c)
                copy.wait_send()
                copy.wait_recv()

    hbm_out = lambda a: pltpu.HBM(a.shape, a.dtype)
    res = pl.pallas_call(
        body, name=name, in_specs=[HBM_SPEC] * (2 * n) + [SEM_SPEC, SEM_SPEC], out_specs=[HBM_SPEC] * (2 * n),
        out_shape=[hbm_out(a) for a in parts + lands], input_output_aliases={t: t for t in range(2 * n)},
        compiler_params=SPLIT_PARAMS)(*parts, *lands, *sems)
    return list(res[:n]), list(res[n:])


def _share_halves(fulls):
    n = len(fulls)
    items = [(a, l) for a in range(n) for l in range(fulls[a].shape[0])]

    def body(*refs):
        outs = refs[n:2 * n]
        send_sems, recv_sems = refs[-2:]
        x, y, c, _ = _place()

        def copy(t, core):
            a, l = items[t]
            half = outs[a].shape[1] // 2
            block = outs[a].at[l, pl.ds(core * half, half), :]
            return pltpu.make_async_remote_copy(src_ref=block, dst_ref=block, send_sem=send_sems.at[t],
                                                recv_sem=recv_sems.at[t], device_id=(x, y, 1 - c), device_id_type=MESH)

        for t in range(len(items)):
            copy(t, c).start()
        for t in range(len(items)):
            copy(t, 1 - c).wait_recv()
        for t in range(len(items)):
            copy(t, c).wait_send()

    out_shape = [jax.ShapeDtypeStruct(f.shape, F32) for f in fulls]
    dma = pltpu.SemaphoreType.DMA
    return list(_comm_call(body, "grad_share_halves", n, out_shape, [dma((len(items),)), dma((len(items),))],
                           aliases={a: a for a in range(n)})(*fulls))


def _allgather_small(packed):
    def body(p_ref, o_ref, send_sems, recv_sems, local_sem):
        x, y, c, _ = _place()
        me = 4 * x + 2 * y + c
        own = pltpu.make_async_copy(p_ref, o_ref.at[me], local_sem)
        own.start()
        flips = [(fx, fy, fc) for fx in (0, 1) for fy in (0, 1) for fc in (0, 1)][1:]
        peers = [(x ^ fx, y ^ fy, c ^ fc) for fx, fy, fc in flips]
        copies = []
        for k, peer in enumerate(peers):
            cp = pltpu.make_async_remote_copy(src_ref=p_ref, dst_ref=o_ref.at[me], send_sem=send_sems.at[k],
                                              recv_sem=recv_sems.at[k], device_id=peer, device_id_type=MESH)
            cp.start()
            copies.append(cp)
        for k, (px, py, pc) in enumerate(peers):
            pltpu.make_async_remote_copy(src_ref=p_ref, dst_ref=o_ref.at[4 * px + 2 * py + pc], send_sem=send_sems.at[k],
                                         recv_sem=recv_sems.at[k], device_id=peers[k], device_id_type=MESH).wait_recv()
        for cp in copies:
            cp.wait_send()
        own.wait()

    dma = pltpu.SemaphoreType.DMA
    return _comm_call(body, "small_allgather", 1, [jax.ShapeDtypeStruct((8,) + packed.shape, F32)],
                      [dma((7,)), dma((7,)), dma(())])(packed)[0]


PAD_ROWS = 8


def kernel(x, norm_a, w_in_a, w_grp_a, scale_a, w_out_a, norm_kv, w_k, w_v, norm_b, w_in_b, w_out_b, norm_f, loss_target, m_norm_a, m_w_in_a, m_w_grp_a, m_scale_a, m_w_out_a, m_norm_kv, m_w_k, m_w_v, m_norm_b, m_w_in_b, m_w_out_b, m_norm_f, v_norm_a, v_w_in_a, v_w_grp_a, v_scale_a, v_w_out_a, v_norm_kv, v_w_k, v_w_v, v_norm_b, v_w_in_b, v_w_out_b, v_norm_f):
    weights = dict(norm_a=norm_a, w_in_a=w_in_a, w_grp_a=w_grp_a, scale_a=scale_a, w_out_a=w_out_a, norm_kv=norm_kv,
                   w_k=w_k, w_v=w_v, norm_b=norm_b, w_in_b=w_in_b, w_out_b=w_out_b, norm_f=norm_f)
    moments_m = dict(norm_a=m_norm_a, w_in_a=m_w_in_a, w_grp_a=m_w_grp_a, scale_a=m_scale_a, w_out_a=m_w_out_a,
                     norm_kv=m_norm_kv, w_k=m_w_k, w_v=m_w_v, norm_b=m_norm_b, w_in_b=m_w_in_b, w_out_b=m_w_out_b,
                     norm_f=m_norm_f)
    moments_v = dict(norm_a=v_norm_a, w_in_a=v_w_in_a, w_grp_a=v_w_grp_a, scale_a=v_scale_a, w_out_a=v_w_out_a,
                     norm_kv=v_norm_kv, w_k=v_w_k, w_v=v_w_v, norm_b=v_norm_b, w_in_b=v_w_in_b, w_out_b=v_w_out_b,
                     norm_f=v_norm_f)
    names = list(weights)
    d = x.shape[-1]
    c_idx = lax.axis_index("c").astype(jnp.int32).reshape(1)
    s_me = 2 * lax.axis_index("x") + lax.axis_index("y")
    s_idx = s_me.astype(jnp.int32).reshape(1)

    def as_lbrc(name):
        a = weights[name]
        if name == "w_grp_a":
            return a
        if a.ndim == 2:
            return a.reshape(1, 1, *a.shape)
        return a.reshape(a.shape[0], 1, *a.shape[1:])

    n_a, n_b = norm_a.shape[0], norm_b.shape[0]
    group_weights = {**{f"a{i}": [("w_in_a", i), ("w_grp_a", i), ("w_out_a", i)] for i in range(n_a)},
                     "kv": [("w_k", 0), ("w_v", 0)],
                     **{f"b{i}": [("w_in_b", i), ("w_out_b", i)] for i in range(n_b)}}
    group_order = [f"a{i}" for i in range(n_a)] + ["kv"] + [f"b{i}" for i in range(n_b)]
    slots, slot_groups = [], []
    for group in group_order:
        slot_groups.append(list(range(len(slots), len(slots) + len(group_weights[group]))))
        slots += [_cast_into_slot(f"cast_{name}{l}", as_lbrc(name), l, s_idx) for name, l in group_weights[group]]
    small_full, started = {}, {}

    def start_group(gi, carry=()):
        sems, bufs, carry = _gather_start(f"gather_start_{group_order[gi]}", [slots[t] for t in slot_groups[gi]], carry)
        started[gi] = (sems, bufs)
        return carry

    start_group(0)

    def gathered_form(name, g):
        if name in ("w_in_a", "w_in_b"):
            return g[0]
        if name == "w_grp_a":
            return g.reshape(g.shape[0], -1, g.shape[-1])
        return g.reshape(-1, g.shape[-1])

    def fetch(group, after):
        if group == "head":
            return {"norm_f": (0, norm_f)}
        gi = group_order.index(group)
        sems, bufs = started[gi]
        bufs = _gather_wait(f"gather_wait_{group}", sems, bufs, after)
        bufs, small_g = _gather_forward(f"gather_forward_{group}", bufs, [norm_a, scale_a] if gi == 0 else [])
        out = {name: (l, gathered_form(name, g)) for (name, l), g in zip(group_weights[group], bufs)}
        if gi == 0:
            for name, g in zip(("norm_a", "scale_a"), small_g):
                small_full[name] = g.transpose(1, 0, 2).reshape(g.shape[1], -1)
        layer = group_weights[group][0][1]
        if group.startswith("a"):
            gain_name, gain = "norm_a", small_full["norm_a"][layer]
            out.update(scale_a=(layer, small_full["scale_a"][layer]))
        elif group == "kv":
            gain_name, gain = "norm_kv", norm_kv
        else:
            gain_name, gain = "norm_b", norm_b[layer]
        gain = gain.reshape(1, -1)
        ahead = [gi + 1] + ([gi + 2] if gi + 2 < len(group_order) and group_order[gi + 1] == "kv" else [])
        for gj in ahead:
            if gj < len(group_order) and gj not in started:
                (gain,) = start_group(gj, (gain,))
        out[gain_name] = (layer, gain)
        return out

    in_flight = []

    def emit(group, grads_of, carry):
        keys = list(grads_of)
        recv1 = _exchange_halves(f"grad_exchange_{group}", [grads_of[k][1] for k in keys])
        parts = [_pair_add(f"pair_add_{k}{grads_of[k][0]}", grads_of[k][1], r, c_idx) for k, r in zip(keys, recv1)]
        sems, parts, lands, carry = _scatter_start(f"grad_scatter_start_{group}", parts, tuple(carry))
        in_flight.append((group, [(k, grads_of[k][0]) for k in keys], sems, parts, lands))
        return carry

    loss_vec, grad_x, small = _local_step(x[0], loss_target[0], n_a, n_b, fetch, emit)

    small_order = [("norm_a", i) for i in range(n_a)] + [("scale_a", i) for i in range(n_a)] + [("norm_kv", 0)] + \
                  [("norm_b", i) for i in range(norm_b.shape[0])] + [("norm_f", 0)]
    pad = lambda vec: jnp.pad(vec, ((0, PAD_ROWS - 1), (0, 0)))
    packed = jnp.concatenate([pad(loss_vec)] + [pad(small[n][i]) for n, i in small_order], axis=0)
    totals = _sum_devices("small_sum", _allgather_small(packed))
    loss = 0.5 * jnp.sum(totals[0]) / d
    small_tot = {}
    for j, (n, i) in enumerate(small_order):
        small_tot.setdefault(n, []).append(totals[PAD_ROWS * (j + 1)])
    grads = {}
    shard_w = norm_a.shape[1]
    for n in ("norm_a", "scale_a"):
        full = jnp.stack(small_tot[n])
        grads[n] = lax.dynamic_slice_in_dim(full, s_me * shard_w, shard_w, axis=1)
    grads["norm_kv"] = small_tot["norm_kv"][0]
    grads["norm_b"] = jnp.stack(small_tot["norm_b"])
    grads["norm_f"] = small_tot["norm_f"][0]

    sc_idx = jnp.concatenate([s_idx, c_idx])
    fulls = {name: None for name in BIG_WEIGHTS}
    for group, keys, sems, parts, lands in in_flight:
        parts, lands = _scatter_wait(f"grad_scatter_wait_{group}", sems, parts, lands)
        for (name, i), p, r in zip(keys, parts, lands):
            n_layers = 1 if weights[name].ndim == 2 else weights[name].shape[0]
            fulls[name] = _final_add(f"final_add_{name}{i}", p, r, sc_idx, i, n_layers, into=fulls[name])
    shared = _share_halves([fulls[name] for name in BIG_WEIGHTS])
    for name, g in zip(BIG_WEIGHTS, shared):
        grads[name] = g.reshape(weights[name].shape)

    deltas, new_m, new_v = {}, {}, {}
    for n in names:
        shape = weights[n].shape
        as2d = (lambda a: a.reshape(1, -1)) if len(shape) == 1 else (lambda a: a)
        dl, mn, vn = _adamw(f"adamw_{n}", as2d(weights[n]), as2d(grads[n]), as2d(moments_m[n]), as2d(moments_v[n]))
        deltas[n], new_m[n], new_v[n] = dl.reshape(shape), mn.reshape(shape), vn.reshape(shape)

    return (loss, grad_x[None], *[grads[n] for n in names], *[deltas[n] for n in names],
            *[new_m[n] for n in names], *[new_v[n] for n in names])
```

```python
import functools
import math

import jax
import jax.numpy as jnp
from jax import lax
from jax.experimental import pallas as pl
from jax.experimental.pallas import tpu as pltpu

F32 = jnp.float32
BF16 = jnp.bfloat16

HEAD_DIM = 128
POOL_WINDOWS = (2, 4, 8, 16)
DILATED_PAIRS = ((128, 1), (512, 4), (2048, 16))
ROPE_THETA = 10000.0
RMS_EPS = 1e-6
NEG_INF = -1e30
N_CHIPS = 4

ADAM_LR = 0.001
ADAM_B1 = 0.9
ADAM_B2 = 0.999
ADAM_EPS = 1e-08
ADAM_WD = 0.01
ADAM_STEP = 10

VMEM_LIMIT_BYTES = 56 * 1024 * 1024
MESH = pl.DeviceIdType.MESH
ANY = pl.BlockSpec(memory_space=pl.ANY)


def _tile(n, pref):
    t = min(n, pref)
    assert n % t == 0, (n, pref)
    return t


def _params(sem=None):
    return pltpu.CompilerParams(dimension_semantics=sem, vmem_limit_bytes=VMEM_LIMIT_BYTES)


def _mm(name, a, b, *, grid2, nk, a_blk, a_map, b_blk, b_map, outs, dims, epi=None, epi_in=(), epi_specs=(),
        acc_shape=None, epi_scratch=(), into=None):
    n_epi, n_out = len(epi_in), len(outs)

    def body(*refs):
        a_ref, b_ref = refs[0], refs[1]
        e_refs = refs[2:2 + n_epi]
        first_out = 2 + n_epi + (0 if into is None else 1)
        o_refs = refs[first_out:first_out + n_out]
        s_refs = refs[first_out + n_out + (0 if nk == 1 else 1):]

        def contrib():
            a_val = a_ref[...]
            if a_val.ndim == 3:
                a_val = a_val.reshape(-1, a_val.shape[-1])
            return lax.dot_general(a_val, b_ref[...], (dims, ((), ())), preferred_element_type=F32)

        def finish(acc):
            if epi is None:
                o_refs[0][...] = acc.reshape(o_refs[0].shape).astype(o_refs[0].dtype)
            else:
                epi(acc, e_refs, o_refs, s_refs)

        if nk == 1:
            finish(contrib())
        else:
            acc_ref = refs[first_out + n_out]
            k = pl.program_id(2)

            @pl.when(k == 0)
            def _():
                acc_ref[...] = contrib()

            @pl.when(k > 0)
            def _():
                acc_ref[...] += contrib()

            @pl.when(k == nk - 1)
            def _():
                finish(acc_ref[...])

    scratch = ([] if nk == 1 else [pltpu.VMEM(acc_shape, F32)]) + list(epi_scratch)
    extra_in, extra_specs, aliases = (), (), {}
    if into is not None:
        extra_in, extra_specs, aliases = (into[0],), (ANY,), {2 + n_epi: into[1]}
    res = pl.pallas_call(
        body, name=name, grid=(grid2[0], grid2[1], nk),
        in_specs=[pl.BlockSpec(a_blk, a_map), pl.BlockSpec(b_blk, b_map), *epi_specs, *extra_specs],
        out_specs=[pl.BlockSpec(blk, imap) for _, blk, imap, _ in outs],
        out_shape=[jax.ShapeDtypeStruct(shape, dtype) for shape, _, _, dtype in outs],
        scratch_shapes=scratch, input_output_aliases=aliases,
        compiler_params=_params(("parallel", "parallel", "arbitrary")),
    )(a, b, *epi_in, *extra_in)
    return res[0] if n_out == 1 else tuple(res)


NN = ((1,), (0,))
NT = ((1,), (1,))
TN = ((0,), (0,))


def _rope_apply(t, cos, sin):
    return t * cos + pltpu.roll(t, HEAD_DIM // 2, 1) * sin


def _epi_add(acc, e_refs, o_refs, s_refs):
    o_refs[0][...] = (acc + e_refs[0][...]).astype(o_refs[0].dtype)


def _col_blocks(width):
    return [slice(c * HEAD_DIM, (c + 1) * HEAD_DIM) for c in range(width // HEAD_DIM)]


def _col_scratch(rows, width):
    return pltpu.VMEM((width // HEAD_DIM, rows, HEAD_DIM), F32)


def _to_residue_major(o_ref, scr, d, sl):
    if d == 1:
        o_ref[0, :, sl] = scr[...].astype(o_ref.dtype)
        return
    rows = scr.shape[0] // d
    for r in range(d):
        o_ref[r, :, sl] = scr[pl.ds(r, rows, stride=d), :].astype(o_ref.dtype)


def _from_residue_major(i_ref, scr, d, sl):
    if d == 1:
        return i_ref[0, :, sl].astype(F32)
    rows = i_ref.shape[1]
    for r in range(d):
        scr[pl.ds(r, rows, stride=d), :] = i_ref[r, :, sl].astype(F32)
    return scr[...]


def _make_epi_orders(dils, rope_scale):
    def epi(acc, e_refs, o_refs, s_refs):
        if rope_scale is not None:
            cos = e_refs[0][...]
            sin = e_refs[1][...]
        for c, sl in enumerate(_col_blocks(acc.shape[1])):
            scr = s_refs[0].at[c]
            scr[...] = acc[:, sl] if rope_scale is None else _rope_apply(acc[:, sl], cos, sin) * rope_scale
            for o_ref, d in zip(o_refs, dils):
                _to_residue_major(o_ref, scr, d, sl)
    return epi


def _make_epi_token_order(d, has_add):
    def epi(acc, e_refs, o_refs, s_refs):
        o_ref = o_refs[0]
        if d == 1:
            o_ref[...] = acc + e_refs[0][...] if has_add else acc
            return
        rows = acc.shape[0] // d
        for c, sl in enumerate(_col_blocks(acc.shape[1])):
            scr = s_refs[0].at[c]
            for r in range(d):
                scr[pl.ds(r, rows, stride=d), :] = acc[r * rows:(r + 1) * rows, sl]
            o_ref[:, sl] = scr[...] + e_refs[0][:, sl] if has_add else scr[...]
    return epi


def _mm_act_w(name, a, w, *, out_dtype=BF16, add=None, rope=None, n_first=0, n_cols=None, dils=None):
    s_len, k_len = a.shape
    bm = _tile(s_len, 1024)
    epi, epi_in, epi_specs, epi_scratch = None, (), (), ()
    if w.ndim == 3:
        ns, _, c = w.shape
        ns_used = ns if n_cols is None else n_cols
        bn = _tile(c, 1024)
        sub = c // bn
        grid2 = (ns_used * sub, s_len // bm)
        b_blk, b_map = (None, k_len, bn), (lambda j, i, k: (j // sub + n_first, 0, j % sub))
        n_len = ns_used * c
    else:
        n_len = w.shape[1]
        bn = _tile(n_len, 1024)
        grid2 = (n_len // bn, s_len // bm)
        b_blk, b_map = (k_len, bn), (lambda j, i, k: (0, j))
    if add is not None:
        epi, epi_in = _epi_add, (add,)
        epi_specs = (pl.BlockSpec((bm, bn), lambda j, i, k: (i, j)),)
    outs = [((s_len, n_len), (bm, bn), lambda j, i, k: (i, j), out_dtype)]
    if dils is not None:
        if rope is not None:
            epi_in = rope[:2]
            epi_specs = (pl.BlockSpec((bm, HEAD_DIM), lambda j, i, k: (i, 0)),) * 2
        epi = _make_epi_orders(dils, None if rope is None else rope[2])
        epi_scratch = (_col_scratch(bm, bn),)
        outs = [((d, s_len // d, n_len), (d, bm // d, bn), lambda j, i, k: (0, i, j), BF16) for d in dils]
    res = _mm(name, a, w, grid2=grid2, nk=1, a_blk=(bm, k_len), a_map=lambda j, i, k: (i, 0),
              b_blk=b_blk, b_map=b_map, outs=outs, dims=NN, epi=epi, epi_in=epi_in, epi_specs=epi_specs,
              epi_scratch=epi_scratch)
    return (res,) if dils is not None and len(dils) == 1 else res


def _mm_grad_act(name, dy, w, *, add=None, slot=None):
    if slot is not None:
        d = 1 if dy.ndim == 2 else dy.shape[0]
        s_len = dy.shape[-2] * d
        _, k_len, c = w.shape
        bm, bn = _tile(s_len, 1024), _tile(k_len, 1024)
        a_blk, a_map = ((bm, c), lambda j, i, k: (i, 0)) if dy.ndim == 2 else ((d, bm // d, c), lambda j, i, k: (0, i, 0))
        epi_in = () if add is None else (add,)
        return _mm(name, dy, w, grid2=(k_len // bn, s_len // bm), nk=1, a_blk=a_blk, a_map=a_map,
                   b_blk=(None, bn, c), b_map=lambda j, i, k: (slot, j, 0),
                   outs=[((s_len, k_len), (bm, bn), lambda j, i, k: (i, j), F32)], dims=NT,
                   epi=_make_epi_token_order(d, add is not None), epi_in=epi_in,
                   epi_specs=(pl.BlockSpec((bm, bn), lambda j, i, k: (i, j)),) * len(epi_in),
                   epi_scratch=(_col_scratch(bm, bn),) if d > 1 else ())
    s_len, n_len = dy.shape
    bm = _tile(s_len, 1024)
    if w.ndim == 3:
        ns, k_len, c = w.shape
        bk, nk = c, ns
        bn = _tile(k_len, 1024)
        b_blk, b_map = (None, bn, c), (lambda j, i, k: (k, j, 0))
    else:
        k_len = w.shape[0]
        bk = _tile(n_len, 1024)
        nk = n_len // bk
        bn = _tile(k_len, 1024)
        b_blk, b_map = (bn, bk), (lambda j, i, k: (j, k))
    epi, epi_in, epi_specs = None, (), ()
    if add is not None:
        epi, epi_in = _epi_add, (add,)
        epi_specs = (pl.BlockSpec((bm, bn), lambda j, i, k: (i, j)),)
    return _mm(name, dy, w, grid2=(k_len // bn, s_len // bm), nk=nk, a_blk=(bm, bk), a_map=lambda j, i, k: (i, k),
               b_blk=b_blk, b_map=b_map, outs=[((s_len, k_len), (bm, bn), lambda j, i, k: (i, j), F32)],
               dims=NT, epi=epi, epi_in=epi_in, epi_specs=epi_specs, acc_shape=(bm, bn))


def _mm_grad_w(name, a, dy, *, col_shards=None, slot=None, into=None):
    s_len, k_len = a.shape
    n_len = dy.shape[1]
    bk = _tile(s_len, 1024)
    bm = _tile(k_len, 1024)
    if slot is not None:
        bn = _tile(n_len, 1024)
        out = ((col_shards, k_len, n_len), (None, bm, bn), lambda j, i, k: (slot, i, j), BF16)
    elif col_shards:
        c = n_len // col_shards
        bn = _tile(c, 1024)
        sub = c // bn
        out = ((col_shards, k_len, c), (None, bm, bn), lambda j, i, k: (j // sub, i, j % sub), BF16)
    else:
        bn = _tile(n_len, 1024)
        out = ((k_len, n_len), (bm, bn), lambda j, i, k: (i, j), BF16)
    return _mm(name, a, dy, grid2=(n_len // bn, k_len // bm), nk=s_len // bk,
               a_blk=(bk, bm), a_map=lambda j, i, k: (k, i), b_blk=(bk, bn), b_map=lambda j, i, k: (k, j),
               outs=[out], dims=TN, acc_shape=(bm, bn), into=None if into is None else (into, 0))


def _mm_grp_fwd(name, pooled, wg):
    s_len, e = pooled.shape
    ng, g, _ = wg.shape
    bm = _tile(s_len, 1024)
    return _mm(name, pooled, wg, grid2=(ng, s_len // bm), nk=1, a_blk=(bm, g), a_map=lambda j, i, k: (i, j),
               b_blk=(None, g, g), b_map=lambda j, i, k: (j, 0, 0),
               outs=[((s_len, e), (bm, g), lambda j, i, k: (i, j), F32)], dims=NN)


def _mm_grp_grad_act(name, dy, wg):
    s_len, e = dy.shape
    ng, g, _ = wg.shape
    bm = _tile(s_len, 1024)
    return _mm(name, dy, wg, grid2=(ng, s_len // bm), nk=1, a_blk=(bm, g), a_map=lambda j, i, k: (i, j),
               b_blk=(None, g, g), b_map=lambda j, i, k: (j, 0, 0),
               outs=[((s_len, e), (bm, g), lambda j, i, k: (i, j), F32)], dims=NT)


def _mm_grp_grad_w(name, pooled, dy, ng):
    s_len, e = pooled.shape
    g = e // ng
    bk = _tile(s_len, 1024)
    return _mm(name, pooled, dy, grid2=(ng, 1), nk=s_len // bk, a_blk=(bk, g), a_map=lambda j, i, k: (k, j),
               b_blk=(bk, g), b_map=lambda j, i, k: (k, j),
               outs=[((N_CHIPS, ng, g // N_CHIPS, g), (N_CHIPS, None, g // N_CHIPS, g), lambda j, i, k: (0, j, 0, 0), BF16)],
               dims=TN, acc_shape=(g, g))


def _row_spec(bs, width, col=0):
    return pl.BlockSpec((bs, width), lambda i: (i, col))


def _vec_spec(width):
    return pl.BlockSpec((1, width), lambda i: (0, 0))


def _rows_call(body, name, s_len, in_specs, out_specs, out_shape, bs, aliases=None, sequential=False):
    return pl.pallas_call(
        body, name=name, grid=(s_len // bs,), in_specs=in_specs, out_specs=out_specs, out_shape=out_shape,
        input_output_aliases=aliases or {},
        compiler_params=_params(("arbitrary",) if sequential else ("parallel",)))


def _accumulate(ref, part):
    i = pl.program_id(0)

    @pl.when(i == 0)
    def _():
        ref[...] = part

    @pl.when(i > 0)
    def _():
        ref[...] += part


def _rms_scale(xf):
    return lax.rsqrt(jnp.mean(xf * xf, axis=-1, keepdims=True) + RMS_EPS)


def _res_spec(dil, bs, width):
    return pl.BlockSpec((dil, bs // dil, width), lambda i: (0, i, 0))


def _res_shape(dil, s_len, width, dtype):
    return jax.ShapeDtypeStruct((dil, s_len // dil, width), dtype)


def _rmsnorm_fwd(name, x, gain, dils=()):
    s_len, d = x.shape
    bs = _tile(s_len, 256)

    def body(x_ref, g_ref, h_ref, *rest):
        xf = x_ref[...]
        h = (xf * _rms_scale(xf)) * g_ref[...]
        h_ref[...] = h.astype(BF16)
        if dils:
            for c, sl in enumerate(_col_blocks(d)):
                scr = rest[-1].at[c]
                scr[...] = h[:, sl]
                for o_ref, dil in zip(rest[:-1], dils):
                    _to_residue_major(o_ref, scr, dil, sl)

    res = pl.pallas_call(
        body, name=name, grid=(s_len // bs,), in_specs=[_row_spec(bs, d), _vec_spec(d)],
        out_specs=[_row_spec(bs, d)] + [_res_spec(dil, bs, d) for dil in dils],
        out_shape=[jax.ShapeDtypeStruct((s_len, d), BF16)] + [_res_shape(dil, s_len, d, BF16) for dil in dils],
        scratch_shapes=[_col_scratch(bs, d)] if dils else [],
        compiler_params=_params(("parallel",)))(x, gain)
    return res[0] if not dils else tuple(res)


def _rmsnorm_bwd(name, x, gain, dh, dres):
    s_len, d = x.shape
    bs = _tile(s_len, 256)

    def body(x_ref, g_ref, dh_ref, dres_ref, dx_ref, dxb_ref, dg_ref):
        xf = x_ref[...]
        r = _rms_scale(xf)
        xh = xf * r
        dh_f = dh_ref[...]
        t = dh_f * g_ref[...]
        dx = dres_ref[...] + r * (t - xh * jnp.mean(t * xh, axis=-1, keepdims=True))
        dx_ref[...] = dx
        dxb_ref[...] = dx.astype(BF16)
        _accumulate(dg_ref, jnp.sum(dh_f * xh, axis=0, keepdims=True))

    return _rows_call(
        body, name, s_len,
        [_row_spec(bs, d), _vec_spec(d), _row_spec(bs, d), _row_spec(bs, d)],
        [_row_spec(bs, d), _row_spec(bs, d), _vec_spec(d)],
        [jax.ShapeDtypeStruct((s_len, d), F32), jax.ShapeDtypeStruct((s_len, d), BF16),
         jax.ShapeDtypeStruct((1, d), F32)], bs, sequential=True)(x, gain, dh, dres)


def _loss_head(name, x, gain, target):
    s_len, d = x.shape
    bs = _tile(s_len, 256)

    def body(x_ref, g_ref, t_ref, lv_ref, dx_ref, dxb_ref, dg_ref):
        xf = x_ref[...]
        r = _rms_scale(xf)
        xh = xf * r
        err = xh * g_ref[...] - t_ref[...]
        dy = err * (1.0 / d)
        t = dy * g_ref[...]
        dx = r * (t - xh * jnp.mean(t * xh, axis=-1, keepdims=True))
        dx_ref[...] = dx
        dxb_ref[...] = dx.astype(BF16)
        _accumulate(lv_ref, jnp.sum(err * err, axis=0, keepdims=True))
        _accumulate(dg_ref, jnp.sum(dy * xh, axis=0, keepdims=True))

    return _rows_call(
        body, name, s_len, [_row_spec(bs, d), _vec_spec(d), _row_spec(bs, d)],
        [_vec_spec(d), _row_spec(bs, d), _row_spec(bs, d), _vec_spec(d)],
        [jax.ShapeDtypeStruct((1, d), F32), jax.ShapeDtypeStruct((s_len, d), F32),
         jax.ShapeDtypeStruct((s_len, d), BF16), jax.ShapeDtypeStruct((1, d), F32)],
        bs, sequential=True)(x, gain, target)


def _sigmoid(g):
    return 1.0 / (1.0 + jnp.exp(-g))


def _gate_a_fwd(name, ypre, proj, scale):
    s_len, e = ypre.shape
    bs = _tile(s_len, 256)

    def body(y_ref, g_ref, sc_ref, z_ref):
        g = g_ref[...]
        z_ref[...] = (y_ref[...] * sc_ref[...] * (g * _sigmoid(g))).astype(BF16)

    return _rows_call(body, name, s_len, [_row_spec(bs, e), _row_spec(bs, e, 1), _vec_spec(e)], _row_spec(bs, e),
                      jax.ShapeDtypeStruct((s_len, e), BF16), bs)(ypre, proj, scale)


def _gate_a_bwd(name, dz, ypre, proj, scale):
    s_len, e = ypre.shape
    bs = _tile(s_len, 256)

    def body(dz_ref, y_ref, g_ref, sc_ref, dy_ref, dproj_ref, dsc_ref):
        g = g_ref[...]
        sg = _sigmoid(g)
        silu = g * sg
        dz_f = dz_ref[...]
        ypre_f = y_ref[...]
        dys = dz_f * silu
        dy_ref[...] = (dys * sc_ref[...]).astype(BF16)
        dproj_ref[...] = (dz_f * (ypre_f * sc_ref[...]) * (sg * (1.0 + g * (1.0 - sg)))).astype(BF16)
        _accumulate(dsc_ref, jnp.sum(dys * ypre_f, axis=0, keepdims=True))

    return _rows_call(
        body, name, s_len, [_row_spec(bs, e), _row_spec(bs, e), _row_spec(bs, e, 1), _vec_spec(e)],
        [_row_spec(bs, e), _row_spec(bs, e, 1), _vec_spec(e)],
        [jax.ShapeDtypeStruct((s_len, e), BF16), jax.ShapeDtypeStruct((s_len, 2 * e), BF16),
         jax.ShapeDtypeStruct((1, e), F32)], bs, sequential=True)(dz, ypre, proj, scale)


def _merge_gate_fwd(name, outs, lses, gate, dils):
    s_len, e = gate.shape
    bs = _tile(s_len, 256)
    n = len(outs)

    def body(*refs):
        o_refs, l_refs, g_ref = refs[:n], refs[n:2 * n], refs[2 * n]
        m_ref, lj_ref, z_ref = refs[2 * n + 1:2 * n + 4]
        scratch = refs[2 * n + 4]
        for c, sl in enumerate(_col_blocks(e)):
            ls = [_from_residue_major(r, scratch.at[2 * j, c], dil, sl) for j, (r, dil) in enumerate(zip(l_refs, dils))]
            os_ = [_from_residue_major(r, scratch.at[2 * j + 1, c], dil, sl) for j, (r, dil) in enumerate(zip(o_refs, dils))]
            mx = functools.reduce(jnp.maximum, ls)
            ws = [jnp.exp(l - mx) for l in ls]
            den = functools.reduce(lambda a, b: a + b, ws)
            merged = functools.reduce(lambda a, b: a + b, [w * o for w, o in zip(ws, os_)]) / den
            g = g_ref[:, sl]
            m_ref[:, sl] = merged.astype(BF16)
            lj_ref[:, sl] = mx + jnp.log(den)
            z_ref[:, sl] = (merged * (g * _sigmoid(g))).astype(BF16)

    spec = _row_spec(bs, e)
    res_specs = [_res_spec(dil, bs, e) for dil in dils]
    return pl.pallas_call(
        body, name=name, grid=(s_len // bs,), in_specs=res_specs + res_specs + [spec], out_specs=[spec] * 3,
        out_shape=[jax.ShapeDtypeStruct((s_len, e), BF16), jax.ShapeDtypeStruct((s_len, e), F32),
                   jax.ShapeDtypeStruct((s_len, e), BF16)],
        scratch_shapes=[pltpu.VMEM((2 * n, e // HEAD_DIM, bs, HEAD_DIM), F32)],
        compiler_params=_params(("parallel",)))(*outs, *lses, gate)


def _gate_b_bwd(name, dz, merged, gate, lse, dils):
    s_len, e = gate.shape
    bs = _tile(s_len, 256)
    n = len(dils)

    def body(dz_ref, m_ref, g_ref, l_ref, dg_ref, *rest):
        out_refs, scratch = rest[:3 * n], rest[3 * n]
        for c, sl in enumerate(_col_blocks(e)):
            g = g_ref[:, sl]
            sg = _sigmoid(g)
            dz_f = dz_ref[:, sl]
            merged = m_ref[:, sl].astype(F32)
            dmerged = dz_f * (g * sg)
            dg_ref[:, sl] = (dz_f * merged * (sg * (1.0 + g * (1.0 - sg)))).astype(BF16)
            values = (dmerged, l_ref[:, sl],
                      jnp.broadcast_to(jnp.sum(dmerged * merged, axis=-1, keepdims=True), (bs, HEAD_DIM)))
            for t, val in enumerate(values):
                scr = scratch.at[t, c]
                scr[...] = val
                for j, dil in enumerate(dils):
                    _to_residue_major(out_refs[3 * j + t], scr, dil, sl)

    spec = _row_spec(bs, e)
    out_specs, out_shape = [spec], [jax.ShapeDtypeStruct((s_len, e), BF16)]
    for dil in dils:
        out_specs += [_res_spec(dil, bs, e)] * 3
        out_shape += [_res_shape(dil, s_len, e, BF16), _res_shape(dil, s_len, e, F32), _res_shape(dil, s_len, e, F32)]
    res = pl.pallas_call(
        body, name=name, grid=(s_len // bs,), in_specs=[spec] * 4, out_specs=out_specs, out_shape=out_shape,
        scratch_shapes=[pltpu.VMEM((3, e // HEAD_DIM, bs, HEAD_DIM), F32)],
        compiler_params=_params(("parallel",)))(dz, merged, gate, lse)
    return res[0], [tuple(res[1 + 3 * j:4 + 3 * j]) for j in range(n)]


def _kv_grad_prep(name, dk_accs, dv_accs, dils, cos, sin_inv):
    n = len(dils)
    e = dk_accs[0].shape[-1]
    s_len = dk_accs[0].shape[0] * dk_accs[0].shape[1]
    bs = _tile(s_len, 256)

    def body(*refs):
        dk_refs, dv_refs = refs[:n], refs[n:2 * n]
        c_ref, s_ref, dkb_ref, dvb_ref, scratch = refs[2 * n:]
        cos_t, sin_t = c_ref[...], s_ref[...]
        add = lambda a, b: a + b
        for c, sl in enumerate(_col_blocks(e)):
            dk = functools.reduce(add, [_from_residue_major(r, scratch.at[j, c], dil, sl)
                                        for j, (r, dil) in enumerate(zip(dk_refs, dils))])
            dkb_ref[:, sl] = _rope_apply(dk, cos_t, sin_t).astype(BF16)
            dv = functools.reduce(add, [_from_residue_major(r, scratch.at[n + j, c], dil, sl)
                                        for j, (r, dil) in enumerate(zip(dv_refs, dils))])
            dvb_ref[:, sl] = dv.astype(BF16)

    spec, rspec = _row_spec(bs, e), _row_spec(bs, HEAD_DIM)
    res_specs = [_res_spec(dil, bs, e) for dil in dils]
    return pl.pallas_call(
        body, name=name, grid=(s_len // bs,), in_specs=res_specs + res_specs + [rspec, rspec], out_specs=[spec, spec],
        out_shape=[jax.ShapeDtypeStruct((s_len, e), BF16)] * 2,
        scratch_shapes=[pltpu.VMEM((2 * n, e // HEAD_DIM, bs, HEAD_DIM), F32)],
        compiler_params=_params(("parallel",)))(*dk_accs, *dv_accs, cos, sin_inv)


def _pool_cols(e):
    return _tile(e // len(POOL_WINDOWS), 256)


def _window_sum(val, grp, s_len, forward):
    rows = lax.broadcasted_iota(jnp.int32, val.shape, 0)
    acc = val
    for level in range(len(POOL_WINDOWS)):
        step = 1 << level
        if forward:
            shifted = jnp.where(rows >= step, pltpu.roll(acc, step, 0), 0.0)
        else:
            shifted = jnp.where(rows < s_len - step, pltpu.roll(acc, s_len - step, 0), 0.0)
        acc = jnp.where(level <= grp, acc + shifted, acc)
    return acc


def _window_count(shape, grp):
    rows = lax.broadcasted_iota(jnp.int32, shape, 0)
    return jnp.minimum(rows + 1, jnp.left_shift(2, grp)).astype(F32)


def _pool_fwd(name, proj):
    s_len, e2 = proj.shape
    e = e2 // 2
    cb = _pool_cols(e)
    per_grp = e // len(POOL_WINDOWS) // cb
    assert POOL_WINDOWS == tuple(2 << g for g in range(len(POOL_WINDOWS)))

    def body(u_ref, p_ref):
        grp = pl.program_id(0)
        u = u_ref[...]
        total = _window_sum(u, grp, s_len, True)
        p_ref[...] = (total / _window_count(u.shape, grp) - u).astype(BF16)

    spec = pl.BlockSpec((s_len, cb), lambda g, c: (0, g * per_grp + c))
    return pl.pallas_call(
        body, name=name, grid=(len(POOL_WINDOWS), per_grp), in_specs=[spec], out_specs=spec,
        out_shape=jax.ShapeDtypeStruct((s_len, e), BF16), compiler_params=_params(("parallel", "parallel")))(proj)


def _pool_bwd(name, dpooled, dproj):
    s_len, e = dpooled.shape
    cb = _pool_cols(e)
    per_grp = e // len(POOL_WINDOWS) // cb

    def body(dp_ref, _, du_ref):
        grp = pl.program_id(0)
        dp = dp_ref[...]
        total = _window_sum(dp / _window_count(dp.shape, grp), grp, s_len, False)
        du_ref[...] = (total - dp).astype(BF16)

    spec = pl.BlockSpec((s_len, cb), lambda g, c: (0, g * per_grp + c))
    return pl.pallas_call(
        body, name=name, grid=(len(POOL_WINDOWS), per_grp), in_specs=[spec, ANY], out_specs=spec,
        out_shape=jax.ShapeDtypeStruct(dproj.shape, BF16), input_output_aliases={1: 0},
        compiler_params=_params(("parallel", "parallel")))(dpooled, dproj)


def _band_masks(nb, first):
    row = lax.broadcasted_iota(jnp.int32, (nb, nb), 0)
    col = lax.broadcasted_iota(jnp.int32, (nb, nb), 1)
    return col >= row + jnp.where(first, 2 * nb, 0), col <= row


def _dot(a, b, dims):
    return lax.dot_general(a, b, (dims, ((), ())), preferred_element_type=F32)


def _attn_fwd(name, window, q, k, v):
    dil, m, e = k.shape
    nb = window // dil
    nblk = m // nb
    heads = e // HEAD_DIM

    def body(q_ref, kc_ref, vc_ref, o_ref, l_ref, kp_ref, vp_ref):
        first = pl.program_id(1) == 0

        @pl.when(first)
        def _():
            kp_ref[...] = jnp.zeros_like(kp_ref)
            vp_ref[...] = jnp.zeros_like(vp_ref)

        mask_p, mask_c = _band_masks(nb, first)
        cols = _col_blocks(e)
        s_p = [jnp.where(mask_p, _dot(q_ref[:, sl], kp_ref[:, sl], NT), NEG_INF) for sl in cols]
        s_c = [jnp.where(mask_c, _dot(q_ref[:, sl], kc_ref[:, sl], NT), NEG_INF) for sl in cols]
        mx = [jnp.maximum(jnp.max(a, axis=-1, keepdims=True), jnp.max(b, axis=-1, keepdims=True))
              for a, b in zip(s_p, s_c)]
        p_p = [jnp.exp(a - m) for a, m in zip(s_p, mx)]
        p_c = [jnp.exp(a - m) for a, m in zip(s_c, mx)]
        den = [jnp.sum(a, axis=-1, keepdims=True) + jnp.sum(b, axis=-1, keepdims=True) for a, b in zip(p_p, p_c)]
        for h, sl in enumerate(cols):
            out = _dot(p_p[h].astype(BF16), vp_ref[:, sl], NN) + _dot(p_c[h].astype(BF16), vc_ref[:, sl], NN)
            o_ref[:, sl] = (out / den[h]).astype(BF16)
            l_ref[:, sl] = jnp.broadcast_to(mx[h] + jnp.log(den[h]), (nb, HEAD_DIM))
        kp_ref[...] = kc_ref[...]
        vp_ref[...] = vc_ref[...]

    blk = (None, nb, e)
    cur = lambda r, n: (r, n, 0)
    return pl.pallas_call(
        body, name=name, grid=(dil, nblk),
        in_specs=[pl.BlockSpec(blk, cur)] * 3,
        out_specs=[pl.BlockSpec(blk, cur), pl.BlockSpec(blk, cur)],
        out_shape=[jax.ShapeDtypeStruct((dil, m, e), BF16), jax.ShapeDtypeStruct((dil, m, e), F32)],
        scratch_shapes=[pltpu.VMEM((nb, e), BF16), pltpu.VMEM((nb, e), BF16)],
        compiler_params=_params(("parallel", "arbitrary")),
    )(q, k, v)


def _attn_bwd(name, window, scale, q, k, v, dout, lse, delta, cos, sin_inv, dk_acc, dv_acc):
    dil, m, e = k.shape
    nb = window // dil
    nblk = m // nb
    heads = e // HEAD_DIM

    accumulate = dk_acc is not None

    def body(k_ref, v_ref, q0_ref, qn_ref, do0_ref, don_ref, l0_ref, ln_ref, dl0_ref, dln_ref, c_ref, s_ref, *rest):
        if accumulate:
            dki_ref, dvi_ref = rest[:2]
            rest = rest[2:]
        dq_ref, dko_ref, dvo_ref, carry_ref, qc_ref, doc_ref, lc_ref, dlc_ref = rest
        n = pl.program_id(1)

        @pl.when(n == 0)
        def _():
            carry_ref[...] = jnp.zeros_like(carry_ref)
            qc_ref[...] = q0_ref[...]
            doc_ref[...] = do0_ref[...]
            lc_ref[...] = l0_ref[...]
            dlc_ref[...] = dl0_ref[...]

        mask_n, mask_c = _band_masks(nb, n == nblk - 1)
        cos_t, sin_t = c_ref[...], s_ref[...]
        cols = _col_blocks(e)
        stat = lambda ref, sl: ref[:, sl] if nb == HEAD_DIM else ref[:, sl][:, :1]
        s_c = [_dot(qc_ref[:, sl], k_ref[:, sl], NT) for sl in cols]
        s_n = [_dot(qn_ref[:, sl], k_ref[:, sl], NT) for sl in cols]
        dp_c = [_dot(doc_ref[:, sl], v_ref[:, sl], NT) for sl in cols]
        dp_n = [_dot(don_ref[:, sl], v_ref[:, sl], NT) for sl in cols]
        p_c = [jnp.where(mask_c, jnp.exp(s - stat(lc_ref, sl)), 0.0) for s, sl in zip(s_c, cols)]
        p_n = [jnp.where(mask_n, jnp.exp(s - stat(ln_ref, sl)), 0.0) for s, sl in zip(s_n, cols)]
        ds_c = [(p * (dp - stat(dlc_ref, sl))).astype(BF16) for p, dp, sl in zip(p_c, dp_c, cols)]
        ds_n = [(p * (dp - stat(dln_ref, sl))).astype(BF16) for p, dp, sl in zip(p_n, dp_n, cols)]
        for h, sl in enumerate(cols):
            dq = (carry_ref[:, sl] + _dot(ds_c[h], k_ref[:, sl], NN)) * scale
            dq_ref[:, sl] = _rope_apply(dq, cos_t, sin_t).astype(BF16)
        for h, sl in enumerate(cols):
            carry_ref[:, sl] = _dot(ds_n[h], k_ref[:, sl], NN)
        for h, sl in enumerate(cols):
            dk = _dot(ds_c[h], qc_ref[:, sl], TN) + _dot(ds_n[h], qn_ref[:, sl], TN)
            dv = _dot(p_c[h].astype(BF16), doc_ref[:, sl], TN) + _dot(p_n[h].astype(BF16), don_ref[:, sl], TN)
            dko_ref[:, sl] = dki_ref[:, sl] + dk if accumulate else dk
            dvo_ref[:, sl] = dvi_ref[:, sl] + dv if accumulate else dv
        qc_ref[...] = qn_ref[...]
        doc_ref[...] = don_ref[...]
        lc_ref[...] = ln_ref[...]
        dlc_ref[...] = dln_ref[...]

    blk = (None, nb, e)
    cur = lambda r, n: (r, n, 0)
    nxt = lambda r, n: (r, jnp.minimum(n + 1, nblk - 1), 0)
    first = lambda r, n: (r, 0, 0)
    rblk = (None, nb, HEAD_DIM)
    both = lambda shape: [pl.BlockSpec(shape, first), pl.BlockSpec(shape, nxt)]
    accs = (dk_acc, dv_acc) if accumulate else ()
    return pl.pallas_call(
        body, name=name, grid=(dil, nblk),
        in_specs=[pl.BlockSpec(blk, cur), pl.BlockSpec(blk, cur), *both(blk), *both(blk), *both(blk), *both(blk),
                  pl.BlockSpec(rblk, cur), pl.BlockSpec(rblk, cur)] + [pl.BlockSpec(blk, cur)] * len(accs),
        out_specs=[pl.BlockSpec(blk, cur)] * 3,
        out_shape=[jax.ShapeDtypeStruct((dil, m, e), BF16),
                   jax.ShapeDtypeStruct((dil, m, e), F32), jax.ShapeDtypeStruct((dil, m, e), F32)],
        scratch_shapes=[pltpu.VMEM((nb, e), F32), pltpu.VMEM((nb, e), BF16), pltpu.VMEM((nb, e), BF16),
                        pltpu.VMEM((nb, e), F32), pltpu.VMEM((nb, e), F32)],
        input_output_aliases={12: 1, 13: 2} if accumulate else {},
        compiler_params=_params(("parallel", "arbitrary")),
    )(k, v, q, q, dout, dout, lse, lse, delta, delta, cos, sin_inv, *accs)


def _rope_tables(s_len):
    inv_freq = 1.0 / (ROPE_THETA ** (jnp.arange(0, HEAD_DIM, 2, dtype=F32) / HEAD_DIM))
    ang = jnp.arange(s_len, dtype=F32)[:, None] * inv_freq[None, :]
    cos, sin = jnp.cos(ang), jnp.sin(ang)
    return jnp.concatenate([cos, cos], axis=1), jnp.concatenate([-sin, sin], axis=1)


def _row(vec):
    return vec.reshape(1, -1)


def _local_step(x, target, n_a, n_b, fetch, emit):
    s_len, d = x.shape
    n_q = len(DILATED_PAIRS)
    cos, sin = _rope_tables(s_len)
    sin_inv = -sin
    q_scale = 1.0 / math.sqrt(HEAD_DIM)
    w = {}

    def need(group, after):
        for name, (layer, arr) in fetch(group, after).items():
            w.setdefault(name, {})[layer] = arr

    saved_a = []
    for i in range(n_a):
        need(f"a{i}", x)
        h = _rmsnorm_fwd(f"a{i}_norm", x, _row(w["norm_a"][i]))
        proj = _mm_act_w(f"a{i}_in", h, w["w_in_a"][i], out_dtype=F32)
        pooled = _pool_fwd(f"a{i}_pool", proj)
        ypre = _mm_grp_fwd(f"a{i}_grp", pooled, w["w_grp_a"][i])
        z = _gate_a_fwd(f"a{i}_gate", ypre, proj, _row(w["scale_a"][i]))
        x_next = _mm_act_w(f"a{i}_out", z, w["w_out_a"][i], out_dtype=F32, add=x)
        saved_a.append((x, h, proj, pooled, ypre, z))
        x = x_next

    x_kv = x
    need("kv", x)
    e = w["w_k"][0].shape[1]
    kv_in = _rmsnorm_fwd("kv_norm", x, _row(w["norm_kv"][0]))
    windows = [window for window, _ in DILATED_PAIRS]
    dils = tuple(dil for _, dil in DILATED_PAIRS)
    far_dils = tuple(dil for dil in dils if dil > 1)
    ks = _mm_act_w("kv_k", kv_in, w["w_k"][0], rope=(cos, sin, 1.0), dils=dils)
    vs = _mm_act_w("kv_v", kv_in, w["w_v"][0], dils=dils)

    saved_b = []
    for i in range(n_b):
        need(f"b{i}", x if i > 0 else vs[0])
        hs = _rmsnorm_fwd(f"b{i}_norm", x, _row(w["norm_b"][i]), dils=far_dils)
        hs = {1: hs[0], **{dil: h_d.reshape(s_len, d) for dil, h_d in zip(far_dils, hs[1:])}}
        qs = [_mm_act_w(f"b{i}_q{g}", hs[1], w["w_in_b"][i], rope=(cos, sin, q_scale), n_first=g, n_cols=1,
                        dils=(dil,))[0] for g, dil in enumerate(dils)]
        gate = _mm_act_w(f"b{i}_g", hs[1], w["w_in_b"][i], out_dtype=F32, n_first=n_q, n_cols=1)
        outs, lses = [], []
        for g in range(n_q):
            o_g, l_g = _attn_fwd(f"b{i}_attn{g}", windows[g], qs[g], ks[g], vs[g])
            outs.append(o_g)
            lses.append(l_g)
        merged, lse, z = _merge_gate_fwd(f"b{i}_merge", outs, lses, gate, dils)
        x_next = _mm_act_w(f"b{i}_out", z, w["w_out_b"][i], out_dtype=F32, add=x)
        saved_b.append((x, hs, qs, gate, merged, lse, z))
        x = x_next

    need("head", x)
    loss_vec, dx, dxb, g_norm_f = _loss_head("loss_head", x, _row(w["norm_f"][0]), target)

    small = {"norm_a": {}, "scale_a": {}, "norm_kv": {}, "norm_b": {}, "norm_f": {0: g_norm_f}}
    shard_rows = lambda g2: g2.reshape(N_CHIPS, g2.shape[0] // N_CHIPS, g2.shape[1])

    res_major = lambda t, dil: t.reshape(s_len // dil, dil, t.shape[1]).transpose(1, 0, 2)
    cos_r = [res_major(cos, dil) for dil in dils]
    sin_inv_r = [res_major(sin_inv, dil) for dil in dils]
    dk_accs = [None] * len(dils)
    dv_accs = [None] * len(dils)
    for i in reversed(range(n_b)):
        x_in, hs, qs, gate, merged, lse, z = saved_b[i]
        dz = _mm_grad_act(f"b{i}_dz", dxb, w["w_out_b"][i])
        g_out = shard_rows(_mm_grad_w(f"b{i}_gwo", z, dxb))
        dgate, stats = _gate_b_bwd(f"b{i}_dgate", dz, merged, gate, lse, dils)
        dh = _mm_grad_act(f"b{i}_dh{n_q}", dgate, w["w_in_b"][i], slot=n_q)
        g_in = _mm_grad_w(f"b{i}_gwi{n_q}", hs[1], dgate, col_shards=n_q + 1, slot=n_q)
        for g, dil in enumerate(dils):
            dout, lse_g, delta_g = stats[g]
            dq, dk_accs[g], dv_accs[g] = _attn_bwd(f"b{i}_dattn{g}", windows[g], q_scale, qs[g], ks[g], vs[g], dout,
                                                   lse_g, delta_g, cos_r[g], sin_inv_r[g], dk_accs[g], dv_accs[g])
            dh = _mm_grad_act(f"b{i}_dh{g}", dq if dil > 1 else dq[0], w["w_in_b"][i], add=dh, slot=g)
            g_in = _mm_grad_w(f"b{i}_gwi{g}", hs[dil], dq.reshape(s_len, e), col_shards=n_q + 1, slot=g, into=g_in)
        dx, dxb, small["norm_b"][i] = _rmsnorm_bwd(f"b{i}_dnorm", x_in, _row(w["norm_b"][i]), dh, dx)
        dx, dxb = emit(f"b{i}", {"w_in_b": (i, g_in), "w_out_b": (i, g_out)}, (dx, dxb))

    dkb, dvb = _kv_grad_prep("kv_dprep", dk_accs, dv_accs, dils, cos, sin_inv)
    dkv = _mm_grad_act("kv_dk", dkb, w["w_k"][0])
    dkv = _mm_grad_act("kv_dv", dvb, w["w_v"][0], add=dkv)
    g_k = shard_rows(_mm_grad_w("kv_gwk", kv_in, dkb))
    g_v = shard_rows(_mm_grad_w("kv_gwv", kv_in, dvb))
    dx, dxb, small["norm_kv"][0] = _rmsnorm_bwd("kv_dnorm", x_kv, _row(w["norm_kv"][0]), dkv, dx)
    dx, dxb = emit("kv", {"w_k": (0, g_k), "w_v": (0, g_v)}, (dx, dxb))

    for i in reversed(range(n_a)):
        x_in, h, proj, pooled, ypre, z = saved_a[i]
        dz = _mm_grad_act(f"a{i}_dz", dxb, w["w_out_a"][i])
        g_out = shard_rows(_mm_grad_w(f"a{i}_gwo", z, dxb))
        dypre, dproj, small["scale_a"][i] = _gate_a_bwd(f"a{i}_dgate", dz, ypre, proj, _row(w["scale_a"][i]))
        dpooled = _mm_grp_grad_act(f"a{i}_dgrp", dypre, w["w_grp_a"][i])
        g_grp = _mm_grp_grad_w(f"a{i}_gwg", pooled, dypre, len(POOL_WINDOWS))
        g_grp = g_grp.reshape(N_CHIPS, -1, g_grp.shape[-1])
        last = i == 0
        if last:
            (dpooled,) = emit(f"a{i}", {"w_grp_a": (i, g_grp), "w_out_a": (i, g_out)}, (dpooled,))
        dproj = _pool_bwd(f"a{i}_dpool", dpooled, dproj)
        g_in = _mm_grad_w(f"a{i}_gwi", h, dproj, col_shards=N_CHIPS)
        if last:
            (dproj,) = emit(f"a{i}i", {"w_in_a": (i, g_in)}, (dproj,))
        dh = _mm_grad_act(f"a{i}_dh", dproj, w["w_in_a"][i])
        dx, dxb, small["norm_a"][i] = _rmsnorm_bwd(f"a{i}_dnorm", x_in, _row(w["norm_a"][i]), dh, dx)
        if not last:
            dx, dxb = emit(f"a{i}", {"w_in_a": (i, g_in), "w_grp_a": (i, g_grp), "w_out_a": (i, g_out)}, (dx, dxb))

    return loss_vec, dx, small


BIG_WEIGHTS = ("w_in_a", "w_grp_a", "w_out_a", "w_k", "w_v", "w_in_b", "w_out_b")


def _pair_add(name, grad, recv, c_idx):
    _, r, cols = grad.shape
    half = r // 2
    rb = _tile(half, 256)
    nrb = half // rb

    def body(c_ref, g_ref, r_ref, o_ref):
        o_ref[...] = (g_ref[...].astype(F32) + r_ref[...].astype(F32)).astype(BF16)

    blk = (None, rb, cols)
    grid_spec = pltpu.PrefetchScalarGridSpec(
        num_scalar_prefetch=1, grid=(N_CHIPS, nrb),
        in_specs=[pl.BlockSpec(blk, lambda s, i, c: (s, c[0] * nrb + i, 0)), pl.BlockSpec(blk, lambda s, i, c: (s, i, 0))],
        out_specs=pl.BlockSpec(blk, lambda s, i, c: (s, i, 0)))
    return pl.pallas_call(body, name=name, grid_spec=grid_spec,
                          out_shape=jax.ShapeDtypeStruct((N_CHIPS, half, cols), BF16),
                          compiler_params=_params(("parallel", "parallel")))(c_idx, grad, recv)


def _final_add(name, part, recv, sc_idx, layer, n_layers, into=None):
    _, half, cols = part.shape
    rb = _tile(half, 256)
    nrb = half // rb
    n_peer = recv.shape[0]

    def body(sc_ref, p_ref, *refs):
        acc = p_ref[...].astype(F32)
        for r_ref in refs[:n_peer]:
            acc = acc + r_ref[...].astype(F32)
        refs[-1][...] = acc

    blk = (None, rb, cols)
    peer_spec = lambda k: pl.BlockSpec(blk, lambda i, sc: (k, i, 0))
    grid_spec = pltpu.PrefetchScalarGridSpec(
        num_scalar_prefetch=1, grid=(nrb,),
        in_specs=[pl.BlockSpec(blk, lambda i, sc: (sc[0], i, 0))] + [peer_spec(k) for k in range(n_peer)]
                 + ([] if into is None else [ANY]),
        out_specs=pl.BlockSpec(blk, lambda i, sc: (layer, sc[1] * nrb + i, 0)))
    extra = () if into is None else (into,)
    return pl.pallas_call(body, name=name, grid_spec=grid_spec,
                          out_shape=jax.ShapeDtypeStruct((n_layers, 2 * half, cols), F32),
                          input_output_aliases={} if into is None else {2 + n_peer: 0},
                          compiler_params=_params(("parallel",)))(sc_idx, part, *([recv] * n_peer), *extra)


def _cast_into_slot(name, arr, layer, s_idx):
    _, b, r, cols = arr.shape
    rb = _tile(r, 512)

    def body(s_ref, a_ref, o_ref):
        o_ref[...] = a_ref[...].astype(BF16)

    blk = (None, None, rb, cols)
    grid_spec = pltpu.PrefetchScalarGridSpec(
        num_scalar_prefetch=1, grid=(b, r // rb),
        in_specs=[pl.BlockSpec(blk, lambda j, i, s: (layer, j, i, 0))],
        out_specs=pl.BlockSpec(blk, lambda j, i, s: (j, s[0], i, 0)))
    return pl.pallas_call(body, name=name, grid_spec=grid_spec,
                          out_shape=jax.ShapeDtypeStruct((b, N_CHIPS, r, cols), BF16),
                          compiler_params=_params(("parallel", "parallel")))(s_idx, arr)


def _sum_devices(name, gathered):
    n_dev, p, d = gathered.shape

    def body(g_ref, o_ref):
        acc = g_ref[0]
        for j in range(1, n_dev):
            acc = acc + g_ref[j]
        o_ref[...] = acc

    return pl.pallas_call(body, name=name, out_shape=jax.ShapeDtypeStruct((p, d), F32),
                          compiler_params=_params())(gathered)


def _adamw(name, w, g, m, v):
    shape = w.shape
    cols = shape[-1]
    flat = lambda a: a.reshape(-1, cols)
    rows = flat(w).shape[0]
    bs = _tile(rows, 256)

    def body(w_ref, g_ref, m_ref, v_ref, d_ref, mo_ref, vo_ref):
        grad = g_ref[...]
        m_new = ADAM_B1 * m_ref[...] + (1.0 - ADAM_B1) * grad
        v_new = ADAM_B2 * v_ref[...] + (1.0 - ADAM_B2) * (grad * grad)
        m_hat = m_new / (1.0 - ADAM_B1 ** ADAM_STEP)
        v_hat = v_new / (1.0 - ADAM_B2 ** ADAM_STEP)
        d_ref[...] = -ADAM_LR * (m_hat / (jnp.sqrt(v_hat) + ADAM_EPS) + ADAM_WD * w_ref[...])
        mo_ref[...] = m_new
        vo_ref[...] = v_new

    spec = _row_spec(bs, cols)
    outs = _rows_call(body, name, rows, [spec] * 4, [spec] * 3, [jax.ShapeDtypeStruct((rows, cols), F32)] * 3, bs)(
        flat(w), flat(g), flat(m), flat(v))
    return tuple(o.reshape(shape) for o in outs)


def _place():
    x, y, c = lax.axis_index("x"), lax.axis_index("y"), lax.axis_index("c")
    chips = [(1 - x, y), (x, 1 - y), (1 - x, 1 - y)]
    return x, y, c, chips


def _chip_index(chip):
    return 2 * chip[0] + chip[1]


def _comm_call(body, name, n_in, out_shape, scratch, aliases=None):
    return pl.pallas_call(body, name=name, in_specs=[ANY] * n_in, out_specs=[ANY] * len(out_shape), out_shape=out_shape,
                          scratch_shapes=scratch, input_output_aliases=aliases or {})


HBM_SPEC = pl.BlockSpec(memory_space=pltpu.HBM)
SEM_SPEC = pl.BlockSpec(memory_space=pltpu.SEMAPHORE)
SPLIT_PARAMS = pltpu.CompilerParams(has_side_effects=pltpu.SideEffectType.DATAFLOW_SIDE_EFFECTING)


def _in_hbm(arr):
    return pltpu.with_memory_space_constraint(arr, pltpu.HBM)


def _slot_half(ref, chip, core):
    half = ref.shape[2] // 2
    return ref.at[:, _chip_index(chip), pl.ds(core * half, half), :]


def _gather_start(name, bufs, carry=()):
    n, n_c = len(bufs), len(carry)

    def body(*refs):
        ins, (send_sems, recv_sems) = refs[:n], refs[n + n_c:n + n_c + 2]
        x, y, c, chips = _place()
        for a in range(n):
            block = _slot_half(ins[a], (x, y), c)
            for k, chip in enumerate(chips):
                pltpu.make_async_remote_copy(src_ref=block, dst_ref=block, send_sem=send_sems.at[3 * a + k],
                                             recv_sem=recv_sems.at[3 * a + k], device_id=(*chip, c),
                                             device_id_type=MESH).start()

    dma = pltpu.SemaphoreType.DMA
    thru = list(bufs) + list(carry)
    res = pl.pallas_call(
        body, name=name, in_specs=[HBM_SPEC] * (n + n_c), out_specs=[SEM_SPEC] * 2 + [HBM_SPEC] * (n + n_c),
        out_shape=[dma((3 * n,)), dma((3 * n,))] + [pltpu.HBM(a.shape, a.dtype) for a in thru],
        input_output_aliases={t: 2 + t for t in range(n + n_c)}, compiler_params=SPLIT_PARAMS,
    )(*[_in_hbm(a) for a in thru])
    return (res[0], res[1]), list(res[2:2 + n]), list(res[2 + n:])


def _gather_wait(name, sems, bufs, after):
    n = len(bufs)

    def body(*refs):
        ins, (send_sems, recv_sems) = refs[:n], refs[n:n + 2]
        x, y, c, chips = _place()
        for a in range(n):
            for k, chip in enumerate(chips):
                mine, theirs = _slot_half(ins[a], (x, y), c), _slot_half(ins[a], chip, c)
                copy = pltpu.make_async_remote_copy(src_ref=mine, dst_ref=theirs, send_sem=send_sems.at[3 * a + k],
                                                    recv_sem=recv_sems.at[3 * a + k], device_id=(*chip, c),
                                                    device_id_type=MESH)
                copy.wait_send()
                copy.wait_recv()

    res = pl.pallas_call(
        body, name=name, in_specs=[HBM_SPEC] * n + [SEM_SPEC, SEM_SPEC, ANY], out_specs=[HBM_SPEC] * n,
        out_shape=[pltpu.HBM(b.shape, b.dtype) for b in bufs], input_output_aliases={a: a for a in range(n)},
        compiler_params=SPLIT_PARAMS)(*bufs, *sems, after)
    return list(res)


def _gather_forward(name, bufs, smalls=()):
    n, n_small = len(bufs), len(smalls)

    def body(*refs):
        small_in = refs[n:n + n_small]
        outs = refs[n + n_small:2 * n + n_small]
        small_out = refs[2 * n + n_small:2 * n + 2 * n_small]
        send_sems, recv_sems, s_send, s_recv, s_local = refs[-5:]
        x, y, c, chips = _place()
        me, sibling = _chip_index((x, y)), (x, y, 1 - c)

        def forward(t, k, core):
            block = _slot_half(outs[t], chips[k], core)
            return pltpu.make_async_remote_copy(src_ref=block, dst_ref=block, send_sem=send_sems.at[t, k],
                                                recv_sem=recv_sems.at[t, k], device_id=sibling, device_id_type=MESH)

        def small_copy(j, k, slot):
            return pltpu.make_async_remote_copy(src_ref=small_in[j], dst_ref=small_out[j].at[slot],
                                                send_sem=s_send.at[j, k], recv_sem=s_recv.at[j, k],
                                                device_id=(*chips[k], c), device_id_type=MESH)

        local = []
        for t in range(n):
            for k in range(3):
                forward(t, k, c).start()
        for j in range(n_small):
            own = pltpu.make_async_copy(small_in[j], small_out[j].at[me], s_local.at[j])
            own.start()
            local.append(own)
            for k in range(3):
                small_copy(j, k, me).start()
        for t in range(n):
            for k in range(3):
                forward(t, k, 1 - c).wait_recv()
        for j in range(n_small):
            for k in range(3):
                small_copy(j, k, _chip_index(chips[k])).wait_recv()
        for t in range(n):
            for k in range(3):
                forward(t, k, c).wait_send()
        for j in range(n_small):
            for k in range(3):
                small_copy(j, k, me).wait_send()
        for own in local:
            own.wait()

    out_shape = [jax.ShapeDtypeStruct(b.shape, BF16) for b in bufs]
    out_shape += [jax.ShapeDtypeStruct((N_CHIPS,) + s.shape, F32) for s in smalls]
    dma = pltpu.SemaphoreType.DMA
    n_s = max(n_small, 1)
    res = _comm_call(body, name, n + n_small, out_shape,
                     [dma((n, 3)), dma((n, 3)), dma((n_s, 3)), dma((n_s, 3)), dma((n_s,))],
                     aliases={t: t for t in range(n)})(*bufs, *smalls)
    return list(res[:n]), list(res[n:])


def _exchange_halves(name, grads):
    n = len(grads)

    def body(*refs):
        g_in, outs = refs[:n], refs[n:2 * n]
        send_sems, recv_sems = refs[-2:]
        x, y, c, _ = _place()
        copies = []
        for t in range(n):
            half = g_in[t].shape[1] // 2
            cp = pltpu.make_async_remote_copy(
                src_ref=g_in[t].at[:, pl.ds((1 - c) * half, half), :], dst_ref=outs[t], send_sem=send_sems.at[t],
                recv_sem=recv_sems.at[t], device_id=(x, y, 1 - c), device_id_type=MESH)
            cp.start()
            copies.append(cp)
        for cp in copies:
            cp.wait()

    out_shape = [jax.ShapeDtypeStruct((g.shape[0], g.shape[1] // 2, g.shape[2]), BF16) for g in grads]
    dma = pltpu.SemaphoreType.DMA
    return list(_comm_call(body, name, n, out_shape, [dma((n,)), dma((n,))])(*grads))


def _scatter_copy(part_ref, land_ref, send_sems, recv_sems, t, k, chip, c):
    return pltpu.make_async_remote_copy(
        src_ref=part_ref.at[_chip_index(chip)], dst_ref=land_ref.at[k], send_sem=send_sems.at[3 * t + k],
        recv_sem=recv_sems.at[3 * t + k], device_id=(*chip, c), device_id_type=MESH)


def _scatter_start(name, parts, carry=()):
    n, n_c = len(parts), len(carry)
    lands = [lax.empty((3,) + p.shape[1:], BF16) for p in parts]

    def body(*refs):
        p_in, l_in = refs[:n], refs[n:2 * n]
        send_sems, recv_sems = refs[2 * n + n_c:2 * n + n_c + 2]
        x, y, c, chips = _place()
        for t in range(n):
            for k, chip in enumerate(chips):
                _scatter_copy(p_in[t], l_in[t], send_sems, recv_sems, t, k, chip, c).start()

    dma = pltpu.SemaphoreType.DMA
    thru = list(parts) + lands + list(carry)
    res = pl.pallas_call(
        body, name=name, in_specs=[HBM_SPEC] * len(thru), out_specs=[SEM_SPEC] * 2 + [HBM_SPEC] * len(thru),
        out_shape=[dma((3 * n,)), dma((3 * n,))] + [pltpu.HBM(a.shape, a.dtype) for a in thru],
        input_output_aliases={t: 2 + t for t in range(len(thru))}, compiler_params=SPLIT_PARAMS,
    )(*[_in_hbm(a) for a in thru])
    return (res[0], res[1]), list(res[2:2 + n]), list(res[2 + n:2 + 2 * n]), list(res[2 + 2 * n:])


def _scatter_wait(name, sems, parts, lands):
    n = len(parts)

    def body(*refs):
        p_in, l_in = refs[:n], refs[n:2 * n]
        send_sems, recv_sems = refs[2 * n:2 * n + 2]
        x, y, c, chips = _place()
        for t in range(n):
            for k, chip in enumerate(chips):
                copy = _scatter_copy(p_in[t], l_in[t], send_sems, recv_sems, t, k, chip, c)
                copy.wait_send()
                copy.wait_recv()

    hbm_out = lambda a: pltpu.HBM(a.shape, a.dtype)
    res = pl.pallas_call(
        body, name=name, in_specs=[HBM_SPEC] * (2 * n) + [SEM_SPEC, SEM_SPEC], out_specs=[HBM_SPEC] * (2 * n),
        out_shape=[hbm_out(a) for a in parts + lands], input_output_aliases={t: t for t in range(2 * n)},
        compiler_params=SPLIT_PARAMS)(*parts, *lands, *sems)
    return list(res[:n]), list(res[n:])


def _share_halves(fulls):
    n = len(fulls)
    items = [(a, l) for a in range(n) for l in range(fulls[a].shape[0])]

    def body(*refs):
        outs = refs[n:2 * n]
        send_sems, recv_sems = refs[-2:]
        x, y, c, _ = _place()

        def copy(t, core):
            a, l = items[t]
            half = outs[a].shape[1] // 2
            block = outs[a].at[l, pl.ds(core * half, half), :]
            return pltpu.make_async_remote_copy(src_ref=block, dst_ref=block, send_sem=send_sems.at[t],
                                                recv_sem=recv_sems.at[t], device_id=(x, y, 1 - c), device_id_type=MESH)

        for t in range(len(items)):
            copy(t, c).start()
        for t in range(len(items)):
            copy(t, 1 - c).wait_recv()
        for t in range(len(items)):
            copy(t, c).wait_send()

    out_shape = [jax.ShapeDtypeStruct(f.shape, F32) for f in fulls]
    dma = pltpu.SemaphoreType.DMA
    return list(_comm_call(body, "grad_share_halves", n, out_shape, [dma((len(items),)), dma((len(items),))],
                           aliases={a: a for a in range(n)})(*fulls))


def _allgather_small(packed):
    def body(p_ref, o_ref, send_sems, recv_sems, local_sem):
        x, y, c, _ = _place()
        me = 4 * x + 2 * y + c
        own = pltpu.make_async_copy(p_ref, o_ref.at[me], local_sem)
        own.start()
        flips = [(fx, fy, fc) for fx in (0, 1) for fy in (0, 1) for fc in (0, 1)][1:]
        peers = [(x ^ fx, y ^ fy, c ^ fc) for fx, fy, fc in flips]
        copies = []
        for k, peer in enumerate(peers):
            cp = pltpu.make_async_remote_copy(src_ref=p_ref, dst_ref=o_ref.at[me], send_sem=send_sems.at[k],
                                              recv_sem=recv_sems.at[k], device_id=peer, device_id_type=MESH)
            cp.start()
            copies.append(cp)
        for k, (px, py, pc) in enumerate(peers):
            pltpu.make_async_remote_copy(src_ref=p_ref, dst_ref=o_ref.at[4 * px + 2 * py + pc], send_sem=send_sems.at[k],
                                         recv_sem=recv_sems.at[k], device_id=peers[k], device_id_type=MESH).wait_recv()
        for cp in copies:
            cp.wait_send()
        own.wait()

    dma = pltpu.SemaphoreType.DMA
    return _comm_call(body, "small_allgather", 1, [jax.ShapeDtypeStruct((8,) + packed.shape, F32)],
                      [dma((7,)), dma((7,)), dma(())])(packed)[0]


PAD_ROWS = 8


def kernel(x, norm_a, w_in_a, w_grp_a, scale_a, w_out_a, norm_kv, w_k, w_v, norm_b, w_in_b, w_out_b, norm_f, loss_target, m_norm_a, m_w_in_a, m_w_grp_a, m_scale_a, m_w_out_a, m_norm_kv, m_w_k, m_w_v, m_norm_b, m_w_in_b, m_w_out_b, m_norm_f, v_norm_a, v_w_in_a, v_w_grp_a, v_scale_a, v_w_out_a, v_norm_kv, v_w_k, v_w_v, v_norm_b, v_w_in_b, v_w_out_b, v_norm_f):
    weights = dict(norm_a=norm_a, w_in_a=w_in_a, w_grp_a=w_grp_a, scale_a=scale_a, w_out_a=w_out_a, norm_kv=norm_kv,
                   w_k=w_k, w_v=w_v, norm_b=norm_b, w_in_b=w_in_b, w_out_b=w_out_b, norm_f=norm_f)
    moments_m = dict(norm_a=m_norm_a, w_in_a=m_w_in_a, w_grp_a=m_w_grp_a, scale_a=m_scale_a, w_out_a=m_w_out_a,
                     norm_kv=m_norm_kv, w_k=m_w_k, w_v=m_w_v, norm_b=m_norm_b, w_in_b=m_w_in_b, w_out_b=m_w_out_b,
                     norm_f=m_norm_f)
    moments_v = dict(norm_a=v_norm_a, w_in_a=v_w_in_a, w_grp_a=v_w_grp_a, scale_a=v_scale_a, w_out_a=v_w_out_a,
                     norm_kv=v_norm_kv, w_k=v_w_k, w_v=v_w_v, norm_b=v_norm_b, w_in_b=v_w_in_b, w_out_b=v_w_out_b,
                     norm_f=v_norm_f)
    names = list(weights)
    d = x.shape[-1]
    c_idx = lax.axis_index("c").astype(jnp.int32).reshape(1)
    s_me = 2 * lax.axis_index("x") + lax.axis_index("y")
    s_idx = s_me.astype(jnp.int32).reshape(1)

    def as_lbrc(name):
        a = weights[name]
        if name == "w_grp_a":
            return a
        if a.ndim == 2:
            return a.reshape(1, 1, *a.shape)
        return a.reshape(a.shape[0], 1, *a.shape[1:])

    n_a, n_b = norm_a.shape[0], norm_b.shape[0]
    group_weights = {**{f"a{i}": [("w_in_a", i), ("w_grp_a", i), ("w_out_a", i)] for i in range(n_a)},
                     "kv": [("w_k", 0), ("w_v", 0)],
                     **{f"b{i}": [("w_in_b", i), ("w_out_b", i)] for i in range(n_b)}}
    group_order = [f"a{i}" for i in range(n_a)] + ["kv"] + [f"b{i}" for i in range(n_b)]
    slots, slot_groups = [], []
    for group in group_order:
        slot_groups.append(list(range(len(slots), len(slots) + len(group_weights[group]))))
        slots += [_cast_into_slot(f"cast_{name}{l}", as_lbrc(name), l, s_idx) for name, l in group_weights[group]]
    small_full, started = {}, {}

    def start_group(gi, carry=()):
        sems, bufs, carry = _gather_start(f"gather_start_{group_order[gi]}", [slots[t] for t in slot_groups[gi]], carry)
        started[gi] = (sems, bufs)
        return carry

    start_group(0)

    def gathered_form(name, g):
        if name in ("w_in_a", "w_in_b"):
            return g[0]
        if name == "w_grp_a":
            return g.reshape(g.shape[0], -1, g.shape[-1])
        return g.reshape(-1, g.shape[-1])

    def fetch(group, after):
        if group == "head":
            return {"norm_f": (0, norm_f)}
        gi = group_order.index(group)
        sems, bufs = started[gi]
        bufs = _gather_wait(f"gather_wait_{group}", sems, bufs, after)
        bufs, small_g = _gather_forward(f"gather_forward_{group}", bufs, [norm_a, scale_a] if gi == 0 else [])
        out = {name: (l, gathered_form(name, g)) for (name, l), g in zip(group_weights[group], bufs)}
        if gi == 0:
            for name, g in zip(("norm_a", "scale_a"), small_g):
                small_full[name] = g.transpose(1, 0, 2).reshape(g.shape[1], -1)
        layer = group_weights[group][0][1]
        if group.startswith("a"):
            gain_name, gain = "norm_a", small_full["norm_a"][layer]
            out.update(scale_a=(layer, small_full["scale_a"][layer]))
        elif group == "kv":
            gain_name, gain = "norm_kv", norm_kv
        else:
            gain_name, gain = "norm_b", norm_b[layer]
        gain = gain.reshape(1, -1)
        ahead = [gi + 1] + ([gi + 2] if gi + 2 < len(group_order) and group_order[gi + 1] == "kv" else [])
        for gj in ahead:
            if gj < len(group_order) and gj not in started:
                (gain,) = start_group(gj, (gain,))
        out[gain_name] = (layer, gain)
        return out

    in_flight = []

    def emit(group, grads_of, carry):
        keys = list(grads_of)
        recv1 = _exchange_halves(f"grad_exchange_{group}", [grads_of[k][1] for k in keys])
        parts = [_pair_add(f"pair_add_{k}{grads_of[k][0]}", grads_of[k][1], r, c_idx) for k, r in zip(keys, recv1)]
        sems, parts, lands, carry = _scatter_start(f"grad_scatter_start_{group}", parts, tuple(carry))
        in_flight.append((group, [(k, grads_of[k][0]) for k in keys], sems, parts, lands))
        return carry

    loss_vec, grad_x, small = _local_step(x[0], loss_target[0], n_a, n_b, fetch, emit)

    small_order = [("norm_a", i) for i in range(n_a)] + [("scale_a", i) for i in range(n_a)] + [("norm_kv", 0)] + \
                  [("norm_b", i) for i in range(norm_b.shape[0])] + [("norm_f", 0)]
    pad = lambda vec: jnp.pad(vec, ((0, PAD_ROWS - 1), (0, 0)))
    packed = jnp.concatenate([pad(loss_vec)] + [pad(small[n][i]) for n, i in small_order], axis=0)
    totals = _sum_devices("small_sum", _allgather_small(packed))
    loss = 0.5 * jnp.sum(totals[0]) / d
    small_tot = {}
    for j, (n, i) in enumerate(small_order):
        small_tot.setdefault(n, []).append(totals[PAD_ROWS * (j + 1)])
    grads = {}
    shard_w = norm_a.shape[1]
    for n in ("norm_a", "scale_a"):
        full = jnp.stack(small_tot[n])
        grads[n] = lax.dynamic_slice_in_dim(full, s_me * shard_w, shard_w, axis=1)
    grads["norm_kv"] = small_tot["norm_kv"][0]
    grads["norm_b"] = jnp.stack(small_tot["norm_b"])
    grads["norm_f"] = small_tot["norm_f"][0]

    sc_idx = jnp.concatenate([s_idx, c_idx])
    fulls = {name: None for name in BIG_WEIGHTS}
    for group, keys, sems, parts, lands in in_flight:
        parts, lands = _scatter_wait(f"grad_scatter_wait_{group}", sems, parts, lands)
        for (name, i), p, r in zip(keys, parts, lands):
            n_layers = 1 if weights[name].ndim == 2 else weights[name].shape[0]
            fulls[name] = _final_add(f"final_add_{name}{i}", p, r, sc_idx, i, n_layers, into=fulls[name])
    shared = _share_halves([fulls[name] for name in BIG_WEIGHTS])
    for name, g in zip(BIG_WEIGHTS, shared):
        grads[name] = g.reshape(weights[name].shape)

    deltas, new_m, new_v = {}, {}, {}
    for n in names:
        shape = weights[n].shape
        as2d = (lambda a: a.reshape(1, -1)) if len(shape) == 1 else (lambda a: a)
        dl, mn, vn = _adamw(f"adamw_{n}", as2d(weights[n]), as2d(grads[n]), as2d(moments_m[n]), as2d(moments_v[n]))
        deltas[n], new_m[n], new_v[n] = dl.reshape(shape), mn.reshape(shape), vn.reshape(shape)

    return (loss, grad_x[None], *[grads[n] for n in names], *[deltas[n] for n in names],
            *[new_m[n] for n in names], *[new_v[n] for n in names])
```

```python
import functools
import math

import jax
import jax.numpy as jnp
from jax import lax
from jax.experimental import pallas as pl
from jax.experimental.pallas import tpu as pltpu

F32 = jnp.float32
BF16 = jnp.bfloat16

HEAD_DIM = 128
POOL_WINDOWS = (2, 4, 8, 16)
DILATED_PAIRS = ((128, 1), (512, 4), (2048, 16))
ROPE_THETA = 10000.0
RMS_EPS = 1e-6
NEG_INF = -1e30
N_CHIPS = 4

ADAM_LR = 0.001
ADAM_B1 = 0.9
ADAM_B2 = 0.999
ADAM_EPS = 1e-08
ADAM_WD = 0.01
ADAM_STEP = 10

VMEM_LIMIT_BYTES = 56 * 1024 * 1024
MESH = pl.DeviceIdType.MESH
ANY = pl.BlockSpec(memory_space=pl.ANY)


def _tile(n, pref):
    t = min(n, pref)
    assert n % t == 0, (n, pref)
    return t


def _params(sem=None):
    return pltpu.CompilerParams(dimension_semantics=sem, vmem_limit_bytes=VMEM_LIMIT_BYTES)


def _mm(name, a, b, *, grid2, nk, a_blk, a_map, b_blk, b_map, outs, dims, epi=None, epi_in=(), epi_specs=(),
        acc_shape=None, epi_scratch=(), into=None):
    n_epi, n_out = len(epi_in), len(outs)

    def body(*refs):
        a_ref, b_ref = refs[0], refs[1]
        e_refs = refs[2:2 + n_epi]
        first_out = 2 + n_epi + (0 if into is None else 1)
        o_refs = refs[first_out:first_out + n_out]
        s_refs = refs[first_out + n_out + (0 if nk == 1 else 1):]

        def contrib():
            a_val = a_ref[...]
            if a_val.ndim == 3:
                a_val = a_val.reshape(-1, a_val.shape[-1])
            return lax.dot_general(a_val, b_ref[...], (dims, ((), ())), preferred_element_type=F32)

        def finish(acc):
            if epi is None:
                o_refs[0][...] = acc.reshape(o_refs[0].shape).astype(o_refs[0].dtype)
            else:
                epi(acc, e_refs, o_refs, s_refs)

        if nk == 1:
            finish(contrib())
        else:
            acc_ref = refs[first_out + n_out]
            k = pl.program_id(2)

            @pl.when(k == 0)
            def _():
                acc_ref[...] = contrib()

            @pl.when(k > 0)
            def _():
                acc_ref[...] += contrib()

            @pl.when(k == nk - 1)
            def _():
                finish(acc_ref[...])

    scratch = ([] if nk == 1 else [pltpu.VMEM(acc_shape, F32)]) + list(epi_scratch)
    extra_in, extra_specs, aliases = (), (), {}
    if into is not None:
        extra_in, extra_specs, aliases = (into[0],), (ANY,), {2 + n_epi: into[1]}
    res = pl.pallas_call(
        body, name=name, grid=(grid2[0], grid2[1], nk),
        in_specs=[pl.BlockSpec(a_blk, a_map), pl.BlockSpec(b_blk, b_map), *epi_specs, *extra_specs],
        out_specs=[pl.BlockSpec(blk, imap) for _, blk, imap, _ in outs],
        out_shape=[jax.ShapeDtypeStruct(shape, dtype) for shape, _, _, dtype in outs],
        scratch_shapes=scratch, input_output_aliases=aliases,
        compiler_params=_params(("parallel", "parallel", "arbitrary")),
    )(a, b, *epi_in, *extra_in)
    return res[0] if n_out == 1 else tuple(res)


NN = ((1,), (0,))
NT = ((1,), (1,))
TN = ((0,), (0,))


def _rope_apply(t, cos, sin):
    return t * cos + pltpu.roll(t, HEAD_DIM // 2, 1) * sin


def _epi_add(acc, e_refs, o_refs, s_refs):
    o_refs[0][...] = (acc + e_refs[0][...]).astype(o_refs[0].dtype)


def _col_blocks(width):
    return [slice(c * HEAD_DIM, (c + 1) * HEAD_DIM) for c in range(width // HEAD_DIM)]


def _col_scratch(rows, width):
    return pltpu.VMEM((width // HEAD_DIM, rows, HEAD_DIM), F32)


def _to_residue_major(o_ref, scr, d, sl):
    if d == 1:
        o_ref[0, :, sl] = scr[...].astype(o_ref.dtype)
        return
    rows = scr.shape[0] // d
    for r in range(d):
        o_ref[r, :, sl] = scr[pl.ds(r, rows, stride=d), :].astype(o_ref.dtype)


def _from_residue_major(i_ref, scr, d, sl):
    if d == 1:
        return i_ref[0, :, sl].astype(F32)
    rows = i_ref.shape[1]
    for r in range(d):
        scr[pl.ds(r, rows, stride=d), :] = i_ref[r, :, sl].astype(F32)
    return scr[...]


def _make_epi_orders(dils, rope_scale):
    def epi(acc, e_refs, o_refs, s_refs):
        if rope_scale is not None:
            cos = e_refs[0][...]
            sin = e_refs[1][...]
        for c, sl in enumerate(_col_blocks(acc.shape[1])):
            scr = s_refs[0].at[c]
            scr[...] = acc[:, sl] if rope_scale is None else _rope_apply(acc[:, sl], cos, sin) * rope_scale
            for o_ref, d in zip(o_refs, dils):
                _to_residue_major(o_ref, scr, d, sl)
    return epi


def _make_epi_token_order(d, has_add):
    def epi(acc, e_refs, o_refs, s_refs):
        o_ref = o_refs[0]
        if d == 1:
            o_ref[...] = acc + e_refs[0][...] if has_add else acc
            return
        rows = acc.shape[0] // d
        for c, sl in enumerate(_col_blocks(acc.shape[1])):
            scr = s_refs[0].at[c]
            for r in range(d):
                scr[pl.ds(r, rows, stride=d), :] = acc[r * rows:(r + 1) * rows, sl]
            o_ref[:, sl] = scr[...] + e_refs[0][:, sl] if has_add else scr[...]
    return epi


def _mm_act_w(name, a, w, *, out_dtype=BF16, add=None, rope=None, n_first=0, n_cols=None, dils=None):
    s_len, k_len = a.shape
    bm = _tile(s_len, 1024)
    epi, epi_in, epi_specs, epi_scratch = None, (), (), ()
    if w.ndim == 3:
        ns, _, c = w.shape
        ns_used = ns if n_cols is None else n_cols
        bn = _tile(c, 1024)
        sub = c // bn
        grid2 = (ns_used * sub, s_len // bm)
        b_blk, b_map = (None, k_len, bn), (lambda j, i, k: (j // sub + n_first, 0, j % sub))
        n_len = ns_used * c
    else:
        n_len = w.shape[1]
        bn = _tile(n_len, 1024)
        grid2 = (n_len // bn, s_len // bm)
        b_blk, b_map = (k_len, bn), (lambda j, i, k: (0, j))
    if add is not None:
        epi, epi_in = _epi_add, (add,)
        epi_specs = (pl.BlockSpec((bm, bn), lambda j, i, k: (i, j)),)
    outs = [((s_len, n_len), (bm, bn), lambda j, i, k: (i, j), out_dtype)]
    if dils is not None:
        if rope is not None:
            epi_in = rope[:2]
            epi_specs = (pl.BlockSpec((bm, HEAD_DIM), lambda j, i, k: (i, 0)),) * 2
        epi = _make_epi_orders(dils, None if rope is None else rope[2])
        epi_scratch = (_col_scratch(bm, bn),)
        outs = [((d, s_len // d, n_len), (d, bm // d, bn), lambda j, i, k: (0, i, j), BF16) for d in dils]
    res = _mm(name, a, w, grid2=grid2, nk=1, a_blk=(bm, k_len), a_map=lambda j, i, k: (i, 0),
              b_blk=b_blk, b_map=b_map, outs=outs, dims=NN, epi=epi, epi_in=epi_in, epi_specs=epi_specs,
              epi_scratch=epi_scratch)
    return (res,) if dils is not None and len(dils) == 1 else res


def _mm_grad_act(name, dy, w, *, add=None, slot=None):
    if slot is not None:
        d = 1 if dy.ndim == 2 else dy.shape[0]
        s_len = dy.shape[-2] * d
        _, k_len, c = w.shape
        bm, bn = _tile(s_len, 1024), _tile(k_len, 1024)
        a_blk, a_map = ((bm, c), lambda j, i, k: (i, 0)) if dy.ndim == 2 else ((d, bm // d, c), lambda j, i, k: (0, i, 0))
        epi_in = () if add is None else (add,)
        return _mm(name, dy, w, grid2=(k_len // bn, s_len // bm), nk=1, a_blk=a_blk, a_map=a_map,
                   b_blk=(None, bn, c), b_map=lambda j, i, k: (slot, j, 0),
                   outs=[((s_len, k_len), (bm, bn), lambda j, i, k: (i, j), F32)], dims=NT,
                   epi=_make_epi_token_order(d, add is not None), epi_in=epi_in,
                   epi_specs=(pl.BlockSpec((bm, bn), lambda j, i, k: (i, j)),) * len(epi_in),
                   epi_scratch=(_col_scratch(bm, bn),) if d > 1 else ())
    s_len, n_len = dy.shape
    bm = _tile(s_len, 1024)
    if w.ndim == 3:
        ns, k_len, c = w.shape
        bk, nk = c, ns
        bn = _tile(k_len, 1024)
        b_blk, b_map = (None, bn, c), (lambda j, i, k: (k, j, 0))
    else:
        k_len = w.shape[0]
        bk = _tile(n_len, 1024)
        nk = n_len // bk
        bn = _tile(k_len, 1024)
        b_blk, b_map = (bn, bk), (lambda j, i, k: (j, k))
    epi, epi_in, epi_specs = None, (), ()
    if add is not None:
        epi, epi_in = _epi_add, (add,)
        epi_specs = (pl.BlockSpec((bm, bn), lambda j, i, k: (i, j)),)
    return _mm(name, dy, w, grid2=(k_len // bn, s_len // bm), nk=nk, a_blk=(bm, bk), a_map=lambda j, i, k: (i, k),
               b_blk=b_blk, b_map=b_map, outs=[((s_len, k_len), (bm, bn), lambda j, i, k: (i, j), F32)],
               dims=NT, epi=epi, epi_in=epi_in, epi_specs=epi_specs, acc_shape=(bm, bn))


def _mm_grad_w(name, a, dy, *, col_shards=None, slot=None, into=None):
    s_len, k_len = a.shape
    n_len = dy.shape[1]
    bk = _tile(s_len, 1024)
    bm = _tile(k_len, 1024)
    if slot is not None:
        bn = _tile(n_len, 1024)
        out = ((col_shards, k_len, n_len), (None, bm, bn), lambda j, i, k: (slot, i, j), BF16)
    elif col_shards:
        c = n_len // col_shards
        bn = _tile(c, 1024)
        sub = c // bn
        out = ((col_shards, k_len, c), (None, bm, bn), lambda j, i, k: (j // sub, i, j % sub), BF16)
    else:
        bn = _tile(n_len, 1024)
        out = ((k_len, n_len), (bm, bn), lambda j, i, k: (i, j), BF16)
    return _mm(name, a, dy, grid2=(n_len // bn, k_len // bm), nk=s_len // bk,
               a_blk=(bk, bm), a_map=lambda j, i, k: (k, i), b_blk=(bk, bn), b_map=lambda j, i, k: (k, j),
               outs=[out], dims=TN, acc_shape=(bm, bn), into=None if into is None else (into, 0))


def _mm_grp_fwd(name, pooled, wg):
    s_len, e = pooled.shape
    ng, g, _ = wg.shape
    bm = _tile(s_len, 1024)
    return _mm(name, pooled, wg, grid2=(ng, s_len // bm), nk=1, a_blk=(bm, g), a_map=lambda j, i, k: (i, j),
               b_blk=(None, g, g), b_map=lambda j, i, k: (j, 0, 0),
               outs=[((s_len, e), (bm, g), lambda j, i, k: (i, j), F32)], dims=NN)


def _mm_grp_grad_act(name, dy, wg):
    s_len, e = dy.shape
    ng, g, _ = wg.shape
    bm = _tile(s_len, 1024)
    return _mm(name, dy, wg, grid2=(ng, s_len // bm), nk=1, a_blk=(bm, g), a_map=lambda j, i, k: (i, j),
               b_blk=(None, g, g), b_map=lambda j, i, k: (j, 0, 0),
               outs=[((s_len, e), (bm, g), lambda j, i, k: (i, j), F32)], dims=NT)


def _mm_grp_grad_w(name, pooled, dy, ng):
    s_len, e = pooled.shape
    g = e // ng
    bk = _tile(s_len, 1024)
    return _mm(name, pooled, dy, grid2=(ng, 1), nk=s_len // bk, a_blk=(bk, g), a_map=lambda j, i, k: (k, j),
               b_blk=(bk, g), b_map=lambda j, i, k: (k, j),
               outs=[((N_CHIPS, ng, g // N_CHIPS, g), (N_CHIPS, None, g // N_CHIPS, g), lambda j, i, k: (0, j, 0, 0), BF16)],
               dims=TN, acc_shape=(g, g))


def _row_spec(bs, width, col=0):
    return pl.BlockSpec((bs, width), lambda i: (i, col))


def _vec_spec(width):
    return pl.BlockSpec((1, width), lambda i: (0, 0))


def _rows_call(body, name, s_len, in_specs, out_specs, out_shape, bs, aliases=None, sequential=False):
    return pl.pallas_call(
        body, name=name, grid=(s_len // bs,), in_specs=in_specs, out_specs=out_specs, out_shape=out_shape,
        input_output_aliases=aliases or {},
        compiler_params=_params(("arbitrary",) if sequential else ("parallel",)))


def _accumulate(ref, part):
    i = pl.program_id(0)

    @pl.when(i == 0)
    def _():
        ref[...] = part

    @pl.when(i > 0)
    def _():
        ref[...] += part


def _rms_scale(xf):
    return lax.rsqrt(jnp.mean(xf * xf, axis=-1, keepdims=True) + RMS_EPS)


def _res_spec(dil, bs, width):
    return pl.BlockSpec((dil, bs // dil, width), lambda i: (0, i, 0))


def _res_shape(dil, s_len, width, dtype):
    return jax.ShapeDtypeStruct((dil, s_len // dil, width), dtype)


def _rmsnorm_fwd(name, x, gain, dils=()):
    s_len, d = x.shape
    bs = _tile(s_len, 256)

    def body(x_ref, g_ref, h_ref, *rest):
        xf = x_ref[...]
        h = (xf * _rms_scale(xf)) * g_ref[...]
        h_ref[...] = h.astype(BF16)
        if dils:
            for c, sl in enumerate(_col_blocks(d)):
                scr = rest[-1].at[c]
                scr[...] = h[:, sl]
                for o_ref, dil in zip(rest[:-1], dils):
                    _to_residue_major(o_ref, scr, dil, sl)

    res = pl.pallas_call(
        body, name=name, grid=(s_len // bs,), in_specs=[_row_spec(bs, d), _vec_spec(d)],
        out_specs=[_row_spec(bs, d)] + [_res_spec(dil, bs, d) for dil in dils],
        out_shape=[jax.ShapeDtypeStruct((s_len, d), BF16)] + [_res_shape(dil, s_len, d, BF16) for dil in dils],
        scratch_shapes=[_col_scratch(bs, d)] if dils else [],
        compiler_params=_params(("parallel",)))(x, gain)
    return res[0] if not dils else tuple(res)


def _rmsnorm_bwd(name, x, gain, dh, dres):
    s_len, d = x.shape
    bs = _tile(s_len, 256)

    def body(x_ref, g_ref, dh_ref, dres_ref, dx_ref, dxb_ref, dg_ref):
        xf = x_ref[...]
        r = _rms_scale(xf)
        xh = xf * r
        dh_f = dh_ref[...]
        t = dh_f * g_ref[...]
        dx = dres_ref[...] + r * (t - xh * jnp.mean(t * xh, axis=-1, keepdims=True))
        dx_ref[...] = dx
        dxb_ref[...] = dx.astype(BF16)
        _accumulate(dg_ref, jnp.sum(dh_f * xh, axis=0, keepdims=True))

    return _rows_call(
        body, name, s_len,
        [_row_spec(bs, d), _vec_spec(d), _row_spec(bs, d), _row_spec(bs, d)],
        [_row_spec(bs, d), _row_spec(bs, d), _vec_spec(d)],
        [jax.ShapeDtypeStruct((s_len, d), F32), jax.ShapeDtypeStruct((s_len, d), BF16),
         jax.ShapeDtypeStruct((1, d), F32)], bs, sequential=True)(x, gain, dh, dres)


def _loss_head(name, x, gain, target):
    s_len, d = x.shape
    bs = _tile(s_len, 256)

    def body(x_ref, g_ref, t_ref, lv_ref, dx_ref, dxb_ref, dg_ref):
        xf = x_ref[...]
        r = _rms_scale(xf)
        xh = xf * r
        err = xh * g_ref[...] - t_ref[...]
        dy = err * (1.0 / d)
        t = dy * g_ref[...]
        dx = r * (t - xh * jnp.mean(t * xh, axis=-1, keepdims=True))
        dx_ref[...] = dx
        dxb_ref[...] = dx.astype(BF16)
        _accumulate(lv_ref, jnp.sum(err * err, axis=0, keepdims=True))
        _accumulate(dg_ref, jnp.sum(dy * xh, axis=0, keepdims=True))

    return _rows_call(
        body, name, s_len, [_row_spec(bs, d), _vec_spec(d), _row_spec(bs, d)],
        [_vec_spec(d), _row_spec(bs, d), _row_spec(bs, d), _vec_spec(d)],
        [jax.ShapeDtypeStruct((1, d), F32), jax.ShapeDtypeStruct((s_len, d), F32),
         jax.ShapeDtypeStruct((s_len, d), BF16), jax.ShapeDtypeStruct((1, d), F32)],
        bs, sequential=True)(x, gain, target)


def _sigmoid(g):
    return 1.0 / (1.0 + jnp.exp(-g))


def _gate_a_fwd(name, ypre, proj, scale):
    s_len, e = ypre.shape
    bs = _tile(s_len, 256)

    def body(y_ref, g_ref, sc_ref, z_ref):
        g = g_ref[...]
        z_ref[...] = (y_ref[...] * sc_ref[...] * (g * _sigmoid(g))).astype(BF16)

    return _rows_call(body, name, s_len, [_row_spec(bs, e), _row_spec(bs, e, 1), _vec_spec(e)], _row_spec(bs, e),
                      jax.ShapeDtypeStruct((s_len, e), BF16), bs)(ypre, proj, scale)


def _gate_a_bwd(name, dz, ypre, proj, scale):
    s_len, e = ypre.shape
    bs = _tile(s_len, 256)

    def body(dz_ref, y_ref, g_ref, sc_ref, dy_ref, dproj_ref, dsc_ref):
        g = g_ref[...]
        sg = _sigmoid(g)
        silu = g * sg
        dz_f = dz_ref[...]
        ypre_f = y_ref[...]
        dys = dz_f * silu
        dy_ref[...] = (dys * sc_ref[...]).astype(BF16)
        dproj_ref[...] = (dz_f * (ypre_f * sc_ref[...]) * (sg * (1.0 + g * (1.0 - sg)))).astype(BF16)
        _accumulate(dsc_ref, jnp.sum(dys * ypre_f, axis=0, keepdims=True))

    return _rows_call(
        body, name, s_len, [_row_spec(bs, e), _row_spec(bs, e), _row_spec(bs, e, 1), _vec_spec(e)],
        [_row_spec(bs, e), _row_spec(bs, e, 1), _vec_spec(e)],
        [jax.ShapeDtypeStruct((s_len, e), BF16), jax.ShapeDtypeStruct((s_len, 2 * e), BF16),
         jax.ShapeDtypeStruct((1, e), F32)], bs, sequential=True)(dz, ypre, proj, scale)


def _merge_gate_fwd(name, outs, lses, gate, dils):
    s_len, e = gate.shape
    bs = _tile(s_len, 256)
    n = len(outs)

    def body(*refs):
        o_refs, l_refs, g_ref = refs[:n], refs[n:2 * n], refs[2 * n]
        m_ref, lj_ref, z_ref = refs[2 * n + 1:2 * n + 4]
        scratch = refs[2 * n + 4]
        for c, sl in enumerate(_col_blocks(e)):
            ls = [_from_residue_major(r, scratch.at[2 * j, c], dil, sl) for j, (r, dil) in enumerate(zip(l_refs, dils))]
            os_ = [_from_residue_major(r, scratch.at[2 * j + 1, c], dil, sl) for j, (r, dil) in enumerate(zip(o_refs, dils))]
            mx = functools.reduce(jnp.maximum, ls)
            ws = [jnp.exp(l - mx) for l in ls]
            den = functools.reduce(lambda a, b: a + b, ws)
            merged = functools.reduce(lambda a, b: a + b, [w * o for w, o in zip(ws, os_)]) / den
            g = g_ref[:, sl]
            m_ref[:, sl] = merged.astype(BF16)
            lj_ref[:, sl] = mx + jnp.log(den)
            z_ref[:, sl] = (merged * (g * _sigmoid(g))).astype(BF16)

    spec = _row_spec(bs, e)
    res_specs = [_res_spec(dil, bs, e) for dil in dils]
    return pl.pallas_call(
        body, name=name, grid=(s_len // bs,), in_specs=res_specs + res_specs + [spec], out_specs=[spec] * 3,
        out_shape=[jax.ShapeDtypeStruct((s_len, e), BF16), jax.ShapeDtypeStruct((s_len, e), F32),
                   jax.ShapeDtypeStruct((s_len, e), BF16)],
        scratch_shapes=[pltpu.VMEM((2 * n, e // HEAD_DIM, bs, HEAD_DIM), F32)],
        compiler_params=_params(("parallel",)))(*outs, *lses, gate)


def _gate_b_bwd(name, dz, merged, gate, lse, dils):
    s_len, e = gate.shape
    bs = _tile(s_len, 256)
    n = len(dils)

    def body(dz_ref, m_ref, g_ref, l_ref, dg_ref, *rest):
        out_refs, scratch = rest[:3 * n], rest[3 * n]
        for c, sl in enumerate(_col_blocks(e)):
            g = g_ref[:, sl]
            sg = _sigmoid(g)
            dz_f = dz_ref[:, sl]
            merged = m_ref[:, sl].astype(F32)
            dmerged = dz_f * (g * sg)
            dg_ref[:, sl] = (dz_f * merged * (sg * (1.0 + g * (1.0 - sg)))).astype(BF16)
            values = (dmerged, l_ref[:, sl],
                      jnp.broadcast_to(jnp.sum(dmerged * merged, axis=-1, keepdims=True), (bs, HEAD_DIM)))
            for t, val in enumerate(values):
                scr = scratch.at[t, c]
                scr[...] = val
                for j, dil in enumerate(dils):
                    _to_residue_major(out_refs[3 * j + t], scr, dil, sl)

    spec = _row_spec(bs, e)
    out_specs, out_shape = [spec], [jax.ShapeDtypeStruct((s_len, e), BF16)]
    for dil in dils:
        out_specs += [_res_spec(dil, bs, e)] * 3
        out_shape += [_res_shape(dil, s_len, e, BF16), _res_shape(dil, s_len, e, F32), _res_shape(dil, s_len, e, F32)]
    res = pl.pallas_call(
        body, name=name, grid=(s_len // bs,), in_specs=[spec] * 4, out_specs=out_specs, out_shape=out_shape,
        scratch_shapes=[pltpu.VMEM((3, e // HEAD_DIM, bs, HEAD_DIM), F32)],
        compiler_params=_params(("parallel",)))(dz, merged, gate, lse)
    return res[0], [tuple(res[1 + 3 * j:4 + 3 * j]) for j in range(n)]


def _kv_grad_prep(name, dk_accs, dv_accs, dils, cos, sin_inv):
    n = len(dils)
    e = dk_accs[0].shape[-1]
    s_len = dk_accs[0].shape[0] * dk_accs[0].shape[1]
    bs = _tile(s_len, 256)

    def body(*refs):
        dk_refs, dv_refs = refs[:n], refs[n:2 * n]
        c_ref, s_ref, dkb_ref, dvb_ref, scratch = refs[2 * n:]
        cos_t, sin_t = c_ref[...], s_ref[...]
        add = lambda a, b: a + b
        for c, sl in enumerate(_col_blocks(e)):
            dk = functools.reduce(add, [_from_residue_major(r, scratch.at[j, c], dil, sl)
                                        for j, (r, dil) in enumerate(zip(dk_refs, dils))])
            dkb_ref[:, sl] = _rope_apply(dk, cos_t, sin_t).astype(BF16)
            dv = functools.reduce(add, [_from_residue_major(r, scratch.at[n + j, c], dil, sl)
                                        for j, (r, dil) in enumerate(zip(dv_refs, dils))])
            dvb_ref[:, sl] = dv.astype(BF16)

    spec, rspec = _row_spec(bs, e), _row_spec(bs, HEAD_DIM)
    res_specs = [_res_spec(dil, bs, e) for dil in dils]
    return pl.pallas_call(
        body, name=name, grid=(s_len // bs,), in_specs=res_specs + res_specs + [rspec, rspec], out_specs=[spec, spec],
        out_shape=[jax.ShapeDtypeStruct((s_len, e), BF16)] * 2,
        scratch_shapes=[pltpu.VMEM((2 * n, e // HEAD_DIM, bs, HEAD_DIM), F32)],
        compiler_params=_params(("parallel",)))(*dk_accs, *dv_accs, cos, sin_inv)


def _pool_cols(e):
    return _tile(e // len(POOL_WINDOWS), 256)


def _window_sum(val, grp, s_len, forward):
    rows = lax.broadcasted_iota(jnp.int32, val.shape, 0)
    acc = val
    for level in range(len(POOL_WINDOWS)):
        step = 1 << level
        if forward:
            shifted = jnp.where(rows >= step, pltpu.roll(acc, step, 0), 0.0)
        else:
            shifted = jnp.where(rows < s_len - step, pltpu.roll(acc, s_len - step, 0), 0.0)
        acc = jnp.where(level <= grp, acc + shifted, acc)
    return acc


def _window_count(shape, grp):
    rows = lax.broadcasted_iota(jnp.int32, shape, 0)
    return jnp.minimum(rows + 1, jnp.left_shift(2, grp)).astype(F32)


def _pool_fwd(name, proj):
    s_len, e2 = proj.shape
    e = e2 // 2
    cb = _pool_cols(e)
    per_grp = e // len(POOL_WINDOWS) // cb
    assert POOL_WINDOWS == tuple(2 << g for g in range(len(POOL_WINDOWS)))

    def body(u_ref, p_ref):
        grp = pl.program_id(0)
        u = u_ref[...]
        total = _window_sum(u, grp, s_len, True)
        p_ref[...] = (total / _window_count(u.shape, grp) - u).astype(BF16)

    spec = pl.BlockSpec((s_len, cb), lambda g, c: (0, g * per_grp + c))
    return pl.pallas_call(
        body, name=name, grid=(len(POOL_WINDOWS), per_grp), in_specs=[spec], out_specs=spec,
        out_shape=jax.ShapeDtypeStruct((s_len, e), BF16), compiler_params=_params(("parallel", "parallel")))(proj)


def _pool_bwd(name, dpooled, dproj):
    s_len, e = dpooled.shape
    cb = _pool_cols(e)
    per_grp = e // len(POOL_WINDOWS) // cb

    def body(dp_ref, _, du_ref):
        grp = pl.program_id(0)
        dp = dp_ref[...]
        total = _window_sum(dp / _window_count(dp.shape, grp), grp, s_len, False)
        du_ref[...] = (total - dp).astype(BF16)

    spec = pl.BlockSpec((s_len, cb), lambda g, c: (0, g * per_grp + c))
    return pl.pallas_call(
        body, name=name, grid=(len(POOL_WINDOWS), per_grp), in_specs=[spec, ANY], out_specs=spec,
        out_shape=jax.ShapeDtypeStruct(dproj.shape, BF16), input_output_aliases={1: 0},
        compiler_params=_params(("parallel", "parallel")))(dpooled, dproj)


def _band_masks(nb, first):
    row = lax.broadcasted_iota(jnp.int32, (nb, nb), 0)
    col = lax.broadcasted_iota(jnp.int32, (nb, nb), 1)
    return col >= row + jnp.where(first, 2 * nb, 0), col <= row


def _dot(a, b, dims):
    return lax.dot_general(a, b, (dims, ((), ())), preferred_element_type=F32)


def _attn_fwd(name, window, q, k, v):
    dil, m, e = k.shape
    nb = window // dil
    nblk = m // nb
    heads = e // HEAD_DIM

    def body(q_ref, kc_ref, vc_ref, o_ref, l_ref, kp_ref, vp_ref):
        first = pl.program_id(1) == 0

        @pl.when(first)
        def _():
            kp_ref[...] = jnp.zeros_like(kp_ref)
            vp_ref[...] = jnp.zeros_like(vp_ref)

        mask_p, mask_c = _band_masks(nb, first)
        cols = _col_blocks(e)
        s_p = [jnp.where(mask_p, _dot(q_ref[:, sl], kp_ref[:, sl], NT), NEG_INF) for sl in cols]
        s_c = [jnp.where(mask_c, _dot(q_ref[:, sl], kc_ref[:, sl], NT), NEG_INF) for sl in cols]
        mx = [jnp.maximum(jnp.max(a, axis=-1, keepdims=True), jnp.max(b, axis=-1, keepdims=True))
              for a, b in zip(s_p, s_c)]
        p_p = [jnp.exp(a - m) for a, m in zip(s_p, mx)]
        p_c = [jnp.exp(a - m) for a, m in zip(s_c, mx)]
        den = [jnp.sum(a, axis=-1, keepdims=True) + jnp.sum(b, axis=-1, keepdims=True) for a, b in zip(p_p, p_c)]
        for h, sl in enumerate(cols):
            out = _dot(p_p[h].astype(BF16), vp_ref[:, sl], NN) + _dot(p_c[h].astype(BF16), vc_ref[:, sl], NN)
            o_ref[:, sl] = (out / den[h]).astype(BF16)
            l_ref[:, sl] = jnp.broadcast_to(mx[h] + jnp.log(den[h]), (nb, HEAD_DIM))
        kp_ref[...] = kc_ref[...]
        vp_ref[...] = vc_ref[...]

    blk = (None, nb, e)
    cur = lambda r, n: (r, n, 0)
    return pl.pallas_call(
        body, name=name, grid=(dil, nblk),
        in_specs=[pl.BlockSpec(blk, cur)] * 3,
        out_specs=[pl.BlockSpec(blk, cur), pl.BlockSpec(blk, cur)],
        out_shape=[jax.ShapeDtypeStruct((dil, m, e), BF16), jax.ShapeDtypeStruct((dil, m, e), F32)],
        scratch_shapes=[pltpu.VMEM((nb, e), BF16), pltpu.VMEM((nb, e), BF16)],
        compiler_params=_params(("parallel", "arbitrary")),
    )(q, k, v)


def _attn_bwd(name, window, scale, q, k, v, dout, lse, delta, cos, sin_inv, dk_acc, dv_acc):
    dil, m, e = k.shape
    nb = window // dil
    nblk = m // nb
    heads = e // HEAD_DIM

    accumulate = dk_acc is not None

    def body(k_ref, v_ref, q0_ref, qn_ref, do0_ref, don_ref, l0_ref, ln_ref, dl0_ref, dln_ref, c_ref, s_ref, *rest):
        if accumulate:
            dki_ref, dvi_ref = rest[:2]
            rest = rest[2:]
        dq_ref, dko_ref, dvo_ref, carry_ref, qc_ref, doc_ref, lc_ref, dlc_ref = rest
        n = pl.program_id(1)

        @pl.when(n == 0)
        def _():
            carry_ref[...] = jnp.zeros_like(carry_ref)
            qc_ref[...] = q0_ref[...]
            doc_ref[...] = do0_ref[...]
            lc_ref[...] = l0_ref[...]
            dlc_ref[...] = dl0_ref[...]

        mask_n, mask_c = _band_masks(nb, n == nblk - 1)
        cos_t, sin_t = c_ref[...], s_ref[...]
        cols = _col_blocks(e)
        stat = lambda ref, sl: ref[:, sl] if nb == HEAD_DIM else ref[:, sl][:, :1]
        s_c = [_dot(qc_ref[:, sl], k_ref[:, sl], NT) for sl in cols]
        s_n = [_dot(qn_ref[:, sl], k_ref[:, sl], NT) for sl in cols]
        dp_c = [_dot(doc_ref[:, sl], v_ref[:, sl], NT) for sl in cols]
        dp_n = [_dot(don_ref[:, sl], v_ref[:, sl], NT) for sl in cols]
        p_c = [jnp.where(mask_c, jnp.exp(s - stat(lc_ref, sl)), 0.0) for s, sl in zip(s_c, cols)]
        p_n = [jnp.where(mask_n, jnp.exp(s - stat(ln_ref, sl)), 0.0) for s, sl in zip(s_n, cols)]
        ds_c = [(p * (dp - stat(dlc_ref, sl))).astype(BF16) for p, dp, sl in zip(p_c, dp_c, cols)]
        ds_n = [(p * (dp - stat(dln_ref, sl))).astype(BF16) for p, dp, sl in zip(p_n, dp_n, cols)]
        for h, sl in enumerate(cols):
            dq = (carry_ref[:, sl] + _dot(ds_c[h], k_ref[:, sl], NN)) * scale
            dq_ref[:, sl] = _rope_apply(dq, cos_t, sin_t).astype(BF16)
        for h, sl in enumerate(cols):
            carry_ref[:, sl] = _dot(ds_n[h], k_ref[:, sl], NN)
        for h, sl in enumerate(cols):
            dk = _dot(ds_c[h], qc_ref[:, sl], TN) + _dot(ds_n[h], qn_ref[:, sl], TN)
            dv = _dot(p_c[h].astype(BF16), doc_ref[:, sl], TN) + _dot(p_n[h].astype(BF16), don_ref[:, sl], TN)
            dko_ref[:, sl] = dki_ref[:, sl] + dk if accumulate else dk
            dvo_ref[:, sl] = dvi_ref[:, sl] + dv if accumulate else dv
        qc_ref[...] = qn_ref[...]
        doc_ref[...] = don_ref[...]
        lc_ref[...] = ln_ref[...]
        dlc_ref[...] = dln_ref[...]

    blk = (None, nb, e)
    cur = lambda r, n: (r, n, 0)
    nxt = lambda r, n: (r, jnp.minimum(n + 1, nblk - 1), 0)
    first = lambda r, n: (r, 0, 0)
    rblk = (None, nb, HEAD_DIM)
    both = lambda shape: [pl.BlockSpec(shape, first), pl.BlockSpec(shape, nxt)]
    accs = (dk_acc, dv_acc) if accumulate else ()
    return pl.pallas_call(
        body, name=name, grid=(dil, nblk),
        in_specs=[pl.BlockSpec(blk, cur), pl.BlockSpec(blk, cur), *both(blk), *both(blk), *both(blk), *both(blk),
                  pl.BlockSpec(rblk, cur), pl.BlockSpec(rblk, cur)] + [pl.BlockSpec(blk, cur)] * len(accs),
        out_specs=[pl.BlockSpec(blk, cur)] * 3,
        out_shape=[jax.ShapeDtypeStruct((dil, m, e), BF16),
                   jax.ShapeDtypeStruct((dil, m, e), F32), jax.ShapeDtypeStruct((dil, m, e), F32)],
        scratch_shapes=[pltpu.VMEM((nb, e), F32), pltpu.VMEM((nb, e), BF16), pltpu.VMEM((nb, e), BF16),
                        pltpu.VMEM((nb, e), F32), pltpu.VMEM((nb, e), F32)],
        input_output_aliases={12: 1, 13: 2} if accumulate else {},
        compiler_params=_params(("parallel", "arbitrary")),
    )(k, v, q, q, dout, dout, lse, lse, delta, delta, cos, sin_inv, *accs)


def _rope_tables(s_len):
    inv_freq = 1.0 / (ROPE_THETA ** (jnp.arange(0, HEAD_DIM, 2, dtype=F32) / HEAD_DIM))
    ang = jnp.arange(s_len, dtype=F32)[:, None] * inv_freq[None, :]
    cos, sin = jnp.cos(ang), jnp.sin(ang)
    return jnp.concatenate([cos, cos], axis=1), jnp.concatenate([-sin, sin], axis=1)


def _row(vec):
    return vec.reshape(1, -1)


def _local_step(x, target, n_a, n_b, fetch, emit):
    s_len, d = x.shape
    n_q = len(DILATED_PAIRS)
    cos, sin = _rope_tables(s_len)
    sin_inv = -sin
    q_scale = 1.0 / math.sqrt(HEAD_DIM)
    w = {}

    def need(group, after):
        for name, (layer, arr) in fetch(group, after).items():
            w.setdefault(name, {})[layer] = arr

    saved_a = []
    for i in range(n_a):
        need(f"a{i}", x)
        h = _rmsnorm_fwd(f"a{i}_norm", x, _row(w["norm_a"][i]))
        proj = _mm_act_w(f"a{i}_in", h, w["w_in_a"][i], out_dtype=F32)
        pooled = _pool_fwd(f"a{i}_pool", proj)
        ypre = _mm_grp_fwd(f"a{i}_grp", pooled, w["w_grp_a"][i])
        z = _gate_a_fwd(f"a{i}_gate", ypre, proj, _row(w["scale_a"][i]))
        x_next = _mm_act_w(f"a{i}_out", z, w["w_out_a"][i], out_dtype=F32, add=x)
        saved_a.append((x, h, proj, pooled, ypre, z))
        x = x_next

    x_kv = x
    need("kv", x)
    e = w["w_k"][0].shape[1]
    kv_in = _rmsnorm_fwd("kv_norm", x, _row(w["norm_kv"][0]))
    windows = [window for window, _ in DILATED_PAIRS]
    dils = tuple(dil for _, dil in DILATED_PAIRS)
    far_dils = tuple(dil for dil in dils if dil > 1)
    ks = _mm_act_w("kv_k", kv_in, w["w_k"][0], rope=(cos, sin, 1.0), dils=dils)
    vs = _mm_act_w("kv_v", kv_in, w["w_v"][0], dils=dils)

    saved_b = []
    for i in range(n_b):
        need(f"b{i}", x if i > 0 else vs[0])
        hs = _rmsnorm_fwd(f"b{i}_norm", x, _row(w["norm_b"][i]), dils=far_dils)
        hs = {1: hs[0], **{dil: h_d.reshape(s_len, d) for dil, h_d in zip(far_dils, hs[1:])}}
        qs = [_mm_act_w(f"b{i}_q{g}", hs[1], w["w_in_b"][i], rope=(cos, sin, q_scale), n_first=g, n_cols=1,
                        dils=(dil,))[0] for g, dil in enumerate(dils)]
        gate = _mm_act_w(f"b{i}_g", hs[1], w["w_in_b"][i], out_dtype=F32, n_first=n_q, n_cols=1)
        outs, lses = [], []
        for g in range(n_q):
            o_g, l_g = _attn_fwd(f"b{i}_attn{g}", windows[g], qs[g], ks[g], vs[g])
            outs.append(o_g)
            lses.append(l_g)
        merged, lse, z = _merge_gate_fwd(f"b{i}_merge", outs, lses, gate, dils)
        x_next = _mm_act_w(f"b{i}_out", z, w["w_out_b"][i], out_dtype=F32, add=x)
        saved_b.append((x, hs, qs, gate, merged, lse, z))
        x = x_next

    need("head", x)
    loss_vec, dx, dxb, g_norm_f = _loss_head("loss_head", x, _row(w["norm_f"][0]), target)

    small = {"norm_a": {}, "scale_a": {}, "norm_kv": {}, "norm_b": {}, "norm_f": {0: g_norm_f}}
    shard_rows = lambda g2: g2.reshape(N_CHIPS, g2.shape[0] // N_CHIPS, g2.shape[1])

    res_major = lambda t, dil: t.reshape(s_len // dil, dil, t.shape[1]).transpose(1, 0, 2)
    cos_r = [res_major(cos, dil) for dil in dils]
    sin_inv_r = [res_major(sin_inv, dil) for dil in dils]
    dk_accs = [None] * len(dils)
    dv_accs = [None] * len(dils)
    for i in reversed(range(n_b)):
        x_in, hs, qs, gate, merged, lse, z = saved_b[i]
        dz = _mm_grad_act(f"b{i}_dz", dxb, w["w_out_b"][i])
        g_out = shard_rows(_mm_grad_w(f"b{i}_gwo", z, dxb))
        dgate, stats = _gate_b_bwd(f"b{i}_dgate", dz, merged, gate, lse, dils)
        dh = _mm_grad_act(f"b{i}_dh{n_q}", dgate, w["w_in_b"][i], slot=n_q)
        g_in = _mm_grad_w(f"b{i}_gwi{n_q}", hs[1], dgate, col_shards=n_q + 1, slot=n_q)
        for g, dil in enumerate(dils):
            dout, lse_g, delta_g = stats[g]
            dq, dk_accs[g], dv_accs[g] = _attn_bwd(f"b{i}_dattn{g}", windows[g], q_scale, qs[g], ks[g], vs[g], dout,
                                                   lse_g, delta_g, cos_r[g], sin_inv_r[g], dk_accs[g], dv_accs[g])
            dh = _mm_grad_act(f"b{i}_dh{g}", dq if dil > 1 else dq[0], w["w_in_b"][i], add=dh, slot=g)
            g_in = _mm_grad_w(f"b{i}_gwi{g}", hs[dil], dq.reshape(s_len, e), col_shards=n_q + 1, slot=g, into=g_in)
        dx, dxb, small["norm_b"][i] = _rmsnorm_bwd(f"b{i}_dnorm", x_in, _row(w["norm_b"][i]), dh, dx)
        dx, dxb = emit(f"b{i}", {"w_in_b": (i, g_in), "w_out_b": (i, g_out)}, (dx, dxb))

    dkb, dvb = _kv_grad_prep("kv_dprep", dk_accs, dv_accs, dils, cos, sin_inv)
    dkv = _mm_grad_act("kv_dk", dkb, w["w_k"][0])
    dkv = _mm_grad_act("kv_dv", dvb, w["w_v"][0], add=dkv)
    g_k = shard_rows(_mm_grad_w("kv_gwk", kv_in, dkb))
    g_v = shard_rows(_mm_grad_w("kv_gwv", kv_in, dvb))
    dx, dxb, small["norm_kv"][0] = _rmsnorm_bwd("kv_dnorm", x_kv, _row(w["norm_kv"][0]), dkv, dx)
    dx, dxb = emit("kv", {"w_k": (0, g_k), "w_v": (0, g_v)}, (dx, dxb))

    for i in reversed(range(n_a)):
        x_in, h, proj, pooled, ypre, z = saved_a[i]
        dz = _mm_grad_act(f"a{i}_dz", dxb, w["w_out_a"][i])
        g_out = shard_rows(_mm_grad_w(f"a{i}_gwo", z, dxb))
        dypre, dproj, small["scale_a"][i] = _gate_a_bwd(f"a{i}_dgate", dz, ypre, proj, _row(w["scale_a"][i]))
        dpooled = _mm_grp_grad_act(f"a{i}_dgrp", dypre, w["w_grp_a"][i])
        g_grp = _mm_grp_grad_w(f"a{i}_gwg", pooled, dypre, len(POOL_WINDOWS))
        g_grp = g_grp.reshape(N_CHIPS, -1, g_grp.shape[-1])
        last = i == 0
        if last:
            (dpooled,) = emit(f"a{i}", {"w_grp_a": (i, g_grp), "w_out_a": (i, g_out)}, (dpooled,))
        dproj = _pool_bwd(f"a{i}_dpool", dpooled, dproj)
        g_in = _mm_grad_w(f"a{i}_gwi", h, dproj, col_shards=N_CHIPS)
        if last:
            (dproj,) = emit(f"a{i}i", {"w_in_a": (i, g_in)}, (dproj,))
        dh = _mm_grad_act(f"a{i}_dh", dproj, w["w_in_a"][i])
        dx, dxb, small["norm_a"][i] = _rmsnorm_bwd(f"a{i}_dnorm", x_in, _row(w["norm_a"][i]), dh, dx)
        if not last:
            dx, dxb = emit(f"a{i}", {"w_in_a": (i, g_in), "w_grp_a": (i, g_grp), "w_out_a": (i, g_out)}, (dx, dxb))

    return loss_vec, dx, small


BIG_WEIGHTS = ("w_in_a", "w_grp_a", "w_out_a", "w_k", "w_v", "w_in_b", "w_out_b")


def _pair_add(name, grad, recv, c_idx):
    _, r, cols = grad.shape
    half = r // 2
    rb = _tile(half, 256)
    nrb = half // rb

    def body(c_ref, g_ref, r_ref, o_ref):
        o_ref[...] = (g_ref[...].astype(F32) + r_ref[...].astype(F32)).astype(BF16)

    blk = (None, rb, cols)
    grid_spec = pltpu.PrefetchScalarGridSpec(
        num_scalar_prefetch=1, grid=(N_CHIPS, nrb),
        in_specs=[pl.BlockSpec(blk, lambda s, i, c: (s, c[0] * nrb + i, 0)), pl.BlockSpec(blk, lambda s, i, c: (s, i, 0))],
        out_specs=pl.BlockSpec(blk, lambda s, i, c: (s, i, 0)))
    return pl.pallas_call(body, name=name, grid_spec=grid_spec,
                          out_shape=jax.ShapeDtypeStruct((N_CHIPS, half, cols), BF16),
                          compiler_params=_params(("parallel", "parallel")))(c_idx, grad, recv)


def _final_add(name, part, recv, sc_idx, layer, n_layers, into=None):
    _, half, cols = part.shape
    rb = _tile(half, 256)
    nrb = half // rb
    n_peer = recv.shape[0]

    def body(sc_ref, p_ref, *refs):
        acc = p_ref[...].astype(F32)
        for r_ref in refs[:n_peer]:
            acc = acc + r_ref[...].astype(F32)
        refs[-1][...] = acc

    blk = (None, rb, cols)
    peer_spec = lambda k: pl.BlockSpec(blk, lambda i, sc: (k, i, 0))
    grid_spec = pltpu.PrefetchScalarGridSpec(
        num_scalar_prefetch=1, grid=(nrb,),
        in_specs=[pl.BlockSpec(blk, lambda i, sc: (sc[0], i, 0))] + [peer_spec(k) for k in range(n_peer)]
                 + ([] if into is None else [ANY]),
        out_specs=pl.BlockSpec(blk, lambda i, sc: (layer, sc[1] * nrb + i, 0)))
    extra = () if into is None else (into,)
    return pl.pallas_call(body, name=name, grid_spec=grid_spec,
                          out_shape=jax.ShapeDtypeStruct((n_layers, 2 * half, cols), F32),
                          input_output_aliases={} if into is None else {2 + n_peer: 0},
                          compiler_params=_params(("parallel",)))(sc_idx, part, *([recv] * n_peer), *extra)


def _cast_into_slot(name, arr, layer, s_idx):
    _, b, r, cols = arr.shape
    rb = _tile(r, 512)

    def body(s_ref, a_ref, o_ref):
        o_ref[...] = a_ref[...].astype(BF16)

    blk = (None, None, rb, cols)
    grid_spec = pltpu.PrefetchScalarGridSpec(
        num_scalar_prefetch=1, grid=(b, r // rb),
        in_specs=[pl.BlockSpec(blk, lambda j, i, s: (layer, j, i, 0))],
        out_specs=pl.BlockSpec(blk, lambda j, i, s: (j, s[0], i, 0)))
    return pl.pallas_call(body, name=name, grid_spec=grid_spec,
                          out_shape=jax.ShapeDtypeStruct((b, N_CHIPS, r, cols), BF16),
                          compiler_params=_params(("parallel", "parallel")))(s_idx, arr)


def _sum_devices(name, gathered):
    n_dev, p, d = gathered.shape

    def body(g_ref, o_ref):
        acc = g_ref[0]
        for j in range(1, n_dev):
            acc = acc + g_ref[j]
        o_ref[...] = acc

    return pl.pallas_call(body, name=name, out_shape=jax.ShapeDtypeStruct((p, d), F32),
                          compiler_params=_params())(gathered)


def _adamw(name, w, g, m, v):
    shape = w.shape
    cols = shape[-1]
    flat = lambda a: a.reshape(-1, cols)
    rows = flat(w).shape[0]
    bs = _tile(rows, 256)

    def body(w_ref, g_ref, m_ref, v_ref, d_ref, mo_ref, vo_ref):
        grad = g_ref[...]
        m_new = ADAM_B1 * m_ref[...] + (1.0 - ADAM_B1) * grad
        v_new = ADAM_B2 * v_ref[...] + (1.0 - ADAM_B2) * (grad * grad)
        m_hat = m_new / (1.0 - ADAM_B1 ** ADAM_STEP)
        v_hat = v_new / (1.0 - ADAM_B2 ** ADAM_STEP)
        d_ref[...] = -ADAM_LR * (m_hat / (jnp.sqrt(v_hat) + ADAM_EPS) + ADAM_WD * w_ref[...])
        mo_ref[...] = m_new
        vo_ref[...] = v_new

    spec = _row_spec(bs, cols)
    outs = _rows_call(body, name, rows, [spec] * 4, [spec] * 3, [jax.ShapeDtypeStruct((rows, cols), F32)] * 3, bs)(
        flat(w), flat(g), flat(m), flat(v))
    return tuple(o.reshape(shape) for o in outs)


def _place():
    x, y, c = lax.axis_index("x"), lax.axis_index("y"), lax.axis_index("c")
    chips = [(1 - x, y), (x, 1 - y), (1 - x, 1 - y)]
    return x, y, c, chips


def _chip_index(chip):
    return 2 * chip[0] + chip[1]


def _comm_call(body, name, n_in, out_shape, scratch, aliases=None):
    return pl.pallas_call(body, name=name, in_specs=[ANY] * n_in, out_specs=[ANY] * len(out_shape), out_shape=out_shape,
                          scratch_shapes=scratch, input_output_aliases=aliases or {})


HBM_SPEC = pl.BlockSpec(memory_space=pltpu.HBM)
SEM_SPEC = pl.BlockSpec(memory_space=pltpu.SEMAPHORE)
SPLIT_PARAMS = pltpu.CompilerParams(has_side_effects=pltpu.SideEffectType.DATAFLOW_SIDE_EFFECTING)


def _in_hbm(arr):
    return pltpu.with_memory_space_constraint(arr, pltpu.HBM)


def _slot_half(ref, chip, core):
    half = ref.shape[2] // 2
    return ref.at[:, _chip_index(chip), pl.ds(core * half, half), :]


def _gather_start(name, bufs, carry=()):
    n, n_c = len(bufs), len(carry)

    def body(*refs):
        ins, (send_sems, recv_sems) = refs[:n], refs[n + n_c:n + n_c + 2]
        x, y, c, chips = _place()
        for a in range(n):
            block = _slot_half(ins[a], (x, y), c)
            for k, chip in enumerate(chips):
                pltpu.make_async_remote_copy(src_ref=block, dst_ref=block, send_sem=send_sems.at[3 * a + k],
                                             recv_sem=recv_sems.at[3 * a + k], device_id=(*chip, c),
                                             device_id_type=MESH).start()

    dma = pltpu.SemaphoreType.DMA
    thru = list(bufs) + list(carry)
    res = pl.pallas_call(
        body, name=name, in_specs=[HBM_SPEC] * (n + n_c), out_specs=[SEM_SPEC] * 2 + [HBM_SPEC] * (n + n_c),
        out_shape=[dma((3 * n,)), dma((3 * n,))] + [pltpu.HBM(a.shape, a.dtype) for a in thru],
        input_output_aliases={t: 2 + t for t in range(n + n_c)}, compiler_params=SPLIT_PARAMS,
    )(*[_in_hbm(a) for a in thru])
    return (res[0], res[1]), list(res[2:2 + n]), list(res[2 + n:])


def _gather_wait(name, sems, bufs, after):
    n = len(bufs)

    def body(*refs):
        ins, (send_sems, recv_sems) = refs[:n], refs[n:n + 2]
        x, y, c, chips = _place()
        for a in range(n):
            for k, chip in enumerate(chips):
                mine, theirs = _slot_half(ins[a], (x, y), c), _slot_half(ins[a], chip, c)
                copy = pltpu.make_async_remote_copy(src_ref=mine, dst_ref=theirs, send_sem=send_sems.at[3 * a + k],
                                                    recv_sem=recv_sems.at[3 * a + k], device_id=(*chip, c),
                                                    device_id_type=MESH)
                copy.wait_send()
                copy.wait_recv()

    res = pl.pallas_call(
        body, name=name, in_specs=[HBM_SPEC] * n + [SEM_SPEC, SEM_SPEC, ANY], out_specs=[HBM_SPEC] * n,
        out_shape=[pltpu.HBM(b.shape, b.dtype) for b in bufs], input_output_aliases={a: a for a in range(n)},
        compiler_params=SPLIT_PARAMS)(*bufs, *sems, after)
    return list(res)


def _gather_forward(name, bufs, smalls=()):
    n, n_small = len(bufs), len(smalls)

    def body(*refs):
        small_in = refs[n:n + n_small]
        outs = refs[n + n_small:2 * n + n_small]
        small_out = refs[2 * n + n_small:2 * n + 2 * n_small]
        send_sems, recv_sems, s_send, s_recv, s_local = refs[-5:]
        x, y, c, chips = _place()
        me, sibling = _chip_index((x, y)), (x, y, 1 - c)

        def forward(t, k, core):
            block = _slot_half(outs[t], chips[k], core)
            return pltpu.make_async_remote_copy(src_ref=block, dst_ref=block, send_sem=send_sems.at[t, k],
                                                recv_sem=recv_sems.at[t, k], device_id=sibling, device_id_type=MESH)

        def small_copy(j, k, slot):
            return pltpu.make_async_remote_copy(src_ref=small_in[j], dst_ref=small_out[j].at[slot],
                                                send_sem=s_send.at[j, k], recv_sem=s_recv.at[j, k],
                                                device_id=(*chips[k], c), device_id_type=MESH)

        local = []
        for t in range(n):
            for k in range(3):
                forward(t, k, c).start()
        for j in range(n_small):
            own = pltpu.make_async_copy(small_in[j], small_out[j].at[me], s_local.at[j])
            own.start()
            local.append(own)
            for k in range(3):
                small_copy(j, k, me).start()
        for t in range(n):
            for k in range(3):
                forward(t, k, 1 - c).wait_recv()
        for j in range(n_small):
            for k in range(3):
                small_copy(j, k, _chip_index(chips[k])).wait_recv()
        for t in range(n):
            for k in range(3):
                forward(t, k, c).wait_send()
        for j in range(n_small):
            for k in range(3):
                small_copy(j, k, me).wait_send()
        for own in local:
            own.wait()

    out_shape = [jax.ShapeDtypeStruct(b.shape, BF16) for b in bufs]
    out_shape += [jax.ShapeDtypeStruct((N_CHIPS,) + s.shape, F32) for s in smalls]
    dma = pltpu.SemaphoreType.DMA
    n_s = max(n_small, 1)
    res = _comm_call(body, name, n + n_small, out_shape,
                     [dma((n, 3)), dma((n, 3)), dma((n_s, 3)), dma((n_s, 3)), dma((n_s,))],
                     aliases={t: t for t in range(n)})(*bufs, *smalls)
    return list(res[:n]), list(res[n:])


def _exchange_halves(name, grads):
    n = len(grads)

    def body(*refs):
        g_in, outs = refs[:n], refs[n:2 * n]
        send_sems, recv_sems = refs[-2:]
        x, y, c, _ = _place()
        copies = []
        for t in range(n):
            half = g_in[t].shape[1] // 2
            cp = pltpu.make_async_remote_copy(
                src_ref=g_in[t].at[:, pl.ds((1 - c) * half, half), :], dst_ref=outs[t], send_sem=send_sems.at[t],
                recv_sem=recv_sems.at[t], device_id=(x, y, 1 - c), device_id_type=MESH)
            cp.start()
            copies.append(cp)
        for cp in copies:
            cp.wait()

    out_shape = [jax.ShapeDtypeStruct((g.shape[0], g.shape[1] // 2, g.shape[2]), BF16) for g in grads]
    dma = pltpu.SemaphoreType.DMA
    return list(_comm_call(body, name, n, out_shape, [dma((n,)), dma((n,))])(*grads))


def _scatter_copy(part_ref, land_ref, send_sems, recv_sems, t, k, chip, c):
    return pltpu.make_async_remote_copy(
        src_ref=part_ref.at[_chip_index(chip)], dst_ref=land_ref.at[k], send_sem=send_sems.at[3 * t + k],
        recv_sem=recv_sems.at[3 * t + k], device_id=(*chip, c), device_id_type=MESH)


def _scatter_start(name, parts, carry=()):
    n, n_c = len(parts), len(carry)
    lands = [lax.empty((3,) + p.shape[1:], BF16) for p in parts]

    def body(*refs):
        p_in, l_in = refs[:n], refs[n:2 * n]
        send_sems, recv_sems = refs[2 * n + n_c:2 * n + n_c + 2]
        x, y, c, chips = _place()
        for t in range(n):
            for k, chip in enumerate(chips):
                _scatter_copy(p_in[t], l_in[t], send_sems, recv_sems, t, k, chip, c).start()

    dma = pltpu.SemaphoreType.DMA
    thru = list(parts) + lands + list(carry)
    res = pl.pallas_call(
        body, name=name, in_specs=[HBM_SPEC] * len(thru), out_specs=[SEM_SPEC] * 2 + [HBM_SPEC] * len(thru),
        out_shape=[dma((3 * n,)), dma((3 * n,))] + [pltpu.HBM(a.shape, a.dtype) for a in thru],
        input_output_aliases={t: 2 + t for t in range(len(thru))}, compiler_params=SPLIT_PARAMS,
    )(*[_in_hbm(a) for a in thru])
    return (res[0], res[1]), list(res[2:2 + n]), list(res[2 + n:2 + 2 * n]), list(res[2 + 2 * n:])


def _scatter_wait(name, sems, parts, lands, after):
    n = len(parts)

    def body(*refs):
        p_in, l_in = refs[:n], refs[n:2 * n]
        send_sems, recv_sems = refs[2 * n:2 * n + 2]
        x, y, c, chips = _place()
        for t in range(n):
            for k, chip in enumerate(chips):
                copy = _scatter_copy(p_in[t], l_in[t], send_sems, recv_sems, t, k, chip, c)
                copy.wait_send()
                copy.wait_recv()

    hbm_out = lambda a: pltpu.HBM(a.shape, a.dtype)
    res = pl.pallas_call(
        body, name=name, in_specs=[HBM_SPEC] * (2 * n) + [SEM_SPEC, SEM_SPEC, ANY], out_specs=[HBM_SPEC] * (2 * n),
        out_shape=[hbm_out(a) for a in parts + lands], input_output_aliases={t: t for t in range(2 * n)},
        compiler_params=SPLIT_PARAMS)(*parts, *lands, *sems, after)
    return list(res[:n]), list(res[n:])


def _share_halves(fulls):
    n = len(fulls)
    items = [(a, l) for a in range(n) for l in range(fulls[a].shape[0])]

    def body(*refs):
        outs = refs[n:2 * n]
        send_sems, recv_sems = refs[-2:]
        x, y, c, _ = _place()

        def copy(t, core):
            a, l = items[t]
            half = outs[a].shape[1] // 2
            block = outs[a].at[l, pl.ds(core * half, half), :]
            return pltpu.make_async_remote_copy(src_ref=block, dst_ref=block, send_sem=send_sems.at[t],
                                                recv_sem=recv_sems.at[t], device_id=(x, y, 1 - c), device_id_type=MESH)

        for t in range(len(items)):
            copy(t, c).start()
        for t in range(len(items)):
            copy(t, 1 - c).wait_recv()
        for t in range(len(items)):
            copy(t, c).wait_send()

    out_shape = [jax.ShapeDtypeStruct(f.shape, F32) for f in fulls]
    dma = pltpu.SemaphoreType.DMA
    return list(_comm_call(body, "grad_share_halves", n, out_shape, [dma((len(items),)), dma((len(items),))],
                           aliases={a: a for a in range(n)})(*fulls))


def _allgather_small(packed):
    def body(p_ref, o_ref, send_sems, recv_sems, local_sem):
        x, y, c, _ = _place()
        me = 4 * x + 2 * y + c
        own = pltpu.make_async_copy(p_ref, o_ref.at[me], local_sem)
        own.start()
        flips = [(fx, fy, fc) for fx in (0, 1) for fy in (0, 1) for fc in (0, 1)][1:]
        peers = [(x ^ fx, y ^ fy, c ^ fc) for fx, fy, fc in flips]
        copies = []
        for k, peer in enumerate(peers):
            cp = pltpu.make_async_remote_copy(src_ref=p_ref, dst_ref=o_ref.at[me], send_sem=send_sems.at[k],
                                              recv_sem=recv_sems.at[k], device_id=peer, device_id_type=MESH)
            cp.start()
            copies.append(cp)
        for k, (px, py, pc) in enumerate(peers):
            pltpu.make_async_remote_copy(src_ref=p_ref, dst_ref=o_ref.at[4 * px + 2 * py + pc], send_sem=send_sems.at[k],
                                         recv_sem=recv_sems.at[k], device_id=peers[k], device_id_type=MESH).wait_recv()
        for cp in copies:
            cp.wait_send()
        own.wait()

    dma = pltpu.SemaphoreType.DMA
    return _comm_call(body, "small_allgather", 1, [jax.ShapeDtypeStruct((8,) + packed.shape, F32)],
                      [dma((7,)), dma((7,)), dma(())])(packed)[0]


PAD_ROWS = 8


def kernel(x, norm_a, w_in_a, w_grp_a, scale_a, w_out_a, norm_kv, w_k, w_v, norm_b, w_in_b, w_out_b, norm_f, loss_target, m_norm_a, m_w_in_a, m_w_grp_a, m_scale_a, m_w_out_a, m_norm_kv, m_w_k, m_w_v, m_norm_b, m_w_in_b, m_w_out_b, m_norm_f, v_norm_a, v_w_in_a, v_w_grp_a, v_scale_a, v_w_out_a, v_norm_kv, v_w_k, v_w_v, v_norm_b, v_w_in_b, v_w_out_b, v_norm_f):
    weights = dict(norm_a=norm_a, w_in_a=w_in_a, w_grp_a=w_grp_a, scale_a=scale_a, w_out_a=w_out_a, norm_kv=norm_kv,
                   w_k=w_k, w_v=w_v, norm_b=norm_b, w_in_b=w_in_b, w_out_b=w_out_b, norm_f=norm_f)
    moments_m = dict(norm_a=m_norm_a, w_in_a=m_w_in_a, w_grp_a=m_w_grp_a, scale_a=m_scale_a, w_out_a=m_w_out_a,
                     norm_kv=m_norm_kv, w_k=m_w_k, w_v=m_w_v, norm_b=m_norm_b, w_in_b=m_w_in_b, w_out_b=m_w_out_b,
                     norm_f=m_norm_f)
    moments_v = dict(norm_a=v_norm_a, w_in_a=v_w_in_a, w_grp_a=v_w_grp_a, scale_a=v_scale_a, w_out_a=v_w_out_a,
                     norm_kv=v_norm_kv, w_k=v_w_k, w_v=v_w_v, norm_b=v_norm_b, w_in_b=v_w_in_b, w_out_b=v_w_out_b,
                     norm_f=v_norm_f)
    names = list(weights)
    d = x.shape[-1]
    c_idx = lax.axis_index("c").astype(jnp.int32).reshape(1)
    s_me = 2 * lax.axis_index("x") + lax.axis_index("y")
    s_idx = s_me.astype(jnp.int32).reshape(1)

    def as_lbrc(name):
        a = weights[name]
        if name == "w_grp_a":
            return a
        if a.ndim == 2:
            return a.reshape(1, 1, *a.shape)
        return a.reshape(a.shape[0], 1, *a.shape[1:])

    n_a, n_b = norm_a.shape[0], norm_b.shape[0]
    group_weights = {**{f"a{i}": [("w_in_a", i), ("w_grp_a", i), ("w_out_a", i)] for i in range(n_a)},
                     "kv": [("w_k", 0), ("w_v", 0)],
                     **{f"b{i}": [("w_in_b", i), ("w_out_b", i)] for i in range(n_b)}}
    group_order = [f"a{i}" for i in range(n_a)] + ["kv"] + [f"b{i}" for i in range(n_b)]
    slots, slot_groups = [], []
    for group in group_order:
        slot_groups.append(list(range(len(slots), len(slots) + len(group_weights[group]))))
        slots += [_cast_into_slot(f"cast_{name}{l}", as_lbrc(name), l, s_idx) for name, l in group_weights[group]]
    small_full, started = {}, {}

    def start_group(gi, carry=()):
        sems, bufs, carry = _gather_start(f"gather_start_{group_order[gi]}", [slots[t] for t in slot_groups[gi]], carry)
        started[gi] = (sems, bufs)
        return carry

    start_group(0)

    def gathered_form(name, g):
        if name in ("w_in_a", "w_in_b"):
            return g[0]
        if name == "w_grp_a":
            return g.reshape(g.shape[0], -1, g.shape[-1])
        return g.reshape(-1, g.shape[-1])

    def fetch(group, after):
        if group == "head":
            return {"norm_f": (0, norm_f)}
        gi = group_order.index(group)
        sems, bufs = started[gi]
        bufs = _gather_wait(f"gather_wait_{group}", sems, bufs, after)
        bufs, small_g = _gather_forward(f"gather_forward_{group}", bufs, [norm_a, scale_a] if gi == 0 else [])
        out = {name: (l, gathered_form(name, g)) for (name, l), g in zip(group_weights[group], bufs)}
        if gi == 0:
            for name, g in zip(("norm_a", "scale_a"), small_g):
                small_full[name] = g.transpose(1, 0, 2).reshape(g.shape[1], -1)
        layer = group_weights[group][0][1]
        if group.startswith("a"):
            gain_name, gain = "norm_a", small_full["norm_a"][layer]
            out.update(scale_a=(layer, small_full["scale_a"][layer]))
        elif group == "kv":
            gain_name, gain = "norm_kv", norm_kv
        else:
            gain_name, gain = "norm_b", norm_b[layer]
        gain = gain.reshape(1, -1)
        ahead = [gi + 1] + ([gi + 2] if gi + 2 < len(group_order) and group_order[gi + 1] == "kv" else [])
        for gj in ahead:
            if gj < len(group_order) and gj not in started:
                (gain,) = start_group(gj, (gain,))
        out[gain_name] = (layer, gain)
        return out

    in_flight = []

    def emit(group, grads_of, carry):
        keys = list(grads_of)
        recv1 = _exchange_halves(f"grad_exchange_{group}", [grads_of[k][1] for k in keys])
        parts = [_pair_add(f"pair_add_{k}{grads_of[k][0]}", grads_of[k][1], r, c_idx) for k, r in zip(keys, recv1)]
        sems, parts, lands, carry = _scatter_start(f"grad_scatter_start_{group}", parts, tuple(carry))
        in_flight.append((group, [(k, grads_of[k][0]) for k in keys], sems, parts, lands))
        return carry

    loss_vec, grad_x, small = _local_step(x[0], loss_target[0], n_a, n_b, fetch, emit)

    small_order = [("norm_a", i) for i in range(n_a)] + [("scale_a", i) for i in range(n_a)] + [("norm_kv", 0)] + \
                  [("norm_b", i) for i in range(norm_b.shape[0])] + [("norm_f", 0)]
    pad = lambda vec: jnp.pad(vec, ((0, PAD_ROWS - 1), (0, 0)))
    packed = jnp.concatenate([pad(loss_vec)] + [pad(small[n][i]) for n, i in small_order], axis=0)
    totals = _sum_devices("small_sum", _allgather_small(packed))
    loss = 0.5 * jnp.sum(totals[0]) / d
    small_tot = {}
    for j, (n, i) in enumerate(small_order):
        small_tot.setdefault(n, []).append(totals[PAD_ROWS * (j + 1)])
    grads = {}
    shard_w = norm_a.shape[1]
    for n in ("norm_a", "scale_a"):
        full = jnp.stack(small_tot[n])
        grads[n] = lax.dynamic_slice_in_dim(full, s_me * shard_w, shard_w, axis=1)
    grads["norm_kv"] = small_tot["norm_kv"][0]
    grads["norm_b"] = jnp.stack(small_tot["norm_b"])
    grads["norm_f"] = small_tot["norm_f"][0]

    sc_idx = jnp.concatenate([s_idx, c_idx])
    fulls = {name: None for name in BIG_WEIGHTS}
    for group, keys, sems, parts, lands in in_flight:
        parts, lands = _scatter_wait(f"grad_scatter_wait_{group}", sems, parts, lands, after=grad_x)
        for (name, i), p, r in zip(keys, parts, lands):
            n_layers = 1 if weights[name].ndim == 2 else weights[name].shape[0]
            fulls[name] = _final_add(f"final_add_{name}{i}", p, r, sc_idx, i, n_layers, into=fulls[name])
    shared = _share_halves([fulls[name] for name in BIG_WEIGHTS])
    for name, g in zip(BIG_WEIGHTS, shared):
        grads[name] = g.reshape(weights[name].shape)

    deltas, new_m, new_v = {}, {}, {}
    for n in names:
        shape = weights[n].shape
        as2d = (lambda a: a.reshape(1, -1)) if len(shape) == 1 else (lambda a: a)
        dl, mn, vn = _adamw(f"adamw_{n}", as2d(weights[n]), as2d(grads[n]), as2d(moments_m[n]), as2d(moments_v[n]))
        deltas[n], new_m[n], new_v[n] = dl.reshape(shape), mn.reshape(shape), vn.reshape(shape)

    return (loss, grad_x[None], *[grads[n] for n in names], *[deltas[n] for n in names],
            *[new_m[n] for n in names], *[new_v[n] for n in names])
```

```python
import functools
import math

import jax
import jax.numpy as jnp
from jax import lax
from jax.experimental import pallas as pl
from jax.experimental.pallas import tpu as pltpu

F32 = jnp.float32
BF16 = jnp.bfloat16

HEAD_DIM = 128
POOL_WINDOWS = (2, 4, 8, 16)
DILATED_PAIRS = ((128, 1), (512, 4), (2048, 16))
ROPE_THETA = 10000.0
RMS_EPS = 1e-6
NEG_INF = -1e30
N_CHIPS = 4

ADAM_LR = 0.001
ADAM_B1 = 0.9
ADAM_B2 = 0.999
ADAM_EPS = 1e-08
ADAM_WD = 0.01
ADAM_STEP = 10

VMEM_LIMIT_BYTES = 56 * 1024 * 1024
MESH = pl.DeviceIdType.MESH
ANY = pl.BlockSpec(memory_space=pl.ANY)


def _tile(n, pref):
    t = min(n, pref)
    assert n % t == 0, (n, pref)
    return t


def _params(sem=None):
    return pltpu.CompilerParams(dimension_semantics=sem, vmem_limit_bytes=VMEM_LIMIT_BYTES)


def _mm(name, a, b, *, grid2, nk, a_blk, a_map, b_blk, b_map, outs, dims, epi=None, epi_in=(), epi_specs=(),
        acc_shape=None, epi_scratch=(), into=None):
    n_epi, n_out = len(epi_in), len(outs)

    def body(*refs):
        a_ref, b_ref = refs[0], refs[1]
        e_refs = refs[2:2 + n_epi]
        first_out = 2 + n_epi + (0 if into is None else 1)
        o_refs = refs[first_out:first_out + n_out]
        s_refs = refs[first_out + n_out + (0 if nk == 1 else 1):]

        def contrib():
            a_val = a_ref[...]
            if a_val.ndim == 3:
                a_val = a_val.reshape(-1, a_val.shape[-1])
            return lax.dot_general(a_val, b_ref[...], (dims, ((), ())), preferred_element_type=F32)

        def finish(acc):
            if epi is None:
                o_refs[0][...] = acc.reshape(o_refs[0].shape).astype(o_refs[0].dtype)
            else:
                epi(acc, e_refs, o_refs, s_refs)

        if nk == 1:
            finish(contrib())
        else:
            acc_ref = refs[first_out + n_out]
            k = pl.program_id(2)

            @pl.when(k == 0)
            def _():
                acc_ref[...] = contrib()

            @pl.when(k > 0)
            def _():
                acc_ref[...] += contrib()

            @pl.when(k == nk - 1)
            def _():
                finish(acc_ref[...])

    scratch = ([] if nk == 1 else [pltpu.VMEM(acc_shape, F32)]) + list(epi_scratch)
    extra_in, extra_specs, aliases = (), (), {}
    if into is not None:
        extra_in, extra_specs, aliases = (into[0],), (ANY,), {2 + n_epi: into[1]}
    res = pl.pallas_call(
        body, name=name, grid=(grid2[0], grid2[1], nk),
        in_specs=[pl.BlockSpec(a_blk, a_map), pl.BlockSpec(b_blk, b_map), *epi_specs, *extra_specs],
        out_specs=[pl.BlockSpec(blk, imap) for _, blk, imap, _ in outs],
        out_shape=[jax.ShapeDtypeStruct(shape, dtype) for shape, _, _, dtype in outs],
        scratch_shapes=scratch, input_output_aliases=aliases,
        compiler_params=_params(("parallel", "parallel", "arbitrary")),
    )(a, b, *epi_in, *extra_in)
    return res[0] if n_out == 1 else tuple(res)


NN = ((1,), (0,))
NT = ((1,), (1,))
TN = ((0,), (0,))


def _rope_apply(t, cos, sin):
    return t * cos + pltpu.roll(t, HEAD_DIM // 2, 1) * sin


def _epi_add(acc, e_refs, o_refs, s_refs):
    o_refs[0][...] = (acc + e_refs[0][...]).astype(o_refs[0].dtype)


def _col_blocks(width):
    return [slice(c * HEAD_DIM, (c + 1) * HEAD_DIM) for c in range(width // HEAD_DIM)]


def _col_scratch(rows, width):
    return pltpu.VMEM((width // HEAD_DIM, rows, HEAD_DIM), F32)


def _to_residue_major(o_ref, scr, d, sl):
    if d == 1:
        o_ref[0, :, sl] = scr[...].astype(o_ref.dtype)
        return
    rows = scr.shape[0] // d
    for r in range(d):
        o_ref[r, :, sl] = scr[pl.ds(r, rows, stride=d), :].astype(o_ref.dtype)


def _from_residue_major(i_ref, scr, d, sl):
    if d == 1:
        return i_ref[0, :, sl].astype(F32)
    rows = i_ref.shape[1]
    for r in range(d):
        scr[pl.ds(r, rows, stride=d), :] = i_ref[r, :, sl].astype(F32)
    return scr[...]


def _make_epi_orders(dils, rope_scale):
    def epi(acc, e_refs, o_refs, s_refs):
        if rope_scale is not None:
            cos = e_refs[0][...]
            sin = e_refs[1][...]
        for c, sl in enumerate(_col_blocks(acc.shape[1])):
            scr = s_refs[0].at[c]
            scr[...] = acc[:, sl] if rope_scale is None else _rope_apply(acc[:, sl], cos, sin) * rope_scale
            for o_ref, d in zip(o_refs, dils):
                _to_residue_major(o_ref, scr, d, sl)
    return epi


def _make_epi_token_order(d, has_add):
    def epi(acc, e_refs, o_refs, s_refs):
        o_ref = o_refs[0]
        if d == 1:
            o_ref[...] = acc + e_refs[0][...] if has_add else acc
            return
        rows = acc.shape[0] // d
        for c, sl in enumerate(_col_blocks(acc.shape[1])):
            scr = s_refs[0].at[c]
            for r in range(d):
                scr[pl.ds(r, rows, stride=d), :] = acc[r * rows:(r + 1) * rows, sl]
            o_ref[:, sl] = scr[...] + e_refs[0][:, sl] if has_add else scr[...]
    return epi


def _mm_act_w(name, a, w, *, out_dtype=BF16, add=None, rope=None, n_first=0, n_cols=None, dils=None):
    s_len, k_len = a.shape
    bm = _tile(s_len, 1024)
    epi, epi_in, epi_specs, epi_scratch = None, (), (), ()
    if w.ndim == 3:
        ns, _, c = w.shape
        ns_used = ns if n_cols is None else n_cols
        bn = _tile(c, 1024)
        sub = c // bn
        grid2 = (ns_used * sub, s_len // bm)
        b_blk, b_map = (None, k_len, bn), (lambda j, i, k: (j // sub + n_first, 0, j % sub))
        n_len = ns_used * c
    else:
        n_len = w.shape[1]
        bn = _tile(n_len, 1024)
        grid2 = (n_len // bn, s_len // bm)
        b_blk, b_map = (k_len, bn), (lambda j, i, k: (0, j))
    if add is not None:
        epi, epi_in = _epi_add, (add,)
        epi_specs = (pl.BlockSpec((bm, bn), lambda j, i, k: (i, j)),)
    outs = [((s_len, n_len), (bm, bn), lambda j, i, k: (i, j), out_dtype)]
    if dils is not None:
        if rope is not None:
            epi_in = rope[:2]
            epi_specs = (pl.BlockSpec((bm, HEAD_DIM), lambda j, i, k: (i, 0)),) * 2
        epi = _make_epi_orders(dils, None if rope is None else rope[2])
        epi_scratch = (_col_scratch(bm, bn),)
        outs = [((d, s_len // d, n_len), (d, bm // d, bn), lambda j, i, k: (0, i, j), BF16) for d in dils]
    res = _mm(name, a, w, grid2=grid2, nk=1, a_blk=(bm, k_len), a_map=lambda j, i, k: (i, 0),
              b_blk=b_blk, b_map=b_map, outs=outs, dims=NN, epi=epi, epi_in=epi_in, epi_specs=epi_specs,
              epi_scratch=epi_scratch)
    return (res,) if dils is not None and len(dils) == 1 else res


def _mm_grad_act(name, dy, w, *, add=None, slot=None):
    if slot is not None:
        d = 1 if dy.ndim == 2 else dy.shape[0]
        s_len = dy.shape[-2] * d
        _, k_len, c = w.shape
        bm, bn = _tile(s_len, 1024), _tile(k_len, 1024)
        a_blk, a_map = ((bm, c), lambda j, i, k: (i, 0)) if dy.ndim == 2 else ((d, bm // d, c), lambda j, i, k: (0, i, 0))
        epi_in = () if add is None else (add,)
        return _mm(name, dy, w, grid2=(k_len // bn, s_len // bm), nk=1, a_blk=a_blk, a_map=a_map,
                   b_blk=(None, bn, c), b_map=lambda j, i, k: (slot, j, 0),
                   outs=[((s_len, k_len), (bm, bn), lambda j, i, k: (i, j), F32)], dims=NT,
                   epi=_make_epi_token_order(d, add is not None), epi_in=epi_in,
                   epi_specs=(pl.BlockSpec((bm, bn), lambda j, i, k: (i, j)),) * len(epi_in),
                   epi_scratch=(_col_scratch(bm, bn),) if d > 1 else ())
    s_len, n_len = dy.shape
    bm = _tile(s_len, 1024)
    if w.ndim == 3:
        ns, k_len, c = w.shape
        bk, nk = c, ns
        bn = _tile(k_len, 1024)
        b_blk, b_map = (None, bn, c), (lambda j, i, k: (k, j, 0))
    else:
        k_len = w.shape[0]
        bk = _tile(n_len, 2048)
        nk = n_len // bk
        bn = _tile(k_len, 1024)
        b_blk, b_map = (bn, bk), (lambda j, i, k: (j, k))
    epi, epi_in, epi_specs = None, (), ()
    if add is not None:
        epi, epi_in = _epi_add, (add,)
        epi_specs = (pl.BlockSpec((bm, bn), lambda j, i, k: (i, j)),)
    return _mm(name, dy, w, grid2=(k_len // bn, s_len // bm), nk=nk, a_blk=(bm, bk), a_map=lambda j, i, k: (i, k),
               b_blk=b_blk, b_map=b_map, outs=[((s_len, k_len), (bm, bn), lambda j, i, k: (i, j), F32)],
               dims=NT, epi=epi, epi_in=epi_in, epi_specs=epi_specs, acc_shape=(bm, bn))


def _mm_grad_w(name, a, dy, *, col_shards=None, slot=None, into=None):
    s_len, k_len = a.shape
    n_len = dy.shape[1]
    bk = _tile(s_len, 2048)
    bm = _tile(k_len, 1024)
    if slot is not None:
        bn = _tile(n_len, 1024)
        out = ((col_shards, k_len, n_len), (None, bm, bn), lambda j, i, k: (slot, i, j), BF16)
    elif col_shards:
        c = n_len // col_shards
        bn = _tile(c, 1024)
        sub = c // bn
        out = ((col_shards, k_len, c), (None, bm, bn), lambda j, i, k: (j // sub, i, j % sub), BF16)
    else:
        bn = _tile(n_len, 1024)
        out = ((k_len, n_len), (bm, bn), lambda j, i, k: (i, j), BF16)
    return _mm(name, a, dy, grid2=(n_len // bn, k_len // bm), nk=s_len // bk,
               a_blk=(bk, bm), a_map=lambda j, i, k: (k, i), b_blk=(bk, bn), b_map=lambda j, i, k: (k, j),
               outs=[out], dims=TN, acc_shape=(bm, bn), into=None if into is None else (into, 0))


def _mm_grp_fwd(name, pooled, wg):
    s_len, e = pooled.shape
    ng, g, _ = wg.shape
    bm = _tile(s_len, 1024)
    return _mm(name, pooled, wg, grid2=(ng, s_len // bm), nk=1, a_blk=(bm, g), a_map=lambda j, i, k: (i, j),
               b_blk=(None, g, g), b_map=lambda j, i, k: (j, 0, 0),
               outs=[((s_len, e), (bm, g), lambda j, i, k: (i, j), F32)], dims=NN)


def _mm_grp_grad_act(name, dy, wg):
    s_len, e = dy.shape
    ng, g, _ = wg.shape
    bm = _tile(s_len, 1024)
    return _mm(name, dy, wg, grid2=(ng, s_len // bm), nk=1, a_blk=(bm, g), a_map=lambda j, i, k: (i, j),
               b_blk=(None, g, g), b_map=lambda j, i, k: (j, 0, 0),
               outs=[((s_len, e), (bm, g), lambda j, i, k: (i, j), F32)], dims=NT)


def _mm_grp_grad_w(name, pooled, dy, ng):
    s_len, e = pooled.shape
    g = e // ng
    bk = _tile(s_len, 1024)
    return _mm(name, pooled, dy, grid2=(ng, 1), nk=s_len // bk, a_blk=(bk, g), a_map=lambda j, i, k: (k, j),
               b_blk=(bk, g), b_map=lambda j, i, k: (k, j),
               outs=[((N_CHIPS, ng, g // N_CHIPS, g), (N_CHIPS, None, g // N_CHIPS, g), lambda j, i, k: (0, j, 0, 0), BF16)],
               dims=TN, acc_shape=(g, g))


def _row_spec(bs, width, col=0):
    return pl.BlockSpec((bs, width), lambda i: (i, col))


def _vec_spec(width):
    return pl.BlockSpec((1, width), lambda i: (0, 0))


def _rows_call(body, name, s_len, in_specs, out_specs, out_shape, bs, aliases=None, sequential=False):
    return pl.pallas_call(
        body, name=name, grid=(s_len // bs,), in_specs=in_specs, out_specs=out_specs, out_shape=out_shape,
        input_output_aliases=aliases or {},
        compiler_params=_params(("arbitrary",) if sequential else ("parallel",)))


def _accumulate(ref, part):
    i = pl.program_id(0)

    @pl.when(i == 0)
    def _():
        ref[...] = part

    @pl.when(i > 0)
    def _():
        ref[...] += part


def _rms_scale(xf):
    return lax.rsqrt(jnp.mean(xf * xf, axis=-1, keepdims=True) + RMS_EPS)


def _res_spec(dil, bs, width):
    return pl.BlockSpec((dil, bs // dil, width), lambda i: (0, i, 0))


def _res_shape(dil, s_len, width, dtype):
    return jax.ShapeDtypeStruct((dil, s_len // dil, width), dtype)


def _rmsnorm_fwd(name, x, gain, dils=()):
    s_len, d = x.shape
    bs = _tile(s_len, 256)

    def body(x_ref, g_ref, h_ref, *rest):
        xf = x_ref[...]
        h = (xf * _rms_scale(xf)) * g_ref[...]
        h_ref[...] = h.astype(BF16)
        if dils:
            for c, sl in enumerate(_col_blocks(d)):
                scr = rest[-1].at[c]
                scr[...] = h[:, sl]
                for o_ref, dil in zip(rest[:-1], dils):
                    _to_residue_major(o_ref, scr, dil, sl)

    res = pl.pallas_call(
        body, name=name, grid=(s_len // bs,), in_specs=[_row_spec(bs, d), _vec_spec(d)],
        out_specs=[_row_spec(bs, d)] + [_res_spec(dil, bs, d) for dil in dils],
        out_shape=[jax.ShapeDtypeStruct((s_len, d), BF16)] + [_res_shape(dil, s_len, d, BF16) for dil in dils],
        scratch_shapes=[_col_scratch(bs, d)] if dils else [],
        compiler_params=_params(("parallel",)))(x, gain)
    return res[0] if not dils else tuple(res)


def _rmsnorm_bwd(name, x, gain, dh, dres):
    s_len, d = x.shape
    bs = _tile(s_len, 256)

    def body(x_ref, g_ref, dh_ref, dres_ref, dx_ref, dxb_ref, dg_ref):
        xf = x_ref[...]
        r = _rms_scale(xf)
        xh = xf * r
        dh_f = dh_ref[...]
        t = dh_f * g_ref[...]
        dx = dres_ref[...] + r * (t - xh * jnp.mean(t * xh, axis=-1, keepdims=True))
        dx_ref[...] = dx
        dxb_ref[...] = dx.astype(BF16)
        _accumulate(dg_ref, jnp.sum(dh_f * xh, axis=0, keepdims=True))

    return _rows_call(
        body, name, s_len,
        [_row_spec(bs, d), _vec_spec(d), _row_spec(bs, d), _row_spec(bs, d)],
        [_row_spec(bs, d), _row_spec(bs, d), _vec_spec(d)],
        [jax.ShapeDtypeStruct((s_len, d), F32), jax.ShapeDtypeStruct((s_len, d), BF16),
         jax.ShapeDtypeStruct((1, d), F32)], bs, sequential=True)(x, gain, dh, dres)


def _loss_head(name, x, gain, target):
    s_len, d = x.shape
    bs = _tile(s_len, 256)

    def body(x_ref, g_ref, t_ref, lv_ref, dx_ref, dxb_ref, dg_ref):
        xf = x_ref[...]
        r = _rms_scale(xf)
        xh = xf * r
        err = xh * g_ref[...] - t_ref[...]
        dy = err * (1.0 / d)
        t = dy * g_ref[...]
        dx = r * (t - xh * jnp.mean(t * xh, axis=-1, keepdims=True))
        dx_ref[...] = dx
        dxb_ref[...] = dx.astype(BF16)
        _accumulate(lv_ref, jnp.sum(err * err, axis=0, keepdims=True))
        _accumulate(dg_ref, jnp.sum(dy * xh, axis=0, keepdims=True))

    return _rows_call(
        body, name, s_len, [_row_spec(bs, d), _vec_spec(d), _row_spec(bs, d)],
        [_vec_spec(d), _row_spec(bs, d), _row_spec(bs, d), _vec_spec(d)],
        [jax.ShapeDtypeStruct((1, d), F32), jax.ShapeDtypeStruct((s_len, d), F32),
         jax.ShapeDtypeStruct((s_len, d), BF16), jax.ShapeDtypeStruct((1, d), F32)],
        bs, sequential=True)(x, gain, target)


def _sigmoid(g):
    return 1.0 / (1.0 + jnp.exp(-g))


def _gate_a_fwd(name, ypre, proj, scale):
    s_len, e = ypre.shape
    bs = _tile(s_len, 256)

    def body(y_ref, g_ref, sc_ref, z_ref):
        g = g_ref[...]
        z_ref[...] = (y_ref[...] * sc_ref[...] * (g * _sigmoid(g))).astype(BF16)

    return _rows_call(body, name, s_len, [_row_spec(bs, e), _row_spec(bs, e, 1), _vec_spec(e)], _row_spec(bs, e),
                      jax.ShapeDtypeStruct((s_len, e), BF16), bs)(ypre, proj, scale)


def _gate_a_bwd(name, dz, ypre, proj, scale):
    s_len, e = ypre.shape
    bs = _tile(s_len, 256)

    def body(dz_ref, y_ref, g_ref, sc_ref, dy_ref, dproj_ref, dsc_ref):
        g = g_ref[...]
        sg = _sigmoid(g)
        silu = g * sg
        dz_f = dz_ref[...]
        ypre_f = y_ref[...]
        dys = dz_f * silu
        dy_ref[...] = (dys * sc_ref[...]).astype(BF16)
        dproj_ref[...] = (dz_f * (ypre_f * sc_ref[...]) * (sg * (1.0 + g * (1.0 - sg)))).astype(BF16)
        _accumulate(dsc_ref, jnp.sum(dys * ypre_f, axis=0, keepdims=True))

    return _rows_call(
        body, name, s_len, [_row_spec(bs, e), _row_spec(bs, e), _row_spec(bs, e, 1), _vec_spec(e)],
        [_row_spec(bs, e), _row_spec(bs, e, 1), _vec_spec(e)],
        [jax.ShapeDtypeStruct((s_len, e), BF16), jax.ShapeDtypeStruct((s_len, 2 * e), BF16),
         jax.ShapeDtypeStruct((1, e), F32)], bs, sequential=True)(dz, ypre, proj, scale)


def _merge_gate_fwd(name, outs, lses, gate, dils):
    s_len, e = gate.shape
    bs = _tile(s_len, 256)
    n = len(outs)

    def body(*refs):
        o_refs, l_refs, g_ref = refs[:n], refs[n:2 * n], refs[2 * n]
        m_ref, lj_ref, z_ref = refs[2 * n + 1:2 * n + 4]
        scratch = refs[2 * n + 4]
        for c, sl in enumerate(_col_blocks(e)):
            ls = [_from_residue_major(r, scratch.at[2 * j, c], dil, sl) for j, (r, dil) in enumerate(zip(l_refs, dils))]
            os_ = [_from_residue_major(r, scratch.at[2 * j + 1, c], dil, sl) for j, (r, dil) in enumerate(zip(o_refs, dils))]
            mx = functools.reduce(jnp.maximum, ls)
            ws = [jnp.exp(l - mx) for l in ls]
            den = functools.reduce(lambda a, b: a + b, ws)
            merged = functools.reduce(lambda a, b: a + b, [w * o for w, o in zip(ws, os_)]) / den
            g = g_ref[:, sl]
            m_ref[:, sl] = merged.astype(BF16)
            lj_ref[:, sl] = mx + jnp.log(den)
            z_ref[:, sl] = (merged * (g * _sigmoid(g))).astype(BF16)

    spec = _row_spec(bs, e)
    res_specs = [_res_spec(dil, bs, e) for dil in dils]
    return pl.pallas_call(
        body, name=name, grid=(s_len // bs,), in_specs=res_specs + res_specs + [spec], out_specs=[spec] * 3,
        out_shape=[jax.ShapeDtypeStruct((s_len, e), BF16), jax.ShapeDtypeStruct((s_len, e), F32),
                   jax.ShapeDtypeStruct((s_len, e), BF16)],
        scratch_shapes=[pltpu.VMEM((2 * n, e // HEAD_DIM, bs, HEAD_DIM), F32)],
        compiler_params=_params(("parallel",)))(*outs, *lses, gate)


def _gate_b_bwd(name, dz, merged, gate, lse, dils):
    s_len, e = gate.shape
    bs = _tile(s_len, 256)
    n = len(dils)

    def body(dz_ref, m_ref, g_ref, l_ref, dg_ref, *rest):
        out_refs, scratch = rest[:3 * n], rest[3 * n]
        for c, sl in enumerate(_col_blocks(e)):
            g = g_ref[:, sl]
            sg = _sigmoid(g)
            dz_f = dz_ref[:, sl]
            merged = m_ref[:, sl].astype(F32)
            dmerged = dz_f * (g * sg)
            dg_ref[:, sl] = (dz_f * merged * (sg * (1.0 + g * (1.0 - sg)))).astype(BF16)
            values = (dmerged, l_ref[:, sl],
                      jnp.broadcast_to(jnp.sum(dmerged * merged, axis=-1, keepdims=True), (bs, HEAD_DIM)))
            for t, val in enumerate(values):
                scr = scratch.at[t, c]
                scr[...] = val
                for j, dil in enumerate(dils):
                    _to_residue_major(out_refs[3 * j + t], scr, dil, sl)

    spec = _row_spec(bs, e)
    out_specs, out_shape = [spec], [jax.ShapeDtypeStruct((s_len, e), BF16)]
    for dil in dils:
        out_specs += [_res_spec(dil, bs, e)] * 3
        out_shape += [_res_shape(dil, s_len, e, BF16), _res_shape(dil, s_len, e, F32), _res_shape(dil, s_len, e, F32)]
    res = pl.pallas_call(
        body, name=name, grid=(s_len // bs,), in_specs=[spec] * 4, out_specs=out_specs, out_shape=out_shape,
        scratch_shapes=[pltpu.VMEM((3, e // HEAD_DIM, bs, HEAD_DIM), F32)],
        compiler_params=_params(("parallel",)))(dz, merged, gate, lse)
    return res[0], [tuple(res[1 + 3 * j:4 + 3 * j]) for j in range(n)]


def _kv_grad_prep(name, dk_accs, dv_accs, dils, cos, sin_inv):
    n = len(dils)
    e = dk_accs[0].shape[-1]
    s_len = dk_accs[0].shape[0] * dk_accs[0].shape[1]
    bs = _tile(s_len, 256)

    def body(*refs):
        dk_refs, dv_refs = refs[:n], refs[n:2 * n]
        c_ref, s_ref, dkb_ref, dvb_ref, scratch = refs[2 * n:]
        cos_t, sin_t = c_ref[...], s_ref[...]
        add = lambda a, b: a + b
        for c, sl in enumerate(_col_blocks(e)):
            dk = functools.reduce(add, [_from_residue_major(r, scratch.at[j, c], dil, sl)
                                        for j, (r, dil) in enumerate(zip(dk_refs, dils))])
            dkb_ref[:, sl] = _rope_apply(dk, cos_t, sin_t).astype(BF16)
            dv = functools.reduce(add, [_from_residue_major(r, scratch.at[n + j, c], dil, sl)
                                        for j, (r, dil) in enumerate(zip(dv_refs, dils))])
            dvb_ref[:, sl] = dv.astype(BF16)

    spec, rspec = _row_spec(bs, e), _row_spec(bs, HEAD_DIM)
    res_specs = [_res_spec(dil, bs, e) for dil in dils]
    return pl.pallas_call(
        body, name=name, grid=(s_len // bs,), in_specs=res_specs + res_specs + [rspec, rspec], out_specs=[spec, spec],
        out_shape=[jax.ShapeDtypeStruct((s_len, e), BF16)] * 2,
        scratch_shapes=[pltpu.VMEM((2 * n, e // HEAD_DIM, bs, HEAD_DIM), F32)],
        compiler_params=_params(("parallel",)))(*dk_accs, *dv_accs, cos, sin_inv)


def _pool_cols(e):
    return _tile(e // len(POOL_WINDOWS), 256)


def _window_sum(val, grp, s_len, forward):
    rows = lax.broadcasted_iota(jnp.int32, val.shape, 0)
    acc = val
    for level in range(len(POOL_WINDOWS)):
        step = 1 << level
        if forward:
            shifted = jnp.where(rows >= step, pltpu.roll(acc, step, 0), 0.0)
        else:
            shifted = jnp.where(rows < s_len - step, pltpu.roll(acc, s_len - step, 0), 0.0)
        acc = jnp.where(level <= grp, acc + shifted, acc)
    return acc


def _window_count(shape, grp):
    rows = lax.broadcasted_iota(jnp.int32, shape, 0)
    return jnp.minimum(rows + 1, jnp.left_shift(2, grp)).astype(F32)


def _pool_fwd(name, proj):
    s_len, e2 = proj.shape
    e = e2 // 2
    cb = _pool_cols(e)
    per_grp = e // len(POOL_WINDOWS) // cb
    assert POOL_WINDOWS == tuple(2 << g for g in range(len(POOL_WINDOWS)))

    def body(u_ref, p_ref):
        grp = pl.program_id(0)
        u = u_ref[...]
        total = _window_sum(u, grp, s_len, True)
        p_ref[...] = (total / _window_count(u.shape, grp) - u).astype(BF16)

    spec = pl.BlockSpec((s_len, cb), lambda g, c: (0, g * per_grp + c))
    return pl.pallas_call(
        body, name=name, grid=(len(POOL_WINDOWS), per_grp), in_specs=[spec], out_specs=spec,
        out_shape=jax.ShapeDtypeStruct((s_len, e), BF16), compiler_params=_params(("parallel", "parallel")))(proj)


def _pool_bwd(name, dpooled, dproj):
    s_len, e = dpooled.shape
    cb = _pool_cols(e)
    per_grp = e // len(POOL_WINDOWS) // cb

    def body(dp_ref, _, du_ref):
        grp = pl.program_id(0)
        dp = dp_ref[...]
        total = _window_sum(dp / _window_count(dp.shape, grp), grp, s_len, False)
        du_ref[...] = (total - dp).astype(BF16)

    spec = pl.BlockSpec((s_len, cb), lambda g, c: (0, g * per_grp + c))
    return pl.pallas_call(
        body, name=name, grid=(len(POOL_WINDOWS), per_grp), in_specs=[spec, ANY], out_specs=spec,
        out_shape=jax.ShapeDtypeStruct(dproj.shape, BF16), input_output_aliases={1: 0},
        compiler_params=_params(("parallel", "parallel")))(dpooled, dproj)


def _band_masks(nb, first):
    row = lax.broadcasted_iota(jnp.int32, (nb, nb), 0)
    col = lax.broadcasted_iota(jnp.int32, (nb, nb), 1)
    return col >= row + jnp.where(first, 2 * nb, 0), col <= row


def _dot(a, b, dims):
    return lax.dot_general(a, b, (dims, ((), ())), preferred_element_type=F32)


def _attn_fwd(name, window, q, k, v):
    dil, m, e = k.shape
    nb = window // dil
    nblk = m // nb
    heads = e // HEAD_DIM

    def body(q_ref, kc_ref, vc_ref, o_ref, l_ref, kp_ref, vp_ref):
        first = pl.program_id(1) == 0

        @pl.when(first)
        def _():
            kp_ref[...] = jnp.zeros_like(kp_ref)
            vp_ref[...] = jnp.zeros_like(vp_ref)

        mask_p, mask_c = _band_masks(nb, first)
        cols = _col_blocks(e)
        s_p = [jnp.where(mask_p, _dot(q_ref[:, sl], kp_ref[:, sl], NT), NEG_INF) for sl in cols]
        s_c = [jnp.where(mask_c, _dot(q_ref[:, sl], kc_ref[:, sl], NT), NEG_INF) for sl in cols]
        mx = [jnp.maximum(jnp.max(a, axis=-1, keepdims=True), jnp.max(b, axis=-1, keepdims=True))
              for a, b in zip(s_p, s_c)]
        p_p = [jnp.exp(a - m) for a, m in zip(s_p, mx)]
        p_c = [jnp.exp(a - m) for a, m in zip(s_c, mx)]
        den = [jnp.sum(a, axis=-1, keepdims=True) + jnp.sum(b, axis=-1, keepdims=True) for a, b in zip(p_p, p_c)]
        for h, sl in enumerate(cols):
            out = _dot(p_p[h].astype(BF16), vp_ref[:, sl], NN) + _dot(p_c[h].astype(BF16), vc_ref[:, sl], NN)
            o_ref[:, sl] = (out / den[h]).astype(BF16)
            l_ref[:, sl] = jnp.broadcast_to(mx[h] + jnp.log(den[h]), (nb, HEAD_DIM))
        kp_ref[...] = kc_ref[...]
        vp_ref[...] = vc_ref[...]

    blk = (None, nb, e)
    cur = lambda r, n: (r, n, 0)
    return pl.pallas_call(
        body, name=name, grid=(dil, nblk),
        in_specs=[pl.BlockSpec(blk, cur)] * 3,
        out_specs=[pl.BlockSpec(blk, cur), pl.BlockSpec(blk, cur)],
        out_shape=[jax.ShapeDtypeStruct((dil, m, e), BF16), jax.ShapeDtypeStruct((dil, m, e), F32)],
        scratch_shapes=[pltpu.VMEM((nb, e), BF16), pltpu.VMEM((nb, e), BF16)],
        compiler_params=_params(("parallel", "arbitrary")),
    )(q, k, v)


def _attn_bwd(name, window, scale, q, k, v, dout, lse, delta, cos, sin_inv, dk_acc, dv_acc):
    dil, m, e = k.shape
    nb = window // dil
    nblk = m // nb
    heads = e // HEAD_DIM

    accumulate = dk_acc is not None

    def body(k_ref, v_ref, q0_ref, qn_ref, do0_ref, don_ref, l0_ref, ln_ref, dl0_ref, dln_ref, c_ref, s_ref, *rest):
        if accumulate:
            dki_ref, dvi_ref = rest[:2]
            rest = rest[2:]
        dq_ref, dko_ref, dvo_ref, carry_ref, qc_ref, doc_ref, lc_ref, dlc_ref = rest
        n = pl.program_id(1)

        @pl.when(n == 0)
        def _():
            carry_ref[...] = jnp.zeros_like(carry_ref)
            qc_ref[...] = q0_ref[...]
            doc_ref[...] = do0_ref[...]
            lc_ref[...] = l0_ref[...]
            dlc_ref[...] = dl0_ref[...]

        mask_n, mask_c = _band_masks(nb, n == nblk - 1)
        cos_t, sin_t = c_ref[...], s_ref[...]
        cols = _col_blocks(e)
        stat = lambda ref, sl: ref[:, sl] if nb == HEAD_DIM else ref[:, sl][:, :1]
        s_c = [_dot(qc_ref[:, sl], k_ref[:, sl], NT) for sl in cols]
        s_n = [_dot(qn_ref[:, sl], k_ref[:, sl], NT) for sl in cols]
        dp_c = [_dot(doc_ref[:, sl], v_ref[:, sl], NT) for sl in cols]
        dp_n = [_dot(don_ref[:, sl], v_ref[:, sl], NT) for sl in cols]
        p_c = [jnp.where(mask_c, jnp.exp(s - stat(lc_ref, sl)), 0.0) for s, sl in zip(s_c, cols)]
        p_n = [jnp.where(mask_n, jnp.exp(s - stat(ln_ref, sl)), 0.0) for s, sl in zip(s_n, cols)]
        ds_c = [(p * (dp - stat(dlc_ref, sl))).astype(BF16) for p, dp, sl in zip(p_c, dp_c, cols)]
        ds_n = [(p * (dp - stat(dln_ref, sl))).astype(BF16) for p, dp, sl in zip(p_n, dp_n, cols)]
        for h, sl in enumerate(cols):
            dq = (carry_ref[:, sl] + _dot(ds_c[h], k_ref[:, sl], NN)) * scale
            dq_ref[:, sl] = _rope_apply(dq, cos_t, sin_t).astype(BF16)
        for h, sl in enumerate(cols):
            carry_ref[:, sl] = _dot(ds_n[h], k_ref[:, sl], NN)
        for h, sl in enumerate(cols):
            dk = _dot(ds_c[h], qc_ref[:, sl], TN) + _dot(ds_n[h], qn_ref[:, sl], TN)
            dv = _dot(p_c[h].astype(BF16), doc_ref[:, sl], TN) + _dot(p_n[h].astype(BF16), don_ref[:, sl], TN)
            dko_ref[:, sl] = dki_ref[:, sl] + dk if accumulate else dk
            dvo_ref[:, sl] = dvi_ref[:, sl] + dv if accumulate else dv
        qc_ref[...] = qn_ref[...]
        doc_ref[...] = don_ref[...]
        lc_ref[...] = ln_ref[...]
        dlc_ref[...] = dln_ref[...]

    blk = (None, nb, e)
    cur = lambda r, n: (r, n, 0)
    nxt = lambda r, n: (r, jnp.minimum(n + 1, nblk - 1), 0)
    first = lambda r, n: (r, 0, 0)
    rblk = (None, nb, HEAD_DIM)
    both = lambda shape: [pl.BlockSpec(shape, first), pl.BlockSpec(shape, nxt)]
    accs = (dk_acc, dv_acc) if accumulate else ()
    return pl.pallas_call(
        body, name=name, grid=(dil, nblk),
        in_specs=[pl.BlockSpec(blk, cur), pl.BlockSpec(blk, cur), *both(blk), *both(blk), *both(blk), *both(blk),
                  pl.BlockSpec(rblk, cur), pl.BlockSpec(rblk, cur)] + [pl.BlockSpec(blk, cur)] * len(accs),
        out_specs=[pl.BlockSpec(blk, cur)] * 3,
        out_shape=[jax.ShapeDtypeStruct((dil, m, e), BF16),
                   jax.ShapeDtypeStruct((dil, m, e), F32), jax.ShapeDtypeStruct((dil, m, e), F32)],
        scratch_shapes=[pltpu.VMEM((nb, e), F32), pltpu.VMEM((nb, e), BF16), pltpu.VMEM((nb, e), BF16),
                        pltpu.VMEM((nb, e), F32), pltpu.VMEM((nb, e), F32)],
        input_output_aliases={12: 1, 13: 2} if accumulate else {},
        compiler_params=_params(("parallel", "arbitrary")),
    )(k, v, q, q, dout, dout, lse, lse, delta, delta, cos, sin_inv, *accs)


def _rope_tables(s_len):
    inv_freq = 1.0 / (ROPE_THETA ** (jnp.arange(0, HEAD_DIM, 2, dtype=F32) / HEAD_DIM))
    ang = jnp.arange(s_len, dtype=F32)[:, None] * inv_freq[None, :]
    cos, sin = jnp.cos(ang), jnp.sin(ang)
    return jnp.concatenate([cos, cos], axis=1), jnp.concatenate([-sin, sin], axis=1)


def _row(vec):
    return vec.reshape(1, -1)


def _local_step(x, target, n_a, n_b, fetch, begin, emit):
    s_len, d = x.shape
    n_q = len(DILATED_PAIRS)
    cos, sin = _rope_tables(s_len)
    sin_inv = -sin
    q_scale = 1.0 / math.sqrt(HEAD_DIM)
    w = {}

    def need(group, after):
        for name, (layer, arr) in fetch(group, after).items():
            w.setdefault(name, {})[layer] = arr

    saved_a = []
    for i in range(n_a):
        need(f"a{i}", x)
        h = _rmsnorm_fwd(f"a{i}_norm", x, _row(w["norm_a"][i]))
        proj = _mm_act_w(f"a{i}_in", h, w["w_in_a"][i], out_dtype=F32)
        pooled = _pool_fwd(f"a{i}_pool", proj)
        ypre = _mm_grp_fwd(f"a{i}_grp", pooled, w["w_grp_a"][i])
        z = _gate_a_fwd(f"a{i}_gate", ypre, proj, _row(w["scale_a"][i]))
        x_next = _mm_act_w(f"a{i}_out", z, w["w_out_a"][i], out_dtype=F32, add=x)
        saved_a.append((x, h, proj, pooled, ypre, z))
        x = x_next

    x_kv = x
    need("kv", x)
    e = w["w_k"][0].shape[1]
    kv_in = _rmsnorm_fwd("kv_norm", x, _row(w["norm_kv"][0]))
    windows = [window for window, _ in DILATED_PAIRS]
    dils = tuple(dil for _, dil in DILATED_PAIRS)
    far_dils = tuple(dil for dil in dils if dil > 1)
    ks = _mm_act_w("kv_k", kv_in, w["w_k"][0], rope=(cos, sin, 1.0), dils=dils)
    vs = _mm_act_w("kv_v", kv_in, w["w_v"][0], dils=dils)

    saved_b = []
    for i in range(n_b):
        need(f"b{i}", x if i > 0 else vs[0])
        hs = _rmsnorm_fwd(f"b{i}_norm", x, _row(w["norm_b"][i]), dils=far_dils)
        hs = {1: hs[0], **{dil: h_d.reshape(s_len, d) for dil, h_d in zip(far_dils, hs[1:])}}
        qs = [_mm_act_w(f"b{i}_q{g}", hs[1], w["w_in_b"][i], rope=(cos, sin, q_scale), n_first=g, n_cols=1,
                        dils=(dil,))[0] for g, dil in enumerate(dils)]
        gate = _mm_act_w(f"b{i}_g", hs[1], w["w_in_b"][i], out_dtype=F32, n_first=n_q, n_cols=1)
        outs, lses = [], []
        for g in range(n_q):
            o_g, l_g = _attn_fwd(f"b{i}_attn{g}", windows[g], qs[g], ks[g], vs[g])
            outs.append(o_g)
            lses.append(l_g)
        merged, lse, z = _merge_gate_fwd(f"b{i}_merge", outs, lses, gate, dils)
        x_next = _mm_act_w(f"b{i}_out", z, w["w_out_b"][i], out_dtype=F32, add=x)
        saved_b.append((x, hs, qs, gate, merged, lse, z))
        x = x_next

    need("head", x)
    loss_vec, dx, dxb, g_norm_f = _loss_head("loss_head", x, _row(w["norm_f"][0]), target)

    small = {"norm_a": {}, "scale_a": {}, "norm_kv": {}, "norm_b": {}, "norm_f": {0: g_norm_f}}
    shard_rows = lambda g2: g2.reshape(N_CHIPS, g2.shape[0] // N_CHIPS, g2.shape[1])

    res_major = lambda t, dil: t.reshape(s_len // dil, dil, t.shape[1]).transpose(1, 0, 2)
    cos_r = [res_major(cos, dil) for dil in dils]
    sin_inv_r = [res_major(sin_inv, dil) for dil in dils]
    dk_accs = [None] * len(dils)
    dv_accs = [None] * len(dils)
    for i in reversed(range(n_b)):
        x_in, hs, qs, gate, merged, lse, z = saved_b[i]
        dz = _mm_grad_act(f"b{i}_dz", dxb, w["w_out_b"][i])
        g_out = shard_rows(_mm_grad_w(f"b{i}_gwo", z, dxb))
        dgate, stats = _gate_b_bwd(f"b{i}_dgate", dz, merged, gate, lse, dils)
        dh = _mm_grad_act(f"b{i}_dh{n_q}", dgate, w["w_in_b"][i], slot=n_q)
        g_in = _mm_grad_w(f"b{i}_gwi{n_q}", hs[1], dgate, col_shards=n_q + 1, slot=n_q)
        for g, dil in enumerate(dils):
            dout, lse_g, delta_g = stats[g]
            dq, dk_accs[g], dv_accs[g] = _attn_bwd(f"b{i}_dattn{g}", windows[g], q_scale, qs[g], ks[g], vs[g], dout,
                                                   lse_g, delta_g, cos_r[g], sin_inv_r[g], dk_accs[g], dv_accs[g])
            dh = _mm_grad_act(f"b{i}_dh{g}", dq if dil > 1 else dq[0], w["w_in_b"][i], add=dh, slot=g)
            g_in = _mm_grad_w(f"b{i}_gwi{g}", hs[dil], dq.reshape(s_len, e), col_shards=n_q + 1, slot=g, into=g_in)
        (dh,) = begin(f"b{i}", {"w_in_b": (i, g_in), "w_out_b": (i, g_out)}, (dh,))
        dx, dxb, small["norm_b"][i] = _rmsnorm_bwd(f"b{i}_dnorm", x_in, _row(w["norm_b"][i]), dh, dx)
        dx, dxb = emit(f"b{i}", (dx, dxb))

    dkb, dvb = _kv_grad_prep("kv_dprep", dk_accs, dv_accs, dils, cos, sin_inv)
    dkv = _mm_grad_act("kv_dk", dkb, w["w_k"][0])
    dkv = _mm_grad_act("kv_dv", dvb, w["w_v"][0], add=dkv)
    g_k = shard_rows(_mm_grad_w("kv_gwk", kv_in, dkb))
    g_v = shard_rows(_mm_grad_w("kv_gwv", kv_in, dvb))
    (dkv,) = begin("kv", {"w_k": (0, g_k), "w_v": (0, g_v)}, (dkv,))
    dx, dxb, small["norm_kv"][0] = _rmsnorm_bwd("kv_dnorm", x_kv, _row(w["norm_kv"][0]), dkv, dx)
    dx, dxb = emit("kv", (dx, dxb))

    for i in reversed(range(n_a)):
        x_in, h, proj, pooled, ypre, z = saved_a[i]
        dz = _mm_grad_act(f"a{i}_dz", dxb, w["w_out_a"][i])
        g_out = shard_rows(_mm_grad_w(f"a{i}_gwo", z, dxb))
        dypre, dproj, small["scale_a"][i] = _gate_a_bwd(f"a{i}_dgate", dz, ypre, proj, _row(w["scale_a"][i]))
        dpooled = _mm_grp_grad_act(f"a{i}_dgrp", dypre, w["w_grp_a"][i])
        g_grp = _mm_grp_grad_w(f"a{i}_gwg", pooled, dypre, len(POOL_WINDOWS))
        g_grp = g_grp.reshape(N_CHIPS, -1, g_grp.shape[-1])
        last = i == 0
        if last:
            (dpooled,) = emit(f"a{i}", begin(f"a{i}", {"w_grp_a": (i, g_grp), "w_out_a": (i, g_out)}, (dpooled,)))
        dproj = _pool_bwd(f"a{i}_dpool", dpooled, dproj)
        g_in = _mm_grad_w(f"a{i}_gwi", h, dproj, col_shards=N_CHIPS)
        if last:
            (dproj,) = emit(f"a{i}i", begin(f"a{i}i", {"w_in_a": (i, g_in)}, (dproj,)))
        dh = _mm_grad_act(f"a{i}_dh", dproj, w["w_in_a"][i])
        if not last:
            (dh,) = begin(f"a{i}", {"w_in_a": (i, g_in), "w_grp_a": (i, g_grp), "w_out_a": (i, g_out)}, (dh,))
        dx, dxb, small["norm_a"][i] = _rmsnorm_bwd(f"a{i}_dnorm", x_in, _row(w["norm_a"][i]), dh, dx)
        if not last:
            dx, dxb = emit(f"a{i}", (dx, dxb))

    return loss_vec, dx, small


BIG_WEIGHTS = ("w_in_a", "w_grp_a", "w_out_a", "w_k", "w_v", "w_in_b", "w_out_b")


def _pair_add(name, grad, recv, c_idx):
    _, r, cols = grad.shape
    half = r // 2
    rb = _tile(half, 256)
    nrb = half // rb

    def body(c_ref, g_ref, r_ref, o_ref):
        o_ref[...] = (g_ref[...].astype(F32) + r_ref[...].astype(F32)).astype(BF16)

    blk = (None, rb, cols)
    grid_spec = pltpu.PrefetchScalarGridSpec(
        num_scalar_prefetch=1, grid=(N_CHIPS, nrb),
        in_specs=[pl.BlockSpec(blk, lambda s, i, c: (s, c[0] * nrb + i, 0)), pl.BlockSpec(blk, lambda s, i, c: (s, i, 0))],
        out_specs=pl.BlockSpec(blk, lambda s, i, c: (s, i, 0)))
    return pl.pallas_call(body, name=name, grid_spec=grid_spec,
                          out_shape=jax.ShapeDtypeStruct((N_CHIPS, half, cols), BF16),
                          compiler_params=_params(("parallel", "parallel")))(c_idx, grad, recv)


def _final_add(name, part, recv, sc_idx, layer, n_layers, into=None):
    _, half, cols = part.shape
    rb = _tile(half, 256)
    nrb = half // rb
    n_peer = recv.shape[0]

    def body(sc_ref, p_ref, *refs):
        acc = p_ref[...].astype(F32)
        for r_ref in refs[:n_peer]:
            acc = acc + r_ref[...].astype(F32)
        refs[-1][...] = acc

    blk = (None, rb, cols)
    peer_spec = lambda k: pl.BlockSpec(blk, lambda i, sc: (k, i, 0))
    grid_spec = pltpu.PrefetchScalarGridSpec(
        num_scalar_prefetch=1, grid=(nrb,),
        in_specs=[pl.BlockSpec(blk, lambda i, sc: (sc[0], i, 0))] + [peer_spec(k) for k in range(n_peer)]
                 + ([] if into is None else [ANY]),
        out_specs=pl.BlockSpec(blk, lambda i, sc: (layer, sc[1] * nrb + i, 0)))
    extra = () if into is None else (into,)
    return pl.pallas_call(body, name=name, grid_spec=grid_spec,
                          out_shape=jax.ShapeDtypeStruct((n_layers, 2 * half, cols), F32),
                          input_output_aliases={} if into is None else {2 + n_peer: 0},
                          compiler_params=_params(("parallel",)))(sc_idx, part, *([recv] * n_peer), *extra)


def _cast_into_slot(name, arr, layer, s_idx):
    _, b, r, cols = arr.shape
    rb = _tile(r, 512)

    def body(s_ref, a_ref, o_ref):
        o_ref[...] = a_ref[...].astype(BF16)

    blk = (None, None, rb, cols)
    grid_spec = pltpu.PrefetchScalarGridSpec(
        num_scalar_prefetch=1, grid=(b, r // rb),
        in_specs=[pl.BlockSpec(blk, lambda j, i, s: (layer, j, i, 0))],
        out_specs=pl.BlockSpec(blk, lambda j, i, s: (j, s[0], i, 0)))
    return pl.pallas_call(body, name=name, grid_spec=grid_spec,
                          out_shape=jax.ShapeDtypeStruct((b, N_CHIPS, r, cols), BF16),
                          compiler_params=_params(("parallel", "parallel")))(s_idx, arr)


def _sum_devices(name, gathered):
    n_dev, p, d = gathered.shape

    def body(g_ref, o_ref):
        acc = g_ref[0]
        for j in range(1, n_dev):
            acc = acc + g_ref[j]
        o_ref[...] = acc

    return pl.pallas_call(body, name=name, out_shape=jax.ShapeDtypeStruct((p, d), F32),
                          compiler_params=_params())(gathered)


def _adamw(name, w, g, m, v):
    shape = w.shape
    cols = shape[-1]
    flat = lambda a: a.reshape(-1, cols)
    rows = flat(w).shape[0]
    bs = _tile(rows, 256)

    def body(w_ref, g_ref, m_ref, v_ref, d_ref, mo_ref, vo_ref):
        grad = g_ref[...]
        m_new = ADAM_B1 * m_ref[...] + (1.0 - ADAM_B1) * grad
        v_new = ADAM_B2 * v_ref[...] + (1.0 - ADAM_B2) * (grad * grad)
        m_hat = m_new / (1.0 - ADAM_B1 ** ADAM_STEP)
        v_hat = v_new / (1.0 - ADAM_B2 ** ADAM_STEP)
        d_ref[...] = -ADAM_LR * (m_hat / (jnp.sqrt(v_hat) + ADAM_EPS) + ADAM_WD * w_ref[...])
        mo_ref[...] = m_new
        vo_ref[...] = v_new

    spec = _row_spec(bs, cols)
    outs = _rows_call(body, name, rows, [spec] * 4, [spec] * 3, [jax.ShapeDtypeStruct((rows, cols), F32)] * 3, bs)(
        flat(w), flat(g), flat(m), flat(v))
    return tuple(o.reshape(shape) for o in outs)


def _place():
    x, y, c = lax.axis_index("x"), lax.axis_index("y"), lax.axis_index("c")
    chips = [(1 - x, y), (x, 1 - y), (1 - x, 1 - y)]
    return x, y, c, chips


def _chip_index(chip):
    return 2 * chip[0] + chip[1]


def _comm_call(body, name, n_in, out_shape, scratch, aliases=None):
    return pl.pallas_call(body, name=name, in_specs=[ANY] * n_in, out_specs=[ANY] * len(out_shape), out_shape=out_shape,
                          scratch_shapes=scratch, input_output_aliases=aliases or {})


HBM_SPEC = pl.BlockSpec(memory_space=pltpu.HBM)
SEM_SPEC = pl.BlockSpec(memory_space=pltpu.SEMAPHORE)
SPLIT_PARAMS = pltpu.CompilerParams(has_side_effects=pltpu.SideEffectType.DATAFLOW_SIDE_EFFECTING)


def _in_hbm(arr):
    return pltpu.with_memory_space_constraint(arr, pltpu.HBM)


def _slot_half(ref, chip, core):
    half = ref.shape[2] // 2
    return ref.at[:, _chip_index(chip), pl.ds(core * half, half), :]


def _gather_start(name, bufs, carry=()):
    n, n_c = len(bufs), len(carry)

    def body(*refs):
        ins, (send_sems, recv_sems) = refs[:n], refs[n + n_c:n + n_c + 2]
        x, y, c, chips = _place()
        for a in range(n):
            block = _slot_half(ins[a], (x, y), c)
            for k, chip in enumerate(chips):
                pltpu.make_async_remote_copy(src_ref=block, dst_ref=block, send_sem=send_sems.at[3 * a + k],
                                             recv_sem=recv_sems.at[3 * a + k], device_id=(*chip, c),
                                             device_id_type=MESH).start()

    dma = pltpu.SemaphoreType.DMA
    thru = list(bufs) + list(carry)
    res = pl.pallas_call(
        body, name=name, in_specs=[HBM_SPEC] * (n + n_c), out_specs=[SEM_SPEC] * 2 + [HBM_SPEC] * (n + n_c),
        out_shape=[dma((3 * n,)), dma((3 * n,))] + [pltpu.HBM(a.shape, a.dtype) for a in thru],
        input_output_aliases={t: 2 + t for t in range(n + n_c)}, compiler_params=SPLIT_PARAMS,
    )(*[_in_hbm(a) for a in thru])
    return (res[0], res[1]), list(res[2:2 + n]), list(res[2 + n:])


def _gather_wait(name, sems, bufs, after):
    n = len(bufs)

    def body(*refs):
        ins, (send_sems, recv_sems) = refs[:n], refs[n:n + 2]
        x, y, c, chips = _place()
        for a in range(n):
            for k, chip in enumerate(chips):
                mine, theirs = _slot_half(ins[a], (x, y), c), _slot_half(ins[a], chip, c)
                copy = pltpu.make_async_remote_copy(src_ref=mine, dst_ref=theirs, send_sem=send_sems.at[3 * a + k],
                                                    recv_sem=recv_sems.at[3 * a + k], device_id=(*chip, c),
                                                    device_id_type=MESH)
                copy.wait_send()
                copy.wait_recv()

    res = pl.pallas_call(
        body, name=name, in_specs=[HBM_SPEC] * n + [SEM_SPEC, SEM_SPEC, ANY], out_specs=[HBM_SPEC] * n,
        out_shape=[pltpu.HBM(b.shape, b.dtype) for b in bufs], input_output_aliases={a: a for a in range(n)},
        compiler_params=SPLIT_PARAMS)(*bufs, *sems, after)
    return list(res)


def _gather_forward(name, bufs, smalls=()):
    n, n_small = len(bufs), len(smalls)

    def body(*refs):
        small_in = refs[n:n + n_small]
        outs = refs[n + n_small:2 * n + n_small]
        small_out = refs[2 * n + n_small:2 * n + 2 * n_small]
        send_sems, recv_sems, s_send, s_recv, s_local = refs[-5:]
        x, y, c, chips = _place()
        me, sibling = _chip_index((x, y)), (x, y, 1 - c)

        def forward(t, k, core):
            block = _slot_half(outs[t], chips[k], core)
            return pltpu.make_async_remote_copy(src_ref=block, dst_ref=block, send_sem=send_sems.at[t, k],
                                                recv_sem=recv_sems.at[t, k], device_id=sibling, device_id_type=MESH)

        def small_copy(j, k, slot):
            return pltpu.make_async_remote_copy(src_ref=small_in[j], dst_ref=small_out[j].at[slot],
                                                send_sem=s_send.at[j, k], recv_sem=s_recv.at[j, k],
                                                device_id=(*chips[k], c), device_id_type=MESH)

        local = []
        for t in range(n):
            for k in range(3):
                forward(t, k, c).start()
        for j in range(n_small):
            own = pltpu.make_async_copy(small_in[j], small_out[j].at[me], s_local.at[j])
            own.start()
            local.append(own)
            for k in range(3):
                small_copy(j, k, me).start()
        for t in range(n):
            for k in range(3):
                forward(t, k, 1 - c).wait_recv()
        for j in range(n_small):
            for k in range(3):
                small_copy(j, k, _chip_index(chips[k])).wait_recv()
        for t in range(n):
            for k in range(3):
                forward(t, k, c).wait_send()
        for j in range(n_small):
            for k in range(3):
                small_copy(j, k, me).wait_send()
        for own in local:
            own.wait()

    out_shape = [jax.ShapeDtypeStruct(b.shape, BF16) for b in bufs]
    out_shape += [jax.ShapeDtypeStruct((N_CHIPS,) + s.shape, F32) for s in smalls]
    dma = pltpu.SemaphoreType.DMA
    n_s = max(n_small, 1)
    res = _comm_call(body, name, n + n_small, out_shape,
                     [dma((n, 3)), dma((n, 3)), dma((n_s, 3)), dma((n_s, 3)), dma((n_s,))],
                     aliases={t: t for t in range(n)})(*bufs, *smalls)
    return list(res[:n]), list(res[n:])


def _halves_copy(grad_ref, land_ref, send_sems, recv_sems, t):
    x, y, c, _ = _place()
    half = grad_ref.shape[1] // 2
    return pltpu.make_async_remote_copy(
        src_ref=grad_ref.at[:, pl.ds((1 - c) * half, half), :], dst_ref=land_ref, send_sem=send_sems.at[t],
        recv_sem=recv_sems.at[t], device_id=(x, y, 1 - c), device_id_type=MESH)


def _exchange_start(name, grads, carry=()):
    n = len(grads)
    lands = [lax.empty((g.shape[0], g.shape[1] // 2, g.shape[2]), BF16) for g in grads]

    def body(*refs):
        send_sems, recv_sems = refs[2 * n + len(carry):2 * n + len(carry) + 2]
        for t in range(n):
            _halves_copy(refs[t], refs[n + t], send_sems, recv_sems, t).start()

    dma = pltpu.SemaphoreType.DMA
    thru = list(grads) + lands + list(carry)
    res = pl.pallas_call(
        body, name=name, in_specs=[HBM_SPEC] * len(thru), out_specs=[SEM_SPEC] * 2 + [HBM_SPEC] * len(thru),
        out_shape=[dma((n,)), dma((n,))] + [pltpu.HBM(a.shape, a.dtype) for a in thru],
        input_output_aliases={t: 2 + t for t in range(len(thru))}, compiler_params=SPLIT_PARAMS,
    )(*[_in_hbm(a) for a in thru])
    return (res[0], res[1]), list(res[2:2 + n]), list(res[2 + n:2 + 2 * n]), list(res[2 + 2 * n:])


def _exchange_wait(name, sems, grads, lands, after):
    n = len(grads)

    def body(*refs):
        send_sems, recv_sems = refs[2 * n:2 * n + 2]
        for t in range(n):
            copy = _halves_copy(refs[t], refs[n + t], send_sems, recv_sems, t)
            copy.wait_send()
            copy.wait_recv()

    res = pl.pallas_call(
        body, name=name, in_specs=[HBM_SPEC] * (2 * n) + [SEM_SPEC, SEM_SPEC, ANY], out_specs=[HBM_SPEC] * (2 * n),
        out_shape=[pltpu.HBM(a.shape, a.dtype) for a in grads + lands],
        input_output_aliases={t: t for t in range(2 * n)}, compiler_params=SPLIT_PARAMS)(*grads, *lands, *sems, after)
    return list(res[:n]), list(res[n:])


def _scatter_copy(part_ref, land_ref, send_sems, recv_sems, t, k, chip, c):
    return pltpu.make_async_remote_copy(
        src_ref=part_ref.at[_chip_index(chip)], dst_ref=land_ref.at[k], send_sem=send_sems.at[3 * t + k],
        recv_sem=recv_sems.at[3 * t + k], device_id=(*chip, c), device_id_type=MESH)


def _scatter_start(name, parts, carry=()):
    n, n_c = len(parts), len(carry)
    lands = [lax.empty((3,) + p.shape[1:], BF16) for p in parts]

    def body(*refs):
        p_in, l_in = refs[:n], refs[n:2 * n]
        send_sems, recv_sems = refs[2 * n + n_c:2 * n + n_c + 2]
        x, y, c, chips = _place()
        for t in range(n):
            for k, chip in enumerate(chips):
                _scatter_copy(p_in[t], l_in[t], send_sems, recv_sems, t, k, chip, c).start()

    dma = pltpu.SemaphoreType.DMA
    thru = list(parts) + lands + list(carry)
    res = pl.pallas_call(
        body, name=name, in_specs=[HBM_SPEC] * len(thru), out_specs=[SEM_SPEC] * 2 + [HBM_SPEC] * len(thru),
        out_shape=[dma((3 * n,)), dma((3 * n,))] + [pltpu.HBM(a.shape, a.dtype) for a in thru],
        input_output_aliases={t: 2 + t for t in range(len(thru))}, compiler_params=SPLIT_PARAMS,
    )(*[_in_hbm(a) for a in thru])
    return (res[0], res[1]), list(res[2:2 + n]), list(res[2 + n:2 + 2 * n]), list(res[2 + 2 * n:])


def _scatter_wait(name, sems, parts, lands, after):
    n = len(parts)

    def body(*refs):
        p_in, l_in = refs[:n], refs[n:2 * n]
        send_sems, recv_sems = refs[2 * n:2 * n + 2]
        x, y, c, chips = _place()
        for t in range(n):
            for k, chip in enumerate(chips):
                copy = _scatter_copy(p_in[t], l_in[t], send_sems, recv_sems, t, k, chip, c)
                copy.wait_send()
                copy.wait_recv()

    hbm_out = lambda a: pltpu.HBM(a.shape, a.dtype)
    res = pl.pallas_call(
        body, name=name, in_specs=[HBM_SPEC] * (2 * n) + [SEM_SPEC, SEM_SPEC, ANY], out_specs=[HBM_SPEC] * (2 * n),
        out_shape=[hbm_out(a) for a in parts + lands], input_output_aliases={t: t for t in range(2 * n)},
        compiler_params=SPLIT_PARAMS)(*parts, *lands, *sems, after)
    return list(res[:n]), list(res[n:])


def _share_halves(fulls):
    n = len(fulls)
    items = [(a, l) for a in range(n) for l in range(fulls[a].shape[0])]

    def body(*refs):
        outs = refs[n:2 * n]
        send_sems, recv_sems = refs[-2:]
        x, y, c, _ = _place()

        def copy(t, core):
            a, l = items[t]
            half = outs[a].shape[1] // 2
            block = outs[a].at[l, pl.ds(core * half, half), :]
            return pltpu.make_async_remote_copy(src_ref=block, dst_ref=block, send_sem=send_sems.at[t],
                                                recv_sem=recv_sems.at[t], device_id=(x, y, 1 - c), device_id_type=MESH)

        for t in range(len(items)):
            copy(t, c).start()
        for t in range(len(items)):
            copy(t, 1 - c).wait_recv()
        for t in range(len(items)):
            copy(t, c).wait_send()

    out_shape = [jax.ShapeDtypeStruct(f.shape, F32) for f in fulls]
    dma = pltpu.SemaphoreType.DMA
    return list(_comm_call(body, "grad_share_halves", n, out_shape, [dma((len(items),)), dma((len(items),))],
                           aliases={a: a for a in range(n)})(*fulls))


def _allgather_small(packed):
    def body(p_ref, o_ref, send_sems, recv_sems, local_sem):
        x, y, c, _ = _place()
        me = 4 * x + 2 * y + c
        own = pltpu.make_async_copy(p_ref, o_ref.at[me], local_sem)
        own.start()
        flips = [(fx, fy, fc) for fx in (0, 1) for fy in (0, 1) for fc in (0, 1)][1:]
        peers = [(x ^ fx, y ^ fy, c ^ fc) for fx, fy, fc in flips]
        copies = []
        for k, peer in enumerate(peers):
            cp = pltpu.make_async_remote_copy(src_ref=p_ref, dst_ref=o_ref.at[me], send_sem=send_sems.at[k],
                                              recv_sem=recv_sems.at[k], device_id=peer, device_id_type=MESH)
            cp.start()
            copies.append(cp)
        for k, (px, py, pc) in enumerate(peers):
            pltpu.make_async_remote_copy(src_ref=p_ref, dst_ref=o_ref.at[4 * px + 2 * py + pc], send_sem=send_sems.at[k],
                                         recv_sem=recv_sems.at[k], device_id=peers[k], device_id_type=MESH).wait_recv()
        for cp in copies:
            cp.wait_send()
        own.wait()

    dma = pltpu.SemaphoreType.DMA
    return _comm_call(body, "small_allgather", 1, [jax.ShapeDtypeStruct((8,) + packed.shape, F32)],
                      [dma((7,)), dma((7,)), dma(())])(packed)[0]


PAD_ROWS = 8


def kernel(x, norm_a, w_in_a, w_grp_a, scale_a, w_out_a, norm_kv, w_k, w_v, norm_b, w_in_b, w_out_b, norm_f, loss_target, m_norm_a, m_w_in_a, m_w_grp_a, m_scale_a, m_w_out_a, m_norm_kv, m_w_k, m_w_v, m_norm_b, m_w_in_b, m_w_out_b, m_norm_f, v_norm_a, v_w_in_a, v_w_grp_a, v_scale_a, v_w_out_a, v_norm_kv, v_w_k, v_w_v, v_norm_b, v_w_in_b, v_w_out_b, v_norm_f):
    weights = dict(norm_a=norm_a, w_in_a=w_in_a, w_grp_a=w_grp_a, scale_a=scale_a, w_out_a=w_out_a, norm_kv=norm_kv,
                   w_k=w_k, w_v=w_v, norm_b=norm_b, w_in_b=w_in_b, w_out_b=w_out_b, norm_f=norm_f)
    moments_m = dict(norm_a=m_norm_a, w_in_a=m_w_in_a, w_grp_a=m_w_grp_a, scale_a=m_scale_a, w_out_a=m_w_out_a,
                     norm_kv=m_norm_kv, w_k=m_w_k, w_v=m_w_v, norm_b=m_norm_b, w_in_b=m_w_in_b, w_out_b=m_w_out_b,
                     norm_f=m_norm_f)
    moments_v = dict(norm_a=v_norm_a, w_in_a=v_w_in_a, w_grp_a=v_w_grp_a, scale_a=v_scale_a, w_out_a=v_w_out_a,
                     norm_kv=v_norm_kv, w_k=v_w_k, w_v=v_w_v, norm_b=v_norm_b, w_in_b=v_w_in_b, w_out_b=v_w_out_b,
                     norm_f=v_norm_f)
    names = list(weights)
    d = x.shape[-1]
    c_idx = lax.axis_index("c").astype(jnp.int32).reshape(1)
    s_me = 2 * lax.axis_index("x") + lax.axis_index("y")
    s_idx = s_me.astype(jnp.int32).reshape(1)

    def as_lbrc(name):
        a = weights[name]
        if name == "w_grp_a":
            return a
        if a.ndim == 2:
            return a.reshape(1, 1, *a.shape)
        return a.reshape(a.shape[0], 1, *a.shape[1:])

    n_a, n_b = norm_a.shape[0], norm_b.shape[0]
    group_weights = {**{f"a{i}": [("w_in_a", i), ("w_grp_a", i), ("w_out_a", i)] for i in range(n_a)},
                     "kv": [("w_k", 0), ("w_v", 0)],
                     **{f"b{i}": [("w_in_b", i), ("w_out_b", i)] for i in range(n_b)}}
    group_order = [f"a{i}" for i in range(n_a)] + ["kv"] + [f"b{i}" for i in range(n_b)]
    slots, slot_groups = [], []
    for group in group_order:
        slot_groups.append(list(range(len(slots), len(slots) + len(group_weights[group]))))
        slots += [_cast_into_slot(f"cast_{name}{l}", as_lbrc(name), l, s_idx) for name, l in group_weights[group]]
    small_full, started = {}, {}

    def start_group(gi, carry=()):
        sems, bufs, carry = _gather_start(f"gather_start_{group_order[gi]}", [slots[t] for t in slot_groups[gi]], carry)
        started[gi] = (sems, bufs)
        return carry

    start_group(0)

    def gathered_form(name, g):
        if name in ("w_in_a", "w_in_b"):
            return g[0]
        if name == "w_grp_a":
            return g.reshape(g.shape[0], -1, g.shape[-1])
        return g.reshape(-1, g.shape[-1])

    def fetch(group, after):
        if group == "head":
            return {"norm_f": (0, norm_f)}
        gi = group_order.index(group)
        sems, bufs = started[gi]
        bufs = _gather_wait(f"gather_wait_{group}", sems, bufs, after)
        bufs, small_g = _gather_forward(f"gather_forward_{group}", bufs, [norm_a, scale_a] if gi == 0 else [])
        out = {name: (l, gathered_form(name, g)) for (name, l), g in zip(group_weights[group], bufs)}
        if gi == 0:
            for name, g in zip(("norm_a", "scale_a"), small_g):
                small_full[name] = g.transpose(1, 0, 2).reshape(g.shape[1], -1)
        layer = group_weights[group][0][1]
        if group.startswith("a"):
            gain_name, gain = "norm_a", small_full["norm_a"][layer]
            out.update(scale_a=(layer, small_full["scale_a"][layer]))
        elif group == "kv":
            gain_name, gain = "norm_kv", norm_kv
        else:
            gain_name, gain = "norm_b", norm_b[layer]
        gain = gain.reshape(1, -1)
        ahead = [gi + 1] + ([gi + 2] if gi + 2 < len(group_order) and group_order[gi + 1] == "kv" else [])
        for gj in ahead:
            if gj < len(group_order) and gj not in started:
                (gain,) = start_group(gj, (gain,))
        out[gain_name] = (layer, gain)
        return out

    exchanging, in_flight = {}, []

    def begin(group, grads_of, carry):
        keys = [(k, grads_of[k][0]) for k in grads_of]
        sems, grads, lands, carry = _exchange_start(f"grad_exchange_start_{group}", [grads_of[k][1] for k in grads_of],
                                                    tuple(carry))
        exchanging[group] = (keys, sems, grads, lands)
        return carry

    def emit(group, carry):
        keys, sems, grads, lands = exchanging.pop(group)
        grads, recv1 = _exchange_wait(f"grad_exchange_wait_{group}", sems, grads, lands, after=carry[0])
        parts = [_pair_add(f"pair_add_{k}{l}", g, r, c_idx) for (k, l), g, r in zip(keys, grads, recv1)]
        sems, parts, lands, carry = _scatter_start(f"grad_scatter_start_{group}", parts, tuple(carry))
        in_flight.append((group, keys, sems, parts, lands))
        return carry

    loss_vec, grad_x, small = _local_step(x[0], loss_target[0], n_a, n_b, fetch, begin, emit)

    small_order = [("norm_a", i) for i in range(n_a)] + [("scale_a", i) for i in range(n_a)] + [("norm_kv", 0)] + \
                  [("norm_b", i) for i in range(norm_b.shape[0])] + [("norm_f", 0)]
    pad = lambda vec: jnp.pad(vec, ((0, PAD_ROWS - 1), (0, 0)))
    packed = jnp.concatenate([pad(loss_vec)] + [pad(small[n][i]) for n, i in small_order], axis=0)
    totals = _sum_devices("small_sum", _allgather_small(packed))
    loss = 0.5 * jnp.sum(totals[0]) / d
    small_tot = {}
    for j, (n, i) in enumerate(small_order):
        small_tot.setdefault(n, []).append(totals[PAD_ROWS * (j + 1)])
    grads = {}
    shard_w = norm_a.shape[1]
    for n in ("norm_a", "scale_a"):
        full = jnp.stack(small_tot[n])
        grads[n] = lax.dynamic_slice_in_dim(full, s_me * shard_w, shard_w, axis=1)
    grads["norm_kv"] = small_tot["norm_kv"][0]
    grads["norm_b"] = jnp.stack(small_tot["norm_b"])
    grads["norm_f"] = small_tot["norm_f"][0]

    sc_idx = jnp.concatenate([s_idx, c_idx])
    fulls = {name: None for name in BIG_WEIGHTS}
    for group, keys, sems, parts, lands in in_flight:
        parts, lands = _scatter_wait(f"grad_scatter_wait_{group}", sems, parts, lands, after=grad_x)
        for (name, i), p, r in zip(keys, parts, lands):
            n_layers = 1 if weights[name].ndim == 2 else weights[name].shape[0]
            fulls[name] = _final_add(f"final_add_{name}{i}", p, r, sc_idx, i, n_layers, into=fulls[name])
    shared = _share_halves([fulls[name] for name in BIG_WEIGHTS])
    for name, g in zip(BIG_WEIGHTS, shared):
        grads[name] = g.reshape(weights[name].shape)

    deltas, new_m, new_v = {}, {}, {}
    for n in names:
        shape = weights[n].shape
        as2d = (lambda a: a.reshape(1, -1)) if len(shape) == 1 else (lambda a: a)
        dl, mn, vn = _adamw(f"adamw_{n}", as2d(weights[n]), as2d(grads[n]), as2d(moments_m[n]), as2d(moments_v[n]))
        deltas[n], new_m[n], new_v[n] = dl.reshape(shape), mn.reshape(shape), vn.reshape(shape)

    return (loss, grad_x[None], *[grads[n] for n in names], *[deltas[n] for n in names],
            *[new_m[n] for n in names], *[new_v[n] for n in names])
```

```python
import functools
import math

import jax
import jax.numpy as jnp
from jax import lax
from jax.experimental import pallas as pl
from jax.experimental.pallas import tpu as pltpu

F32 = jnp.float32
BF16 = jnp.bfloat16

HEAD_DIM = 128
POOL_WINDOWS = (2, 4, 8, 16)
DILATED_PAIRS = ((128, 1), (512, 4), (2048, 16))
ROPE_THETA = 10000.0
RMS_EPS = 1e-6
NEG_INF = -1e30
N_CHIPS = 4

ADAM_LR = 0.001
ADAM_B1 = 0.9
ADAM_B2 = 0.999
ADAM_EPS = 1e-08
ADAM_WD = 0.01
ADAM_STEP = 10

VMEM_LIMIT_BYTES = 56 * 1024 * 1024
MESH = pl.DeviceIdType.MESH
ANY = pl.BlockSpec(memory_space=pl.ANY)


def _tile(n, pref):
    t = min(n, pref)
    assert n % t == 0, (n, pref)
    return t


def _params(sem=None):
    return pltpu.CompilerParams(dimension_semantics=sem, vmem_limit_bytes=VMEM_LIMIT_BYTES)


def _mm(name, a, b, *, grid2, nk, a_blk, a_map, b_blk, b_map, outs, dims, epi=None, epi_in=(), epi_specs=(),
        acc_shape=None, epi_scratch=(), into=None):
    n_epi, n_out = len(epi_in), len(outs)

    def body(*refs):
        a_ref, b_ref = refs[0], refs[1]
        e_refs = refs[2:2 + n_epi]
        first_out = 2 + n_epi + (0 if into is None else 1)
        o_refs = refs[first_out:first_out + n_out]
        s_refs = refs[first_out + n_out + (0 if nk == 1 else 1):]

        def contrib():
            a_val = a_ref[...]
            if a_val.ndim == 3:
                a_val = a_val.reshape(-1, a_val.shape[-1])
            return lax.dot_general(a_val, b_ref[...], (dims, ((), ())), preferred_element_type=F32)

        def finish(acc):
            if epi is None:
                o_refs[0][...] = acc.reshape(o_refs[0].shape).astype(o_refs[0].dtype)
            else:
                epi(acc, e_refs, o_refs, s_refs)

        if nk == 1:
            finish(contrib())
        else:
            acc_ref = refs[first_out + n_out]
            k = pl.program_id(2)

            @pl.when(k == 0)
            def _():
                acc_ref[...] = contrib()

            @pl.when(k > 0)
            def _():
                acc_ref[...] += contrib()

            @pl.when(k == nk - 1)
            def _():
                finish(acc_ref[...])

    scratch = ([] if nk == 1 else [pltpu.VMEM(acc_shape, F32)]) + list(epi_scratch)
    extra_in, extra_specs, aliases = (), (), {}
    if into is not None:
        extra_in, extra_specs, aliases = (into[0],), (ANY,), {2 + n_epi: into[1]}
    res = pl.pallas_call(
        body, name=name, grid=(grid2[0], grid2[1], nk),
        in_specs=[pl.BlockSpec(a_blk, a_map), pl.BlockSpec(b_blk, b_map), *epi_specs, *extra_specs],
        out_specs=[pl.BlockSpec(blk, imap) for _, blk, imap, _ in outs],
        out_shape=[jax.ShapeDtypeStruct(shape, dtype) for shape, _, _, dtype in outs],
        scratch_shapes=scratch, input_output_aliases=aliases,
        compiler_params=_params(("parallel", "parallel", "arbitrary")),
    )(a, b, *epi_in, *extra_in)
    return res[0] if n_out == 1 else tuple(res)


NN = ((1,), (0,))
NT = ((1,), (1,))
TN = ((0,), (0,))


def _rope_apply(t, cos, sin):
    return t * cos + pltpu.roll(t, HEAD_DIM // 2, 1) * sin


def _epi_add(acc, e_refs, o_refs, s_refs):
    o_refs[0][...] = (acc + e_refs[0][...]).astype(o_refs[0].dtype)


def _col_blocks(width):
    return [slice(c * HEAD_DIM, (c + 1) * HEAD_DIM) for c in range(width // HEAD_DIM)]


def _col_scratch(rows, width):
    return pltpu.VMEM((width // HEAD_DIM, rows, HEAD_DIM), F32)


def _to_residue_major(o_ref, scr, d, sl):
    if d == 1:
        o_ref[0, :, sl] = scr[...].astype(o_ref.dtype)
        return
    rows = scr.shape[0] // d
    for r in range(d):
        o_ref[r, :, sl] = scr[pl.ds(r, rows, stride=d), :].astype(o_ref.dtype)


def _from_residue_major(i_ref, scr, d, sl):
    if d == 1:
        return i_ref[0, :, sl].astype(F32)
    rows = i_ref.shape[1]
    for r in range(d):
        scr[pl.ds(r, rows, stride=d), :] = i_ref[r, :, sl].astype(F32)
    return scr[...]


def _make_epi_orders(dils, rope_scale):
    def epi(acc, e_refs, o_refs, s_refs):
        if rope_scale is not None:
            cos = e_refs[0][...]
            sin = e_refs[1][...]
        for c, sl in enumerate(_col_blocks(acc.shape[1])):
            scr = s_refs[0].at[c]
            scr[...] = acc[:, sl] if rope_scale is None else _rope_apply(acc[:, sl], cos, sin) * rope_scale
            for o_ref, d in zip(o_refs, dils):
                _to_residue_major(o_ref, scr, d, sl)
    return epi


def _make_epi_token_order(d, has_add):
    def epi(acc, e_refs, o_refs, s_refs):
        o_ref = o_refs[0]
        if d == 1:
            o_ref[...] = acc + e_refs[0][...] if has_add else acc
            return
        rows = acc.shape[0] // d
        for c, sl in enumerate(_col_blocks(acc.shape[1])):
            scr = s_refs[0].at[c]
            for r in range(d):
                scr[pl.ds(r, rows, stride=d), :] = acc[r * rows:(r + 1) * rows, sl]
            o_ref[:, sl] = scr[...] + e_refs[0][:, sl] if has_add else scr[...]
    return epi


def _mm_act_w(name, a, w, *, out_dtype=BF16, add=None, rope=None, n_first=0, n_cols=None, dils=None):
    s_len, k_len = a.shape
    bm = _tile(s_len, 1024)
    epi, epi_in, epi_specs, epi_scratch = None, (), (), ()
    if w.ndim == 3:
        ns, _, c = w.shape
        ns_used = ns if n_cols is None else n_cols
        bn = _tile(c, 1024)
        sub = c // bn
        grid2 = (ns_used * sub, s_len // bm)
        b_blk, b_map = (None, k_len, bn), (lambda j, i, k: (j // sub + n_first, 0, j % sub))
        n_len = ns_used * c
    else:
        n_len = w.shape[1]
        bn = _tile(n_len, 1024)
        grid2 = (n_len // bn, s_len // bm)
        b_blk, b_map = (k_len, bn), (lambda j, i, k: (0, j))
    if add is not None:
        epi, epi_in = _epi_add, (add,)
        epi_specs = (pl.BlockSpec((bm, bn), lambda j, i, k: (i, j)),)
    outs = [((s_len, n_len), (bm, bn), lambda j, i, k: (i, j), out_dtype)]
    if dils is not None:
        if rope is not None:
            epi_in = rope[:2]
            epi_specs = (pl.BlockSpec((bm, HEAD_DIM), lambda j, i, k: (i, 0)),) * 2
        epi = _make_epi_orders(dils, None if rope is None else rope[2])
        epi_scratch = (_col_scratch(bm, bn),)
        outs = [((d, s_len // d, n_len), (d, bm // d, bn), lambda j, i, k: (0, i, j), BF16) for d in dils]
    res = _mm(name, a, w, grid2=grid2, nk=1, a_blk=(bm, k_len), a_map=lambda j, i, k: (i, 0),
              b_blk=b_blk, b_map=b_map, outs=outs, dims=NN, epi=epi, epi_in=epi_in, epi_specs=epi_specs,
              epi_scratch=epi_scratch)
    return (res,) if dils is not None and len(dils) == 1 else res


def _mm_grad_act(name, dy, w, *, add=None, slot=None):
    if slot is not None:
        d = 1 if dy.ndim == 2 else dy.shape[0]
        s_len = dy.shape[-2] * d
        _, k_len, c = w.shape
        bm, bn = _tile(s_len, 1024), _tile(k_len, 1024)
        a_blk, a_map = ((bm, c), lambda j, i, k: (i, 0)) if dy.ndim == 2 else ((d, bm // d, c), lambda j, i, k: (0, i, 0))
        epi_in = () if add is None else (add,)
        return _mm(name, dy, w, grid2=(k_len // bn, s_len // bm), nk=1, a_blk=a_blk, a_map=a_map,
                   b_blk=(None, bn, c), b_map=lambda j, i, k: (slot, j, 0),
                   outs=[((s_len, k_len), (bm, bn), lambda j, i, k: (i, j), F32)], dims=NT,
                   epi=_make_epi_token_order(d, add is not None), epi_in=epi_in,
                   epi_specs=(pl.BlockSpec((bm, bn), lambda j, i, k: (i, j)),) * len(epi_in),
                   epi_scratch=(_col_scratch(bm, bn),) if d > 1 else ())
    s_len, n_len = dy.shape
    bm = _tile(s_len, 1024)
    if w.ndim == 3:
        ns, k_len, c = w.shape
        bk, nk = c, ns
        bn = _tile(k_len, 1024)
        b_blk, b_map = (None, bn, c), (lambda j, i, k: (k, j, 0))
    else:
        k_len = w.shape[0]
        bk = _tile(n_len, 2048)
        nk = n_len // bk
        bn = _tile(k_len, 1024)
        b_blk, b_map = (bn, bk), (lambda j, i, k: (j, k))
    epi, epi_in, epi_specs = None, (), ()
    if add is not None:
        epi, epi_in = _epi_add, (add,)
        epi_specs = (pl.BlockSpec((bm, bn), lambda j, i, k: (i, j)),)
    return _mm(name, dy, w, grid2=(k_len // bn, s_len // bm), nk=nk, a_blk=(bm, bk), a_map=lambda j, i, k: (i, k),
               b_blk=b_blk, b_map=b_map, outs=[((s_len, k_len), (bm, bn), lambda j, i, k: (i, j), F32)],
               dims=NT, epi=epi, epi_in=epi_in, epi_specs=epi_specs, acc_shape=(bm, bn))


def _mm_grad_w(name, a, dy, *, col_shards=None, slot=None, into=None):
    s_len, k_len = a.shape
    n_len = dy.shape[1]
    bk = _tile(s_len, 2048)
    bm = _tile(k_len, 1024)
    if slot is not None:
        bn = _tile(n_len, 1024)
        out = ((col_shards, k_len, n_len), (None, bm, bn), lambda j, i, k: (slot, i, j), BF16)
    elif col_shards:
        c = n_len // col_shards
        bn = _tile(c, 1024)
        sub = c // bn
        out = ((col_shards, k_len, c), (None, bm, bn), lambda j, i, k: (j // sub, i, j % sub), BF16)
    else:
        bn = _tile(n_len, 1024)
        out = ((k_len, n_len), (bm, bn), lambda j, i, k: (i, j), BF16)
    return _mm(name, a, dy, grid2=(n_len // bn, k_len // bm), nk=s_len // bk,
               a_blk=(bk, bm), a_map=lambda j, i, k: (k, i), b_blk=(bk, bn), b_map=lambda j, i, k: (k, j),
               outs=[out], dims=TN, acc_shape=(bm, bn), into=None if into is None else (into, 0))


def _mm_grp_fwd(name, pooled, wg):
    s_len, e = pooled.shape
    ng, g, _ = wg.shape
    bm = _tile(s_len, 1024)
    return _mm(name, pooled, wg, grid2=(ng, s_len // bm), nk=1, a_blk=(bm, g), a_map=lambda j, i, k: (i, j),
               b_blk=(None, g, g), b_map=lambda j, i, k: (j, 0, 0),
               outs=[((s_len, e), (bm, g), lambda j, i, k: (i, j), F32)], dims=NN)


def _mm_grp_grad_act(name, dy, wg):
    s_len, e = dy.shape
    ng, g, _ = wg.shape
    bm = _tile(s_len, 1024)
    return _mm(name, dy, wg, grid2=(ng, s_len // bm), nk=1, a_blk=(bm, g), a_map=lambda j, i, k: (i, j),
               b_blk=(None, g, g), b_map=lambda j, i, k: (j, 0, 0),
               outs=[((s_len, e), (bm, g), lambda j, i, k: (i, j), F32)], dims=NT)


def _mm_grp_grad_w(name, pooled, dy, ng):
    s_len, e = pooled.shape
    g = e // ng
    bk = _tile(s_len, 1024)
    return _mm(name, pooled, dy, grid2=(ng, 1), nk=s_len // bk, a_blk=(bk, g), a_map=lambda j, i, k: (k, j),
               b_blk=(bk, g), b_map=lambda j, i, k: (k, j),
               outs=[((N_CHIPS, ng, g // N_CHIPS, g), (N_CHIPS, None, g // N_CHIPS, g), lambda j, i, k: (0, j, 0, 0), BF16)],
               dims=TN, acc_shape=(g, g))


def _row_spec(bs, width, col=0):
    return pl.BlockSpec((bs, width), lambda i: (i, col))


def _vec_spec(width):
    return pl.BlockSpec((1, width), lambda i: (0, 0))


def _rows_call(body, name, s_len, in_specs, out_specs, out_shape, bs, aliases=None, sequential=False):
    return pl.pallas_call(
        body, name=name, grid=(s_len // bs,), in_specs=in_specs, out_specs=out_specs, out_shape=out_shape,
        input_output_aliases=aliases or {},
        compiler_params=_params(("arbitrary",) if sequential else ("parallel",)))


def _accumulate(ref, part):
    i = pl.program_id(0)

    @pl.when(i == 0)
    def _():
        ref[...] = part

    @pl.when(i > 0)
    def _():
        ref[...] += part


def _rms_scale(xf):
    return lax.rsqrt(jnp.mean(xf * xf, axis=-1, keepdims=True) + RMS_EPS)


def _res_spec(dil, bs, width):
    return pl.BlockSpec((dil, bs // dil, width), lambda i: (0, i, 0))


def _res_shape(dil, s_len, width, dtype):
    return jax.ShapeDtypeStruct((dil, s_len // dil, width), dtype)


def _rmsnorm_fwd(name, x, gain, dils=()):
    s_len, d = x.shape
    bs = _tile(s_len, 256)

    def body(x_ref, g_ref, h_ref, *rest):
        xf = x_ref[...]
        h = (xf * _rms_scale(xf)) * g_ref[...]
        h_ref[...] = h.astype(BF16)
        if dils:
            for c, sl in enumerate(_col_blocks(d)):
                scr = rest[-1].at[c]
                scr[...] = h[:, sl]
                for o_ref, dil in zip(rest[:-1], dils):
                    _to_residue_major(o_ref, scr, dil, sl)

    res = pl.pallas_call(
        body, name=name, grid=(s_len // bs,), in_specs=[_row_spec(bs, d), _vec_spec(d)],
        out_specs=[_row_spec(bs, d)] + [_res_spec(dil, bs, d) for dil in dils],
        out_shape=[jax.ShapeDtypeStruct((s_len, d), BF16)] + [_res_shape(dil, s_len, d, BF16) for dil in dils],
        scratch_shapes=[_col_scratch(bs, d)] if dils else [],
        compiler_params=_params(("parallel",)))(x, gain)
    return res[0] if not dils else tuple(res)


def _rmsnorm_bwd(name, x, gain, dh, dres):
    s_len, d = x.shape
    bs = _tile(s_len, 256)

    def body(x_ref, g_ref, dh_ref, dres_ref, dx_ref, dxb_ref, dg_ref):
        xf = x_ref[...]
        r = _rms_scale(xf)
        xh = xf * r
        dh_f = dh_ref[...]
        t = dh_f * g_ref[...]
        dx = dres_ref[...] + r * (t - xh * jnp.mean(t * xh, axis=-1, keepdims=True))
        dx_ref[...] = dx
        dxb_ref[...] = dx.astype(BF16)
        _accumulate(dg_ref, jnp.sum(dh_f * xh, axis=0, keepdims=True))

    return _rows_call(
        body, name, s_len,
        [_row_spec(bs, d), _vec_spec(d), _row_spec(bs, d), _row_spec(bs, d)],
        [_row_spec(bs, d), _row_spec(bs, d), _vec_spec(d)],
        [jax.ShapeDtypeStruct((s_len, d), F32), jax.ShapeDtypeStruct((s_len, d), BF16),
         jax.ShapeDtypeStruct((1, d), F32)], bs, sequential=True)(x, gain, dh, dres)


def _loss_head(name, x, gain, target):
    s_len, d = x.shape
    bs = _tile(s_len, 256)

    def body(x_ref, g_ref, t_ref, lv_ref, dx_ref, dxb_ref, dg_ref):
        xf = x_ref[...]
        r = _rms_scale(xf)
        xh = xf * r
        err = xh * g_ref[...] - t_ref[...]
        dy = err * (1.0 / d)
        t = dy * g_ref[...]
        dx = r * (t - xh * jnp.mean(t * xh, axis=-1, keepdims=True))
        dx_ref[...] = dx
        dxb_ref[...] = dx.astype(BF16)
        _accumulate(lv_ref, jnp.sum(err * err, axis=0, keepdims=True))
        _accumulate(dg_ref, jnp.sum(dy * xh, axis=0, keepdims=True))

    return _rows_call(
        body, name, s_len, [_row_spec(bs, d), _vec_spec(d), _row_spec(bs, d)],
        [_vec_spec(d), _row_spec(bs, d), _row_spec(bs, d), _vec_spec(d)],
        [jax.ShapeDtypeStruct((1, d), F32), jax.ShapeDtypeStruct((s_len, d), F32),
         jax.ShapeDtypeStruct((s_len, d), BF16), jax.ShapeDtypeStruct((1, d), F32)],
        bs, sequential=True)(x, gain, target)


def _sigmoid(g):
    return 1.0 / (1.0 + jnp.exp(-g))


def _gate_a_fwd(name, ypre, proj, scale):
    s_len, e = ypre.shape
    bs = _tile(s_len, 256)

    def body(y_ref, g_ref, sc_ref, z_ref):
        g = g_ref[...]
        z_ref[...] = (y_ref[...] * sc_ref[...] * (g * _sigmoid(g))).astype(BF16)

    return _rows_call(body, name, s_len, [_row_spec(bs, e), _row_spec(bs, e, 1), _vec_spec(e)], _row_spec(bs, e),
                      jax.ShapeDtypeStruct((s_len, e), BF16), bs)(ypre, proj, scale)


def _gate_a_bwd(name, dz, ypre, proj, scale):
    s_len, e = ypre.shape
    bs = _tile(s_len, 256)

    def body(dz_ref, y_ref, g_ref, sc_ref, dy_ref, dproj_ref, dsc_ref):
        g = g_ref[...]
        sg = _sigmoid(g)
        silu = g * sg
        dz_f = dz_ref[...]
        ypre_f = y_ref[...]
        dys = dz_f * silu
        dy_ref[...] = (dys * sc_ref[...]).astype(BF16)
        dproj_ref[...] = (dz_f * (ypre_f * sc_ref[...]) * (sg * (1.0 + g * (1.0 - sg)))).astype(BF16)
        _accumulate(dsc_ref, jnp.sum(dys * ypre_f, axis=0, keepdims=True))

    return _rows_call(
        body, name, s_len, [_row_spec(bs, e), _row_spec(bs, e), _row_spec(bs, e, 1), _vec_spec(e)],
        [_row_spec(bs, e), _row_spec(bs, e, 1), _vec_spec(e)],
        [jax.ShapeDtypeStruct((s_len, e), BF16), jax.ShapeDtypeStruct((s_len, 2 * e), BF16),
         jax.ShapeDtypeStruct((1, e), F32)], bs, sequential=True)(dz, ypre, proj, scale)


def _merge_gate_fwd(name, outs, lses, gate, dils):
    s_len, e = gate.shape
    bs = _tile(s_len, 256)
    n = len(outs)

    def body(*refs):
        o_refs, l_refs, g_ref = refs[:n], refs[n:2 * n], refs[2 * n]
        m_ref, lj_ref, z_ref = refs[2 * n + 1:2 * n + 4]
        scratch = refs[2 * n + 4]
        for c, sl in enumerate(_col_blocks(e)):
            ls = [_from_residue_major(r, scratch.at[2 * j, c], dil, sl) for j, (r, dil) in enumerate(zip(l_refs, dils))]
            os_ = [_from_residue_major(r, scratch.at[2 * j + 1, c], dil, sl) for j, (r, dil) in enumerate(zip(o_refs, dils))]
            mx = functools.reduce(jnp.maximum, ls)
            ws = [jnp.exp(l - mx) for l in ls]
            den = functools.reduce(lambda a, b: a + b, ws)
            merged = functools.reduce(lambda a, b: a + b, [w * o for w, o in zip(ws, os_)]) / den
            g = g_ref[:, sl]
            m_ref[:, sl] = merged.astype(BF16)
            lj_ref[:, sl] = mx + jnp.log(den)
            z_ref[:, sl] = (merged * (g * _sigmoid(g))).astype(BF16)

    spec = _row_spec(bs, e)
    res_specs = [_res_spec(dil, bs, e) for dil in dils]
    return pl.pallas_call(
        body, name=name, grid=(s_len // bs,), in_specs=res_specs + res_specs + [spec], out_specs=[spec] * 3,
        out_shape=[jax.ShapeDtypeStruct((s_len, e), BF16), jax.ShapeDtypeStruct((s_len, e), F32),
                   jax.ShapeDtypeStruct((s_len, e), BF16)],
        scratch_shapes=[pltpu.VMEM((2 * n, e // HEAD_DIM, bs, HEAD_DIM), F32)],
        compiler_params=_params(("parallel",)))(*outs, *lses, gate)


def _gate_b_bwd(name, dz, merged, gate, lse, dils):
    s_len, e = gate.shape
    bs = _tile(s_len, 256)
    n = len(dils)

    def body(dz_ref, m_ref, g_ref, l_ref, dg_ref, *rest):
        out_refs, scratch = rest[:3 * n], rest[3 * n]
        for c, sl in enumerate(_col_blocks(e)):
            g = g_ref[:, sl]
            sg = _sigmoid(g)
            dz_f = dz_ref[:, sl]
            merged = m_ref[:, sl].astype(F32)
            dmerged = dz_f * (g * sg)
            dg_ref[:, sl] = (dz_f * merged * (sg * (1.0 + g * (1.0 - sg)))).astype(BF16)
            values = (dmerged, l_ref[:, sl],
                      jnp.broadcast_to(jnp.sum(dmerged * merged, axis=-1, keepdims=True), (bs, HEAD_DIM)))
            for t, val in enumerate(values):
                scr = scratch.at[t, c]
                scr[...] = val
                for j, dil in enumerate(dils):
                    _to_residue_major(out_refs[3 * j + t], scr, dil, sl)

    spec = _row_spec(bs, e)
    out_specs, out_shape = [spec], [jax.ShapeDtypeStruct((s_len, e), BF16)]
    for dil in dils:
        out_specs += [_res_spec(dil, bs, e)] * 3
        out_shape += [_res_shape(dil, s_len, e, BF16), _res_shape(dil, s_len, e, F32), _res_shape(dil, s_len, e, F32)]
    res = pl.pallas_call(
        body, name=name, grid=(s_len // bs,), in_specs=[spec] * 4, out_specs=out_specs, out_shape=out_shape,
        scratch_shapes=[pltpu.VMEM((3, e // HEAD_DIM, bs, HEAD_DIM), F32)],
        compiler_params=_params(("parallel",)))(dz, merged, gate, lse)
    return res[0], [tuple(res[1 + 3 * j:4 + 3 * j]) for j in range(n)]


def _kv_grad_prep(name, dk_accs, dv_accs, dils, cos, sin_inv):
    n = len(dils)
    e = dk_accs[0].shape[-1]
    s_len = dk_accs[0].shape[0] * dk_accs[0].shape[1]
    bs = _tile(s_len, 256)

    def body(*refs):
        dk_refs, dv_refs = refs[:n], refs[n:2 * n]
        c_ref, s_ref, dkb_ref, dvb_ref, scratch = refs[2 * n:]
        cos_t, sin_t = c_ref[...], s_ref[...]
        add = lambda a, b: a + b
        for c, sl in enumerate(_col_blocks(e)):
            dk = functools.reduce(add, [_from_residue_major(r, scratch.at[j, c], dil, sl)
                                        for j, (r, dil) in enumerate(zip(dk_refs, dils))])
            dkb_ref[:, sl] = _rope_apply(dk, cos_t, sin_t).astype(BF16)
            dv = functools.reduce(add, [_from_residue_major(r, scratch.at[n + j, c], dil, sl)
                                        for j, (r, dil) in enumerate(zip(dv_refs, dils))])
            dvb_ref[:, sl] = dv.astype(BF16)

    spec, rspec = _row_spec(bs, e), _row_spec(bs, HEAD_DIM)
    res_specs = [_res_spec(dil, bs, e) for dil in dils]
    return pl.pallas_call(
        body, name=name, grid=(s_len // bs,), in_specs=res_specs + res_specs + [rspec, rspec], out_specs=[spec, spec],
        out_shape=[jax.ShapeDtypeStruct((s_len, e), BF16)] * 2,
        scratch_shapes=[pltpu.VMEM((2 * n, e // HEAD_DIM, bs, HEAD_DIM), F32)],
        compiler_params=_params(("parallel",)))(*dk_accs, *dv_accs, cos, sin_inv)


def _pool_cols(e):
    return _tile(e // len(POOL_WINDOWS), 256)


def _window_sum(val, grp, s_len, forward):
    rows = lax.broadcasted_iota(jnp.int32, val.shape, 0)
    acc = val
    for level in range(len(POOL_WINDOWS)):
        step = 1 << level
        if forward:
            shifted = jnp.where(rows >= step, pltpu.roll(acc, step, 0), 0.0)
        else:
            shifted = jnp.where(rows < s_len - step, pltpu.roll(acc, s_len - step, 0), 0.0)
        acc = jnp.where(level <= grp, acc + shifted, acc)
    return acc


def _window_count(shape, grp):
    rows = lax.broadcasted_iota(jnp.int32, shape, 0)
    return jnp.minimum(rows + 1, jnp.left_shift(2, grp)).astype(F32)


def _pool_fwd(name, proj):
    s_len, e2 = proj.shape
    e = e2 // 2
    cb = _pool_cols(e)
    per_grp = e // len(POOL_WINDOWS) // cb
    assert POOL_WINDOWS == tuple(2 << g for g in range(len(POOL_WINDOWS)))

    def body(u_ref, p_ref):
        grp = pl.program_id(0)
        u = u_ref[...]
        total = _window_sum(u, grp, s_len, True)
        p_ref[...] = (total / _window_count(u.shape, grp) - u).astype(BF16)

    spec = pl.BlockSpec((s_len, cb), lambda g, c: (0, g * per_grp + c))
    return pl.pallas_call(
        body, name=name, grid=(len(POOL_WINDOWS), per_grp), in_specs=[spec], out_specs=spec,
        out_shape=jax.ShapeDtypeStruct((s_len, e), BF16), compiler_params=_params(("parallel", "parallel")))(proj)


def _pool_bwd(name, dpooled, dproj):
    s_len, e = dpooled.shape
    cb = _pool_cols(e)
    per_grp = e // len(POOL_WINDOWS) // cb

    def body(dp_ref, _, du_ref):
        grp = pl.program_id(0)
        dp = dp_ref[...]
        total = _window_sum(dp / _window_count(dp.shape, grp), grp, s_len, False)
        du_ref[...] = (total - dp).astype(BF16)

    spec = pl.BlockSpec((s_len, cb), lambda g, c: (0, g * per_grp + c))
    return pl.pallas_call(
        body, name=name, grid=(len(POOL_WINDOWS), per_grp), in_specs=[spec, ANY], out_specs=spec,
        out_shape=jax.ShapeDtypeStruct(dproj.shape, BF16), input_output_aliases={1: 0},
        compiler_params=_params(("parallel", "parallel")))(dpooled, dproj)


def _band_masks(nb, first):
    row = lax.broadcasted_iota(jnp.int32, (nb, nb), 0)
    col = lax.broadcasted_iota(jnp.int32, (nb, nb), 1)
    return col >= row + jnp.where(first, 2 * nb, 0), col <= row


def _dot(a, b, dims):
    return lax.dot_general(a, b, (dims, ((), ())), preferred_element_type=F32)


def _attn_fwd(name, window, q, k, v):
    dil, m, e = k.shape
    nb = window // dil
    nblk = m // nb
    heads = e // HEAD_DIM

    def body(q_ref, kc_ref, vc_ref, o_ref, l_ref, kp_ref, vp_ref):
        first = pl.program_id(1) == 0

        @pl.when(first)
        def _():
            kp_ref[...] = jnp.zeros_like(kp_ref)
            vp_ref[...] = jnp.zeros_like(vp_ref)

        mask_p, mask_c = _band_masks(nb, first)
        cols = _col_blocks(e)
        s_p = [jnp.where(mask_p, _dot(q_ref[:, sl], kp_ref[:, sl], NT), NEG_INF) for sl in cols]
        s_c = [jnp.where(mask_c, _dot(q_ref[:, sl], kc_ref[:, sl], NT), NEG_INF) for sl in cols]
        mx = [jnp.maximum(jnp.max(a, axis=-1, keepdims=True), jnp.max(b, axis=-1, keepdims=True))
              for a, b in zip(s_p, s_c)]
        p_p = [jnp.exp(a - m) for a, m in zip(s_p, mx)]
        p_c = [jnp.exp(a - m) for a, m in zip(s_c, mx)]
        den = [jnp.sum(a, axis=-1, keepdims=True) + jnp.sum(b, axis=-1, keepdims=True) for a, b in zip(p_p, p_c)]
        for h, sl in enumerate(cols):
            out = _dot(p_p[h].astype(BF16), vp_ref[:, sl], NN) + _dot(p_c[h].astype(BF16), vc_ref[:, sl], NN)
            o_ref[:, sl] = (out / den[h]).astype(BF16)
            l_ref[:, sl] = jnp.broadcast_to(mx[h] + jnp.log(den[h]), (nb, HEAD_DIM))
        kp_ref[...] = kc_ref[...]
        vp_ref[...] = vc_ref[...]

    blk = (None, nb, e)
    cur = lambda r, n: (r, n, 0)
    return pl.pallas_call(
        body, name=name, grid=(dil, nblk),
        in_specs=[pl.BlockSpec(blk, cur)] * 3,
        out_specs=[pl.BlockSpec(blk, cur), pl.BlockSpec(blk, cur)],
        out_shape=[jax.ShapeDtypeStruct((dil, m, e), BF16), jax.ShapeDtypeStruct((dil, m, e), F32)],
        scratch_shapes=[pltpu.VMEM((nb, e), BF16), pltpu.VMEM((nb, e), BF16)],
        compiler_params=_params(("parallel", "arbitrary")),
    )(q, k, v)


def _attn_bwd(name, window, scale, q, k, v, dout, lse, delta, cos, sin_inv, dk_acc, dv_acc):
    dil, m, e = k.shape
    nb = window // dil
    nblk = m // nb
    heads = e // HEAD_DIM

    accumulate = dk_acc is not None

    def body(k_ref, v_ref, q0_ref, qn_ref, do0_ref, don_ref, l0_ref, ln_ref, dl0_ref, dln_ref, c_ref, s_ref, *rest):
        if accumulate:
            dki_ref, dvi_ref = rest[:2]
            rest = rest[2:]
        dq_ref, dko_ref, dvo_ref, carry_ref, qc_ref, doc_ref, lc_ref, dlc_ref = rest
        n = pl.program_id(1)

        @pl.when(n == 0)
        def _():
            carry_ref[...] = jnp.zeros_like(carry_ref)
            qc_ref[...] = q0_ref[...]
            doc_ref[...] = do0_ref[...]
            lc_ref[...] = l0_ref[...]
            dlc_ref[...] = dl0_ref[...]

        mask_n, mask_c = _band_masks(nb, n == nblk - 1)
        cos_t, sin_t = c_ref[...], s_ref[...]
        cols = _col_blocks(e)
        stat = lambda ref, sl: ref[:, sl] if nb == HEAD_DIM else ref[:, sl][:, :1]
        s_c = [_dot(qc_ref[:, sl], k_ref[:, sl], NT) for sl in cols]
        s_n = [_dot(qn_ref[:, sl], k_ref[:, sl], NT) for sl in cols]
        dp_c = [_dot(doc_ref[:, sl], v_ref[:, sl], NT) for sl in cols]
        dp_n = [_dot(don_ref[:, sl], v_ref[:, sl], NT) for sl in cols]
        p_c = [jnp.where(mask_c, jnp.exp(s - stat(lc_ref, sl)), 0.0) for s, sl in zip(s_c, cols)]
        p_n = [jnp.where(mask_n, jnp.exp(s - stat(ln_ref, sl)), 0.0) for s, sl in zip(s_n, cols)]
        ds_c = [(p * (dp - stat(dlc_ref, sl))).astype(BF16) for p, dp, sl in zip(p_c, dp_c, cols)]
        ds_n = [(p * (dp - stat(dln_ref, sl))).astype(BF16) for p, dp, sl in zip(p_n, dp_n, cols)]
        for h, sl in enumerate(cols):
            dq = (carry_ref[:, sl] + _dot(ds_c[h], k_ref[:, sl], NN)) * scale
            dq_ref[:, sl] = _rope_apply(dq, cos_t, sin_t).astype(BF16)
        for h, sl in enumerate(cols):
            carry_ref[:, sl] = _dot(ds_n[h], k_ref[:, sl], NN)
        for h, sl in enumerate(cols):
            dk = _dot(ds_c[h], qc_ref[:, sl], TN) + _dot(ds_n[h], qn_ref[:, sl], TN)
            dv = _dot(p_c[h].astype(BF16), doc_ref[:, sl], TN) + _dot(p_n[h].astype(BF16), don_ref[:, sl], TN)
            dko_ref[:, sl] = dki_ref[:, sl] + dk if accumulate else dk
            dvo_ref[:, sl] = dvi_ref[:, sl] + dv if accumulate else dv
        qc_ref[...] = qn_ref[...]
        doc_ref[...] = don_ref[...]
        lc_ref[...] = ln_ref[...]
        dlc_ref[...] = dln_ref[...]

    blk = (None, nb, e)
    cur = lambda r, n: (r, n, 0)
    nxt = lambda r, n: (r, jnp.minimum(n + 1, nblk - 1), 0)
    first = lambda r, n: (r, 0, 0)
    rblk = (None, nb, HEAD_DIM)
    both = lambda shape: [pl.BlockSpec(shape, first), pl.BlockSpec(shape, nxt)]
    accs = (dk_acc, dv_acc) if accumulate else ()
    return pl.pallas_call(
        body, name=name, grid=(dil, nblk),
        in_specs=[pl.BlockSpec(blk, cur), pl.BlockSpec(blk, cur), *both(blk), *both(blk), *both(blk), *both(blk),
                  pl.BlockSpec(rblk, cur), pl.BlockSpec(rblk, cur)] + [pl.BlockSpec(blk, cur)] * len(accs),
        out_specs=[pl.BlockSpec(blk, cur)] * 3,
        out_shape=[jax.ShapeDtypeStruct((dil, m, e), BF16),
                   jax.ShapeDtypeStruct((dil, m, e), F32), jax.ShapeDtypeStruct((dil, m, e), F32)],
        scratch_shapes=[pltpu.VMEM((nb, e), F32), pltpu.VMEM((nb, e), BF16), pltpu.VMEM((nb, e), BF16),
                        pltpu.VMEM((nb, e), F32), pltpu.VMEM((nb, e), F32)],
        input_output_aliases={12: 1, 13: 2} if accumulate else {},
        compiler_params=_params(("parallel", "arbitrary")),
    )(k, v, q, q, dout, dout, lse, lse, delta, delta, cos, sin_inv, *accs)


def _rope_tables(s_len):
    inv_freq = 1.0 / (ROPE_THETA ** (jnp.arange(0, HEAD_DIM, 2, dtype=F32) / HEAD_DIM))
    ang = jnp.arange(s_len, dtype=F32)[:, None] * inv_freq[None, :]
    cos, sin = jnp.cos(ang), jnp.sin(ang)
    return jnp.concatenate([cos, cos], axis=1), jnp.concatenate([-sin, sin], axis=1)


def _row(vec):
    return vec.reshape(1, -1)


def _local_step(x, target, n_a, n_b, fetch, begin, emit):
    s_len, d = x.shape
    n_q = len(DILATED_PAIRS)
    cos, sin = _rope_tables(s_len)
    sin_inv = -sin
    q_scale = 1.0 / math.sqrt(HEAD_DIM)
    w = {}

    def need(group, after):
        for name, (layer, arr) in fetch(group, after).items():
            w.setdefault(name, {})[layer] = arr

    saved_a = []
    for i in range(n_a):
        need(f"a{i}", x)
        h = _rmsnorm_fwd(f"a{i}_norm", x, _row(w["norm_a"][i]))
        proj = _mm_act_w(f"a{i}_in", h, w["w_in_a"][i], out_dtype=F32)
        pooled = _pool_fwd(f"a{i}_pool", proj)
        ypre = _mm_grp_fwd(f"a{i}_grp", pooled, w["w_grp_a"][i])
        z = _gate_a_fwd(f"a{i}_gate", ypre, proj, _row(w["scale_a"][i]))
        x_next = _mm_act_w(f"a{i}_out", z, w["w_out_a"][i], out_dtype=F32, add=x)
        saved_a.append((x, h, proj, pooled, ypre, z))
        x = x_next

    x_kv = x
    need("kv", x)
    e = w["w_k"][0].shape[1]
    kv_in = _rmsnorm_fwd("kv_norm", x, _row(w["norm_kv"][0]))
    windows = [window for window, _ in DILATED_PAIRS]
    dils = tuple(dil for _, dil in DILATED_PAIRS)
    far_dils = tuple(dil for dil in dils if dil > 1)
    ks = _mm_act_w("kv_k", kv_in, w["w_k"][0], rope=(cos, sin, 1.0), dils=dils)
    vs = _mm_act_w("kv_v", kv_in, w["w_v"][0], dils=dils)

    saved_b = []
    for i in range(n_b):
        need(f"b{i}", x if i > 0 else vs[0])
        hs = _rmsnorm_fwd(f"b{i}_norm", x, _row(w["norm_b"][i]), dils=far_dils)
        hs = {1: hs[0], **{dil: h_d.reshape(s_len, d) for dil, h_d in zip(far_dils, hs[1:])}}
        qs = [_mm_act_w(f"b{i}_q{g}", hs[1], w["w_in_b"][i], rope=(cos, sin, q_scale), n_first=g, n_cols=1,
                        dils=(dil,))[0] for g, dil in enumerate(dils)]
        gate = _mm_act_w(f"b{i}_g", hs[1], w["w_in_b"][i], out_dtype=F32, n_first=n_q, n_cols=1)
        outs, lses = [], []
        for g in range(n_q):
            o_g, l_g = _attn_fwd(f"b{i}_attn{g}", windows[g], qs[g], ks[g], vs[g])
            outs.append(o_g)
            lses.append(l_g)
        merged, lse, z = _merge_gate_fwd(f"b{i}_merge", outs, lses, gate, dils)
        x_next = _mm_act_w(f"b{i}_out", z, w["w_out_b"][i], out_dtype=F32, add=x)
        saved_b.append((x, hs, qs, gate, merged, lse, z))
        x = x_next

    need("head", x)
    loss_vec, dx, dxb, g_norm_f = _loss_head("loss_head", x, _row(w["norm_f"][0]), target)

    small = {"norm_a": {}, "scale_a": {}, "norm_kv": {}, "norm_b": {}, "norm_f": {0: g_norm_f}}
    shard_rows = lambda g2: g2.reshape(N_CHIPS, g2.shape[0] // N_CHIPS, g2.shape[1])

    res_major = lambda t, dil: t.reshape(s_len // dil, dil, t.shape[1]).transpose(1, 0, 2)
    cos_r = [res_major(cos, dil) for dil in dils]
    sin_inv_r = [res_major(sin_inv, dil) for dil in dils]
    dk_accs = [None] * len(dils)
    dv_accs = [None] * len(dils)
    for i in reversed(range(n_b)):
        x_in, hs, qs, gate, merged, lse, z = saved_b[i]
        dz = _mm_grad_act(f"b{i}_dz", dxb, w["w_out_b"][i])
        g_out = shard_rows(_mm_grad_w(f"b{i}_gwo", z, dxb))
        dgate, stats = _gate_b_bwd(f"b{i}_dgate", dz, merged, gate, lse, dils)
        dh = _mm_grad_act(f"b{i}_dh{n_q}", dgate, w["w_in_b"][i], slot=n_q)
        g_in = _mm_grad_w(f"b{i}_gwi{n_q}", hs[1], dgate, col_shards=n_q + 1, slot=n_q)
        for g, dil in enumerate(dils):
            dout, lse_g, delta_g = stats[g]
            dq, dk_accs[g], dv_accs[g] = _attn_bwd(f"b{i}_dattn{g}", windows[g], q_scale, qs[g], ks[g], vs[g], dout,
                                                   lse_g, delta_g, cos_r[g], sin_inv_r[g], dk_accs[g], dv_accs[g])
            dh = _mm_grad_act(f"b{i}_dh{g}", dq if dil > 1 else dq[0], w["w_in_b"][i], add=dh, slot=g)
            g_in = _mm_grad_w(f"b{i}_gwi{g}", hs[dil], dq.reshape(s_len, e), col_shards=n_q + 1, slot=g, into=g_in)
        (dh,) = begin(f"b{i}", {"w_in_b": (i, g_in), "w_out_b": (i, g_out)}, (dh,))
        dx, dxb, small["norm_b"][i] = _rmsnorm_bwd(f"b{i}_dnorm", x_in, _row(w["norm_b"][i]), dh, dx)
        dx, dxb = emit(f"b{i}", (dx, dxb))

    dkb, dvb = _kv_grad_prep("kv_dprep", dk_accs, dv_accs, dils, cos, sin_inv)
    dkv = _mm_grad_act("kv_dk", dkb, w["w_k"][0])
    dkv = _mm_grad_act("kv_dv", dvb, w["w_v"][0], add=dkv)
    g_k = shard_rows(_mm_grad_w("kv_gwk", kv_in, dkb))
    g_v = shard_rows(_mm_grad_w("kv_gwv", kv_in, dvb))
    (dkv,) = begin("kv", {"w_k": (0, g_k), "w_v": (0, g_v)}, (dkv,))
    dx, dxb, small["norm_kv"][0] = _rmsnorm_bwd("kv_dnorm", x_kv, _row(w["norm_kv"][0]), dkv, dx)
    dx, dxb = emit("kv", (dx, dxb))

    for i in reversed(range(n_a)):
        x_in, h, proj, pooled, ypre, z = saved_a[i]
        dz = _mm_grad_act(f"a{i}_dz", dxb, w["w_out_a"][i])
        g_out = shard_rows(_mm_grad_w(f"a{i}_gwo", z, dxb))
        dypre, dproj, small["scale_a"][i] = _gate_a_bwd(f"a{i}_dgate", dz, ypre, proj, _row(w["scale_a"][i]))
        dpooled = _mm_grp_grad_act(f"a{i}_dgrp", dypre, w["w_grp_a"][i])
        g_grp = _mm_grp_grad_w(f"a{i}_gwg", pooled, dypre, len(POOL_WINDOWS))
        g_grp = g_grp.reshape(N_CHIPS, -1, g_grp.shape[-1])
        last = i == 0
        if last:
            (dpooled,) = emit(f"a{i}", begin(f"a{i}", {"w_grp_a": (i, g_grp), "w_out_a": (i, g_out)}, (dpooled,)))
        dproj = _pool_bwd(f"a{i}_dpool", dpooled, dproj)
        g_in = _mm_grad_w(f"a{i}_gwi", h, dproj, col_shards=N_CHIPS)
        if last:
            (dproj,) = emit(f"a{i}i", begin(f"a{i}i", {"w_in_a": (i, g_in)}, (dproj,)))
        dh = _mm_grad_act(f"a{i}_dh", dproj, w["w_in_a"][i])
        if not last:
            (dh,) = begin(f"a{i}", {"w_in_a": (i, g_in), "w_grp_a": (i, g_grp), "w_out_a": (i, g_out)}, (dh,))
        dx, dxb, small["norm_a"][i] = _rmsnorm_bwd(f"a{i}_dnorm", x_in, _row(w["norm_a"][i]), dh, dx)
        if not last:
            dx, dxb = emit(f"a{i}", (dx, dxb))

    return loss_vec, dx, small


BIG_WEIGHTS = ("w_in_a", "w_grp_a", "w_out_a", "w_k", "w_v", "w_in_b", "w_out_b")


def _pair_add(name, grad, recv, c_idx):
    _, r, cols = grad.shape
    half = r // 2
    rb = _tile(half, 256)
    nrb = half // rb

    def body(c_ref, g_ref, r_ref, o_ref):
        o_ref[...] = (g_ref[...].astype(F32) + r_ref[...].astype(F32)).astype(BF16)

    blk = (None, rb, cols)
    grid_spec = pltpu.PrefetchScalarGridSpec(
        num_scalar_prefetch=1, grid=(N_CHIPS, nrb),
        in_specs=[pl.BlockSpec(blk, lambda s, i, c: (s, c[0] * nrb + i, 0)), pl.BlockSpec(blk, lambda s, i, c: (s, i, 0))],
        out_specs=pl.BlockSpec(blk, lambda s, i, c: (s, i, 0)))
    return pl.pallas_call(body, name=name, grid_spec=grid_spec,
                          out_shape=jax.ShapeDtypeStruct((N_CHIPS, half, cols), BF16),
                          compiler_params=_params(("parallel", "parallel")))(c_idx, grad, recv)


def _final_add(name, part, recv, sc_idx, layer, n_layers, into=None):
    _, half, cols = part.shape
    rb = _tile(half, 256)
    nrb = half // rb
    n_peer = recv.shape[0]

    def body(sc_ref, p_ref, *refs):
        acc = p_ref[...].astype(F32)
        for r_ref in refs[:n_peer]:
            acc = acc + r_ref[...].astype(F32)
        refs[-1][...] = acc

    blk = (None, rb, cols)
    peer_spec = lambda k: pl.BlockSpec(blk, lambda i, sc: (k, i, 0))
    grid_spec = pltpu.PrefetchScalarGridSpec(
        num_scalar_prefetch=1, grid=(nrb,),
        in_specs=[pl.BlockSpec(blk, lambda i, sc: (sc[0], i, 0))] + [peer_spec(k) for k in range(n_peer)]
                 + ([] if into is None else [ANY]),
        out_specs=pl.BlockSpec(blk, lambda i, sc: (layer, sc[1] * nrb + i, 0)))
    extra = () if into is None else (into,)
    return pl.pallas_call(body, name=name, grid_spec=grid_spec,
                          out_shape=jax.ShapeDtypeStruct((n_layers, 2 * half, cols), F32),
                          input_output_aliases={} if into is None else {2 + n_peer: 0},
                          compiler_params=_params(("parallel",)))(sc_idx, part, *([recv] * n_peer), *extra)


def _cast_into_slot(name, arr, layer, s_idx, after=None):
    _, b, r, cols = arr.shape
    rb = _tile(r, 512)

    def body(s_ref, a_ref, *rest):
        rest[-1][...] = a_ref[...].astype(BF16)

    blk = (None, None, rb, cols)
    grid_spec = pltpu.PrefetchScalarGridSpec(
        num_scalar_prefetch=1, grid=(b, r // rb),
        in_specs=[pl.BlockSpec(blk, lambda j, i, s: (layer, j, i, 0))] + ([] if after is None else [ANY]),
        out_specs=pl.BlockSpec(blk, lambda j, i, s: (j, s[0], i, 0)))
    return pl.pallas_call(body, name=name, grid_spec=grid_spec,
                          out_shape=jax.ShapeDtypeStruct((b, N_CHIPS, r, cols), BF16),
                          compiler_params=_params(("parallel", "parallel")))(
                              s_idx, arr, *(() if after is None else (after,)))


def _sum_devices(name, gathered, own, me_idx):
    n_dev, p, d = gathered.shape

    def body(me_ref, g_ref, own_ref, o_ref):
        acc = None
        for j in range(n_dev):
            term = jnp.where(me_ref[0] == j, own_ref[...], g_ref[j])
            acc = term if acc is None else acc + term
        o_ref[...] = acc

    grid_spec = pltpu.PrefetchScalarGridSpec(
        num_scalar_prefetch=1, grid=(1,),
        in_specs=[pl.BlockSpec((n_dev, p, d), lambda i, me: (0, 0, 0)), pl.BlockSpec((p, d), lambda i, me: (0, 0))],
        out_specs=pl.BlockSpec((p, d), lambda i, me: (0, 0)))
    return pl.pallas_call(body, name=name, grid_spec=grid_spec, out_shape=jax.ShapeDtypeStruct((p, d), F32),
                          compiler_params=_params(("arbitrary",)))(me_idx, gathered, own)


def _adamw(name, w, g, m, v):
    shape = w.shape
    cols = shape[-1]
    flat = lambda a: a.reshape(-1, cols)
    rows = flat(w).shape[0]
    bs = _tile(rows, 256)

    def body(w_ref, g_ref, m_ref, v_ref, d_ref, mo_ref, vo_ref):
        grad = g_ref[...]
        m_new = ADAM_B1 * m_ref[...] + (1.0 - ADAM_B1) * grad
        v_new = ADAM_B2 * v_ref[...] + (1.0 - ADAM_B2) * (grad * grad)
        m_hat = m_new / (1.0 - ADAM_B1 ** ADAM_STEP)
        v_hat = v_new / (1.0 - ADAM_B2 ** ADAM_STEP)
        d_ref[...] = -ADAM_LR * (m_hat / (jnp.sqrt(v_hat) + ADAM_EPS) + ADAM_WD * w_ref[...])
        mo_ref[...] = m_new
        vo_ref[...] = v_new

    spec = _row_spec(bs, cols)
    outs = _rows_call(body, name, rows, [spec] * 4, [spec] * 3, [jax.ShapeDtypeStruct((rows, cols), F32)] * 3, bs)(
        flat(w), flat(g), flat(m), flat(v))
    return tuple(o.reshape(shape) for o in outs)


def _place():
    x, y, c = lax.axis_index("x"), lax.axis_index("y"), lax.axis_index("c")
    chips = [(1 - x, y), (x, 1 - y), (1 - x, 1 - y)]
    return x, y, c, chips


def _chip_index(chip):
    return 2 * chip[0] + chip[1]


def _comm_call(body, name, n_in, out_shape, scratch, aliases=None):
    return pl.pallas_call(body, name=name, in_specs=[ANY] * n_in, out_specs=[ANY] * len(out_shape), out_shape=out_shape,
                          scratch_shapes=scratch, input_output_aliases=aliases or {})


HBM_SPEC = pl.BlockSpec(memory_space=pltpu.HBM)
SEM_SPEC = pl.BlockSpec(memory_space=pltpu.SEMAPHORE)
SPLIT_PARAMS = pltpu.CompilerParams(has_side_effects=pltpu.SideEffectType.DATAFLOW_SIDE_EFFECTING)


def _in_hbm(arr):
    return pltpu.with_memory_space_constraint(arr, pltpu.HBM)


def _slot_half(ref, chip, core):
    half = ref.shape[2] // 2
    return ref.at[:, _chip_index(chip), pl.ds(core * half, half), :]


def _gather_start(name, bufs, carry=()):
    n, n_c = len(bufs), len(carry)

    def body(*refs):
        ins, (send_sems, recv_sems) = refs[:n], refs[n + n_c:n + n_c + 2]
        x, y, c, chips = _place()
        for a in range(n):
            block = _slot_half(ins[a], (x, y), c)
            for k, chip in enumerate(chips):
                pltpu.make_async_remote_copy(src_ref=block, dst_ref=block, send_sem=send_sems.at[3 * a + k],
                                             recv_sem=recv_sems.at[3 * a + k], device_id=(*chip, c),
                                             device_id_type=MESH).start()

    dma = pltpu.SemaphoreType.DMA
    thru = list(bufs) + list(carry)
    res = pl.pallas_call(
        body, name=name, in_specs=[HBM_SPEC] * (n + n_c), out_specs=[SEM_SPEC] * 2 + [HBM_SPEC] * (n + n_c),
        out_shape=[dma((3 * n,)), dma((3 * n,))] + [pltpu.HBM(a.shape, a.dtype) for a in thru],
        input_output_aliases={t: 2 + t for t in range(n + n_c)}, compiler_params=SPLIT_PARAMS,
    )(*[_in_hbm(a) for a in thru])
    return (res[0], res[1]), list(res[2:2 + n]), list(res[2 + n:])


def _gather_wait(name, sems, bufs, after):
    n = len(bufs)

    def body(*refs):
        ins, (send_sems, recv_sems) = refs[:n], refs[n:n + 2]
        x, y, c, chips = _place()
        for a in range(n):
            for k, chip in enumerate(chips):
                mine, theirs = _slot_half(ins[a], (x, y), c), _slot_half(ins[a], chip, c)
                copy = pltpu.make_async_remote_copy(src_ref=mine, dst_ref=theirs, send_sem=send_sems.at[3 * a + k],
                                                    recv_sem=recv_sems.at[3 * a + k], device_id=(*chip, c),
                                                    device_id_type=MESH)
                copy.wait_send()
                copy.wait_recv()

    res = pl.pallas_call(
        body, name=name, in_specs=[HBM_SPEC] * n + [SEM_SPEC, SEM_SPEC, ANY], out_specs=[HBM_SPEC] * n,
        out_shape=[pltpu.HBM(b.shape, b.dtype) for b in bufs], input_output_aliases={a: a for a in range(n)},
        compiler_params=SPLIT_PARAMS)(*bufs, *sems, after)
    return list(res)


def _gather_forward(name, bufs, smalls=()):
    n, n_small = len(bufs), len(smalls)

    def body(*refs):
        small_in = refs[n:n + n_small]
        outs = refs[n + n_small:2 * n + n_small]
        small_out = refs[2 * n + n_small:2 * n + 2 * n_small]
        send_sems, recv_sems, s_send, s_recv, s_local = refs[-5:]
        x, y, c, chips = _place()
        me, sibling = _chip_index((x, y)), (x, y, 1 - c)

        def forward(t, k, core):
            block = _slot_half(outs[t], chips[k], core)
            return pltpu.make_async_remote_copy(src_ref=block, dst_ref=block, send_sem=send_sems.at[t, k],
                                                recv_sem=recv_sems.at[t, k], device_id=sibling, device_id_type=MESH)

        def small_copy(j, k, slot):
            return pltpu.make_async_remote_copy(src_ref=small_in[j], dst_ref=small_out[j].at[slot],
                                                send_sem=s_send.at[j, k], recv_sem=s_recv.at[j, k],
                                                device_id=(*chips[k], c), device_id_type=MESH)

        local = []
        for t in range(n):
            for k in range(3):
                forward(t, k, c).start()
        for j in range(n_small):
            own = pltpu.make_async_copy(small_in[j], small_out[j].at[me], s_local.at[j])
            own.start()
            local.append(own)
            for k in range(3):
                small_copy(j, k, me).start()
        for t in range(n):
            for k in range(3):
                forward(t, k, 1 - c).wait_recv()
        for j in range(n_small):
            for k in range(3):
                small_copy(j, k, _chip_index(chips[k])).wait_recv()
        for t in range(n):
            for k in range(3):
                forward(t, k, c).wait_send()
        for j in range(n_small):
            for k in range(3):
                small_copy(j, k, me).wait_send()
        for own in local:
            own.wait()

    out_shape = [jax.ShapeDtypeStruct(b.shape, BF16) for b in bufs]
    out_shape += [jax.ShapeDtypeStruct((N_CHIPS,) + s.shape, F32) for s in smalls]
    dma = pltpu.SemaphoreType.DMA
    n_s = max(n_small, 1)
    res = _comm_call(body, name, n + n_small, out_shape,
                     [dma((n, 3)), dma((n, 3)), dma((n_s, 3)), dma((n_s, 3)), dma((n_s,))],
                     aliases={t: t for t in range(n)})(*bufs, *smalls)
    return list(res[:n]), list(res[n:])


def _halves_copy(grad_ref, land_ref, send_sems, recv_sems, t):
    x, y, c, _ = _place()
    half = grad_ref.shape[1] // 2
    return pltpu.make_async_remote_copy(
        src_ref=grad_ref.at[:, pl.ds((1 - c) * half, half), :], dst_ref=land_ref, send_sem=send_sems.at[t],
        recv_sem=recv_sems.at[t], device_id=(x, y, 1 - c), device_id_type=MESH)


def _exchange_start(name, grads, carry=()):
    n = len(grads)
    lands = [lax.empty((g.shape[0], g.shape[1] // 2, g.shape[2]), BF16) for g in grads]

    def body(*refs):
        send_sems, recv_sems = refs[2 * n + len(carry):2 * n + len(carry) + 2]
        for t in range(n):
            _halves_copy(refs[t], refs[n + t], send_sems, recv_sems, t).start()

    dma = pltpu.SemaphoreType.DMA
    thru = list(grads) + lands + list(carry)
    res = pl.pallas_call(
        body, name=name, in_specs=[HBM_SPEC] * len(thru), out_specs=[SEM_SPEC] * 2 + [HBM_SPEC] * len(thru),
        out_shape=[dma((n,)), dma((n,))] + [pltpu.HBM(a.shape, a.dtype) for a in thru],
        input_output_aliases={t: 2 + t for t in range(len(thru))}, compiler_params=SPLIT_PARAMS,
    )(*[_in_hbm(a) for a in thru])
    return (res[0], res[1]), list(res[2:2 + n]), list(res[2 + n:2 + 2 * n]), list(res[2 + 2 * n:])


def _exchange_wait(name, sems, grads, lands, after):
    n = len(grads)

    def body(*refs):
        send_sems, recv_sems = refs[2 * n:2 * n + 2]
        for t in range(n):
            copy = _halves_copy(refs[t], refs[n + t], send_sems, recv_sems, t)
            copy.wait_send()
            copy.wait_recv()

    res = pl.pallas_call(
        body, name=name, in_specs=[HBM_SPEC] * (2 * n) + [SEM_SPEC, SEM_SPEC, ANY], out_specs=[HBM_SPEC] * (2 * n),
        out_shape=[pltpu.HBM(a.shape, a.dtype) for a in grads + lands],
        input_output_aliases={t: t for t in range(2 * n)}, compiler_params=SPLIT_PARAMS)(*grads, *lands, *sems, after)
    return list(res[:n]), list(res[n:])


def _scatter_copy(part_ref, land_ref, send_sems, recv_sems, t, k, chip, c):
    return pltpu.make_async_remote_copy(
        src_ref=part_ref.at[_chip_index(chip)], dst_ref=land_ref.at[k], send_sem=send_sems.at[3 * t + k],
        recv_sem=recv_sems.at[3 * t + k], device_id=(*chip, c), device_id_type=MESH)


def _scatter_start(name, parts, carry=()):
    n, n_c = len(parts), len(carry)
    lands = [lax.empty((3,) + p.shape[1:], BF16) for p in parts]

    def body(*refs):
        p_in, l_in = refs[:n], refs[n:2 * n]
        send_sems, recv_sems = refs[2 * n + n_c:2 * n + n_c + 2]
        x, y, c, chips = _place()
        for t in range(n):
            for k, chip in enumerate(chips):
                _scatter_copy(p_in[t], l_in[t], send_sems, recv_sems, t, k, chip, c).start()

    dma = pltpu.SemaphoreType.DMA
    thru = list(parts) + lands + list(carry)
    res = pl.pallas_call(
        body, name=name, in_specs=[HBM_SPEC] * len(thru), out_specs=[SEM_SPEC] * 2 + [HBM_SPEC] * len(thru),
        out_shape=[dma((3 * n,)), dma((3 * n,))] + [pltpu.HBM(a.shape, a.dtype) for a in thru],
        input_output_aliases={t: 2 + t for t in range(len(thru))}, compiler_params=SPLIT_PARAMS,
    )(*[_in_hbm(a) for a in thru])
    return (res[0], res[1]), list(res[2:2 + n]), list(res[2 + n:2 + 2 * n]), list(res[2 + 2 * n:])


def _scatter_wait(name, sems, parts, lands, after):
    n = len(parts)

    def body(*refs):
        p_in, l_in = refs[:n], refs[n:2 * n]
        send_sems, recv_sems = refs[2 * n:2 * n + 2]
        x, y, c, chips = _place()
        for t in range(n):
            for k, chip in enumerate(chips):
                copy = _scatter_copy(p_in[t], l_in[t], send_sems, recv_sems, t, k, chip, c)
                copy.wait_send()
                copy.wait_recv()

    hbm_out = lambda a: pltpu.HBM(a.shape, a.dtype)
    res = pl.pallas_call(
        body, name=name, in_specs=[HBM_SPEC] * (2 * n) + [SEM_SPEC, SEM_SPEC, ANY], out_specs=[HBM_SPEC] * (2 * n),
        out_shape=[hbm_out(a) for a in parts + lands], input_output_aliases={t: t for t in range(2 * n)},
        compiler_params=SPLIT_PARAMS)(*parts, *lands, *sems, after)
    return list(res[:n]), list(res[n:])


def _share_halves(fulls):
    n = len(fulls)
    items = [(a, l) for a in range(n) for l in range(fulls[a].shape[0])]

    def body(*refs):
        outs = refs[n:2 * n]
        send_sems, recv_sems = refs[-2:]
        x, y, c, _ = _place()

        def copy(t, core):
            a, l = items[t]
            half = outs[a].shape[1] // 2
            block = outs[a].at[l, pl.ds(core * half, half), :]
            return pltpu.make_async_remote_copy(src_ref=block, dst_ref=block, send_sem=send_sems.at[t],
                                                recv_sem=recv_sems.at[t], device_id=(x, y, 1 - c), device_id_type=MESH)

        for t in range(len(items)):
            copy(t, c).start()
        for t in range(len(items)):
            copy(t, 1 - c).wait_recv()
        for t in range(len(items)):
            copy(t, c).wait_send()

    out_shape = [jax.ShapeDtypeStruct(f.shape, F32) for f in fulls]
    dma = pltpu.SemaphoreType.DMA
    return list(_comm_call(body, "grad_share_halves", n, out_shape, [dma((len(items),)), dma((len(items),))],
                           aliases={a: a for a in range(n)})(*fulls))


N_DEVICES = 8


def _device_index(x, y, c):
    return 4 * x + 2 * y + c


def _small_peers():
    x, y, c, _ = _place()
    flips = [(fx, fy, fc) for fx in (0, 1) for fy in (0, 1) for fc in (0, 1)][1:]
    return _device_index(x, y, c), [(x ^ fx, y ^ fy, c ^ fc) for fx, fy, fc in flips]


def _small_copy(p_ref, land_ref, send_sems, recv_sems, k, peer, slot):
    return pltpu.make_async_remote_copy(src_ref=p_ref, dst_ref=land_ref.at[slot], send_sem=send_sems.at[k],
                                        recv_sem=recv_sems.at[k], device_id=peer, device_id_type=MESH)


def _allgather_small_start(packed, carry=()):
    land = jnp.zeros((N_DEVICES,) + packed.shape, F32)

    def body(*refs):
        p_ref, l_ref = refs[:2]
        send_sems, recv_sems = refs[2 + len(carry):4 + len(carry)]
        me, peers = _small_peers()
        for k, peer in enumerate(peers):
            _small_copy(p_ref, l_ref, send_sems, recv_sems, k, peer, me).start()

    dma = pltpu.SemaphoreType.DMA
    thru = [packed, land] + list(carry)
    res = pl.pallas_call(
        body, name="small_allgather_start", in_specs=[HBM_SPEC] * len(thru),
        out_specs=[SEM_SPEC] * 2 + [HBM_SPEC] * len(thru),
        out_shape=[dma((N_DEVICES - 1,)), dma((N_DEVICES - 1,))] + [pltpu.HBM(a.shape, a.dtype) for a in thru],
        input_output_aliases={t: 2 + t for t in range(len(thru))}, compiler_params=SPLIT_PARAMS,
    )(*[_in_hbm(a) for a in thru])
    return (res[0], res[1]), res[2], res[3], list(res[4:])


def _allgather_small_wait(sems, packed, land, after):
    def body(p_ref, l_ref, send_sems, recv_sems, *_):
        _, peers = _small_peers()
        for k, peer in enumerate(peers):
            copy = _small_copy(p_ref, l_ref, send_sems, recv_sems, k, peer, _device_index(*peer))
            copy.wait_send()
            copy.wait_recv()

    res = pl.pallas_call(
        body, name="small_allgather_wait", in_specs=[HBM_SPEC] * 2 + [SEM_SPEC, SEM_SPEC, ANY],
        out_specs=[HBM_SPEC] * 2, out_shape=[pltpu.HBM(a.shape, a.dtype) for a in (packed, land)],
        input_output_aliases={0: 0, 1: 1}, compiler_params=SPLIT_PARAMS)(packed, land, *sems, after)
    return res[0], res[1]


PAD_ROWS = 8


def kernel(x, norm_a, w_in_a, w_grp_a, scale_a, w_out_a, norm_kv, w_k, w_v, norm_b, w_in_b, w_out_b, norm_f, loss_target, m_norm_a, m_w_in_a, m_w_grp_a, m_scale_a, m_w_out_a, m_norm_kv, m_w_k, m_w_v, m_norm_b, m_w_in_b, m_w_out_b, m_norm_f, v_norm_a, v_w_in_a, v_w_grp_a, v_scale_a, v_w_out_a, v_norm_kv, v_w_k, v_w_v, v_norm_b, v_w_in_b, v_w_out_b, v_norm_f):
    weights = dict(norm_a=norm_a, w_in_a=w_in_a, w_grp_a=w_grp_a, scale_a=scale_a, w_out_a=w_out_a, norm_kv=norm_kv,
                   w_k=w_k, w_v=w_v, norm_b=norm_b, w_in_b=w_in_b, w_out_b=w_out_b, norm_f=norm_f)
    moments_m = dict(norm_a=m_norm_a, w_in_a=m_w_in_a, w_grp_a=m_w_grp_a, scale_a=m_scale_a, w_out_a=m_w_out_a,
                     norm_kv=m_norm_kv, w_k=m_w_k, w_v=m_w_v, norm_b=m_norm_b, w_in_b=m_w_in_b, w_out_b=m_w_out_b,
                     norm_f=m_norm_f)
    moments_v = dict(norm_a=v_norm_a, w_in_a=v_w_in_a, w_grp_a=v_w_grp_a, scale_a=v_scale_a, w_out_a=v_w_out_a,
                     norm_kv=v_norm_kv, w_k=v_w_k, w_v=v_w_v, norm_b=v_norm_b, w_in_b=v_w_in_b, w_out_b=v_w_out_b,
                     norm_f=v_norm_f)
    names = list(weights)
    d = x.shape[-1]
    c_idx = lax.axis_index("c").astype(jnp.int32).reshape(1)
    s_me = 2 * lax.axis_index("x") + lax.axis_index("y")
    s_idx = s_me.astype(jnp.int32).reshape(1)

    def as_lbrc(name):
        a = weights[name]
        if name == "w_grp_a":
            return a
        if a.ndim == 2:
            return a.reshape(1, 1, *a.shape)
        return a.reshape(a.shape[0], 1, *a.shape[1:])

    n_a, n_b = norm_a.shape[0], norm_b.shape[0]
    group_weights = {**{f"a{i}": [("w_in_a", i), ("w_grp_a", i), ("w_out_a", i)] for i in range(n_a)},
                     "kv": [("w_k", 0), ("w_v", 0)],
                     **{f"b{i}": [("w_in_b", i), ("w_out_b", i)] for i in range(n_b)}}
    group_order = [f"a{i}" for i in range(n_a)] + ["kv"] + [f"b{i}" for i in range(n_b)]
    slots, slot_groups = [], []
    small_full, started = {}, {}

    def start_group(gi, carry=()):
        sems, bufs, carry = _gather_start(f"gather_start_{group_order[gi]}", [slots[t] for t in slot_groups[gi]], carry)
        started[gi] = (sems, bufs)
        return carry

    previous = None
    for gi, group in enumerate(group_order):
        slot_groups.append(list(range(len(slots), len(slots) + len(group_weights[group]))))
        for name, l in group_weights[group]:
            slots.append(_cast_into_slot(f"cast_{name}{l}", as_lbrc(name), l, s_idx, previous))
            previous = slots[-1] if gi > 0 else None
        if gi == 0:
            (previous,) = start_group(0, (norm_kv.reshape(1, -1),))

    def gathered_form(name, g):
        if name in ("w_in_a", "w_in_b"):
            return g[0]
        if name == "w_grp_a":
            return g.reshape(g.shape[0], -1, g.shape[-1])
        return g.reshape(-1, g.shape[-1])

    def fetch(group, after):
        if group == "head":
            return {"norm_f": (0, norm_f)}
        gi = group_order.index(group)
        sems, bufs = started[gi]
        if gi == 0:
            after = slots[-1]
        bufs = _gather_wait(f"gather_wait_{group}", sems, bufs, after)
        bufs, small_g = _gather_forward(f"gather_forward_{group}", bufs, [norm_a, scale_a] if gi == 0 else [])
        out = {name: (l, gathered_form(name, g)) for (name, l), g in zip(group_weights[group], bufs)}
        if gi == 0:
            for name, g in zip(("norm_a", "scale_a"), small_g):
                small_full[name] = g.transpose(1, 0, 2).reshape(g.shape[1], -1)
        layer = group_weights[group][0][1]
        if group.startswith("a"):
            gain_name, gain = "norm_a", small_full["norm_a"][layer]
            out.update(scale_a=(layer, small_full["scale_a"][layer]))
        elif group == "kv":
            gain_name, gain = "norm_kv", norm_kv
        else:
            gain_name, gain = "norm_b", norm_b[layer]
        gain = gain.reshape(1, -1)
        ahead = [gi + 1] + ([gi + 2] if gi + 2 < len(group_order) and group_order[gi + 1] == "kv" else [])
        for gj in ahead:
            if gj < len(group_order) and gj not in started:
                (gain,) = start_group(gj, (gain,))
        out[gain_name] = (layer, gain)
        return out

    exchanging, in_flight = {}, []

    def begin(group, grads_of, carry):
        keys = [(k, grads_of[k][0]) for k in grads_of]
        sems, grads, lands, carry = _exchange_start(f"grad_exchange_start_{group}", [grads_of[k][1] for k in grads_of],
                                                    tuple(carry))
        exchanging[group] = (keys, sems, grads, lands)
        return carry

    def emit(group, carry):
        keys, sems, grads, lands = exchanging.pop(group)
        grads, recv1 = _exchange_wait(f"grad_exchange_wait_{group}", sems, grads, lands, after=carry[0])
        parts = [_pair_add(f"pair_add_{k}{l}", g, r, c_idx) for (k, l), g, r in zip(keys, grads, recv1)]
        sems, parts, lands, carry = _scatter_start(f"grad_scatter_start_{group}", parts, tuple(carry))
        in_flight.append((group, keys, sems, parts, lands))
        return carry

    loss_vec, grad_x, small = _local_step(x[0], loss_target[0], n_a, n_b, fetch, begin, emit)

    small_order = [("norm_a", i) for i in range(n_a)] + [("scale_a", i) for i in range(n_a)] + [("norm_kv", 0)] + \
                  [("norm_b", i) for i in range(norm_b.shape[0])] + [("norm_f", 0)]
    pad = lambda vec: jnp.pad(vec, ((0, PAD_ROWS - 1), (0, 0)))
    packed = jnp.concatenate([pad(loss_vec)] + [pad(small[n][i]) for n, i in small_order], axis=0)
    small_sems, packed, small_land, _ = _allgather_small_start(packed)

    sc_idx = jnp.concatenate([s_idx, c_idx])
    fulls = {name: None for name in BIG_WEIGHTS}
    for group, keys, sems, parts, lands in in_flight:
        parts, lands = _scatter_wait(f"grad_scatter_wait_{group}", sems, parts, lands, after=packed)
        for (name, i), p, r in zip(keys, parts, lands):
            n_layers = 1 if weights[name].ndim == 2 else weights[name].shape[0]
            fulls[name] = _final_add(f"final_add_{name}{i}", p, r, sc_idx, i, n_layers, into=fulls[name])
    shared = _share_halves([fulls[name] for name in BIG_WEIGHTS])
    grads = {name: g.reshape(weights[name].shape) for name, g in zip(BIG_WEIGHTS, shared)}

    deltas, new_m, new_v = {}, {}, {}

    def update(n):
        shape = weights[n].shape
        as2d = (lambda a: a.reshape(1, -1)) if len(shape) == 1 else (lambda a: a)
        dl, mn, vn = _adamw(f"adamw_{n}", as2d(weights[n]), as2d(grads[n]), as2d(moments_m[n]), as2d(moments_v[n]))
        deltas[n], new_m[n], new_v[n] = dl.reshape(shape), mn.reshape(shape), vn.reshape(shape)

    for n in BIG_WEIGHTS:
        update(n)
    packed, small_land = _allgather_small_wait(small_sems, packed, small_land, after=new_v[BIG_WEIGHTS[-1]])
    me_idx = _device_index(lax.axis_index("x"), lax.axis_index("y"), lax.axis_index("c")).astype(jnp.int32).reshape(1)
    totals = _sum_devices("small_sum", small_land, packed, me_idx)
    loss = 0.5 * jnp.sum(totals[0]) / d
    small_tot = {}
    for j, (n, i) in enumerate(small_order):
        small_tot.setdefault(n, []).append(totals[PAD_ROWS * (j + 1)])
    shard_w = norm_a.shape[1]
    for n in ("norm_a", "scale_a"):
        full = jnp.stack(small_tot[n])
        grads[n] = lax.dynamic_slice_in_dim(full, s_me * shard_w, shard_w, axis=1)
    grads["norm_kv"] = small_tot["norm_kv"][0]
    grads["norm_b"] = jnp.stack(small_tot["norm_b"])
    grads["norm_f"] = small_tot["norm_f"][0]
    for n in names:
        if n not in BIG_WEIGHTS:
            update(n)

    return (loss, grad_x[None], *[grads[n] for n in names], *[deltas[n] for n in names],
            *[new_m[n] for n in names], *[new_v[n] for n in names])
```

```python
import functools
import math

import jax
import jax.numpy as jnp
from jax import lax
from jax.experimental import pallas as pl
from jax.experimental.pallas import tpu as pltpu

F32 = jnp.float32
BF16 = jnp.bfloat16

HEAD_DIM = 128
POOL_WINDOWS = (2, 4, 8, 16)
DILATED_PAIRS = ((128, 1), (512, 4), (2048, 16))
ROPE_THETA = 10000.0
RMS_EPS = 1e-6
NEG_INF = -1e30
N_CHIPS = 4

ADAM_LR = 0.001
ADAM_B1 = 0.9
ADAM_B2 = 0.999
ADAM_EPS = 1e-08
ADAM_WD = 0.01
ADAM_STEP = 10

VMEM_LIMIT_BYTES = 56 * 1024 * 1024
MESH = pl.DeviceIdType.MESH
ANY = pl.BlockSpec(memory_space=pl.ANY)


def _tile(n, pref):
    t = min(n, pref)
    assert n % t == 0, (n, pref)
    return t


def _params(sem=None):
    return pltpu.CompilerParams(dimension_semantics=sem, vmem_limit_bytes=VMEM_LIMIT_BYTES)


def _mm(name, a, b, *, grid2, nk, a_blk, a_map, b_blk, b_map, outs, dims, epi=None, epi_in=(), epi_specs=(),
        acc_shape=None, epi_scratch=(), into=None):
    n_epi, n_out = len(epi_in), len(outs)

    def body(*refs):
        a_ref, b_ref = refs[0], refs[1]
        e_refs = refs[2:2 + n_epi]
        first_out = 2 + n_epi + (0 if into is None else 1)
        o_refs = refs[first_out:first_out + n_out]
        s_refs = refs[first_out + n_out + (0 if nk == 1 else 1):]

        def contrib():
            a_val = a_ref[...]
            if a_val.ndim == 3:
                a_val = a_val.reshape(-1, a_val.shape[-1])
            return lax.dot_general(a_val, b_ref[...], (dims, ((), ())), preferred_element_type=F32)

        def finish(acc):
            if epi is None:
                o_refs[0][...] = acc.reshape(o_refs[0].shape).astype(o_refs[0].dtype)
            else:
                epi(acc, e_refs, o_refs, s_refs)

        if nk == 1:
            finish(contrib())
        else:
            acc_ref = refs[first_out + n_out]
            k = pl.program_id(2)

            @pl.when(k == 0)
            def _():
                acc_ref[...] = contrib()

            @pl.when(k > 0)
            def _():
                acc_ref[...] += contrib()

            @pl.when(k == nk - 1)
            def _():
                finish(acc_ref[...])

    scratch = ([] if nk == 1 else [pltpu.VMEM(acc_shape, F32)]) + list(epi_scratch)
    extra_in, extra_specs, aliases = (), (), {}
    if into is not None:
        extra_in, extra_specs, aliases = (into[0],), (ANY,), {2 + n_epi: into[1]}
    res = pl.pallas_call(
        body, name=name, grid=(grid2[0], grid2[1], nk),
        in_specs=[pl.BlockSpec(a_blk, a_map), pl.BlockSpec(b_blk, b_map), *epi_specs, *extra_specs],
        out_specs=[pl.BlockSpec(blk, imap) for _, blk, imap, _ in outs],
        out_shape=[jax.ShapeDtypeStruct(shape, dtype) for shape, _, _, dtype in outs],
        scratch_shapes=scratch, input_output_aliases=aliases,
        compiler_params=_params(("parallel", "parallel", "arbitrary")),
    )(a, b, *epi_in, *extra_in)
    return res[0] if n_out == 1 else tuple(res)


NN = ((1,), (0,))
NT = ((1,), (1,))
TN = ((0,), (0,))


def _rope_apply(t, cos, sin):
    return t * cos + pltpu.roll(t, HEAD_DIM // 2, 1) * sin


def _epi_add(acc, e_refs, o_refs, s_refs):
    o_refs[0][...] = (acc + e_refs[0][...]).astype(o_refs[0].dtype)


def _col_blocks(width):
    return [slice(c * HEAD_DIM, (c + 1) * HEAD_DIM) for c in range(width // HEAD_DIM)]


def _col_scratch(rows, width):
    return pltpu.VMEM((width // HEAD_DIM, rows, HEAD_DIM), F32)


def _to_residue_major(o_ref, scr, d, sl):
    if d == 1:
        o_ref[0, :, sl] = scr[...].astype(o_ref.dtype)
        return
    rows = scr.shape[0] // d
    for r in range(d):
        o_ref[r, :, sl] = scr[pl.ds(r, rows, stride=d), :].astype(o_ref.dtype)


def _from_residue_major(i_ref, scr, d, sl):
    if d == 1:
        return i_ref[0, :, sl].astype(F32)
    rows = i_ref.shape[1]
    for r in range(d):
        scr[pl.ds(r, rows, stride=d), :] = i_ref[r, :, sl].astype(F32)
    return scr[...]


def _make_epi_orders(dils, rope_scale):
    def epi(acc, e_refs, o_refs, s_refs):
        if rope_scale is not None:
            cos = e_refs[0][...]
            sin = e_refs[1][...]
        for c, sl in enumerate(_col_blocks(acc.shape[1])):
            scr = s_refs[0].at[c]
            scr[...] = acc[:, sl] if rope_scale is None else _rope_apply(acc[:, sl], cos, sin) * rope_scale
            for o_ref, d in zip(o_refs, dils):
                _to_residue_major(o_ref, scr, d, sl)
    return epi


def _make_epi_token_order(d, has_add):
    def epi(acc, e_refs, o_refs, s_refs):
        o_ref = o_refs[0]
        if d == 1:
            o_ref[...] = acc + e_refs[0][...] if has_add else acc
            return
        rows = acc.shape[0] // d
        for c, sl in enumerate(_col_blocks(acc.shape[1])):
            scr = s_refs[0].at[c]
            for r in range(d):
                scr[pl.ds(r, rows, stride=d), :] = acc[r * rows:(r + 1) * rows, sl]
            o_ref[:, sl] = scr[...] + e_refs[0][:, sl] if has_add else scr[...]
    return epi


def _mm_act_w(name, a, w, *, out_dtype=BF16, add=None, rope=None, n_first=0, n_cols=None, dils=None):
    s_len, k_len = a.shape
    bm = _tile(s_len, 1024)
    epi, epi_in, epi_specs, epi_scratch = None, (), (), ()
    if w.ndim == 3:
        ns, _, c = w.shape
        ns_used = ns if n_cols is None else n_cols
        bn = _tile(c, 1024)
        sub = c // bn
        grid2 = (ns_used * sub, s_len // bm)
        b_blk, b_map = (None, k_len, bn), (lambda j, i, k: (j // sub + n_first, 0, j % sub))
        n_len = ns_used * c
    else:
        n_len = w.shape[1]
        bn = _tile(n_len, 1024)
        grid2 = (n_len // bn, s_len // bm)
        b_blk, b_map = (k_len, bn), (lambda j, i, k: (0, j))
    if add is not None:
        epi, epi_in = _epi_add, (add,)
        epi_specs = (pl.BlockSpec((bm, bn), lambda j, i, k: (i, j)),)
    outs = [((s_len, n_len), (bm, bn), lambda j, i, k: (i, j), out_dtype)]
    if dils is not None:
        if rope is not None:
            epi_in = rope[:2]
            epi_specs = (pl.BlockSpec((bm, HEAD_DIM), lambda j, i, k: (i, 0)),) * 2
        epi = _make_epi_orders(dils, None if rope is None else rope[2])
        epi_scratch = (_col_scratch(bm, bn),)
        outs = [((d, s_len // d, n_len), (d, bm // d, bn), lambda j, i, k: (0, i, j), BF16) for d in dils]
    res = _mm(name, a, w, grid2=grid2, nk=1, a_blk=(bm, k_len), a_map=lambda j, i, k: (i, 0),
              b_blk=b_blk, b_map=b_map, outs=outs, dims=NN, epi=epi, epi_in=epi_in, epi_specs=epi_specs,
              epi_scratch=epi_scratch)
    return (res,) if dils is not None and len(dils) == 1 else res


def _mm_grad_act(name, dy, w, *, add=None, slot=None):
    if slot is not None:
        d = 1 if dy.ndim == 2 else dy.shape[0]
        s_len = dy.shape[-2] * d
        _, k_len, c = w.shape
        bm, bn = _tile(s_len, 1024), _tile(k_len, 1024)
        a_blk, a_map = ((bm, c), lambda j, i, k: (i, 0)) if dy.ndim == 2 else ((d, bm // d, c), lambda j, i, k: (0, i, 0))
        epi_in = () if add is None else (add,)
        return _mm(name, dy, w, grid2=(k_len // bn, s_len // bm), nk=1, a_blk=a_blk, a_map=a_map,
                   b_blk=(None, bn, c), b_map=lambda j, i, k: (slot, j, 0),
                   outs=[((s_len, k_len), (bm, bn), lambda j, i, k: (i, j), F32)], dims=NT,
                   epi=_make_epi_token_order(d, add is not None), epi_in=epi_in,
                   epi_specs=(pl.BlockSpec((bm, bn), lambda j, i, k: (i, j)),) * len(epi_in),
                   epi_scratch=(_col_scratch(bm, bn),) if d > 1 else ())
    s_len, n_len = dy.shape
    bm = _tile(s_len, 1024)
    if w.ndim == 3:
        ns, k_len, c = w.shape
        bk, nk = c, ns
        bn = _tile(k_len, 1024)
        b_blk, b_map = (None, bn, c), (lambda j, i, k: (k, j, 0))
    else:
        k_len = w.shape[0]
        bk = _tile(n_len, 2048)
        nk = n_len // bk
        bn = _tile(k_len, 1024)
        b_blk, b_map = (bn, bk), (lambda j, i, k: (j, k))
    epi, epi_in, epi_specs = None, (), ()
    if add is not None:
        epi, epi_in = _epi_add, (add,)
        epi_specs = (pl.BlockSpec((bm, bn), lambda j, i, k: (i, j)),)
    return _mm(name, dy, w, grid2=(k_len // bn, s_len // bm), nk=nk, a_blk=(bm, bk), a_map=lambda j, i, k: (i, k),
               b_blk=b_blk, b_map=b_map, outs=[((s_len, k_len), (bm, bn), lambda j, i, k: (i, j), F32)],
               dims=NT, epi=epi, epi_in=epi_in, epi_specs=epi_specs, acc_shape=(bm, bn))


def _mm_grad_w(name, a, dy, *, col_shards=None, slot=None, into=None):
    s_len, k_len = a.shape
    n_len = dy.shape[1]
    bk = _tile(s_len, 2048)
    bm = _tile(k_len, 1024)
    if slot is not None:
        bn = _tile(n_len, 1024)
        out = ((col_shards, k_len, n_len), (None, bm, bn), lambda j, i, k: (slot, i, j), BF16)
    elif col_shards:
        c = n_len // col_shards
        bn = _tile(c, 1024)
        sub = c // bn
        out = ((col_shards, k_len, c), (None, bm, bn), lambda j, i, k: (j // sub, i, j % sub), BF16)
    else:
        bn = _tile(n_len, 1024)
        out = ((k_len, n_len), (bm, bn), lambda j, i, k: (i, j), BF16)
    return _mm(name, a, dy, grid2=(n_len // bn, k_len // bm), nk=s_len // bk,
               a_blk=(bk, bm), a_map=lambda j, i, k: (k, i), b_blk=(bk, bn), b_map=lambda j, i, k: (k, j),
               outs=[out], dims=TN, acc_shape=(bm, bn), into=None if into is None else (into, 0))


def _mm_grp_fwd(name, pooled, wg):
    s_len, e = pooled.shape
    ng, g, _ = wg.shape
    bm = _tile(s_len, 1024)
    return _mm(name, pooled, wg, grid2=(ng, s_len // bm), nk=1, a_blk=(bm, g), a_map=lambda j, i, k: (i, j),
               b_blk=(None, g, g), b_map=lambda j, i, k: (j, 0, 0),
               outs=[((s_len, e), (bm, g), lambda j, i, k: (i, j), F32)], dims=NN)


def _mm_grp_grad_act(name, dy, wg):
    s_len, e = dy.shape
    ng, g, _ = wg.shape
    bm = _tile(s_len, 1024)
    return _mm(name, dy, wg, grid2=(ng, s_len // bm), nk=1, a_blk=(bm, g), a_map=lambda j, i, k: (i, j),
               b_blk=(None, g, g), b_map=lambda j, i, k: (j, 0, 0),
               outs=[((s_len, e), (bm, g), lambda j, i, k: (i, j), F32)], dims=NT)


def _mm_grp_grad_w(name, pooled, dy, ng):
    s_len, e = pooled.shape
    g = e // ng
    bk = _tile(s_len, 1024)
    return _mm(name, pooled, dy, grid2=(ng, 1), nk=s_len // bk, a_blk=(bk, g), a_map=lambda j, i, k: (k, j),
               b_blk=(bk, g), b_map=lambda j, i, k: (k, j),
               outs=[((N_CHIPS, ng, g // N_CHIPS, g), (N_CHIPS, None, g // N_CHIPS, g), lambda j, i, k: (0, j, 0, 0), BF16)],
               dims=TN, acc_shape=(g, g))


def _row_spec(bs, width, col=0):
    return pl.BlockSpec((bs, width), lambda i: (i, col))


def _vec_spec(width):
    return pl.BlockSpec((1, width), lambda i: (0, 0))


def _rows_call(body, name, s_len, in_specs, out_specs, out_shape, bs, aliases=None, sequential=False):
    return pl.pallas_call(
        body, name=name, grid=(s_len // bs,), in_specs=in_specs, out_specs=out_specs, out_shape=out_shape,
        input_output_aliases=aliases or {},
        compiler_params=_params(("arbitrary",) if sequential else ("parallel",)))


def _accumulate(ref, part):
    i = pl.program_id(0)

    @pl.when(i == 0)
    def _():
        ref[...] = part

    @pl.when(i > 0)
    def _():
        ref[...] += part


def _rms_scale(xf):
    return lax.rsqrt(jnp.mean(xf * xf, axis=-1, keepdims=True) + RMS_EPS)


def _res_spec(dil, bs, width):
    return pl.BlockSpec((dil, bs // dil, width), lambda i: (0, i, 0))


def _res_shape(dil, s_len, width, dtype):
    return jax.ShapeDtypeStruct((dil, s_len // dil, width), dtype)


def _rmsnorm_fwd(name, x, gain, dils=()):
    s_len, d = x.shape
    bs = _tile(s_len, 256)

    def body(x_ref, g_ref, h_ref, *rest):
        xf = x_ref[...]
        h = (xf * _rms_scale(xf)) * g_ref[...]
        h_ref[...] = h.astype(BF16)
        if dils:
            for c, sl in enumerate(_col_blocks(d)):
                scr = rest[-1].at[c]
                scr[...] = h[:, sl]
                for o_ref, dil in zip(rest[:-1], dils):
                    _to_residue_major(o_ref, scr, dil, sl)

    res = pl.pallas_call(
        body, name=name, grid=(s_len // bs,), in_specs=[_row_spec(bs, d), _vec_spec(d)],
        out_specs=[_row_spec(bs, d)] + [_res_spec(dil, bs, d) for dil in dils],
        out_shape=[jax.ShapeDtypeStruct((s_len, d), BF16)] + [_res_shape(dil, s_len, d, BF16) for dil in dils],
        scratch_shapes=[_col_scratch(bs, d)] if dils else [],
        compiler_params=_params(("parallel",)))(x, gain)
    return res[0] if not dils else tuple(res)


def _rmsnorm_bwd(name, x, gain, dh, dres):
    s_len, d = x.shape
    bs = _tile(s_len, 256)

    def body(x_ref, g_ref, dh_ref, dres_ref, dx_ref, dxb_ref, dg_ref):
        xf = x_ref[...]
        r = _rms_scale(xf)
        xh = xf * r
        dh_f = dh_ref[...]
        t = dh_f * g_ref[...]
        dx = dres_ref[...] + r * (t - xh * jnp.mean(t * xh, axis=-1, keepdims=True))
        dx_ref[...] = dx
        dxb_ref[...] = dx.astype(BF16)
        _accumulate(dg_ref, jnp.sum(dh_f * xh, axis=0, keepdims=True))

    return _rows_call(
        body, name, s_len,
        [_row_spec(bs, d), _vec_spec(d), _row_spec(bs, d), _row_spec(bs, d)],
        [_row_spec(bs, d), _row_spec(bs, d), _vec_spec(d)],
        [jax.ShapeDtypeStruct((s_len, d), F32), jax.ShapeDtypeStruct((s_len, d), BF16),
         jax.ShapeDtypeStruct((1, d), F32)], bs, sequential=True)(x, gain, dh, dres)


def _loss_head(name, x, gain, target):
    s_len, d = x.shape
    bs = _tile(s_len, 256)

    def body(x_ref, g_ref, t_ref, lv_ref, dx_ref, dxb_ref, dg_ref):
        xf = x_ref[...]
        r = _rms_scale(xf)
        xh = xf * r
        err = xh * g_ref[...] - t_ref[...]
        dy = err * (1.0 / d)
        t = dy * g_ref[...]
        dx = r * (t - xh * jnp.mean(t * xh, axis=-1, keepdims=True))
        dx_ref[...] = dx
        dxb_ref[...] = dx.astype(BF16)
        _accumulate(lv_ref, jnp.sum(err * err, axis=0, keepdims=True))
        _accumulate(dg_ref, jnp.sum(dy * xh, axis=0, keepdims=True))

    return _rows_call(
        body, name, s_len, [_row_spec(bs, d), _vec_spec(d), _row_spec(bs, d)],
        [_vec_spec(d), _row_spec(bs, d), _row_spec(bs, d), _vec_spec(d)],
        [jax.ShapeDtypeStruct((1, d), F32), jax.ShapeDtypeStruct((s_len, d), F32),
         jax.ShapeDtypeStruct((s_len, d), BF16), jax.ShapeDtypeStruct((1, d), F32)],
        bs, sequential=True)(x, gain, target)


def _sigmoid(g):
    return 1.0 / (1.0 + jnp.exp(-g))


def _gate_a_fwd(name, ypre, proj, scale):
    s_len, e = ypre.shape
    bs = _tile(s_len, 256)

    def body(y_ref, g_ref, sc_ref, z_ref):
        g = g_ref[...]
        z_ref[...] = (y_ref[...] * sc_ref[...] * (g * _sigmoid(g))).astype(BF16)

    return _rows_call(body, name, s_len, [_row_spec(bs, e), _row_spec(bs, e, 1), _vec_spec(e)], _row_spec(bs, e),
                      jax.ShapeDtypeStruct((s_len, e), BF16), bs)(ypre, proj, scale)


def _gate_a_bwd(name, dz, ypre, proj, scale):
    s_len, e = ypre.shape
    bs = _tile(s_len, 256)

    def body(dz_ref, y_ref, g_ref, sc_ref, dy_ref, dproj_ref, dsc_ref):
        g = g_ref[...]
        sg = _sigmoid(g)
        silu = g * sg
        dz_f = dz_ref[...]
        ypre_f = y_ref[...]
        dys = dz_f * silu
        dy_ref[...] = (dys * sc_ref[...]).astype(BF16)
        dproj_ref[...] = (dz_f * (ypre_f * sc_ref[...]) * (sg * (1.0 + g * (1.0 - sg)))).astype(BF16)
        _accumulate(dsc_ref, jnp.sum(dys * ypre_f, axis=0, keepdims=True))

    return _rows_call(
        body, name, s_len, [_row_spec(bs, e), _row_spec(bs, e), _row_spec(bs, e, 1), _vec_spec(e)],
        [_row_spec(bs, e), _row_spec(bs, e, 1), _vec_spec(e)],
        [jax.ShapeDtypeStruct((s_len, e), BF16), jax.ShapeDtypeStruct((s_len, 2 * e), BF16),
         jax.ShapeDtypeStruct((1, e), F32)], bs, sequential=True)(dz, ypre, proj, scale)


def _merge_gate_fwd(name, outs, lses, gate, dils):
    s_len, e = gate.shape
    bs = _tile(s_len, 256)
    n = len(outs)

    def body(*refs):
        o_refs, l_refs, g_ref = refs[:n], refs[n:2 * n], refs[2 * n]
        m_ref, lj_ref, z_ref = refs[2 * n + 1:2 * n + 4]
        scratch = refs[2 * n + 4]
        for c, sl in enumerate(_col_blocks(e)):
            ls = [_from_residue_major(r, scratch.at[2 * j, c], dil, sl) for j, (r, dil) in enumerate(zip(l_refs, dils))]
            os_ = [_from_residue_major(r, scratch.at[2 * j + 1, c], dil, sl) for j, (r, dil) in enumerate(zip(o_refs, dils))]
            mx = functools.reduce(jnp.maximum, ls)
            ws = [jnp.exp(l - mx) for l in ls]
            den = functools.reduce(lambda a, b: a + b, ws)
            merged = functools.reduce(lambda a, b: a + b, [w * o for w, o in zip(ws, os_)]) / den
            g = g_ref[:, sl]
            m_ref[:, sl] = merged.astype(BF16)
            lj_ref[:, sl] = mx + jnp.log(den)
            z_ref[:, sl] = (merged * (g * _sigmoid(g))).astype(BF16)

    spec = _row_spec(bs, e)
    res_specs = [_res_spec(dil, bs, e) for dil in dils]
    return pl.pallas_call(
        body, name=name, grid=(s_len // bs,), in_specs=res_specs + res_specs + [spec], out_specs=[spec] * 3,
        out_shape=[jax.ShapeDtypeStruct((s_len, e), BF16), jax.ShapeDtypeStruct((s_len, e), F32),
                   jax.ShapeDtypeStruct((s_len, e), BF16)],
        scratch_shapes=[pltpu.VMEM((2 * n, e // HEAD_DIM, bs, HEAD_DIM), F32)],
        compiler_params=_params(("parallel",)))(*outs, *lses, gate)


def _gate_b_bwd(name, dz, merged, gate, lse, dils):
    s_len, e = gate.shape
    bs = _tile(s_len, 256)
    n = len(dils)

    def body(dz_ref, m_ref, g_ref, l_ref, dg_ref, *rest):
        out_refs, scratch = rest[:3 * n], rest[3 * n]
        for c, sl in enumerate(_col_blocks(e)):
            g = g_ref[:, sl]
            sg = _sigmoid(g)
            dz_f = dz_ref[:, sl]
            merged = m_ref[:, sl].astype(F32)
            dmerged = dz_f * (g * sg)
            dg_ref[:, sl] = (dz_f * merged * (sg * (1.0 + g * (1.0 - sg)))).astype(BF16)
            values = (dmerged, l_ref[:, sl],
                      jnp.broadcast_to(jnp.sum(dmerged * merged, axis=-1, keepdims=True), (bs, HEAD_DIM)))
            for t, val in enumerate(values):
                scr = scratch.at[t, c]
                scr[...] = val
                for j, dil in enumerate(dils):
                    _to_residue_major(out_refs[3 * j + t], scr, dil, sl)

    spec = _row_spec(bs, e)
    out_specs, out_shape = [spec], [jax.ShapeDtypeStruct((s_len, e), BF16)]
    for dil in dils:
        out_specs += [_res_spec(dil, bs, e)] * 3
        out_shape += [_res_shape(dil, s_len, e, BF16), _res_shape(dil, s_len, e, F32), _res_shape(dil, s_len, e, F32)]
    res = pl.pallas_call(
        body, name=name, grid=(s_len // bs,), in_specs=[spec] * 4, out_specs=out_specs, out_shape=out_shape,
        scratch_shapes=[pltpu.VMEM((3, e // HEAD_DIM, bs, HEAD_DIM), F32)],
        compiler_params=_params(("parallel",)))(dz, merged, gate, lse)
    return res[0], [tuple(res[1 + 3 * j:4 + 3 * j]) for j in range(n)]


def _kv_grad_prep(name, dk_accs, dv_accs, dils, cos, sin_inv):
    n = len(dils)
    e = dk_accs[0].shape[-1]
    s_len = dk_accs[0].shape[0] * dk_accs[0].shape[1]
    bs = _tile(s_len, 256)

    def body(*refs):
        dk_refs, dv_refs = refs[:n], refs[n:2 * n]
        c_ref, s_ref, dkb_ref, dvb_ref, scratch = refs[2 * n:]
        cos_t, sin_t = c_ref[...], s_ref[...]
        add = lambda a, b: a + b
        for c, sl in enumerate(_col_blocks(e)):
            dk = functools.reduce(add, [_from_residue_major(r, scratch.at[j, c], dil, sl)
                                        for j, (r, dil) in enumerate(zip(dk_refs, dils))])
            dkb_ref[:, sl] = _rope_apply(dk, cos_t, sin_t).astype(BF16)
            dv = functools.reduce(add, [_from_residue_major(r, scratch.at[n + j, c], dil, sl)
                                        for j, (r, dil) in enumerate(zip(dv_refs, dils))])
            dvb_ref[:, sl] = dv.astype(BF16)

    spec, rspec = _row_spec(bs, e), _row_spec(bs, HEAD_DIM)
    res_specs = [_res_spec(dil, bs, e) for dil in dils]
    return pl.pallas_call(
        body, name=name, grid=(s_len // bs,), in_specs=res_specs + res_specs + [rspec, rspec], out_specs=[spec, spec],
        out_shape=[jax.ShapeDtypeStruct((s_len, e), BF16)] * 2,
        scratch_shapes=[pltpu.VMEM((2 * n, e // HEAD_DIM, bs, HEAD_DIM), F32)],
        compiler_params=_params(("parallel",)))(*dk_accs, *dv_accs, cos, sin_inv)


def _pool_cols(e):
    return _tile(e // len(POOL_WINDOWS), 256)


def _window_sum(val, grp, s_len, forward):
    rows = lax.broadcasted_iota(jnp.int32, val.shape, 0)
    acc = val
    for level in range(len(POOL_WINDOWS)):
        step = 1 << level
        if forward:
            shifted = jnp.where(rows >= step, pltpu.roll(acc, step, 0), 0.0)
        else:
            shifted = jnp.where(rows < s_len - step, pltpu.roll(acc, s_len - step, 0), 0.0)
        acc = jnp.where(level <= grp, acc + shifted, acc)
    return acc


def _window_count(shape, grp):
    rows = lax.broadcasted_iota(jnp.int32, shape, 0)
    return jnp.minimum(rows + 1, jnp.left_shift(2, grp)).astype(F32)


def _pool_fwd(name, proj):
    s_len, e2 = proj.shape
    e = e2 // 2
    cb = _pool_cols(e)
    per_grp = e // len(POOL_WINDOWS) // cb
    assert POOL_WINDOWS == tuple(2 << g for g in range(len(POOL_WINDOWS)))

    def body(u_ref, p_ref):
        grp = pl.program_id(0)
        u = u_ref[...]
        total = _window_sum(u, grp, s_len, True)
        p_ref[...] = (total / _window_count(u.shape, grp) - u).astype(BF16)

    spec = pl.BlockSpec((s_len, cb), lambda g, c: (0, g * per_grp + c))
    return pl.pallas_call(
        body, name=name, grid=(len(POOL_WINDOWS), per_grp), in_specs=[spec], out_specs=spec,
        out_shape=jax.ShapeDtypeStruct((s_len, e), BF16), compiler_params=_params(("parallel", "parallel")))(proj)


def _pool_bwd(name, dpooled, dproj):
    s_len, e = dpooled.shape
    cb = _pool_cols(e)
    per_grp = e // len(POOL_WINDOWS) // cb

    def body(dp_ref, _, du_ref):
        grp = pl.program_id(0)
        dp = dp_ref[...]
        total = _window_sum(dp / _window_count(dp.shape, grp), grp, s_len, False)
        du_ref[...] = (total - dp).astype(BF16)

    spec = pl.BlockSpec((s_len, cb), lambda g, c: (0, g * per_grp + c))
    return pl.pallas_call(
        body, name=name, grid=(len(POOL_WINDOWS), per_grp), in_specs=[spec, ANY], out_specs=spec,
        out_shape=jax.ShapeDtypeStruct(dproj.shape, BF16), input_output_aliases={1: 0},
        compiler_params=_params(("parallel", "parallel")))(dpooled, dproj)


def _band_masks(nb, first):
    row = lax.broadcasted_iota(jnp.int32, (nb, nb), 0)
    col = lax.broadcasted_iota(jnp.int32, (nb, nb), 1)
    return col >= row + jnp.where(first, 2 * nb, 0), col <= row


def _dot(a, b, dims):
    return lax.dot_general(a, b, (dims, ((), ())), preferred_element_type=F32)


def _attn_fwd(name, window, q, k, v):
    dil, m, e = k.shape
    nb = window // dil
    nblk = m // nb
    heads = e // HEAD_DIM

    def body(q_ref, kc_ref, vc_ref, o_ref, l_ref, kp_ref, vp_ref):
        first = pl.program_id(1) == 0

        @pl.when(first)
        def _():
            kp_ref[...] = jnp.zeros_like(kp_ref)
            vp_ref[...] = jnp.zeros_like(vp_ref)

        mask_p, mask_c = _band_masks(nb, first)
        cols = _col_blocks(e)
        s_p = [jnp.where(mask_p, _dot(q_ref[:, sl], kp_ref[:, sl], NT), NEG_INF) for sl in cols]
        s_c = [jnp.where(mask_c, _dot(q_ref[:, sl], kc_ref[:, sl], NT), NEG_INF) for sl in cols]
        mx = [jnp.maximum(jnp.max(a, axis=-1, keepdims=True), jnp.max(b, axis=-1, keepdims=True))
              for a, b in zip(s_p, s_c)]
        p_p = [jnp.exp(a - m) for a, m in zip(s_p, mx)]
        p_c = [jnp.exp(a - m) for a, m in zip(s_c, mx)]
        den = [jnp.sum(a, axis=-1, keepdims=True) + jnp.sum(b, axis=-1, keepdims=True) for a, b in zip(p_p, p_c)]
        for h, sl in enumerate(cols):
            out = _dot(p_p[h].astype(BF16), vp_ref[:, sl], NN) + _dot(p_c[h].astype(BF16), vc_ref[:, sl], NN)
            o_ref[:, sl] = (out / den[h]).astype(BF16)
            l_ref[:, sl] = jnp.broadcast_to(mx[h] + jnp.log(den[h]), (nb, HEAD_DIM))
        kp_ref[...] = kc_ref[...]
        vp_ref[...] = vc_ref[...]

    blk = (None, nb, e)
    cur = lambda r, n: (r, n, 0)
    return pl.pallas_call(
        body, name=name, grid=(dil, nblk),
        in_specs=[pl.BlockSpec(blk, cur)] * 3,
        out_specs=[pl.BlockSpec(blk, cur), pl.BlockSpec(blk, cur)],
        out_shape=[jax.ShapeDtypeStruct((dil, m, e), BF16), jax.ShapeDtypeStruct((dil, m, e), F32)],
        scratch_shapes=[pltpu.VMEM((nb, e), BF16), pltpu.VMEM((nb, e), BF16)],
        compiler_params=_params(("parallel", "arbitrary")),
    )(q, k, v)


def _attn_bwd(name, window, scale, q, k, v, dout, lse, delta, cos, sin_inv, dk_acc, dv_acc):
    dil, m, e = k.shape
    nb = window // dil
    nblk = m // nb
    heads = e // HEAD_DIM

    accumulate = dk_acc is not None

    def body(k_ref, v_ref, q0_ref, qn_ref, do0_ref, don_ref, l0_ref, ln_ref, dl0_ref, dln_ref, c_ref, s_ref, *rest):
        if accumulate:
            dki_ref, dvi_ref = rest[:2]
            rest = rest[2:]
        dq_ref, dko_ref, dvo_ref, carry_ref, qc_ref, doc_ref, lc_ref, dlc_ref = rest
        n = pl.program_id(1)

        @pl.when(n == 0)
        def _():
            carry_ref[...] = jnp.zeros_like(carry_ref)
            qc_ref[...] = q0_ref[...]
            doc_ref[...] = do0_ref[...]
            lc_ref[...] = l0_ref[...]
            dlc_ref[...] = dl0_ref[...]

        mask_n, mask_c = _band_masks(nb, n == nblk - 1)
        cos_t, sin_t = c_ref[...], s_ref[...]
        cols = _col_blocks(e)
        stat = lambda ref, sl: ref[:, sl] if nb == HEAD_DIM else ref[:, sl][:, :1]
        s_c = [_dot(qc_ref[:, sl], k_ref[:, sl], NT) for sl in cols]
        s_n = [_dot(qn_ref[:, sl], k_ref[:, sl], NT) for sl in cols]
        dp_c = [_dot(doc_ref[:, sl], v_ref[:, sl], NT) for sl in cols]
        dp_n = [_dot(don_ref[:, sl], v_ref[:, sl], NT) for sl in cols]
        p_c = [jnp.where(mask_c, jnp.exp(s - stat(lc_ref, sl)), 0.0) for s, sl in zip(s_c, cols)]
        p_n = [jnp.where(mask_n, jnp.exp(s - stat(ln_ref, sl)), 0.0) for s, sl in zip(s_n, cols)]
        ds_c = [(p * (dp - stat(dlc_ref, sl))).astype(BF16) for p, dp, sl in zip(p_c, dp_c, cols)]
        ds_n = [(p * (dp - stat(dln_ref, sl))).astype(BF16) for p, dp, sl in zip(p_n, dp_n, cols)]
        for h, sl in enumerate(cols):
            dq = (carry_ref[:, sl] + _dot(ds_c[h], k_ref[:, sl], NN)) * scale
            dq_ref[:, sl] = _rope_apply(dq, cos_t, sin_t).astype(BF16)
        for h, sl in enumerate(cols):
            carry_ref[:, sl] = _dot(ds_n[h], k_ref[:, sl], NN)
        for h, sl in enumerate(cols):
            dk = _dot(ds_c[h], qc_ref[:, sl], TN) + _dot(ds_n[h], qn_ref[:, sl], TN)
            dv = _dot(p_c[h].astype(BF16), doc_ref[:, sl], TN) + _dot(p_n[h].astype(BF16), don_ref[:, sl], TN)
            dko_ref[:, sl] = dki_ref[:, sl] + dk if accumulate else dk
            dvo_ref[:, sl] = dvi_ref[:, sl] + dv if accumulate else dv
        qc_ref[...] = qn_ref[...]
        doc_ref[...] = don_ref[...]
        lc_ref[...] = ln_ref[...]
        dlc_ref[...] = dln_ref[...]

    blk = (None, nb, e)
    cur = lambda r, n: (r, n, 0)
    nxt = lambda r, n: (r, jnp.minimum(n + 1, nblk - 1), 0)
    first = lambda r, n: (r, 0, 0)
    rblk = (None, nb, HEAD_DIM)
    both = lambda shape: [pl.BlockSpec(shape, first), pl.BlockSpec(shape, nxt)]
    accs = (dk_acc, dv_acc) if accumulate else ()
    return pl.pallas_call(
        body, name=name, grid=(dil, nblk),
        in_specs=[pl.BlockSpec(blk, cur), pl.BlockSpec(blk, cur), *both(blk), *both(blk), *both(blk), *both(blk),
                  pl.BlockSpec(rblk, cur), pl.BlockSpec(rblk, cur)] + [pl.BlockSpec(blk, cur)] * len(accs),
        out_specs=[pl.BlockSpec(blk, cur)] * 3,
        out_shape=[jax.ShapeDtypeStruct((dil, m, e), BF16),
                   jax.ShapeDtypeStruct((dil, m, e), F32), jax.ShapeDtypeStruct((dil, m, e), F32)],
        scratch_shapes=[pltpu.VMEM((nb, e), F32), pltpu.VMEM((nb, e), BF16), pltpu.VMEM((nb, e), BF16),
                        pltpu.VMEM((nb, e), F32), pltpu.VMEM((nb, e), F32)],
        input_output_aliases={12: 1, 13: 2} if accumulate else {},
        compiler_params=_params(("parallel", "arbitrary")),
    )(k, v, q, q, dout, dout, lse, lse, delta, delta, cos, sin_inv, *accs)


def _rope_tables(s_len):
    inv_freq = 1.0 / (ROPE_THETA ** (jnp.arange(0, HEAD_DIM, 2, dtype=F32) / HEAD_DIM))
    ang = jnp.arange(s_len, dtype=F32)[:, None] * inv_freq[None, :]
    cos, sin = jnp.cos(ang), jnp.sin(ang)
    return jnp.concatenate([cos, cos], axis=1), jnp.concatenate([-sin, sin], axis=1)


def _row(vec):
    return vec.reshape(1, -1)


def _local_step(x, target, n_a, n_b, fetch, begin, emit):
    s_len, d = x.shape
    n_q = len(DILATED_PAIRS)
    cos, sin = _rope_tables(s_len)
    sin_inv = -sin
    q_scale = 1.0 / math.sqrt(HEAD_DIM)
    w = {}

    def need(group, after):
        for name, (layer, arr) in fetch(group, after).items():
            w.setdefault(name, {})[layer] = arr

    saved_a = []
    for i in range(n_a):
        need(f"a{i}", x)
        h = _rmsnorm_fwd(f"a{i}_norm", x, _row(w["norm_a"][i]))
        proj = _mm_act_w(f"a{i}_in", h, w["w_in_a"][i], out_dtype=F32)
        pooled = _pool_fwd(f"a{i}_pool", proj)
        ypre = _mm_grp_fwd(f"a{i}_grp", pooled, w["w_grp_a"][i])
        z = _gate_a_fwd(f"a{i}_gate", ypre, proj, _row(w["scale_a"][i]))
        x_next = _mm_act_w(f"a{i}_out", z, w["w_out_a"][i], out_dtype=F32, add=x)
        saved_a.append((x, h, proj, pooled, ypre, z))
        x = x_next

    x_kv = x
    need("kv", x)
    e = w["w_k"][0].shape[1]
    kv_in = _rmsnorm_fwd("kv_norm", x, _row(w["norm_kv"][0]))
    windows = [window for window, _ in DILATED_PAIRS]
    dils = tuple(dil for _, dil in DILATED_PAIRS)
    far_dils = tuple(dil for dil in dils if dil > 1)
    ks = _mm_act_w("kv_k", kv_in, w["w_k"][0], rope=(cos, sin, 1.0), dils=dils)
    vs = _mm_act_w("kv_v", kv_in, w["w_v"][0], dils=dils)

    saved_b = []
    for i in range(n_b):
        need(f"b{i}", x if i > 0 else vs[0])
        hs = _rmsnorm_fwd(f"b{i}_norm", x, _row(w["norm_b"][i]), dils=far_dils)
        hs = {1: hs[0], **{dil: h_d.reshape(s_len, d) for dil, h_d in zip(far_dils, hs[1:])}}
        qs = [_mm_act_w(f"b{i}_q{g}", hs[1], w["w_in_b"][i], rope=(cos, sin, q_scale), n_first=g, n_cols=1,
                        dils=(dil,))[0] for g, dil in enumerate(dils)]
        gate = _mm_act_w(f"b{i}_g", hs[1], w["w_in_b"][i], out_dtype=F32, n_first=n_q, n_cols=1)
        outs, lses = [], []
        for g in range(n_q):
            o_g, l_g = _attn_fwd(f"b{i}_attn{g}", windows[g], qs[g], ks[g], vs[g])
            outs.append(o_g)
            lses.append(l_g)
        merged, lse, z = _merge_gate_fwd(f"b{i}_merge", outs, lses, gate, dils)
        x_next = _mm_act_w(f"b{i}_out", z, w["w_out_b"][i], out_dtype=F32, add=x)
        saved_b.append((x, hs, qs, gate, merged, lse, z))
        x = x_next

    need("head", x)
    loss_vec, dx, dxb, g_norm_f = _loss_head("loss_head", x, _row(w["norm_f"][0]), target)

    small = {"norm_a": {}, "scale_a": {}, "norm_kv": {}, "norm_b": {}, "norm_f": {0: g_norm_f}}
    shard_rows = lambda g2: g2.reshape(N_CHIPS, g2.shape[0] // N_CHIPS, g2.shape[1])

    res_major = lambda t, dil: t.reshape(s_len // dil, dil, t.shape[1]).transpose(1, 0, 2)
    cos_r = [res_major(cos, dil) for dil in dils]
    sin_inv_r = [res_major(sin_inv, dil) for dil in dils]
    dk_accs = [None] * len(dils)
    dv_accs = [None] * len(dils)
    for i in reversed(range(n_b)):
        x_in, hs, qs, gate, merged, lse, z = saved_b[i]
        dz = _mm_grad_act(f"b{i}_dz", dxb, w["w_out_b"][i])
        g_out = shard_rows(_mm_grad_w(f"b{i}_gwo", z, dxb))
        dgate, stats = _gate_b_bwd(f"b{i}_dgate", dz, merged, gate, lse, dils)
        dh = _mm_grad_act(f"b{i}_dh{n_q}", dgate, w["w_in_b"][i], slot=n_q)
        g_in = _mm_grad_w(f"b{i}_gwi{n_q}", hs[1], dgate, col_shards=n_q + 1, slot=n_q)
        for g, dil in enumerate(dils):
            dout, lse_g, delta_g = stats[g]
            dq, dk_accs[g], dv_accs[g] = _attn_bwd(f"b{i}_dattn{g}", windows[g], q_scale, qs[g], ks[g], vs[g], dout,
                                                   lse_g, delta_g, cos_r[g], sin_inv_r[g], dk_accs[g], dv_accs[g])
            dh = _mm_grad_act(f"b{i}_dh{g}", dq if dil > 1 else dq[0], w["w_in_b"][i], add=dh, slot=g)
            g_in = _mm_grad_w(f"b{i}_gwi{g}", hs[dil], dq.reshape(s_len, e), col_shards=n_q + 1, slot=g, into=g_in)
        (dh,) = begin(f"b{i}", {"w_in_b": (i, g_in), "w_out_b": (i, g_out)}, (dh,))
        dx, dxb, small["norm_b"][i] = _rmsnorm_bwd(f"b{i}_dnorm", x_in, _row(w["norm_b"][i]), dh, dx)
        dx, dxb = emit(f"b{i}", (dx, dxb))

    dkb, dvb = _kv_grad_prep("kv_dprep", dk_accs, dv_accs, dils, cos, sin_inv)
    dkv = _mm_grad_act("kv_dk", dkb, w["w_k"][0])
    dkv = _mm_grad_act("kv_dv", dvb, w["w_v"][0], add=dkv)
    g_k = shard_rows(_mm_grad_w("kv_gwk", kv_in, dkb))
    g_v = shard_rows(_mm_grad_w("kv_gwv", kv_in, dvb))
    (dkv,) = begin("kv", {"w_k": (0, g_k), "w_v": (0, g_v)}, (dkv,))
    dx, dxb, small["norm_kv"][0] = _rmsnorm_bwd("kv_dnorm", x_kv, _row(w["norm_kv"][0]), dkv, dx)
    dx, dxb = emit("kv", (dx, dxb))

    for i in reversed(range(n_a)):
        x_in, h, proj, pooled, ypre, z = saved_a[i]
        dz = _mm_grad_act(f"a{i}_dz", dxb, w["w_out_a"][i])
        g_out = shard_rows(_mm_grad_w(f"a{i}_gwo", z, dxb))
        dypre, dproj, small["scale_a"][i] = _gate_a_bwd(f"a{i}_dgate", dz, ypre, proj, _row(w["scale_a"][i]))
        dpooled = _mm_grp_grad_act(f"a{i}_dgrp", dypre, w["w_grp_a"][i])
        g_grp = _mm_grp_grad_w(f"a{i}_gwg", pooled, dypre, len(POOL_WINDOWS))
        g_grp = g_grp.reshape(N_CHIPS, -1, g_grp.shape[-1])
        last = i == 0
        if last:
            (dpooled,) = emit(f"a{i}", begin(f"a{i}", {"w_grp_a": (i, g_grp), "w_out_a": (i, g_out)}, (dpooled,)))
        dproj = _pool_bwd(f"a{i}_dpool", dpooled, dproj)
        g_in = _mm_grad_w(f"a{i}_gwi", h, dproj, col_shards=N_CHIPS)
        if last:
            (dproj,) = emit(f"a{i}i", begin(f"a{i}i", {"w_in_a": (i, g_in)}, (dproj,)))
        dh = _mm_grad_act(f"a{i}_dh", dproj, w["w_in_a"][i])
        if not last:
            (dh,) = begin(f"a{i}", {"w_in_a": (i, g_in), "w_grp_a": (i, g_grp), "w_out_a": (i, g_out)}, (dh,))
        dx, dxb, small["norm_a"][i] = _rmsnorm_bwd(f"a{i}_dnorm", x_in, _row(w["norm_a"][i]), dh, dx)
        if not last:
            dx, dxb = emit(f"a{i}", (dx, dxb))

    return loss_vec, dx, small


BIG_WEIGHTS = ("w_in_a", "w_grp_a", "w_out_a", "w_k", "w_v", "w_in_b", "w_out_b")


def _pair_add(name, grad, recv, c_idx):
    _, r, cols = grad.shape
    half = r // 2
    rb = _tile(half, 256)
    nrb = half // rb

    def body(c_ref, g_ref, r_ref, o_ref):
        o_ref[...] = (g_ref[...].astype(F32) + r_ref[...].astype(F32)).astype(BF16)

    blk = (None, rb, cols)
    grid_spec = pltpu.PrefetchScalarGridSpec(
        num_scalar_prefetch=1, grid=(N_CHIPS, nrb),
        in_specs=[pl.BlockSpec(blk, lambda s, i, c: (s, c[0] * nrb + i, 0)), pl.BlockSpec(blk, lambda s, i, c: (s, i, 0))],
        out_specs=pl.BlockSpec(blk, lambda s, i, c: (s, i, 0)))
    return pl.pallas_call(body, name=name, grid_spec=grid_spec,
                          out_shape=jax.ShapeDtypeStruct((N_CHIPS, half, cols), BF16),
                          compiler_params=_params(("parallel", "parallel")))(c_idx, grad, recv)


def _final_add(name, part, recv, sc_idx, layer, n_layers, into=None):
    _, half, cols = part.shape
    rb = _tile(half, 256)
    nrb = half // rb
    n_peer = recv.shape[0]

    def body(sc_ref, p_ref, *refs):
        acc = p_ref[...].astype(F32)
        for r_ref in refs[:n_peer]:
            acc = acc + r_ref[...].astype(F32)
        refs[-1][...] = acc

    blk = (None, rb, cols)
    peer_spec = lambda k: pl.BlockSpec(blk, lambda i, sc: (k, i, 0))
    grid_spec = pltpu.PrefetchScalarGridSpec(
        num_scalar_prefetch=1, grid=(nrb,),
        in_specs=[pl.BlockSpec(blk, lambda i, sc: (sc[0], i, 0))] + [peer_spec(k) for k in range(n_peer)]
                 + ([] if into is None else [ANY]),
        out_specs=pl.BlockSpec(blk, lambda i, sc: (layer, sc[1] * nrb + i, 0)))
    extra = () if into is None else (into,)
    return pl.pallas_call(body, name=name, grid_spec=grid_spec,
                          out_shape=jax.ShapeDtypeStruct((n_layers, 2 * half, cols), F32),
                          input_output_aliases={} if into is None else {2 + n_peer: 0},
                          compiler_params=_params(("parallel",)))(sc_idx, part, *([recv] * n_peer), *extra)


def _cast_into_slot(name, arr, layer, s_idx, after=None):
    _, b, r, cols = arr.shape
    rb = _tile(r, 512)

    def body(s_ref, a_ref, *rest):
        rest[-1][...] = a_ref[...].astype(BF16)

    blk = (None, None, rb, cols)
    grid_spec = pltpu.PrefetchScalarGridSpec(
        num_scalar_prefetch=1, grid=(b, r // rb),
        in_specs=[pl.BlockSpec(blk, lambda j, i, s: (layer, j, i, 0))] + ([] if after is None else [ANY]),
        out_specs=pl.BlockSpec(blk, lambda j, i, s: (j, s[0], i, 0)))
    return pl.pallas_call(body, name=name, grid_spec=grid_spec,
                          out_shape=jax.ShapeDtypeStruct((b, N_CHIPS, r, cols), BF16),
                          compiler_params=_params(("parallel", "parallel")))(
                              s_idx, arr, *(() if after is None else (after,)))


def _sum_devices(name, gathered, own, me_idx):
    n_dev, p, d = gathered.shape

    def body(me_ref, g_ref, own_ref, o_ref):
        acc = None
        for j in range(n_dev):
            term = jnp.where(me_ref[0] == j, own_ref[...], g_ref[j])
            acc = term if acc is None else acc + term
        o_ref[...] = acc

    grid_spec = pltpu.PrefetchScalarGridSpec(
        num_scalar_prefetch=1, grid=(1,),
        in_specs=[pl.BlockSpec((n_dev, p, d), lambda i, me: (0, 0, 0)), pl.BlockSpec((p, d), lambda i, me: (0, 0))],
        out_specs=pl.BlockSpec((p, d), lambda i, me: (0, 0)))
    return pl.pallas_call(body, name=name, grid_spec=grid_spec, out_shape=jax.ShapeDtypeStruct((p, d), F32),
                          compiler_params=_params(("arbitrary",)))(me_idx, gathered, own)


def _adamw_block(w_ref, g_ref, m_ref, v_ref, d_ref, mo_ref, vo_ref):
    grad = g_ref[...]
    m_new = ADAM_B1 * m_ref[...] + (1.0 - ADAM_B1) * grad
    v_new = ADAM_B2 * v_ref[...] + (1.0 - ADAM_B2) * (grad * grad)
    m_hat = m_new / (1.0 - ADAM_B1 ** ADAM_STEP)
    v_hat = v_new / (1.0 - ADAM_B2 ** ADAM_STEP)
    d_ref[...] = -ADAM_LR * (m_hat / (jnp.sqrt(v_hat) + ADAM_EPS) + ADAM_WD * w_ref[...])
    mo_ref[...] = m_new
    vo_ref[...] = v_new


def _adamw(name, w, g, m, v):
    shape = w.shape
    cols = shape[-1]
    flat = lambda a: a.reshape(-1, cols)
    rows = flat(w).shape[0]
    bs = _tile(rows, 256)

    def body(*refs):
        _adamw_block(*refs)

    spec = _row_spec(bs, cols)
    outs = _rows_call(body, name, rows, [spec] * 4, [spec] * 3, [jax.ShapeDtypeStruct((rows, cols), F32)] * 3, bs)(
        flat(w), flat(g), flat(m), flat(v))
    return tuple(o.reshape(shape) for o in outs)


def _adamw_half(name, w, g, m, v, half_idx, into=None):
    n_l, r, cols = w.shape
    rb = _tile(r // 2, 256)
    nrb = r // 2 // rb

    def body(h_ref, w_ref, g_ref, m_ref, v_ref, *rest):
        _adamw_block(w_ref, g_ref, m_ref, v_ref, *rest[-3:])

    spec = pl.BlockSpec((None, rb, cols), lambda j, i, h: (j, h[0] * nrb + i, 0))
    extra = () if into is None else tuple(into)
    grid_spec = pltpu.PrefetchScalarGridSpec(
        num_scalar_prefetch=1, grid=(n_l, nrb), in_specs=[spec] * 4 + [ANY] * len(extra), out_specs=[spec] * 3)
    return tuple(pl.pallas_call(
        body, name=name, grid_spec=grid_spec, out_shape=[jax.ShapeDtypeStruct(w.shape, F32)] * 3,
        input_output_aliases={5 + k: k for k in range(len(extra))},
        compiler_params=_params(("parallel", "parallel")))(half_idx, w, g, m, v, *extra))


def _place():
    x, y, c = lax.axis_index("x"), lax.axis_index("y"), lax.axis_index("c")
    chips = [(1 - x, y), (x, 1 - y), (1 - x, 1 - y)]
    return x, y, c, chips


def _chip_index(chip):
    return 2 * chip[0] + chip[1]


def _comm_call(body, name, n_in, out_shape, scratch, aliases=None):
    return pl.pallas_call(body, name=name, in_specs=[ANY] * n_in, out_specs=[ANY] * len(out_shape), out_shape=out_shape,
                          scratch_shapes=scratch, input_output_aliases=aliases or {})


HBM_SPEC = pl.BlockSpec(memory_space=pltpu.HBM)
SEM_SPEC = pl.BlockSpec(memory_space=pltpu.SEMAPHORE)
SPLIT_PARAMS = pltpu.CompilerParams(has_side_effects=pltpu.SideEffectType.DATAFLOW_SIDE_EFFECTING)


def _in_hbm(arr):
    return pltpu.with_memory_space_constraint(arr, pltpu.HBM)


def _slot_half(ref, chip, core):
    half = ref.shape[2] // 2
    return ref.at[:, _chip_index(chip), pl.ds(core * half, half), :]


def _gather_start(name, bufs, carry=()):
    n, n_c = len(bufs), len(carry)

    def body(*refs):
        ins, (send_sems, recv_sems) = refs[:n], refs[n + n_c:n + n_c + 2]
        x, y, c, chips = _place()
        for a in range(n):
            block = _slot_half(ins[a], (x, y), c)
            for k, chip in enumerate(chips):
                pltpu.make_async_remote_copy(src_ref=block, dst_ref=block, send_sem=send_sems.at[3 * a + k],
                                             recv_sem=recv_sems.at[3 * a + k], device_id=(*chip, c),
                                             device_id_type=MESH).start()

    dma = pltpu.SemaphoreType.DMA
    thru = list(bufs) + list(carry)
    res = pl.pallas_call(
        body, name=name, in_specs=[HBM_SPEC] * (n + n_c), out_specs=[SEM_SPEC] * 2 + [HBM_SPEC] * (n + n_c),
        out_shape=[dma((3 * n,)), dma((3 * n,))] + [pltpu.HBM(a.shape, a.dtype) for a in thru],
        input_output_aliases={t: 2 + t for t in range(n + n_c)}, compiler_params=SPLIT_PARAMS,
    )(*[_in_hbm(a) for a in thru])
    return (res[0], res[1]), list(res[2:2 + n]), list(res[2 + n:])


def _gather_wait(name, sems, bufs, after):
    n = len(bufs)

    def body(*refs):
        ins, (send_sems, recv_sems) = refs[:n], refs[n:n + 2]
        x, y, c, chips = _place()
        for a in range(n):
            for k, chip in enumerate(chips):
                mine, theirs = _slot_half(ins[a], (x, y), c), _slot_half(ins[a], chip, c)
                copy = pltpu.make_async_remote_copy(src_ref=mine, dst_ref=theirs, send_sem=send_sems.at[3 * a + k],
                                                    recv_sem=recv_sems.at[3 * a + k], device_id=(*chip, c),
                                                    device_id_type=MESH)
                copy.wait_send()
                copy.wait_recv()

    res = pl.pallas_call(
        body, name=name, in_specs=[HBM_SPEC] * n + [SEM_SPEC, SEM_SPEC, ANY], out_specs=[HBM_SPEC] * n,
        out_shape=[pltpu.HBM(b.shape, b.dtype) for b in bufs], input_output_aliases={a: a for a in range(n)},
        compiler_params=SPLIT_PARAMS)(*bufs, *sems, after)
    return list(res)


def _gather_forward(name, bufs, smalls=()):
    n, n_small = len(bufs), len(smalls)

    def body(*refs):
        small_in = refs[n:n + n_small]
        outs = refs[n + n_small:2 * n + n_small]
        small_out = refs[2 * n + n_small:2 * n + 2 * n_small]
        send_sems, recv_sems, s_send, s_recv, s_local = refs[-5:]
        x, y, c, chips = _place()
        me, sibling = _chip_index((x, y)), (x, y, 1 - c)

        def forward(t, k, core):
            block = _slot_half(outs[t], chips[k], core)
            return pltpu.make_async_remote_copy(src_ref=block, dst_ref=block, send_sem=send_sems.at[t, k],
                                                recv_sem=recv_sems.at[t, k], device_id=sibling, device_id_type=MESH)

        def small_copy(j, k, slot):
            return pltpu.make_async_remote_copy(src_ref=small_in[j], dst_ref=small_out[j].at[slot],
                                                send_sem=s_send.at[j, k], recv_sem=s_recv.at[j, k],
                                                device_id=(*chips[k], c), device_id_type=MESH)

        local = []
        for t in range(n):
            for k in range(3):
                forward(t, k, c).start()
        for j in range(n_small):
            own = pltpu.make_async_copy(small_in[j], small_out[j].at[me], s_local.at[j])
            own.start()
            local.append(own)
            for k in range(3):
                small_copy(j, k, me).start()
        for t in range(n):
            for k in range(3):
                forward(t, k, 1 - c).wait_recv()
        for j in range(n_small):
            for k in range(3):
                small_copy(j, k, _chip_index(chips[k])).wait_recv()
        for t in range(n):
            for k in range(3):
                forward(t, k, c).wait_send()
        for j in range(n_small):
            for k in range(3):
                small_copy(j, k, me).wait_send()
        for own in local:
            own.wait()

    out_shape = [jax.ShapeDtypeStruct(b.shape, BF16) for b in bufs]
    out_shape += [jax.ShapeDtypeStruct((N_CHIPS,) + s.shape, F32) for s in smalls]
    dma = pltpu.SemaphoreType.DMA
    n_s = max(n_small, 1)
    res = _comm_call(body, name, n + n_small, out_shape,
                     [dma((n, 3)), dma((n, 3)), dma((n_s, 3)), dma((n_s, 3)), dma((n_s,))],
                     aliases={t: t for t in range(n)})(*bufs, *smalls)
    return list(res[:n]), list(res[n:])


def _halves_copy(grad_ref, land_ref, send_sems, recv_sems, t):
    x, y, c, _ = _place()
    half = grad_ref.shape[1] // 2
    return pltpu.make_async_remote_copy(
        src_ref=grad_ref.at[:, pl.ds((1 - c) * half, half), :], dst_ref=land_ref, send_sem=send_sems.at[t],
        recv_sem=recv_sems.at[t], device_id=(x, y, 1 - c), device_id_type=MESH)


def _exchange_start(name, grads, carry=()):
    n = len(grads)
    lands = [lax.empty((g.shape[0], g.shape[1] // 2, g.shape[2]), BF16) for g in grads]

    def body(*refs):
        send_sems, recv_sems = refs[2 * n + len(carry):2 * n + len(carry) + 2]
        for t in range(n):
            _halves_copy(refs[t], refs[n + t], send_sems, recv_sems, t).start()

    dma = pltpu.SemaphoreType.DMA
    thru = list(grads) + lands + list(carry)
    res = pl.pallas_call(
        body, name=name, in_specs=[HBM_SPEC] * len(thru), out_specs=[SEM_SPEC] * 2 + [HBM_SPEC] * len(thru),
        out_shape=[dma((n,)), dma((n,))] + [pltpu.HBM(a.shape, a.dtype) for a in thru],
        input_output_aliases={t: 2 + t for t in range(len(thru))}, compiler_params=SPLIT_PARAMS,
    )(*[_in_hbm(a) for a in thru])
    return (res[0], res[1]), list(res[2:2 + n]), list(res[2 + n:2 + 2 * n]), list(res[2 + 2 * n:])


def _exchange_wait(name, sems, grads, lands, after):
    n = len(grads)

    def body(*refs):
        send_sems, recv_sems = refs[2 * n:2 * n + 2]
        for t in range(n):
            copy = _halves_copy(refs[t], refs[n + t], send_sems, recv_sems, t)
            copy.wait_send()
            copy.wait_recv()

    res = pl.pallas_call(
        body, name=name, in_specs=[HBM_SPEC] * (2 * n) + [SEM_SPEC, SEM_SPEC, ANY], out_specs=[HBM_SPEC] * (2 * n),
        out_shape=[pltpu.HBM(a.shape, a.dtype) for a in grads + lands],
        input_output_aliases={t: t for t in range(2 * n)}, compiler_params=SPLIT_PARAMS)(*grads, *lands, *sems, after)
    return list(res[:n]), list(res[n:])


def _scatter_copy(part_ref, land_ref, send_sems, recv_sems, t, k, chip, c):
    return pltpu.make_async_remote_copy(
        src_ref=part_ref.at[_chip_index(chip)], dst_ref=land_ref.at[k], send_sem=send_sems.at[3 * t + k],
        recv_sem=recv_sems.at[3 * t + k], device_id=(*chip, c), device_id_type=MESH)


def _scatter_start(name, parts, carry=()):
    n, n_c = len(parts), len(carry)
    lands = [lax.empty((3,) + p.shape[1:], BF16) for p in parts]

    def body(*refs):
        p_in, l_in = refs[:n], refs[n:2 * n]
        send_sems, recv_sems = refs[2 * n + n_c:2 * n + n_c + 2]
        x, y, c, chips = _place()
        for t in range(n):
            for k, chip in enumerate(chips):
                _scatter_copy(p_in[t], l_in[t], send_sems, recv_sems, t, k, chip, c).start()

    dma = pltpu.SemaphoreType.DMA
    thru = list(parts) + lands + list(carry)
    res = pl.pallas_call(
        body, name=name, in_specs=[HBM_SPEC] * len(thru), out_specs=[SEM_SPEC] * 2 + [HBM_SPEC] * len(thru),
        out_shape=[dma((3 * n,)), dma((3 * n,))] + [pltpu.HBM(a.shape, a.dtype) for a in thru],
        input_output_aliases={t: 2 + t for t in range(len(thru))}, compiler_params=SPLIT_PARAMS,
    )(*[_in_hbm(a) for a in thru])
    return (res[0], res[1]), list(res[2:2 + n]), list(res[2 + n:2 + 2 * n]), list(res[2 + 2 * n:])


def _scatter_wait(name, sems, parts, lands, after):
    n = len(parts)

    def body(*refs):
        p_in, l_in = refs[:n], refs[n:2 * n]
        send_sems, recv_sems = refs[2 * n:2 * n + 2]
        x, y, c, chips = _place()
        for t in range(n):
            for k, chip in enumerate(chips):
                copy = _scatter_copy(p_in[t], l_in[t], send_sems, recv_sems, t, k, chip, c)
                copy.wait_send()
                copy.wait_recv()

    hbm_out = lambda a: pltpu.HBM(a.shape, a.dtype)
    res = pl.pallas_call(
        body, name=name, in_specs=[HBM_SPEC] * (2 * n) + [SEM_SPEC, SEM_SPEC, ANY], out_specs=[HBM_SPEC] * (2 * n),
        out_shape=[hbm_out(a) for a in parts + lands], input_output_aliases={t: t for t in range(2 * n)},
        compiler_params=SPLIT_PARAMS)(*parts, *lands, *sems, after)
    return list(res[:n]), list(res[n:])


def _share_items(fulls):
    return [(a, l) for a in range(len(fulls)) for l in range(fulls[a].shape[0])]


def _share_copy(ref, layer, send_sems, recv_sems, t):
    x, y, c, _ = _place()
    half = ref.shape[1] // 2
    return pltpu.make_async_remote_copy(
        src_ref=ref.at[layer, pl.ds(c * half, half), :], dst_ref=ref.at[layer, pl.ds((1 - c) * half, half), :],
        send_sem=send_sems.at[t], recv_sem=recv_sems.at[t], device_id=(x, y, 1 - c), device_id_type=MESH)


def _share_start(fulls):
    n, items = len(fulls), _share_items(fulls)

    def body(*refs):
        send_sems, recv_sems = refs[n:n + 2]
        x, y, c, _ = _place()
        for t, (a, l) in enumerate(items):
            half = refs[a].shape[1] // 2
            mine = refs[a].at[l, pl.ds(c * half, half), :]
            pltpu.make_async_remote_copy(src_ref=mine, dst_ref=mine, send_sem=send_sems.at[t], recv_sem=recv_sems.at[t],
                                         device_id=(x, y, 1 - c), device_id_type=MESH).start()

    dma = pltpu.SemaphoreType.DMA
    res = pl.pallas_call(
        body, name="grad_share_start", in_specs=[HBM_SPEC] * n, out_specs=[SEM_SPEC] * 2 + [HBM_SPEC] * n,
        out_shape=[dma((len(items),)), dma((len(items),))] + [pltpu.HBM(f.shape, f.dtype) for f in fulls],
        input_output_aliases={t: 2 + t for t in range(n)}, compiler_params=SPLIT_PARAMS,
    )(*[_in_hbm(f) for f in fulls])
    return (res[0], res[1]), list(res[2:])


def _share_wait(sems, fulls, after):
    n, items = len(fulls), _share_items(fulls)

    def body(*refs):
        send_sems, recv_sems = refs[n:n + 2]
        for t, (a, l) in enumerate(items):
            copy = _share_copy(refs[a], l, send_sems, recv_sems, t)
            copy.wait_send()
            copy.wait_recv()

    res = pl.pallas_call(
        body, name="grad_share_wait", in_specs=[HBM_SPEC] * n + [SEM_SPEC, SEM_SPEC, ANY], out_specs=[HBM_SPEC] * n,
        out_shape=[pltpu.HBM(f.shape, f.dtype) for f in fulls], input_output_aliases={t: t for t in range(n)},
        compiler_params=SPLIT_PARAMS)(*fulls, *sems, after)
    return list(res)


N_DEVICES = 8


def _device_index(x, y, c):
    return 4 * x + 2 * y + c


def _small_peers():
    x, y, c, _ = _place()
    flips = [(fx, fy, fc) for fx in (0, 1) for fy in (0, 1) for fc in (0, 1)][1:]
    return _device_index(x, y, c), [(x ^ fx, y ^ fy, c ^ fc) for fx, fy, fc in flips]


def _small_copy(p_ref, land_ref, send_sems, recv_sems, k, peer, slot):
    return pltpu.make_async_remote_copy(src_ref=p_ref, dst_ref=land_ref.at[slot], send_sem=send_sems.at[k],
                                        recv_sem=recv_sems.at[k], device_id=peer, device_id_type=MESH)


def _allgather_small_start(packed, carry=()):
    land = jnp.zeros((N_DEVICES,) + packed.shape, F32)

    def body(*refs):
        p_ref, l_ref = refs[:2]
        send_sems, recv_sems = refs[2 + len(carry):4 + len(carry)]
        me, peers = _small_peers()
        for k, peer in enumerate(peers):
            _small_copy(p_ref, l_ref, send_sems, recv_sems, k, peer, me).start()

    dma = pltpu.SemaphoreType.DMA
    thru = [packed, land] + list(carry)
    res = pl.pallas_call(
        body, name="small_allgather_start", in_specs=[HBM_SPEC] * len(thru),
        out_specs=[SEM_SPEC] * 2 + [HBM_SPEC] * len(thru),
        out_shape=[dma((N_DEVICES - 1,)), dma((N_DEVICES - 1,))] + [pltpu.HBM(a.shape, a.dtype) for a in thru],
        input_output_aliases={t: 2 + t for t in range(len(thru))}, compiler_params=SPLIT_PARAMS,
    )(*[_in_hbm(a) for a in thru])
    return (res[0], res[1]), res[2], res[3], list(res[4:])


def _allgather_small_wait(sems, packed, land, after):
    def body(p_ref, l_ref, send_sems, recv_sems, *_):
        _, peers = _small_peers()
        for k, peer in enumerate(peers):
            copy = _small_copy(p_ref, l_ref, send_sems, recv_sems, k, peer, _device_index(*peer))
            copy.wait_send()
            copy.wait_recv()

    res = pl.pallas_call(
        body, name="small_allgather_wait", in_specs=[HBM_SPEC] * 2 + [SEM_SPEC, SEM_SPEC, ANY],
        out_specs=[HBM_SPEC] * 2, out_shape=[pltpu.HBM(a.shape, a.dtype) for a in (packed, land)],
        input_output_aliases={0: 0, 1: 1}, compiler_params=SPLIT_PARAMS)(packed, land, *sems, after)
    return res[0], res[1]


PAD_ROWS = 8


def kernel(x, norm_a, w_in_a, w_grp_a, scale_a, w_out_a, norm_kv, w_k, w_v, norm_b, w_in_b, w_out_b, norm_f, loss_target, m_norm_a, m_w_in_a, m_w_grp_a, m_scale_a, m_w_out_a, m_norm_kv, m_w_k, m_w_v, m_norm_b, m_w_in_b, m_w_out_b, m_norm_f, v_norm_a, v_w_in_a, v_w_grp_a, v_scale_a, v_w_out_a, v_norm_kv, v_w_k, v_w_v, v_norm_b, v_w_in_b, v_w_out_b, v_norm_f):
    weights = dict(norm_a=norm_a, w_in_a=w_in_a, w_grp_a=w_grp_a, scale_a=scale_a, w_out_a=w_out_a, norm_kv=norm_kv,
                   w_k=w_k, w_v=w_v, norm_b=norm_b, w_in_b=w_in_b, w_out_b=w_out_b, norm_f=norm_f)
    moments_m = dict(norm_a=m_norm_a, w_in_a=m_w_in_a, w_grp_a=m_w_grp_a, scale_a=m_scale_a, w_out_a=m_w_out_a,
                     norm_kv=m_norm_kv, w_k=m_w_k, w_v=m_w_v, norm_b=m_norm_b, w_in_b=m_w_in_b, w_out_b=m_w_out_b,
                     norm_f=m_norm_f)
    moments_v = dict(norm_a=v_norm_a, w_in_a=v_w_in_a, w_grp_a=v_w_grp_a, scale_a=v_scale_a, w_out_a=v_w_out_a,
                     norm_kv=v_norm_kv, w_k=v_w_k, w_v=v_w_v, norm_b=v_norm_b, w_in_b=v_w_in_b, w_out_b=v_w_out_b,
                     norm_f=v_norm_f)
    names = list(weights)
    d = x.shape[-1]
    c_idx = lax.axis_index("c").astype(jnp.int32).reshape(1)
    s_me = 2 * lax.axis_index("x") + lax.axis_index("y")
    s_idx = s_me.astype(jnp.int32).reshape(1)

    def as_lbrc(name):
        a = weights[name]
        if name == "w_grp_a":
            return a
        if a.ndim == 2:
            return a.reshape(1, 1, *a.shape)
        return a.reshape(a.shape[0], 1, *a.shape[1:])

    n_a, n_b = norm_a.shape[0], norm_b.shape[0]
    group_weights = {**{f"a{i}": [("w_in_a", i), ("w_grp_a", i), ("w_out_a", i)] for i in range(n_a)},
                     "kv": [("w_k", 0), ("w_v", 0)],
                     **{f"b{i}": [("w_in_b", i), ("w_out_b", i)] for i in range(n_b)}}
    group_order = [f"a{i}" for i in range(n_a)] + ["kv"] + [f"b{i}" for i in range(n_b)]
    slots, slot_groups = [], []
    small_full, started = {}, {}

    def start_group(gi, carry=()):
        sems, bufs, carry = _gather_start(f"gather_start_{group_order[gi]}", [slots[t] for t in slot_groups[gi]], carry)
        started[gi] = (sems, bufs)
        return carry

    previous = None
    for gi, group in enumerate(group_order):
        slot_groups.append(list(range(len(slots), len(slots) + len(group_weights[group]))))
        for name, l in group_weights[group]:
            slots.append(_cast_into_slot(f"cast_{name}{l}", as_lbrc(name), l, s_idx, previous))
            previous = slots[-1] if gi > 0 else None
        if gi == 0:
            (previous,) = start_group(0, (norm_kv.reshape(1, -1),))

    def gathered_form(name, g):
        if name in ("w_in_a", "w_in_b"):
            return g[0]
        if name == "w_grp_a":
            return g.reshape(g.shape[0], -1, g.shape[-1])
        return g.reshape(-1, g.shape[-1])

    def fetch(group, after):
        if group == "head":
            return {"norm_f": (0, norm_f)}
        gi = group_order.index(group)
        sems, bufs = started[gi]
        if gi == 0:
            after = slots[-1]
        bufs = _gather_wait(f"gather_wait_{group}", sems, bufs, after)
        bufs, small_g = _gather_forward(f"gather_forward_{group}", bufs, [norm_a, scale_a] if gi == 0 else [])
        out = {name: (l, gathered_form(name, g)) for (name, l), g in zip(group_weights[group], bufs)}
        if gi == 0:
            for name, g in zip(("norm_a", "scale_a"), small_g):
                small_full[name] = g.transpose(1, 0, 2).reshape(g.shape[1], -1)
        layer = group_weights[group][0][1]
        if group.startswith("a"):
            gain_name, gain = "norm_a", small_full["norm_a"][layer]
            out.update(scale_a=(layer, small_full["scale_a"][layer]))
        elif group == "kv":
            gain_name, gain = "norm_kv", norm_kv
        else:
            gain_name, gain = "norm_b", norm_b[layer]
        gain = gain.reshape(1, -1)
        ahead = [gi + 1] + ([gi + 2] if gi + 2 < len(group_order) and group_order[gi + 1] == "kv" else [])
        for gj in ahead:
            if gj < len(group_order) and gj not in started:
                (gain,) = start_group(gj, (gain,))
        out[gain_name] = (layer, gain)
        return out

    exchanging, in_flight = {}, []

    def begin(group, grads_of, carry):
        keys = [(k, grads_of[k][0]) for k in grads_of]
        sems, grads, lands, carry = _exchange_start(f"grad_exchange_start_{group}", [grads_of[k][1] for k in grads_of],
                                                    tuple(carry))
        exchanging[group] = (keys, sems, grads, lands)
        return carry

    def emit(group, carry):
        keys, sems, grads, lands = exchanging.pop(group)
        grads, recv1 = _exchange_wait(f"grad_exchange_wait_{group}", sems, grads, lands, after=carry[0])
        parts = [_pair_add(f"pair_add_{k}{l}", g, r, c_idx) for (k, l), g, r in zip(keys, grads, recv1)]
        sems, parts, lands, carry = _scatter_start(f"grad_scatter_start_{group}", parts, tuple(carry))
        in_flight.append((group, keys, sems, parts, lands))
        return carry

    loss_vec, grad_x, small = _local_step(x[0], loss_target[0], n_a, n_b, fetch, begin, emit)

    small_order = [("norm_a", i) for i in range(n_a)] + [("scale_a", i) for i in range(n_a)] + [("norm_kv", 0)] + \
                  [("norm_b", i) for i in range(norm_b.shape[0])] + [("norm_f", 0)]
    pad = lambda vec: jnp.pad(vec, ((0, PAD_ROWS - 1), (0, 0)))
    packed = jnp.concatenate([pad(loss_vec)] + [pad(small[n][i]) for n, i in small_order], axis=0)
    small_sems, packed, small_land, _ = _allgather_small_start(packed)

    sc_idx = jnp.concatenate([s_idx, c_idx])
    fulls = {name: None for name in BIG_WEIGHTS}
    for group, keys, sems, parts, lands in in_flight:
        parts, lands = _scatter_wait(f"grad_scatter_wait_{group}", sems, parts, lands, after=packed)
        for (name, i), p, r in zip(keys, parts, lands):
            n_layers = 1 if weights[name].ndim == 2 else weights[name].shape[0]
            fulls[name] = _final_add(f"final_add_{name}{i}", p, r, sc_idx, i, n_layers, into=fulls[name])
    share_sems, sharing = _share_start([fulls[name] for name in BIG_WEIGHTS])

    deltas, new_m, new_v, grads = {}, {}, {}, {}
    as_full = lambda n, full: (a.reshape(full.shape) for a in (weights[n], moments_m[n], moments_v[n]))
    mine = {}
    for n, full in zip(BIG_WEIGHTS, sharing):
        w3, m3, v3 = as_full(n, full)
        mine[n] = _adamw_half(f"adamw_mine_{n}", w3, full, m3, v3, c_idx)
    shared = _share_wait(share_sems, sharing, after=mine[BIG_WEIGHTS[-1]][2])
    for n, full in zip(BIG_WEIGHTS, shared):
        w3, m3, v3 = as_full(n, full)
        shape = weights[n].shape
        dl, mn, vn = _adamw_half(f"adamw_theirs_{n}", w3, full, m3, v3, 1 - c_idx, into=mine[n])
        deltas[n], new_m[n], new_v[n], grads[n] = dl.reshape(shape), mn.reshape(shape), vn.reshape(shape), full.reshape(shape)

    def update(n):
        shape = weights[n].shape
        as2d = (lambda a: a.reshape(1, -1)) if len(shape) == 1 else (lambda a: a)
        dl, mn, vn = _adamw(f"adamw_{n}", as2d(weights[n]), as2d(grads[n]), as2d(moments_m[n]), as2d(moments_v[n]))
        deltas[n], new_m[n], new_v[n] = dl.reshape(shape), mn.reshape(shape), vn.reshape(shape)

    packed, small_land = _allgather_small_wait(small_sems, packed, small_land, after=new_v[BIG_WEIGHTS[-1]])
    me_idx = _device_index(lax.axis_index("x"), lax.axis_index("y"), lax.axis_index("c")).astype(jnp.int32).reshape(1)
    totals = _sum_devices("small_sum", small_land, packed, me_idx)
    loss = 0.5 * jnp.sum(totals[0]) / d
    small_tot = {}
    for j, (n, i) in enumerate(small_order):
        small_tot.setdefault(n, []).append(totals[PAD_ROWS * (j + 1)])
    shard_w = norm_a.shape[1]
    for n in ("norm_a", "scale_a"):
        full = jnp.stack(small_tot[n])
        grads[n] = lax.dynamic_slice_in_dim(full, s_me * shard_w, shard_w, axis=1)
    grads["norm_kv"] = small_tot["norm_kv"][0]
    grads["norm_b"] = jnp.stack(small_tot["norm_b"])
    grads["norm_f"] = small_tot["norm_f"][0]
    for n in names:
        if n not in BIG_WEIGHTS:
            update(n)

    return (loss, grad_x[None], *[grads[n] for n in names], *[deltas[n] for n in names],
            *[new_m[n] for n in names], *[new_v[n] for n in names])
```

```python
import functools
import math

import jax
import jax.numpy as jnp
from jax import lax
from jax.experimental import pallas as pl
from jax.experimental.pallas import tpu as pltpu

F32 = jnp.float32
BF16 = jnp.bfloat16

HEAD_DIM = 128
POOL_WINDOWS = (2, 4, 8, 16)
DILATED_PAIRS = ((128, 1), (512, 4), (2048, 16))
ROPE_THETA = 10000.0
RMS_EPS = 1e-6
NEG_INF = -1e30
N_CHIPS = 4

ADAM_LR = 0.001
ADAM_B1 = 0.9
ADAM_B2 = 0.999
ADAM_EPS = 1e-08
ADAM_WD = 0.01
ADAM_STEP = 10

VMEM_LIMIT_BYTES = 56 * 1024 * 1024
MESH = pl.DeviceIdType.MESH
ANY = pl.BlockSpec(memory_space=pl.ANY)


def _tile(n, pref):
    t = min(n, pref)
    assert n % t == 0, (n, pref)
    return t


def _params(sem=None):
    return pltpu.CompilerParams(dimension_semantics=sem, vmem_limit_bytes=VMEM_LIMIT_BYTES)


def _mm(name, a, b, *, grid2, nk, a_blk, a_map, b_blk, b_map, outs, dims, epi=None, epi_in=(), epi_specs=(),
        acc_shape=None, epi_scratch=(), into=None):
    n_epi, n_out = len(epi_in), len(outs)

    def body(*refs):
        a_ref, b_ref = refs[0], refs[1]
        e_refs = refs[2:2 + n_epi]
        first_out = 2 + n_epi + (0 if into is None else 1)
        o_refs = refs[first_out:first_out + n_out]
        s_refs = refs[first_out + n_out + (0 if nk == 1 else 1):]

        def contrib():
            a_val = a_ref[...]
            if a_val.ndim == 3:
                a_val = a_val.reshape(-1, a_val.shape[-1])
            return lax.dot_general(a_val, b_ref[...], (dims, ((), ())), preferred_element_type=F32)

        def finish(acc):
            if epi is None:
                o_refs[0][...] = acc.reshape(o_refs[0].shape).astype(o_refs[0].dtype)
            else:
                epi(acc, e_refs, o_refs, s_refs)

        if nk == 1:
            finish(contrib())
        else:
            acc_ref = refs[first_out + n_out]
            k = pl.program_id(2)

            @pl.when(k == 0)
            def _():
                acc_ref[...] = contrib()

            @pl.when(k > 0)
            def _():
                acc_ref[...] += contrib()

            @pl.when(k == nk - 1)
            def _():
                finish(acc_ref[...])

    scratch = ([] if nk == 1 else [pltpu.VMEM(acc_shape, F32)]) + list(epi_scratch)
    extra_in, extra_specs, aliases = (), (), {}
    if into is not None:
        extra_in, extra_specs, aliases = (into[0],), (ANY,), {2 + n_epi: into[1]}
    res = pl.pallas_call(
        body, name=name, grid=(grid2[0], grid2[1], nk),
        in_specs=[pl.BlockSpec(a_blk, a_map), pl.BlockSpec(b_blk, b_map), *epi_specs, *extra_specs],
        out_specs=[pl.BlockSpec(blk, imap) for _, blk, imap, _ in outs],
        out_shape=[jax.ShapeDtypeStruct(shape, dtype) for shape, _, _, dtype in outs],
        scratch_shapes=scratch, input_output_aliases=aliases,
        compiler_params=_params(("parallel", "parallel", "arbitrary")),
    )(a, b, *epi_in, *extra_in)
    return res[0] if n_out == 1 else tuple(res)


NN = ((1,), (0,))
NT = ((1,), (1,))
TN = ((0,), (0,))


def _rope_apply(t, cos, sin):
    return t * cos + pltpu.roll(t, HEAD_DIM // 2, 1) * sin


def _epi_add(acc, e_refs, o_refs, s_refs):
    o_refs[0][...] = (acc + e_refs[0][...]).astype(o_refs[0].dtype)


def _col_blocks(width):
    return [slice(c * HEAD_DIM, (c + 1) * HEAD_DIM) for c in range(width // HEAD_DIM)]


def _col_scratch(rows, width):
    return pltpu.VMEM((width // HEAD_DIM, rows, HEAD_DIM), F32)


def _to_residue_major(o_ref, scr, d, sl):
    if d == 1:
        o_ref[0, :, sl] = scr[...].astype(o_ref.dtype)
        return
    rows = scr.shape[0] // d
    for r in range(d):
        o_ref[r, :, sl] = scr[pl.ds(r, rows, stride=d), :].astype(o_ref.dtype)


def _from_residue_major(i_ref, scr, d, sl):
    if d == 1:
        return i_ref[0, :, sl].astype(F32)
    rows = i_ref.shape[1]
    for r in range(d):
        scr[pl.ds(r, rows, stride=d), :] = i_ref[r, :, sl].astype(F32)
    return scr[...]


def _make_epi_orders(dils, rope_scale):
    def epi(acc, e_refs, o_refs, s_refs):
        if rope_scale is not None:
            cos = e_refs[0][...]
            sin = e_refs[1][...]
        for c, sl in enumerate(_col_blocks(acc.shape[1])):
            scr = s_refs[0].at[c]
            scr[...] = acc[:, sl] if rope_scale is None else _rope_apply(acc[:, sl], cos, sin) * rope_scale
            for o_ref, d in zip(o_refs, dils):
                _to_residue_major(o_ref, scr, d, sl)
    return epi


def _make_epi_token_order(d, has_add):
    def epi(acc, e_refs, o_refs, s_refs):
        o_ref = o_refs[0]
        if d == 1:
            o_ref[...] = acc + e_refs[0][...] if has_add else acc
            return
        rows = acc.shape[0] // d
        for c, sl in enumerate(_col_blocks(acc.shape[1])):
            scr = s_refs[0].at[c]
            for r in range(d):
                scr[pl.ds(r, rows, stride=d), :] = acc[r * rows:(r + 1) * rows, sl]
            o_ref[:, sl] = scr[...] + e_refs[0][:, sl] if has_add else scr[...]
    return epi


def _mm_act_w(name, a, w, *, out_dtype=BF16, add=None, rope=None, n_first=0, n_cols=None, dils=None):
    s_len, k_len = a.shape
    bm = _tile(s_len, 1024)
    epi, epi_in, epi_specs, epi_scratch = None, (), (), ()
    if w.ndim == 3:
        ns, _, c = w.shape
        ns_used = ns if n_cols is None else n_cols
        bn = _tile(c, 1024)
        sub = c // bn
        grid2 = (ns_used * sub, s_len // bm)
        b_blk, b_map = (None, k_len, bn), (lambda j, i, k: (j // sub + n_first, 0, j % sub))
        n_len = ns_used * c
    else:
        n_len = w.shape[1]
        bn = _tile(n_len, 1024)
        grid2 = (n_len // bn, s_len // bm)
        b_blk, b_map = (k_len, bn), (lambda j, i, k: (0, j))
    if add is not None:
        epi, epi_in = _epi_add, (add,)
        epi_specs = (pl.BlockSpec((bm, bn), lambda j, i, k: (i, j)),)
    outs = [((s_len, n_len), (bm, bn), lambda j, i, k: (i, j), out_dtype)]
    if dils is not None:
        if rope is not None:
            epi_in = rope[:2]
            epi_specs = (pl.BlockSpec((bm, HEAD_DIM), lambda j, i, k: (i, 0)),) * 2
        epi = _make_epi_orders(dils, None if rope is None else rope[2])
        epi_scratch = (_col_scratch(bm, bn),)
        outs = [((d, s_len // d, n_len), (d, bm // d, bn), lambda j, i, k: (0, i, j), BF16) for d in dils]
    res = _mm(name, a, w, grid2=grid2, nk=1, a_blk=(bm, k_len), a_map=lambda j, i, k: (i, 0),
              b_blk=b_blk, b_map=b_map, outs=outs, dims=NN, epi=epi, epi_in=epi_in, epi_specs=epi_specs,
              epi_scratch=epi_scratch)
    return (res,) if dils is not None and len(dils) == 1 else res


def _mm_grad_act(name, dy, w, *, add=None, slot=None):
    if slot is not None:
        d = 1 if dy.ndim == 2 else dy.shape[0]
        s_len = dy.shape[-2] * d
        _, k_len, c = w.shape
        bm, bn = _tile(s_len, 1024), _tile(k_len, 1024)
        a_blk, a_map = ((bm, c), lambda j, i, k: (i, 0)) if dy.ndim == 2 else ((d, bm // d, c), lambda j, i, k: (0, i, 0))
        epi_in = () if add is None else (add,)
        return _mm(name, dy, w, grid2=(k_len // bn, s_len // bm), nk=1, a_blk=a_blk, a_map=a_map,
                   b_blk=(None, bn, c), b_map=lambda j, i, k: (slot, j, 0),
                   outs=[((s_len, k_len), (bm, bn), lambda j, i, k: (i, j), F32)], dims=NT,
                   epi=_make_epi_token_order(d, add is not None), epi_in=epi_in,
                   epi_specs=(pl.BlockSpec((bm, bn), lambda j, i, k: (i, j)),) * len(epi_in),
                   epi_scratch=(_col_scratch(bm, bn),) if d > 1 else ())
    s_len, n_len = dy.shape
    bm = _tile(s_len, 1024)
    if w.ndim == 3:
        ns, k_len, c = w.shape
        bk, nk = c, ns
        bn = _tile(k_len, 1024)
        b_blk, b_map = (None, bn, c), (lambda j, i, k: (k, j, 0))
    else:
        k_len = w.shape[0]
        bk = _tile(n_len, 2048)
        nk = n_len // bk
        bn = _tile(k_len, 1024)
        b_blk, b_map = (bn, bk), (lambda j, i, k: (j, k))
    epi, epi_in, epi_specs = None, (), ()
    if add is not None:
        epi, epi_in = _epi_add, (add,)
        epi_specs = (pl.BlockSpec((bm, bn), lambda j, i, k: (i, j)),)
    return _mm(name, dy, w, grid2=(k_len // bn, s_len // bm), nk=nk, a_blk=(bm, bk), a_map=lambda j, i, k: (i, k),
               b_blk=b_blk, b_map=b_map, outs=[((s_len, k_len), (bm, bn), lambda j, i, k: (i, j), F32)],
               dims=NT, epi=epi, epi_in=epi_in, epi_specs=epi_specs, acc_shape=(bm, bn))


def _mm_grad_w(name, a, dy, *, col_shards=None, slot=None, into=None):
    s_len, k_len = a.shape
    n_len = dy.shape[1]
    bk = _tile(s_len, 2048)
    bm = _tile(k_len, 1024)
    if slot is not None:
        bn = _tile(n_len, 1024)
        out = ((col_shards, k_len, n_len), (None, bm, bn), lambda j, i, k: (slot, i, j), BF16)
    elif col_shards:
        c = n_len // col_shards
        bn = _tile(c, 1024)
        sub = c // bn
        out = ((col_shards, k_len, c), (None, bm, bn), lambda j, i, k: (j // sub, i, j % sub), BF16)
    else:
        bn = _tile(n_len, 1024)
        out = ((k_len, n_len), (bm, bn), lambda j, i, k: (i, j), BF16)
    return _mm(name, a, dy, grid2=(n_len // bn, k_len // bm), nk=s_len // bk,
               a_blk=(bk, bm), a_map=lambda j, i, k: (k, i), b_blk=(bk, bn), b_map=lambda j, i, k: (k, j),
               outs=[out], dims=TN, acc_shape=(bm, bn), into=None if into is None else (into, 0))


def _mm_grp_fwd(name, pooled, wg):
    s_len, e = pooled.shape
    ng, g, _ = wg.shape
    bm = _tile(s_len, 1024)
    return _mm(name, pooled, wg, grid2=(ng, s_len // bm), nk=1, a_blk=(bm, g), a_map=lambda j, i, k: (i, j),
               b_blk=(None, g, g), b_map=lambda j, i, k: (j, 0, 0),
               outs=[((s_len, e), (bm, g), lambda j, i, k: (i, j), F32)], dims=NN)


def _mm_grp_grad_act(name, dy, wg):
    s_len, e = dy.shape
    ng, g, _ = wg.shape
    bm = _tile(s_len, 1024)
    return _mm(name, dy, wg, grid2=(ng, s_len // bm), nk=1, a_blk=(bm, g), a_map=lambda j, i, k: (i, j),
               b_blk=(None, g, g), b_map=lambda j, i, k: (j, 0, 0),
               outs=[((s_len, e), (bm, g), lambda j, i, k: (i, j), F32)], dims=NT)


def _mm_grp_grad_w(name, pooled, dy, ng):
    s_len, e = pooled.shape
    g = e // ng
    bk = _tile(s_len, 1024)
    return _mm(name, pooled, dy, grid2=(ng, 1), nk=s_len // bk, a_blk=(bk, g), a_map=lambda j, i, k: (k, j),
               b_blk=(bk, g), b_map=lambda j, i, k: (k, j),
               outs=[((N_CHIPS, ng, g // N_CHIPS, g), (N_CHIPS, None, g // N_CHIPS, g), lambda j, i, k: (0, j, 0, 0), BF16)],
               dims=TN, acc_shape=(g, g))


def _row_spec(bs, width, col=0):
    return pl.BlockSpec((bs, width), lambda i: (i, col))


def _vec_spec(width):
    return pl.BlockSpec((1, width), lambda i: (0, 0))


def _rows_call(body, name, s_len, in_specs, out_specs, out_shape, bs, aliases=None, sequential=False):
    return pl.pallas_call(
        body, name=name, grid=(s_len // bs,), in_specs=in_specs, out_specs=out_specs, out_shape=out_shape,
        input_output_aliases=aliases or {},
        compiler_params=_params(("arbitrary",) if sequential else ("parallel",)))


def _accumulate(ref, part):
    i = pl.program_id(0)

    @pl.when(i == 0)
    def _():
        ref[...] = part

    @pl.when(i > 0)
    def _():
        ref[...] += part


def _rms_scale(xf):
    return lax.rsqrt(jnp.mean(xf * xf, axis=-1, keepdims=True) + RMS_EPS)


def _res_spec(dil, bs, width):
    return pl.BlockSpec((dil, bs // dil, width), lambda i: (0, i, 0))


def _res_shape(dil, s_len, width, dtype):
    return jax.ShapeDtypeStruct((dil, s_len // dil, width), dtype)


def _rmsnorm_fwd(name, x, gain, dils=()):
    s_len, d = x.shape
    bs = _tile(s_len, 256)

    def body(x_ref, g_ref, h_ref, *rest):
        xf = x_ref[...]
        h = (xf * _rms_scale(xf)) * g_ref[...]
        h_ref[...] = h.astype(BF16)
        if dils:
            for c, sl in enumerate(_col_blocks(d)):
                scr = rest[-1].at[c]
                scr[...] = h[:, sl]
                for o_ref, dil in zip(rest[:-1], dils):
                    _to_residue_major(o_ref, scr, dil, sl)

    res = pl.pallas_call(
        body, name=name, grid=(s_len // bs,), in_specs=[_row_spec(bs, d), _vec_spec(d)],
        out_specs=[_row_spec(bs, d)] + [_res_spec(dil, bs, d) for dil in dils],
        out_shape=[jax.ShapeDtypeStruct((s_len, d), BF16)] + [_res_shape(dil, s_len, d, BF16) for dil in dils],
        scratch_shapes=[_col_scratch(bs, d)] if dils else [],
        compiler_params=_params(("parallel",)))(x, gain)
    return res[0] if not dils else tuple(res)


def _rmsnorm_bwd(name, x, gain, dh, dres):
    s_len, d = x.shape
    bs = _tile(s_len, 256)

    def body(x_ref, g_ref, dh_ref, dres_ref, dx_ref, dxb_ref, dg_ref):
        xf = x_ref[...]
        r = _rms_scale(xf)
        xh = xf * r
        dh_f = dh_ref[...]
        t = dh_f * g_ref[...]
        dx = dres_ref[...] + r * (t - xh * jnp.mean(t * xh, axis=-1, keepdims=True))
        dx_ref[...] = dx
        dxb_ref[...] = dx.astype(BF16)
        _accumulate(dg_ref, jnp.sum(dh_f * xh, axis=0, keepdims=True))

    return _rows_call(
        body, name, s_len,
        [_row_spec(bs, d), _vec_spec(d), _row_spec(bs, d), _row_spec(bs, d)],
        [_row_spec(bs, d), _row_spec(bs, d), _vec_spec(d)],
        [jax.ShapeDtypeStruct((s_len, d), F32), jax.ShapeDtypeStruct((s_len, d), BF16),
         jax.ShapeDtypeStruct((1, d), F32)], bs, sequential=True)(x, gain, dh, dres)


def _loss_head(name, x, gain, target):
    s_len, d = x.shape
    bs = _tile(s_len, 256)

    def body(x_ref, g_ref, t_ref, lv_ref, dx_ref, dxb_ref, dg_ref):
        xf = x_ref[...]
        r = _rms_scale(xf)
        xh = xf * r
        err = xh * g_ref[...] - t_ref[...]
        dy = err * (1.0 / d)
        t = dy * g_ref[...]
        dx = r * (t - xh * jnp.mean(t * xh, axis=-1, keepdims=True))
        dx_ref[...] = dx
        dxb_ref[...] = dx.astype(BF16)
        _accumulate(lv_ref, jnp.sum(err * err, axis=0, keepdims=True))
        _accumulate(dg_ref, jnp.sum(dy * xh, axis=0, keepdims=True))

    return _rows_call(
        body, name, s_len, [_row_spec(bs, d), _vec_spec(d), _row_spec(bs, d)],
        [_vec_spec(d), _row_spec(bs, d), _row_spec(bs, d), _vec_spec(d)],
        [jax.ShapeDtypeStruct((1, d), F32), jax.ShapeDtypeStruct((s_len, d), F32),
         jax.ShapeDtypeStruct((s_len, d), BF16), jax.ShapeDtypeStruct((1, d), F32)],
        bs, sequential=True)(x, gain, target)


def _sigmoid(g):
    return 1.0 / (1.0 + jnp.exp(-g))


def _gate_a_fwd(name, ypre, proj, scale):
    s_len, e = ypre.shape
    bs = _tile(s_len, 256)

    def body(y_ref, g_ref, sc_ref, z_ref):
        g = g_ref[...]
        z_ref[...] = (y_ref[...] * sc_ref[...] * (g * _sigmoid(g))).astype(BF16)

    return _rows_call(body, name, s_len, [_row_spec(bs, e), _row_spec(bs, e, 1), _vec_spec(e)], _row_spec(bs, e),
                      jax.ShapeDtypeStruct((s_len, e), BF16), bs)(ypre, proj, scale)


def _gate_a_bwd(name, dz, ypre, proj, scale):
    s_len, e = ypre.shape
    bs = _tile(s_len, 256)

    def body(dz_ref, y_ref, g_ref, sc_ref, dy_ref, dproj_ref, dsc_ref):
        g = g_ref[...]
        sg = _sigmoid(g)
        silu = g * sg
        dz_f = dz_ref[...]
        ypre_f = y_ref[...]
        dys = dz_f * silu
        dy_ref[...] = (dys * sc_ref[...]).astype(BF16)
        dproj_ref[...] = (dz_f * (ypre_f * sc_ref[...]) * (sg * (1.0 + g * (1.0 - sg)))).astype(BF16)
        _accumulate(dsc_ref, jnp.sum(dys * ypre_f, axis=0, keepdims=True))

    return _rows_call(
        body, name, s_len, [_row_spec(bs, e), _row_spec(bs, e), _row_spec(bs, e, 1), _vec_spec(e)],
        [_row_spec(bs, e), _row_spec(bs, e, 1), _vec_spec(e)],
        [jax.ShapeDtypeStruct((s_len, e), BF16), jax.ShapeDtypeStruct((s_len, 2 * e), BF16),
         jax.ShapeDtypeStruct((1, e), F32)], bs, sequential=True)(dz, ypre, proj, scale)


def _merge_gate_fwd(name, outs, lses, gate, dils):
    s_len, e = gate.shape
    bs = _tile(s_len, 256)
    n = len(outs)

    def body(*refs):
        o_refs, l_refs, g_ref = refs[:n], refs[n:2 * n], refs[2 * n]
        m_ref, lj_ref, z_ref = refs[2 * n + 1:2 * n + 4]
        scratch = refs[2 * n + 4]
        for c, sl in enumerate(_col_blocks(e)):
            ls = [_from_residue_major(r, scratch.at[2 * j, c], dil, sl) for j, (r, dil) in enumerate(zip(l_refs, dils))]
            os_ = [_from_residue_major(r, scratch.at[2 * j + 1, c], dil, sl) for j, (r, dil) in enumerate(zip(o_refs, dils))]
            mx = functools.reduce(jnp.maximum, ls)
            ws = [jnp.exp(l - mx) for l in ls]
            den = functools.reduce(lambda a, b: a + b, ws)
            merged = functools.reduce(lambda a, b: a + b, [w * o for w, o in zip(ws, os_)]) / den
            g = g_ref[:, sl]
            m_ref[:, sl] = merged.astype(BF16)
            lj_ref[:, sl] = mx + jnp.log(den)
            z_ref[:, sl] = (merged * (g * _sigmoid(g))).astype(BF16)

    spec = _row_spec(bs, e)
    res_specs = [_res_spec(dil, bs, e) for dil in dils]
    return pl.pallas_call(
        body, name=name, grid=(s_len // bs,), in_specs=res_specs + res_specs + [spec], out_specs=[spec] * 3,
        out_shape=[jax.ShapeDtypeStruct((s_len, e), BF16), jax.ShapeDtypeStruct((s_len, e), F32),
                   jax.ShapeDtypeStruct((s_len, e), BF16)],
        scratch_shapes=[pltpu.VMEM((2 * n, e // HEAD_DIM, bs, HEAD_DIM), F32)],
        compiler_params=_params(("parallel",)))(*outs, *lses, gate)


def _gate_b_bwd(name, dz, merged, gate, lse, dils):
    s_len, e = gate.shape
    bs = _tile(s_len, 256)
    n = len(dils)
    plan = [(j, t) for j, dil in enumerate(dils) for t in range(3) if not (t == 1 and dil == 1)]

    def body(dz_ref, m_ref, g_ref, l_ref, dg_ref, *rest):
        out_refs, scratch = rest[:len(plan)], rest[len(plan)]
        for c, sl in enumerate(_col_blocks(e)):
            g = g_ref[:, sl]
            sg = _sigmoid(g)
            dz_f = dz_ref[:, sl]
            merged = m_ref[:, sl].astype(F32)
            dmerged = dz_f * (g * sg)
            dg_ref[:, sl] = (dz_f * merged * (sg * (1.0 + g * (1.0 - sg)))).astype(BF16)
            values = (dmerged, l_ref[:, sl],
                      jnp.broadcast_to(jnp.sum(dmerged * merged, axis=-1, keepdims=True), (bs, HEAD_DIM)))
            for t, val in enumerate(values):
                scratch[t, c] = val
            for o_ref, (j, t) in zip(out_refs, plan):
                _to_residue_major(o_ref, scratch.at[t, c], dils[j], sl)

    spec = _row_spec(bs, e)
    out_specs, out_shape = [spec], [jax.ShapeDtypeStruct((s_len, e), BF16)]
    for j, t in plan:
        out_specs.append(_res_spec(dils[j], bs, e))
        out_shape.append(_res_shape(dils[j], s_len, e, BF16 if t == 0 else F32))
    res = pl.pallas_call(
        body, name=name, grid=(s_len // bs,), in_specs=[spec] * 4, out_specs=out_specs, out_shape=out_shape,
        scratch_shapes=[pltpu.VMEM((3, e // HEAD_DIM, bs, HEAD_DIM), F32)],
        compiler_params=_params(("parallel",)))(dz, merged, gate, lse)
    made = {jt: arr for jt, arr in zip(plan, res[1:])}
    return res[0], [tuple(made.get((j, t), lse.reshape(1, s_len, e)) for t in range(3)) for j in range(n)]


def _kv_grad_prep(name, dk_accs, dv_accs, dils, cos, sin_inv):
    n = len(dils)
    e = dk_accs[0].shape[-1]
    s_len = dk_accs[0].shape[0] * dk_accs[0].shape[1]
    bs = _tile(s_len, 256)

    def body(*refs):
        dk_refs, dv_refs = refs[:n], refs[n:2 * n]
        c_ref, s_ref, dkb_ref, dvb_ref, scratch = refs[2 * n:]
        cos_t, sin_t = c_ref[...], s_ref[...]
        add = lambda a, b: a + b
        for c, sl in enumerate(_col_blocks(e)):
            dk = functools.reduce(add, [_from_residue_major(r, scratch.at[j, c], dil, sl)
                                        for j, (r, dil) in enumerate(zip(dk_refs, dils))])
            dkb_ref[:, sl] = _rope_apply(dk, cos_t, sin_t).astype(BF16)
            dv = functools.reduce(add, [_from_residue_major(r, scratch.at[n + j, c], dil, sl)
                                        for j, (r, dil) in enumerate(zip(dv_refs, dils))])
            dvb_ref[:, sl] = dv.astype(BF16)

    spec, rspec = _row_spec(bs, e), _row_spec(bs, HEAD_DIM)
    res_specs = [_res_spec(dil, bs, e) for dil in dils]
    return pl.pallas_call(
        body, name=name, grid=(s_len // bs,), in_specs=res_specs + res_specs + [rspec, rspec], out_specs=[spec, spec],
        out_shape=[jax.ShapeDtypeStruct((s_len, e), BF16)] * 2,
        scratch_shapes=[pltpu.VMEM((2 * n, e // HEAD_DIM, bs, HEAD_DIM), F32)],
        compiler_params=_params(("parallel",)))(*dk_accs, *dv_accs, cos, sin_inv)


def _pool_cols(e):
    return _tile(e // len(POOL_WINDOWS), 256)


def _window_sum(val, grp, s_len, forward):
    rows = lax.broadcasted_iota(jnp.int32, val.shape, 0)
    acc = val
    for level in range(len(POOL_WINDOWS)):
        step = 1 << level
        if forward:
            shifted = jnp.where(rows >= step, pltpu.roll(acc, step, 0), 0.0)
        else:
            shifted = jnp.where(rows < s_len - step, pltpu.roll(acc, s_len - step, 0), 0.0)
        acc = jnp.where(level <= grp, acc + shifted, acc)
    return acc


def _window_count(shape, grp):
    rows = lax.broadcasted_iota(jnp.int32, shape, 0)
    return jnp.minimum(rows + 1, jnp.left_shift(2, grp)).astype(F32)


def _pool_fwd(name, proj):
    s_len, e2 = proj.shape
    e = e2 // 2
    cb = _pool_cols(e)
    per_grp = e // len(POOL_WINDOWS) // cb
    assert POOL_WINDOWS == tuple(2 << g for g in range(len(POOL_WINDOWS)))

    def body(u_ref, p_ref):
        grp = pl.program_id(0)
        u = u_ref[...]
        total = _window_sum(u, grp, s_len, True)
        p_ref[...] = (total / _window_count(u.shape, grp) - u).astype(BF16)

    spec = pl.BlockSpec((s_len, cb), lambda g, c: (0, g * per_grp + c))
    return pl.pallas_call(
        body, name=name, grid=(len(POOL_WINDOWS), per_grp), in_specs=[spec], out_specs=spec,
        out_shape=jax.ShapeDtypeStruct((s_len, e), BF16), compiler_params=_params(("parallel", "parallel")))(proj)


def _pool_bwd(name, dpooled, dproj):
    s_len, e = dpooled.shape
    cb = _pool_cols(e)
    per_grp = e // len(POOL_WINDOWS) // cb

    def body(dp_ref, _, du_ref):
        grp = pl.program_id(0)
        dp = dp_ref[...]
        total = _window_sum(dp / _window_count(dp.shape, grp), grp, s_len, False)
        du_ref[...] = (total - dp).astype(BF16)

    spec = pl.BlockSpec((s_len, cb), lambda g, c: (0, g * per_grp + c))
    return pl.pallas_call(
        body, name=name, grid=(len(POOL_WINDOWS), per_grp), in_specs=[spec, ANY], out_specs=spec,
        out_shape=jax.ShapeDtypeStruct(dproj.shape, BF16), input_output_aliases={1: 0},
        compiler_params=_params(("parallel", "parallel")))(dpooled, dproj)


def _band_masks(nb, first):
    row = lax.broadcasted_iota(jnp.int32, (nb, nb), 0)
    col = lax.broadcasted_iota(jnp.int32, (nb, nb), 1)
    return col >= row + jnp.where(first, 2 * nb, 0), col <= row


def _dot(a, b, dims):
    return lax.dot_general(a, b, (dims, ((), ())), preferred_element_type=F32)


def _attn_fwd(name, window, q, k, v):
    dil, m, e = k.shape
    nb = window // dil
    nblk = m // nb
    heads = e // HEAD_DIM

    def body(q_ref, kc_ref, vc_ref, o_ref, l_ref, kp_ref, vp_ref):
        first = pl.program_id(1) == 0

        @pl.when(first)
        def _():
            kp_ref[...] = jnp.zeros_like(kp_ref)
            vp_ref[...] = jnp.zeros_like(vp_ref)

        mask_p, mask_c = _band_masks(nb, first)
        cols = _col_blocks(e)
        s_p = [jnp.where(mask_p, _dot(q_ref[:, sl], kp_ref[:, sl], NT), NEG_INF) for sl in cols]
        s_c = [jnp.where(mask_c, _dot(q_ref[:, sl], kc_ref[:, sl], NT), NEG_INF) for sl in cols]
        mx = [jnp.maximum(jnp.max(a, axis=-1, keepdims=True), jnp.max(b, axis=-1, keepdims=True))
              for a, b in zip(s_p, s_c)]
        p_p = [jnp.exp(a - m) for a, m in zip(s_p, mx)]
        p_c = [jnp.exp(a - m) for a, m in zip(s_c, mx)]
        den = [jnp.sum(a, axis=-1, keepdims=True) + jnp.sum(b, axis=-1, keepdims=True) for a, b in zip(p_p, p_c)]
        for h, sl in enumerate(cols):
            out = _dot(p_p[h].astype(BF16), vp_ref[:, sl], NN) + _dot(p_c[h].astype(BF16), vc_ref[:, sl], NN)
            o_ref[:, sl] = (out / den[h]).astype(BF16)
            l_ref[:, sl] = jnp.broadcast_to(mx[h] + jnp.log(den[h]), (nb, HEAD_DIM))
        kp_ref[...] = kc_ref[...]
        vp_ref[...] = vc_ref[...]

    blk = (None, nb, e)
    cur = lambda r, n: (r, n, 0)
    return pl.pallas_call(
        body, name=name, grid=(dil, nblk),
        in_specs=[pl.BlockSpec(blk, cur)] * 3,
        out_specs=[pl.BlockSpec(blk, cur), pl.BlockSpec(blk, cur)],
        out_shape=[jax.ShapeDtypeStruct((dil, m, e), BF16), jax.ShapeDtypeStruct((dil, m, e), F32)],
        scratch_shapes=[pltpu.VMEM((nb, e), BF16), pltpu.VMEM((nb, e), BF16)],
        compiler_params=_params(("parallel", "arbitrary")),
    )(q, k, v)


def _attn_bwd(name, window, scale, q, k, v, dout, lse, delta, cos, sin_inv, dk_acc, dv_acc):
    dil, m, e = k.shape
    nb = window // dil
    nblk = m // nb
    heads = e // HEAD_DIM

    accumulate = dk_acc is not None

    def body(k_ref, v_ref, q0_ref, qn_ref, do0_ref, don_ref, l0_ref, ln_ref, dl0_ref, dln_ref, c_ref, s_ref, *rest):
        if accumulate:
            dki_ref, dvi_ref = rest[:2]
            rest = rest[2:]
        dq_ref, dko_ref, dvo_ref, carry_ref, qc_ref, doc_ref, lc_ref, dlc_ref = rest
        n = pl.program_id(1)

        @pl.when(n == 0)
        def _():
            carry_ref[...] = jnp.zeros_like(carry_ref)
            qc_ref[...] = q0_ref[...]
            doc_ref[...] = do0_ref[...]
            lc_ref[...] = l0_ref[...]
            dlc_ref[...] = dl0_ref[...]

        mask_n, mask_c = _band_masks(nb, n == nblk - 1)
        cos_t, sin_t = c_ref[...], s_ref[...]
        cols = _col_blocks(e)
        stat = lambda ref, sl: ref[:, sl] if nb == HEAD_DIM else ref[:, sl][:, :1]
        s_c = [_dot(qc_ref[:, sl], k_ref[:, sl], NT) for sl in cols]
        s_n = [_dot(qn_ref[:, sl], k_ref[:, sl], NT) for sl in cols]
        dp_c = [_dot(doc_ref[:, sl], v_ref[:, sl], NT) for sl in cols]
        dp_n = [_dot(don_ref[:, sl], v_ref[:, sl], NT) for sl in cols]
        p_c = [jnp.where(mask_c, jnp.exp(s - stat(lc_ref, sl)), 0.0) for s, sl in zip(s_c, cols)]
        p_n = [jnp.where(mask_n, jnp.exp(s - stat(ln_ref, sl)), 0.0) for s, sl in zip(s_n, cols)]
        ds_c = [(p * (dp - stat(dlc_ref, sl))).astype(BF16) for p, dp, sl in zip(p_c, dp_c, cols)]
        ds_n = [(p * (dp - stat(dln_ref, sl))).astype(BF16) for p, dp, sl in zip(p_n, dp_n, cols)]
        for h, sl in enumerate(cols):
            dq = (carry_ref[:, sl] + _dot(ds_c[h], k_ref[:, sl], NN)) * scale
            dq_ref[:, sl] = _rope_apply(dq, cos_t, sin_t).astype(BF16)
        for h, sl in enumerate(cols):
            carry_ref[:, sl] = _dot(ds_n[h], k_ref[:, sl], NN)
        for h, sl in enumerate(cols):
            dk = _dot(ds_c[h], qc_ref[:, sl], TN) + _dot(ds_n[h], qn_ref[:, sl], TN)
            dv = _dot(p_c[h].astype(BF16), doc_ref[:, sl], TN) + _dot(p_n[h].astype(BF16), don_ref[:, sl], TN)
            dko_ref[:, sl] = dki_ref[:, sl] + dk if accumulate else dk
            dvo_ref[:, sl] = dvi_ref[:, sl] + dv if accumulate else dv
        qc_ref[...] = qn_ref[...]
        doc_ref[...] = don_ref[...]
        lc_ref[...] = ln_ref[...]
        dlc_ref[...] = dln_ref[...]

    blk = (None, nb, e)
    cur = lambda r, n: (r, n, 0)
    nxt = lambda r, n: (r, jnp.minimum(n + 1, nblk - 1), 0)
    first = lambda r, n: (r, 0, 0)
    rblk = (None, nb, HEAD_DIM)
    both = lambda shape: [pl.BlockSpec(shape, first), pl.BlockSpec(shape, nxt)]
    accs = (dk_acc, dv_acc) if accumulate else ()
    return pl.pallas_call(
        body, name=name, grid=(dil, nblk),
        in_specs=[pl.BlockSpec(blk, cur), pl.BlockSpec(blk, cur), *both(blk), *both(blk), *both(blk), *both(blk),
                  pl.BlockSpec(rblk, cur), pl.BlockSpec(rblk, cur)] + [pl.BlockSpec(blk, cur)] * len(accs),
        out_specs=[pl.BlockSpec(blk, cur)] * 3,
        out_shape=[jax.ShapeDtypeStruct((dil, m, e), BF16),
                   jax.ShapeDtypeStruct((dil, m, e), F32), jax.ShapeDtypeStruct((dil, m, e), F32)],
        scratch_shapes=[pltpu.VMEM((nb, e), F32), pltpu.VMEM((nb, e), BF16), pltpu.VMEM((nb, e), BF16),
                        pltpu.VMEM((nb, e), F32), pltpu.VMEM((nb, e), F32)],
        input_output_aliases={12: 1, 13: 2} if accumulate else {},
        compiler_params=_params(("parallel", "arbitrary")),
    )(k, v, q, q, dout, dout, lse, lse, delta, delta, cos, sin_inv, *accs)


def _rope_tables(s_len):
    inv_freq = 1.0 / (ROPE_THETA ** (jnp.arange(0, HEAD_DIM, 2, dtype=F32) / HEAD_DIM))
    ang = jnp.arange(s_len, dtype=F32)[:, None] * inv_freq[None, :]
    cos, sin = jnp.cos(ang), jnp.sin(ang)
    return jnp.concatenate([cos, cos], axis=1), jnp.concatenate([-sin, sin], axis=1)


def _row(vec):
    return vec.reshape(1, -1)


def _local_step(x, target, n_a, n_b, fetch, begin, emit):
    s_len, d = x.shape
    n_q = len(DILATED_PAIRS)
    cos, sin = _rope_tables(s_len)
    sin_inv = -sin
    q_scale = 1.0 / math.sqrt(HEAD_DIM)
    w = {}

    def need(group, after):
        for name, (layer, arr) in fetch(group, after).items():
            w.setdefault(name, {})[layer] = arr

    saved_a = []
    for i in range(n_a):
        need(f"a{i}", x)
        h = _rmsnorm_fwd(f"a{i}_norm", x, _row(w["norm_a"][i]))
        proj = _mm_act_w(f"a{i}_in", h, w["w_in_a"][i], out_dtype=F32)
        pooled = _pool_fwd(f"a{i}_pool", proj)
        ypre = _mm_grp_fwd(f"a{i}_grp", pooled, w["w_grp_a"][i])
        z = _gate_a_fwd(f"a{i}_gate", ypre, proj, _row(w["scale_a"][i]))
        x_next = _mm_act_w(f"a{i}_out", z, w["w_out_a"][i], out_dtype=F32, add=x)
        saved_a.append((x, h, proj, pooled, ypre, z))
        x = x_next

    x_kv = x
    need("kv", x)
    e = w["w_k"][0].shape[1]
    kv_in = _rmsnorm_fwd("kv_norm", x, _row(w["norm_kv"][0]))
    windows = [window for window, _ in DILATED_PAIRS]
    dils = tuple(dil for _, dil in DILATED_PAIRS)
    far_dils = tuple(dil for dil in dils if dil > 1)
    ks = _mm_act_w("kv_k", kv_in, w["w_k"][0], rope=(cos, sin, 1.0), dils=dils)
    vs = _mm_act_w("kv_v", kv_in, w["w_v"][0], dils=dils)

    saved_b = []
    for i in range(n_b):
        need(f"b{i}", x if i > 0 else vs[0])
        hs = _rmsnorm_fwd(f"b{i}_norm", x, _row(w["norm_b"][i]), dils=far_dils)
        hs = {1: hs[0], **{dil: h_d.reshape(s_len, d) for dil, h_d in zip(far_dils, hs[1:])}}
        qs = [_mm_act_w(f"b{i}_q{g}", hs[1], w["w_in_b"][i], rope=(cos, sin, q_scale), n_first=g, n_cols=1,
                        dils=(dil,))[0] for g, dil in enumerate(dils)]
        gate = _mm_act_w(f"b{i}_g", hs[1], w["w_in_b"][i], out_dtype=F32, n_first=n_q, n_cols=1)
        outs, lses = [], []
        for g in range(n_q):
            o_g, l_g = _attn_fwd(f"b{i}_attn{g}", windows[g], qs[g], ks[g], vs[g])
            outs.append(o_g)
            lses.append(l_g)
        merged, lse, z = _merge_gate_fwd(f"b{i}_merge", outs, lses, gate, dils)
        x_next = _mm_act_w(f"b{i}_out", z, w["w_out_b"][i], out_dtype=F32, add=x)
        saved_b.append((x, hs, qs, gate, merged, lse, z))
        x = x_next

    need("head", x)
    loss_vec, dx, dxb, g_norm_f = _loss_head("loss_head", x, _row(w["norm_f"][0]), target)

    small = {"norm_a": {}, "scale_a": {}, "norm_kv": {}, "norm_b": {}, "norm_f": {0: g_norm_f}}
    shard_rows = lambda g2: g2.reshape(N_CHIPS, g2.shape[0] // N_CHIPS, g2.shape[1])

    res_major = lambda t, dil: t.reshape(s_len // dil, dil, t.shape[1]).transpose(1, 0, 2)
    cos_r = [res_major(cos, dil) for dil in dils]
    sin_inv_r = [res_major(sin_inv, dil) for dil in dils]
    dk_accs = [None] * len(dils)
    dv_accs = [None] * len(dils)
    for i in reversed(range(n_b)):
        x_in, hs, qs, gate, merged, lse, z = saved_b[i]
        dz = _mm_grad_act(f"b{i}_dz", dxb, w["w_out_b"][i])
        g_out = shard_rows(_mm_grad_w(f"b{i}_gwo", z, dxb))
        dgate, stats = _gate_b_bwd(f"b{i}_dgate", dz, merged, gate, lse, dils)
        dh = _mm_grad_act(f"b{i}_dh{n_q}", dgate, w["w_in_b"][i], slot=n_q)
        g_in = _mm_grad_w(f"b{i}_gwi{n_q}", hs[1], dgate, col_shards=n_q + 1, slot=n_q)
        for g, dil in enumerate(dils):
            dout, lse_g, delta_g = stats[g]
            dq, dk_accs[g], dv_accs[g] = _attn_bwd(f"b{i}_dattn{g}", windows[g], q_scale, qs[g], ks[g], vs[g], dout,
                                                   lse_g, delta_g, cos_r[g], sin_inv_r[g], dk_accs[g], dv_accs[g])
            dh = _mm_grad_act(f"b{i}_dh{g}", dq if dil > 1 else dq[0], w["w_in_b"][i], add=dh, slot=g)
            g_in = _mm_grad_w(f"b{i}_gwi{g}", hs[dil], dq.reshape(s_len, e), col_shards=n_q + 1, slot=g, into=g_in)
        (dh,) = begin(f"b{i}", {"w_in_b": (i, g_in), "w_out_b": (i, g_out)}, (dh,))
        dx, dxb, small["norm_b"][i] = _rmsnorm_bwd(f"b{i}_dnorm", x_in, _row(w["norm_b"][i]), dh, dx)
        dx, dxb = emit(f"b{i}", (dx, dxb))

    dkb, dvb = _kv_grad_prep("kv_dprep", dk_accs, dv_accs, dils, cos, sin_inv)
    dkv = _mm_grad_act("kv_dk", dkb, w["w_k"][0])
    dkv = _mm_grad_act("kv_dv", dvb, w["w_v"][0], add=dkv)
    g_k = shard_rows(_mm_grad_w("kv_gwk", kv_in, dkb))
    g_v = shard_rows(_mm_grad_w("kv_gwv", kv_in, dvb))
    (dkv,) = begin("kv", {"w_k": (0, g_k), "w_v": (0, g_v)}, (dkv,))
    dx, dxb, small["norm_kv"][0] = _rmsnorm_bwd("kv_dnorm", x_kv, _row(w["norm_kv"][0]), dkv, dx)
    dx, dxb = emit("kv", (dx, dxb))

    for i in reversed(range(n_a)):
        x_in, h, proj, pooled, ypre, z = saved_a[i]
        dz = _mm_grad_act(f"a{i}_dz", dxb, w["w_out_a"][i])
        g_out = shard_rows(_mm_grad_w(f"a{i}_gwo", z, dxb))
        dypre, dproj, small["scale_a"][i] = _gate_a_bwd(f"a{i}_dgate", dz, ypre, proj, _row(w["scale_a"][i]))
        dpooled = _mm_grp_grad_act(f"a{i}_dgrp", dypre, w["w_grp_a"][i])
        g_grp = _mm_grp_grad_w(f"a{i}_gwg", pooled, dypre, len(POOL_WINDOWS))
        g_grp = g_grp.reshape(N_CHIPS, -1, g_grp.shape[-1])
        last = i == 0
        if last:
            (dpooled,) = emit(f"a{i}", begin(f"a{i}", {"w_grp_a": (i, g_grp), "w_out_a": (i, g_out)}, (dpooled,)))
        dproj = _pool_bwd(f"a{i}_dpool", dpooled, dproj)
        g_in = _mm_grad_w(f"a{i}_gwi", h, dproj, col_shards=N_CHIPS)
        if last:
            (dproj,) = emit(f"a{i}i", begin(f"a{i}i", {"w_in_a": (i, g_in)}, (dproj,)))
        dh = _mm_grad_act(f"a{i}_dh", dproj, w["w_in_a"][i])
        if not last:
            (dh,) = begin(f"a{i}", {"w_in_a": (i, g_in), "w_grp_a": (i, g_grp), "w_out_a": (i, g_out)}, (dh,))
        dx, dxb, small["norm_a"][i] = _rmsnorm_bwd(f"a{i}_dnorm", x_in, _row(w["norm_a"][i]), dh, dx)
        if not last:
            dx, dxb = emit(f"a{i}", (dx, dxb))

    return loss_vec, dx, small


BIG_WEIGHTS = ("w_in_a", "w_grp_a", "w_out_a", "w_k", "w_v", "w_in_b", "w_out_b")


def _pair_add(name, grad, recv, c_idx):
    _, r, cols = grad.shape
    half = r // 2
    rb = _tile(half, 256)
    nrb = half // rb

    def body(c_ref, g_ref, r_ref, o_ref):
        o_ref[...] = (g_ref[...].astype(F32) + r_ref[...].astype(F32)).astype(BF16)

    blk = (None, rb, cols)
    grid_spec = pltpu.PrefetchScalarGridSpec(
        num_scalar_prefetch=1, grid=(N_CHIPS, nrb),
        in_specs=[pl.BlockSpec(blk, lambda s, i, c: (s, c[0] * nrb + i, 0)), pl.BlockSpec(blk, lambda s, i, c: (s, i, 0))],
        out_specs=pl.BlockSpec(blk, lambda s, i, c: (s, i, 0)))
    return pl.pallas_call(body, name=name, grid_spec=grid_spec,
                          out_shape=jax.ShapeDtypeStruct((N_CHIPS, half, cols), BF16),
                          compiler_params=_params(("parallel", "parallel")))(c_idx, grad, recv)


def _final_add(name, part, recv, sc_idx, layer, n_layers, into=None):
    _, half, cols = part.shape
    rb = _tile(half, 256)
    nrb = half // rb
    n_peer = recv.shape[0]

    def body(sc_ref, p_ref, *refs):
        acc = p_ref[...].astype(F32)
        for r_ref in refs[:n_peer]:
            acc = acc + r_ref[...].astype(F32)
        refs[-1][...] = acc

    blk = (None, rb, cols)
    peer_spec = lambda k: pl.BlockSpec(blk, lambda i, sc: (k, i, 0))
    grid_spec = pltpu.PrefetchScalarGridSpec(
        num_scalar_prefetch=1, grid=(nrb,),
        in_specs=[pl.BlockSpec(blk, lambda i, sc: (sc[0], i, 0))] + [peer_spec(k) for k in range(n_peer)]
                 + ([] if into is None else [ANY]),
        out_specs=pl.BlockSpec(blk, lambda i, sc: (layer, sc[1] * nrb + i, 0)))
    extra = () if into is None else (into,)
    return pl.pallas_call(body, name=name, grid_spec=grid_spec,
                          out_shape=jax.ShapeDtypeStruct((n_layers, 2 * half, cols), F32),
                          input_output_aliases={} if into is None else {2 + n_peer: 0},
                          compiler_params=_params(("parallel",)))(sc_idx, part, *([recv] * n_peer), *extra)


def _cast_into_slot(name, arr, layer, s_idx, after=None):
    _, b, r, cols = arr.shape
    rb = _tile(r, 512)

    def body(s_ref, a_ref, *rest):
        rest[-1][...] = a_ref[...].astype(BF16)

    blk = (None, None, rb, cols)
    grid_spec = pltpu.PrefetchScalarGridSpec(
        num_scalar_prefetch=1, grid=(b, r // rb),
        in_specs=[pl.BlockSpec(blk, lambda j, i, s: (layer, j, i, 0))] + ([] if after is None else [ANY]),
        out_specs=pl.BlockSpec(blk, lambda j, i, s: (j, s[0], i, 0)))
    return pl.pallas_call(body, name=name, grid_spec=grid_spec,
                          out_shape=jax.ShapeDtypeStruct((b, N_CHIPS, r, cols), BF16),
                          compiler_params=_params(("parallel", "parallel")))(
                              s_idx, arr, *(() if after is None else (after,)))


def _sum_devices(name, gathered, own, me_idx):
    n_dev, p, d = gathered.shape

    def body(me_ref, g_ref, own_ref, o_ref):
        acc = None
        for j in range(n_dev):
            term = jnp.where(me_ref[0] == j, own_ref[...], g_ref[j])
            acc = term if acc is None else acc + term
        o_ref[...] = acc

    grid_spec = pltpu.PrefetchScalarGridSpec(
        num_scalar_prefetch=1, grid=(1,),
        in_specs=[pl.BlockSpec((n_dev, p, d), lambda i, me: (0, 0, 0)), pl.BlockSpec((p, d), lambda i, me: (0, 0))],
        out_specs=pl.BlockSpec((p, d), lambda i, me: (0, 0)))
    return pl.pallas_call(body, name=name, grid_spec=grid_spec, out_shape=jax.ShapeDtypeStruct((p, d), F32),
                          compiler_params=_params(("arbitrary",)))(me_idx, gathered, own)


def _adamw_block(w_ref, g_ref, m_ref, v_ref, d_ref, mo_ref, vo_ref):
    grad = g_ref[...]
    m_new = ADAM_B1 * m_ref[...] + (1.0 - ADAM_B1) * grad
    v_new = ADAM_B2 * v_ref[...] + (1.0 - ADAM_B2) * (grad * grad)
    m_hat = m_new / (1.0 - ADAM_B1 ** ADAM_STEP)
    v_hat = v_new / (1.0 - ADAM_B2 ** ADAM_STEP)
    d_ref[...] = -ADAM_LR * (m_hat / (jnp.sqrt(v_hat) + ADAM_EPS) + ADAM_WD * w_ref[...])
    mo_ref[...] = m_new
    vo_ref[...] = v_new


def _adamw(name, w, g, m, v):
    shape = w.shape
    cols = shape[-1]
    flat = lambda a: a.reshape(-1, cols)
    rows = flat(w).shape[0]
    bs = _tile(rows, 256)

    def body(*refs):
        _adamw_block(*refs)

    spec = _row_spec(bs, cols)
    outs = _rows_call(body, name, rows, [spec] * 4, [spec] * 3, [jax.ShapeDtypeStruct((rows, cols), F32)] * 3, bs)(
        flat(w), flat(g), flat(m), flat(v))
    return tuple(o.reshape(shape) for o in outs)


def _adamw_half(name, w, g, m, v, half_idx, into=None):
    n_l, r, cols = w.shape
    rb = _tile(r // 2, 256)
    nrb = r // 2 // rb

    def body(h_ref, w_ref, g_ref, m_ref, v_ref, *rest):
        _adamw_block(w_ref, g_ref, m_ref, v_ref, *rest[-3:])

    spec = pl.BlockSpec((None, rb, cols), lambda j, i, h: (j, h[0] * nrb + i, 0))
    extra = () if into is None else tuple(into)
    grid_spec = pltpu.PrefetchScalarGridSpec(
        num_scalar_prefetch=1, grid=(n_l, nrb), in_specs=[spec] * 4 + [ANY] * len(extra), out_specs=[spec] * 3)
    return tuple(pl.pallas_call(
        body, name=name, grid_spec=grid_spec, out_shape=[jax.ShapeDtypeStruct(w.shape, F32)] * 3,
        input_output_aliases={5 + k: k for k in range(len(extra))},
        compiler_params=_params(("parallel", "parallel")))(half_idx, w, g, m, v, *extra))


def _place():
    x, y, c = lax.axis_index("x"), lax.axis_index("y"), lax.axis_index("c")
    chips = [(1 - x, y), (x, 1 - y), (1 - x, 1 - y)]
    return x, y, c, chips


def _chip_index(chip):
    return 2 * chip[0] + chip[1]


def _comm_call(body, name, n_in, out_shape, scratch, aliases=None):
    return pl.pallas_call(body, name=name, in_specs=[ANY] * n_in, out_specs=[ANY] * len(out_shape), out_shape=out_shape,
                          scratch_shapes=scratch, input_output_aliases=aliases or {})


HBM_SPEC = pl.BlockSpec(memory_space=pltpu.HBM)
SEM_SPEC = pl.BlockSpec(memory_space=pltpu.SEMAPHORE)
SPLIT_PARAMS = pltpu.CompilerParams(has_side_effects=pltpu.SideEffectType.DATAFLOW_SIDE_EFFECTING)


def _in_hbm(arr):
    return pltpu.with_memory_space_constraint(arr, pltpu.HBM)


def _slot_half(ref, chip, core):
    half = ref.shape[2] // 2
    return ref.at[:, _chip_index(chip), pl.ds(core * half, half), :]


def _gather_start(name, bufs, carry=()):
    n, n_c = len(bufs), len(carry)

    def body(*refs):
        ins, (send_sems, recv_sems) = refs[:n], refs[n + n_c:n + n_c + 2]
        x, y, c, chips = _place()
        for a in range(n):
            block = _slot_half(ins[a], (x, y), c)
            for k, chip in enumerate(chips):
                pltpu.make_async_remote_copy(src_ref=block, dst_ref=block, send_sem=send_sems.at[3 * a + k],
                                             recv_sem=recv_sems.at[3 * a + k], device_id=(*chip, c),
                                             device_id_type=MESH).start()

    dma = pltpu.SemaphoreType.DMA
    thru = list(bufs) + list(carry)
    res = pl.pallas_call(
        body, name=name, in_specs=[HBM_SPEC] * (n + n_c), out_specs=[SEM_SPEC] * 2 + [HBM_SPEC] * (n + n_c),
        out_shape=[dma((3 * n,)), dma((3 * n,))] + [pltpu.HBM(a.shape, a.dtype) for a in thru],
        input_output_aliases={t: 2 + t for t in range(n + n_c)}, compiler_params=SPLIT_PARAMS,
    )(*[_in_hbm(a) for a in thru])
    return (res[0], res[1]), list(res[2:2 + n]), list(res[2 + n:])


def _gather_wait(name, sems, bufs, after):
    n = len(bufs)

    def body(*refs):
        ins, (send_sems, recv_sems) = refs[:n], refs[n:n + 2]
        x, y, c, chips = _place()
        for a in range(n):
            for k, chip in enumerate(chips):
                mine, theirs = _slot_half(ins[a], (x, y), c), _slot_half(ins[a], chip, c)
                copy = pltpu.make_async_remote_copy(src_ref=mine, dst_ref=theirs, send_sem=send_sems.at[3 * a + k],
                                                    recv_sem=recv_sems.at[3 * a + k], device_id=(*chip, c),
                                                    device_id_type=MESH)
                copy.wait_send()
                copy.wait_recv()

    res = pl.pallas_call(
        body, name=name, in_specs=[HBM_SPEC] * n + [SEM_SPEC, SEM_SPEC, ANY], out_specs=[HBM_SPEC] * n,
        out_shape=[pltpu.HBM(b.shape, b.dtype) for b in bufs], input_output_aliases={a: a for a in range(n)},
        compiler_params=SPLIT_PARAMS)(*bufs, *sems, after)
    return list(res)


def _gather_forward(name, bufs, smalls=()):
    n, n_small = len(bufs), len(smalls)

    def body(*refs):
        small_in = refs[n:n + n_small]
        outs = refs[n + n_small:2 * n + n_small]
        small_out = refs[2 * n + n_small:2 * n + 2 * n_small]
        send_sems, recv_sems, s_send, s_recv, s_local = refs[-5:]
        x, y, c, chips = _place()
        me, sibling = _chip_index((x, y)), (x, y, 1 - c)

        def forward(t, k, core):
            block = _slot_half(outs[t], chips[k], core)
            return pltpu.make_async_remote_copy(src_ref=block, dst_ref=block, send_sem=send_sems.at[t, k],
                                                recv_sem=recv_sems.at[t, k], device_id=sibling, device_id_type=MESH)

        def small_copy(j, k, slot):
            return pltpu.make_async_remote_copy(src_ref=small_in[j], dst_ref=small_out[j].at[slot],
                                                send_sem=s_send.at[j, k], recv_sem=s_recv.at[j, k],
                                                device_id=(*chips[k], c), device_id_type=MESH)

        local = []
        for t in range(n):
            for k in range(3):
                forward(t, k, c).start()
        for j in range(n_small):
            own = pltpu.make_async_copy(small_in[j], small_out[j].at[me], s_local.at[j])
            own.start()
            local.append(own)
            for k in range(3):
                small_copy(j, k, me).start()
        for t in range(n):
            for k in range(3):
                forward(t, k, 1 - c).wait_recv()
        for j in range(n_small):
            for k in range(3):
                small_copy(j, k, _chip_index(chips[k])).wait_recv()
        for t in range(n):
            for k in range(3):
                forward(t, k, c).wait_send()
        for j in range(n_small):
            for k in range(3):
                small_copy(j, k, me).wait_send()
        for own in local:
            own.wait()

    out_shape = [jax.ShapeDtypeStruct(b.shape, BF16) for b in bufs]
    out_shape += [jax.ShapeDtypeStruct((N_CHIPS,) + s.shape, F32) for s in smalls]
    dma = pltpu.SemaphoreType.DMA
    n_s = max(n_small, 1)
    res = _comm_call(body, name, n + n_small, out_shape,
                     [dma((n, 3)), dma((n, 3)), dma((n_s, 3)), dma((n_s, 3)), dma((n_s,))],
                     aliases={t: t for t in range(n)})(*bufs, *smalls)
    return list(res[:n]), list(res[n:])


def _halves_copy(grad_ref, land_ref, send_sems, recv_sems, t):
    x, y, c, _ = _place()
    half = grad_ref.shape[1] // 2
    return pltpu.make_async_remote_copy(
        src_ref=grad_ref.at[:, pl.ds((1 - c) * half, half), :], dst_ref=land_ref, send_sem=send_sems.at[t],
        recv_sem=recv_sems.at[t], device_id=(x, y, 1 - c), device_id_type=MESH)


def _exchange_start(name, grads, carry=()):
    n = len(grads)
    lands = [lax.empty((g.shape[0], g.shape[1] // 2, g.shape[2]), BF16) for g in grads]

    def body(*refs):
        send_sems, recv_sems = refs[2 * n + len(carry):2 * n + len(carry) + 2]
        for t in range(n):
            _halves_copy(refs[t], refs[n + t], send_sems, recv_sems, t).start()

    dma = pltpu.SemaphoreType.DMA
    thru = list(grads) + lands + list(carry)
    res = pl.pallas_call(
        body, name=name, in_specs=[HBM_SPEC] * len(thru), out_specs=[SEM_SPEC] * 2 + [HBM_SPEC] * len(thru),
        out_shape=[dma((n,)), dma((n,))] + [pltpu.HBM(a.shape, a.dtype) for a in thru],
        input_output_aliases={t: 2 + t for t in range(len(thru))}, compiler_params=SPLIT_PARAMS,
    )(*[_in_hbm(a) for a in thru])
    return (res[0], res[1]), list(res[2:2 + n]), list(res[2 + n:2 + 2 * n]), list(res[2 + 2 * n:])


def _exchange_wait(name, sems, grads, lands, after):
    n = len(grads)

    def body(*refs):
        send_sems, recv_sems = refs[2 * n:2 * n + 2]
        for t in range(n):
            copy = _halves_copy(refs[t], refs[n + t], send_sems, recv_sems, t)
            copy.wait_send()
            copy.wait_recv()

    res = pl.pallas_call(
        body, name=name, in_specs=[HBM_SPEC] * (2 * n) + [SEM_SPEC, SEM_SPEC, ANY], out_specs=[HBM_SPEC] * (2 * n),
        out_shape=[pltpu.HBM(a.shape, a.dtype) for a in grads + lands],
        input_output_aliases={t: t for t in range(2 * n)}, compiler_params=SPLIT_PARAMS)(*grads, *lands, *sems, after)
    return list(res[:n]), list(res[n:])


def _scatter_copy(part_ref, land_ref, send_sems, recv_sems, t, k, chip, c):
    return pltpu.make_async_remote_copy(
        src_ref=part_ref.at[_chip_index(chip)], dst_ref=land_ref.at[k], send_sem=send_sems.at[3 * t + k],
        recv_sem=recv_sems.at[3 * t + k], device_id=(*chip, c), device_id_type=MESH)


def _scatter_start(name, parts, carry=()):
    n, n_c = len(parts), len(carry)
    lands = [lax.empty((3,) + p.shape[1:], BF16) for p in parts]

    def body(*refs):
        p_in, l_in = refs[:n], refs[n:2 * n]
        send_sems, recv_sems = refs[2 * n + n_c:2 * n + n_c + 2]
        x, y, c, chips = _place()
        for t in range(n):
            for k, chip in enumerate(chips):
                _scatter_copy(p_in[t], l_in[t], send_sems, recv_sems, t, k, chip, c).start()

    dma = pltpu.SemaphoreType.DMA
    thru = list(parts) + lands + list(carry)
    res = pl.pallas_call(
        body, name=name, in_specs=[HBM_SPEC] * len(thru), out_specs=[SEM_SPEC] * 2 + [HBM_SPEC] * len(thru),
        out_shape=[dma((3 * n,)), dma((3 * n,))] + [pltpu.HBM(a.shape, a.dtype) for a in thru],
        input_output_aliases={t: 2 + t for t in range(len(thru))}, compiler_params=SPLIT_PARAMS,
    )(*[_in_hbm(a) for a in thru])
    return (res[0], res[1]), list(res[2:2 + n]), list(res[2 + n:2 + 2 * n]), list(res[2 + 2 * n:])


def _scatter_wait(name, sems, parts, lands, after):
    n = len(parts)

    def body(*refs):
        p_in, l_in = refs[:n], refs[n:2 * n]
        send_sems, recv_sems = refs[2 * n:2 * n + 2]
        x, y, c, chips = _place()
        for t in range(n):
            for k, chip in enumerate(chips):
                copy = _scatter_copy(p_in[t], l_in[t], send_sems, recv_sems, t, k, chip, c)
                copy.wait_send()
                copy.wait_recv()

    hbm_out = lambda a: pltpu.HBM(a.shape, a.dtype)
    res = pl.pallas_call(
        body, name=name, in_specs=[HBM_SPEC] * (2 * n) + [SEM_SPEC, SEM_SPEC, ANY], out_specs=[HBM_SPEC] * (2 * n),
        out_shape=[hbm_out(a) for a in parts + lands], input_output_aliases={t: t for t in range(2 * n)},
        compiler_params=SPLIT_PARAMS)(*parts, *lands, *sems, after)
    return list(res[:n]), list(res[n:])


def _share_items(fulls):
    return [(a, l) for a in range(len(fulls)) for l in range(fulls[a].shape[0])]


def _share_copy(ref, layer, send_sems, recv_sems, t):
    x, y, c, _ = _place()
    half = ref.shape[1] // 2
    return pltpu.make_async_remote_copy(
        src_ref=ref.at[layer, pl.ds(c * half, half), :], dst_ref=ref.at[layer, pl.ds((1 - c) * half, half), :],
        send_sem=send_sems.at[t], recv_sem=recv_sems.at[t], device_id=(x, y, 1 - c), device_id_type=MESH)


def _share_start(fulls):
    n, items = len(fulls), _share_items(fulls)

    def body(*refs):
        send_sems, recv_sems = refs[n:n + 2]
        x, y, c, _ = _place()
        for t, (a, l) in enumerate(items):
            half = refs[a].shape[1] // 2
            mine = refs[a].at[l, pl.ds(c * half, half), :]
            pltpu.make_async_remote_copy(src_ref=mine, dst_ref=mine, send_sem=send_sems.at[t], recv_sem=recv_sems.at[t],
                                         device_id=(x, y, 1 - c), device_id_type=MESH).start()

    dma = pltpu.SemaphoreType.DMA
    res = pl.pallas_call(
        body, name="grad_share_start", in_specs=[HBM_SPEC] * n, out_specs=[SEM_SPEC] * 2 + [HBM_SPEC] * n,
        out_shape=[dma((len(items),)), dma((len(items),))] + [pltpu.HBM(f.shape, f.dtype) for f in fulls],
        input_output_aliases={t: 2 + t for t in range(n)}, compiler_params=SPLIT_PARAMS,
    )(*[_in_hbm(f) for f in fulls])
    return (res[0], res[1]), list(res[2:])


def _share_wait(sems, fulls, after):
    n, items = len(fulls), _share_items(fulls)

    def body(*refs):
        send_sems, recv_sems = refs[n:n + 2]
        for t, (a, l) in enumerate(items):
            copy = _share_copy(refs[a], l, send_sems, recv_sems, t)
            copy.wait_send()
            copy.wait_recv()

    res = pl.pallas_call(
        body, name="grad_share_wait", in_specs=[HBM_SPEC] * n + [SEM_SPEC, SEM_SPEC, ANY], out_specs=[HBM_SPEC] * n,
        out_shape=[pltpu.HBM(f.shape, f.dtype) for f in fulls], input_output_aliases={t: t for t in range(n)},
        compiler_params=SPLIT_PARAMS)(*fulls, *sems, after)
    return list(res)


N_DEVICES = 8


def _device_index(x, y, c):
    return 4 * x + 2 * y + c


def _small_peers():
    x, y, c, _ = _place()
    flips = [(fx, fy, fc) for fx in (0, 1) for fy in (0, 1) for fc in (0, 1)][1:]
    return _device_index(x, y, c), [(x ^ fx, y ^ fy, c ^ fc) for fx, fy, fc in flips]


def _small_copy(p_ref, land_ref, send_sems, recv_sems, k, peer, slot):
    return pltpu.make_async_remote_copy(src_ref=p_ref, dst_ref=land_ref.at[slot], send_sem=send_sems.at[k],
                                        recv_sem=recv_sems.at[k], device_id=peer, device_id_type=MESH)


def _allgather_small_start(packed, carry=()):
    land = jnp.zeros((N_DEVICES,) + packed.shape, F32)

    def body(*refs):
        p_ref, l_ref = refs[:2]
        send_sems, recv_sems = refs[2 + len(carry):4 + len(carry)]
        me, peers = _small_peers()
        for k, peer in enumerate(peers):
            _small_copy(p_ref, l_ref, send_sems, recv_sems, k, peer, me).start()

    dma = pltpu.SemaphoreType.DMA
    thru = [packed, land] + list(carry)
    res = pl.pallas_call(
        body, name="small_allgather_start", in_specs=[HBM_SPEC] * len(thru),
        out_specs=[SEM_SPEC] * 2 + [HBM_SPEC] * len(thru),
        out_shape=[dma((N_DEVICES - 1,)), dma((N_DEVICES - 1,))] + [pltpu.HBM(a.shape, a.dtype) for a in thru],
        input_output_aliases={t: 2 + t for t in range(len(thru))}, compiler_params=SPLIT_PARAMS,
    )(*[_in_hbm(a) for a in thru])
    return (res[0], res[1]), res[2], res[3], list(res[4:])


def _allgather_small_wait(sems, packed, land, after):
    def body(p_ref, l_ref, send_sems, recv_sems, *_):
        _, peers = _small_peers()
        for k, peer in enumerate(peers):
            copy = _small_copy(p_ref, l_ref, send_sems, recv_sems, k, peer, _device_index(*peer))
            copy.wait_send()
            copy.wait_recv()

    res = pl.pallas_call(
        body, name="small_allgather_wait", in_specs=[HBM_SPEC] * 2 + [SEM_SPEC, SEM_SPEC, ANY],
        out_specs=[HBM_SPEC] * 2, out_shape=[pltpu.HBM(a.shape, a.dtype) for a in (packed, land)],
        input_output_aliases={0: 0, 1: 1}, compiler_params=SPLIT_PARAMS)(packed, land, *sems, after)
    return res[0], res[1]


PAD_ROWS = 8


def kernel(x, norm_a, w_in_a, w_grp_a, scale_a, w_out_a, norm_kv, w_k, w_v, norm_b, w_in_b, w_out_b, norm_f, loss_target, m_norm_a, m_w_in_a, m_w_grp_a, m_scale_a, m_w_out_a, m_norm_kv, m_w_k, m_w_v, m_norm_b, m_w_in_b, m_w_out_b, m_norm_f, v_norm_a, v_w_in_a, v_w_grp_a, v_scale_a, v_w_out_a, v_norm_kv, v_w_k, v_w_v, v_norm_b, v_w_in_b, v_w_out_b, v_norm_f):
    weights = dict(norm_a=norm_a, w_in_a=w_in_a, w_grp_a=w_grp_a, scale_a=scale_a, w_out_a=w_out_a, norm_kv=norm_kv,
                   w_k=w_k, w_v=w_v, norm_b=norm_b, w_in_b=w_in_b, w_out_b=w_out_b, norm_f=norm_f)
    moments_m = dict(norm_a=m_norm_a, w_in_a=m_w_in_a, w_grp_a=m_w_grp_a, scale_a=m_scale_a, w_out_a=m_w_out_a,
                     norm_kv=m_norm_kv, w_k=m_w_k, w_v=m_w_v, norm_b=m_norm_b, w_in_b=m_w_in_b, w_out_b=m_w_out_b,
                     norm_f=m_norm_f)
    moments_v = dict(norm_a=v_norm_a, w_in_a=v_w_in_a, w_grp_a=v_w_grp_a, scale_a=v_scale_a, w_out_a=v_w_out_a,
                     norm_kv=v_norm_kv, w_k=v_w_k, w_v=v_w_v, norm_b=v_norm_b, w_in_b=v_w_in_b, w_out_b=v_w_out_b,
                     norm_f=v_norm_f)
    names = list(weights)
    d = x.shape[-1]
    c_idx = lax.axis_index("c").astype(jnp.int32).reshape(1)
    s_me = 2 * lax.axis_index("x") + lax.axis_index("y")
    s_idx = s_me.astype(jnp.int32).reshape(1)

    def as_lbrc(name):
        a = weights[name]
        if name == "w_grp_a":
            return a
        if a.ndim == 2:
            return a.reshape(1, 1, *a.shape)
        return a.reshape(a.shape[0], 1, *a.shape[1:])

    n_a, n_b = norm_a.shape[0], norm_b.shape[0]
    group_weights = {**{f"a{i}": [("w_in_a", i), ("w_grp_a", i), ("w_out_a", i)] for i in range(n_a)},
                     "kv": [("w_k", 0), ("w_v", 0)],
                     **{f"b{i}": [("w_in_b", i), ("w_out_b", i)] for i in range(n_b)}}
    group_order = [f"a{i}" for i in range(n_a)] + ["kv"] + [f"b{i}" for i in range(n_b)]
    slots, slot_groups = [], []
    small_full, started = {}, {}

    def start_group(gi, carry=()):
        sems, bufs, carry = _gather_start(f"gather_start_{group_order[gi]}", [slots[t] for t in slot_groups[gi]], carry)
        started[gi] = (sems, bufs)
        return carry

    previous = None
    for gi, group in enumerate(group_order):
        slot_groups.append(list(range(len(slots), len(slots) + len(group_weights[group]))))
        for name, l in group_weights[group]:
            slots.append(_cast_into_slot(f"cast_{name}{l}", as_lbrc(name), l, s_idx, previous))
            previous = slots[-1] if gi > 0 else None
        if gi == 0:
            (previous,) = start_group(0, (norm_kv.reshape(1, -1),))

    def gathered_form(name, g):
        if name in ("w_in_a", "w_in_b"):
            return g[0]
        if name == "w_grp_a":
            return g.reshape(g.shape[0], -1, g.shape[-1])
        return g.reshape(-1, g.shape[-1])

    def fetch(group, after):
        if group == "head":
            return {"norm_f": (0, norm_f)}
        gi = group_order.index(group)
        sems, bufs = started[gi]
        if gi == 0:
            after = slots[-1]
        bufs = _gather_wait(f"gather_wait_{group}", sems, bufs, after)
        bufs, small_g = _gather_forward(f"gather_forward_{group}", bufs, [norm_a, scale_a] if gi == 0 else [])
        out = {name: (l, gathered_form(name, g)) for (name, l), g in zip(group_weights[group], bufs)}
        if gi == 0:
            for name, g in zip(("norm_a", "scale_a"), small_g):
                small_full[name] = g.transpose(1, 0, 2).reshape(g.shape[1], -1)
        layer = group_weights[group][0][1]
        if group.startswith("a"):
            gain_name, gain = "norm_a", small_full["norm_a"][layer]
            out.update(scale_a=(layer, small_full["scale_a"][layer]))
        elif group == "kv":
            gain_name, gain = "norm_kv", norm_kv
        else:
            gain_name, gain = "norm_b", norm_b[layer]
        gain = gain.reshape(1, -1)
        ahead = [gi + 1] + ([gi + 2] if gi + 2 < len(group_order) and group_order[gi + 1] == "kv" else [])
        for gj in ahead:
            if gj < len(group_order) and gj not in started:
                (gain,) = start_group(gj, (gain,))
        out[gain_name] = (layer, gain)
        return out

    exchanging, in_flight = {}, []

    def begin(group, grads_of, carry):
        keys = [(k, grads_of[k][0]) for k in grads_of]
        sems, grads, lands, carry = _exchange_start(f"grad_exchange_start_{group}", [grads_of[k][1] for k in grads_of],
                                                    tuple(carry))
        exchanging[group] = (keys, sems, grads, lands)
        return carry

    def emit(group, carry):
        keys, sems, grads, lands = exchanging.pop(group)
        grads, recv1 = _exchange_wait(f"grad_exchange_wait_{group}", sems, grads, lands, after=carry[0])
        parts = [_pair_add(f"pair_add_{k}{l}", g, r, c_idx) for (k, l), g, r in zip(keys, grads, recv1)]
        sems, parts, lands, carry = _scatter_start(f"grad_scatter_start_{group}", parts, tuple(carry))
        in_flight.append((group, keys, sems, parts, lands))
        return carry

    loss_vec, grad_x, small = _local_step(x[0], loss_target[0], n_a, n_b, fetch, begin, emit)

    small_order = [("norm_a", i) for i in range(n_a)] + [("scale_a", i) for i in range(n_a)] + [("norm_kv", 0)] + \
                  [("norm_b", i) for i in range(norm_b.shape[0])] + [("norm_f", 0)]
    pad = lambda vec: jnp.pad(vec, ((0, PAD_ROWS - 1), (0, 0)))
    packed = jnp.concatenate([pad(loss_vec)] + [pad(small[n][i]) for n, i in small_order], axis=0)
    small_sems, packed, small_land, _ = _allgather_small_start(packed)

    sc_idx = jnp.concatenate([s_idx, c_idx])
    fulls = {name: None for name in BIG_WEIGHTS}
    for group, keys, sems, parts, lands in in_flight:
        parts, lands = _scatter_wait(f"grad_scatter_wait_{group}", sems, parts, lands, after=packed)
        for (name, i), p, r in zip(keys, parts, lands):
            n_layers = 1 if weights[name].ndim == 2 else weights[name].shape[0]
            fulls[name] = _final_add(f"final_add_{name}{i}", p, r, sc_idx, i, n_layers, into=fulls[name])
    share_sems, sharing = _share_start([fulls[name] for name in BIG_WEIGHTS])

    deltas, new_m, new_v, grads = {}, {}, {}, {}
    as_full = lambda n, full: (a.reshape(full.shape) for a in (weights[n], moments_m[n], moments_v[n]))
    mine = {}
    for n, full in zip(BIG_WEIGHTS, sharing):
        w3, m3, v3 = as_full(n, full)
        mine[n] = _adamw_half(f"adamw_mine_{n}", w3, full, m3, v3, c_idx)
    shared = _share_wait(share_sems, sharing, after=mine[BIG_WEIGHTS[-1]][2])
    for n, full in zip(BIG_WEIGHTS, shared):
        w3, m3, v3 = as_full(n, full)
        shape = weights[n].shape
        dl, mn, vn = _adamw_half(f"adamw_theirs_{n}", w3, full, m3, v3, 1 - c_idx, into=mine[n])
        deltas[n], new_m[n], new_v[n], grads[n] = dl.reshape(shape), mn.reshape(shape), vn.reshape(shape), full.reshape(shape)

    def update(n):
        shape = weights[n].shape
        as2d = (lambda a: a.reshape(1, -1)) if len(shape) == 1 else (lambda a: a)
        dl, mn, vn = _adamw(f"adamw_{n}", as2d(weights[n]), as2d(grads[n]), as2d(moments_m[n]), as2d(moments_v[n]))
        deltas[n], new_m[n], new_v[n] = dl.reshape(shape), mn.reshape(shape), vn.reshape(shape)

    packed, small_land = _allgather_small_wait(small_sems, packed, small_land, after=new_v[BIG_WEIGHTS[-1]])
    me_idx = _device_index(lax.axis_index("x"), lax.axis_index("y"), lax.axis_index("c")).astype(jnp.int32).reshape(1)
    totals = _sum_devices("small_sum", small_land, packed, me_idx)
    loss = 0.5 * jnp.sum(totals[0]) / d
    small_tot = {}
    for j, (n, i) in enumerate(small_order):
        small_tot.setdefault(n, []).append(totals[PAD_ROWS * (j + 1)])
    shard_w = norm_a.shape[1]
    for n in ("norm_a", "scale_a"):
        full = jnp.stack(small_tot[n])
        grads[n] = lax.dynamic_slice_in_dim(full, s_me * shard_w, shard_w, axis=1)
    grads["norm_kv"] = small_tot["norm_kv"][0]
    grads["norm_b"] = jnp.stack(small_tot["norm_b"])
    grads["norm_f"] = small_tot["norm_f"][0]
    for n in names:
        if n not in BIG_WEIGHTS:
            update(n)

    return (loss, grad_x[None], *[grads[n] for n in names], *[deltas[n] for n in names],
            *[new_m[n] for n in names], *[new_v[n] for n in names])
```

```python
import functools
import math

import jax
import jax.numpy as jnp
from jax import lax
from jax.experimental import pallas as pl
from jax.experimental.pallas import tpu as pltpu

F32 = jnp.float32
BF16 = jnp.bfloat16

HEAD_DIM = 128
POOL_WINDOWS = (2, 4, 8, 16)
DILATED_PAIRS = ((128, 1), (512, 4), (2048, 16))
ROPE_THETA = 10000.0
RMS_EPS = 1e-6
NEG_INF = -1e30
N_CHIPS = 4

ADAM_LR = 0.001
ADAM_B1 = 0.9
ADAM_B2 = 0.999
ADAM_EPS = 1e-08
ADAM_WD = 0.01
ADAM_STEP = 10

VMEM_LIMIT_BYTES = 56 * 1024 * 1024
MESH = pl.DeviceIdType.MESH
ANY = pl.BlockSpec(memory_space=pl.ANY)


def _tile(n, pref):
    t = min(n, pref)
    assert n % t == 0, (n, pref)
    return t


def _params(sem=None):
    return pltpu.CompilerParams(dimension_semantics=sem, vmem_limit_bytes=VMEM_LIMIT_BYTES)


def _mm(name, a, b, *, grid2, nk, a_blk, a_map, b_blk, b_map, outs, dims, epi=None, epi_in=(), epi_specs=(),
        acc_shape=None, epi_scratch=(), into=None):
    n_epi, n_out = len(epi_in), len(outs)

    def body(*refs):
        a_ref, b_ref = refs[0], refs[1]
        e_refs = refs[2:2 + n_epi]
        first_out = 2 + n_epi + (0 if into is None else 1)
        o_refs = refs[first_out:first_out + n_out]
        s_refs = refs[first_out + n_out + (0 if nk == 1 else 1):]

        def contrib():
            a_val = a_ref[...]
            if a_val.ndim == 3:
                a_val = a_val.reshape(-1, a_val.shape[-1])
            return lax.dot_general(a_val, b_ref[...], (dims, ((), ())), preferred_element_type=F32)

        def finish(acc):
            if epi is None:
                o_refs[0][...] = acc.reshape(o_refs[0].shape).astype(o_refs[0].dtype)
            else:
                epi(acc, e_refs, o_refs, s_refs)

        if nk == 1:
            finish(contrib())
        else:
            acc_ref = refs[first_out + n_out]
            k = pl.program_id(2)

            @pl.when(k == 0)
            def _():
                acc_ref[...] = contrib()

            @pl.when(k > 0)
            def _():
                acc_ref[...] += contrib()

            @pl.when(k == nk - 1)
            def _():
                finish(acc_ref[...])

    scratch = ([] if nk == 1 else [pltpu.VMEM(acc_shape, F32)]) + list(epi_scratch)
    extra_in, extra_specs, aliases = (), (), {}
    if into is not None:
        extra_in, extra_specs, aliases = (into[0],), (ANY,), {2 + n_epi: into[1]}
    res = pl.pallas_call(
        body, name=name, grid=(grid2[0], grid2[1], nk),
        in_specs=[pl.BlockSpec(a_blk, a_map), pl.BlockSpec(b_blk, b_map), *epi_specs, *extra_specs],
        out_specs=[pl.BlockSpec(blk, imap) for _, blk, imap, _ in outs],
        out_shape=[jax.ShapeDtypeStruct(shape, dtype) for shape, _, _, dtype in outs],
        scratch_shapes=scratch, input_output_aliases=aliases,
        compiler_params=_params(("parallel", "parallel", "arbitrary")),
    )(a, b, *epi_in, *extra_in)
    return res[0] if n_out == 1 else tuple(res)


NN = ((1,), (0,))
NT = ((1,), (1,))
TN = ((0,), (0,))


def _rope_apply(t, cos, sin):
    return t * cos + pltpu.roll(t, HEAD_DIM // 2, 1) * sin


def _epi_add(acc, e_refs, o_refs, s_refs):
    o_refs[0][...] = (acc + e_refs[0][...]).astype(o_refs[0].dtype)


def _col_blocks(width):
    return [slice(c * HEAD_DIM, (c + 1) * HEAD_DIM) for c in range(width // HEAD_DIM)]


def _col_scratch(rows, width):
    return pltpu.VMEM((width // HEAD_DIM, rows, HEAD_DIM), F32)


def _to_residue_major(o_ref, scr, d, sl):
    if d == 1:
        o_ref[0, :, sl] = scr[...].astype(o_ref.dtype)
        return
    rows = scr.shape[0] // d
    for r in range(d):
        o_ref[r, :, sl] = scr[pl.ds(r, rows, stride=d), :].astype(o_ref.dtype)


def _from_residue_major(i_ref, scr, d, sl):
    if d == 1:
        return i_ref[0, :, sl].astype(F32)
    rows = i_ref.shape[1]
    for r in range(d):
        scr[pl.ds(r, rows, stride=d), :] = i_ref[r, :, sl].astype(F32)
    return scr[...]


def _make_epi_orders(dils, rope_scale):
    def epi(acc, e_refs, o_refs, s_refs):
        if rope_scale is not None:
            cos = e_refs[0][...]
            sin = e_refs[1][...]
        for c, sl in enumerate(_col_blocks(acc.shape[1])):
            scr = s_refs[0].at[c]
            scr[...] = acc[:, sl] if rope_scale is None else _rope_apply(acc[:, sl], cos, sin) * rope_scale
            for o_ref, d in zip(o_refs, dils):
                _to_residue_major(o_ref, scr, d, sl)
    return epi


def _make_epi_token_order(d, has_add):
    def epi(acc, e_refs, o_refs, s_refs):
        o_ref = o_refs[0]
        if d == 1:
            o_ref[...] = acc + e_refs[0][...] if has_add else acc
            return
        rows = acc.shape[0] // d
        for c, sl in enumerate(_col_blocks(acc.shape[1])):
            scr = s_refs[0].at[c]
            for r in range(d):
                scr[pl.ds(r, rows, stride=d), :] = acc[r * rows:(r + 1) * rows, sl]
            o_ref[:, sl] = scr[...] + e_refs[0][:, sl] if has_add else scr[...]
    return epi


def _mm_act_w(name, a, w, *, out_dtype=BF16, add=None, rope=None, n_first=0, n_cols=None, dils=None):
    s_len, k_len = a.shape
    bm = _tile(s_len, 1024)
    epi, epi_in, epi_specs, epi_scratch = None, (), (), ()
    if w.ndim == 3:
        ns, _, c = w.shape
        ns_used = ns if n_cols is None else n_cols
        bn = _tile(c, 1024)
        sub = c // bn
        grid2 = (ns_used * sub, s_len // bm)
        b_blk, b_map = (None, k_len, bn), (lambda j, i, k: (j // sub + n_first, 0, j % sub))
        n_len = ns_used * c
    else:
        n_len = w.shape[1]
        bn = _tile(n_len, 1024)
        grid2 = (n_len // bn, s_len // bm)
        b_blk, b_map = (k_len, bn), (lambda j, i, k: (0, j))
    if add is not None:
        epi, epi_in = _epi_add, (add,)
        epi_specs = (pl.BlockSpec((bm, bn), lambda j, i, k: (i, j)),)
    outs = [((s_len, n_len), (bm, bn), lambda j, i, k: (i, j), out_dtype)]
    if dils is not None:
        if rope is not None:
            epi_in = rope[:2]
            epi_specs = (pl.BlockSpec((bm, HEAD_DIM), lambda j, i, k: (i, 0)),) * 2
        epi = _make_epi_orders(dils, None if rope is None else rope[2])
        epi_scratch = (_col_scratch(bm, bn),)
        outs = [((d, s_len // d, n_len), (d, bm // d, bn), lambda j, i, k: (0, i, j), BF16) for d in dils]
    res = _mm(name, a, w, grid2=grid2, nk=1, a_blk=(bm, k_len), a_map=lambda j, i, k: (i, 0),
              b_blk=b_blk, b_map=b_map, outs=outs, dims=NN, epi=epi, epi_in=epi_in, epi_specs=epi_specs,
              epi_scratch=epi_scratch)
    return (res,) if dils is not None and len(dils) == 1 else res


def _mm_grad_act(name, dy, w, *, add=None, slot=None):
    if slot is not None:
        d = 1 if dy.ndim == 2 else dy.shape[0]
        s_len = dy.shape[-2] * d
        _, k_len, c = w.shape
        bm, bn = _tile(s_len, 1024), _tile(k_len, 1024)
        a_blk, a_map = ((bm, c), lambda j, i, k: (i, 0)) if dy.ndim == 2 else ((d, bm // d, c), lambda j, i, k: (0, i, 0))
        epi_in = () if add is None else (add,)
        return _mm(name, dy, w, grid2=(k_len // bn, s_len // bm), nk=1, a_blk=a_blk, a_map=a_map,
                   b_blk=(None, bn, c), b_map=lambda j, i, k: (slot, j, 0),
                   outs=[((s_len, k_len), (bm, bn), lambda j, i, k: (i, j), F32)], dims=NT,
                   epi=_make_epi_token_order(d, add is not None), epi_in=epi_in,
                   epi_specs=(pl.BlockSpec((bm, bn), lambda j, i, k: (i, j)),) * len(epi_in),
                   epi_scratch=(_col_scratch(bm, bn),) if d > 1 else ())
    s_len, n_len = dy.shape
    bm = _tile(s_len, 1024)
    if w.ndim == 3:
        ns, k_len, c = w.shape
        bk, nk = c, ns
        bn = _tile(k_len, 1024)
        b_blk, b_map = (None, bn, c), (lambda j, i, k: (k, j, 0))
    else:
        k_len = w.shape[0]
        bk = _tile(n_len, 2048)
        nk = n_len // bk
        bn = _tile(k_len, 1024)
        b_blk, b_map = (bn, bk), (lambda j, i, k: (j, k))
    epi, epi_in, epi_specs = None, (), ()
    if add is not None:
        epi, epi_in = _epi_add, (add,)
        epi_specs = (pl.BlockSpec((bm, bn), lambda j, i, k: (i, j)),)
    return _mm(name, dy, w, grid2=(k_len // bn, s_len // bm), nk=nk, a_blk=(bm, bk), a_map=lambda j, i, k: (i, k),
               b_blk=b_blk, b_map=b_map, outs=[((s_len, k_len), (bm, bn), lambda j, i, k: (i, j), F32)],
               dims=NT, epi=epi, epi_in=epi_in, epi_specs=epi_specs, acc_shape=(bm, bn))


def _mm_grad_w(name, a, dy, *, col_shards=None, slot=None, into=None):
    s_len, k_len = a.shape
    n_len = dy.shape[1]
    bk = _tile(s_len, 2048)
    bm = _tile(k_len, 1024)
    if slot is not None:
        bn = _tile(n_len, 1024)
        out = ((col_shards, k_len, n_len), (None, bm, bn), lambda j, i, k: (slot, i, j), BF16)
    elif col_shards:
        c = n_len // col_shards
        bn = _tile(c, 1024)
        sub = c // bn
        out = ((col_shards, k_len, c), (None, bm, bn), lambda j, i, k: (j // sub, i, j % sub), BF16)
    else:
        bn = _tile(n_len, 1024)
        out = ((k_len, n_len), (bm, bn), lambda j, i, k: (i, j), BF16)
    return _mm(name, a, dy, grid2=(n_len // bn, k_len // bm), nk=s_len // bk,
               a_blk=(bk, bm), a_map=lambda j, i, k: (k, i), b_blk=(bk, bn), b_map=lambda j, i, k: (k, j),
               outs=[out], dims=TN, acc_shape=(bm, bn), into=None if into is None else (into, 0))


def _mm_grp_fwd(name, pooled, wg):
    s_len, e = pooled.shape
    ng, g, _ = wg.shape
    bm = _tile(s_len, 1024)
    return _mm(name, pooled, wg, grid2=(ng, s_len // bm), nk=1, a_blk=(bm, g), a_map=lambda j, i, k: (i, j),
               b_blk=(None, g, g), b_map=lambda j, i, k: (j, 0, 0),
               outs=[((s_len, e), (bm, g), lambda j, i, k: (i, j), F32)], dims=NN)


def _mm_grp_grad_act(name, dy, wg):
    s_len, e = dy.shape
    ng, g, _ = wg.shape
    bm = _tile(s_len, 1024)
    return _mm(name, dy, wg, grid2=(ng, s_len // bm), nk=1, a_blk=(bm, g), a_map=lambda j, i, k: (i, j),
               b_blk=(None, g, g), b_map=lambda j, i, k: (j, 0, 0),
               outs=[((s_len, e), (bm, g), lambda j, i, k: (i, j), F32)], dims=NT)


def _mm_grp_grad_w(name, pooled, dy, ng):
    s_len, e = pooled.shape
    g = e // ng
    bk = _tile(s_len, 1024)
    return _mm(name, pooled, dy, grid2=(ng, 1), nk=s_len // bk, a_blk=(bk, g), a_map=lambda j, i, k: (k, j),
               b_blk=(bk, g), b_map=lambda j, i, k: (k, j),
               outs=[((N_CHIPS, ng, g // N_CHIPS, g), (N_CHIPS, None, g // N_CHIPS, g), lambda j, i, k: (0, j, 0, 0), BF16)],
               dims=TN, acc_shape=(g, g))


def _row_spec(bs, width, col=0):
    return pl.BlockSpec((bs, width), lambda i: (i, col))


def _vec_spec(width):
    return pl.BlockSpec((1, width), lambda i: (0, 0))


def _rows_call(body, name, s_len, in_specs, out_specs, out_shape, bs, aliases=None, sequential=False):
    return pl.pallas_call(
        body, name=name, grid=(s_len // bs,), in_specs=in_specs, out_specs=out_specs, out_shape=out_shape,
        input_output_aliases=aliases or {},
        compiler_params=_params(("arbitrary",) if sequential else ("parallel",)))


def _accumulate(ref, part):
    i = pl.program_id(0)

    @pl.when(i == 0)
    def _():
        ref[...] = part

    @pl.when(i > 0)
    def _():
        ref[...] += part


def _rms_scale(xf):
    return lax.rsqrt(jnp.mean(xf * xf, axis=-1, keepdims=True) + RMS_EPS)


def _res_spec(dil, bs, width):
    return pl.BlockSpec((dil, bs // dil, width), lambda i: (0, i, 0))


def _res_shape(dil, s_len, width, dtype):
    return jax.ShapeDtypeStruct((dil, s_len // dil, width), dtype)


def _rmsnorm_fwd(name, x, gain, dils=()):
    s_len, d = x.shape
    bs = _tile(s_len, 256)

    def body(x_ref, g_ref, h_ref, *rest):
        xf = x_ref[...]
        h = (xf * _rms_scale(xf)) * g_ref[...]
        h_ref[...] = h.astype(BF16)
        if dils:
            for c, sl in enumerate(_col_blocks(d)):
                scr = rest[-1].at[c]
                scr[...] = h[:, sl]
                for o_ref, dil in zip(rest[:-1], dils):
                    _to_residue_major(o_ref, scr, dil, sl)

    res = pl.pallas_call(
        body, name=name, grid=(s_len // bs,), in_specs=[_row_spec(bs, d), _vec_spec(d)],
        out_specs=[_row_spec(bs, d)] + [_res_spec(dil, bs, d) for dil in dils],
        out_shape=[jax.ShapeDtypeStruct((s_len, d), BF16)] + [_res_shape(dil, s_len, d, BF16) for dil in dils],
        scratch_shapes=[_col_scratch(bs, d)] if dils else [],
        compiler_params=_params(("parallel",)))(x, gain)
    return res[0] if not dils else tuple(res)


def _rmsnorm_bwd(name, x, gain, dh, dres):
    s_len, d = x.shape
    bs = _tile(s_len, 256)

    def body(x_ref, g_ref, dh_ref, dres_ref, dx_ref, dxb_ref, dg_ref):
        xf = x_ref[...]
        r = _rms_scale(xf)
        xh = xf * r
        dh_f = dh_ref[...]
        t = dh_f * g_ref[...]
        dx = dres_ref[...] + r * (t - xh * jnp.mean(t * xh, axis=-1, keepdims=True))
        dx_ref[...] = dx
        dxb_ref[...] = dx.astype(BF16)
        _accumulate(dg_ref, jnp.sum(dh_f * xh, axis=0, keepdims=True))

    return _rows_call(
        body, name, s_len,
        [_row_spec(bs, d), _vec_spec(d), _row_spec(bs, d), _row_spec(bs, d)],
        [_row_spec(bs, d), _row_spec(bs, d), _vec_spec(d)],
        [jax.ShapeDtypeStruct((s_len, d), F32), jax.ShapeDtypeStruct((s_len, d), BF16),
         jax.ShapeDtypeStruct((1, d), F32)], bs, sequential=True)(x, gain, dh, dres)


def _loss_head(name, x, gain, target):
    s_len, d = x.shape
    bs = _tile(s_len, 256)

    def body(x_ref, g_ref, t_ref, lv_ref, dx_ref, dxb_ref, dg_ref):
        xf = x_ref[...]
        r = _rms_scale(xf)
        xh = xf * r
        err = xh * g_ref[...] - t_ref[...]
        dy = err * (1.0 / d)
        t = dy * g_ref[...]
        dx = r * (t - xh * jnp.mean(t * xh, axis=-1, keepdims=True))
        dx_ref[...] = dx
        dxb_ref[...] = dx.astype(BF16)
        _accumulate(lv_ref, jnp.sum(err * err, axis=0, keepdims=True))
        _accumulate(dg_ref, jnp.sum(dy * xh, axis=0, keepdims=True))

    return _rows_call(
        body, name, s_len, [_row_spec(bs, d), _vec_spec(d), _row_spec(bs, d)],
        [_vec_spec(d), _row_spec(bs, d), _row_spec(bs, d), _vec_spec(d)],
        [jax.ShapeDtypeStruct((1, d), F32), jax.ShapeDtypeStruct((s_len, d), F32),
         jax.ShapeDtypeStruct((s_len, d), BF16), jax.ShapeDtypeStruct((1, d), F32)],
        bs, sequential=True)(x, gain, target)


def _sigmoid(g):
    return 1.0 / (1.0 + jnp.exp(-g))


def _gate_a_fwd(name, ypre, proj, scale):
    s_len, e = ypre.shape
    bs = _tile(s_len, 256)

    def body(y_ref, g_ref, sc_ref, z_ref):
        g = g_ref[...]
        z_ref[...] = (y_ref[...] * sc_ref[...] * (g * _sigmoid(g))).astype(BF16)

    return _rows_call(body, name, s_len, [_row_spec(bs, e), _row_spec(bs, e, 1), _vec_spec(e)], _row_spec(bs, e),
                      jax.ShapeDtypeStruct((s_len, e), BF16), bs)(ypre, proj, scale)


def _gate_a_bwd(name, dz, ypre, proj, scale):
    s_len, e = ypre.shape
    bs = _tile(s_len, 256)

    def body(dz_ref, y_ref, g_ref, sc_ref, dy_ref, dproj_ref, dsc_ref):
        g = g_ref[...]
        sg = _sigmoid(g)
        silu = g * sg
        dz_f = dz_ref[...]
        ypre_f = y_ref[...]
        dys = dz_f * silu
        dy_ref[...] = (dys * sc_ref[...]).astype(BF16)
        dproj_ref[...] = (dz_f * (ypre_f * sc_ref[...]) * (sg * (1.0 + g * (1.0 - sg)))).astype(BF16)
        _accumulate(dsc_ref, jnp.sum(dys * ypre_f, axis=0, keepdims=True))

    return _rows_call(
        body, name, s_len, [_row_spec(bs, e), _row_spec(bs, e), _row_spec(bs, e, 1), _vec_spec(e)],
        [_row_spec(bs, e), _row_spec(bs, e, 1), _vec_spec(e)],
        [jax.ShapeDtypeStruct((s_len, e), BF16), jax.ShapeDtypeStruct((s_len, 2 * e), BF16),
         jax.ShapeDtypeStruct((1, e), F32)], bs, sequential=True)(dz, ypre, proj, scale)


def _merge_gate_fwd(name, outs, lses, gate, dils):
    s_len, e = gate.shape
    bs = _tile(s_len, 256)
    n = len(outs)

    def body(*refs):
        o_refs, l_refs, g_ref = refs[:n], refs[n:2 * n], refs[2 * n]
        m_ref, lj_ref, z_ref = refs[2 * n + 1:2 * n + 4]
        scratch = refs[2 * n + 4]
        for c, sl in enumerate(_col_blocks(e)):
            ls = [_from_residue_major(r, scratch.at[2 * j, c], dil, sl) for j, (r, dil) in enumerate(zip(l_refs, dils))]
            os_ = [_from_residue_major(r, scratch.at[2 * j + 1, c], dil, sl) for j, (r, dil) in enumerate(zip(o_refs, dils))]
            mx = functools.reduce(jnp.maximum, ls)
            ws = [jnp.exp(l - mx) for l in ls]
            den = functools.reduce(lambda a, b: a + b, ws)
            merged = functools.reduce(lambda a, b: a + b, [w * o for w, o in zip(ws, os_)]) / den
            g = g_ref[:, sl]
            m_ref[:, sl] = merged.astype(BF16)
            lj_ref[:, sl] = mx + jnp.log(den)
            z_ref[:, sl] = (merged * (g * _sigmoid(g))).astype(BF16)

    spec = _row_spec(bs, e)
    res_specs = [_res_spec(dil, bs, e) for dil in dils]
    return pl.pallas_call(
        body, name=name, grid=(s_len // bs,), in_specs=res_specs + res_specs + [spec], out_specs=[spec] * 3,
        out_shape=[jax.ShapeDtypeStruct((s_len, e), BF16), jax.ShapeDtypeStruct((s_len, e), F32),
                   jax.ShapeDtypeStruct((s_len, e), BF16)],
        scratch_shapes=[pltpu.VMEM((2 * n, e // HEAD_DIM, bs, HEAD_DIM), F32)],
        compiler_params=_params(("parallel",)))(*outs, *lses, gate)


def _gate_b_bwd(name, dz, merged, gate, lse, dils):
    s_len, e = gate.shape
    bs = _tile(s_len, 256)
    n = len(dils)

    def body(dz_ref, m_ref, g_ref, l_ref, dg_ref, *rest):
        out_refs, scratch = rest[:3 * n], rest[3 * n]
        for c, sl in enumerate(_col_blocks(e)):
            g = g_ref[:, sl]
            sg = _sigmoid(g)
            dz_f = dz_ref[:, sl]
            merged = m_ref[:, sl].astype(F32)
            dmerged = dz_f * (g * sg)
            dg_ref[:, sl] = (dz_f * merged * (sg * (1.0 + g * (1.0 - sg)))).astype(BF16)
            values = (dmerged, l_ref[:, sl],
                      jnp.broadcast_to(jnp.sum(dmerged * merged, axis=-1, keepdims=True), (bs, HEAD_DIM)))
            for t, val in enumerate(values):
                scr = scratch.at[t, c]
                scr[...] = val
                for j, dil in enumerate(dils):
                    _to_residue_major(out_refs[3 * j + t], scr, dil, sl)

    spec = _row_spec(bs, e)
    out_specs, out_shape = [spec], [jax.ShapeDtypeStruct((s_len, e), BF16)]
    for dil in dils:
        out_specs += [_res_spec(dil, bs, e)] * 3
        out_shape += [_res_shape(dil, s_len, e, BF16), _res_shape(dil, s_len, e, F32), _res_shape(dil, s_len, e, F32)]
    res = pl.pallas_call(
        body, name=name, grid=(s_len // bs,), in_specs=[spec] * 4, out_specs=out_specs, out_shape=out_shape,
        scratch_shapes=[pltpu.VMEM((3, e // HEAD_DIM, bs, HEAD_DIM), F32)],
        compiler_params=_params(("parallel",)))(dz, merged, gate, lse)
    return res[0], [tuple(res[1 + 3 * j:4 + 3 * j]) for j in range(n)]


def _kv_grad_prep(name, dk_accs, dv_accs, dils, cos, sin_inv):
    n = len(dils)
    e = dk_accs[0].shape[-1]
    s_len = dk_accs[0].shape[0] * dk_accs[0].shape[1]
    bs = _tile(s_len, 256)

    def body(*refs):
        dk_refs, dv_refs = refs[:n], refs[n:2 * n]
        c_ref, s_ref, dkb_ref, dvb_ref, scratch = refs[2 * n:]
        cos_t, sin_t = c_ref[...], s_ref[...]
        add = lambda a, b: a + b
        for c, sl in enumerate(_col_blocks(e)):
            dk = functools.reduce(add, [_from_residue_major(r, scratch.at[j, c], dil, sl)
                                        for j, (r, dil) in enumerate(zip(dk_refs, dils))])
            dkb_ref[:, sl] = _rope_apply(dk, cos_t, sin_t).astype(BF16)
            dv = functools.reduce(add, [_from_residue_major(r, scratch.at[n + j, c], dil, sl)
                                        for j, (r, dil) in enumerate(zip(dv_refs, dils))])
            dvb_ref[:, sl] = dv.astype(BF16)

    spec, rspec = _row_spec(bs, e), _row_spec(bs, HEAD_DIM)
    res_specs = [_res_spec(dil, bs, e) for dil in dils]
    return pl.pallas_call(
        body, name=name, grid=(s_len // bs,), in_specs=res_specs + res_specs + [rspec, rspec], out_specs=[spec, spec],
        out_shape=[jax.ShapeDtypeStruct((s_len, e), BF16)] * 2,
        scratch_shapes=[pltpu.VMEM((2 * n, e // HEAD_DIM, bs, HEAD_DIM), F32)],
        compiler_params=_params(("parallel",)))(*dk_accs, *dv_accs, cos, sin_inv)


def _pool_cols(e):
    return _tile(e // len(POOL_WINDOWS), 256)


def _window_sum(val, grp, s_len, forward):
    rows = lax.broadcasted_iota(jnp.int32, val.shape, 0)
    acc = val
    for level in range(len(POOL_WINDOWS)):
        step = 1 << level
        if forward:
            shifted = jnp.where(rows >= step, pltpu.roll(acc, step, 0), 0.0)
        else:
            shifted = jnp.where(rows < s_len - step, pltpu.roll(acc, s_len - step, 0), 0.0)
        acc = jnp.where(level <= grp, acc + shifted, acc)
    return acc


def _window_count(shape, grp):
    rows = lax.broadcasted_iota(jnp.int32, shape, 0)
    return jnp.minimum(rows + 1, jnp.left_shift(2, grp)).astype(F32)


def _pool_fwd(name, proj):
    s_len, e2 = proj.shape
    e = e2 // 2
    cb = _pool_cols(e)
    per_grp = e // len(POOL_WINDOWS) // cb
    assert POOL_WINDOWS == tuple(2 << g for g in range(len(POOL_WINDOWS)))

    def body(u_ref, p_ref):
        grp = pl.program_id(0)
        u = u_ref[...]
        total = _window_sum(u, grp, s_len, True)
        p_ref[...] = (total / _window_count(u.shape, grp) - u).astype(BF16)

    spec = pl.BlockSpec((s_len, cb), lambda g, c: (0, g * per_grp + c))
    return pl.pallas_call(
        body, name=name, grid=(len(POOL_WINDOWS), per_grp), in_specs=[spec], out_specs=spec,
        out_shape=jax.ShapeDtypeStruct((s_len, e), BF16), compiler_params=_params(("parallel", "parallel")))(proj)


def _pool_bwd(name, dpooled, dproj):
    s_len, e = dpooled.shape
    cb = _pool_cols(e)
    per_grp = e // len(POOL_WINDOWS) // cb

    def body(dp_ref, _, du_ref):
        grp = pl.program_id(0)
        dp = dp_ref[...]
        total = _window_sum(dp / _window_count(dp.shape, grp), grp, s_len, False)
        du_ref[...] = (total - dp).astype(BF16)

    spec = pl.BlockSpec((s_len, cb), lambda g, c: (0, g * per_grp + c))
    return pl.pallas_call(
        body, name=name, grid=(len(POOL_WINDOWS), per_grp), in_specs=[spec, ANY], out_specs=spec,
        out_shape=jax.ShapeDtypeStruct(dproj.shape, BF16), input_output_aliases={1: 0},
        compiler_params=_params(("parallel", "parallel")))(dpooled, dproj)


def _band_masks(nb, first):
    row = lax.broadcasted_iota(jnp.int32, (nb, nb), 0)
    col = lax.broadcasted_iota(jnp.int32, (nb, nb), 1)
    return col >= row + jnp.where(first, 2 * nb, 0), col <= row


def _dot(a, b, dims):
    return lax.dot_general(a, b, (dims, ((), ())), preferred_element_type=F32)


def _attn_fwd(name, window, q, k, v):
    dil, m, e = k.shape
    nb = window // dil
    nblk = m // nb
    heads = e // HEAD_DIM

    def body(q_ref, kc_ref, vc_ref, o_ref, l_ref, kp_ref, vp_ref):
        first = pl.program_id(1) == 0

        @pl.when(first)
        def _():
            kp_ref[...] = jnp.zeros_like(kp_ref)
            vp_ref[...] = jnp.zeros_like(vp_ref)

        mask_p, mask_c = _band_masks(nb, first)
        cols = _col_blocks(e)
        s_p = [jnp.where(mask_p, _dot(q_ref[:, sl], kp_ref[:, sl], NT), NEG_INF) for sl in cols]
        s_c = [jnp.where(mask_c, _dot(q_ref[:, sl], kc_ref[:, sl], NT), NEG_INF) for sl in cols]
        mx = [jnp.maximum(jnp.max(a, axis=-1, keepdims=True), jnp.max(b, axis=-1, keepdims=True))
              for a, b in zip(s_p, s_c)]
        p_p = [jnp.exp(a - m) for a, m in zip(s_p, mx)]
        p_c = [jnp.exp(a - m) for a, m in zip(s_c, mx)]
        den = [jnp.sum(a, axis=-1, keepdims=True) + jnp.sum(b, axis=-1, keepdims=True) for a, b in zip(p_p, p_c)]
        for h, sl in enumerate(cols):
            out = _dot(p_p[h].astype(BF16), vp_ref[:, sl], NN) + _dot(p_c[h].astype(BF16), vc_ref[:, sl], NN)
            o_ref[:, sl] = (out / den[h]).astype(BF16)
            l_ref[:, sl] = jnp.broadcast_to(mx[h] + jnp.log(den[h]), (nb, HEAD_DIM))
        kp_ref[...] = kc_ref[...]
        vp_ref[...] = vc_ref[...]

    blk = (None, nb, e)
    cur = lambda r, n: (r, n, 0)
    return pl.pallas_call(
        body, name=name, grid=(dil, nblk),
        in_specs=[pl.BlockSpec(blk, cur)] * 3,
        out_specs=[pl.BlockSpec(blk, cur), pl.BlockSpec(blk, cur)],
        out_shape=[jax.ShapeDtypeStruct((dil, m, e), BF16), jax.ShapeDtypeStruct((dil, m, e), F32)],
        scratch_shapes=[pltpu.VMEM((nb, e), BF16), pltpu.VMEM((nb, e), BF16)],
        compiler_params=_params(("parallel", "arbitrary")),
    )(q, k, v)


def _attn_bwd(name, window, scale, q, k, v, dout, lse, delta, cos, sin_inv, dk_acc, dv_acc):
    dil, m, e = k.shape
    nb = window // dil
    nblk = m // nb
    heads = e // HEAD_DIM

    accumulate = dk_acc is not None

    def body(k_ref, v_ref, q0_ref, qn_ref, do0_ref, don_ref, l0_ref, ln_ref, dl0_ref, dln_ref, c_ref, s_ref, *rest):
        if accumulate:
            dki_ref, dvi_ref = rest[:2]
            rest = rest[2:]
        dq_ref, dko_ref, dvo_ref, carry_ref, qc_ref, doc_ref, lc_ref, dlc_ref = rest
        n = pl.program_id(1)

        @pl.when(n == 0)
        def _():
            carry_ref[...] = jnp.zeros_like(carry_ref)
            qc_ref[...] = q0_ref[...]
            doc_ref[...] = do0_ref[...]
            lc_ref[...] = l0_ref[...]
            dlc_ref[...] = dl0_ref[...]

        mask_n, mask_c = _band_masks(nb, n == nblk - 1)
        cos_t, sin_t = c_ref[...], s_ref[...]
        cols = _col_blocks(e)
        stat = lambda ref, sl: ref[:, sl] if nb == HEAD_DIM else ref[:, sl][:, :1]
        s_c = [_dot(qc_ref[:, sl], k_ref[:, sl], NT) for sl in cols]
        s_n = [_dot(qn_ref[:, sl], k_ref[:, sl], NT) for sl in cols]
        dp_c = [_dot(doc_ref[:, sl], v_ref[:, sl], NT) for sl in cols]
        dp_n = [_dot(don_ref[:, sl], v_ref[:, sl], NT) for sl in cols]
        p_c = [jnp.where(mask_c, jnp.exp(s - stat(lc_ref, sl)), 0.0) for s, sl in zip(s_c, cols)]
        p_n = [jnp.where(mask_n, jnp.exp(s - stat(ln_ref, sl)), 0.0) for s, sl in zip(s_n, cols)]
        ds_c = [(p * (dp - stat(dlc_ref, sl))).astype(BF16) for p, dp, sl in zip(p_c, dp_c, cols)]
        ds_n = [(p * (dp - stat(dln_ref, sl))).astype(BF16) for p, dp, sl in zip(p_n, dp_n, cols)]
        for h, sl in enumerate(cols):
            dq = (carry_ref[:, sl] + _dot(ds_c[h], k_ref[:, sl], NN)) * scale
            dq_ref[:, sl] = _rope_apply(dq, cos_t, sin_t).astype(BF16)
        for h, sl in enumerate(cols):
            carry_ref[:, sl] = _dot(ds_n[h], k_ref[:, sl], NN)
        for h, sl in enumerate(cols):
            dk = _dot(ds_c[h], qc_ref[:, sl], TN) + _dot(ds_n[h], qn_ref[:, sl], TN)
            dv = _dot(p_c[h].astype(BF16), doc_ref[:, sl], TN) + _dot(p_n[h].astype(BF16), don_ref[:, sl], TN)
            dko_ref[:, sl] = dki_ref[:, sl] + dk if accumulate else dk
            dvo_ref[:, sl] = dvi_ref[:, sl] + dv if accumulate else dv
        qc_ref[...] = qn_ref[...]
        doc_ref[...] = don_ref[...]
        lc_ref[...] = ln_ref[...]
        dlc_ref[...] = dln_ref[...]

    blk = (None, nb, e)
    cur = lambda r, n: (r, n, 0)
    nxt = lambda r, n: (r, jnp.minimum(n + 1, nblk - 1), 0)
    first = lambda r, n: (r, 0, 0)
    rblk = (None, nb, HEAD_DIM)
    both = lambda shape: [pl.BlockSpec(shape, first), pl.BlockSpec(shape, nxt)]
    accs = (dk_acc, dv_acc) if accumulate else ()
    return pl.pallas_call(
        body, name=name, grid=(dil, nblk),
        in_specs=[pl.BlockSpec(blk, cur), pl.BlockSpec(blk, cur), *both(blk), *both(blk), *both(blk), *both(blk),
                  pl.BlockSpec(rblk, cur), pl.BlockSpec(rblk, cur)] + [pl.BlockSpec(blk, cur)] * len(accs),
        out_specs=[pl.BlockSpec(blk, cur)] * 3,
        out_shape=[jax.ShapeDtypeStruct((dil, m, e), BF16),
                   jax.ShapeDtypeStruct((dil, m, e), F32), jax.ShapeDtypeStruct((dil, m, e), F32)],
        scratch_shapes=[pltpu.VMEM((nb, e), F32), pltpu.VMEM((nb, e), BF16), pltpu.VMEM((nb, e), BF16),
                        pltpu.VMEM((nb, e), F32), pltpu.VMEM((nb, e), F32)],
        input_output_aliases={12: 1, 13: 2} if accumulate else {},
        compiler_params=_params(("parallel", "arbitrary")),
    )(k, v, q, q, dout, dout, lse, lse, delta, delta, cos, sin_inv, *accs)


def _rope_tables(s_len):
    inv_freq = 1.0 / (ROPE_THETA ** (jnp.arange(0, HEAD_DIM, 2, dtype=F32) / HEAD_DIM))
    ang = jnp.arange(s_len, dtype=F32)[:, None] * inv_freq[None, :]
    cos, sin = jnp.cos(ang), jnp.sin(ang)
    return jnp.concatenate([cos, cos], axis=1), jnp.concatenate([-sin, sin], axis=1)


def _row(vec):
    return vec.reshape(1, -1)


def _local_step(x, target, n_a, n_b, fetch, begin, emit, relay=lambda group, carry: carry):
    s_len, d = x.shape
    n_q = len(DILATED_PAIRS)
    cos, sin = _rope_tables(s_len)
    sin_inv = -sin
    q_scale = 1.0 / math.sqrt(HEAD_DIM)
    w = {}

    def need(group, after):
        for name, (layer, arr) in fetch(group, after).items():
            w.setdefault(name, {})[layer] = arr

    saved_a = []
    for i in range(n_a):
        need(f"a{i}", x)
        h = _rmsnorm_fwd(f"a{i}_norm", x, _row(w["norm_a"][i]))
        proj = _mm_act_w(f"a{i}_in", h, w["w_in_a"][i], out_dtype=F32)
        pooled = _pool_fwd(f"a{i}_pool", proj)
        ypre = _mm_grp_fwd(f"a{i}_grp", pooled, w["w_grp_a"][i])
        z = _gate_a_fwd(f"a{i}_gate", ypre, proj, _row(w["scale_a"][i]))
        if i + 1 < n_a:
            (z,) = relay(f"a{i + 1}", (z,))
        x_next = _mm_act_w(f"a{i}_out", z, w["w_out_a"][i], out_dtype=F32, add=x)
        saved_a.append((x, h, proj, pooled, ypre, z))
        x = x_next

    x_kv = x
    need("kv", x)
    e = w["w_k"][0].shape[1]
    kv_in = _rmsnorm_fwd("kv_norm", x, _row(w["norm_kv"][0]))
    windows = [window for window, _ in DILATED_PAIRS]
    dils = tuple(dil for _, dil in DILATED_PAIRS)
    far_dils = tuple(dil for dil in dils if dil > 1)
    ks = _mm_act_w("kv_k", kv_in, w["w_k"][0], rope=(cos, sin, 1.0), dils=dils)
    if n_b:
        (kv_in,) = relay("b0", (kv_in,))
    vs = _mm_act_w("kv_v", kv_in, w["w_v"][0], dils=dils)

    saved_b = []
    for i in range(n_b):
        need(f"b{i}", x if i > 0 else vs[0])
        hs = _rmsnorm_fwd(f"b{i}_norm", x, _row(w["norm_b"][i]), dils=far_dils)
        hs = {1: hs[0], **{dil: h_d.reshape(s_len, d) for dil, h_d in zip(far_dils, hs[1:])}}
        qs = [_mm_act_w(f"b{i}_q{g}", hs[1], w["w_in_b"][i], rope=(cos, sin, q_scale), n_first=g, n_cols=1,
                        dils=(dil,))[0] for g, dil in enumerate(dils)]
        gate = _mm_act_w(f"b{i}_g", hs[1], w["w_in_b"][i], out_dtype=F32, n_first=n_q, n_cols=1)
        outs, lses = [], []
        for g in range(n_q):
            o_g, l_g = _attn_fwd(f"b{i}_attn{g}", windows[g], qs[g], ks[g], vs[g])
            outs.append(o_g)
            lses.append(l_g)
        merged, lse, z = _merge_gate_fwd(f"b{i}_merge", outs, lses, gate, dils)
        if i + 1 < n_b:
            (z,) = relay(f"b{i + 1}", (z,))
        x_next = _mm_act_w(f"b{i}_out", z, w["w_out_b"][i], out_dtype=F32, add=x)
        saved_b.append((x, hs, qs, gate, merged, lse, z))
        x = x_next

    need("head", x)
    loss_vec, dx, dxb, g_norm_f = _loss_head("loss_head", x, _row(w["norm_f"][0]), target)

    small = {"norm_a": {}, "scale_a": {}, "norm_kv": {}, "norm_b": {}, "norm_f": {0: g_norm_f}}
    shard_rows = lambda g2: g2.reshape(N_CHIPS, g2.shape[0] // N_CHIPS, g2.shape[1])

    res_major = lambda t, dil: t.reshape(s_len // dil, dil, t.shape[1]).transpose(1, 0, 2)
    cos_r = [res_major(cos, dil) for dil in dils]
    sin_inv_r = [res_major(sin_inv, dil) for dil in dils]
    dk_accs = [None] * len(dils)
    dv_accs = [None] * len(dils)
    for i in reversed(range(n_b)):
        x_in, hs, qs, gate, merged, lse, z = saved_b[i]
        dz = _mm_grad_act(f"b{i}_dz", dxb, w["w_out_b"][i])
        g_out = shard_rows(_mm_grad_w(f"b{i}_gwo", z, dxb))
        dgate, stats = _gate_b_bwd(f"b{i}_dgate", dz, merged, gate, lse, dils)
        dh = _mm_grad_act(f"b{i}_dh{n_q}", dgate, w["w_in_b"][i], slot=n_q)
        g_in = _mm_grad_w(f"b{i}_gwi{n_q}", hs[1], dgate, col_shards=n_q + 1, slot=n_q)
        for g, dil in enumerate(dils):
            dout, lse_g, delta_g = stats[g]
            dq, dk_accs[g], dv_accs[g] = _attn_bwd(f"b{i}_dattn{g}", windows[g], q_scale, qs[g], ks[g], vs[g], dout,
                                                   lse_g, delta_g, cos_r[g], sin_inv_r[g], dk_accs[g], dv_accs[g])
            dh = _mm_grad_act(f"b{i}_dh{g}", dq if dil > 1 else dq[0], w["w_in_b"][i], add=dh, slot=g)
            g_in = _mm_grad_w(f"b{i}_gwi{g}", hs[dil], dq.reshape(s_len, e), col_shards=n_q + 1, slot=g, into=g_in)
        (dh,) = begin(f"b{i}", {"w_in_b": (i, g_in), "w_out_b": (i, g_out)}, (dh,))
        dx, dxb, small["norm_b"][i] = _rmsnorm_bwd(f"b{i}_dnorm", x_in, _row(w["norm_b"][i]), dh, dx)
        dx, dxb = emit(f"b{i}", (dx, dxb))

    dkb, dvb = _kv_grad_prep("kv_dprep", dk_accs, dv_accs, dils, cos, sin_inv)
    dkv = _mm_grad_act("kv_dk", dkb, w["w_k"][0])
    dkv = _mm_grad_act("kv_dv", dvb, w["w_v"][0], add=dkv)
    g_k = shard_rows(_mm_grad_w("kv_gwk", kv_in, dkb))
    g_v = shard_rows(_mm_grad_w("kv_gwv", kv_in, dvb))
    (dkv,) = begin("kv", {"w_k": (0, g_k), "w_v": (0, g_v)}, (dkv,))
    dx, dxb, small["norm_kv"][0] = _rmsnorm_bwd("kv_dnorm", x_kv, _row(w["norm_kv"][0]), dkv, dx)
    dx, dxb = emit("kv", (dx, dxb))

    for i in reversed(range(n_a)):
        x_in, h, proj, pooled, ypre, z = saved_a[i]
        dz = _mm_grad_act(f"a{i}_dz", dxb, w["w_out_a"][i])
        g_out = shard_rows(_mm_grad_w(f"a{i}_gwo", z, dxb))
        dypre, dproj, small["scale_a"][i] = _gate_a_bwd(f"a{i}_dgate", dz, ypre, proj, _row(w["scale_a"][i]))
        dpooled = _mm_grp_grad_act(f"a{i}_dgrp", dypre, w["w_grp_a"][i])
        g_grp = _mm_grp_grad_w(f"a{i}_gwg", pooled, dypre, len(POOL_WINDOWS))
        g_grp = g_grp.reshape(N_CHIPS, -1, g_grp.shape[-1])
        last = i == 0
        if last:
            (dpooled,) = emit(f"a{i}", begin(f"a{i}", {"w_grp_a": (i, g_grp), "w_out_a": (i, g_out)}, (dpooled,)))
        dproj = _pool_bwd(f"a{i}_dpool", dpooled, dproj)
        g_in = _mm_grad_w(f"a{i}_gwi", h, dproj, col_shards=N_CHIPS)
        if last:
            (dproj,) = emit(f"a{i}i", begin(f"a{i}i", {"w_in_a": (i, g_in)}, (dproj,)))
        dh = _mm_grad_act(f"a{i}_dh", dproj, w["w_in_a"][i])
        if not last:
            (dh,) = begin(f"a{i}", {"w_in_a": (i, g_in), "w_grp_a": (i, g_grp), "w_out_a": (i, g_out)}, (dh,))
        dx, dxb, small["norm_a"][i] = _rmsnorm_bwd(f"a{i}_dnorm", x_in, _row(w["norm_a"][i]), dh, dx)
        if not last:
            dx, dxb = emit(f"a{i}", (dx, dxb))

    return loss_vec, dx, small


BIG_WEIGHTS = ("w_in_a", "w_grp_a", "w_out_a", "w_k", "w_v", "w_in_b", "w_out_b")


def _pair_add(name, grad, recv, c_idx):
    _, r, cols = grad.shape
    half = r // 2
    rb = _tile(half, 256)
    nrb = half // rb

    def body(c_ref, g_ref, r_ref, o_ref):
        o_ref[...] = (g_ref[...].astype(F32) + r_ref[...].astype(F32)).astype(BF16)

    blk = (None, rb, cols)
    grid_spec = pltpu.PrefetchScalarGridSpec(
        num_scalar_prefetch=1, grid=(N_CHIPS, nrb),
        in_specs=[pl.BlockSpec(blk, lambda s, i, c: (s, c[0] * nrb + i, 0)), pl.BlockSpec(blk, lambda s, i, c: (s, i, 0))],
        out_specs=pl.BlockSpec(blk, lambda s, i, c: (s, i, 0)))
    return pl.pallas_call(body, name=name, grid_spec=grid_spec,
                          out_shape=jax.ShapeDtypeStruct((N_CHIPS, half, cols), BF16),
                          compiler_params=_params(("parallel", "parallel")))(c_idx, grad, recv)


def _final_add(name, part, recv, sc_idx, layer, n_layers, into=None):
    _, half, cols = part.shape
    rb = _tile(half, 256)
    nrb = half // rb
    n_peer = recv.shape[0]

    def body(sc_ref, p_ref, *refs):
        acc = p_ref[...].astype(F32)
        for r_ref in refs[:n_peer]:
            acc = acc + r_ref[...].astype(F32)
        refs[-1][...] = acc

    blk = (None, rb, cols)
    peer_spec = lambda k: pl.BlockSpec(blk, lambda i, sc: (k, i, 0))
    grid_spec = pltpu.PrefetchScalarGridSpec(
        num_scalar_prefetch=1, grid=(nrb,),
        in_specs=[pl.BlockSpec(blk, lambda i, sc: (sc[0], i, 0))] + [peer_spec(k) for k in range(n_peer)]
                 + ([] if into is None else [ANY]),
        out_specs=pl.BlockSpec(blk, lambda i, sc: (layer, sc[1] * nrb + i, 0)))
    extra = () if into is None else (into,)
    return pl.pallas_call(body, name=name, grid_spec=grid_spec,
                          out_shape=jax.ShapeDtypeStruct((n_layers, 2 * half, cols), F32),
                          input_output_aliases={} if into is None else {2 + n_peer: 0},
                          compiler_params=_params(("parallel",)))(sc_idx, part, *([recv] * n_peer), *extra)


def _cast_into_slot(name, arr, layer, s_idx, after=None):
    _, b, r, cols = arr.shape
    rb = _tile(r, 512)

    def body(s_ref, a_ref, *rest):
        rest[-1][...] = a_ref[...].astype(BF16)

    blk = (None, None, rb, cols)
    grid_spec = pltpu.PrefetchScalarGridSpec(
        num_scalar_prefetch=1, grid=(b, r // rb),
        in_specs=[pl.BlockSpec(blk, lambda j, i, s: (layer, j, i, 0))] + ([] if after is None else [ANY]),
        out_specs=pl.BlockSpec(blk, lambda j, i, s: (j, s[0], i, 0)))
    return pl.pallas_call(body, name=name, grid_spec=grid_spec,
                          out_shape=jax.ShapeDtypeStruct((b, N_CHIPS, r, cols), BF16),
                          compiler_params=_params(("parallel", "parallel")))(
                              s_idx, arr, *(() if after is None else (after,)))


def _sum_devices(name, gathered, own, me_idx):
    n_dev, p, d = gathered.shape

    def body(me_ref, g_ref, own_ref, o_ref):
        acc = None
        for j in range(n_dev):
            term = jnp.where(me_ref[0] == j, own_ref[...], g_ref[j])
            acc = term if acc is None else acc + term
        o_ref[...] = acc

    grid_spec = pltpu.PrefetchScalarGridSpec(
        num_scalar_prefetch=1, grid=(1,),
        in_specs=[pl.BlockSpec((n_dev, p, d), lambda i, me: (0, 0, 0)), pl.BlockSpec((p, d), lambda i, me: (0, 0))],
        out_specs=pl.BlockSpec((p, d), lambda i, me: (0, 0)))
    return pl.pallas_call(body, name=name, grid_spec=grid_spec, out_shape=jax.ShapeDtypeStruct((p, d), F32),
                          compiler_params=_params(("arbitrary",)))(me_idx, gathered, own)


def _adamw_block(w_ref, g_ref, m_ref, v_ref, d_ref, mo_ref, vo_ref):
    grad = g_ref[...]
    m_new = ADAM_B1 * m_ref[...] + (1.0 - ADAM_B1) * grad
    v_new = ADAM_B2 * v_ref[...] + (1.0 - ADAM_B2) * (grad * grad)
    m_hat = m_new / (1.0 - ADAM_B1 ** ADAM_STEP)
    v_hat = v_new / (1.0 - ADAM_B2 ** ADAM_STEP)
    d_ref[...] = -ADAM_LR * (m_hat / (jnp.sqrt(v_hat) + ADAM_EPS) + ADAM_WD * w_ref[...])
    mo_ref[...] = m_new
    vo_ref[...] = v_new


def _adamw(name, w, g, m, v):
    shape = w.shape
    cols = shape[-1]
    flat = lambda a: a.reshape(-1, cols)
    rows = flat(w).shape[0]
    bs = _tile(rows, 256)

    def body(*refs):
        _adamw_block(*refs)

    spec = _row_spec(bs, cols)
    outs = _rows_call(body, name, rows, [spec] * 4, [spec] * 3, [jax.ShapeDtypeStruct((rows, cols), F32)] * 3, bs)(
        flat(w), flat(g), flat(m), flat(v))
    return tuple(o.reshape(shape) for o in outs)


def _adamw_half(name, w, g, m, v, half_idx, into=None):
    n_l, r, cols = w.shape
    rb = _tile(r // 2, 256)
    nrb = r // 2 // rb

    def body(h_ref, w_ref, g_ref, m_ref, v_ref, *rest):
        _adamw_block(w_ref, g_ref, m_ref, v_ref, *rest[-3:])

    spec = pl.BlockSpec((None, rb, cols), lambda j, i, h: (j, h[0] * nrb + i, 0))
    extra = () if into is None else tuple(into)
    grid_spec = pltpu.PrefetchScalarGridSpec(
        num_scalar_prefetch=1, grid=(n_l, nrb), in_specs=[spec] * 4 + [ANY] * len(extra), out_specs=[spec] * 3)
    return tuple(pl.pallas_call(
        body, name=name, grid_spec=grid_spec, out_shape=[jax.ShapeDtypeStruct(w.shape, F32)] * 3,
        input_output_aliases={5 + k: k for k in range(len(extra))},
        compiler_params=_params(("parallel", "parallel")))(half_idx, w, g, m, v, *extra))


def _place():
    x, y, c = lax.axis_index("x"), lax.axis_index("y"), lax.axis_index("c")
    chips = [(1 - x, y), (x, 1 - y), (1 - x, 1 - y)]
    return x, y, c, chips


def _chip_index(chip):
    return 2 * chip[0] + chip[1]


def _comm_call(body, name, n_in, out_shape, scratch, aliases=None):
    return pl.pallas_call(body, name=name, in_specs=[ANY] * n_in, out_specs=[ANY] * len(out_shape), out_shape=out_shape,
                          scratch_shapes=scratch, input_output_aliases=aliases or {})


HBM_SPEC = pl.BlockSpec(memory_space=pltpu.HBM)
SEM_SPEC = pl.BlockSpec(memory_space=pltpu.SEMAPHORE)
SPLIT_PARAMS = pltpu.CompilerParams(has_side_effects=pltpu.SideEffectType.DATAFLOW_SIDE_EFFECTING)


def _in_hbm(arr):
    return pltpu.with_memory_space_constraint(arr, pltpu.HBM)


def _slot_half(ref, chip, core):
    half = ref.shape[2] // 2
    return ref.at[:, _chip_index(chip), pl.ds(core * half, half), :]


def _gather_start(name, bufs, carry=()):
    n, n_c = len(bufs), len(carry)

    def body(*refs):
        ins, (send_sems, recv_sems) = refs[:n], refs[n + n_c:n + n_c + 2]
        x, y, c, chips = _place()
        for a in range(n):
            block = _slot_half(ins[a], (x, y), c)
            for k, chip in enumerate(chips):
                pltpu.make_async_remote_copy(src_ref=block, dst_ref=block, send_sem=send_sems.at[3 * a + k],
                                             recv_sem=recv_sems.at[3 * a + k], device_id=(*chip, c),
                                             device_id_type=MESH).start()

    dma = pltpu.SemaphoreType.DMA
    thru = list(bufs) + list(carry)
    res = pl.pallas_call(
        body, name=name, in_specs=[HBM_SPEC] * (n + n_c), out_specs=[SEM_SPEC] * 2 + [HBM_SPEC] * (n + n_c),
        out_shape=[dma((3 * n,)), dma((3 * n,))] + [pltpu.HBM(a.shape, a.dtype) for a in thru],
        input_output_aliases={t: 2 + t for t in range(n + n_c)}, compiler_params=SPLIT_PARAMS,
    )(*[_in_hbm(a) for a in thru])
    return (res[0], res[1]), list(res[2:2 + n]), list(res[2 + n:])


def _gather_wait(name, sems, bufs, after):
    n = len(bufs)

    def body(*refs):
        ins, (send_sems, recv_sems) = refs[:n], refs[n:n + 2]
        x, y, c, chips = _place()
        for a in range(n):
            for k, chip in enumerate(chips):
                mine, theirs = _slot_half(ins[a], (x, y), c), _slot_half(ins[a], chip, c)
                copy = pltpu.make_async_remote_copy(src_ref=mine, dst_ref=theirs, send_sem=send_sems.at[3 * a + k],
                                                    recv_sem=recv_sems.at[3 * a + k], device_id=(*chip, c),
                                                    device_id_type=MESH)
                copy.wait_send()
                copy.wait_recv()

    res = pl.pallas_call(
        body, name=name, in_specs=[HBM_SPEC] * n + [SEM_SPEC, SEM_SPEC, ANY], out_specs=[HBM_SPEC] * n,
        out_shape=[pltpu.HBM(b.shape, b.dtype) for b in bufs], input_output_aliases={a: a for a in range(n)},
        compiler_params=SPLIT_PARAMS)(*bufs, *sems, after)
    return list(res)


def _gather_relay(name, sems, bufs, carry):
    n, n_c = len(bufs), len(carry)

    def body(*refs):
        ins = refs[:n]
        send_in, recv_in = refs[n + n_c:n + n_c + 2]
        send_out, recv_out = refs[n + n_c + 2:n + n_c + 4]
        x, y, c, chips = _place()
        for a in range(n):
            for k, chip in enumerate(chips):
                mine, theirs = _slot_half(ins[a], (x, y), c), _slot_half(ins[a], chip, c)
                copy = pltpu.make_async_remote_copy(src_ref=mine, dst_ref=theirs, send_sem=send_in.at[3 * a + k],
                                                    recv_sem=recv_in.at[3 * a + k], device_id=(*chip, c),
                                                    device_id_type=MESH)
                copy.wait_send()
                copy.wait_recv()
        for a in range(n):
            for k, chip in enumerate(chips):
                block = _slot_half(ins[a], chip, c)
                pltpu.make_async_remote_copy(src_ref=block, dst_ref=block, send_sem=send_out.at[3 * a + k],
                                             recv_sem=recv_out.at[3 * a + k], device_id=(x, y, 1 - c),
                                             device_id_type=MESH).start()

    dma = pltpu.SemaphoreType.DMA
    thru = list(bufs) + list(carry)
    res = pl.pallas_call(
        body, name=name, in_specs=[HBM_SPEC] * (n + n_c) + [SEM_SPEC, SEM_SPEC],
        out_specs=[SEM_SPEC] * 2 + [HBM_SPEC] * (n + n_c),
        out_shape=[dma((3 * n,)), dma((3 * n,))] + [pltpu.HBM(a.shape, a.dtype) for a in thru],
        input_output_aliases={t: 2 + t for t in range(n + n_c)}, compiler_params=SPLIT_PARAMS,
    )(*[_in_hbm(a) for a in thru], *sems)
    return (res[0], res[1]), list(res[2:2 + n]), list(res[2 + n:])


def _gather_relay_wait(name, sems, bufs, after):
    n = len(bufs)

    def body(*refs):
        ins, (send_sems, recv_sems) = refs[:n], refs[n:n + 2]
        x, y, c, chips = _place()
        for a in range(n):
            for k, chip in enumerate(chips):
                copy = pltpu.make_async_remote_copy(
                    src_ref=_slot_half(ins[a], chip, c), dst_ref=_slot_half(ins[a], chip, 1 - c),
                    send_sem=send_sems.at[3 * a + k], recv_sem=recv_sems.at[3 * a + k], device_id=(x, y, 1 - c),
                    device_id_type=MESH)
                copy.wait_send()
                copy.wait_recv()

    res = pl.pallas_call(
        body, name=name, in_specs=[HBM_SPEC] * n + [SEM_SPEC, SEM_SPEC, ANY], out_specs=[HBM_SPEC] * n,
        out_shape=[pltpu.HBM(b.shape, b.dtype) for b in bufs], input_output_aliases={a: a for a in range(n)},
        compiler_params=SPLIT_PARAMS)(*bufs, *sems, after)
    return list(res)


def _gather_forward(name, bufs, smalls=()):
    n, n_small = len(bufs), len(smalls)

    def body(*refs):
        small_in = refs[n:n + n_small]
        outs = refs[n + n_small:2 * n + n_small]
        small_out = refs[2 * n + n_small:2 * n + 2 * n_small]
        send_sems, recv_sems, s_send, s_recv, s_local = refs[-5:]
        x, y, c, chips = _place()
        me, sibling = _chip_index((x, y)), (x, y, 1 - c)

        def forward(t, k, core):
            block = _slot_half(outs[t], chips[k], core)
            return pltpu.make_async_remote_copy(src_ref=block, dst_ref=block, send_sem=send_sems.at[t, k],
                                                recv_sem=recv_sems.at[t, k], device_id=sibling, device_id_type=MESH)

        def small_copy(j, k, slot):
            return pltpu.make_async_remote_copy(src_ref=small_in[j], dst_ref=small_out[j].at[slot],
                                                send_sem=s_send.at[j, k], recv_sem=s_recv.at[j, k],
                                                device_id=(*chips[k], c), device_id_type=MESH)

        local = []
        for t in range(n):
            for k in range(3):
                forward(t, k, c).start()
        for j in range(n_small):
            own = pltpu.make_async_copy(small_in[j], small_out[j].at[me], s_local.at[j])
            own.start()
            local.append(own)
            for k in range(3):
                small_copy(j, k, me).start()
        for t in range(n):
            for k in range(3):
                forward(t, k, 1 - c).wait_recv()
        for j in range(n_small):
            for k in range(3):
                small_copy(j, k, _chip_index(chips[k])).wait_recv()
        for t in range(n):
            for k in range(3):
                forward(t, k, c).wait_send()
        for j in range(n_small):
            for k in range(3):
                small_copy(j, k, me).wait_send()
        for own in local:
            own.wait()

    out_shape = [jax.ShapeDtypeStruct(b.shape, BF16) for b in bufs]
    out_shape += [jax.ShapeDtypeStruct((N_CHIPS,) + s.shape, F32) for s in smalls]
    dma = pltpu.SemaphoreType.DMA
    n_s = max(n_small, 1)
    res = _comm_call(body, name, n + n_small, out_shape,
                     [dma((n, 3)), dma((n, 3)), dma((n_s, 3)), dma((n_s, 3)), dma((n_s,))],
                     aliases={t: t for t in range(n)})(*bufs, *smalls)
    return list(res[:n]), list(res[n:])


def _halves_copy(grad_ref, land_ref, send_sems, recv_sems, t):
    x, y, c, _ = _place()
    half = grad_ref.shape[1] // 2
    return pltpu.make_async_remote_copy(
        src_ref=grad_ref.at[:, pl.ds((1 - c) * half, half), :], dst_ref=land_ref, send_sem=send_sems.at[t],
        recv_sem=recv_sems.at[t], device_id=(x, y, 1 - c), device_id_type=MESH)


def _exchange_start(name, grads, carry=()):
    n = len(grads)
    lands = [lax.empty((g.shape[0], g.shape[1] // 2, g.shape[2]), BF16) for g in grads]

    def body(*refs):
        send_sems, recv_sems = refs[2 * n + len(carry):2 * n + len(carry) + 2]
        for t in range(n):
            _halves_copy(refs[t], refs[n + t], send_sems, recv_sems, t).start()

    dma = pltpu.SemaphoreType.DMA
    thru = list(grads) + lands + list(carry)
    res = pl.pallas_call(
        body, name=name, in_specs=[HBM_SPEC] * len(thru), out_specs=[SEM_SPEC] * 2 + [HBM_SPEC] * len(thru),
        out_shape=[dma((n,)), dma((n,))] + [pltpu.HBM(a.shape, a.dtype) for a in thru],
        input_output_aliases={t: 2 + t for t in range(len(thru))}, compiler_params=SPLIT_PARAMS,
    )(*[_in_hbm(a) for a in thru])
    return (res[0], res[1]), list(res[2:2 + n]), list(res[2 + n:2 + 2 * n]), list(res[2 + 2 * n:])


def _exchange_wait(name, sems, grads, lands, after):
    n = len(grads)

    def body(*refs):
        send_sems, recv_sems = refs[2 * n:2 * n + 2]
        for t in range(n):
            copy = _halves_copy(refs[t], refs[n + t], send_sems, recv_sems, t)
            copy.wait_send()
            copy.wait_recv()

    res = pl.pallas_call(
        body, name=name, in_specs=[HBM_SPEC] * (2 * n) + [SEM_SPEC, SEM_SPEC, ANY], out_specs=[HBM_SPEC] * (2 * n),
        out_shape=[pltpu.HBM(a.shape, a.dtype) for a in grads + lands],
        input_output_aliases={t: t for t in range(2 * n)}, compiler_params=SPLIT_PARAMS)(*grads, *lands, *sems, after)
    return list(res[:n]), list(res[n:])


def _scatter_copy(part_ref, land_ref, send_sems, recv_sems, t, k, chip, c):
    return pltpu.make_async_remote_copy(
        src_ref=part_ref.at[_chip_index(chip)], dst_ref=land_ref.at[k], send_sem=send_sems.at[3 * t + k],
        recv_sem=recv_sems.at[3 * t + k], device_id=(*chip, c), device_id_type=MESH)


def _scatter_start(name, parts, carry=()):
    n, n_c = len(parts), len(carry)
    lands = [lax.empty((3,) + p.shape[1:], BF16) for p in parts]

    def body(*refs):
        p_in, l_in = refs[:n], refs[n:2 * n]
        send_sems, recv_sems = refs[2 * n + n_c:2 * n + n_c + 2]
        x, y, c, chips = _place()
        for t in range(n):
            for k, chip in enumerate(chips):
                _scatter_copy(p_in[t], l_in[t], send_sems, recv_sems, t, k, chip, c).start()

    dma = pltpu.SemaphoreType.DMA
    thru = list(parts) + lands + list(carry)
    res = pl.pallas_call(
        body, name=name, in_specs=[HBM_SPEC] * len(thru), out_specs=[SEM_SPEC] * 2 + [HBM_SPEC] * len(thru),
        out_shape=[dma((3 * n,)), dma((3 * n,))] + [pltpu.HBM(a.shape, a.dtype) for a in thru],
        input_output_aliases={t: 2 + t for t in range(len(thru))}, compiler_params=SPLIT_PARAMS,
    )(*[_in_hbm(a) for a in thru])
    return (res[0], res[1]), list(res[2:2 + n]), list(res[2 + n:2 + 2 * n]), list(res[2 + 2 * n:])


def _scatter_wait(name, sems, parts, lands, after):
    n = len(parts)

    def body(*refs):
        p_in, l_in = refs[:n], refs[n:2 * n]
        send_sems, recv_sems = refs[2 * n:2 * n + 2]
        x, y, c, chips = _place()
        for t in range(n):
            for k, chip in enumerate(chips):
                copy = _scatter_copy(p_in[t], l_in[t], send_sems, recv_sems, t, k, chip, c)
                copy.wait_send()
                copy.wait_recv()

    hbm_out = lambda a: pltpu.HBM(a.shape, a.dtype)
    res = pl.pallas_call(
        body, name=name, in_specs=[HBM_SPEC] * (2 * n) + [SEM_SPEC, SEM_SPEC, ANY], out_specs=[HBM_SPEC] * (2 * n),
        out_shape=[hbm_out(a) for a in parts + lands], input_output_aliases={t: t for t in range(2 * n)},
        compiler_params=SPLIT_PARAMS)(*parts, *lands, *sems, after)
    return list(res[:n]), list(res[n:])


def _share_items(fulls):
    return [(a, l) for a in range(len(fulls)) for l in range(fulls[a].shape[0])]


def _share_copy(ref, layer, send_sems, recv_sems, t):
    x, y, c, _ = _place()
    half = ref.shape[1] // 2
    return pltpu.make_async_remote_copy(
        src_ref=ref.at[layer, pl.ds(c * half, half), :], dst_ref=ref.at[layer, pl.ds((1 - c) * half, half), :],
        send_sem=send_sems.at[t], recv_sem=recv_sems.at[t], device_id=(x, y, 1 - c), device_id_type=MESH)


def _share_start(fulls):
    n, items = len(fulls), _share_items(fulls)

    def body(*refs):
        send_sems, recv_sems = refs[n:n + 2]
        x, y, c, _ = _place()
        for t, (a, l) in enumerate(items):
            half = refs[a].shape[1] // 2
            mine = refs[a].at[l, pl.ds(c * half, half), :]
            pltpu.make_async_remote_copy(src_ref=mine, dst_ref=mine, send_sem=send_sems.at[t], recv_sem=recv_sems.at[t],
                                         device_id=(x, y, 1 - c), device_id_type=MESH).start()

    dma = pltpu.SemaphoreType.DMA
    res = pl.pallas_call(
        body, name="grad_share_start", in_specs=[HBM_SPEC] * n, out_specs=[SEM_SPEC] * 2 + [HBM_SPEC] * n,
        out_shape=[dma((len(items),)), dma((len(items),))] + [pltpu.HBM(f.shape, f.dtype) for f in fulls],
        input_output_aliases={t: 2 + t for t in range(n)}, compiler_params=SPLIT_PARAMS,
    )(*[_in_hbm(f) for f in fulls])
    return (res[0], res[1]), list(res[2:])


def _share_wait(sems, fulls, after):
    n, items = len(fulls), _share_items(fulls)

    def body(*refs):
        send_sems, recv_sems = refs[n:n + 2]
        for t, (a, l) in enumerate(items):
            copy = _share_copy(refs[a], l, send_sems, recv_sems, t)
            copy.wait_send()
            copy.wait_recv()

    res = pl.pallas_call(
        body, name="grad_share_wait", in_specs=[HBM_SPEC] * n + [SEM_SPEC, SEM_SPEC, ANY], out_specs=[HBM_SPEC] * n,
        out_shape=[pltpu.HBM(f.shape, f.dtype) for f in fulls], input_output_aliases={t: t for t in range(n)},
        compiler_params=SPLIT_PARAMS)(*fulls, *sems, after)
    return list(res)


N_DEVICES = 8


def _device_index(x, y, c):
    return 4 * x + 2 * y + c


def _small_peers():
    x, y, c, _ = _place()
    flips = [(fx, fy, fc) for fx in (0, 1) for fy in (0, 1) for fc in (0, 1)][1:]
    return _device_index(x, y, c), [(x ^ fx, y ^ fy, c ^ fc) for fx, fy, fc in flips]


def _small_copy(p_ref, land_ref, send_sems, recv_sems, k, peer, slot):
    return pltpu.make_async_remote_copy(src_ref=p_ref, dst_ref=land_ref.at[slot], send_sem=send_sems.at[k],
                                        recv_sem=recv_sems.at[k], device_id=peer, device_id_type=MESH)


def _allgather_small_start(packed, carry=()):
    land = jnp.zeros((N_DEVICES,) + packed.shape, F32)

    def body(*refs):
        p_ref, l_ref = refs[:2]
        send_sems, recv_sems = refs[2 + len(carry):4 + len(carry)]
        me, peers = _small_peers()
        for k, peer in enumerate(peers):
            _small_copy(p_ref, l_ref, send_sems, recv_sems, k, peer, me).start()

    dma = pltpu.SemaphoreType.DMA
    thru = [packed, land] + list(carry)
    res = pl.pallas_call(
        body, name="small_allgather_start", in_specs=[HBM_SPEC] * len(thru),
        out_specs=[SEM_SPEC] * 2 + [HBM_SPEC] * len(thru),
        out_shape=[dma((N_DEVICES - 1,)), dma((N_DEVICES - 1,))] + [pltpu.HBM(a.shape, a.dtype) for a in thru],
        input_output_aliases={t: 2 + t for t in range(len(thru))}, compiler_params=SPLIT_PARAMS,
    )(*[_in_hbm(a) for a in thru])
    return (res[0], res[1]), res[2], res[3], list(res[4:])


def _allgather_small_wait(sems, packed, land, after):
    def body(p_ref, l_ref, send_sems, recv_sems, *_):
        _, peers = _small_peers()
        for k, peer in enumerate(peers):
            copy = _small_copy(p_ref, l_ref, send_sems, recv_sems, k, peer, _device_index(*peer))
            copy.wait_send()
            copy.wait_recv()

    res = pl.pallas_call(
        body, name="small_allgather_wait", in_specs=[HBM_SPEC] * 2 + [SEM_SPEC, SEM_SPEC, ANY],
        out_specs=[HBM_SPEC] * 2, out_shape=[pltpu.HBM(a.shape, a.dtype) for a in (packed, land)],
        input_output_aliases={0: 0, 1: 1}, compiler_params=SPLIT_PARAMS)(packed, land, *sems, after)
    return res[0], res[1]


PAD_ROWS = 8


def kernel(x, norm_a, w_in_a, w_grp_a, scale_a, w_out_a, norm_kv, w_k, w_v, norm_b, w_in_b, w_out_b, norm_f, loss_target, m_norm_a, m_w_in_a, m_w_grp_a, m_scale_a, m_w_out_a, m_norm_kv, m_w_k, m_w_v, m_norm_b, m_w_in_b, m_w_out_b, m_norm_f, v_norm_a, v_w_in_a, v_w_grp_a, v_scale_a, v_w_out_a, v_norm_kv, v_w_k, v_w_v, v_norm_b, v_w_in_b, v_w_out_b, v_norm_f):
    weights = dict(norm_a=norm_a, w_in_a=w_in_a, w_grp_a=w_grp_a, scale_a=scale_a, w_out_a=w_out_a, norm_kv=norm_kv,
                   w_k=w_k, w_v=w_v, norm_b=norm_b, w_in_b=w_in_b, w_out_b=w_out_b, norm_f=norm_f)
    moments_m = dict(norm_a=m_norm_a, w_in_a=m_w_in_a, w_grp_a=m_w_grp_a, scale_a=m_scale_a, w_out_a=m_w_out_a,
                     norm_kv=m_norm_kv, w_k=m_w_k, w_v=m_w_v, norm_b=m_norm_b, w_in_b=m_w_in_b, w_out_b=m_w_out_b,
                     norm_f=m_norm_f)
    moments_v = dict(norm_a=v_norm_a, w_in_a=v_w_in_a, w_grp_a=v_w_grp_a, scale_a=v_scale_a, w_out_a=v_w_out_a,
                     norm_kv=v_norm_kv, w_k=v_w_k, w_v=v_w_v, norm_b=v_norm_b, w_in_b=v_w_in_b, w_out_b=v_w_out_b,
                     norm_f=v_norm_f)
    names = list(weights)
    d = x.shape[-1]
    c_idx = lax.axis_index("c").astype(jnp.int32).reshape(1)
    s_me = 2 * lax.axis_index("x") + lax.axis_index("y")
    s_idx = s_me.astype(jnp.int32).reshape(1)

    def as_lbrc(name):
        a = weights[name]
        if name == "w_grp_a":
            return a
        if a.ndim == 2:
            return a.reshape(1, 1, *a.shape)
        return a.reshape(a.shape[0], 1, *a.shape[1:])

    n_a, n_b = norm_a.shape[0], norm_b.shape[0]
    group_weights = {**{f"a{i}": [("w_in_a", i), ("w_grp_a", i), ("w_out_a", i)] for i in range(n_a)},
                     "kv": [("w_k", 0), ("w_v", 0)],
                     **{f"b{i}": [("w_in_b", i), ("w_out_b", i)] for i in range(n_b)}}
    group_order = [f"a{i}" for i in range(n_a)] + ["kv"] + [f"b{i}" for i in range(n_b)]
    slots, slot_groups = [], []
    small_full, started = {}, {}

    def start_group(gi, carry=()):
        sems, bufs, carry = _gather_start(f"gather_start_{group_order[gi]}", [slots[t] for t in slot_groups[gi]], carry)
        started[gi] = (sems, bufs)
        return carry

    relayed = {}

    def relay(group, carry):
        sems, bufs = started[group_order.index(group)]
        fsems, bufs, carry = _gather_relay(f"gather_relay_{group}", sems, bufs, tuple(carry))
        relayed[group] = (fsems, bufs)
        return carry

    previous = None
    for gi, group in enumerate(group_order):
        slot_groups.append(list(range(len(slots), len(slots) + len(group_weights[group]))))
        for name, l in group_weights[group]:
            slots.append(_cast_into_slot(f"cast_{name}{l}", as_lbrc(name), l, s_idx, previous))
            previous = slots[-1] if gi > 0 else None
        if gi == 0:
            (previous,) = start_group(0, (norm_kv.reshape(1, -1),))

    def gathered_form(name, g):
        if name in ("w_in_a", "w_in_b"):
            return g[0]
        if name == "w_grp_a":
            return g.reshape(g.shape[0], -1, g.shape[-1])
        return g.reshape(-1, g.shape[-1])

    def fetch(group, after):
        if group == "head":
            return {"norm_f": (0, norm_f)}
        gi = group_order.index(group)
        sems, bufs = started[gi]
        if gi == 0:
            after = slots[-1]
        if group in relayed:
            bufs, small_g = _gather_relay_wait(f"gather_relay_wait_{group}", *relayed[group], after), []
        else:
            bufs = _gather_wait(f"gather_wait_{group}", sems, bufs, after)
            bufs, small_g = _gather_forward(f"gather_forward_{group}", bufs, [norm_a, scale_a] if gi == 0 else [])
        out = {name: (l, gathered_form(name, g)) for (name, l), g in zip(group_weights[group], bufs)}
        if gi == 0:
            for name, g in zip(("norm_a", "scale_a"), small_g):
                small_full[name] = g.transpose(1, 0, 2).reshape(g.shape[1], -1)
        layer = group_weights[group][0][1]
        if group.startswith("a"):
            gain_name, gain = "norm_a", small_full["norm_a"][layer]
            out.update(scale_a=(layer, small_full["scale_a"][layer]))
        elif group == "kv":
            gain_name, gain = "norm_kv", norm_kv
        else:
            gain_name, gain = "norm_b", norm_b[layer]
        gain = gain.reshape(1, -1)
        ahead = [gi + 1] + ([gi + 2] if gi + 2 < len(group_order) and group_order[gi + 1] == "kv" else [])
        for gj in ahead:
            if gj < len(group_order) and gj not in started:
                (gain,) = start_group(gj, (gain,))
        out[gain_name] = (layer, gain)
        return out

    exchanging, in_flight = {}, []

    def begin(group, grads_of, carry):
        keys = [(k, grads_of[k][0]) for k in grads_of]
        sems, grads, lands, carry = _exchange_start(f"grad_exchange_start_{group}", [grads_of[k][1] for k in grads_of],
                                                    tuple(carry))
        exchanging[group] = (keys, sems, grads, lands)
        return carry

    def emit(group, carry):
        keys, sems, grads, lands = exchanging.pop(group)
        grads, recv1 = _exchange_wait(f"grad_exchange_wait_{group}", sems, grads, lands, after=carry[0])
        parts = [_pair_add(f"pair_add_{k}{l}", g, r, c_idx) for (k, l), g, r in zip(keys, grads, recv1)]
        sems, parts, lands, carry = _scatter_start(f"grad_scatter_start_{group}", parts, tuple(carry))
        in_flight.append((group, keys, sems, parts, lands))
        return carry

    loss_vec, grad_x, small = _local_step(x[0], loss_target[0], n_a, n_b, fetch, begin, emit, relay)

    small_order = [("norm_a", i) for i in range(n_a)] + [("scale_a", i) for i in range(n_a)] + [("norm_kv", 0)] + \
                  [("norm_b", i) for i in range(norm_b.shape[0])] + [("norm_f", 0)]
    pad = lambda vec: jnp.pad(vec, ((0, PAD_ROWS - 1), (0, 0)))
    packed = jnp.concatenate([pad(loss_vec)] + [pad(small[n][i]) for n, i in small_order], axis=0)
    small_sems, packed, small_land, _ = _allgather_small_start(packed)

    sc_idx = jnp.concatenate([s_idx, c_idx])
    fulls = {name: None for name in BIG_WEIGHTS}
    for group, keys, sems, parts, lands in in_flight:
        parts, lands = _scatter_wait(f"grad_scatter_wait_{group}", sems, parts, lands, after=packed)
        for (name, i), p, r in zip(keys, parts, lands):
            n_layers = 1 if weights[name].ndim == 2 else weights[name].shape[0]
            fulls[name] = _final_add(f"final_add_{name}{i}", p, r, sc_idx, i, n_layers, into=fulls[name])
    share_sems, sharing = _share_start([fulls[name] for name in BIG_WEIGHTS])

    deltas, new_m, new_v, grads = {}, {}, {}, {}
    as_full = lambda n, full: (a.reshape(full.shape) for a in (weights[n], moments_m[n], moments_v[n]))
    mine = {}
    for n, full in zip(BIG_WEIGHTS, sharing):
        w3, m3, v3 = as_full(n, full)
        mine[n] = _adamw_half(f"adamw_mine_{n}", w3, full, m3, v3, c_idx)
    shared = _share_wait(share_sems, sharing, after=mine[BIG_WEIGHTS[-1]][2])
    for n, full in zip(BIG_WEIGHTS, shared):
        w3, m3, v3 = as_full(n, full)
        shape = weights[n].shape
        dl, mn, vn = _adamw_half(f"adamw_theirs_{n}", w3, full, m3, v3, 1 - c_idx, into=mine[n])
        deltas[n], new_m[n], new_v[n], grads[n] = dl.reshape(shape), mn.reshape(shape), vn.reshape(shape), full.reshape(shape)

    def update(n):
        shape = weights[n].shape
        as2d = (lambda a: a.reshape(1, -1)) if len(shape) == 1 else (lambda a: a)
        dl, mn, vn = _adamw(f"adamw_{n}", as2d(weights[n]), as2d(grads[n]), as2d(moments_m[n]), as2d(moments_v[n]))
        deltas[n], new_m[n], new_v[n] = dl.reshape(shape), mn.reshape(shape), vn.reshape(shape)

    packed, small_land = _allgather_small_wait(small_sems, packed, small_land, after=new_v[BIG_WEIGHTS[-1]])
    me_idx = _device_index(lax.axis_index("x"), lax.axis_index("y"), lax.axis_index("c")).astype(jnp.int32).reshape(1)
    totals = _sum_devices("small_sum", small_land, packed, me_idx)
    loss = 0.5 * jnp.sum(totals[0]) / d
    small_tot = {}
    for j, (n, i) in enumerate(small_order):
        small_tot.setdefault(n, []).append(totals[PAD_ROWS * (j + 1)])
    shard_w = norm_a.shape[1]
    for n in ("norm_a", "scale_a"):
        full = jnp.stack(small_tot[n])
        grads[n] = lax.dynamic_slice_in_dim(full, s_me * shard_w, shard_w, axis=1)
    grads["norm_kv"] = small_tot["norm_kv"][0]
    grads["norm_b"] = jnp.stack(small_tot["norm_b"])
    grads["norm_f"] = small_tot["norm_f"][0]
    for n in names:
        if n not in BIG_WEIGHTS:
            update(n)

    return (loss, grad_x[None], *[grads[n] for n in names], *[deltas[n] for n in names],
            *[new_m[n] for n in names], *[new_v[n] for n in names])
```

```python
import functools
import math

import jax
import jax.numpy as jnp
from jax import lax
from jax.experimental import pallas as pl
from jax.experimental.pallas import tpu as pltpu

F32 = jnp.float32
BF16 = jnp.bfloat16

HEAD_DIM = 128
POOL_WINDOWS = (2, 4, 8, 16)
DILATED_PAIRS = ((128, 1), (512, 4), (2048, 16))
ROPE_THETA = 10000.0
RMS_EPS = 1e-6
NEG_INF = -1e30
N_CHIPS = 4

ADAM_LR = 0.001
ADAM_B1 = 0.9
ADAM_B2 = 0.999
ADAM_EPS = 1e-08
ADAM_WD = 0.01
ADAM_STEP = 10

VMEM_LIMIT_BYTES = 56 * 1024 * 1024
MESH = pl.DeviceIdType.MESH
ANY = pl.BlockSpec(memory_space=pl.ANY)


def _tile(n, pref):
    t = min(n, pref)
    assert n % t == 0, (n, pref)
    return t


def _params(sem=None):
    return pltpu.CompilerParams(dimension_semantics=sem, vmem_limit_bytes=VMEM_LIMIT_BYTES)


def _mm(name, a, b, *, grid2, nk, a_blk, a_map, b_blk, b_map, outs, dims, epi=None, epi_in=(), epi_specs=(),
        acc_shape=None, epi_scratch=(), into=None):
    n_epi, n_out = len(epi_in), len(outs)

    def body(*refs):
        a_ref, b_ref = refs[0], refs[1]
        e_refs = refs[2:2 + n_epi]
        first_out = 2 + n_epi + (0 if into is None else 1)
        o_refs = refs[first_out:first_out + n_out]
        s_refs = refs[first_out + n_out + (0 if nk == 1 else 1):]

        def contrib():
            a_val = a_ref[...]
            if a_val.ndim == 3:
                a_val = a_val.reshape(-1, a_val.shape[-1])
            return lax.dot_general(a_val, b_ref[...], (dims, ((), ())), preferred_element_type=F32)

        def finish(acc):
            if epi is None:
                o_refs[0][...] = acc.reshape(o_refs[0].shape).astype(o_refs[0].dtype)
            else:
                epi(acc, e_refs, o_refs, s_refs)

        if nk == 1:
            finish(contrib())
        else:
            acc_ref = refs[first_out + n_out]
            k = pl.program_id(2)

            @pl.when(k == 0)
            def _():
                acc_ref[...] = contrib()

            @pl.when(k > 0)
            def _():
                acc_ref[...] += contrib()

            @pl.when(k == nk - 1)
            def _():
                finish(acc_ref[...])

    scratch = ([] if nk == 1 else [pltpu.VMEM(acc_shape, F32)]) + list(epi_scratch)
    extra_in, extra_specs, aliases = (), (), {}
    if into is not None:
        extra_in, extra_specs, aliases = (into[0],), (ANY,), {2 + n_epi: into[1]}
    res = pl.pallas_call(
        body, name=name, grid=(grid2[0], grid2[1], nk),
        in_specs=[pl.BlockSpec(a_blk, a_map), pl.BlockSpec(b_blk, b_map), *epi_specs, *extra_specs],
        out_specs=[pl.BlockSpec(blk, imap) for _, blk, imap, _ in outs],
        out_shape=[jax.ShapeDtypeStruct(shape, dtype) for shape, _, _, dtype in outs],
        scratch_shapes=scratch, input_output_aliases=aliases,
        compiler_params=_params(("parallel", "parallel", "arbitrary")),
    )(a, b, *epi_in, *extra_in)
    return res[0] if n_out == 1 else tuple(res)


NN = ((1,), (0,))
NT = ((1,), (1,))
TN = ((0,), (0,))


def _rope_apply(t, cos, sin):
    return t * cos + pltpu.roll(t, HEAD_DIM // 2, 1) * sin


def _epi_add(acc, e_refs, o_refs, s_refs):
    o_refs[0][...] = (acc + e_refs[0][...]).astype(o_refs[0].dtype)


def _col_blocks(width):
    return [slice(c * HEAD_DIM, (c + 1) * HEAD_DIM) for c in range(width // HEAD_DIM)]


def _col_scratch(rows, width):
    return pltpu.VMEM((width // HEAD_DIM, rows, HEAD_DIM), F32)


def _to_residue_major(o_ref, scr, d, sl):
    if d == 1:
        o_ref[0, :, sl] = scr[...].astype(o_ref.dtype)
        return
    rows = scr.shape[0] // d
    for r in range(d):
        o_ref[r, :, sl] = scr[pl.ds(r, rows, stride=d), :].astype(o_ref.dtype)


def _from_residue_major(i_ref, scr, d, sl):
    if d == 1:
        return i_ref[0, :, sl].astype(F32)
    rows = i_ref.shape[1]
    for r in range(d):
        scr[pl.ds(r, rows, stride=d), :] = i_ref[r, :, sl].astype(F32)
    return scr[...]


def _make_epi_orders(dils, rope_scale):
    def epi(acc, e_refs, o_refs, s_refs):
        if rope_scale is not None:
            cos = e_refs[0][...]
            sin = e_refs[1][...]
        for c, sl in enumerate(_col_blocks(acc.shape[1])):
            scr = s_refs[0].at[c]
            scr[...] = acc[:, sl] if rope_scale is None else _rope_apply(acc[:, sl], cos, sin) * rope_scale
            for o_ref, d in zip(o_refs, dils):
                _to_residue_major(o_ref, scr, d, sl)
    return epi


def _make_epi_token_order(d, has_add):
    def epi(acc, e_refs, o_refs, s_refs):
        o_ref = o_refs[0]
        if d == 1:
            o_ref[...] = acc + e_refs[0][...] if has_add else acc
            return
        rows = acc.shape[0] // d
        for c, sl in enumerate(_col_blocks(acc.shape[1])):
            scr = s_refs[0].at[c]
            for r in range(d):
                scr[pl.ds(r, rows, stride=d), :] = acc[r * rows:(r + 1) * rows, sl]
            o_ref[:, sl] = scr[...] + e_refs[0][:, sl] if has_add else scr[...]
    return epi


def _mm_act_w(name, a, w, *, out_dtype=BF16, add=None, rope=None, n_first=0, n_cols=None, dils=None):
    s_len, k_len = a.shape
    bm = _tile(s_len, 1024)
    epi, epi_in, epi_specs, epi_scratch = None, (), (), ()
    if w.ndim == 3:
        ns, _, c = w.shape
        ns_used = ns if n_cols is None else n_cols
        bn = _tile(c, 1024)
        sub = c // bn
        grid2 = (ns_used * sub, s_len // bm)
        b_blk, b_map = (None, k_len, bn), (lambda j, i, k: (j // sub + n_first, 0, j % sub))
        n_len = ns_used * c
    else:
        n_len = w.shape[1]
        bn = _tile(n_len, 1024)
        grid2 = (n_len // bn, s_len // bm)
        b_blk, b_map = (k_len, bn), (lambda j, i, k: (0, j))
    if add is not None:
        epi, epi_in = _epi_add, (add,)
        epi_specs = (pl.BlockSpec((bm, bn), lambda j, i, k: (i, j)),)
    outs = [((s_len, n_len), (bm, bn), lambda j, i, k: (i, j), out_dtype)]
    if dils is not None:
        if rope is not None:
            epi_in = rope[:2]
            epi_specs = (pl.BlockSpec((bm, HEAD_DIM), lambda j, i, k: (i, 0)),) * 2
        epi = _make_epi_orders(dils, None if rope is None else rope[2])
        epi_scratch = (_col_scratch(bm, bn),)
        outs = [((d, s_len // d, n_len), (d, bm // d, bn), lambda j, i, k: (0, i, j), BF16) for d in dils]
    res = _mm(name, a, w, grid2=grid2, nk=1, a_blk=(bm, k_len), a_map=lambda j, i, k: (i, 0),
              b_blk=b_blk, b_map=b_map, outs=outs, dims=NN, epi=epi, epi_in=epi_in, epi_specs=epi_specs,
              epi_scratch=epi_scratch)
    return (res,) if dils is not None and len(dils) == 1 else res


def _mm_grad_act(name, dy, w, *, add=None, slot=None):
    if slot is not None:
        d = 1 if dy.ndim == 2 else dy.shape[0]
        s_len = dy.shape[-2] * d
        _, k_len, c = w.shape
        bm, bn = _tile(s_len, 1024), _tile(k_len, 1024)
        a_blk, a_map = ((bm, c), lambda j, i, k: (i, 0)) if dy.ndim == 2 else ((d, bm // d, c), lambda j, i, k: (0, i, 0))
        epi_in = () if add is None else (add,)
        return _mm(name, dy, w, grid2=(k_len // bn, s_len // bm), nk=1, a_blk=a_blk, a_map=a_map,
                   b_blk=(None, bn, c), b_map=lambda j, i, k: (slot, j, 0),
                   outs=[((s_len, k_len), (bm, bn), lambda j, i, k: (i, j), F32)], dims=NT,
                   epi=_make_epi_token_order(d, add is not None), epi_in=epi_in,
                   epi_specs=(pl.BlockSpec((bm, bn), lambda j, i, k: (i, j)),) * len(epi_in),
                   epi_scratch=(_col_scratch(bm, bn),) if d > 1 else ())
    s_len, n_len = dy.shape
    bm = _tile(s_len, 1024)
    if w.ndim == 3:
        ns, k_len, c = w.shape
        bk, nk = c, ns
        bn = _tile(k_len, 1024)
        b_blk, b_map = (None, bn, c), (lambda j, i, k: (k, j, 0))
    else:
        k_len = w.shape[0]
        bk = _tile(n_len, 2048)
        nk = n_len // bk
        bn = _tile(k_len, 1024)
        b_blk, b_map = (bn, bk), (lambda j, i, k: (j, k))
    epi, epi_in, epi_specs = None, (), ()
    if add is not None:
        epi, epi_in = _epi_add, (add,)
        epi_specs = (pl.BlockSpec((bm, bn), lambda j, i, k: (i, j)),)
    return _mm(name, dy, w, grid2=(k_len // bn, s_len // bm), nk=nk, a_blk=(bm, bk), a_map=lambda j, i, k: (i, k),
               b_blk=b_blk, b_map=b_map, outs=[((s_len, k_len), (bm, bn), lambda j, i, k: (i, j), F32)],
               dims=NT, epi=epi, epi_in=epi_in, epi_specs=epi_specs, acc_shape=(bm, bn))


def _mm_grad_w(name, a, dy, *, col_shards=None, slot=None, into=None):
    s_len, k_len = a.shape
    n_len = dy.shape[1]
    bk = _tile(s_len, 2048)
    bm = _tile(k_len, 1024)
    if slot is not None:
        bn = _tile(n_len, 1024)
        out = ((col_shards, k_len, n_len), (None, bm, bn), lambda j, i, k: (slot, i, j), BF16)
    elif col_shards:
        c = n_len // col_shards
        bn = _tile(c, 1024)
        sub = c // bn
        out = ((col_shards, k_len, c), (None, bm, bn), lambda j, i, k: (j // sub, i, j % sub), BF16)
    else:
        bn = _tile(n_len, 1024)
        out = ((k_len, n_len), (bm, bn), lambda j, i, k: (i, j), BF16)
    return _mm(name, a, dy, grid2=(n_len // bn, k_len // bm), nk=s_len // bk,
               a_blk=(bk, bm), a_map=lambda j, i, k: (k, i), b_blk=(bk, bn), b_map=lambda j, i, k: (k, j),
               outs=[out], dims=TN, acc_shape=(bm, bn), into=None if into is None else (into, 0))


def _mm_grp_fwd(name, pooled, wg):
    s_len, e = pooled.shape
    ng, g, _ = wg.shape
    bm = _tile(s_len, 1024)
    return _mm(name, pooled, wg, grid2=(ng, s_len // bm), nk=1, a_blk=(bm, g), a_map=lambda j, i, k: (i, j),
               b_blk=(None, g, g), b_map=lambda j, i, k: (j, 0, 0),
               outs=[((s_len, e), (bm, g), lambda j, i, k: (i, j), F32)], dims=NN)


def _mm_grp_grad_act(name, dy, wg):
    s_len, e = dy.shape
    ng, g, _ = wg.shape
    bm = _tile(s_len, 1024)
    return _mm(name, dy, wg, grid2=(ng, s_len // bm), nk=1, a_blk=(bm, g), a_map=lambda j, i, k: (i, j),
               b_blk=(None, g, g), b_map=lambda j, i, k: (j, 0, 0),
               outs=[((s_len, e), (bm, g), lambda j, i, k: (i, j), F32)], dims=NT)


def _mm_grp_grad_w(name, pooled, dy, ng):
    s_len, e = pooled.shape
    g = e // ng
    bk = _tile(s_len, 1024)
    return _mm(name, pooled, dy, grid2=(ng, 1), nk=s_len // bk, a_blk=(bk, g), a_map=lambda j, i, k: (k, j),
               b_blk=(bk, g), b_map=lambda j, i, k: (k, j),
               outs=[((N_CHIPS, ng, g // N_CHIPS, g), (N_CHIPS, None, g // N_CHIPS, g), lambda j, i, k: (0, j, 0, 0), BF16)],
               dims=TN, acc_shape=(g, g))


def _row_spec(bs, width, col=0):
    return pl.BlockSpec((bs, width), lambda i: (i, col))


def _vec_spec(width):
    return pl.BlockSpec((1, width), lambda i: (0, 0))


def _rows_call(body, name, s_len, in_specs, out_specs, out_shape, bs, aliases=None, sequential=False):
    return pl.pallas_call(
        body, name=name, grid=(s_len // bs,), in_specs=in_specs, out_specs=out_specs, out_shape=out_shape,
        input_output_aliases=aliases or {},
        compiler_params=_params(("arbitrary",) if sequential else ("parallel",)))


def _accumulate(ref, part):
    i = pl.program_id(0)

    @pl.when(i == 0)
    def _():
        ref[...] = part

    @pl.when(i > 0)
    def _():
        ref[...] += part


def _rms_scale(xf):
    return lax.rsqrt(jnp.mean(xf * xf, axis=-1, keepdims=True) + RMS_EPS)


def _res_spec(dil, bs, width):
    return pl.BlockSpec((dil, bs // dil, width), lambda i: (0, i, 0))


def _res_shape(dil, s_len, width, dtype):
    return jax.ShapeDtypeStruct((dil, s_len // dil, width), dtype)


def _rmsnorm_fwd(name, x, gain, dils=()):
    s_len, d = x.shape
    bs = _tile(s_len, 256)

    def body(x_ref, g_ref, h_ref, *rest):
        xf = x_ref[...]
        h = (xf * _rms_scale(xf)) * g_ref[...]
        h_ref[...] = h.astype(BF16)
        if dils:
            for c, sl in enumerate(_col_blocks(d)):
                scr = rest[-1].at[c]
                scr[...] = h[:, sl]
                for o_ref, dil in zip(rest[:-1], dils):
                    _to_residue_major(o_ref, scr, dil, sl)

    res = pl.pallas_call(
        body, name=name, grid=(s_len // bs,), in_specs=[_row_spec(bs, d), _vec_spec(d)],
        out_specs=[_row_spec(bs, d)] + [_res_spec(dil, bs, d) for dil in dils],
        out_shape=[jax.ShapeDtypeStruct((s_len, d), BF16)] + [_res_shape(dil, s_len, d, BF16) for dil in dils],
        scratch_shapes=[_col_scratch(bs, d)] if dils else [],
        compiler_params=_params(("parallel",)))(x, gain)
    return res[0] if not dils else tuple(res)


def _rmsnorm_bwd(name, x, gain, dh, dres):
    s_len, d = x.shape
    bs = _tile(s_len, 256)

    def body(x_ref, g_ref, dh_ref, dres_ref, dx_ref, dxb_ref, dg_ref):
        xf = x_ref[...]
        r = _rms_scale(xf)
        xh = xf * r
        dh_f = dh_ref[...]
        t = dh_f * g_ref[...]
        dx = dres_ref[...] + r * (t - xh * jnp.mean(t * xh, axis=-1, keepdims=True))
        dx_ref[...] = dx
        dxb_ref[...] = dx.astype(BF16)
        _accumulate(dg_ref, jnp.sum(dh_f * xh, axis=0, keepdims=True))

    return _rows_call(
        body, name, s_len,
        [_row_spec(bs, d), _vec_spec(d), _row_spec(bs, d), _row_spec(bs, d)],
        [_row_spec(bs, d), _row_spec(bs, d), _vec_spec(d)],
        [jax.ShapeDtypeStruct((s_len, d), F32), jax.ShapeDtypeStruct((s_len, d), BF16),
         jax.ShapeDtypeStruct((1, d), F32)], bs, sequential=True)(x, gain, dh, dres)


def _loss_head(name, x, gain, target):
    s_len, d = x.shape
    bs = _tile(s_len, 256)

    def body(x_ref, g_ref, t_ref, lv_ref, dx_ref, dxb_ref, dg_ref):
        xf = x_ref[...]
        r = _rms_scale(xf)
        xh = xf * r
        err = xh * g_ref[...] - t_ref[...]
        dy = err * (1.0 / d)
        t = dy * g_ref[...]
        dx = r * (t - xh * jnp.mean(t * xh, axis=-1, keepdims=True))
        dx_ref[...] = dx
        dxb_ref[...] = dx.astype(BF16)
        _accumulate(lv_ref, jnp.sum(err * err, axis=0, keepdims=True))
        _accumulate(dg_ref, jnp.sum(dy * xh, axis=0, keepdims=True))

    return _rows_call(
        body, name, s_len, [_row_spec(bs, d), _vec_spec(d), _row_spec(bs, d)],
        [_vec_spec(d), _row_spec(bs, d), _row_spec(bs, d), _vec_spec(d)],
        [jax.ShapeDtypeStruct((1, d), F32), jax.ShapeDtypeStruct((s_len, d), F32),
         jax.ShapeDtypeStruct((s_len, d), BF16), jax.ShapeDtypeStruct((1, d), F32)],
        bs, sequential=True)(x, gain, target)


def _sigmoid(g):
    return 1.0 / (1.0 + jnp.exp(-g))


def _gate_a_fwd(name, ypre, proj, scale):
    s_len, e = ypre.shape
    bs = _tile(s_len, 256)

    def body(y_ref, g_ref, sc_ref, z_ref):
        g = g_ref[...]
        z_ref[...] = (y_ref[...] * sc_ref[...] * (g * _sigmoid(g))).astype(BF16)

    return _rows_call(body, name, s_len, [_row_spec(bs, e), _row_spec(bs, e, 1), _vec_spec(e)], _row_spec(bs, e),
                      jax.ShapeDtypeStruct((s_len, e), BF16), bs)(ypre, proj, scale)


def _gate_a_bwd(name, dz, ypre, proj, scale):
    s_len, e = ypre.shape
    bs = _tile(s_len, 256)

    def body(dz_ref, y_ref, g_ref, sc_ref, dy_ref, dproj_ref, dsc_ref):
        g = g_ref[...]
        sg = _sigmoid(g)
        silu = g * sg
        dz_f = dz_ref[...]
        ypre_f = y_ref[...]
        dys = dz_f * silu
        dy_ref[...] = (dys * sc_ref[...]).astype(BF16)
        dproj_ref[...] = (dz_f * (ypre_f * sc_ref[...]) * (sg * (1.0 + g * (1.0 - sg)))).astype(BF16)
        _accumulate(dsc_ref, jnp.sum(dys * ypre_f, axis=0, keepdims=True))

    return _rows_call(
        body, name, s_len, [_row_spec(bs, e), _row_spec(bs, e), _row_spec(bs, e, 1), _vec_spec(e)],
        [_row_spec(bs, e), _row_spec(bs, e, 1), _vec_spec(e)],
        [jax.ShapeDtypeStruct((s_len, e), BF16), jax.ShapeDtypeStruct((s_len, 2 * e), BF16),
         jax.ShapeDtypeStruct((1, e), F32)], bs, sequential=True)(dz, ypre, proj, scale)


def _merge_gate_fwd(name, outs, lses, gate, dils):
    s_len, e = gate.shape
    bs = _tile(s_len, 256)
    n = len(outs)

    def body(*refs):
        o_refs, l_refs, g_ref = refs[:n], refs[n:2 * n], refs[2 * n]
        m_ref, lj_ref, z_ref = refs[2 * n + 1:2 * n + 4]
        scratch = refs[2 * n + 4]
        for c, sl in enumerate(_col_blocks(e)):
            ls = [_from_residue_major(r, scratch.at[2 * j, c], dil, sl) for j, (r, dil) in enumerate(zip(l_refs, dils))]
            os_ = [_from_residue_major(r, scratch.at[2 * j + 1, c], dil, sl) for j, (r, dil) in enumerate(zip(o_refs, dils))]
            mx = functools.reduce(jnp.maximum, ls)
            ws = [jnp.exp(l - mx) for l in ls]
            den = functools.reduce(lambda a, b: a + b, ws)
            merged = functools.reduce(lambda a, b: a + b, [w * o for w, o in zip(ws, os_)]) / den
            g = g_ref[:, sl]
            m_ref[:, sl] = merged.astype(BF16)
            lj_ref[:, sl] = mx + jnp.log(den)
            z_ref[:, sl] = (merged * (g * _sigmoid(g))).astype(BF16)

    spec = _row_spec(bs, e)
    res_specs = [_res_spec(dil, bs, e) for dil in dils]
    return pl.pallas_call(
        body, name=name, grid=(s_len // bs,), in_specs=res_specs + res_specs + [spec], out_specs=[spec] * 3,
        out_shape=[jax.ShapeDtypeStruct((s_len, e), BF16), jax.ShapeDtypeStruct((s_len, e), F32),
                   jax.ShapeDtypeStruct((s_len, e), BF16)],
        scratch_shapes=[pltpu.VMEM((2 * n, e // HEAD_DIM, bs, HEAD_DIM), F32)],
        compiler_params=_params(("parallel",)))(*outs, *lses, gate)


def _gate_b_bwd(name, dz, merged, gate, lse, dils):
    s_len, e = gate.shape
    bs = _tile(s_len, 256)
    n = len(dils)
    plan = [(j, t) for j, dil in enumerate(dils) for t in range(3) if not (t == 1 and dil == 1)]

    def body(dz_ref, m_ref, g_ref, l_ref, dg_ref, *rest):
        out_refs, scratch = rest[:len(plan)], rest[len(plan)]
        for c, sl in enumerate(_col_blocks(e)):
            g = g_ref[:, sl]
            sg = _sigmoid(g)
            dz_f = dz_ref[:, sl]
            merged = m_ref[:, sl].astype(F32)
            dmerged = dz_f * (g * sg)
            dg_ref[:, sl] = (dz_f * merged * (sg * (1.0 + g * (1.0 - sg)))).astype(BF16)
            values = (dmerged, l_ref[:, sl],
                      jnp.broadcast_to(jnp.sum(dmerged * merged, axis=-1, keepdims=True), (bs, HEAD_DIM)))
            for t, val in enumerate(values):
                scratch[t, c] = val
            for o_ref, (j, t) in zip(out_refs, plan):
                _to_residue_major(o_ref, scratch.at[t, c], dils[j], sl)

    spec = _row_spec(bs, e)
    out_specs, out_shape = [spec], [jax.ShapeDtypeStruct((s_len, e), BF16)]
    for j, t in plan:
        out_specs.append(_res_spec(dils[j], bs, e))
        out_shape.append(_res_shape(dils[j], s_len, e, BF16 if t == 0 else F32))
    res = pl.pallas_call(
        body, name=name, grid=(s_len // bs,), in_specs=[spec] * 4, out_specs=out_specs, out_shape=out_shape,
        scratch_shapes=[pltpu.VMEM((3, e // HEAD_DIM, bs, HEAD_DIM), F32)],
        compiler_params=_params(("parallel",)))(dz, merged, gate, lse)
    made = {jt: arr for jt, arr in zip(plan, res[1:])}
    return res[0], [tuple(made.get((j, t), lse.reshape(1, s_len, e)) for t in range(3)) for j in range(n)]


def _kv_grad_prep(name, dk_accs, dv_accs, dils, cos, sin_inv):
    n = len(dils)
    e = dk_accs[0].shape[-1]
    s_len = dk_accs[0].shape[0] * dk_accs[0].shape[1]
    bs = _tile(s_len, 256)

    def body(*refs):
        dk_refs, dv_refs = refs[:n], refs[n:2 * n]
        c_ref, s_ref, dkb_ref, dvb_ref, scratch = refs[2 * n:]
        cos_t, sin_t = c_ref[...], s_ref[...]
        add = lambda a, b: a + b
        for c, sl in enumerate(_col_blocks(e)):
            dk = functools.reduce(add, [_from_residue_major(r, scratch.at[j, c], dil, sl)
                                        for j, (r, dil) in enumerate(zip(dk_refs, dils))])
            dkb_ref[:, sl] = _rope_apply(dk, cos_t, sin_t).astype(BF16)
            dv = functools.reduce(add, [_from_residue_major(r, scratch.at[n + j, c], dil, sl)
                                        for j, (r, dil) in enumerate(zip(dv_refs, dils))])
            dvb_ref[:, sl] = dv.astype(BF16)

    spec, rspec = _row_spec(bs, e), _row_spec(bs, HEAD_DIM)
    res_specs = [_res_spec(dil, bs, e) for dil in dils]
    return pl.pallas_call(
        body, name=name, grid=(s_len // bs,), in_specs=res_specs + res_specs + [rspec, rspec], out_specs=[spec, spec],
        out_shape=[jax.ShapeDtypeStruct((s_len, e), BF16)] * 2,
        scratch_shapes=[pltpu.VMEM((2 * n, e // HEAD_DIM, bs, HEAD_DIM), F32)],
        compiler_params=_params(("parallel",)))(*dk_accs, *dv_accs, cos, sin_inv)


def _pool_cols(e):
    return _tile(e // len(POOL_WINDOWS), 256)


def _window_sum(val, grp, s_len, forward):
    rows = lax.broadcasted_iota(jnp.int32, val.shape, 0)
    acc = val
    for level in range(len(POOL_WINDOWS)):
        step = 1 << level
        if forward:
            shifted = jnp.where(rows >= step, pltpu.roll(acc, step, 0), 0.0)
        else:
            shifted = jnp.where(rows < s_len - step, pltpu.roll(acc, s_len - step, 0), 0.0)
        acc = jnp.where(level <= grp, acc + shifted, acc)
    return acc


def _window_count(shape, grp):
    rows = lax.broadcasted_iota(jnp.int32, shape, 0)
    return jnp.minimum(rows + 1, jnp.left_shift(2, grp)).astype(F32)


def _pool_fwd(name, proj):
    s_len, e2 = proj.shape
    e = e2 // 2
    cb = _pool_cols(e)
    per_grp = e // len(POOL_WINDOWS) // cb
    assert POOL_WINDOWS == tuple(2 << g for g in range(len(POOL_WINDOWS)))

    def body(u_ref, p_ref):
        grp = pl.program_id(0)
        u = u_ref[...]
        total = _window_sum(u, grp, s_len, True)
        p_ref[...] = (total / _window_count(u.shape, grp) - u).astype(BF16)

    spec = pl.BlockSpec((s_len, cb), lambda g, c: (0, g * per_grp + c))
    return pl.pallas_call(
        body, name=name, grid=(len(POOL_WINDOWS), per_grp), in_specs=[spec], out_specs=spec,
        out_shape=jax.ShapeDtypeStruct((s_len, e), BF16), compiler_params=_params(("parallel", "parallel")))(proj)


def _pool_bwd(name, dpooled, dproj):
    s_len, e = dpooled.shape
    cb = _pool_cols(e)
    per_grp = e // len(POOL_WINDOWS) // cb

    def body(dp_ref, _, du_ref):
        grp = pl.program_id(0)
        dp = dp_ref[...]
        total = _window_sum(dp / _window_count(dp.shape, grp), grp, s_len, False)
        du_ref[...] = (total - dp).astype(BF16)

    spec = pl.BlockSpec((s_len, cb), lambda g, c: (0, g * per_grp + c))
    return pl.pallas_call(
        body, name=name, grid=(len(POOL_WINDOWS), per_grp), in_specs=[spec, ANY], out_specs=spec,
        out_shape=jax.ShapeDtypeStruct(dproj.shape, BF16), input_output_aliases={1: 0},
        compiler_params=_params(("parallel", "parallel")))(dpooled, dproj)


def _band_masks(nb, first):
    row = lax.broadcasted_iota(jnp.int32, (nb, nb), 0)
    col = lax.broadcasted_iota(jnp.int32, (nb, nb), 1)
    return col >= row + jnp.where(first, 2 * nb, 0), col <= row


def _dot(a, b, dims):
    return lax.dot_general(a, b, (dims, ((), ())), preferred_element_type=F32)


def _attn_fwd(name, window, q, k, v):
    dil, m, e = k.shape
    nb = window // dil
    nblk = m // nb
    heads = e // HEAD_DIM

    def body(q_ref, kc_ref, vc_ref, o_ref, l_ref, kp_ref, vp_ref):
        first = pl.program_id(1) == 0

        @pl.when(first)
        def _():
            kp_ref[...] = jnp.zeros_like(kp_ref)
            vp_ref[...] = jnp.zeros_like(vp_ref)

        mask_p, mask_c = _band_masks(nb, first)
        cols = _col_blocks(e)
        s_p = [jnp.where(mask_p, _dot(q_ref[:, sl], kp_ref[:, sl], NT), NEG_INF) for sl in cols]
        s_c = [jnp.where(mask_c, _dot(q_ref[:, sl], kc_ref[:, sl], NT), NEG_INF) for sl in cols]
        mx = [jnp.maximum(jnp.max(a, axis=-1, keepdims=True), jnp.max(b, axis=-1, keepdims=True))
              for a, b in zip(s_p, s_c)]
        p_p = [jnp.exp(a - m) for a, m in zip(s_p, mx)]
        p_c = [jnp.exp(a - m) for a, m in zip(s_c, mx)]
        den = [jnp.sum(a, axis=-1, keepdims=True) + jnp.sum(b, axis=-1, keepdims=True) for a, b in zip(p_p, p_c)]
        for h, sl in enumerate(cols):
            out = _dot(p_p[h].astype(BF16), vp_ref[:, sl], NN) + _dot(p_c[h].astype(BF16), vc_ref[:, sl], NN)
            o_ref[:, sl] = (out / den[h]).astype(BF16)
            l_ref[:, sl] = jnp.broadcast_to(mx[h] + jnp.log(den[h]), (nb, HEAD_DIM))
        kp_ref[...] = kc_ref[...]
        vp_ref[...] = vc_ref[...]

    blk = (None, nb, e)
    cur = lambda r, n: (r, n, 0)
    return pl.pallas_call(
        body, name=name, grid=(dil, nblk),
        in_specs=[pl.BlockSpec(blk, cur)] * 3,
        out_specs=[pl.BlockSpec(blk, cur), pl.BlockSpec(blk, cur)],
        out_shape=[jax.ShapeDtypeStruct((dil, m, e), BF16), jax.ShapeDtypeStruct((dil, m, e), F32)],
        scratch_shapes=[pltpu.VMEM((nb, e), BF16), pltpu.VMEM((nb, e), BF16)],
        compiler_params=_params(("parallel", "arbitrary")),
    )(q, k, v)


def _attn_bwd(name, window, scale, q, k, v, dout, lse, delta, cos, sin_inv, dk_acc, dv_acc):
    dil, m, e = k.shape
    nb = window // dil
    nblk = m // nb
    heads = e // HEAD_DIM

    accumulate = dk_acc is not None

    def body(k_ref, v_ref, q0_ref, qn_ref, do0_ref, don_ref, l0_ref, ln_ref, dl0_ref, dln_ref, c_ref, s_ref, *rest):
        if accumulate:
            dki_ref, dvi_ref = rest[:2]
            rest = rest[2:]
        dq_ref, dko_ref, dvo_ref, carry_ref, qc_ref, doc_ref, lc_ref, dlc_ref = rest
        n = pl.program_id(1)

        @pl.when(n == 0)
        def _():
            carry_ref[...] = jnp.zeros_like(carry_ref)
            qc_ref[...] = q0_ref[...]
            doc_ref[...] = do0_ref[...]
            lc_ref[...] = l0_ref[...]
            dlc_ref[...] = dl0_ref[...]

        mask_n, mask_c = _band_masks(nb, n == nblk - 1)
        cos_t, sin_t = c_ref[...], s_ref[...]
        cols = _col_blocks(e)
        stat = lambda ref, sl: ref[:, sl] if nb == HEAD_DIM else ref[:, sl][:, :1]
        s_c = [_dot(qc_ref[:, sl], k_ref[:, sl], NT) for sl in cols]
        s_n = [_dot(qn_ref[:, sl], k_ref[:, sl], NT) for sl in cols]
        dp_c = [_dot(doc_ref[:, sl], v_ref[:, sl], NT) for sl in cols]
        dp_n = [_dot(don_ref[:, sl], v_ref[:, sl], NT) for sl in cols]
        p_c = [jnp.where(mask_c, jnp.exp(s - stat(lc_ref, sl)), 0.0) for s, sl in zip(s_c, cols)]
        p_n = [jnp.where(mask_n, jnp.exp(s - stat(ln_ref, sl)), 0.0) for s, sl in zip(s_n, cols)]
        ds_c = [(p * (dp - stat(dlc_ref, sl))).astype(BF16) for p, dp, sl in zip(p_c, dp_c, cols)]
        ds_n = [(p * (dp - stat(dln_ref, sl))).astype(BF16) for p, dp, sl in zip(p_n, dp_n, cols)]
        for h, sl in enumerate(cols):
            dq = (carry_ref[:, sl] + _dot(ds_c[h], k_ref[:, sl], NN)) * scale
            dq_ref[:, sl] = _rope_apply(dq, cos_t, sin_t).astype(BF16)
        for h, sl in enumerate(cols):
            carry_ref[:, sl] = _dot(ds_n[h], k_ref[:, sl], NN)
        for h, sl in enumerate(cols):
            dk = _dot(ds_c[h], qc_ref[:, sl], TN) + _dot(ds_n[h], qn_ref[:, sl], TN)
            dv = _dot(p_c[h].astype(BF16), doc_ref[:, sl], TN) + _dot(p_n[h].astype(BF16), don_ref[:, sl], TN)
            dko_ref[:, sl] = dki_ref[:, sl] + dk if accumulate else dk
            dvo_ref[:, sl] = dvi_ref[:, sl] + dv if accumulate else dv
        qc_ref[...] = qn_ref[...]
        doc_ref[...] = don_ref[...]
        lc_ref[...] = ln_ref[...]
        dlc_ref[...] = dln_ref[...]

    blk = (None, nb, e)
    cur = lambda r, n: (r, n, 0)
    nxt = lambda r, n: (r, jnp.minimum(n + 1, nblk - 1), 0)
    first = lambda r, n: (r, 0, 0)
    rblk = (None, nb, HEAD_DIM)
    both = lambda shape: [pl.BlockSpec(shape, first), pl.BlockSpec(shape, nxt)]
    accs = (dk_acc, dv_acc) if accumulate else ()
    return pl.pallas_call(
        body, name=name, grid=(dil, nblk),
        in_specs=[pl.BlockSpec(blk, cur), pl.BlockSpec(blk, cur), *both(blk), *both(blk), *both(blk), *both(blk),
                  pl.BlockSpec(rblk, cur), pl.BlockSpec(rblk, cur)] + [pl.BlockSpec(blk, cur)] * len(accs),
        out_specs=[pl.BlockSpec(blk, cur)] * 3,
        out_shape=[jax.ShapeDtypeStruct((dil, m, e), BF16),
                   jax.ShapeDtypeStruct((dil, m, e), F32), jax.ShapeDtypeStruct((dil, m, e), F32)],
        scratch_shapes=[pltpu.VMEM((nb, e), F32), pltpu.VMEM((nb, e), BF16), pltpu.VMEM((nb, e), BF16),
                        pltpu.VMEM((nb, e), F32), pltpu.VMEM((nb, e), F32)],
        input_output_aliases={12: 1, 13: 2} if accumulate else {},
        compiler_params=_params(("parallel", "arbitrary")),
    )(k, v, q, q, dout, dout, lse, lse, delta, delta, cos, sin_inv, *accs)


def _rope_tables(s_len):
    inv_freq = 1.0 / (ROPE_THETA ** (jnp.arange(0, HEAD_DIM, 2, dtype=F32) / HEAD_DIM))
    ang = jnp.arange(s_len, dtype=F32)[:, None] * inv_freq[None, :]
    cos, sin = jnp.cos(ang), jnp.sin(ang)
    return jnp.concatenate([cos, cos], axis=1), jnp.concatenate([-sin, sin], axis=1)


def _row(vec):
    return vec.reshape(1, -1)


def _local_step(x, target, n_a, n_b, fetch, begin, emit, relay=lambda group, carry: carry):
    s_len, d = x.shape
    n_q = len(DILATED_PAIRS)
    cos, sin = _rope_tables(s_len)
    sin_inv = -sin
    q_scale = 1.0 / math.sqrt(HEAD_DIM)
    w = {}

    def need(group, after):
        for name, (layer, arr) in fetch(group, after).items():
            w.setdefault(name, {})[layer] = arr

    saved_a = []
    for i in range(n_a):
        need(f"a{i}", x)
        h = _rmsnorm_fwd(f"a{i}_norm", x, _row(w["norm_a"][i]))
        proj = _mm_act_w(f"a{i}_in", h, w["w_in_a"][i], out_dtype=F32)
        pooled = _pool_fwd(f"a{i}_pool", proj)
        ypre = _mm_grp_fwd(f"a{i}_grp", pooled, w["w_grp_a"][i])
        z = _gate_a_fwd(f"a{i}_gate", ypre, proj, _row(w["scale_a"][i]))
        if i + 1 < n_a:
            (z,) = relay(f"a{i + 1}", (z,))
        x_next = _mm_act_w(f"a{i}_out", z, w["w_out_a"][i], out_dtype=F32, add=x)
        saved_a.append((x, h, proj, pooled, ypre, z))
        x = x_next

    x_kv = x
    need("kv", x)
    e = w["w_k"][0].shape[1]
    kv_in = _rmsnorm_fwd("kv_norm", x, _row(w["norm_kv"][0]))
    windows = [window for window, _ in DILATED_PAIRS]
    dils = tuple(dil for _, dil in DILATED_PAIRS)
    far_dils = tuple(dil for dil in dils if dil > 1)
    ks = _mm_act_w("kv_k", kv_in, w["w_k"][0], rope=(cos, sin, 1.0), dils=dils)
    if n_b:
        (kv_in,) = relay("b0", (kv_in,))
    vs = _mm_act_w("kv_v", kv_in, w["w_v"][0], dils=dils)

    saved_b = []
    for i in range(n_b):
        need(f"b{i}", x if i > 0 else vs[0])
        hs = _rmsnorm_fwd(f"b{i}_norm", x, _row(w["norm_b"][i]), dils=far_dils)
        hs = {1: hs[0], **{dil: h_d.reshape(s_len, d) for dil, h_d in zip(far_dils, hs[1:])}}
        qs = [_mm_act_w(f"b{i}_q{g}", hs[1], w["w_in_b"][i], rope=(cos, sin, q_scale), n_first=g, n_cols=1,
                        dils=(dil,))[0] for g, dil in enumerate(dils)]
        gate = _mm_act_w(f"b{i}_g", hs[1], w["w_in_b"][i], out_dtype=F32, n_first=n_q, n_cols=1)
        outs, lses = [], []
        for g in range(n_q):
            o_g, l_g = _attn_fwd(f"b{i}_attn{g}", windows[g], qs[g], ks[g], vs[g])
            outs.append(o_g)
            lses.append(l_g)
        merged, lse, z = _merge_gate_fwd(f"b{i}_merge", outs, lses, gate, dils)
        if i + 1 < n_b:
            (z,) = relay(f"b{i + 1}", (z,))
        x_next = _mm_act_w(f"b{i}_out", z, w["w_out_b"][i], out_dtype=F32, add=x)
        saved_b.append((x, hs, qs, gate, merged, lse, z))
        x = x_next

    need("head", x)
    loss_vec, dx, dxb, g_norm_f = _loss_head("loss_head", x, _row(w["norm_f"][0]), target)

    small = {"norm_a": {}, "scale_a": {}, "norm_kv": {}, "norm_b": {}, "norm_f": {0: g_norm_f}}
    shard_rows = lambda g2: g2.reshape(N_CHIPS, g2.shape[0] // N_CHIPS, g2.shape[1])

    res_major = lambda t, dil: t.reshape(s_len // dil, dil, t.shape[1]).transpose(1, 0, 2)
    cos_r = [res_major(cos, dil) for dil in dils]
    sin_inv_r = [res_major(sin_inv, dil) for dil in dils]
    dk_accs = [None] * len(dils)
    dv_accs = [None] * len(dils)
    for i in reversed(range(n_b)):
        x_in, hs, qs, gate, merged, lse, z = saved_b[i]
        dz = _mm_grad_act(f"b{i}_dz", dxb, w["w_out_b"][i])
        g_out = shard_rows(_mm_grad_w(f"b{i}_gwo", z, dxb))
        dgate, stats = _gate_b_bwd(f"b{i}_dgate", dz, merged, gate, lse, dils)
        dh = _mm_grad_act(f"b{i}_dh{n_q}", dgate, w["w_in_b"][i], slot=n_q)
        g_in = _mm_grad_w(f"b{i}_gwi{n_q}", hs[1], dgate, col_shards=n_q + 1, slot=n_q)
        for g, dil in enumerate(dils):
            dout, lse_g, delta_g = stats[g]
            dq, dk_accs[g], dv_accs[g] = _attn_bwd(f"b{i}_dattn{g}", windows[g], q_scale, qs[g], ks[g], vs[g], dout,
                                                   lse_g, delta_g, cos_r[g], sin_inv_r[g], dk_accs[g], dv_accs[g])
            dh = _mm_grad_act(f"b{i}_dh{g}", dq if dil > 1 else dq[0], w["w_in_b"][i], add=dh, slot=g)
            g_in = _mm_grad_w(f"b{i}_gwi{g}", hs[dil], dq.reshape(s_len, e), col_shards=n_q + 1, slot=g, into=g_in)
        (dh,) = begin(f"b{i}", {"w_in_b": (i, g_in), "w_out_b": (i, g_out)}, (dh,))
        dx, dxb, small["norm_b"][i] = _rmsnorm_bwd(f"b{i}_dnorm", x_in, _row(w["norm_b"][i]), dh, dx)
        dx, dxb = emit(f"b{i}", (dx, dxb))

    dkb, dvb = _kv_grad_prep("kv_dprep", dk_accs, dv_accs, dils, cos, sin_inv)
    dkv = _mm_grad_act("kv_dk", dkb, w["w_k"][0])
    dkv = _mm_grad_act("kv_dv", dvb, w["w_v"][0], add=dkv)
    g_k = shard_rows(_mm_grad_w("kv_gwk", kv_in, dkb))
    g_v = shard_rows(_mm_grad_w("kv_gwv", kv_in, dvb))
    (dkv,) = begin("kv", {"w_k": (0, g_k), "w_v": (0, g_v)}, (dkv,))
    dx, dxb, small["norm_kv"][0] = _rmsnorm_bwd("kv_dnorm", x_kv, _row(w["norm_kv"][0]), dkv, dx)
    dx, dxb = emit("kv", (dx, dxb))

    for i in reversed(range(n_a)):
        x_in, h, proj, pooled, ypre, z = saved_a[i]
        dz = _mm_grad_act(f"a{i}_dz", dxb, w["w_out_a"][i])
        g_out = shard_rows(_mm_grad_w(f"a{i}_gwo", z, dxb))
        dypre, dproj, small["scale_a"][i] = _gate_a_bwd(f"a{i}_dgate", dz, ypre, proj, _row(w["scale_a"][i]))
        dpooled = _mm_grp_grad_act(f"a{i}_dgrp", dypre, w["w_grp_a"][i])
        g_grp = _mm_grp_grad_w(f"a{i}_gwg", pooled, dypre, len(POOL_WINDOWS))
        g_grp = g_grp.reshape(N_CHIPS, -1, g_grp.shape[-1])
        last = i == 0
        if last:
            (dpooled,) = emit(f"a{i}", begin(f"a{i}", {"w_grp_a": (i, g_grp), "w_out_a": (i, g_out)}, (dpooled,)))
        dproj = _pool_bwd(f"a{i}_dpool", dpooled, dproj)
        g_in = _mm_grad_w(f"a{i}_gwi", h, dproj, col_shards=N_CHIPS)
        if last:
            (dproj,) = emit(f"a{i}i", begin(f"a{i}i", {"w_in_a": (i, g_in)}, (dproj,)))
        dh = _mm_grad_act(f"a{i}_dh", dproj, w["w_in_a"][i])
        if not last:
            (dh,) = begin(f"a{i}", {"w_in_a": (i, g_in), "w_grp_a": (i, g_grp), "w_out_a": (i, g_out)}, (dh,))
        dx, dxb, small["norm_a"][i] = _rmsnorm_bwd(f"a{i}_dnorm", x_in, _row(w["norm_a"][i]), dh, dx)
        if not last:
            dx, dxb = emit(f"a{i}", (dx, dxb))

    return loss_vec, dx, small


BIG_WEIGHTS = ("w_in_a", "w_grp_a", "w_out_a", "w_k", "w_v", "w_in_b", "w_out_b")


def _pair_add(name, grad, recv, c_idx):
    _, r, cols = grad.shape
    half = r // 2
    rb = _tile(half, 256)
    nrb = half // rb

    def body(c_ref, g_ref, r_ref, o_ref):
        o_ref[...] = (g_ref[...].astype(F32) + r_ref[...].astype(F32)).astype(BF16)

    blk = (None, rb, cols)
    grid_spec = pltpu.PrefetchScalarGridSpec(
        num_scalar_prefetch=1, grid=(N_CHIPS, nrb),
        in_specs=[pl.BlockSpec(blk, lambda s, i, c: (s, c[0] * nrb + i, 0)), pl.BlockSpec(blk, lambda s, i, c: (s, i, 0))],
        out_specs=pl.BlockSpec(blk, lambda s, i, c: (s, i, 0)))
    return pl.pallas_call(body, name=name, grid_spec=grid_spec,
                          out_shape=jax.ShapeDtypeStruct((N_CHIPS, half, cols), BF16),
                          compiler_params=_params(("parallel", "parallel")))(c_idx, grad, recv)


def _final_add(name, part, recv, sc_idx, layer, n_layers, into=None):
    _, half, cols = part.shape
    rb = _tile(half, 256)
    nrb = half // rb
    n_peer = recv.shape[0]

    def body(sc_ref, p_ref, *refs):
        acc = p_ref[...].astype(F32)
        for r_ref in refs[:n_peer]:
            acc = acc + r_ref[...].astype(F32)
        refs[-1][...] = acc

    blk = (None, rb, cols)
    peer_spec = lambda k: pl.BlockSpec(blk, lambda i, sc: (k, i, 0))
    grid_spec = pltpu.PrefetchScalarGridSpec(
        num_scalar_prefetch=1, grid=(nrb,),
        in_specs=[pl.BlockSpec(blk, lambda i, sc: (sc[0], i, 0))] + [peer_spec(k) for k in range(n_peer)]
                 + ([] if into is None else [ANY]),
        out_specs=pl.BlockSpec(blk, lambda i, sc: (layer, sc[1] * nrb + i, 0)))
    extra = () if into is None else (into,)
    return pl.pallas_call(body, name=name, grid_spec=grid_spec,
                          out_shape=jax.ShapeDtypeStruct((n_layers, 2 * half, cols), F32),
                          input_output_aliases={} if into is None else {2 + n_peer: 0},
                          compiler_params=_params(("parallel",)))(sc_idx, part, *([recv] * n_peer), *extra)


def _cast_into_slot(name, arr, layer, s_idx, after=None):
    _, b, r, cols = arr.shape
    rb = _tile(r, 512)

    def body(s_ref, a_ref, *rest):
        rest[-1][...] = a_ref[...].astype(BF16)

    blk = (None, None, rb, cols)
    grid_spec = pltpu.PrefetchScalarGridSpec(
        num_scalar_prefetch=1, grid=(b, r // rb),
        in_specs=[pl.BlockSpec(blk, lambda j, i, s: (layer, j, i, 0))] + ([] if after is None else [ANY]),
        out_specs=pl.BlockSpec(blk, lambda j, i, s: (j, s[0], i, 0)))
    return pl.pallas_call(body, name=name, grid_spec=grid_spec,
                          out_shape=jax.ShapeDtypeStruct((b, N_CHIPS, r, cols), BF16),
                          compiler_params=_params(("parallel", "parallel")))(
                              s_idx, arr, *(() if after is None else (after,)))


def _sum_devices(name, gathered, own, me_idx):
    n_dev, p, d = gathered.shape

    def body(me_ref, g_ref, own_ref, o_ref):
        acc = None
        for j in range(n_dev):
            term = jnp.where(me_ref[0] == j, own_ref[...], g_ref[j])
            acc = term if acc is None else acc + term
        o_ref[...] = acc

    grid_spec = pltpu.PrefetchScalarGridSpec(
        num_scalar_prefetch=1, grid=(1,),
        in_specs=[pl.BlockSpec((n_dev, p, d), lambda i, me: (0, 0, 0)), pl.BlockSpec((p, d), lambda i, me: (0, 0))],
        out_specs=pl.BlockSpec((p, d), lambda i, me: (0, 0)))
    return pl.pallas_call(body, name=name, grid_spec=grid_spec, out_shape=jax.ShapeDtypeStruct((p, d), F32),
                          compiler_params=_params(("arbitrary",)))(me_idx, gathered, own)


def _adamw_block(w_ref, g_ref, m_ref, v_ref, d_ref, mo_ref, vo_ref):
    grad = g_ref[...]
    m_new = ADAM_B1 * m_ref[...] + (1.0 - ADAM_B1) * grad
    v_new = ADAM_B2 * v_ref[...] + (1.0 - ADAM_B2) * (grad * grad)
    m_hat = m_new / (1.0 - ADAM_B1 ** ADAM_STEP)
    v_hat = v_new / (1.0 - ADAM_B2 ** ADAM_STEP)
    d_ref[...] = -ADAM_LR * (m_hat / (jnp.sqrt(v_hat) + ADAM_EPS) + ADAM_WD * w_ref[...])
    mo_ref[...] = m_new
    vo_ref[...] = v_new


def _adamw(name, w, g, m, v):
    shape = w.shape
    cols = shape[-1]
    flat = lambda a: a.reshape(-1, cols)
    rows = flat(w).shape[0]
    bs = _tile(rows, 256)

    def body(*refs):
        _adamw_block(*refs)

    spec = _row_spec(bs, cols)
    outs = _rows_call(body, name, rows, [spec] * 4, [spec] * 3, [jax.ShapeDtypeStruct((rows, cols), F32)] * 3, bs)(
        flat(w), flat(g), flat(m), flat(v))
    return tuple(o.reshape(shape) for o in outs)


def _adamw_half(name, w, g, m, v, half_idx, into=None):
    n_l, r, cols = w.shape
    rb = _tile(r // 2, 256)
    nrb = r // 2 // rb

    def body(h_ref, w_ref, g_ref, m_ref, v_ref, *rest):
        _adamw_block(w_ref, g_ref, m_ref, v_ref, *rest[-3:])

    spec = pl.BlockSpec((None, rb, cols), lambda j, i, h: (j, h[0] * nrb + i, 0))
    extra = () if into is None else tuple(into)
    grid_spec = pltpu.PrefetchScalarGridSpec(
        num_scalar_prefetch=1, grid=(n_l, nrb), in_specs=[spec] * 4 + [ANY] * len(extra), out_specs=[spec] * 3)
    return tuple(pl.pallas_call(
        body, name=name, grid_spec=grid_spec, out_shape=[jax.ShapeDtypeStruct(w.shape, F32)] * 3,
        input_output_aliases={5 + k: k for k in range(len(extra))},
        compiler_params=_params(("parallel", "parallel")))(half_idx, w, g, m, v, *extra))


def _place():
    x, y, c = lax.axis_index("x"), lax.axis_index("y"), lax.axis_index("c")
    chips = [(1 - x, y), (x, 1 - y), (1 - x, 1 - y)]
    return x, y, c, chips


def _chip_index(chip):
    return 2 * chip[0] + chip[1]


def _comm_call(body, name, n_in, out_shape, scratch, aliases=None):
    return pl.pallas_call(body, name=name, in_specs=[ANY] * n_in, out_specs=[ANY] * len(out_shape), out_shape=out_shape,
                          scratch_shapes=scratch, input_output_aliases=aliases or {})


HBM_SPEC = pl.BlockSpec(memory_space=pltpu.HBM)
SEM_SPEC = pl.BlockSpec(memory_space=pltpu.SEMAPHORE)
SPLIT_PARAMS = pltpu.CompilerParams(has_side_effects=pltpu.SideEffectType.DATAFLOW_SIDE_EFFECTING)


def _in_hbm(arr):
    return pltpu.with_memory_space_constraint(arr, pltpu.HBM)


def _slot_half(ref, chip, core):
    half = ref.shape[2] // 2
    return ref.at[:, _chip_index(chip), pl.ds(core * half, half), :]


def _gather_start(name, bufs, carry=()):
    n, n_c = len(bufs), len(carry)

    def body(*refs):
        ins, (send_sems, recv_sems) = refs[:n], refs[n + n_c:n + n_c + 2]
        x, y, c, chips = _place()
        for a in range(n):
            block = _slot_half(ins[a], (x, y), c)
            for k, chip in enumerate(chips):
                pltpu.make_async_remote_copy(src_ref=block, dst_ref=block, send_sem=send_sems.at[3 * a + k],
                                             recv_sem=recv_sems.at[3 * a + k], device_id=(*chip, c),
                                             device_id_type=MESH).start()

    dma = pltpu.SemaphoreType.DMA
    thru = list(bufs) + list(carry)
    res = pl.pallas_call(
        body, name=name, in_specs=[HBM_SPEC] * (n + n_c), out_specs=[SEM_SPEC] * 2 + [HBM_SPEC] * (n + n_c),
        out_shape=[dma((3 * n,)), dma((3 * n,))] + [pltpu.HBM(a.shape, a.dtype) for a in thru],
        input_output_aliases={t: 2 + t for t in range(n + n_c)}, compiler_params=SPLIT_PARAMS,
    )(*[_in_hbm(a) for a in thru])
    return (res[0], res[1]), list(res[2:2 + n]), list(res[2 + n:])


def _gather_wait(name, sems, bufs, after):
    n = len(bufs)

    def body(*refs):
        ins, (send_sems, recv_sems) = refs[:n], refs[n:n + 2]
        x, y, c, chips = _place()
        for a in range(n):
            for k, chip in enumerate(chips):
                mine, theirs = _slot_half(ins[a], (x, y), c), _slot_half(ins[a], chip, c)
                copy = pltpu.make_async_remote_copy(src_ref=mine, dst_ref=theirs, send_sem=send_sems.at[3 * a + k],
                                                    recv_sem=recv_sems.at[3 * a + k], device_id=(*chip, c),
                                                    device_id_type=MESH)
                copy.wait_send()
                copy.wait_recv()

    res = pl.pallas_call(
        body, name=name, in_specs=[HBM_SPEC] * n + [SEM_SPEC, SEM_SPEC, ANY], out_specs=[HBM_SPEC] * n,
        out_shape=[pltpu.HBM(b.shape, b.dtype) for b in bufs], input_output_aliases={a: a for a in range(n)},
        compiler_params=SPLIT_PARAMS)(*bufs, *sems, after)
    return list(res)


def _gather_relay(name, sems, bufs, carry):
    n, n_c = len(bufs), len(carry)

    def body(*refs):
        ins = refs[:n]
        send_in, recv_in = refs[n + n_c:n + n_c + 2]
        send_out, recv_out = refs[n + n_c + 2:n + n_c + 4]
        x, y, c, chips = _place()
        for a in range(n):
            for k, chip in enumerate(chips):
                mine, theirs = _slot_half(ins[a], (x, y), c), _slot_half(ins[a], chip, c)
                copy = pltpu.make_async_remote_copy(src_ref=mine, dst_ref=theirs, send_sem=send_in.at[3 * a + k],
                                                    recv_sem=recv_in.at[3 * a + k], device_id=(*chip, c),
                                                    device_id_type=MESH)
                copy.wait_send()
                copy.wait_recv()
        for a in range(n):
            for k, chip in enumerate(chips):
                block = _slot_half(ins[a], chip, c)
                pltpu.make_async_remote_copy(src_ref=block, dst_ref=block, send_sem=send_out.at[3 * a + k],
                                             recv_sem=recv_out.at[3 * a + k], device_id=(x, y, 1 - c),
                                             device_id_type=MESH).start()

    dma = pltpu.SemaphoreType.DMA
    thru = list(bufs) + list(carry)
    res = pl.pallas_call(
        body, name=name, in_specs=[HBM_SPEC] * (n + n_c) + [SEM_SPEC, SEM_SPEC],
        out_specs=[SEM_SPEC] * 2 + [HBM_SPEC] * (n + n_c),
        out_shape=[dma((3 * n,)), dma((3 * n,))] + [pltpu.HBM(a.shape, a.dtype) for a in thru],
        input_output_aliases={t: 2 + t for t in range(n + n_c)}, compiler_params=SPLIT_PARAMS,
    )(*[_in_hbm(a) for a in thru], *sems)
    return (res[0], res[1]), list(res[2:2 + n]), list(res[2 + n:])


def _gather_relay_wait(name, sems, bufs, after):
    n = len(bufs)

    def body(*refs):
        ins, (send_sems, recv_sems) = refs[:n], refs[n:n + 2]
        x, y, c, chips = _place()
        for a in range(n):
            for k, chip in enumerate(chips):
                copy = pltpu.make_async_remote_copy(
                    src_ref=_slot_half(ins[a], chip, c), dst_ref=_slot_half(ins[a], chip, 1 - c),
                    send_sem=send_sems.at[3 * a + k], recv_sem=recv_sems.at[3 * a + k], device_id=(x, y, 1 - c),
                    device_id_type=MESH)
                copy.wait_send()
                copy.wait_recv()

    res = pl.pallas_call(
        body, name=name, in_specs=[HBM_SPEC] * n + [SEM_SPEC, SEM_SPEC, ANY], out_specs=[HBM_SPEC] * n,
        out_shape=[pltpu.HBM(b.shape, b.dtype) for b in bufs], input_output_aliases={a: a for a in range(n)},
        compiler_params=SPLIT_PARAMS)(*bufs, *sems, after)
    return list(res)


def _gather_forward(name, bufs, smalls=()):
    n, n_small = len(bufs), len(smalls)

    def body(*refs):
        small_in = refs[n:n + n_small]
        outs = refs[n + n_small:2 * n + n_small]
        small_out = refs[2 * n + n_small:2 * n + 2 * n_small]
        send_sems, recv_sems, s_send, s_recv, s_local = refs[-5:]
        x, y, c, chips = _place()
        me, sibling = _chip_index((x, y)), (x, y, 1 - c)

        def forward(t, k, core):
            block = _slot_half(outs[t], chips[k], core)
            return pltpu.make_async_remote_copy(src_ref=block, dst_ref=block, send_sem=send_sems.at[t, k],
                                                recv_sem=recv_sems.at[t, k], device_id=sibling, device_id_type=MESH)

        def small_copy(j, k, slot):
            return pltpu.make_async_remote_copy(src_ref=small_in[j], dst_ref=small_out[j].at[slot],
                                                send_sem=s_send.at[j, k], recv_sem=s_recv.at[j, k],
                                                device_id=(*chips[k], c), device_id_type=MESH)

        local = []
        for t in range(n):
            for k in range(3):
                forward(t, k, c).start()
        for j in range(n_small):
            own = pltpu.make_async_copy(small_in[j], small_out[j].at[me], s_local.at[j])
            own.start()
            local.append(own)
            for k in range(3):
                small_copy(j, k, me).start()
        for t in range(n):
            for k in range(3):
                forward(t, k, 1 - c).wait_recv()
        for j in range(n_small):
            for k in range(3):
                small_copy(j, k, _chip_index(chips[k])).wait_recv()
        for t in range(n):
            for k in range(3):
                forward(t, k, c).wait_send()
        for j in range(n_small):
            for k in range(3):
                small_copy(j, k, me).wait_send()
        for own in local:
            own.wait()

    out_shape = [jax.ShapeDtypeStruct(b.shape, BF16) for b in bufs]
    out_shape += [jax.ShapeDtypeStruct((N_CHIPS,) + s.shape, F32) for s in smalls]
    dma = pltpu.SemaphoreType.DMA
    n_s = max(n_small, 1)
    res = _comm_call(body, name, n + n_small, out_shape,
                     [dma((n, 3)), dma((n, 3)), dma((n_s, 3)), dma((n_s, 3)), dma((n_s,))],
                     aliases={t: t for t in range(n)})(*bufs, *smalls)
    return list(res[:n]), list(res[n:])


def _halves_copy(grad_ref, land_ref, send_sems, recv_sems, t):
    x, y, c, _ = _place()
    half = grad_ref.shape[1] // 2
    return pltpu.make_async_remote_copy(
        src_ref=grad_ref.at[:, pl.ds((1 - c) * half, half), :], dst_ref=land_ref, send_sem=send_sems.at[t],
        recv_sem=recv_sems.at[t], device_id=(x, y, 1 - c), device_id_type=MESH)


def _exchange_start(name, grads, carry=()):
    n = len(grads)
    lands = [lax.empty((g.shape[0], g.shape[1] // 2, g.shape[2]), BF16) for g in grads]

    def body(*refs):
        send_sems, recv_sems = refs[2 * n + len(carry):2 * n + len(carry) + 2]
        for t in range(n):
            _halves_copy(refs[t], refs[n + t], send_sems, recv_sems, t).start()

    dma = pltpu.SemaphoreType.DMA
    thru = list(grads) + lands + list(carry)
    res = pl.pallas_call(
        body, name=name, in_specs=[HBM_SPEC] * len(thru), out_specs=[SEM_SPEC] * 2 + [HBM_SPEC] * len(thru),
        out_shape=[dma((n,)), dma((n,))] + [pltpu.HBM(a.shape, a.dtype) for a in thru],
        input_output_aliases={t: 2 + t for t in range(len(thru))}, compiler_params=SPLIT_PARAMS,
    )(*[_in_hbm(a) for a in thru])
    return (res[0], res[1]), list(res[2:2 + n]), list(res[2 + n:2 + 2 * n]), list(res[2 + 2 * n:])


def _exchange_wait(name, sems, grads, lands, after):
    n = len(grads)

    def body(*refs):
        send_sems, recv_sems = refs[2 * n:2 * n + 2]
        for t in range(n):
            copy = _halves_copy(refs[t], refs[n + t], send_sems, recv_sems, t)
            copy.wait_send()
            copy.wait_recv()

    res = pl.pallas_call(
        body, name=name, in_specs=[HBM_SPEC] * (2 * n) + [SEM_SPEC, SEM_SPEC, ANY], out_specs=[HBM_SPEC] * (2 * n),
        out_shape=[pltpu.HBM(a.shape, a.dtype) for a in grads + lands],
        input_output_aliases={t: t for t in range(2 * n)}, compiler_params=SPLIT_PARAMS)(*grads, *lands, *sems, after)
    return list(res[:n]), list(res[n:])


def _scatter_copy(part_ref, land_ref, send_sems, recv_sems, t, k, chip, c):
    return pltpu.make_async_remote_copy(
        src_ref=part_ref.at[_chip_index(chip)], dst_ref=land_ref.at[k], send_sem=send_sems.at[3 * t + k],
        recv_sem=recv_sems.at[3 * t + k], device_id=(*chip, c), device_id_type=MESH)


def _scatter_start(name, parts, carry=()):
    n, n_c = len(parts), len(carry)
    lands = [lax.empty((3,) + p.shape[1:], BF16) for p in parts]

    def body(*refs):
        p_in, l_in = refs[:n], refs[n:2 * n]
        send_sems, recv_sems = refs[2 * n + n_c:2 * n + n_c + 2]
        x, y, c, chips = _place()
        for t in range(n):
            for k, chip in enumerate(chips):
                _scatter_copy(p_in[t], l_in[t], send_sems, recv_sems, t, k, chip, c).start()

    dma = pltpu.SemaphoreType.DMA
    thru = list(parts) + lands + list(carry)
    res = pl.pallas_call(
        body, name=name, in_specs=[HBM_SPEC] * len(thru), out_specs=[SEM_SPEC] * 2 + [HBM_SPEC] * len(thru),
        out_shape=[dma((3 * n,)), dma((3 * n,))] + [pltpu.HBM(a.shape, a.dtype) for a in thru],
        input_output_aliases={t: 2 + t for t in range(len(thru))}, compiler_params=SPLIT_PARAMS,
    )(*[_in_hbm(a) for a in thru])
    return (res[0], res[1]), list(res[2:2 + n]), list(res[2 + n:2 + 2 * n]), list(res[2 + 2 * n:])


def _scatter_wait(name, sems, parts, lands, after):
    n = len(parts)

    def body(*refs):
        p_in, l_in = refs[:n], refs[n:2 * n]
        send_sems, recv_sems = refs[2 * n:2 * n + 2]
        x, y, c, chips = _place()
        for t in range(n):
            for k, chip in enumerate(chips):
                copy = _scatter_copy(p_in[t], l_in[t], send_sems, recv_sems, t, k, chip, c)
                copy.wait_send()
                copy.wait_recv()

    hbm_out = lambda a: pltpu.HBM(a.shape, a.dtype)
    res = pl.pallas_call(
        body, name=name, in_specs=[HBM_SPEC] * (2 * n) + [SEM_SPEC, SEM_SPEC, ANY], out_specs=[HBM_SPEC] * (2 * n),
        out_shape=[hbm_out(a) for a in parts + lands], input_output_aliases={t: t for t in range(2 * n)},
        compiler_params=SPLIT_PARAMS)(*parts, *lands, *sems, after)
    return list(res[:n]), list(res[n:])


def _share_items(fulls):
    return [(a, l) for a in range(len(fulls)) for l in range(fulls[a].shape[0])]


def _share_copy(ref, layer, send_sems, recv_sems, t):
    x, y, c, _ = _place()
    half = ref.shape[1] // 2
    return pltpu.make_async_remote_copy(
        src_ref=ref.at[layer, pl.ds(c * half, half), :], dst_ref=ref.at[layer, pl.ds((1 - c) * half, half), :],
        send_sem=send_sems.at[t], recv_sem=recv_sems.at[t], device_id=(x, y, 1 - c), device_id_type=MESH)


def _share_start(fulls):
    n, items = len(fulls), _share_items(fulls)

    def body(*refs):
        send_sems, recv_sems = refs[n:n + 2]
        x, y, c, _ = _place()
        for t, (a, l) in enumerate(items):
            half = refs[a].shape[1] // 2
            mine = refs[a].at[l, pl.ds(c * half, half), :]
            pltpu.make_async_remote_copy(src_ref=mine, dst_ref=mine, send_sem=send_sems.at[t], recv_sem=recv_sems.at[t],
                                         device_id=(x, y, 1 - c), device_id_type=MESH).start()

    dma = pltpu.SemaphoreType.DMA
    res = pl.pallas_call(
        body, name="grad_share_start", in_specs=[HBM_SPEC] * n, out_specs=[SEM_SPEC] * 2 + [HBM_SPEC] * n,
        out_shape=[dma((len(items),)), dma((len(items),))] + [pltpu.HBM(f.shape, f.dtype) for f in fulls],
        input_output_aliases={t: 2 + t for t in range(n)}, compiler_params=SPLIT_PARAMS,
    )(*[_in_hbm(f) for f in fulls])
    return (res[0], res[1]), list(res[2:])


def _share_wait(sems, fulls, after):
    n, items = len(fulls), _share_items(fulls)

    def body(*refs):
        send_sems, recv_sems = refs[n:n + 2]
        for t, (a, l) in enumerate(items):
            copy = _share_copy(refs[a], l, send_sems, recv_sems, t)
            copy.wait_send()
            copy.wait_recv()

    res = pl.pallas_call(
        body, name="grad_share_wait", in_specs=[HBM_SPEC] * n + [SEM_SPEC, SEM_SPEC, ANY], out_specs=[HBM_SPEC] * n,
        out_shape=[pltpu.HBM(f.shape, f.dtype) for f in fulls], input_output_aliases={t: t for t in range(n)},
        compiler_params=SPLIT_PARAMS)(*fulls, *sems, after)
    return list(res)


N_DEVICES = 8


def _device_index(x, y, c):
    return 4 * x + 2 * y + c


def _small_peers():
    x, y, c, _ = _place()
    flips = [(fx, fy, fc) for fx in (0, 1) for fy in (0, 1) for fc in (0, 1)][1:]
    return _device_index(x, y, c), [(x ^ fx, y ^ fy, c ^ fc) for fx, fy, fc in flips]


def _small_copy(p_ref, land_ref, send_sems, recv_sems, k, peer, slot):
    return pltpu.make_async_remote_copy(src_ref=p_ref, dst_ref=land_ref.at[slot], send_sem=send_sems.at[k],
                                        recv_sem=recv_sems.at[k], device_id=peer, device_id_type=MESH)


def _allgather_small_start(packed, carry=()):
    land = jnp.zeros((N_DEVICES,) + packed.shape, F32)

    def body(*refs):
        p_ref, l_ref = refs[:2]
        send_sems, recv_sems = refs[2 + len(carry):4 + len(carry)]
        me, peers = _small_peers()
        for k, peer in enumerate(peers):
            _small_copy(p_ref, l_ref, send_sems, recv_sems, k, peer, me).start()

    dma = pltpu.SemaphoreType.DMA
    thru = [packed, land] + list(carry)
    res = pl.pallas_call(
        body, name="small_allgather_start", in_specs=[HBM_SPEC] * len(thru),
        out_specs=[SEM_SPEC] * 2 + [HBM_SPEC] * len(thru),
        out_shape=[dma((N_DEVICES - 1,)), dma((N_DEVICES - 1,))] + [pltpu.HBM(a.shape, a.dtype) for a in thru],
        input_output_aliases={t: 2 + t for t in range(len(thru))}, compiler_params=SPLIT_PARAMS,
    )(*[_in_hbm(a) for a in thru])
    return (res[0], res[1]), res[2], res[3], list(res[4:])


def _allgather_small_wait(sems, packed, land, after):
    def body(p_ref, l_ref, send_sems, recv_sems, *_):
        _, peers = _small_peers()
        for k, peer in enumerate(peers):
            copy = _small_copy(p_ref, l_ref, send_sems, recv_sems, k, peer, _device_index(*peer))
            copy.wait_send()
            copy.wait_recv()

    res = pl.pallas_call(
        body, name="small_allgather_wait", in_specs=[HBM_SPEC] * 2 + [SEM_SPEC, SEM_SPEC, ANY],
        out_specs=[HBM_SPEC] * 2, out_shape=[pltpu.HBM(a.shape, a.dtype) for a in (packed, land)],
        input_output_aliases={0: 0, 1: 1}, compiler_params=SPLIT_PARAMS)(packed, land, *sems, after)
    return res[0], res[1]


PAD_ROWS = 8


def kernel(x, norm_a, w_in_a, w_grp_a, scale_a, w_out_a, norm_kv, w_k, w_v, norm_b, w_in_b, w_out_b, norm_f, loss_target, m_norm_a, m_w_in_a, m_w_grp_a, m_scale_a, m_w_out_a, m_norm_kv, m_w_k, m_w_v, m_norm_b, m_w_in_b, m_w_out_b, m_norm_f, v_norm_a, v_w_in_a, v_w_grp_a, v_scale_a, v_w_out_a, v_norm_kv, v_w_k, v_w_v, v_norm_b, v_w_in_b, v_w_out_b, v_norm_f):
    weights = dict(norm_a=norm_a, w_in_a=w_in_a, w_grp_a=w_grp_a, scale_a=scale_a, w_out_a=w_out_a, norm_kv=norm_kv,
                   w_k=w_k, w_v=w_v, norm_b=norm_b, w_in_b=w_in_b, w_out_b=w_out_b, norm_f=norm_f)
    moments_m = dict(norm_a=m_norm_a, w_in_a=m_w_in_a, w_grp_a=m_w_grp_a, scale_a=m_scale_a, w_out_a=m_w_out_a,
                     norm_kv=m_norm_kv, w_k=m_w_k, w_v=m_w_v, norm_b=m_norm_b, w_in_b=m_w_in_b, w_out_b=m_w_out_b,
                     norm_f=m_norm_f)
    moments_v = dict(norm_a=v_norm_a, w_in_a=v_w_in_a, w_grp_a=v_w_grp_a, scale_a=v_scale_a, w_out_a=v_w_out_a,
                     norm_kv=v_norm_kv, w_k=v_w_k, w_v=v_w_v, norm_b=v_norm_b, w_in_b=v_w_in_b, w_out_b=v_w_out_b,
                     norm_f=v_norm_f)
    names = list(weights)
    d = x.shape[-1]
    c_idx = lax.axis_index("c").astype(jnp.int32).reshape(1)
    s_me = 2 * lax.axis_index("x") + lax.axis_index("y")
    s_idx = s_me.astype(jnp.int32).reshape(1)

    def as_lbrc(name):
        a = weights[name]
        if name == "w_grp_a":
            return a
        if a.ndim == 2:
            return a.reshape(1, 1, *a.shape)
        return a.reshape(a.shape[0], 1, *a.shape[1:])

    n_a, n_b = norm_a.shape[0], norm_b.shape[0]
    group_weights = {**{f"a{i}": [("w_in_a", i), ("w_grp_a", i), ("w_out_a", i)] for i in range(n_a)},
                     "kv": [("w_k", 0), ("w_v", 0)],
                     **{f"b{i}": [("w_in_b", i), ("w_out_b", i)] for i in range(n_b)}}
    group_order = [f"a{i}" for i in range(n_a)] + ["kv"] + [f"b{i}" for i in range(n_b)]
    slots, slot_groups = [], []
    small_full, started = {}, {}

    def start_group(gi, carry=()):
        sems, bufs, carry = _gather_start(f"gather_start_{group_order[gi]}", [slots[t] for t in slot_groups[gi]], carry)
        started[gi] = (sems, bufs)
        return carry

    relayed = {}

    def relay(group, carry):
        sems, bufs = started[group_order.index(group)]
        fsems, bufs, carry = _gather_relay(f"gather_relay_{group}", sems, bufs, tuple(carry))
        relayed[group] = (fsems, bufs)
        return carry

    previous = None
    for gi, group in enumerate(group_order):
        slot_groups.append(list(range(len(slots), len(slots) + len(group_weights[group]))))
        for name, l in group_weights[group]:
            slots.append(_cast_into_slot(f"cast_{name}{l}", as_lbrc(name), l, s_idx, previous))
            previous = slots[-1] if gi > 0 else None
        if gi == 0:
            (previous,) = start_group(0, (norm_kv.reshape(1, -1),))

    def gathered_form(name, g):
        if name in ("w_in_a", "w_in_b"):
            return g[0]
        if name == "w_grp_a":
            return g.reshape(g.shape[0], -1, g.shape[-1])
        return g.reshape(-1, g.shape[-1])

    def fetch(group, after):
        if group == "head":
            return {"norm_f": (0, norm_f)}
        gi = group_order.index(group)
        sems, bufs = started[gi]
        if gi == 0:
            after = slots[-1]
        if group in relayed:
            bufs, small_g = _gather_relay_wait(f"gather_relay_wait_{group}", *relayed[group], after), []
        else:
            bufs = _gather_wait(f"gather_wait_{group}", sems, bufs, after)
            bufs, small_g = _gather_forward(f"gather_forward_{group}", bufs, [norm_a, scale_a] if gi == 0 else [])
        out = {name: (l, gathered_form(name, g)) for (name, l), g in zip(group_weights[group], bufs)}
        if gi == 0:
            for name, g in zip(("norm_a", "scale_a"), small_g):
                small_full[name] = g.transpose(1, 0, 2).reshape(g.shape[1], -1)
        layer = group_weights[group][0][1]
        if group.startswith("a"):
            gain_name, gain = "norm_a", small_full["norm_a"][layer]
            out.update(scale_a=(layer, small_full["scale_a"][layer]))
        elif group == "kv":
            gain_name, gain = "norm_kv", norm_kv
        else:
            gain_name, gain = "norm_b", norm_b[layer]
        gain = gain.reshape(1, -1)
        ahead = [gi + 1] + ([gi + 2] if gi + 2 < len(group_order) and group_order[gi + 1] == "kv" else [])
        for gj in ahead:
            if gj < len(group_order) and gj not in started:
                (gain,) = start_group(gj, (gain,))
        out[gain_name] = (layer, gain)
        return out

    exchanging, in_flight = {}, []

    def begin(group, grads_of, carry):
        keys = [(k, grads_of[k][0]) for k in grads_of]
        sems, grads, lands, carry = _exchange_start(f"grad_exchange_start_{group}", [grads_of[k][1] for k in grads_of],
                                                    tuple(carry))
        exchanging[group] = (keys, sems, grads, lands)
        return carry

    def emit(group, carry):
        keys, sems, grads, lands = exchanging.pop(group)
        grads, recv1 = _exchange_wait(f"grad_exchange_wait_{group}", sems, grads, lands, after=carry[0])
        parts = [_pair_add(f"pair_add_{k}{l}", g, r, c_idx) for (k, l), g, r in zip(keys, grads, recv1)]
        sems, parts, lands, carry = _scatter_start(f"grad_scatter_start_{group}", parts, tuple(carry))
        in_flight.append((group, keys, sems, parts, lands))
        return carry

    loss_vec, grad_x, small = _local_step(x[0], loss_target[0], n_a, n_b, fetch, begin, emit, relay)

    small_order = [("norm_a", i) for i in range(n_a)] + [("scale_a", i) for i in range(n_a)] + [("norm_kv", 0)] + \
                  [("norm_b", i) for i in range(norm_b.shape[0])] + [("norm_f", 0)]
    pad = lambda vec: jnp.pad(vec, ((0, PAD_ROWS - 1), (0, 0)))
    packed = jnp.concatenate([pad(loss_vec)] + [pad(small[n][i]) for n, i in small_order], axis=0)
    small_sems, packed, small_land, _ = _allgather_small_start(packed)

    sc_idx = jnp.concatenate([s_idx, c_idx])
    fulls = {name: None for name in BIG_WEIGHTS}
    for group, keys, sems, parts, lands in in_flight:
        parts, lands = _scatter_wait(f"grad_scatter_wait_{group}", sems, parts, lands, after=packed)
        for (name, i), p, r in zip(keys, parts, lands):
            n_layers = 1 if weights[name].ndim == 2 else weights[name].shape[0]
            fulls[name] = _final_add(f"final_add_{name}{i}", p, r, sc_idx, i, n_layers, into=fulls[name])
    share_sems, sharing = _share_start([fulls[name] for name in BIG_WEIGHTS])

    deltas, new_m, new_v, grads = {}, {}, {}, {}
    as_full = lambda n, full: (a.reshape(full.shape) for a in (weights[n], moments_m[n], moments_v[n]))
    mine = {}
    for n, full in zip(BIG_WEIGHTS, sharing):
        w3, m3, v3 = as_full(n, full)
        mine[n] = _adamw_half(f"adamw_mine_{n}", w3, full, m3, v3, c_idx)
    shared = _share_wait(share_sems, sharing, after=mine[BIG_WEIGHTS[-1]][2])
    for n, full in zip(BIG_WEIGHTS, shared):
        w3, m3, v3 = as_full(n, full)
        shape = weights[n].shape
        dl, mn, vn = _adamw_half(f"adamw_theirs_{n}", w3, full, m3, v3, 1 - c_idx, into=mine[n])
        deltas[n], new_m[n], new_v[n], grads[n] = dl.reshape(shape), mn.reshape(shape), vn.reshape(shape), full.reshape(shape)

    def update(n):
        shape = weights[n].shape
        as2d = (lambda a: a.reshape(1, -1)) if len(shape) == 1 else (lambda a: a)
        dl, mn, vn = _adamw(f"adamw_{n}", as2d(weights[n]), as2d(grads[n]), as2d(moments_m[n]), as2d(moments_v[n]))
        deltas[n], new_m[n], new_v[n] = dl.reshape(shape), mn.reshape(shape), vn.reshape(shape)

    packed, small_land = _allgather_small_wait(small_sems, packed, small_land, after=new_v[BIG_WEIGHTS[-1]])
    me_idx = _device_index(lax.axis_index("x"), lax.axis_index("y"), lax.axis_index("c")).astype(jnp.int32).reshape(1)
    totals = _sum_devices("small_sum", small_land, packed, me_idx)
    loss = 0.5 * jnp.sum(totals[0]) / d
    small_tot = {}
    for j, (n, i) in enumerate(small_order):
        small_tot.setdefault(n, []).append(totals[PAD_ROWS * (j + 1)])
    shard_w = norm_a.shape[1]
    for n in ("norm_a", "scale_a"):
        full = jnp.stack(small_tot[n])
        grads[n] = lax.dynamic_slice_in_dim(full, s_me * shard_w, shard_w, axis=1)
    grads["norm_kv"] = small_tot["norm_kv"][0]
    grads["norm_b"] = jnp.stack(small_tot["norm_b"])
    grads["norm_f"] = small_tot["norm_f"][0]
    for n in names:
        if n not in BIG_WEIGHTS:
            update(n)

    return (loss, grad_x[None], *[grads[n] for n in names], *[deltas[n] for n in names],
            *[new_m[n] for n in names], *[new_v[n] for n in names])
```
